```python
import math
import jax, jax.numpy as jnp
from jax import lax
import numpy as np

D_MODEL = 1024
BATCH = 8
SEQ = 8192
DEPTH = 4

N_MIXERS = 2
GDN_HEADS = 8
GDN_HEAD_DIM = 128
GDN_DIM = GDN_HEADS * GDN_HEAD_DIM
GDN_CONV = 4
GDN_CHUNK = 64
S5_DIM = D_MODEL
S5_GROUP = 16
S5_GROUPS = S5_DIM // S5_GROUP
S5_STATE = 64
S5_CHUNK = 128
XA_HEADS = 4
XA_HEAD_DIM = 128
XA_DIM = XA_HEADS * XA_HEAD_DIM
MEM_LEN = 256
D_FF = 4 * D_MODEL
MIX_DIM = GDN_DIM + XA_DIM
GDN_IN = 4 * GDN_DIM + 2 * GDN_HEADS + XA_DIM
S5_IN = S5_DIM + XA_DIM
DN_ALPHA = (2 * DEPTH) ** 0.25
DN_BETA = (8 * DEPTH) ** -0.25
LN_EPS = 1e-5
RMS_EPS = 1e-6
N_GDN_LAYERS = (DEPTH + N_MIXERS - 1) // N_MIXERS
N_S5_LAYERS = DEPTH // N_MIXERS

kernel_name = "hybrid_gdn_s5_memxattn_deepnorm"


def layer_norm(x, g, b):
    xf = x.astype(jnp.float32)
    mu = jnp.mean(xf, axis=-1, keepdims=True)
    var = jnp.mean(jnp.square(xf - mu), axis=-1, keepdims=True)
    return ((xf - mu) * lax.rsqrt(var + LN_EPS) * g.astype(jnp.float32) + b.astype(jnp.float32)).astype(x.dtype)


def l2_normalize(t):
    return t * lax.rsqrt(jnp.sum(jnp.square(t), axis=-1, keepdims=True) + 1e-6)


def causal_depthwise_conv(u, w):
    k_width = w.shape[0]
    length = u.shape[1]
    up = jnp.pad(u, ((0, 0), (k_width - 1, 0), (0, 0)))
    out = up[:, 0:length] * w[0]
    for k in range(1, k_width):
        out = out + up[:, k:k + length] * w[k]
    return out


def gated_delta_rule_chunked(q, k, v, g, beta):
    bsz, length, heads, dk = q.shape
    dv = v.shape[-1]
    c = GDN_CHUNK
    n = length // c

    def to_chunks(t):
        return t.reshape(bsz, n, c, heads, -1).transpose(0, 3, 1, 2, 4)

    q, k, v = to_chunks(q), to_chunks(k), to_chunks(v)
    g = g.reshape(bsz, n, c, heads).transpose(0, 3, 1, 2)
    beta = beta.reshape(bsz, n, c, heads).transpose(0, 3, 1, 2)
    gc = jnp.cumsum(g, axis=-1)

    causal = jnp.tril(jnp.ones((c, c), dtype=bool))
    strict = jnp.tril(jnp.ones((c, c), dtype=bool), k=-1)
    diff = gc[..., :, None] - gc[..., None, :]
    decay = jnp.where(causal, jnp.exp(jnp.where(causal, diff, 0.0)), 0.0)

    kb = k * beta[..., None]
    vb = v * beta[..., None]
    a_strict = jnp.where(strict, jnp.einsum('bhnid,bhnjd->bhnij', kb, k) * decay, 0.0)
    eye = jnp.eye(c, dtype=q.dtype)
    rhs = jnp.concatenate([vb, kb * jnp.exp(gc)[..., None]], axis=-1)
    sol = lax.linalg.triangular_solve(eye + a_strict, rhs, left_side=True, lower=True,
                                      unit_diagonal=True)
    u_blk, w_blk = sol[..., :dv], sol[..., dv:]

    def mv(t):
        return jnp.moveaxis(t, 2, 0)

    def step(state, inp):
        qi, ki, ui, wi, gci, deci = inp
        v_new = ui - jnp.einsum('bhcd,bhde->bhce', wi, state)
        attn = jnp.einsum('bhid,bhjd->bhij', qi, ki) * deci
        o = (jnp.einsum('bhcd,bhde->bhce', qi * jnp.exp(gci)[..., None], state)
             + jnp.einsum('bhij,bhje->bhie', attn, v_new))
        g_last = gci[..., -1]
        k_dec = ki * jnp.exp(g_last[..., None] - gci)[..., None]
        state = state * jnp.exp(g_last)[..., None, None] + jnp.einsum('bhcd,bhce->bhde', k_dec, v_new)
        return state, o

    s0 = jnp.zeros((bsz, heads, dk, dv), dtype=q.dtype)
    _, out = lax.scan(step, s0, (mv(q), mv(k), mv(u_blk), mv(w_blk), mv(gc), mv(decay)))
    return out.transpose(1, 0, 3, 2, 4).reshape(bsz, length, heads, dv)


def gdn_mixer(x, w_in, conv_w, a_log, dt_bias, norm_g):
    bsz, length, _ = x.shape
    f32 = jnp.float32
    proj = x @ w_in
    qkv, z, b_logit, a_logit, xq = jnp.split(
        proj, [3 * GDN_DIM, 4 * GDN_DIM, 4 * GDN_DIM + GDN_HEADS, 4 * GDN_DIM + 2 * GDN_HEADS], axis=-1)
    qkv = jax.nn.silu(causal_depthwise_conv(qkv.astype(f32), conv_w.astype(f32)))
    q, k, v = [t.reshape(bsz, length, GDN_HEADS, GDN_HEAD_DIM) for t in jnp.split(qkv, 3, axis=-1)]
    q = l2_normalize(q) * (GDN_HEAD_DIM ** -0.5)
    k = l2_normalize(k)
    beta = jax.nn.sigmoid(b_logit.astype(f32))
    g = -jnp.exp(a_log.astype(f32)) * jax.nn.softplus(a_logit.astype(f32) + dt_bias.astype(f32))
    o = gated_delta_rule_chunked(q, k, v, g, beta)
    o = o * lax.rsqrt(jnp.mean(jnp.square(o), axis=-1, keepdims=True) + RMS_EPS) * norm_g.astype(f32)
    o = o * jax.nn.silu(z.astype(f32).reshape(bsz, length, GDN_HEADS, GDN_HEAD_DIM))
    return o.reshape(bsz, length, GDN_DIM).astype(x.dtype), xq


def s5_scan(u, abar_re, abar_im, bbar_re, bbar_im, c_re, c_im):
    bsz, length, _ = u.shape
    n = length // S5_CHUNK
    uc = u.reshape(bsz, n, S5_CHUNK, S5_GROUPS, S5_GROUP).transpose(1, 0, 2, 3, 4)
    a_re = jnp.broadcast_to(abar_re, (S5_CHUNK, 1, S5_GROUPS, S5_STATE))
    a_im = jnp.broadcast_to(abar_im, (S5_CHUNK, 1, S5_GROUPS, S5_STATE))

    def combine(e1, e2):
        a1r, a1i, b1r, b1i = e1
        a2r, a2i, b2r, b2i = e2
        return (a2r * a1r - a2i * a1i, a2r * a1i + a2i * a1r,
                a2r * b1r - a2i * b1i + b2r, a2r * b1i + a2i * b1r + b2i)

    def step(h, u_blk):
        h_re, h_im = h
        bu_re = jnp.einsum('bcgi,gpi->cbgp', u_blk, bbar_re)
        bu_im = jnp.einsum('bcgi,gpi->cbgp', u_blk, bbar_im)
        pr, pi, sr, si = lax.associative_scan(combine, (a_re, a_im, bu_re, bu_im), axis=0)
        st_re = sr + pr * h_re - pi * h_im
        st_im = si + pr * h_im + pi * h_re
        y = jnp.einsum('cbgp,gip->bcgi', st_re, c_re) - jnp.einsum('cbgp,gip->bcgi', st_im, c_im)
        return (st_re[-1], st_im[-1]), y

    h0 = jnp.zeros((bsz, S5_GROUPS, S5_STATE), dtype=u.dtype)
    _, ys = lax.scan(step, (h0, h0), uc)
    return ys.transpose(1, 0, 2, 3, 4).reshape(bsz, length, S5_DIM)


def s5_mixer(x, w_in, a_re, a_im, b_re, b_im, c_re, c_im, log_dt, d_skip, w_glu, b_glu):
    f32 = jnp.float32
    proj = x @ w_in
    u, xq = jnp.split(proj, [S5_DIM], axis=-1)
    u = u.astype(f32)
    a_re, a_im = a_re.astype(f32), a_im.astype(f32)
    b_re, b_im = b_re.astype(f32), b_im.astype(f32)
    dt = jnp.exp(log_dt.astype(f32))[:, None]
    mag = jnp.exp(a_re * dt)
    abar_re = mag * jnp.cos(a_im * dt)
    abar_im = mag * jnp.sin(a_im * dt)
    den = jnp.square(a_re) + jnp.square(a_im)
    n_re, n_im = abar_re - 1.0, abar_im
    f_re = (n_re * a_re + n_im * a_im) / den
    f_im = (n_im * a_re - n_re * a_im) / den
    bbar_re = f_re[..., None] * b_re - f_im[..., None] * b_im
    bbar_im = f_re[..., None] * b_im + f_im[..., None] * b_re
    y = s5_scan(u, abar_re, abar_im, bbar_re, bbar_im, c_re.astype(f32), c_im.astype(f32))
    y = y + d_skip.astype(f32) * u
    zg = jax.nn.gelu(y)
    out = zg * jax.nn.sigmoid(zg @ w_glu.astype(f32) + b_glu.astype(f32))
    return out.astype(x.dtype), xq


def memory_attention(xq, mem, w_kv):
    bsz, length, _ = xq.shape
    q = xq.reshape(bsz, length, XA_HEADS, XA_HEAD_DIM)
    k, v = jnp.split(mem @ w_kv, 2, axis=-1)
    k = k.reshape(bsz, -1, XA_HEADS, XA_HEAD_DIM)
    v = v.reshape(bsz, -1, XA_HEADS, XA_HEAD_DIM)
    s = jnp.einsum('blhd,bmhd->bhlm', q, k).astype(jnp.float32) * (XA_HEAD_DIM ** -0.5)
    p = jax.nn.softmax(s, axis=-1).astype(v.dtype)
    o = jnp.einsum('bhlm,bmhd->blhd', p, v)
    return o.reshape(bsz, length, XA_DIM)


def _fwd_setup_inputs(seed: int = 0) -> dict:
    key = jax.random.key(seed)
    ks = jax.random.split(key, 32)
    nrm = jax.random.normal
    f32 = jnp.float32
    D = D_MODEL
    inp = {}
    inp["x"] = nrm(ks[0], (BATCH, SEQ, D), f32)
    inp["mem"] = nrm(ks[1], (BATCH, MEM_LEN, D), f32)
    inp["w_kv_mem"] = nrm(ks[2], (DEPTH, D, 2 * XA_DIM), f32) * D ** -0.5
    inp["w_o"] = nrm(ks[3], (DEPTH, MIX_DIM, D), f32) * (MIX_DIM ** -0.5) * DN_BETA
    inp["ln1_g"] = 1.0 + 0.02 * nrm(ks[4], (DEPTH, D), f32)
    inp["ln1_b"] = 0.02 * nrm(ks[5], (DEPTH, D), f32)
    inp["ln2_g"] = 1.0 + 0.02 * nrm(ks[6], (DEPTH, D), f32)
    inp["ln2_b"] = 0.02 * nrm(ks[7], (DEPTH, D), f32)
    inp["mlp_w1"] = nrm(ks[8], (DEPTH, D, D_FF), f32) * D ** -0.5
    inp["mlp_w2"] = nrm(ks[9], (DEPTH, D_FF, D), f32) * (D_FF ** -0.5) * DN_BETA
    inp["gdn_w_in"] = nrm(ks[10], (N_GDN_LAYERS, D, GDN_IN), f32) * D ** -0.5
    inp["gdn_conv_w"] = nrm(ks[11], (N_GDN_LAYERS, GDN_CONV, 3 * GDN_DIM), f32) * GDN_CONV ** -0.5
    inp["gdn_a_log"] = jnp.log(jax.random.uniform(ks[12], (N_GDN_LAYERS, GDN_HEADS), f32, 1.0, 16.0))
    dt0 = jnp.exp(jax.random.uniform(ks[13], (N_GDN_LAYERS, GDN_HEADS), f32, math.log(1e-3), math.log(1e-1)))
    inp["gdn_dt_bias"] = dt0 + jnp.log(-jnp.expm1(-dt0))
    inp["gdn_norm_g"] = 1.0 + 0.02 * nrm(ks[14], (N_GDN_LAYERS, GDN_HEAD_DIM), f32)
    inp["s5_w_in"] = nrm(ks[15], (N_S5_LAYERS, D, S5_IN), f32) * D ** -0.5
    sh = (N_S5_LAYERS, S5_GROUPS, S5_STATE)
    inp["s5_a_re"] = -0.5 + 0.01 * nrm(ks[16], sh, f32)
    inp["s5_a_im"] = math.pi * jnp.arange(S5_STATE, dtype=f32) + 0.01 * nrm(ks[17], sh, f32)
    inp["s5_b_re"] = nrm(ks[18], sh + (S5_GROUP,), f32) * (2 * S5_GROUP) ** -0.5
    inp["s5_b_im"] = nrm(ks[19], sh + (S5_GROUP,), f32) * (2 * S5_GROUP) ** -0.5
    shc = (N_S5_LAYERS, S5_GROUPS, S5_GROUP, S5_STATE)
    inp["s5_c_re"] = nrm(ks[20], shc, f32) * (2 * S5_STATE) ** -0.5 * 4.0
    inp["s5_c_im"] = nrm(ks[21], shc, f32) * (2 * S5_STATE) ** -0.5 * 4.0
    inp["s5_log_dt"] = jax.random.uniform(ks[22], (N_S5_LAYERS, S5_GROUPS), f32, math.log(1e-3), math.log(1e-1))
    inp["s5_d"] = nrm(ks[23], (N_S5_LAYERS, S5_DIM), f32)
    inp["s5_w_glu"] = nrm(ks[24], (N_S5_LAYERS, S5_DIM, S5_DIM), f32) * S5_DIM ** -0.5
    inp["s5_b_glu"] = 0.02 * nrm(ks[25], (N_S5_LAYERS, S5_DIM), f32)
    return inp


def _fwd_reference(x, mem, w_kv_mem, w_o, ln1_g, ln1_b, ln2_g, ln2_b, mlp_w1, mlp_w2,
              gdn_w_in, gdn_conv_w, gdn_a_log, gdn_dt_bias, gdn_norm_g,
              s5_w_in, s5_a_re, s5_a_im, s5_b_re, s5_b_im, s5_c_re, s5_c_im,
              s5_log_dt, s5_d, s5_w_glu, s5_b_glu):
    for i in range(DEPTH):
        j = i // N_MIXERS
        if i % N_MIXERS == 0:
            mix, xq = gdn_mixer(x, gdn_w_in[j], gdn_conv_w[j], gdn_a_log[j], gdn_dt_bias[j], gdn_norm_g[j])
        else:
            mix, xq = s5_mixer(x, s5_w_in[j], s5_a_re[j], s5_a_im[j], s5_b_re[j], s5_b_im[j],
                               s5_c_re[j], s5_c_im[j], s5_log_dt[j], s5_d[j], s5_w_glu[j], s5_b_glu[j])
        cross = memory_attention(xq, mem, w_kv_mem[i])
        h = jnp.concatenate([mix, cross], axis=-1) @ w_o[i]
        x = layer_norm(DN_ALPHA * x + h, ln1_g[i], ln1_b[i])
        f = jnp.square(jax.nn.relu(x @ mlp_w1[i])) @ mlp_w2[i]
        x = layer_norm(DN_ALPHA * x + f, ln2_g[i], ln2_b[i])
    return x


import jax as _jax
import jax.numpy as _jnp

TWIN_FORMAT = 'train_step'
FWD_PARAMS = ['x', 'mem', 'w_kv_mem', 'w_o', 'ln1_g', 'ln1_b', 'ln2_g', 'ln2_b', 'mlp_w1', 'mlp_w2', 'gdn_w_in', 'gdn_conv_w', 'gdn_a_log', 'gdn_dt_bias', 'gdn_norm_g', 's5_w_in', 's5_a_re', 's5_a_im', 's5_b_re', 's5_b_im', 's5_c_re', 's5_c_im', 's5_log_dt', 's5_d', 's5_w_glu', 's5_b_glu']
TWIN_WEIGHTS = ['w_kv_mem', 'w_o', 'ln1_g', 'ln1_b', 'ln2_g', 'ln2_b', 'mlp_w1', 'mlp_w2', 'gdn_w_in', 'gdn_conv_w', 'gdn_a_log', 'gdn_dt_bias', 'gdn_norm_g', 's5_w_in', 's5_a_re', 's5_a_im', 's5_b_re', 's5_b_im', 's5_c_re', 's5_c_im', 's5_log_dt', 's5_d', 's5_w_glu', 's5_b_glu']
TWIN_DIFF_INPUT = 'x'
TWIN_INPUTS = ['x', 'mem', 'w_kv_mem', 'w_o', 'ln1_g', 'ln1_b', 'ln2_g', 'ln2_b', 'mlp_w1', 'mlp_w2', 'gdn_w_in', 'gdn_conv_w', 'gdn_a_log', 'gdn_dt_bias', 'gdn_norm_g', 's5_w_in', 's5_a_re', 's5_a_im', 's5_b_re', 's5_b_im', 's5_c_re', 's5_c_im', 's5_log_dt', 's5_d', 's5_w_glu', 's5_b_glu', 'loss_target', 'm_w_kv_mem', 'm_w_o', 'm_ln1_g', 'm_ln1_b', 'm_ln2_g', 'm_ln2_b', 'm_mlp_w1', 'm_mlp_w2', 'm_gdn_w_in', 'm_gdn_conv_w', 'm_gdn_a_log', 'm_gdn_dt_bias', 'm_gdn_norm_g', 'm_s5_w_in', 'm_s5_a_re', 'm_s5_a_im', 'm_s5_b_re', 'm_s5_b_im', 'm_s5_c_re', 'm_s5_c_im', 'm_s5_log_dt', 'm_s5_d', 'm_s5_w_glu', 'm_s5_b_glu', 'v_w_kv_mem', 'v_w_o', 'v_ln1_g', 'v_ln1_b', 'v_ln2_g', 'v_ln2_b', 'v_mlp_w1', 'v_mlp_w2', 'v_gdn_w_in', 'v_gdn_conv_w', 'v_gdn_a_log', 'v_gdn_dt_bias', 'v_gdn_norm_g', 'v_s5_w_in', 'v_s5_a_re', 'v_s5_a_im', 'v_s5_b_re', 'v_s5_b_im', 'v_s5_c_re', 'v_s5_c_im', 'v_s5_log_dt', 'v_s5_d', 'v_s5_w_glu', 'v_s5_b_glu']
TWIN_OUTPUTS = ['loss', 'grad_x', 'grad_w_kv_mem', 'grad_w_o', 'grad_ln1_g', 'grad_ln1_b', 'grad_ln2_g', 'grad_ln2_b', 'grad_mlp_w1', 'grad_mlp_w2', 'grad_gdn_w_in', 'grad_gdn_conv_w', 'grad_gdn_a_log', 'grad_gdn_dt_bias', 'grad_gdn_norm_g', 'grad_s5_w_in', 'grad_s5_a_re', 'grad_s5_a_im', 'grad_s5_b_re', 'grad_s5_b_im', 'grad_s5_c_re', 'grad_s5_c_im', 'grad_s5_log_dt', 'grad_s5_d', 'grad_s5_w_glu', 'grad_s5_b_glu', 'delta_w_kv_mem', 'delta_w_o', 'delta_ln1_g', 'delta_ln1_b', 'delta_ln2_g', 'delta_ln2_b', 'delta_mlp_w1', 'delta_mlp_w2', 'delta_gdn_w_in', 'delta_gdn_conv_w', 'delta_gdn_a_log', 'delta_gdn_dt_bias', 'delta_gdn_norm_g', 'delta_s5_w_in', 'delta_s5_a_re', 'delta_s5_a_im', 'delta_s5_b_re', 'delta_s5_b_im', 'delta_s5_c_re', 'delta_s5_c_im', 'delta_s5_log_dt', 'delta_s5_d', 'delta_s5_w_glu', 'delta_s5_b_glu', 'new_m_w_kv_mem', 'new_m_w_o', 'new_m_ln1_g', 'new_m_ln1_b', 'new_m_ln2_g', 'new_m_ln2_b', 'new_m_mlp_w1', 'new_m_mlp_w2', 'new_m_gdn_w_in', 'new_m_gdn_conv_w', 'new_m_gdn_a_log', 'new_m_gdn_dt_bias', 'new_m_gdn_norm_g', 'new_m_s5_w_in', 'new_m_s5_a_re', 'new_m_s5_a_im', 'new_m_s5_b_re', 'new_m_s5_b_im', 'new_m_s5_c_re', 'new_m_s5_c_im', 'new_m_s5_log_dt', 'new_m_s5_d', 'new_m_s5_w_glu', 'new_m_s5_b_glu', 'new_v_w_kv_mem', 'new_v_w_o', 'new_v_ln1_g', 'new_v_ln1_b', 'new_v_ln2_g', 'new_v_ln2_b', 'new_v_mlp_w1', 'new_v_mlp_w2', 'new_v_gdn_w_in', 'new_v_gdn_conv_w', 'new_v_gdn_a_log', 'new_v_gdn_dt_bias', 'new_v_gdn_norm_g', 'new_v_s5_w_in', 'new_v_s5_a_re', 'new_v_s5_a_im', 'new_v_s5_b_re', 'new_v_s5_b_im', 'new_v_s5_c_re', 'new_v_s5_c_im', 'new_v_s5_log_dt', 'new_v_s5_d', 'new_v_s5_w_glu', 'new_v_s5_b_glu']
TWIN_LEAF_KINDS = {'loss': 'loss', 'grad_x': 'grad_x', 'grad_w_kv_mem': 'grad_w', 'grad_w_o': 'grad_w', 'grad_ln1_g': 'grad_w', 'grad_ln1_b': 'grad_w', 'grad_ln2_g': 'grad_w', 'grad_ln2_b': 'grad_w', 'grad_mlp_w1': 'grad_w', 'grad_mlp_w2': 'grad_w', 'grad_gdn_w_in': 'grad_w', 'grad_gdn_conv_w': 'grad_w', 'grad_gdn_a_log': 'grad_w', 'grad_gdn_dt_bias': 'grad_w', 'grad_gdn_norm_g': 'grad_w', 'grad_s5_w_in': 'grad_w', 'grad_s5_a_re': 'grad_w', 'grad_s5_a_im': 'grad_w', 'grad_s5_b_re': 'grad_w', 'grad_s5_b_im': 'grad_w', 'grad_s5_c_re': 'grad_w', 'grad_s5_c_im': 'grad_w', 'grad_s5_log_dt': 'grad_w', 'grad_s5_d': 'grad_w', 'grad_s5_w_glu': 'grad_w', 'grad_s5_b_glu': 'grad_w', 'delta_w_kv_mem': 'delta_w', 'delta_w_o': 'delta_w', 'delta_ln1_g': 'delta_w', 'delta_ln1_b': 'delta_w', 'delta_ln2_g': 'delta_w', 'delta_ln2_b': 'delta_w', 'delta_mlp_w1': 'delta_w', 'delta_mlp_w2': 'delta_w', 'delta_gdn_w_in': 'delta_w', 'delta_gdn_conv_w': 'delta_w', 'delta_gdn_a_log': 'delta_w', 'delta_gdn_dt_bias': 'delta_w', 'delta_gdn_norm_g': 'delta_w', 'delta_s5_w_in': 'delta_w', 'delta_s5_a_re': 'delta_w', 'delta_s5_a_im': 'delta_w', 'delta_s5_b_re': 'delta_w', 'delta_s5_b_im': 'delta_w', 'delta_s5_c_re': 'delta_w', 'delta_s5_c_im': 'delta_w', 'delta_s5_log_dt': 'delta_w', 'delta_s5_d': 'delta_w', 'delta_s5_w_glu': 'delta_w', 'delta_s5_b_glu': 'delta_w', 'new_m_w_kv_mem': 'new_m', 'new_m_w_o': 'new_m', 'new_m_ln1_g': 'new_m', 'new_m_ln1_b': 'new_m', 'new_m_ln2_g': 'new_m', 'new_m_ln2_b': 'new_m', 'new_m_mlp_w1': 'new_m', 'new_m_mlp_w2': 'new_m', 'new_m_gdn_w_in': 'new_m', 'new_m_gdn_conv_w': 'new_m', 'new_m_gdn_a_log': 'new_m', 'new_m_gdn_dt_bias': 'new_m', 'new_m_gdn_norm_g': 'new_m', 'new_m_s5_w_in': 'new_m', 'new_m_s5_a_re': 'new_m', 'new_m_s5_a_im': 'new_m', 'new_m_s5_b_re': 'new_m', 'new_m_s5_b_im': 'new_m', 'new_m_s5_c_re': 'new_m', 'new_m_s5_c_im': 'new_m', 'new_m_s5_log_dt': 'new_m', 'new_m_s5_d': 'new_m', 'new_m_s5_w_glu': 'new_m', 'new_m_s5_b_glu': 'new_m', 'new_v_w_kv_mem': 'new_v', 'new_v_w_o': 'new_v', 'new_v_ln1_g': 'new_v', 'new_v_ln1_b': 'new_v', 'new_v_ln2_g': 'new_v', 'new_v_ln2_b': 'new_v', 'new_v_mlp_w1': 'new_v', 'new_v_mlp_w2': 'new_v', 'new_v_gdn_w_in': 'new_v', 'new_v_gdn_conv_w': 'new_v', 'new_v_gdn_a_log': 'new_v', 'new_v_gdn_dt_bias': 'new_v', 'new_v_gdn_norm_g': 'new_v', 'new_v_s5_w_in': 'new_v', 'new_v_s5_a_re': 'new_v', 'new_v_s5_a_im': 'new_v', 'new_v_s5_b_re': 'new_v', 'new_v_s5_b_im': 'new_v', 'new_v_s5_c_re': 'new_v', 'new_v_s5_c_im': 'new_v', 'new_v_s5_log_dt': 'new_v', 'new_v_s5_d': 'new_v', 'new_v_s5_w_glu': 'new_v', 'new_v_s5_b_glu': 'new_v'}


def _forward(args):
    return _fwd_reference(*[args[k] for k in FWD_PARAMS])


def _output_shape():
    def fwd():
        inp = _fwd_setup_inputs(0)
        return _fwd_reference(*[inp[k] for k in FWD_PARAMS])
    out = _jax.eval_shape(fwd)
    return out.shape, out.dtype

N_MICROBATCH = 1
ADAM_LR = 0.001
ADAM_B1 = 0.9
ADAM_B2 = 0.999
ADAM_EPS = 1e-08
ADAM_WD = 0.01
ADAM_STEP = 10
PER_EXAMPLE_BATCH_AXIS = {'x': 0, 'mem': 0, 'loss_target': 0}
SHARED_INPUTS = []
_WEIGHT_DTYPES = {'w_kv_mem': _jnp.float32, 'w_o': _jnp.float32, 'ln1_g': _jnp.float32, 'ln1_b': _jnp.float32, 'ln2_g': _jnp.float32, 'ln2_b': _jnp.float32, 'mlp_w1': _jnp.float32, 'mlp_w2': _jnp.float32, 'gdn_w_in': _jnp.float32, 'gdn_conv_w': _jnp.float32, 'gdn_a_log': _jnp.float32, 'gdn_dt_bias': _jnp.float32, 'gdn_norm_g': _jnp.float32, 's5_w_in': _jnp.float32, 's5_a_re': _jnp.float32, 's5_a_im': _jnp.float32, 's5_b_re': _jnp.float32, 's5_b_im': _jnp.float32, 's5_c_re': _jnp.float32, 's5_c_im': _jnp.float32, 's5_log_dt': _jnp.float32, 's5_d': _jnp.float32, 's5_w_glu': _jnp.float32, 's5_b_glu': _jnp.float32}
MOMENT_SCALE = {'w_kv_mem': 6.100033e-03, 'w_o': 8.347643e-02, 'ln1_g': 1.810647e+00, 'ln1_b': 1.042989e+00, 'ln2_g': 3.219689e+01, 'ln2_b': 7.429979e+00, 'mlp_w1': 4.509463e-02, 'mlp_w2': 2.953029e-01, 'gdn_w_in': 2.495705e-02, 'gdn_conv_w': 2.519024e-02, 'gdn_a_log': 1.470498e-01, 'gdn_dt_bias': 1.424454e-01, 'gdn_norm_g': 9.455119e-02, 's5_w_in': 1.814336e-02, 's5_a_re': 7.742595e-03, 's5_a_im': 6.736529e-03, 's5_b_re': 3.713683e-03, 's5_b_im': 3.676677e-03, 's5_c_re': 1.738113e-03, 's5_c_im': 1.885620e-03, 's5_log_dt': 2.511222e+00, 's5_d': 3.456354e-02, 's5_w_glu': 7.107485e-03, 's5_b_glu': 1.495304e-02}


def _to_microbatches(a, axis):
    t = _jnp.moveaxis(a, axis, 0)
    t = t.reshape((N_MICROBATCH, t.shape[0] // N_MICROBATCH) + t.shape[1:])
    return _jnp.moveaxis(t, 1, axis + 1)


def setup_inputs(seed: int = 0) -> dict:
    inp = _fwd_setup_inputs(seed)
    key = _jax.random.fold_in(_jax.random.key(seed), 7919)
    shape, _ = _output_shape()
    out = dict(inp)
    out["loss_target"] = _jax.random.normal(_jax.random.fold_in(key, 0), shape, _jnp.float32)
    for i, name in enumerate(TWIN_WEIGHTS):
        w = inp[name].astype(_jnp.float32)
        if MOMENT_SCALE is None:
            s = _jnp.sqrt(_jnp.mean(_jnp.square(w)) + 1e-30)
        else:
            s = MOMENT_SCALE[name]
        km, kv = _jax.random.split(_jax.random.fold_in(key, i + 1))
        out[name] = w
        out["m_" + name] = s * _jax.random.normal(km, w.shape, _jnp.float32)
        out["v_" + name] = (s * s) * _jax.random.uniform(kv, w.shape, _jnp.float32, 0.5, 1.5)
    if N_MICROBATCH > 1:
        for name, axis in PER_EXAMPLE_BATCH_AXIS.items():
            out[name] = _to_microbatches(out[name], axis)
    return {'x': out['x'], 'mem': out['mem'], 'w_kv_mem': out['w_kv_mem'], 'w_o': out['w_o'], 'ln1_g': out['ln1_g'], 'ln1_b': out['ln1_b'], 'ln2_g': out['ln2_g'], 'ln2_b': out['ln2_b'], 'mlp_w1': out['mlp_w1'], 'mlp_w2': out['mlp_w2'], 'gdn_w_in': out['gdn_w_in'], 'gdn_conv_w': out['gdn_conv_w'], 'gdn_a_log': out['gdn_a_log'], 'gdn_dt_bias': out['gdn_dt_bias'], 'gdn_norm_g': out['gdn_norm_g'], 's5_w_in': out['s5_w_in'], 's5_a_re': out['s5_a_re'], 's5_a_im': out['s5_a_im'], 's5_b_re': out['s5_b_re'], 's5_b_im': out['s5_b_im'], 's5_c_re': out['s5_c_re'], 's5_c_im': out['s5_c_im'], 's5_log_dt': out['s5_log_dt'], 's5_d': out['s5_d'], 's5_w_glu': out['s5_w_glu'], 's5_b_glu': out['s5_b_glu'], 'loss_target': out['loss_target'], 'm_w_kv_mem': out['m_w_kv_mem'], 'm_w_o': out['m_w_o'], 'm_ln1_g': out['m_ln1_g'], 'm_ln1_b': out['m_ln1_b'], 'm_ln2_g': out['m_ln2_g'], 'm_ln2_b': out['m_ln2_b'], 'm_mlp_w1': out['m_mlp_w1'], 'm_mlp_w2': out['m_mlp_w2'], 'm_gdn_w_in': out['m_gdn_w_in'], 'm_gdn_conv_w': out['m_gdn_conv_w'], 'm_gdn_a_log': out['m_gdn_a_log'], 'm_gdn_dt_bias': out['m_gdn_dt_bias'], 'm_gdn_norm_g': out['m_gdn_norm_g'], 'm_s5_w_in': out['m_s5_w_in'], 'm_s5_a_re': out['m_s5_a_re'], 'm_s5_a_im': out['m_s5_a_im'], 'm_s5_b_re': out['m_s5_b_re'], 'm_s5_b_im': out['m_s5_b_im'], 'm_s5_c_re': out['m_s5_c_re'], 'm_s5_c_im': out['m_s5_c_im'], 'm_s5_log_dt': out['m_s5_log_dt'], 'm_s5_d': out['m_s5_d'], 'm_s5_w_glu': out['m_s5_w_glu'], 'm_s5_b_glu': out['m_s5_b_glu'], 'v_w_kv_mem': out['v_w_kv_mem'], 'v_w_o': out['v_w_o'], 'v_ln1_g': out['v_ln1_g'], 'v_ln1_b': out['v_ln1_b'], 'v_ln2_g': out['v_ln2_g'], 'v_ln2_b': out['v_ln2_b'], 'v_mlp_w1': out['v_mlp_w1'], 'v_mlp_w2': out['v_mlp_w2'], 'v_gdn_w_in': out['v_gdn_w_in'], 'v_gdn_conv_w': out['v_gdn_conv_w'], 'v_gdn_a_log': out['v_gdn_a_log'], 'v_gdn_dt_bias': out['v_gdn_dt_bias'], 'v_gdn_norm_g': out['v_gdn_norm_g'], 'v_s5_w_in': out['v_s5_w_in'], 'v_s5_a_re': out['v_s5_a_re'], 'v_s5_a_im': out['v_s5_a_im'], 'v_s5_b_re': out['v_s5_b_re'], 'v_s5_b_im': out['v_s5_b_im'], 'v_s5_c_re': out['v_s5_c_re'], 'v_s5_c_im': out['v_s5_c_im'], 'v_s5_log_dt': out['v_s5_log_dt'], 'v_s5_d': out['v_s5_d'], 'v_s5_w_glu': out['v_s5_w_glu'], 'v_s5_b_glu': out['v_s5_b_glu']}


def _loss(weights, diff, rest, loss_target):
    with _jax.named_scope("forward"):
        args = {**rest, TWIN_DIFF_INPUT: diff, **{k: w.astype(_WEIGHT_DTYPES[k]) for k, w in weights.items()}}
        y = _forward(args)
    with _jax.named_scope("loss_head"):
        err = _jnp.square(y.astype(_jnp.float32) - loss_target)
        return 0.5 * _jnp.sum(_jnp.mean(err, axis=-1)) if err.ndim else 0.5 * err


def _adamw(w, g, m, v):
    m = ADAM_B1 * m + (1.0 - ADAM_B1) * g
    v = ADAM_B2 * v + (1.0 - ADAM_B2) * _jnp.square(g)
    m_hat = m / (1.0 - ADAM_B1 ** ADAM_STEP)
    v_hat = v / (1.0 - ADAM_B2 ** ADAM_STEP)
    delta = -ADAM_LR * (m_hat / (_jnp.sqrt(v_hat) + ADAM_EPS) + ADAM_WD * w)
    return delta, m, v


def reference(x, mem, w_kv_mem, w_o, ln1_g, ln1_b, ln2_g, ln2_b, mlp_w1, mlp_w2, gdn_w_in, gdn_conv_w, gdn_a_log, gdn_dt_bias, gdn_norm_g, s5_w_in, s5_a_re, s5_a_im, s5_b_re, s5_b_im, s5_c_re, s5_c_im, s5_log_dt, s5_d, s5_w_glu, s5_b_glu, loss_target, m_w_kv_mem, m_w_o, m_ln1_g, m_ln1_b, m_ln2_g, m_ln2_b, m_mlp_w1, m_mlp_w2, m_gdn_w_in, m_gdn_conv_w, m_gdn_a_log, m_gdn_dt_bias, m_gdn_norm_g, m_s5_w_in, m_s5_a_re, m_s5_a_im, m_s5_b_re, m_s5_b_im, m_s5_c_re, m_s5_c_im, m_s5_log_dt, m_s5_d, m_s5_w_glu, m_s5_b_glu, v_w_kv_mem, v_w_o, v_ln1_g, v_ln1_b, v_ln2_g, v_ln2_b, v_mlp_w1, v_mlp_w2, v_gdn_w_in, v_gdn_conv_w, v_gdn_a_log, v_gdn_dt_bias, v_gdn_norm_g, v_s5_w_in, v_s5_a_re, v_s5_a_im, v_s5_b_re, v_s5_b_im, v_s5_c_re, v_s5_c_im, v_s5_log_dt, v_s5_d, v_s5_w_glu, v_s5_b_glu):
    given = dict(x=x, mem=mem, w_kv_mem=w_kv_mem, w_o=w_o, ln1_g=ln1_g, ln1_b=ln1_b, ln2_g=ln2_g, ln2_b=ln2_b, mlp_w1=mlp_w1, mlp_w2=mlp_w2, gdn_w_in=gdn_w_in, gdn_conv_w=gdn_conv_w, gdn_a_log=gdn_a_log, gdn_dt_bias=gdn_dt_bias, gdn_norm_g=gdn_norm_g, s5_w_in=s5_w_in, s5_a_re=s5_a_re, s5_a_im=s5_a_im, s5_b_re=s5_b_re, s5_b_im=s5_b_im, s5_c_re=s5_c_re, s5_c_im=s5_c_im, s5_log_dt=s5_log_dt, s5_d=s5_d, s5_w_glu=s5_w_glu, s5_b_glu=s5_b_glu, loss_target=loss_target, m_w_kv_mem=m_w_kv_mem, m_w_o=m_w_o, m_ln1_g=m_ln1_g, m_ln1_b=m_ln1_b, m_ln2_g=m_ln2_g, m_ln2_b=m_ln2_b, m_mlp_w1=m_mlp_w1, m_mlp_w2=m_mlp_w2, m_gdn_w_in=m_gdn_w_in, m_gdn_conv_w=m_gdn_conv_w, m_gdn_a_log=m_gdn_a_log, m_gdn_dt_bias=m_gdn_dt_bias, m_gdn_norm_g=m_gdn_norm_g, m_s5_w_in=m_s5_w_in, m_s5_a_re=m_s5_a_re, m_s5_a_im=m_s5_a_im, m_s5_b_re=m_s5_b_re, m_s5_b_im=m_s5_b_im, m_s5_c_re=m_s5_c_re, m_s5_c_im=m_s5_c_im, m_s5_log_dt=m_s5_log_dt, m_s5_d=m_s5_d, m_s5_w_glu=m_s5_w_glu, m_s5_b_glu=m_s5_b_glu, v_w_kv_mem=v_w_kv_mem, v_w_o=v_w_o, v_ln1_g=v_ln1_g, v_ln1_b=v_ln1_b, v_ln2_g=v_ln2_g, v_ln2_b=v_ln2_b, v_mlp_w1=v_mlp_w1, v_mlp_w2=v_mlp_w2, v_gdn_w_in=v_gdn_w_in, v_gdn_conv_w=v_gdn_conv_w, v_gdn_a_log=v_gdn_a_log, v_gdn_dt_bias=v_gdn_dt_bias, v_gdn_norm_g=v_gdn_norm_g, v_s5_w_in=v_s5_w_in, v_s5_a_re=v_s5_a_re, v_s5_a_im=v_s5_a_im, v_s5_b_re=v_s5_b_re, v_s5_b_im=v_s5_b_im, v_s5_c_re=v_s5_c_re, v_s5_c_im=v_s5_c_im, v_s5_log_dt=v_s5_log_dt, v_s5_d=v_s5_d, v_s5_w_glu=v_s5_w_glu, v_s5_b_glu=v_s5_b_glu)
    weights = {n: given[n] for n in TWIN_WEIGHTS}
    shared = {n: given[n] for n in SHARED_INPUTS}
    per_example = {n: given[n] for n in ['x', 'mem']}
    grad_fn = _jax.value_and_grad(_loss, argnums=(0, 1))

    def one_microbatch(ex, loss_target):
        ex = dict(ex)
        diff = ex.pop(TWIN_DIFF_INPUT)
        return grad_fn(weights, diff, {**shared, **ex}, loss_target)

    if N_MICROBATCH == 1:
        loss, (grad_w, grad_x) = one_microbatch(per_example, given["loss_target"])
    else:
        def body(carry, xs):
            loss_sum, grad_sum = carry
            l_k, (gw_k, gx_k) = one_microbatch(xs[0], xs[1])
            with _jax.named_scope("update"):
                return (loss_sum + l_k, _jax.tree.map(_jnp.add, grad_sum, gw_k)), gx_k

        init = (_jnp.zeros((), _jnp.float32), _jax.tree.map(_jnp.zeros_like, weights))
        (loss, grad_w), grad_x = _jax.lax.scan(body, init, (per_example, given["loss_target"]))
    with _jax.named_scope("update"):
        delta_w, new_m, new_v = {}, {}, {}
        for n in TWIN_WEIGHTS:
            delta_w[n], new_m[n], new_v[n] = _adamw(weights[n], grad_w[n], given["m_" + n], given["v_" + n])
    return (loss, grad_x, *[grad_w[n] for n in TWIN_WEIGHTS], *[delta_w[n] for n in TWIN_WEIGHTS],
            *[new_m[n] for n in TWIN_WEIGHTS], *[new_v[n] for n in TWIN_WEIGHTS])
```

```python
import functools
import math

import jax
import jax.numpy as jnp
from jax import lax
from jax.experimental import pallas as pl
from jax.experimental.pallas import tpu as pltpu

F32 = jnp.float32
BF16 = jnp.bfloat16
MESH = pl.DeviceIdType.MESH

D_MODEL = 1024
DEPTH = 4
GDN_HEADS = 8
HEAD_DIM = 128
GDN_CONV = 4
GDN_CHUNK = 64
S5_GROUPS = 64
S5_GROUP = 16
S5_STATE = 64
XA_HEADS = 4
XA_DIM = 512
D_FF = 4096
DN_ALPHA = (2 * DEPTH) ** 0.25
LN_EPS = 1e-5
RMS_EPS = 1e-6
ADAM_LR, ADAM_B1, ADAM_B2, ADAM_EPS, ADAM_WD, ADAM_STEP = 0.001, 0.9, 0.999, 1e-08, 0.01, 10

VMEM_LIMIT_V7X = 56 * 1024 * 1024
LANES = 128
SUBLANES = 8
S5_T = 16
S5_TILES = D_MODEL // LANES
N_CHIPS = 4
N_DEV = 8


def _params(sem):
    return pltpu.CompilerParams(dimension_semantics=sem, vmem_limit_bytes=VMEM_LIMIT_V7X)


def _tile(n, pref):
    if n <= pref:
        return n
    t = (pref // LANES) * LANES
    while n % t:
        t -= LANES
    return t


def _row_tile(n, pref):
    t = min(pref, n) // SUBLANES * SUBLANES
    while n % t:
        t -= SUBLANES
    return t


def _mm(a, b, *, ta=False, tb=False, acc=None, name, tm=512, tn=512, tk=1024):
    batched = a.ndim == 3
    if not batched:
        a, b = a[None], b[None]
        acc = None if acc is None else acc[None]
    nb = a.shape[0]
    k_dim, m_dim = (a.shape[1], a.shape[2]) if ta else (a.shape[2], a.shape[1])
    n_dim = b.shape[1] if tb else b.shape[2]
    assert (b.shape[2] if tb else b.shape[1]) == k_dim, (a.shape, b.shape, ta, tb)
    tm, tn, tk = _tile(m_dim, tm), _tile(n_dim, tn), _tile(k_dim, tk)
    nk = k_dim // tk
    a_spec = (pl.BlockSpec((None, tk, tm), lambda g, i, j, k: (g, k, i)) if ta
              else pl.BlockSpec((None, tm, tk), lambda g, i, j, k: (g, i, k)))
    b_spec = (pl.BlockSpec((None, tn, tk), lambda g, i, j, k: (g, j, k)) if tb
              else pl.BlockSpec((None, tk, tn), lambda g, i, j, k: (g, k, j)))
    o_spec = pl.BlockSpec((None, tm, tn), lambda g, i, j, k: (g, i, j))
    dn = (((0 if ta else 1,), (1 if tb else 0,)), ((), ()))
    has_acc = acc is not None

    def body(*refs):
        a_ref, b_ref = refs[0], refs[1]
        o_ref = refs[-1]
        k = pl.program_id(3)
        p = lax.dot_general(a_ref[...].astype(BF16), b_ref[...].astype(BF16), dn,
                            preferred_element_type=F32)

        @pl.when(k == 0)
        def _():
            o_ref[...] = p + refs[2][...] if has_acc else p

        @pl.when(k > 0)
        def _():
            o_ref[...] += p

    out = pl.pallas_call(
        body, name=name,
        out_shape=jax.ShapeDtypeStruct((nb, m_dim, n_dim), F32),
        grid=(nb, m_dim // tm, n_dim // tn, nk),
        in_specs=[a_spec, b_spec] + ([o_spec] if has_acc else []),
        out_specs=o_spec,
        compiler_params=_params(("parallel", "parallel", "parallel", "arbitrary")),
    )(*([a, b] + ([acc] if has_acc else [])))
    return out if batched else out[0]


def _rowwise(f, rows, params, row_out, acc_out, *, tm, name):
    length = rows[0].shape[0]
    tm = _row_tile(length, tm)
    nr, npar, nro = len(rows), len(params), len(row_out)

    def body(*refs):
        ins = [r[...] for r in refs[:nr + npar]]
        outs = refs[nr + npar:]
        r_o, a_o = f(*ins)
        for ref, val in zip(outs[:nro], r_o):
            ref[...] = val.astype(ref.dtype)
        i = pl.program_id(0)
        for ref, val in zip(outs[nro:], a_o):
            @pl.when(i == 0)
            def _(ref=ref, val=val):
                ref[...] = val.astype(ref.dtype)

            @pl.when(i > 0)
            def _(ref=ref, val=val):
                ref[...] += val.astype(ref.dtype)

    in_specs = ([pl.BlockSpec((tm, r.shape[1]), lambda i: (i, 0)) for r in rows]
                + [pl.BlockSpec(p.shape, lambda i: (0, 0)) for p in params])
    out_specs = ([pl.BlockSpec((tm, w), lambda i: (i, 0)) for w, _ in row_out]
                 + [pl.BlockSpec(s, lambda i: (0, 0)) for s, _ in acc_out])
    out_shape = ([jax.ShapeDtypeStruct((length, w), dt) for w, dt in row_out]
                 + [jax.ShapeDtypeStruct(s, dt) for s, dt in acc_out])
    res = pl.pallas_call(
        body, name=name, out_shape=out_shape, grid=(length // tm,),
        in_specs=in_specs, out_specs=out_specs,
        compiler_params=_params(("arbitrary",) if acc_out else ("parallel",)),
    )(*rows, *params)
    return res[:nro], res[nro:]


def _rw_fwd(f, rows, params, *, tm, name):
    tm_ = _row_tile(rows[0].shape[0], tm)
    shapes = jax.eval_shape(f, *[jax.ShapeDtypeStruct((tm_, r.shape[1]), r.dtype) for r in rows],
                            *[jax.ShapeDtypeStruct(p.shape, p.dtype) for p in params])
    row_out = [(s.shape[1], s.dtype) for s in shapes]
    outs, _ = _rowwise(lambda *v: (f(*v), ()), rows, params, row_out, [], tm=tm, name=name)
    return outs


def _rw_bwd(f, rows, params, cots, *, row_grad, param_grad, tm, name):
    nr, npar, nct = len(rows), len(params), len(cots)

    def g(*vals):
        prim = vals[:nr] + vals[nr + nct:]
        ct = vals[nr:nr + nct]
        _, vjp = jax.vjp(f, *prim)
        grads = vjp(tuple(ct))
        return (tuple(grads[i] for i in range(nr) if row_grad[i]),
                tuple(grads[nr + i] for i in range(npar) if param_grad[i]))

    row_out = [(rows[i].shape[1], F32) for i in range(nr) if row_grad[i]]
    acc_out = [(params[i].shape, F32) for i in range(npar) if param_grad[i]]
    return _rowwise(g, list(rows) + list(cots), params, row_out, acc_out, tm=tm, name=name)


def _f_ln_res(x, h, g, b):
    pre = DN_ALPHA * x + h
    mu = jnp.mean(pre, axis=-1, keepdims=True)
    d = pre - mu
    var = jnp.mean(d * d, axis=-1, keepdims=True)
    return (d * lax.rsqrt(var + LN_EPS) * g + b,)


def _f_relu2(h):
    r = jnp.maximum(h, 0.0)
    return (r * r,)


def _silu(t):
    return t * jax.nn.sigmoid(t)


def _f_gdn_qkv(c):
    a = _silu(c)
    outs = []
    for part, scale in ((0, HEAD_DIM ** -0.5), (1, 1.0)):
        heads = []
        for h in range(GDN_HEADS):
            t = a[:, part * D_MODEL + h * HEAD_DIM: part * D_MODEL + (h + 1) * HEAD_DIM]
            t = t * lax.rsqrt(jnp.sum(t * t, axis=-1, keepdims=True) + 1e-6)
            heads.append(t * scale if scale != 1.0 else t)
        outs.append(jnp.concatenate(heads, axis=-1))
    outs.append(a[:, 2 * D_MODEL:])
    return tuple(outs)


def _f_gdn_out(o, z, norm_g):
    heads = []
    for h in range(GDN_HEADS):
        t = o[:, h * HEAD_DIM:(h + 1) * HEAD_DIM]
        t = t * lax.rsqrt(jnp.mean(t * t, axis=-1, keepdims=True) + RMS_EPS) * norm_g
        heads.append(t)
    return (jnp.concatenate(heads, axis=-1) * _silu(z),)


def _f_attn(xq, kmem, vmem):
    heads = []
    for h in range(XA_HEADS):
        sl = slice(h * HEAD_DIM, (h + 1) * HEAD_DIM)
        s = lax.dot_general(xq[:, sl].astype(BF16), kmem[:, sl].astype(BF16),
                            (((1,), (1,)), ((), ())), preferred_element_type=F32) * (HEAD_DIM ** -0.5)
        m = lax.stop_gradient(jnp.max(s, axis=-1, keepdims=True))
        e = jnp.exp(s - m)
        p = e / jnp.sum(e, axis=-1, keepdims=True)
        heads.append(jnp.dot(p.astype(BF16), vmem[:, sl].astype(BF16), preferred_element_type=F32))
    return (jnp.concatenate(heads, axis=-1),)


def _f_s5_gelu(y, u, d):
    return (jax.nn.gelu(y + d * u),)


def _f_s5_gate(zg, t, b):
    return (zg * jax.nn.sigmoid(t + b),)


def _f_add(a, b):
    return (a + b,)


def _f_add4(a, b, c, d):
    return (((a + b) + c) + d,)


def _f_adamw(w, g, m, v):
    m = ADAM_B1 * m + (1.0 - ADAM_B1) * g
    v = ADAM_B2 * v + (1.0 - ADAM_B2) * jnp.square(g)
    m_hat = m / (1.0 - ADAM_B1 ** ADAM_STEP)
    v_hat = v / (1.0 - ADAM_B2 ** ADAM_STEP)
    delta = -ADAM_LR * (m_hat / (jnp.sqrt(v_hat) + ADAM_EPS) + ADAM_WD * w)
    return delta, m, v


def _conv_fwd(u, w, *, tm, name):
    length, chans = u.shape
    tm = min(tm, length)
    tc = _tile(chans, 1024)
    hb = tm // SUBLANES

    def body(cur_ref, prev_ref, w_ref, o_ref, buf):
        i = pl.program_id(1)
        buf[0:SUBLANES, :] = jnp.where(i > 0, prev_ref[...], 0.0)
        buf[SUBLANES:, :] = cur_ref[...]
        acc = buf[pl.ds(SUBLANES - 3, tm), :] * w_ref[0:1, :]
        for k in range(1, GDN_CONV):
            acc = acc + buf[pl.ds(SUBLANES - 3 + k, tm), :] * w_ref[k:k + 1, :]
        o_ref[...] = acc

    return pl.pallas_call(
        body, name=name, out_shape=jax.ShapeDtypeStruct(u.shape, F32),
        grid=(chans // tc, length // tm),
        in_specs=[pl.BlockSpec((tm, tc), lambda j, i: (i, j)),
                  pl.BlockSpec((SUBLANES, tc), lambda j, i: (jnp.maximum(i * hb - 1, 0), j)),
                  pl.BlockSpec((GDN_CONV, tc), lambda j, i: (0, j))],
        out_specs=pl.BlockSpec((tm, tc), lambda j, i: (i, j)),
        scratch_shapes=[pltpu.VMEM((tm + SUBLANES, tc), F32)],
        compiler_params=_params(("parallel", "parallel")),
    )(u, u, w)


def _conv_bwd(u, w, dc, *, tm, name):
    length, chans = u.shape
    tm = min(tm, length)
    tc = _tile(chans, 1024)
    hb = tm // SUBLANES
    last = length // tm - 1

    def body(u_ref, uprev_ref, dc_ref, dcnext_ref, w_ref, du_ref, dw_ref, ubuf, dbuf):
        i = pl.program_id(1)
        ubuf[0:SUBLANES, :] = jnp.where(i > 0, uprev_ref[...], 0.0)
        ubuf[SUBLANES:, :] = u_ref[...]
        dbuf[0:tm, :] = dc_ref[...]
        dbuf[tm:, :] = jnp.where(i < last, dcnext_ref[...], 0.0)
        dcv = dc_ref[...]
        du = dbuf[pl.ds(3, tm), :] * w_ref[0:1, :]
        rows = [jnp.sum(dcv * ubuf[pl.ds(SUBLANES - 3, tm), :], axis=0, keepdims=True)]
        for k in range(1, GDN_CONV):
            du = du + dbuf[pl.ds(3 - k, tm), :] * w_ref[k:k + 1, :]
            rows.append(jnp.sum(dcv * ubuf[pl.ds(SUBLANES - 3 + k, tm), :], axis=0, keepdims=True))
        du_ref[...] = du
        dwv = jnp.concatenate(rows, axis=0)

        @pl.when(i == 0)
        def _():
            dw_ref[...] = dwv

        @pl.when(i > 0)
        def _():
            dw_ref[...] += dwv

    return pl.pallas_call(
        body, name=name,
        out_shape=(jax.ShapeDtypeStruct(u.shape, F32), jax.ShapeDtypeStruct((GDN_CONV, chans), F32)),
        grid=(chans // tc, length // tm),
        in_specs=[pl.BlockSpec((tm, tc), lambda j, i: (i, j)),
                  pl.BlockSpec((SUBLANES, tc), lambda j, i: (jnp.maximum(i * hb - 1, 0), j)),
                  pl.BlockSpec((tm, tc), lambda j, i: (i, j)),
                  pl.BlockSpec((SUBLANES, tc), lambda j, i: (jnp.minimum((i + 1) * hb, (last + 1) * hb - 1), j)),
                  pl.BlockSpec((GDN_CONV, tc), lambda j, i: (0, j))],
        out_specs=(pl.BlockSpec((tm, tc), lambda j, i: (i, j)),
                   pl.BlockSpec((GDN_CONV, tc), lambda j, i: (0, j))),
        scratch_shapes=[pltpu.VMEM((tm + SUBLANES, tc), F32), pltpu.VMEM((tm + SUBLANES, tc), F32)],
        compiler_params=_params(("parallel", "arbitrary")),
    )(u, u, dc, dc, w)


def _dot(a, b, dims, precision=None):
    if precision is None:
        a, b = a.astype(BF16), b.astype(BF16)
    return lax.dot_general(a, b, (dims, ((), ())), preferred_element_type=F32, precision=precision)


NN = ((1,), (0,))
NT = ((1,), (1,))
TN = ((0,), (0,))
HI = lax.Precision.HIGHEST


@jax.custom_vjp
def _unit_lower_inverse(a):
    c = a.shape[0]
    eye = (lax.broadcasted_iota(jnp.int32, (c, c), 0) == lax.broadcasted_iota(jnp.int32, (c, c), 1)).astype(F32)
    n = -a
    t = eye + n
    p = n
    steps = int(math.log2(c)) - 1
    for _ in range(steps):
        p = _dot(p, p, NN, HI)
        t = t + _dot(t, p, NN, HI)
    return t


def _uli_fwd(a):
    t = _unit_lower_inverse(a)
    return t, t


def _uli_bwd(t, dt):
    return (-_dot(_dot(t, dt, TN, HI), t, NT, HI),)


_unit_lower_inverse.defvjp(_uli_fwd, _uli_bwd)


def _gdn_chunk(q, k, v, bl, al, a_log, dt_bias, state):
    c = q.shape[0]
    row = lax.broadcasted_iota(jnp.int32, (c, c), 0)
    col = lax.broadcasted_iota(jnp.int32, (c, c), 1)
    causal = row >= col
    strict = row > col
    eye = (row == col).astype(F32)
    beta = jax.nn.sigmoid(bl)
    g = -jnp.exp(a_log) * jax.nn.softplus(al + dt_bias)
    g_r = jnp.sum(eye * g, axis=0, keepdims=True)
    gc = jnp.sum(jnp.where(causal, g_r, 0.0), axis=1, keepdims=True)
    gc_r = jnp.sum(jnp.where(row <= col, g, 0.0), axis=0, keepdims=True)
    decay = jnp.where(causal, jnp.exp(jnp.where(causal, gc - gc_r, 0.0)), 0.0)
    kb = k * beta
    vb = v * beta
    a_mat = jnp.where(strict, _dot(kb, k, NT) * decay, 0.0)
    t_inv = _unit_lower_inverse(a_mat)
    u_blk = _dot(t_inv, vb, NN)
    w_blk = _dot(t_inv, kb * jnp.exp(gc), NN)
    v_new = u_blk - _dot(w_blk, state, NN)
    attn = _dot(q, k, NT) * decay
    o = _dot(q * jnp.exp(gc), state, NN) + _dot(attn, v_new, NN)
    g_last = jnp.sum(g, axis=0, keepdims=True)
    k_dec = k * jnp.exp(g_last - gc)
    new_state = state * jnp.exp(g_last) + _dot(k_dec, v_new, TN)
    return o, new_state


def _lane_pick(slab, lane):
    sel = lax.broadcasted_iota(jnp.int32, slab.shape, 1) == lane
    return jnp.sum(jnp.where(sel, slab, 0.0), axis=1, keepdims=True)


def _row_pick(slab, row):
    sel = lax.broadcasted_iota(jnp.int32, slab.shape, 0) == row
    return jnp.sum(jnp.where(sel, slab, 0.0), axis=0, keepdims=True)[:, 0:1]


def _gdn_scan_fwd(q, k, v, ba, a_log, dt_bias, *, name):
    length = q.shape[0]
    n = length // GDN_CHUNK
    c = GDN_CHUNK

    def body(q_ref, k_ref, v_ref, ba_ref, alog_ref, dtb_ref, o_ref, s_ref, state):
        i, h = pl.program_id(0), pl.program_id(1)

        @pl.when(i == 0)
        def _():
            state[h] = jnp.zeros((HEAD_DIM, HEAD_DIM), F32)

        s_in = state[h]
        s_ref[...] = s_in
        bav = ba_ref[...]
        o, s_out = _gdn_chunk(q_ref[...], k_ref[...], v_ref[...],
                              _lane_pick(bav, h), _lane_pick(bav, h + GDN_HEADS),
                              _row_pick(alog_ref[...], h), _row_pick(dtb_ref[...], h), s_in)
        o_ref[...] = o
        state[h] = s_out

    head_spec = pl.BlockSpec((c, HEAD_DIM), lambda i, h: (i, h))
    return pl.pallas_call(
        body, name=name,
        out_shape=(jax.ShapeDtypeStruct((length, D_MODEL), F32),
                   jax.ShapeDtypeStruct((n, GDN_HEADS, HEAD_DIM, HEAD_DIM), F32)),
        grid=(n, GDN_HEADS),
        in_specs=[head_spec, head_spec, head_spec,
                  pl.BlockSpec((c, LANES), lambda i, h: (i, 0)),
                  pl.BlockSpec((GDN_HEADS, LANES), lambda i, h: (0, 0)),
                  pl.BlockSpec((GDN_HEADS, LANES), lambda i, h: (0, 0))],
        out_specs=(head_spec, pl.BlockSpec((None, None, HEAD_DIM, HEAD_DIM), lambda i, h: (i, h, 0, 0))),
        scratch_shapes=[pltpu.VMEM((GDN_HEADS, HEAD_DIM, HEAD_DIM), F32)],
        compiler_params=_params(("arbitrary", "arbitrary")),
    )(q, k, v, ba, a_log, dt_bias)


def _gdn_scan_bwd(q, k, v, ba, a_log, dt_bias, states, do, *, name):
    length = q.shape[0]
    n = length // GDN_CHUNK
    c = GDN_CHUNK

    def body(q_ref, k_ref, v_ref, ba_ref, alog_ref, dtb_ref, s_ref, do_ref,
             dq_ref, dk_ref, dv_ref, dba_ref, dalog_ref, ddtb_ref, dstate):
        i, h = pl.program_id(0), pl.program_id(1)

        @pl.when(i == 0)
        def _():
            dstate[h] = jnp.zeros((HEAD_DIM, HEAD_DIM), F32)

        @pl.when((i == 0) & (h == 0))
        def _():
            dalog_ref[...] = jnp.zeros_like(dalog_ref)
            ddtb_ref[...] = jnp.zeros_like(ddtb_ref)

        bav = ba_ref[...]
        prim = (q_ref[...], k_ref[...], v_ref[...], _lane_pick(bav, h), _lane_pick(bav, h + GDN_HEADS),
                _row_pick(alog_ref[...], h), _row_pick(dtb_ref[...], h), s_ref[...])
        _, vjp = jax.vjp(_gdn_chunk, *prim)
        dq, dk, dv, dbl, dal, dalog, ddtb, ds = vjp((do_ref[...], dstate[h]))
        dq_ref[...] = dq
        dk_ref[...] = dk
        dv_ref[...] = dv
        dstate[h] = ds
        lane = lax.broadcasted_iota(jnp.int32, (c, LANES), 1)
        slab = jnp.where(lane == h, dbl, 0.0) + jnp.where(lane == h + GDN_HEADS, dal, 0.0)

        @pl.when(h == 0)
        def _():
            dba_ref[...] = slab

        @pl.when(h > 0)
        def _():
            dba_ref[...] += slab

        here = ((lax.broadcasted_iota(jnp.int32, (GDN_HEADS, LANES), 0) == h)
                & (lax.broadcasted_iota(jnp.int32, (GDN_HEADS, LANES), 1) == 0))
        dalog_ref[...] += jnp.where(here, dalog, 0.0)
        ddtb_ref[...] += jnp.where(here, ddtb, 0.0)

    head_spec = pl.BlockSpec((c, HEAD_DIM), lambda i, h: (n - 1 - i, h))
    small = pl.BlockSpec((GDN_HEADS, LANES), lambda i, h: (0, 0))
    return pl.pallas_call(
        body, name=name,
        out_shape=(jax.ShapeDtypeStruct((length, D_MODEL), F32),) * 3
        + (jax.ShapeDtypeStruct((length, LANES), F32),
           jax.ShapeDtypeStruct((GDN_HEADS, LANES), F32), jax.ShapeDtypeStruct((GDN_HEADS, LANES), F32)),
        grid=(n, GDN_HEADS),
        in_specs=[head_spec, head_spec, head_spec,
                  pl.BlockSpec((c, LANES), lambda i, h: (n - 1 - i, 0)), small, small,
                  pl.BlockSpec((None, None, HEAD_DIM, HEAD_DIM), lambda i, h: (n - 1 - i, h, 0, 0)),
                  head_spec],
        out_specs=(head_spec, head_spec, head_spec,
                   pl.BlockSpec((c, LANES), lambda i, h: (n - 1 - i, 0)), small, small),
        scratch_shapes=[pltpu.VMEM((GDN_HEADS, HEAD_DIM, HEAD_DIM), F32)],
        compiler_params=_params(("arbitrary", "arbitrary")),
    )(q, k, v, ba, a_log, dt_bias, states, do)


S5_W = S5_T * LANES
S5_S = 2 * 8 * S5_STATE
S5_SH = S5_S // 2


def _s5_assemble(kb_ref, m_scr):
    zero = jnp.zeros((LANES, LANES), m_scr.dtype)
    for s in range(S5_T):
        for t in range(S5_T):
            m_scr[s * LANES:(s + 1) * LANES, t * LANES:(t + 1) * LANES] = kb_ref[t - s] if t >= s else zero


def _s5_scan_fwd(ur, kb, e, f, at, *, name):
    n = ur.shape[1]
    assert n % SUBLANES == 0

    def body(u_ref, kb_ref, e_ref, f_ref, at_ref, y_ref, h_ref, m_scr, g_scr):
        _s5_assemble(kb_ref, m_scr)
        u = u_ref[...]
        g_scr[...] = jnp.dot(u, f_ref[...], preferred_element_type=F32)
        ar, ai = at_ref[:, :S5_SH], at_ref[:, S5_SH:]

        def step(blk, h):
            base = pl.multiple_of(blk * SUBLANES, SUBLANES)
            g8 = g_scr[pl.ds(base, SUBLANES), :]
            rows = []
            for r in range(SUBLANES):
                rows.append(h)
                hr, hi = h[:, :S5_SH], h[:, S5_SH:]
                h = jnp.concatenate([ar * hr - ai * hi, ar * hi + ai * hr], axis=1) + g8[r:r + 1, :]
            h_ref[pl.ds(base, SUBLANES), :] = jnp.concatenate(rows, axis=0)
            return h

        lax.fori_loop(0, n // SUBLANES, step, jnp.zeros((1, S5_S), F32))
        y_ref[...] = (jnp.dot(u, m_scr[...], preferred_element_type=F32)
                      + jnp.dot(h_ref[...].astype(BF16), e_ref[...], preferred_element_type=F32))

    def spec(*tail):
        return pl.BlockSpec((None,) + tail, lambda k: (k,) + (0,) * len(tail))

    return pl.pallas_call(
        body, name=name,
        out_shape=(jax.ShapeDtypeStruct((S5_TILES, n, S5_W), F32), jax.ShapeDtypeStruct((S5_TILES, n, S5_S), F32)),
        grid=(S5_TILES,),
        in_specs=[spec(n, S5_W), spec(S5_T, LANES, LANES), spec(S5_S, S5_W), spec(S5_W, S5_S), spec(1, S5_S)],
        out_specs=(spec(n, S5_W), spec(n, S5_S)),
        scratch_shapes=[pltpu.VMEM((S5_W, S5_W), BF16), pltpu.VMEM((n, S5_S), F32)],
        compiler_params=_params(("parallel",)),
    )(ur, kb, e, f, at)


def _s5_scan_bwd(dyr, kb, e, f, at, hs, *, name):
    n = dyr.shape[1]

    def body(dy_ref, kb_ref, e_ref, f_ref, at_ref, h_ref, du_ref, dg_ref, dat_ref, m_scr, dh_scr):
        _s5_assemble(kb_ref, m_scr)
        dy = dy_ref[...]
        dh_scr[...] = _dot(dy, e_ref[...], NT)
        ar, ai = at_ref[:, :S5_SH], at_ref[:, S5_SH:]

        def step(it, carry):
            cy, dat = carry
            base = pl.multiple_of((n // SUBLANES - 1 - it) * SUBLANES, SUBLANES)
            dh8 = dh_scr[pl.ds(base, SUBLANES), :]
            h8 = h_ref[pl.ds(base, SUBLANES), :]
            rows = [None] * SUBLANES
            for r in reversed(range(SUBLANES)):
                rows[r] = cy
                cr, ci = cy[:, :S5_SH], cy[:, S5_SH:]
                hr, hi = h8[r:r + 1, :S5_SH], h8[r:r + 1, S5_SH:]
                dat = dat + jnp.concatenate([cr * hr + ci * hi, ci * hr - cr * hi], axis=1)
                cy = dh8[r:r + 1, :] + jnp.concatenate([ar * cr + ai * ci, ar * ci - ai * cr], axis=1)
            dg_ref[pl.ds(base, SUBLANES), :] = jnp.concatenate(rows, axis=0)
            return cy, dat

        zero = jnp.zeros((1, S5_S), F32)
        _, dat = lax.fori_loop(0, n // SUBLANES, step, (zero, zero))
        dat_ref[...] = dat
        du_ref[...] = _dot(dy, m_scr[...], NT) + _dot(dg_ref[...], f_ref[...], NT)

    def spec(*tail):
        return pl.BlockSpec((None,) + tail, lambda k: (k,) + (0,) * len(tail))

    return pl.pallas_call(
        body, name=name,
        out_shape=(jax.ShapeDtypeStruct((S5_TILES, n, S5_W), F32), jax.ShapeDtypeStruct((S5_TILES, n, S5_S), F32),
                   jax.ShapeDtypeStruct((S5_TILES, 1, S5_S), F32)),
        grid=(S5_TILES,),
        in_specs=[spec(n, S5_W), spec(S5_T, LANES, LANES), spec(S5_S, S5_W), spec(S5_W, S5_S), spec(1, S5_S),
                  spec(n, S5_S)],
        out_specs=(spec(n, S5_W), spec(n, S5_S), spec(1, S5_S)),
        scratch_shapes=[pltpu.VMEM((S5_W, S5_W), BF16), pltpu.VMEM((n, S5_S), F32)],
        compiler_params=_params(("parallel",)),
    )(dyr, kb, e, f, at, hs)


def _s5_prep(a_re, a_im, b_re, b_im, c_re, c_im, log_dt):
    t_len, tiles = S5_T, S5_TILES
    dt = jnp.exp(log_dt)[:, None]
    mag = jnp.exp(a_re * dt)
    ab_re, ab_im = mag * jnp.cos(a_im * dt), mag * jnp.sin(a_im * dt)
    den = jnp.square(a_re) + jnp.square(a_im)
    n_re, n_im = ab_re - 1.0, ab_im
    f_re = (n_re * a_re + n_im * a_im) / den
    f_im = (n_im * a_re - n_re * a_im) / den
    bb_re = f_re[..., None] * b_re - f_im[..., None] * b_im
    bb_im = f_re[..., None] * b_im + f_im[..., None] * b_re
    p_re, p_im = [jnp.ones_like(ab_re)], [jnp.zeros_like(ab_re)]
    for _ in range(t_len):
        p_re, p_im = (p_re + [p_re[-1] * ab_re - p_im[-1] * ab_im],
                      p_im + [p_re[-1] * ab_im + p_im[-1] * ab_re])
    rev_re, rev_im = jnp.stack(p_re[t_len - 1::-1]), jnp.stack(p_im[t_len - 1::-1])
    p_re, p_im = jnp.stack(p_re), jnp.stack(p_im)
    ca_re = c_re[None] * p_re[:, :, None, :] - c_im[None] * p_im[:, :, None, :]
    ca_im = c_re[None] * p_im[:, :, None, :] + c_im[None] * p_re[:, :, None, :]
    lag = (jnp.einsum('tgip,gpj->tgij', ca_re[:t_len], bb_re, precision=HI)
           - jnp.einsum('tgip,gpj->tgij', ca_im[:t_len], bb_im, precision=HI))
    eye = jnp.eye(8, dtype=F32)
    lag = lag.reshape(t_len, tiles, 8, S5_GROUP, S5_GROUP).transpose(1, 0, 2, 4, 3)
    kb = (lag[:, :, :, :, None, :] * eye[None, None, :, None, :, None]).reshape(tiles, t_len, LANES, LANES)
    e_st = jnp.stack([ca_re[1:], -ca_im[1:]])
    e_st = e_st.reshape(2, t_len, tiles, 8, S5_GROUP, S5_STATE).transpose(2, 0, 3, 5, 1, 4)
    e_op = (e_st[:, :, :, :, :, None, :] * eye[None, None, :, None, None, :, None]).reshape(tiles, S5_S, S5_W)
    ab_b = jnp.stack([rev_re[..., None] * bb_re[None] - rev_im[..., None] * bb_im[None],
                      rev_re[..., None] * bb_im[None] + rev_im[..., None] * bb_re[None]])
    ab_b = ab_b.reshape(2, t_len, tiles, 8, S5_STATE, S5_GROUP).transpose(2, 1, 3, 5, 0, 4)
    f_op = (ab_b[:, :, :, :, :, None, :] * eye[None, None, :, None, None, :, None]).reshape(tiles, S5_W, S5_S)
    a_t = jnp.stack([p_re[t_len], p_im[t_len]]).reshape(2, tiles, 8 * S5_STATE).transpose(1, 0, 2)
    return kb, e_op, f_op, a_t.reshape(tiles, 1, S5_S)


def _s5_to_chunks(u):
    n = u.shape[0] // S5_T
    return u.reshape(n, S5_T, S5_TILES, LANES).transpose(2, 0, 1, 3).reshape(S5_TILES, n, S5_W)


def _s5_from_chunks(yr):
    n = yr.shape[1]
    return yr.reshape(S5_TILES, n, S5_T, LANES).transpose(1, 2, 0, 3).reshape(n * S5_T, D_MODEL)


def _s5_dkb(dm):
    dmr = dm.reshape(S5_TILES, S5_T, LANES, S5_T, LANES)
    lags = []
    for lag in range(S5_T):
        acc = dmr[:, 0, :, lag, :]
        for s in range(1, S5_T - lag):
            acc = acc + dmr[:, s, :, s + lag, :]
        lags.append(acc)
    return jnp.stack(lags, axis=1)


TM_ROW = 256


def _gdn_fwd(x, w, tag):
    qkv = _mm(x, w["wqkv"], name="gdn_proj_qkv")
    z = _mm(x, w["wz"], name="gdn_proj_z")
    ba = _mm(x, w["wba"], name="gdn_proj_ba")
    cv = _conv_fwd(qkv, w["conv_w"], tm=TM_ROW, name="gdn_conv")
    q, k, v = _rw_fwd(_f_gdn_qkv, [cv], [], tm=TM_ROW, name="gdn_qkv")
    o, states = _gdn_scan_fwd(q, k, v, ba, w["a_log8"], w["dt_bias8"], name="gdn_scan")
    (mix,) = _rw_fwd(_f_gdn_out, [o, z], [w["norm_g"]], tm=TM_ROW, name="gdn_out")
    return mix, (qkv, z, ba, cv, q, k, v, states, o)


def _gdn_bwd(x, w, saved, dmix, dx_acc):
    qkv, z, ba, cv, q, k, v, states, o = saved
    (do, dz), (dnorm_g,) = _rw_bwd(_f_gdn_out, [o, z], [w["norm_g"]], [dmix], row_grad=[1, 1], param_grad=[1],
                                   tm=TM_ROW, name="gdn_out_bwd")
    dq, dk, dv, dba, dalog, ddtb = _gdn_scan_bwd(q, k, v, ba, w["a_log8"], w["dt_bias8"], states, do,
                                                  name="gdn_scan_bwd")
    (dcv,), _ = _rw_bwd(_f_gdn_qkv, [cv], [], [dq, dk, dv], row_grad=[1], param_grad=[], tm=TM_ROW,
                        name="gdn_qkv_bwd")
    dqkv, dconv_w = _conv_bwd(qkv, w["conv_w"], dcv, tm=TM_ROW, name="gdn_conv_bwd")
    dx = _mm(dqkv, w["wqkv"], tb=True, acc=dx_acc, name="gdn_dx_qkv")
    dx = _mm(dz, w["wz"], tb=True, acc=dx, name="gdn_dx_z")
    dx = _mm(dba, w["wba"], tb=True, acc=dx, name="gdn_dx_ba")
    grads = dict(wqkv=_mm(x, dqkv, ta=True, name="gdn_dw_qkv"), wz=_mm(x, dz, ta=True, name="gdn_dw_z"),
                 wba=_mm(x, dba, ta=True, name="gdn_dw_ba"), conv_w=dconv_w,
                 a_log=dalog[:, 0], dt_bias=ddtb[:, 0], norm_g=dnorm_g[0])
    return dx, grads


def _s5_fwd(x, w, tag):
    u = _mm(x, w["wu"], name="s5_proj_u")
    ur = _s5_to_chunks(u).astype(BF16)
    yr, hs = _s5_scan_fwd(ur, w["kb"], w["e_op"], w["f_op"], w["a_t"], name="s5_scan")
    y = _s5_from_chunks(yr)
    (zg,) = _rw_fwd(_f_s5_gelu, [y, u], [w["d"]], tm=TM_ROW, name="s5_gelu")
    t = _mm(zg, w["w_glu"], name="s5_glu")
    (mix,) = _rw_fwd(_f_s5_gate, [zg, t], [w["b_glu"]], tm=TM_ROW, name="s5_gate")
    return mix, (u, ur, hs, y, zg, t)


def _s5_bwd(x, w, saved, dmix, dx_acc):
    u, ur, hs, y, zg, t = saved
    (dzg, dt), (db_glu,) = _rw_bwd(_f_s5_gate, [zg, t], [w["b_glu"]], [dmix], row_grad=[1, 1], param_grad=[1],
                                   tm=TM_ROW, name="s5_gate_bwd")
    dzg = _mm(dt, w["w_glu"], tb=True, acc=dzg, name="s5_dzg")
    dw_glu = _mm(zg, dt, ta=True, name="s5_dw_glu")
    (dy, du), (dd,) = _rw_bwd(_f_s5_gelu, [y, u], [w["d"]], [dzg], row_grad=[1, 1], param_grad=[1],
                              tm=TM_ROW, name="s5_gelu_bwd")
    dyr = _s5_to_chunks(dy).astype(BF16)
    dur, dg, dat = _s5_scan_bwd(dyr, w["kb"], w["e_op"], w["f_op"], w["a_t"], hs, name="s5_scan_bwd")
    (du,) = _rw_fwd(_f_add, [du, _s5_from_chunks(dur)], [], tm=TM_ROW, name="s5_du_add")
    dm = _mm(ur, dyr, ta=True, name="s5_dm", tm=1024, tn=1024)
    de = _mm(hs, dyr, ta=True, name="s5_de", tm=1024, tn=1024)
    df = _mm(ur, dg, ta=True, name="s5_df", tm=1024, tn=1024)
    d_a_re, d_a_im, d_b_re, d_b_im, d_c_re, d_c_im, d_log_dt = w["prep_vjp"]((_s5_dkb(dm), de, df, dat))
    dx = _mm(du, w["wu"], tb=True, acc=dx_acc, name="s5_dx_u")
    grads = dict(wu=_mm(x, du, ta=True, name="s5_dw_u"), w_glu=dw_glu, b_glu=db_glu[0], d=dd[0],
                 a_re=d_a_re, a_im=d_a_im, b_re=d_b_re, b_im=d_b_im, c_re=d_c_re, c_im=d_c_im, log_dt=d_log_dt)
    return dx, grads


def _layer_fwd(x, mem, w, is_gdn):
    mix, msave = (_gdn_fwd if is_gdn else _s5_fwd)(x, w, "")
    xq = _mm(x, w["wxq"], name="proj_xq")
    kv = _mm(mem, w["wkv"], name="mem_kv")
    kmem, vmem = kv[:, :XA_DIM], kv[:, XA_DIM:]
    (cross,) = _rw_fwd(_f_attn, [xq], [kmem, vmem], tm=TM_ROW, name="attn")
    h = _mm(mix, w["wo_mix"], name="wo_mix")
    h = _mm(cross, w["wo_cross"], acc=h, name="wo_cross")
    (x1,) = _rw_fwd(_f_ln_res, [x, h], [w["ln1_g"], w["ln1_b"]], tm=TM_ROW, name="ln_res")
    hm = _mm(x1, w["w1"], name="mlp_up")
    (act,) = _rw_fwd(_f_relu2, [hm], [], tm=TM_ROW, name="relu2")
    f = _mm(act, w["w2"], name="mlp_down")
    (x2,) = _rw_fwd(_f_ln_res, [x1, f], [w["ln2_g"], w["ln2_b"]], tm=TM_ROW, name="ln_res")
    return x2, (x, msave, xq, kmem, vmem, mix, cross, h, x1, hm, act, f)


def _layer_bwd(mem, w, is_gdn, saved, dx2):
    x, msave, xq, kmem, vmem, mix, cross, h, x1, hm, act, f = saved
    (dx1, df), (dg2, db2) = _rw_bwd(_f_ln_res, [x1, f], [w["ln2_g"], w["ln2_b"]], [dx2], row_grad=[1, 1],
                                    param_grad=[1, 1], tm=TM_ROW, name="ln_res_bwd")
    dact = _mm(df, w["w2"], tb=True, name="mlp_dact")
    dw2 = _mm(act, df, ta=True, name="mlp_dw2")
    (dhm,), _ = _rw_bwd(_f_relu2, [hm], [], [dact], row_grad=[1], param_grad=[], tm=TM_ROW, name="relu2_bwd")
    dx1 = _mm(dhm, w["w1"], tb=True, acc=dx1, name="mlp_dx")
    dw1 = _mm(x1, dhm, ta=True, name="mlp_dw1")
    (dx, dh), (dg1, db1) = _rw_bwd(_f_ln_res, [x, h], [w["ln1_g"], w["ln1_b"]], [dx1], row_grad=[1, 1],
                                   param_grad=[1, 1], tm=TM_ROW, name="ln_res_bwd")
    dmix = _mm(dh, w["wo_mix"], tb=True, name="wo_dmix")
    dcross = _mm(dh, w["wo_cross"], tb=True, name="wo_dcross")
    dwo = jnp.concatenate([_mm(mix, dh, ta=True, name="wo_dw_mix"), _mm(cross, dh, ta=True, name="wo_dw_cross")], 0)
    (dxq,), (dkmem, dvmem) = _rw_bwd(_f_attn, [xq], [kmem, vmem], [dcross], row_grad=[1], param_grad=[1, 1],
                                     tm=TM_ROW, name="attn_bwd")
    dwkv = _mm(mem, jnp.concatenate([dkmem, dvmem], axis=1), ta=True, name="mem_dw_kv")
    dx = _mm(dxq, w["wxq"], tb=True, acc=dx, name="dx_xq")
    dwxq = _mm(x, dxq, ta=True, name="dw_xq")
    dx, mg = (_gdn_bwd if is_gdn else _s5_bwd)(x, w, msave, dmix, dx)
    grads = dict(mixer=mg, wxq=dwxq, wkv=dwkv, wo=dwo, w1=dw1, w2=dw2,
                 ln1_g=dg1[0], ln1_b=db1[0], ln2_g=dg2[0], ln2_b=db2[0])
    return dx, grads


def _loss_and_grad(y, target):
    def f(yv, tv):
        err = yv - tv
        return (err * (1.0 / D_MODEL),), (0.5 / D_MODEL * jnp.sum(err * err, axis=0, keepdims=True),)

    (dy,), (part,) = _rowwise(f, [y, target], [], [(D_MODEL, F32)], [((1, D_MODEL), F32)], tm=512, name="loss")
    return jnp.sum(part), dy


def _layer_weights(full, i):
    j = i // 2
    w = dict(wkv=full["w_kv_mem"][i].astype(BF16),
             wo_mix=full["w_o"][i, :D_MODEL].astype(BF16), wo_cross=full["w_o"][i, D_MODEL:].astype(BF16),
             ln1_g=full["ln1_g"][i][None], ln1_b=full["ln1_b"][i][None],
             ln2_g=full["ln2_g"][i][None], ln2_b=full["ln2_b"][i][None],
             w1=full["mlp_w1"][i].astype(BF16), w2=full["mlp_w2"][i].astype(BF16))
    if i % 2 == 0:
        w_in = full["gdn_w_in"][j]
        gd = 3 * D_MODEL
        w.update(wqkv=w_in[:, :gd].astype(BF16), wz=w_in[:, gd:gd + D_MODEL].astype(BF16),
                 wba=jnp.pad(w_in[:, gd + D_MODEL:gd + D_MODEL + 2 * GDN_HEADS],
                             ((0, 0), (0, LANES - 2 * GDN_HEADS))).astype(BF16),
                 wxq=w_in[:, gd + D_MODEL + 2 * GDN_HEADS:].astype(BF16),
                 conv_w=full["gdn_conv_w"][j],
                 a_log8=jnp.broadcast_to(full["gdn_a_log"][j][:, None], (GDN_HEADS, LANES)),
                 dt_bias8=jnp.broadcast_to(full["gdn_dt_bias"][j][:, None], (GDN_HEADS, LANES)),
                 norm_g=full["gdn_norm_g"][j][None])
    else:
        w_in = full["s5_w_in"][j]
        (kb, e_op, f_op, a_t), prep_vjp = jax.vjp(
            _s5_prep, full["s5_a_re"][j], full["s5_a_im"][j], full["s5_b_re"][j], full["s5_b_im"][j],
            full["s5_c_re"][j], full["s5_c_im"][j], full["s5_log_dt"][j])
        w.update(wu=w_in[:, :D_MODEL].astype(BF16), wxq=w_in[:, D_MODEL:].astype(BF16),
                 kb=kb.astype(BF16), e_op=e_op.astype(BF16), f_op=f_op.astype(BF16), a_t=a_t, prep_vjp=prep_vjp,
                 d=full["s5_d"][j][None], w_glu=full["s5_w_glu"][j].astype(BF16), b_glu=full["s5_b_glu"][j][None])
    return w


def _full_grads(layer_grads):
    g = layer_grads
    gdn = [g[i] for i in range(DEPTH) if i % 2 == 0]
    s5 = [g[i] for i in range(DEPTH) if i % 2 == 1]
    out = dict(
        w_kv_mem=jnp.stack([l["wkv"] for l in g]), w_o=jnp.stack([l["wo"] for l in g]),
        ln1_g=jnp.stack([l["ln1_g"] for l in g]), ln1_b=jnp.stack([l["ln1_b"] for l in g]),
        ln2_g=jnp.stack([l["ln2_g"] for l in g]), ln2_b=jnp.stack([l["ln2_b"] for l in g]),
        mlp_w1=jnp.stack([l["w1"] for l in g]), mlp_w2=jnp.stack([l["w2"] for l in g]),
        gdn_w_in=jnp.stack([jnp.concatenate([l["mixer"]["wqkv"], l["mixer"]["wz"],
                                             l["mixer"]["wba"][:, :2 * GDN_HEADS], l["wxq"]], axis=1) for l in gdn]),
        gdn_conv_w=jnp.stack([l["mixer"]["conv_w"] for l in gdn]),
        gdn_a_log=jnp.stack([l["mixer"]["a_log"] for l in gdn]),
        gdn_dt_bias=jnp.stack([l["mixer"]["dt_bias"] for l in gdn]),
        gdn_norm_g=jnp.stack([l["mixer"]["norm_g"] for l in gdn]),
        s5_w_in=jnp.stack([jnp.concatenate([l["mixer"]["wu"], l["wxq"]], axis=1) for l in s5]),
        s5_d=jnp.stack([l["mixer"]["d"] for l in s5]),
        s5_w_glu=jnp.stack([l["mixer"]["w_glu"] for l in s5]),
        s5_b_glu=jnp.stack([l["mixer"]["b_glu"] for l in s5]))
    for n in ("a_re", "a_im", "b_re", "b_im", "c_re", "c_im", "log_dt"):
        out["s5_" + n] = jnp.stack([l["mixer"][n] for l in s5])
    return out


def _local_step(x, mem, target, full):
    lw = [_layer_weights(full, i) for i in range(DEPTH)]
    saves = []
    h = x
    for i in range(DEPTH):
        h, s = _layer_fwd(h, mem, lw[i], i % 2 == 0)
        saves.append(s)
    loss, d = _loss_and_grad(h, target)
    grads = [None] * DEPTH
    for i in reversed(range(DEPTH)):
        d, grads[i] = _layer_bwd(mem, lw[i], i % 2 == 0, saves[i], d)
    return loss, d, _full_grads(grads)


ANY = pl.BlockSpec(memory_space=pl.ANY)
SHARD_ROWS = 1024
SMALL_ROWS = 128


def _place():
    return lax.axis_index("x"), lax.axis_index("y"), lax.axis_index("c")


def _other_chips(x, y):
    return [(1 - x, y), (x, 1 - y), (1 - x, 1 - y)]


def _all_gather_chips(wpack, *, name):
    rows = wpack.shape[0]
    half = rows // 2

    def body(w_ref, out_ref, send_sems, recv_sems, local_sem):
        x, y, c = _place()
        sibling = (x, y, 1 - c)
        chips = _other_chips(x, y)

        def blk(cx, cy, cc):
            return out_ref.at[2 * cx + cy, pl.ds(cc * half, half), :]

        def copy(k, src, dst, to):
            return pltpu.make_async_remote_copy(src_ref=src, dst_ref=dst, send_sem=send_sems.at[k],
                                                recv_sem=recv_sems.at[k], device_id=to, device_id_type=MESH)

        mine = pltpu.make_async_copy(w_ref, out_ref.at[2 * x + y], local_sem)
        mine.start()
        first = [copy(j, w_ref.at[pl.ds(c * half, half), :], blk(x, y, c), (cx, cy, c))
                 for j, (cx, cy) in enumerate(chips)]
        for cp in first:
            cp.start()
        passed = [copy(3 + j, blk(cx, cy, c), blk(cx, cy, c), sibling) for j, (cx, cy) in enumerate(chips)]
        for j, (cx, cy) in enumerate(chips):
            copy(j, blk(cx, cy, c), blk(cx, cy, c), (cx, cy, c)).wait_recv()
            passed[j].start()
        for j, (cx, cy) in enumerate(chips):
            copy(3 + j, blk(cx, cy, 1 - c), blk(cx, cy, 1 - c), sibling).wait_recv()
        for cp in first + passed:
            cp.wait_send()
        mine.wait()

    return pl.pallas_call(
        body, name=name, out_shape=jax.ShapeDtypeStruct((N_CHIPS, rows, D_MODEL), wpack.dtype),
        in_specs=[ANY], out_specs=ANY,
        scratch_shapes=[pltpu.SemaphoreType.DMA((6,)), pltpu.SemaphoreType.DMA((6,)), pltpu.SemaphoreType.DMA],
    )(wpack)


def _sibling_swap(buf, *, name):
    def body(in_ref, out_ref, send_sem, recv_sem):
        x, y, c = _place()
        cp = pltpu.make_async_remote_copy(src_ref=in_ref, dst_ref=out_ref, send_sem=send_sem, recv_sem=recv_sem,
                                          device_id=(x, y, 1 - c), device_id_type=MESH)
        cp.start()
        cp.wait()

    return pl.pallas_call(
        body, name=name, out_shape=jax.ShapeDtypeStruct(buf.shape, buf.dtype), in_specs=[ANY], out_specs=ANY,
        scratch_shapes=[pltpu.SemaphoreType.DMA, pltpu.SemaphoreType.DMA],
    )(buf)


def _chip_exchange(pieces, *, name):
    _, rows, width = pieces.shape

    def body(in_ref, out_ref, send_sems, recv_sems):
        x, y, c = _place()
        cps = [pltpu.make_async_remote_copy(src_ref=in_ref.at[2 * cx + cy], dst_ref=out_ref.at[j],
                                            send_sem=send_sems.at[j], recv_sem=recv_sems.at[j],
                                            device_id=(cx, cy, c), device_id_type=MESH)
               for j, (cx, cy) in enumerate(_other_chips(x, y))]
        for cp in cps:
            cp.start()
        for cp in cps:
            cp.wait()

    return pl.pallas_call(
        body, name=name, out_shape=jax.ShapeDtypeStruct((3, rows, width), pieces.dtype), in_specs=[ANY], out_specs=ANY,
        scratch_shapes=[pltpu.SemaphoreType.DMA((3,)), pltpu.SemaphoreType.DMA((3,))],
    )(pieces)


def _all_reduce_small(v, *, name):
    rows, width = v.shape

    def body(in_ref, out_ref, gath, send_sems, recv_sems):
        x, y, c = _place()
        me = 4 * x + 2 * y + c
        gath[me] = in_ref[...]
        peers = []
        for m in range(1, N_DEV):
            px = 1 - x if m & 4 else x
            py = 1 - y if m & 2 else y
            pc = 1 - c if m & 1 else c
            peers.append((m - 1, (px, py, pc), 4 * px + 2 * py + pc))
        for k, peer, _ in peers:
            pltpu.make_async_remote_copy(src_ref=in_ref, dst_ref=gath.at[me], send_sem=send_sems.at[k],
                                         recv_sem=recv_sems.at[k], device_id=peer, device_id_type=MESH).start()
        for k, peer, plin in peers:
            cp = pltpu.make_async_remote_copy(src_ref=in_ref, dst_ref=gath.at[plin], send_sem=send_sems.at[k],
                                              recv_sem=recv_sems.at[k], device_id=peer, device_id_type=MESH)
            cp.wait_send()
            cp.wait_recv()
        acc = gath[0]
        for d in range(1, N_DEV):
            acc = acc + gath[d]
        out_ref[...] = acc

    vmem = pl.BlockSpec(memory_space=pltpu.VMEM)
    return pl.pallas_call(
        body, name=name, out_shape=jax.ShapeDtypeStruct(v.shape, v.dtype), in_specs=[vmem], out_specs=vmem,
        scratch_shapes=[pltpu.VMEM((N_DEV, rows, width), v.dtype),
                        pltpu.SemaphoreType.DMA((N_DEV - 1,)), pltpu.SemaphoreType.DMA((N_DEV - 1,))],
        compiler_params=pltpu.CompilerParams(vmem_limit_bytes=VMEM_LIMIT_V7X),
    )(v)


def _reduce_scatter(gpack):
    x, y, c = _place()
    half = gpack.shape[1] // 2
    own = lax.dynamic_slice_in_dim(gpack, c * half, half, axis=1).reshape(N_CHIPS * half, D_MODEL)
    other = lax.dynamic_slice_in_dim(gpack, (1 - c) * half, half, axis=1).reshape(N_CHIPS * half, D_MODEL)
    got = _sibling_swap(other, name="rs_pair_swap")
    (pair,) = _rw_fwd(_f_add, [own, got], [], tm=512, name="rs_pair_add")
    pair = pair.reshape(N_CHIPS, half, D_MODEL)
    recv = _chip_exchange(pair, name="rs_chip_exchange")
    mine = lax.dynamic_index_in_dim(pair, 2 * x + y, axis=0, keepdims=False)
    (total,) = _rw_fwd(_f_add4, [mine, recv[0], recv[1], recv[2]], [], tm=512, name="rs_chip_add")
    theirs = _sibling_swap(total, name="rs_share_swap")
    return jnp.concatenate([jnp.where(c == 0, total, theirs), jnp.where(c == 0, theirs, total)], axis=0)


_SHARDED = (("w_kv_mem", 1), ("w_o", 1), ("mlp_w1", 2), ("mlp_w2", 1), ("gdn_w_in", 2), ("gdn_conv_w", 2),
            ("s5_w_in", 2), ("s5_d", 1), ("s5_w_glu", 1), ("s5_b_glu", 1))
_REPLICATED = ("ln1_g", "ln1_b", "ln2_g", "ln2_b", "gdn_a_log", "gdn_dt_bias", "gdn_norm_g", "s5_a_re", "s5_a_im",
               "s5_b_re", "s5_b_im", "s5_c_re", "s5_c_im", "s5_log_dt")
_WEIGHTS = ("w_kv_mem", "w_o", "ln1_g", "ln1_b", "ln2_g", "ln2_b", "mlp_w1", "mlp_w2", "gdn_w_in", "gdn_conv_w",
            "gdn_a_log", "gdn_dt_bias", "gdn_norm_g", "s5_w_in", "s5_a_re", "s5_a_im", "s5_b_re", "s5_b_im",
            "s5_c_re", "s5_c_im", "s5_log_dt", "s5_d", "s5_w_glu", "s5_b_glu")


def _pack(arrs, lead=(), unit_rows=SHARD_ROWS):
    nl = len(lead)
    flat = jnp.concatenate([a.reshape(lead + (-1,)) for a in arrs], axis=nl)
    unit = unit_rows * D_MODEL
    pad = -flat.shape[nl] % unit
    flat = jnp.pad(flat, ((0, 0),) * nl + ((0, pad),))
    return flat.reshape(lead + (-1, D_MODEL))


def _unpack(packed, shapes, lead=()):
    nl = len(lead)
    flat = packed.reshape(lead + (-1,))
    out, off = [], 0
    for s in shapes:
        n = math.prod(s)
        out.append(lax.slice_in_dim(flat, off, off + n, axis=nl).reshape(lead + tuple(s)))
        off += n
    return out


def _merge_chips(blocks, axis):
    t = jnp.moveaxis(blocks, 0, axis)
    return t.reshape(t.shape[:axis] + (t.shape[axis] * t.shape[axis + 1],) + t.shape[axis + 2:])


def _split_chips(full, axis):
    s = full.shape
    t = full.reshape(s[:axis] + (N_CHIPS, s[axis] // N_CHIPS) + s[axis + 1:])
    return jnp.moveaxis(t, axis, 0)


def kernel(x, mem, w_kv_mem, w_o, ln1_g, ln1_b, ln2_g, ln2_b, mlp_w1, mlp_w2, gdn_w_in, gdn_conv_w, gdn_a_log, gdn_dt_bias, gdn_norm_g, s5_w_in, s5_a_re, s5_a_im, s5_b_re, s5_b_im, s5_c_re, s5_c_im, s5_log_dt, s5_d, s5_w_glu, s5_b_glu, loss_target, m_w_kv_mem, m_w_o, m_ln1_g, m_ln1_b, m_ln2_g, m_ln2_b, m_mlp_w1, m_mlp_w2, m_gdn_w_in, m_gdn_conv_w, m_gdn_a_log, m_gdn_dt_bias, m_gdn_norm_g, m_s5_w_in, m_s5_a_re, m_s5_a_im, m_s5_b_re, m_s5_b_im, m_s5_c_re, m_s5_c_im, m_s5_log_dt, m_s5_d, m_s5_w_glu, m_s5_b_glu, v_w_kv_mem, v_w_o, v_ln1_g, v_ln1_b, v_ln2_g, v_ln2_b, v_mlp_w1, v_mlp_w2, v_gdn_w_in, v_gdn_conv_w, v_gdn_a_log, v_gdn_dt_bias, v_gdn_norm_g, v_s5_w_in, v_s5_a_re, v_s5_a_im, v_s5_b_re, v_s5_b_im, v_s5_c_re, v_s5_c_im, v_s5_log_dt, v_s5_d, v_s5_w_glu, v_s5_b_glu):
    given = dict(locals())
    w = {n: given[n] for n in _WEIGHTS}
    mom = {n: given["m_" + n] for n in _WEIGHTS}
    var = {n: given["v_" + n] for n in _WEIGHTS}
    shard_names = [n for n, _ in _SHARDED]
    shard_shapes = [w[n].shape for n in shard_names]
    rep_shapes = [w[n].shape for n in _REPLICATED]

    wpack = _pack([w[n] for n in shard_names])
    gathered = _all_gather_chips(wpack, name="gather_weights")
    full = {n: _merge_chips(blk, ax) for (n, ax), blk in
            zip(_SHARDED, _unpack(gathered, shard_shapes, lead=(N_CHIPS,)))}
    full.update({n: w[n] for n in _REPLICATED})

    loss, grad_x, grads = _local_step(x[0], mem[0], loss_target[0], full)
    loss = lax.psum(loss, ("x", "y", "c"))

    gpack = _pack([_split_chips(grads[n], ax) for n, ax in _SHARDED], lead=(N_CHIPS,))
    g_shard = _reduce_scatter(gpack)
    def pack_small(d):
        return _pack([d[n] for n in _REPLICATED], unit_rows=SMALL_ROWS)

    g_rep = _all_reduce_small(pack_small(grads), name="reduce_replicated")

    def adamw(wp, gp, mp, vp, name):
        return _rw_fwd(_f_adamw, [wp, gp, mp, vp], [], tm=512, name=name)

    d_s, m_s, v_s = adamw(wpack, g_shard, _pack([mom[n] for n in shard_names]), _pack([var[n] for n in shard_names]),
                          "adamw_shards")
    d_r, m_r, v_r = adamw(pack_small(w), g_rep, pack_small(mom), pack_small(var), "adamw_replicated")
    outs = {}
    for kind, ps, pr in (("grad", g_shard, g_rep), ("delta", d_s, d_r), ("new_m", m_s, m_r), ("new_v", v_s, v_r)):
        outs.update({(kind, n): a for n, a in zip(shard_names, _unpack(ps, shard_shapes))})
        outs.update({(kind, n): a for n, a in zip(_REPLICATED, _unpack(pr, rep_shapes))})
    return (loss, grad_x[None]) + tuple(outs[(kind, n)] for kind in ("grad", "delta", "new_m", "new_v")
                                        for n in _WEIGHTS)
```

```python
import functools
import math

import jax
import jax.numpy as jnp
from jax import lax
from jax.experimental import pallas as pl
from jax.experimental.pallas import tpu as pltpu

F32 = jnp.float32
BF16 = jnp.bfloat16
MESH = pl.DeviceIdType.MESH

D_MODEL = 1024
DEPTH = 4
GDN_HEADS = 8
HEAD_DIM = 128
GDN_CONV = 4
GDN_CHUNK = 64
S5_GROUPS = 64
S5_GROUP = 16
S5_STATE = 64
XA_HEADS = 4
XA_DIM = 512
D_FF = 4096
DN_ALPHA = (2 * DEPTH) ** 0.25
LN_EPS = 1e-5
RMS_EPS = 1e-6
ADAM_LR, ADAM_B1, ADAM_B2, ADAM_EPS, ADAM_WD, ADAM_STEP = 0.001, 0.9, 0.999, 1e-08, 0.01, 10

VMEM_LIMIT_V7X = 56 * 1024 * 1024
LANES = 128
SUBLANES = 8
S5_T = 16
S5_TILES = D_MODEL // LANES
N_CHIPS = 4
N_DEV = 8


def _params(sem):
    return pltpu.CompilerParams(dimension_semantics=sem, vmem_limit_bytes=VMEM_LIMIT_V7X)


def _tile(n, pref):
    if n <= pref:
        return n
    t = (pref // LANES) * LANES
    while n % t:
        t -= LANES
    return t


def _row_tile(n, pref):
    t = min(pref, n) // SUBLANES * SUBLANES
    while n % t:
        t -= SUBLANES
    return t


def _mm(a, b, *, ta=False, tb=False, acc=None, name, tm=1024, tn=1024, tk=1024):
    batched = a.ndim == 3
    if not batched:
        a, b = a[None], b[None]
        acc = None if acc is None else acc[None]
    nb = a.shape[0]
    k_dim, m_dim = (a.shape[1], a.shape[2]) if ta else (a.shape[2], a.shape[1])
    n_dim = b.shape[1] if tb else b.shape[2]
    assert (b.shape[2] if tb else b.shape[1]) == k_dim, (a.shape, b.shape, ta, tb)
    tm, tn, tk = _tile(m_dim, tm), _tile(n_dim, tn), _tile(k_dim, tk)
    nk = k_dim // tk
    a_spec = (pl.BlockSpec((None, tk, tm), lambda g, i, j, k: (g, k, i)) if ta
              else pl.BlockSpec((None, tm, tk), lambda g, i, j, k: (g, i, k)))
    b_spec = (pl.BlockSpec((None, tn, tk), lambda g, i, j, k: (g, j, k)) if tb
              else pl.BlockSpec((None, tk, tn), lambda g, i, j, k: (g, k, j)))
    o_spec = pl.BlockSpec((None, tm, tn), lambda g, i, j, k: (g, i, j))
    dn = (((0 if ta else 1,), (1 if tb else 0,)), ((), ()))
    has_acc = acc is not None

    def body(*refs):
        a_ref, b_ref = refs[0], refs[1]
        o_ref = refs[-1]
        k = pl.program_id(3)
        p = lax.dot_general(a_ref[...].astype(BF16), b_ref[...].astype(BF16), dn,
                            preferred_element_type=F32)

        @pl.when(k == 0)
        def _():
            o_ref[...] = p + refs[2][...] if has_acc else p

        @pl.when(k > 0)
        def _():
            o_ref[...] += p

    out = pl.pallas_call(
        body, name=name,
        out_shape=jax.ShapeDtypeStruct((nb, m_dim, n_dim), F32),
        grid=(nb, m_dim // tm, n_dim // tn, nk),
        in_specs=[a_spec, b_spec] + ([o_spec] if has_acc else []),
        out_specs=o_spec,
        compiler_params=_params(("parallel", "parallel", "parallel", "arbitrary")),
    )(*([a, b] + ([acc] if has_acc else [])))
    return out if batched else out[0]


def _rowwise(f, rows, params, row_out, acc_out, *, tm, name):
    length = rows[0].shape[0]
    tm = _row_tile(length, tm)
    nr, npar, nro = len(rows), len(params), len(row_out)

    def body(*refs):
        ins = [r[...] for r in refs[:nr + npar]]
        outs = refs[nr + npar:]
        r_o, a_o = f(*ins)
        for ref, val in zip(outs[:nro], r_o):
            ref[...] = val.astype(ref.dtype)
        i = pl.program_id(0)
        for ref, val in zip(outs[nro:], a_o):
            @pl.when(i == 0)
            def _(ref=ref, val=val):
                ref[...] = val.astype(ref.dtype)

            @pl.when(i > 0)
            def _(ref=ref, val=val):
                ref[...] += val.astype(ref.dtype)

    in_specs = ([pl.BlockSpec((tm, r.shape[1]), lambda i: (i, 0)) for r in rows]
                + [pl.BlockSpec(p.shape, lambda i: (0, 0)) for p in params])
    out_specs = ([pl.BlockSpec((tm, w), lambda i: (i, 0)) for w, _ in row_out]
                 + [pl.BlockSpec(s, lambda i: (0, 0)) for s, _ in acc_out])
    out_shape = ([jax.ShapeDtypeStruct((length, w), dt) for w, dt in row_out]
                 + [jax.ShapeDtypeStruct(s, dt) for s, dt in acc_out])
    res = pl.pallas_call(
        body, name=name, out_shape=out_shape, grid=(length // tm,),
        in_specs=in_specs, out_specs=out_specs,
        compiler_params=_params(("arbitrary",) if acc_out else ("parallel",)),
    )(*rows, *params)
    return res[:nro], res[nro:]


def _rw_fwd(f, rows, params, *, tm, name):
    tm_ = _row_tile(rows[0].shape[0], tm)
    shapes = jax.eval_shape(f, *[jax.ShapeDtypeStruct((tm_, r.shape[1]), r.dtype) for r in rows],
                            *[jax.ShapeDtypeStruct(p.shape, p.dtype) for p in params])
    row_out = [(s.shape[1], s.dtype) for s in shapes]
    outs, _ = _rowwise(lambda *v: (f(*v), ()), rows, params, row_out, [], tm=tm, name=name)
    return outs


def _rw_bwd(f, rows, params, cots, *, row_grad, param_grad, tm, name):
    nr, npar, nct = len(rows), len(params), len(cots)

    def g(*vals):
        prim = vals[:nr] + vals[nr + nct:]
        ct = vals[nr:nr + nct]
        _, vjp = jax.vjp(f, *prim)
        grads = vjp(tuple(ct))
        return (tuple(grads[i] for i in range(nr) if row_grad[i]),
                tuple(grads[nr + i] for i in range(npar) if param_grad[i]))

    row_out = [(rows[i].shape[1], F32) for i in range(nr) if row_grad[i]]
    acc_out = [(params[i].shape, F32) for i in range(npar) if param_grad[i]]
    return _rowwise(g, list(rows) + list(cots), params, row_out, acc_out, tm=tm, name=name)


def _f_ln_res(x, h, g, b):
    pre = DN_ALPHA * x + h
    mu = jnp.mean(pre, axis=-1, keepdims=True)
    d = pre - mu
    var = jnp.mean(d * d, axis=-1, keepdims=True)
    return (d * lax.rsqrt(var + LN_EPS) * g + b,)


def _f_relu2(h):
    r = jnp.maximum(h, 0.0)
    return (r * r,)


def _silu(t):
    return t * jax.nn.sigmoid(t)


def _f_gdn_qkv(c):
    a = _silu(c)
    outs = []
    for part, scale in ((0, HEAD_DIM ** -0.5), (1, 1.0)):
        heads = []
        for h in range(GDN_HEADS):
            t = a[:, part * D_MODEL + h * HEAD_DIM: part * D_MODEL + (h + 1) * HEAD_DIM]
            t = t * lax.rsqrt(jnp.sum(t * t, axis=-1, keepdims=True) + 1e-6)
            heads.append(t * scale if scale != 1.0 else t)
        outs.append(jnp.concatenate(heads, axis=-1))
    outs.append(a[:, 2 * D_MODEL:])
    return tuple(outs)


def _f_gdn_out(o, z, norm_g):
    heads = []
    for h in range(GDN_HEADS):
        t = o[:, h * HEAD_DIM:(h + 1) * HEAD_DIM]
        t = t * lax.rsqrt(jnp.mean(t * t, axis=-1, keepdims=True) + RMS_EPS) * norm_g
        heads.append(t)
    return (jnp.concatenate(heads, axis=-1) * _silu(z),)


def _f_attn(xq, kmem, vmem):
    heads = []
    for h in range(XA_HEADS):
        sl = slice(h * HEAD_DIM, (h + 1) * HEAD_DIM)
        s = lax.dot_general(xq[:, sl].astype(BF16), kmem[:, sl].astype(BF16),
                            (((1,), (1,)), ((), ())), preferred_element_type=F32) * (HEAD_DIM ** -0.5)
        m = lax.stop_gradient(jnp.max(s, axis=-1, keepdims=True))
        e = jnp.exp(s - m)
        p = e / jnp.sum(e, axis=-1, keepdims=True)
        heads.append(jnp.dot(p.astype(BF16), vmem[:, sl].astype(BF16), preferred_element_type=F32))
    return (jnp.concatenate(heads, axis=-1),)


def _f_s5_gelu(y, u, d):
    return (jax.nn.gelu(y + d * u),)


def _f_s5_gate(zg, t, b):
    return (zg * jax.nn.sigmoid(t + b),)


def _f_add(a, b):
    return (a + b,)


def _f_add_and_narrow(a, b):
    s = a + b
    return s, s.astype(BF16)


def _f_add4(a, b, c, d):
    return (((a + b.astype(F32)) + c.astype(F32)) + d.astype(F32),)


def _f_adamw(w, g, m, v):
    m = ADAM_B1 * m + (1.0 - ADAM_B1) * g
    v = ADAM_B2 * v + (1.0 - ADAM_B2) * jnp.square(g)
    m_hat = m / (1.0 - ADAM_B1 ** ADAM_STEP)
    v_hat = v / (1.0 - ADAM_B2 ** ADAM_STEP)
    delta = -ADAM_LR * (m_hat / (jnp.sqrt(v_hat) + ADAM_EPS) + ADAM_WD * w)
    return delta, m, v


def _conv_fwd(u, w, *, tm, name):
    length, chans = u.shape
    tm = min(tm, length)
    tc = _tile(chans, 1024)
    hb = tm // SUBLANES

    def body(cur_ref, prev_ref, w_ref, o_ref, buf):
        i = pl.program_id(1)
        buf[0:SUBLANES, :] = jnp.where(i > 0, prev_ref[...], 0.0)
        buf[SUBLANES:, :] = cur_ref[...]
        acc = buf[pl.ds(SUBLANES - 3, tm), :] * w_ref[0:1, :]
        for k in range(1, GDN_CONV):
            acc = acc + buf[pl.ds(SUBLANES - 3 + k, tm), :] * w_ref[k:k + 1, :]
        o_ref[...] = acc

    return pl.pallas_call(
        body, name=name, out_shape=jax.ShapeDtypeStruct(u.shape, F32),
        grid=(chans // tc, length // tm),
        in_specs=[pl.BlockSpec((tm, tc), lambda j, i: (i, j)),
                  pl.BlockSpec((SUBLANES, tc), lambda j, i: (jnp.maximum(i * hb - 1, 0), j)),
                  pl.BlockSpec((GDN_CONV, tc), lambda j, i: (0, j))],
        out_specs=pl.BlockSpec((tm, tc), lambda j, i: (i, j)),
        scratch_shapes=[pltpu.VMEM((tm + SUBLANES, tc), F32)],
        compiler_params=_params(("parallel", "parallel")),
    )(u, u, w)


def _conv_bwd(u, w, dc, *, tm, name):
    length, chans = u.shape
    tm = min(tm, length)
    tc = _tile(chans, 1024)
    hb = tm // SUBLANES
    last = length // tm - 1

    def body(u_ref, uprev_ref, dc_ref, dcnext_ref, w_ref, du_ref, dw_ref, ubuf, dbuf):
        i = pl.program_id(1)
        ubuf[0:SUBLANES, :] = jnp.where(i > 0, uprev_ref[...], 0.0)
        ubuf[SUBLANES:, :] = u_ref[...]
        dbuf[0:tm, :] = dc_ref[...]
        dbuf[tm:, :] = jnp.where(i < last, dcnext_ref[...], 0.0)
        dcv = dc_ref[...]
        du = dbuf[pl.ds(3, tm), :] * w_ref[0:1, :]
        rows = [jnp.sum(dcv * ubuf[pl.ds(SUBLANES - 3, tm), :], axis=0, keepdims=True)]
        for k in range(1, GDN_CONV):
            du = du + dbuf[pl.ds(3 - k, tm), :] * w_ref[k:k + 1, :]
            rows.append(jnp.sum(dcv * ubuf[pl.ds(SUBLANES - 3 + k, tm), :], axis=0, keepdims=True))
        du_ref[...] = du
        dwv = jnp.concatenate(rows, axis=0)

        @pl.when(i == 0)
        def _():
            dw_ref[...] = dwv

        @pl.when(i > 0)
        def _():
            dw_ref[...] += dwv

    return pl.pallas_call(
        body, name=name,
        out_shape=(jax.ShapeDtypeStruct(u.shape, F32), jax.ShapeDtypeStruct((GDN_CONV, chans), F32)),
        grid=(chans // tc, length // tm),
        in_specs=[pl.BlockSpec((tm, tc), lambda j, i: (i, j)),
                  pl.BlockSpec((SUBLANES, tc), lambda j, i: (jnp.maximum(i * hb - 1, 0), j)),
                  pl.BlockSpec((tm, tc), lambda j, i: (i, j)),
                  pl.BlockSpec((SUBLANES, tc), lambda j, i: (jnp.minimum((i + 1) * hb, (last + 1) * hb - 1), j)),
                  pl.BlockSpec((GDN_CONV, tc), lambda j, i: (0, j))],
        out_specs=(pl.BlockSpec((tm, tc), lambda j, i: (i, j)),
                   pl.BlockSpec((GDN_CONV, tc), lambda j, i: (0, j))),
        scratch_shapes=[pltpu.VMEM((tm + SUBLANES, tc), F32), pltpu.VMEM((tm + SUBLANES, tc), F32)],
        compiler_params=_params(("parallel", "arbitrary")),
    )(u, u, dc, dc, w)


def _dot(a, b, dims, precision=None):
    if precision is None:
        a, b = a.astype(BF16), b.astype(BF16)
    return lax.dot_general(a, b, (dims, ((), ())), preferred_element_type=F32, precision=precision)


NN = ((1,), (0,))
NT = ((1,), (1,))
TN = ((0,), (0,))
HI = lax.Precision.HIGHEST


def _hmap(f, *lists):
    return [f(*t) for t in zip(*lists)]


@jax.custom_vjp
def _unit_lower_inverse(a):
    c = a[0].shape[0]
    eye = (lax.broadcasted_iota(jnp.int32, (c, c), 0) == lax.broadcasted_iota(jnp.int32, (c, c), 1)).astype(F32)
    p = _hmap(lambda x: -x, a)
    t = _hmap(lambda x: eye + x, p)
    for _ in range(int(math.log2(c)) - 1):
        p = _hmap(lambda x: _dot(x, x, NN, HI), p)
        t = _hmap(lambda x, y: x + _dot(x, y, NN, HI), t, p)
    return t


def _uli_fwd(a):
    t = _unit_lower_inverse(a)
    return t, t


def _uli_bwd(t, dt):
    left = _hmap(lambda x, y: _dot(x, y, TN, HI), t, dt)
    return (_hmap(lambda x, y: -_dot(x, y, NT, HI), left, t),)


_unit_lower_inverse.defvjp(_uli_fwd, _uli_bwd)


def _gdn_chunk(q, k, v, bl, al, a_log, dt_bias, state):
    c = q[0].shape[0]
    row = lax.broadcasted_iota(jnp.int32, (c, c), 0)
    col = lax.broadcasted_iota(jnp.int32, (c, c), 1)
    causal = row >= col
    strict = row > col
    eye = (row == col).astype(F32)
    beta = _hmap(jax.nn.sigmoid, bl)
    g = _hmap(lambda a_, l_, d_: -jnp.exp(a_) * jax.nn.softplus(l_ + d_), a_log, al, dt_bias)
    g_r = _hmap(lambda x: jnp.sum(eye * x, axis=0, keepdims=True), g)
    gc = _hmap(lambda x: jnp.sum(jnp.where(causal, x, 0.0), axis=1, keepdims=True), g_r)
    gc_r = _hmap(lambda x: jnp.sum(jnp.where(row <= col, x, 0.0), axis=0, keepdims=True), g)
    decay = _hmap(lambda x, y: jnp.where(causal, jnp.exp(jnp.where(causal, x - y, 0.0)), 0.0), gc, gc_r)
    e_gc = _hmap(jnp.exp, gc)
    kb = _hmap(jnp.multiply, k, beta)
    vb = _hmap(jnp.multiply, v, beta)
    a_mat = _hmap(lambda x, y, d: jnp.where(strict, _dot(x, y, NT) * d, 0.0), kb, k, decay)
    t_inv = _unit_lower_inverse(a_mat)
    u_blk = _hmap(lambda t, x: _dot(t, x, NN), t_inv, vb)
    w_blk = _hmap(lambda t, x, e: _dot(t, x * e, NN), t_inv, kb, e_gc)
    v_new = _hmap(lambda u, w, s: u - _dot(w, s, NN), u_blk, w_blk, state)
    attn = _hmap(lambda x, y, d: _dot(x, y, NT) * d, q, k, decay)
    o_state = _hmap(lambda x, e, s: _dot(x * e, s, NN), q, e_gc, state)
    o = _hmap(lambda base, at, vn: base + _dot(at, vn, NN), o_state, attn, v_new)
    g_last = _hmap(lambda x: jnp.sum(x, axis=0, keepdims=True), g)
    k_dec = _hmap(lambda x, gl, c_: x * jnp.exp(gl - c_), k, g_last, gc)
    new_state = _hmap(lambda s, gl, kd, vn: s * jnp.exp(gl) + _dot(kd, vn, TN), state, g_last, k_dec, v_new)
    return o, new_state


def _gdn_operands(q_ref, k_ref, v_ref, bav, alog_ref, dtb_ref):
    hs = range(GDN_HEADS)
    cols = [slice(h * HEAD_DIM, (h + 1) * HEAD_DIM) for h in hs]
    return ([q_ref[:, sl] for sl in cols], [k_ref[:, sl] for sl in cols], [v_ref[:, sl] for sl in cols],
            [bav[:, h:h + 1] for h in hs], [bav[:, h + GDN_HEADS:h + GDN_HEADS + 1] for h in hs],
            [alog_ref[h:h + 1, 0:1] for h in hs], [dtb_ref[h:h + 1, 0:1] for h in hs])


def _gdn_scan_fwd(q, k, v, ba, a_log, dt_bias, *, name):
    length = q.shape[0]
    n = length // GDN_CHUNK
    c = GDN_CHUNK

    def body(q_ref, k_ref, v_ref, ba_ref, alog_ref, dtb_ref, o_ref, s_ref, state):
        i = pl.program_id(0)

        @pl.when(i == 0)
        def _():
            state[...] = jnp.zeros_like(state)

        bav = ba_ref[...]
        heads = [slice(h * HEAD_DIM, (h + 1) * HEAD_DIM) for h in range(GDN_HEADS)]
        s_in = [state[h] for h in range(GDN_HEADS)]
        o, s_out = _gdn_chunk(*_gdn_operands(q_ref, k_ref, v_ref, bav, alog_ref, dtb_ref), s_in)
        for h, sl in enumerate(heads):
            s_ref[h] = s_in[h]
            o_ref[:, sl] = o[h]
            state[h] = s_out[h]

    row_spec = pl.BlockSpec((c, D_MODEL), lambda i: (i, 0))
    small = pl.BlockSpec((GDN_HEADS, LANES), lambda i: (0, 0))
    return pl.pallas_call(
        body, name=name,
        out_shape=(jax.ShapeDtypeStruct((length, D_MODEL), F32),
                   jax.ShapeDtypeStruct((n, GDN_HEADS, HEAD_DIM, HEAD_DIM), F32)),
        grid=(n,),
        in_specs=[row_spec, row_spec, row_spec, pl.BlockSpec((c, LANES), lambda i: (i, 0)), small, small],
        out_specs=(row_spec, pl.BlockSpec((None, GDN_HEADS, HEAD_DIM, HEAD_DIM), lambda i: (i, 0, 0, 0))),
        scratch_shapes=[pltpu.VMEM((GDN_HEADS, HEAD_DIM, HEAD_DIM), F32)],
        compiler_params=_params(("arbitrary",)),
    )(q, k, v, ba, a_log, dt_bias)


def _gdn_scan_bwd(q, k, v, ba, a_log, dt_bias, states, do, *, name):
    length = q.shape[0]
    n = length // GDN_CHUNK
    c = GDN_CHUNK

    def body(q_ref, k_ref, v_ref, ba_ref, alog_ref, dtb_ref, s_ref, do_ref,
             dq_ref, dk_ref, dv_ref, dba_ref, dalog_ref, ddtb_ref, dstate):
        i = pl.program_id(0)

        @pl.when(i == 0)
        def _():
            dstate[...] = jnp.zeros_like(dstate)
            dalog_ref[...] = jnp.zeros_like(dalog_ref)
            ddtb_ref[...] = jnp.zeros_like(ddtb_ref)

        bav = ba_ref[...]
        lane = lax.broadcasted_iota(jnp.int32, (c, LANES), 1)
        sub8 = lax.broadcasted_iota(jnp.int32, (GDN_HEADS, LANES), 0)
        lane8 = lax.broadcasted_iota(jnp.int32, (GDN_HEADS, LANES), 1)
        slab = jnp.zeros((c, LANES), F32)
        dalog_all = jnp.zeros((GDN_HEADS, LANES), F32)
        ddtb_all = jnp.zeros((GDN_HEADS, LANES), F32)
        heads = [slice(h * HEAD_DIM, (h + 1) * HEAD_DIM) for h in range(GDN_HEADS)]
        ds_in = [dstate[h] for h in range(GDN_HEADS)]
        s_in = [s_ref[h] for h in range(GDN_HEADS)]
        _, vjp = jax.vjp(_gdn_chunk, *_gdn_operands(q_ref, k_ref, v_ref, bav, alog_ref, dtb_ref), s_in)
        dq, dk, dv, dbl, dal, dalog, ddtb, ds = vjp(([do_ref[:, sl] for sl in heads], ds_in))
        for h, sl in enumerate(heads):
            dq_ref[:, sl] = dq[h]
            dk_ref[:, sl] = dk[h]
            dv_ref[:, sl] = dv[h]
            dstate[h] = ds[h]
            slab = slab + jnp.where(lane == h, dbl[h], 0.0) + jnp.where(lane == h + GDN_HEADS, dal[h], 0.0)
            here = (sub8 == h) & (lane8 == 0)
            dalog_all = dalog_all + jnp.where(here, dalog[h], 0.0)
            ddtb_all = ddtb_all + jnp.where(here, ddtb[h], 0.0)
        dba_ref[...] = slab
        dalog_ref[...] += dalog_all
        ddtb_ref[...] += ddtb_all

    row_spec = pl.BlockSpec((c, D_MODEL), lambda i: (n - 1 - i, 0))
    small = pl.BlockSpec((GDN_HEADS, LANES), lambda i: (0, 0))
    return pl.pallas_call(
        body, name=name,
        out_shape=(jax.ShapeDtypeStruct((length, D_MODEL), F32),) * 3
        + (jax.ShapeDtypeStruct((length, LANES), F32),
           jax.ShapeDtypeStruct((GDN_HEADS, LANES), F32), jax.ShapeDtypeStruct((GDN_HEADS, LANES), F32)),
        grid=(n,),
        in_specs=[row_spec, row_spec, row_spec,
                  pl.BlockSpec((c, LANES), lambda i: (n - 1 - i, 0)), small, small,
                  pl.BlockSpec((None, GDN_HEADS, HEAD_DIM, HEAD_DIM), lambda i: (n - 1 - i, 0, 0, 0)),
                  row_spec],
        out_specs=(row_spec, row_spec, row_spec,
                   pl.BlockSpec((c, LANES), lambda i: (n - 1 - i, 0)), small, small),
        scratch_shapes=[pltpu.VMEM((GDN_HEADS, HEAD_DIM, HEAD_DIM), F32)],
        compiler_params=_params(("arbitrary",)),
    )(q, k, v, ba, a_log, dt_bias, states, do)


S5_W = S5_T * LANES
S5_S = 2 * 8 * S5_STATE
S5_SH = S5_S // 2


def _s5_assemble(kb_ref, m_scr):
    zero = jnp.zeros((LANES, LANES), m_scr.dtype)
    for s in range(S5_T):
        for t in range(S5_T):
            m_scr[s * LANES:(s + 1) * LANES, t * LANES:(t + 1) * LANES] = kb_ref[t - s] if t >= s else zero


def _s5_scan_fwd(ur, kb, e, f, at, *, name):
    n = ur.shape[1]
    assert n % SUBLANES == 0

    def body(u_ref, kb_ref, e_ref, f_ref, at_ref, y_ref, h_ref, m_scr, g_scr):
        _s5_assemble(kb_ref, m_scr)
        u = u_ref[...]
        g_scr[...] = jnp.dot(u, f_ref[...], preferred_element_type=F32)
        ar, ai = at_ref[:, :S5_SH], at_ref[:, S5_SH:]

        def step(blk, h):
            base = pl.multiple_of(blk * SUBLANES, SUBLANES)
            g8 = g_scr[pl.ds(base, SUBLANES), :]
            rows = []
            for r in range(SUBLANES):
                rows.append(h)
                hr, hi = h[:, :S5_SH], h[:, S5_SH:]
                h = jnp.concatenate([ar * hr - ai * hi, ar * hi + ai * hr], axis=1) + g8[r:r + 1, :]
            h_ref[pl.ds(base, SUBLANES), :] = jnp.concatenate(rows, axis=0)
            return h

        lax.fori_loop(0, n // SUBLANES, step, jnp.zeros((1, S5_S), F32))
        y_ref[...] = (jnp.dot(u, m_scr[...], preferred_element_type=F32)
                      + jnp.dot(h_ref[...].astype(BF16), e_ref[...], preferred_element_type=F32))

    def spec(*tail):
        return pl.BlockSpec((None,) + tail, lambda k: (k,) + (0,) * len(tail))

    return pl.pallas_call(
        body, name=name,
        out_shape=(jax.ShapeDtypeStruct((S5_TILES, n, S5_W), F32), jax.ShapeDtypeStruct((S5_TILES, n, S5_S), F32)),
        grid=(S5_TILES,),
        in_specs=[spec(n, S5_W), spec(S5_T, LANES, LANES), spec(S5_S, S5_W), spec(S5_W, S5_S), spec(1, S5_S)],
        out_specs=(spec(n, S5_W), spec(n, S5_S)),
        scratch_shapes=[pltpu.VMEM((S5_W, S5_W), BF16), pltpu.VMEM((n, S5_S), F32)],
        compiler_params=_params(("parallel",)),
    )(ur, kb, e, f, at)


def _s5_scan_bwd(dyr, kb, e, f, at, hs, *, name):
    n = dyr.shape[1]

    def body(dy_ref, kb_ref, e_ref, f_ref, at_ref, h_ref, du_ref, dg_ref, dat_ref, m_scr, dh_scr):
        _s5_assemble(kb_ref, m_scr)
        dy = dy_ref[...]
        dh_scr[...] = _dot(dy, e_ref[...], NT)
        ar, ai = at_ref[:, :S5_SH], at_ref[:, S5_SH:]

        def step(it, carry):
            cy, dat = carry
            base = pl.multiple_of((n // SUBLANES - 1 - it) * SUBLANES, SUBLANES)
            dh8 = dh_scr[pl.ds(base, SUBLANES), :]
            h8 = h_ref[pl.ds(base, SUBLANES), :]
            rows = [None] * SUBLANES
            for r in reversed(range(SUBLANES)):
                rows[r] = cy
                cr, ci = cy[:, :S5_SH], cy[:, S5_SH:]
                hr, hi = h8[r:r + 1, :S5_SH], h8[r:r + 1, S5_SH:]
                dat = dat + jnp.concatenate([cr * hr + ci * hi, ci * hr - cr * hi], axis=1)
                cy = dh8[r:r + 1, :] + jnp.concatenate([ar * cr + ai * ci, ar * ci - ai * cr], axis=1)
            dg_ref[pl.ds(base, SUBLANES), :] = jnp.concatenate(rows, axis=0)
            return cy, dat

        zero = jnp.zeros((1, S5_S), F32)
        _, dat = lax.fori_loop(0, n // SUBLANES, step, (zero, zero))
        dat_ref[...] = dat
        du_ref[...] = _dot(dy, m_scr[...], NT) + _dot(dg_ref[...], f_ref[...], NT)

    def spec(*tail):
        return pl.BlockSpec((None,) + tail, lambda k: (k,) + (0,) * len(tail))

    return pl.pallas_call(
        body, name=name,
        out_shape=(jax.ShapeDtypeStruct((S5_TILES, n, S5_W), F32), jax.ShapeDtypeStruct((S5_TILES, n, S5_S), F32),
                   jax.ShapeDtypeStruct((S5_TILES, 1, S5_S), F32)),
        grid=(S5_TILES,),
        in_specs=[spec(n, S5_W), spec(S5_T, LANES, LANES), spec(S5_S, S5_W), spec(S5_W, S5_S), spec(1, S5_S),
                  spec(n, S5_S)],
        out_specs=(spec(n, S5_W), spec(n, S5_S), spec(1, S5_S)),
        scratch_shapes=[pltpu.VMEM((S5_W, S5_W), BF16), pltpu.VMEM((n, S5_S), F32)],
        compiler_params=_params(("parallel",)),
    )(dyr, kb, e, f, at, hs)


def _s5_prep(a_re, a_im, b_re, b_im, c_re, c_im, log_dt):
    t_len, tiles = S5_T, S5_TILES
    dt = jnp.exp(log_dt)[:, None]
    mag = jnp.exp(a_re * dt)
    ab_re, ab_im = mag * jnp.cos(a_im * dt), mag * jnp.sin(a_im * dt)
    den = jnp.square(a_re) + jnp.square(a_im)
    n_re, n_im = ab_re - 1.0, ab_im
    f_re = (n_re * a_re + n_im * a_im) / den
    f_im = (n_im * a_re - n_re * a_im) / den
    bb_re = f_re[..., None] * b_re - f_im[..., None] * b_im
    bb_im = f_re[..., None] * b_im + f_im[..., None] * b_re
    p_re, p_im = [jnp.ones_like(ab_re)], [jnp.zeros_like(ab_re)]
    for _ in range(t_len):
        p_re, p_im = (p_re + [p_re[-1] * ab_re - p_im[-1] * ab_im],
                      p_im + [p_re[-1] * ab_im + p_im[-1] * ab_re])
    rev_re, rev_im = jnp.stack(p_re[t_len - 1::-1]), jnp.stack(p_im[t_len - 1::-1])
    p_re, p_im = jnp.stack(p_re), jnp.stack(p_im)
    ca_re = c_re[None] * p_re[:, :, None, :] - c_im[None] * p_im[:, :, None, :]
    ca_im = c_re[None] * p_im[:, :, None, :] + c_im[None] * p_re[:, :, None, :]
    lag = (jnp.einsum('tgip,gpj->tgij', ca_re[:t_len], bb_re, precision=HI)
           - jnp.einsum('tgip,gpj->tgij', ca_im[:t_len], bb_im, precision=HI))
    eye = jnp.eye(8, dtype=F32)
    lag = lag.reshape(t_len, tiles, 8, S5_GROUP, S5_GROUP).transpose(1, 0, 2, 4, 3)
    kb = (lag[:, :, :, :, None, :] * eye[None, None, :, None, :, None]).reshape(tiles, t_len, LANES, LANES)
    e_st = jnp.stack([ca_re[1:], -ca_im[1:]])
    e_st = e_st.reshape(2, t_len, tiles, 8, S5_GROUP, S5_STATE).transpose(2, 0, 3, 5, 1, 4)
    e_op = (e_st[:, :, :, :, :, None, :] * eye[None, None, :, None, None, :, None]).reshape(tiles, S5_S, S5_W)
    ab_b = jnp.stack([rev_re[..., None] * bb_re[None] - rev_im[..., None] * bb_im[None],
                      rev_re[..., None] * bb_im[None] + rev_im[..., None] * bb_re[None]])
    ab_b = ab_b.reshape(2, t_len, tiles, 8, S5_STATE, S5_GROUP).transpose(2, 1, 3, 5, 0, 4)
    f_op = (ab_b[:, :, :, :, :, None, :] * eye[None, None, :, None, None, :, None]).reshape(tiles, S5_W, S5_S)
    a_t = jnp.stack([p_re[t_len], p_im[t_len]]).reshape(2, tiles, 8 * S5_STATE).transpose(1, 0, 2)
    return kb, e_op, f_op, a_t.reshape(tiles, 1, S5_S)


def _s5_to_chunks(u):
    n = u.shape[0] // S5_T
    return u.reshape(n, S5_T, S5_TILES, LANES).transpose(2, 0, 1, 3).reshape(S5_TILES, n, S5_W)


def _s5_from_chunks(yr):
    n = yr.shape[1]
    return yr.reshape(S5_TILES, n, S5_T, LANES).transpose(1, 2, 0, 3).reshape(n * S5_T, D_MODEL)


def _s5_dkb(dm):
    dmr = dm.reshape(S5_TILES, S5_T, LANES, S5_T, LANES)
    lags = []
    for lag in range(S5_T):
        acc = dmr[:, 0, :, lag, :]
        for s in range(1, S5_T - lag):
            acc = acc + dmr[:, s, :, s + lag, :]
        lags.append(acc)
    return jnp.stack(lags, axis=1)


TM_ROW = 256


def _gdn_fwd(x, w, tag):
    qkv = _mm(x, w["wqkv"], name="gdn_proj_qkv")
    z = _mm(x, w["wz"], name="gdn_proj_z")
    ba = _mm(x, w["wba"], name="gdn_proj_ba")
    cv = _conv_fwd(qkv, w["conv_w"], tm=TM_ROW, name="gdn_conv")
    q, k, v = _rw_fwd(_f_gdn_qkv, [cv], [], tm=TM_ROW, name="gdn_qkv")
    o, states = _gdn_scan_fwd(q, k, v, ba, w["a_log8"], w["dt_bias8"], name="gdn_scan")
    (mix,) = _rw_fwd(_f_gdn_out, [o, z], [w["norm_g"]], tm=TM_ROW, name="gdn_out")
    return mix, (qkv, z, ba, cv, q, k, v, states, o)


def _gdn_bwd(x, w, saved, dmix, dx_acc):
    qkv, z, ba, cv, q, k, v, states, o = saved
    (do, dz), (dnorm_g,) = _rw_bwd(_f_gdn_out, [o, z], [w["norm_g"]], [dmix], row_grad=[1, 1], param_grad=[1],
                                   tm=TM_ROW, name="gdn_out_bwd")
    dq, dk, dv, dba, dalog, ddtb = _gdn_scan_bwd(q, k, v, ba, w["a_log8"], w["dt_bias8"], states, do,
                                                  name="gdn_scan_bwd")
    (dcv,), _ = _rw_bwd(_f_gdn_qkv, [cv], [], [dq, dk, dv], row_grad=[1], param_grad=[], tm=TM_ROW,
                        name="gdn_qkv_bwd")
    dqkv, dconv_w = _conv_bwd(qkv, w["conv_w"], dcv, tm=TM_ROW, name="gdn_conv_bwd")
    dx = _mm(dqkv, w["wqkv"], tb=True, acc=dx_acc, name="gdn_dx_qkv")
    dx = _mm(dz, w["wz"], tb=True, acc=dx, name="gdn_dx_z")
    dx = _mm(dba, w["wba"], tb=True, acc=dx, name="gdn_dx_ba")
    grads = dict(wqkv=_mm(x, dqkv, ta=True, name="gdn_dw_qkv"), wz=_mm(x, dz, ta=True, name="gdn_dw_z"),
                 wba=_mm(x, dba, ta=True, name="gdn_dw_ba"), conv_w=dconv_w,
                 a_log=dalog[:, 0], dt_bias=ddtb[:, 0], norm_g=dnorm_g[0])
    return dx, grads


def _s5_fwd(x, w, tag):
    u = _mm(x, w["wu"], name="s5_proj_u")
    ur = _s5_to_chunks(u).astype(BF16)
    yr, hs = _s5_scan_fwd(ur, w["kb"], w["e_op"], w["f_op"], w["a_t"], name="s5_scan")
    y = _s5_from_chunks(yr)
    (zg,) = _rw_fwd(_f_s5_gelu, [y, u], [w["d"]], tm=TM_ROW, name="s5_gelu")
    t = _mm(zg, w["w_glu"], name="s5_glu")
    (mix,) = _rw_fwd(_f_s5_gate, [zg, t], [w["b_glu"]], tm=TM_ROW, name="s5_gate")
    return mix, (u, ur, hs, y, zg, t)


def _s5_bwd(x, w, saved, dmix, dx_acc):
    u, ur, hs, y, zg, t = saved
    (dzg, dt), (db_glu,) = _rw_bwd(_f_s5_gate, [zg, t], [w["b_glu"]], [dmix], row_grad=[1, 1], param_grad=[1],
                                   tm=TM_ROW, name="s5_gate_bwd")
    dzg = _mm(dt, w["w_glu"], tb=True, acc=dzg, name="s5_dzg")
    dw_glu = _mm(zg, dt, ta=True, name="s5_dw_glu")
    (dy, du), (dd,) = _rw_bwd(_f_s5_gelu, [y, u], [w["d"]], [dzg], row_grad=[1, 1], param_grad=[1],
                              tm=TM_ROW, name="s5_gelu_bwd")
    dyr = _s5_to_chunks(dy).astype(BF16)
    dur, dg, dat = _s5_scan_bwd(dyr, w["kb"], w["e_op"], w["f_op"], w["a_t"], hs, name="s5_scan_bwd")
    (du,) = _rw_fwd(_f_add, [du, _s5_from_chunks(dur)], [], tm=TM_ROW, name="s5_du_add")
    dm = _mm(ur, dyr, ta=True, name="s5_dm", tm=1024, tn=1024)
    de = _mm(hs, dyr, ta=True, name="s5_de", tm=1024, tn=1024)
    df = _mm(ur, dg, ta=True, name="s5_df", tm=1024, tn=1024)
    d_a_re, d_a_im, d_b_re, d_b_im, d_c_re, d_c_im, d_log_dt = w["prep_vjp"]((_s5_dkb(dm), de, df, dat))
    dx = _mm(du, w["wu"], tb=True, acc=dx_acc, name="s5_dx_u")
    grads = dict(wu=_mm(x, du, ta=True, name="s5_dw_u"), w_glu=dw_glu, b_glu=db_glu[0], d=dd[0],
                 a_re=d_a_re, a_im=d_a_im, b_re=d_b_re, b_im=d_b_im, c_re=d_c_re, c_im=d_c_im, log_dt=d_log_dt)
    return dx, grads


def _layer_fwd(x, mem, w, is_gdn):
    mix, msave = (_gdn_fwd if is_gdn else _s5_fwd)(x, w, "")
    xq = _mm(x, w["wxq"], name="proj_xq")
    kv = _mm(mem, w["wkv"], name="mem_kv")
    kmem, vmem = kv[:, :XA_DIM], kv[:, XA_DIM:]
    (cross,) = _rw_fwd(_f_attn, [xq], [kmem, vmem], tm=TM_ROW, name="attn")
    h = _mm(mix, w["wo_mix"], name="wo_mix")
    h = _mm(cross, w["wo_cross"], acc=h, name="wo_cross")
    (x1,) = _rw_fwd(_f_ln_res, [x, h], [w["ln1_g"], w["ln1_b"]], tm=TM_ROW, name="ln_res")
    hm = _mm(x1, w["w1"], name="mlp_up")
    (act,) = _rw_fwd(_f_relu2, [hm], [], tm=TM_ROW, name="relu2")
    f = _mm(act, w["w2"], name="mlp_down")
    (x2,) = _rw_fwd(_f_ln_res, [x1, f], [w["ln2_g"], w["ln2_b"]], tm=TM_ROW, name="ln_res")
    return x2, (x, msave, xq, kmem, vmem, mix, cross, h, x1, hm, act, f)


def _layer_bwd(mem, w, is_gdn, saved, dx2):
    x, msave, xq, kmem, vmem, mix, cross, h, x1, hm, act, f = saved
    (dx1, df), (dg2, db2) = _rw_bwd(_f_ln_res, [x1, f], [w["ln2_g"], w["ln2_b"]], [dx2], row_grad=[1, 1],
                                    param_grad=[1, 1], tm=TM_ROW, name="ln_res_bwd")
    dact = _mm(df, w["w2"], tb=True, name="mlp_dact")
    dw2 = _mm(act, df, ta=True, name="mlp_dw2")
    (dhm,), _ = _rw_bwd(_f_relu2, [hm], [], [dact], row_grad=[1], param_grad=[], tm=TM_ROW, name="relu2_bwd")
    dx1 = _mm(dhm, w["w1"], tb=True, acc=dx1, name="mlp_dx")
    dw1 = _mm(x1, dhm, ta=True, name="mlp_dw1")
    (dx, dh), (dg1, db1) = _rw_bwd(_f_ln_res, [x, h], [w["ln1_g"], w["ln1_b"]], [dx1], row_grad=[1, 1],
                                   param_grad=[1, 1], tm=TM_ROW, name="ln_res_bwd")
    dmix = _mm(dh, w["wo_mix"], tb=True, name="wo_dmix")
    dcross = _mm(dh, w["wo_cross"], tb=True, name="wo_dcross")
    dwo = jnp.concatenate([_mm(mix, dh, ta=True, name="wo_dw_mix"), _mm(cross, dh, ta=True, name="wo_dw_cross")], 0)
    (dxq,), (dkmem, dvmem) = _rw_bwd(_f_attn, [xq], [kmem, vmem], [dcross], row_grad=[1], param_grad=[1, 1],
                                     tm=TM_ROW, name="attn_bwd")
    dwkv = _mm(mem, jnp.concatenate([dkmem, dvmem], axis=1), ta=True, name="mem_dw_kv")
    dx = _mm(dxq, w["wxq"], tb=True, acc=dx, name="dx_xq")
    dwxq = _mm(x, dxq, ta=True, name="dw_xq")
    dx, mg = (_gdn_bwd if is_gdn else _s5_bwd)(x, w, msave, dmix, dx)
    grads = dict(mixer=mg, wxq=dwxq, wkv=dwkv, wo=dwo, w1=dw1, w2=dw2,
                 ln1_g=dg1[0], ln1_b=db1[0], ln2_g=dg2[0], ln2_b=db2[0])
    return dx, grads


def _loss_and_grad(y, target):
    def f(yv, tv):
        err = yv - tv
        return (err * (1.0 / D_MODEL),), (0.5 / D_MODEL * jnp.sum(err * err, axis=0, keepdims=True),)

    (dy,), (part,) = _rowwise(f, [y, target], [], [(D_MODEL, F32)], [((1, D_MODEL), F32)], tm=512, name="loss")
    return jnp.sum(part), dy


def _layer_weights(full, i):
    j = i // 2
    w = dict(wkv=full["w_kv_mem"][i].astype(BF16),
             wo_mix=full["w_o"][i, :D_MODEL].astype(BF16), wo_cross=full["w_o"][i, D_MODEL:].astype(BF16),
             ln1_g=full["ln1_g"][i][None], ln1_b=full["ln1_b"][i][None],
             ln2_g=full["ln2_g"][i][None], ln2_b=full["ln2_b"][i][None],
             w1=full["mlp_w1"][i].astype(BF16), w2=full["mlp_w2"][i].astype(BF16))
    if i % 2 == 0:
        w_in = full["gdn_w_in"][j]
        gd = 3 * D_MODEL
        w.update(wqkv=w_in[:, :gd].astype(BF16), wz=w_in[:, gd:gd + D_MODEL].astype(BF16),
                 wba=jnp.pad(w_in[:, gd + D_MODEL:gd + D_MODEL + 2 * GDN_HEADS],
                             ((0, 0), (0, LANES - 2 * GDN_HEADS))).astype(BF16),
                 wxq=w_in[:, gd + D_MODEL + 2 * GDN_HEADS:].astype(BF16),
                 conv_w=full["gdn_conv_w"][j],
                 a_log8=jnp.broadcast_to(full["gdn_a_log"][j][:, None], (GDN_HEADS, LANES)),
                 dt_bias8=jnp.broadcast_to(full["gdn_dt_bias"][j][:, None], (GDN_HEADS, LANES)),
                 norm_g=full["gdn_norm_g"][j][None])
    else:
        w_in = full["s5_w_in"][j]
        (kb, e_op, f_op, a_t), prep_vjp = jax.vjp(
            _s5_prep, full["s5_a_re"][j], full["s5_a_im"][j], full["s5_b_re"][j], full["s5_b_im"][j],
            full["s5_c_re"][j], full["s5_c_im"][j], full["s5_log_dt"][j])
        w.update(wu=w_in[:, :D_MODEL].astype(BF16), wxq=w_in[:, D_MODEL:].astype(BF16),
                 kb=kb.astype(BF16), e_op=e_op.astype(BF16), f_op=f_op.astype(BF16), a_t=a_t, prep_vjp=prep_vjp,
                 d=full["s5_d"][j][None], w_glu=full["s5_w_glu"][j].astype(BF16), b_glu=full["s5_b_glu"][j][None])
    return w


def _full_grads(layer_grads):
    g = layer_grads
    gdn = [g[i] for i in range(DEPTH) if i % 2 == 0]
    s5 = [g[i] for i in range(DEPTH) if i % 2 == 1]
    out = dict(
        w_kv_mem=jnp.stack([l["wkv"] for l in g]), w_o=jnp.stack([l["wo"] for l in g]),
        ln1_g=jnp.stack([l["ln1_g"] for l in g]), ln1_b=jnp.stack([l["ln1_b"] for l in g]),
        ln2_g=jnp.stack([l["ln2_g"] for l in g]), ln2_b=jnp.stack([l["ln2_b"] for l in g]),
        mlp_w1=jnp.stack([l["w1"] for l in g]), mlp_w2=jnp.stack([l["w2"] for l in g]),
        gdn_w_in=jnp.stack([jnp.concatenate([l["mixer"]["wqkv"], l["mixer"]["wz"],
                                             l["mixer"]["wba"][:, :2 * GDN_HEADS], l["wxq"]], axis=1) for l in gdn]),
        gdn_conv_w=jnp.stack([l["mixer"]["conv_w"] for l in gdn]),
        gdn_a_log=jnp.stack([l["mixer"]["a_log"] for l in gdn]),
        gdn_dt_bias=jnp.stack([l["mixer"]["dt_bias"] for l in gdn]),
        gdn_norm_g=jnp.stack([l["mixer"]["norm_g"] for l in gdn]),
        s5_w_in=jnp.stack([jnp.concatenate([l["mixer"]["wu"], l["wxq"]], axis=1) for l in s5]),
        s5_d=jnp.stack([l["mixer"]["d"] for l in s5]),
        s5_w_glu=jnp.stack([l["mixer"]["w_glu"] for l in s5]),
        s5_b_glu=jnp.stack([l["mixer"]["b_glu"] for l in s5]))
    for n in ("a_re", "a_im", "b_re", "b_im", "c_re", "c_im", "log_dt"):
        out["s5_" + n] = jnp.stack([l["mixer"][n] for l in s5])
    return out


def _local_step(x, mem, target, full):
    lw = [_layer_weights(full, i) for i in range(DEPTH)]
    saves = []
    h = x
    for i in range(DEPTH):
        h, s = _layer_fwd(h, mem, lw[i], i % 2 == 0)
        saves.append(s)
    loss, d = _loss_and_grad(h, target)
    grads = [None] * DEPTH
    for i in reversed(range(DEPTH)):
        d, grads[i] = _layer_bwd(mem, lw[i], i % 2 == 0, saves[i], d)
    return loss, d, _full_grads(grads)


ANY = pl.BlockSpec(memory_space=pl.ANY)
SHARD_ROWS = 1024
SMALL_ROWS = 128


def _place():
    return lax.axis_index("x"), lax.axis_index("y"), lax.axis_index("c")


def _other_chips(x, y):
    return [(1 - x, y), (x, 1 - y), (1 - x, 1 - y)]


def _all_gather_chips(wpack, *, name):
    rows = wpack.shape[0]
    half = rows // 2

    def body(w_ref, out_ref, send_sems, recv_sems, local_sem):
        x, y, c = _place()
        sibling = (x, y, 1 - c)
        chips = _other_chips(x, y)

        def blk(cx, cy, cc):
            return out_ref.at[2 * cx + cy, pl.ds(cc * half, half), :]

        def copy(k, src, dst, to):
            return pltpu.make_async_remote_copy(src_ref=src, dst_ref=dst, send_sem=send_sems.at[k],
                                                recv_sem=recv_sems.at[k], device_id=to, device_id_type=MESH)

        mine = pltpu.make_async_copy(w_ref, out_ref.at[2 * x + y], local_sem)
        mine.start()
        first = [copy(j, w_ref.at[pl.ds(c * half, half), :], blk(x, y, c), (cx, cy, c))
                 for j, (cx, cy) in enumerate(chips)]
        for cp in first:
            cp.start()
        passed = [copy(3 + j, blk(cx, cy, c), blk(cx, cy, c), sibling) for j, (cx, cy) in enumerate(chips)]
        for j, (cx, cy) in enumerate(chips):
            copy(j, blk(cx, cy, c), blk(cx, cy, c), (cx, cy, c)).wait_recv()
            passed[j].start()
        for j, (cx, cy) in enumerate(chips):
            copy(3 + j, blk(cx, cy, 1 - c), blk(cx, cy, 1 - c), sibling).wait_recv()
        for cp in first + passed:
            cp.wait_send()
        mine.wait()

    return pl.pallas_call(
        body, name=name, out_shape=jax.ShapeDtypeStruct((N_CHIPS, rows, D_MODEL), wpack.dtype),
        in_specs=[ANY], out_specs=ANY,
        scratch_shapes=[pltpu.SemaphoreType.DMA((6,)), pltpu.SemaphoreType.DMA((6,)), pltpu.SemaphoreType.DMA],
    )(wpack)


def _sibling_swap(buf, *, name):
    def body(in_ref, out_ref, send_sem, recv_sem):
        x, y, c = _place()
        cp = pltpu.make_async_remote_copy(src_ref=in_ref, dst_ref=out_ref, send_sem=send_sem, recv_sem=recv_sem,
                                          device_id=(x, y, 1 - c), device_id_type=MESH)
        cp.start()
        cp.wait()

    return pl.pallas_call(
        body, name=name, out_shape=jax.ShapeDtypeStruct(buf.shape, buf.dtype), in_specs=[ANY], out_specs=ANY,
        scratch_shapes=[pltpu.SemaphoreType.DMA, pltpu.SemaphoreType.DMA],
    )(buf)


def _chip_exchange(pieces, *, name):
    _, rows, width = pieces.shape

    def body(in_ref, out_ref, send_sems, recv_sems):
        x, y, c = _place()
        cps = [pltpu.make_async_remote_copy(src_ref=in_ref.at[2 * cx + cy], dst_ref=out_ref.at[j],
                                            send_sem=send_sems.at[j], recv_sem=recv_sems.at[j],
                                            device_id=(cx, cy, c), device_id_type=MESH)
               for j, (cx, cy) in enumerate(_other_chips(x, y))]
        for cp in cps:
            cp.start()
        for cp in cps:
            cp.wait()

    return pl.pallas_call(
        body, name=name, out_shape=jax.ShapeDtypeStruct((3, rows, width), pieces.dtype), in_specs=[ANY], out_specs=ANY,
        scratch_shapes=[pltpu.SemaphoreType.DMA((3,)), pltpu.SemaphoreType.DMA((3,))],
    )(pieces)


def _all_reduce_small(v, *, name):
    rows, width = v.shape

    def body(in_ref, out_ref, gath, send_sems, recv_sems):
        x, y, c = _place()
        me = 4 * x + 2 * y + c
        gath[me] = in_ref[...]
        peers = []
        for m in range(1, N_DEV):
            px = 1 - x if m & 4 else x
            py = 1 - y if m & 2 else y
            pc = 1 - c if m & 1 else c
            peers.append((m - 1, (px, py, pc), 4 * px + 2 * py + pc))
        for k, peer, _ in peers:
            pltpu.make_async_remote_copy(src_ref=in_ref, dst_ref=gath.at[me], send_sem=send_sems.at[k],
                                         recv_sem=recv_sems.at[k], device_id=peer, device_id_type=MESH).start()
        for k, peer, plin in peers:
            cp = pltpu.make_async_remote_copy(src_ref=in_ref, dst_ref=gath.at[plin], send_sem=send_sems.at[k],
                                              recv_sem=recv_sems.at[k], device_id=peer, device_id_type=MESH)
            cp.wait_send()
            cp.wait_recv()
        acc = gath[0]
        for d in range(1, N_DEV):
            acc = acc + gath[d]
        out_ref[...] = acc

    vmem = pl.BlockSpec(memory_space=pltpu.VMEM)
    return pl.pallas_call(
        body, name=name, out_shape=jax.ShapeDtypeStruct(v.shape, v.dtype), in_specs=[vmem], out_specs=vmem,
        scratch_shapes=[pltpu.VMEM((N_DEV, rows, width), v.dtype),
                        pltpu.SemaphoreType.DMA((N_DEV - 1,)), pltpu.SemaphoreType.DMA((N_DEV - 1,))],
        compiler_params=pltpu.CompilerParams(vmem_limit_bytes=VMEM_LIMIT_V7X),
    )(v)


def _reduce_scatter(gpack):
    x, y, c = _place()
    half = gpack.shape[1] // 2
    own = lax.dynamic_slice_in_dim(gpack, c * half, half, axis=1).reshape(N_CHIPS * half, D_MODEL)
    other = lax.dynamic_slice_in_dim(gpack, (1 - c) * half, half, axis=1).reshape(N_CHIPS * half, D_MODEL)
    got = _sibling_swap(other, name="rs_pair_swap")
    pair, pair16 = _rw_fwd(_f_add_and_narrow, [own, got], [], tm=512, name="rs_pair_add")
    pair = pair.reshape(N_CHIPS, half, D_MODEL)
    recv = _chip_exchange(pair16.reshape(N_CHIPS, half, D_MODEL), name="rs_chip_exchange")
    mine = lax.dynamic_index_in_dim(pair, 2 * x + y, axis=0, keepdims=False)
    (total,) = _rw_fwd(_f_add4, [mine, recv[0], recv[1], recv[2]], [], tm=512, name="rs_chip_add")
    theirs = _sibling_swap(total, name="rs_share_swap")
    return jnp.concatenate([jnp.where(c == 0, total, theirs), jnp.where(c == 0, theirs, total)], axis=0)


_SHARDED = (("w_kv_mem", 1), ("w_o", 1), ("mlp_w1", 2), ("mlp_w2", 1), ("gdn_w_in", 2), ("gdn_conv_w", 2),
            ("s5_w_in", 2), ("s5_d", 1), ("s5_w_glu", 1), ("s5_b_glu", 1))
_MATMUL_ONLY = ("w_kv_mem", "w_o", "mlp_w1", "mlp_w2", "gdn_w_in", "s5_w_in", "s5_w_glu")
_REPLICATED = ("ln1_g", "ln1_b", "ln2_g", "ln2_b", "gdn_a_log", "gdn_dt_bias", "gdn_norm_g", "s5_a_re", "s5_a_im",
               "s5_b_re", "s5_b_im", "s5_c_re", "s5_c_im", "s5_log_dt")
_WEIGHTS = ("w_kv_mem", "w_o", "ln1_g", "ln1_b", "ln2_g", "ln2_b", "mlp_w1", "mlp_w2", "gdn_w_in", "gdn_conv_w",
            "gdn_a_log", "gdn_dt_bias", "gdn_norm_g", "s5_w_in", "s5_a_re", "s5_a_im", "s5_b_re", "s5_b_im",
            "s5_c_re", "s5_c_im", "s5_log_dt", "s5_d", "s5_w_glu", "s5_b_glu")


def _pack(arrs, lead=(), unit_rows=SHARD_ROWS):
    nl = len(lead)
    flat = jnp.concatenate([a.reshape(lead + (-1,)) for a in arrs], axis=nl)
    unit = unit_rows * D_MODEL
    pad = -flat.shape[nl] % unit
    flat = jnp.pad(flat, ((0, 0),) * nl + ((0, pad),))
    return flat.reshape(lead + (-1, D_MODEL))


def _unpack(packed, shapes, lead=()):
    nl = len(lead)
    flat = packed.reshape(lead + (-1,))
    out, off = [], 0
    for s in shapes:
        n = math.prod(s)
        out.append(lax.slice_in_dim(flat, off, off + n, axis=nl).reshape(lead + tuple(s)))
        off += n
    return out


def _split3(t):
    hi = t.astype(BF16)
    r1 = t - hi.astype(F32)
    mid = r1.astype(BF16)
    lo = (r1 - mid.astype(F32)).astype(BF16)
    return jnp.stack([hi, mid, lo], axis=-1)


def _join3(t):
    return (t[..., 0].astype(F32) + t[..., 1].astype(F32)) + t[..., 2].astype(F32)


def _merge_chips(blocks, axis):
    return jnp.concatenate([blocks[s] for s in range(N_CHIPS)], axis=axis)


def _split_chips(full, axis):
    n = full.shape[axis] // N_CHIPS
    return jnp.stack([lax.slice_in_dim(full, s * n, (s + 1) * n, axis=axis) for s in range(N_CHIPS)])


def kernel(x, mem, w_kv_mem, w_o, ln1_g, ln1_b, ln2_g, ln2_b, mlp_w1, mlp_w2, gdn_w_in, gdn_conv_w, gdn_a_log, gdn_dt_bias, gdn_norm_g, s5_w_in, s5_a_re, s5_a_im, s5_b_re, s5_b_im, s5_c_re, s5_c_im, s5_log_dt, s5_d, s5_w_glu, s5_b_glu, loss_target, m_w_kv_mem, m_w_o, m_ln1_g, m_ln1_b, m_ln2_g, m_ln2_b, m_mlp_w1, m_mlp_w2, m_gdn_w_in, m_gdn_conv_w, m_gdn_a_log, m_gdn_dt_bias, m_gdn_norm_g, m_s5_w_in, m_s5_a_re, m_s5_a_im, m_s5_b_re, m_s5_b_im, m_s5_c_re, m_s5_c_im, m_s5_log_dt, m_s5_d, m_s5_w_glu, m_s5_b_glu, v_w_kv_mem, v_w_o, v_ln1_g, v_ln1_b, v_ln2_g, v_ln2_b, v_mlp_w1, v_mlp_w2, v_gdn_w_in, v_gdn_conv_w, v_gdn_a_log, v_gdn_dt_bias, v_gdn_norm_g, v_s5_w_in, v_s5_a_re, v_s5_a_im, v_s5_b_re, v_s5_b_im, v_s5_c_re, v_s5_c_im, v_s5_log_dt, v_s5_d, v_s5_w_glu, v_s5_b_glu):
    given = dict(locals())
    w = {n: given[n] for n in _WEIGHTS}
    mom = {n: given["m_" + n] for n in _WEIGHTS}
    var = {n: given["v_" + n] for n in _WEIGHTS}
    shard_names = [n for n, _ in _SHARDED]
    shard_shapes = [w[n].shape for n in shard_names]
    rep_shapes = [w[n].shape for n in _REPLICATED]

    wire = [w[n].astype(BF16) if n in _MATMUL_ONLY else _split3(w[n]) for n in shard_names]
    gathered = _all_gather_chips(_pack(wire), name="gather_weights")
    blocks = _unpack(gathered, [a.shape for a in wire], lead=(N_CHIPS,))
    full = {n: _merge_chips(blk if n in _MATMUL_ONLY else _join3(blk), ax) for (n, ax), blk in zip(_SHARDED, blocks)}
    full.update({n: w[n] for n in _REPLICATED})
    wpack = _pack([w[n] for n in shard_names])

    loss, grad_x, grads = _local_step(x[0], mem[0], loss_target[0], full)
    loss = lax.psum(loss, ("x", "y", "c"))

    gpack = _pack([_split_chips(grads[n], ax) for n, ax in _SHARDED], lead=(N_CHIPS,))
    g_shard = _reduce_scatter(gpack)
    def pack_small(d):
        return _pack([d[n] for n in _REPLICATED], unit_rows=SMALL_ROWS)

    g_rep = _all_reduce_small(pack_small(grads), name="reduce_replicated")

    def adamw(wp, gp, mp, vp, name):
        return _rw_fwd(_f_adamw, [wp, gp, mp, vp], [], tm=512, name=name)

    d_s, m_s, v_s = adamw(wpack, g_shard, _pack([mom[n] for n in shard_names]), _pack([var[n] for n in shard_names]),
                          "adamw_shards")
    d_r, m_r, v_r = adamw(pack_small(w), g_rep, pack_small(mom), pack_small(var), "adamw_replicated")
    outs = {}
    for kind, ps, pr in (("grad", g_shard, g_rep), ("delta", d_s, d_r), ("new_m", m_s, m_r), ("new_v", v_s, v_r)):
        outs.update({(kind, n): a for n, a in zip(shard_names, _unpack(ps, shard_shapes))})
        outs.update({(kind, n): a for n, a in zip(_REPLICATED, _unpack(pr, rep_shapes))})
    return (loss, grad_x[None]) + tuple(outs[(kind, n)] for kind in ("grad", "delta", "new_m", "new_v")
                                        for n in _WEIGHTS)
```

```python
import functools
import math

import jax
import jax.numpy as jnp
from jax import lax
from jax.experimental import pallas as pl
from jax.experimental.pallas import tpu as pltpu

F32 = jnp.float32
BF16 = jnp.bfloat16
MESH = pl.DeviceIdType.MESH

D_MODEL = 1024
DEPTH = 4
GDN_HEADS = 8
HEAD_DIM = 128
GDN_CONV = 4
GDN_CHUNK = 64
S5_GROUPS = 64
S5_GROUP = 16
S5_STATE = 64
XA_HEADS = 4
XA_DIM = 512
D_FF = 4096
DN_ALPHA = (2 * DEPTH) ** 0.25
LN_EPS = 1e-5
RMS_EPS = 1e-6
ADAM_LR, ADAM_B1, ADAM_B2, ADAM_EPS, ADAM_WD, ADAM_STEP = 0.001, 0.9, 0.999, 1e-08, 0.01, 10

VMEM_LIMIT_V7X = 56 * 1024 * 1024
LANES = 128
SUBLANES = 8
S5_T = 16
S5_TILES = D_MODEL // LANES
N_CHIPS = 4
N_DEV = 8


def _params(sem):
    return pltpu.CompilerParams(dimension_semantics=sem, vmem_limit_bytes=VMEM_LIMIT_V7X)


def _tile(n, pref):
    if n <= pref:
        return n
    t = (pref // LANES) * LANES
    while n % t:
        t -= LANES
    return t


def _row_tile(n, pref):
    if n % SUBLANES:
        return n
    t = min(pref, n) // SUBLANES * SUBLANES
    while n % t:
        t -= SUBLANES
    return t


def _mm(a, b, *, ta=False, tb=False, acc=None, name, tm=1024, tn=1024, tk=1024):
    k_dim, m_dim = a.shape if ta else a.shape[::-1]
    n_dim = b.shape[0] if tb else b.shape[1]
    assert (b.shape[1] if tb else b.shape[0]) == k_dim, (a.shape, b.shape, ta, tb)
    tm, tn, tk = _tile(m_dim, tm), _tile(n_dim, tn), _tile(k_dim, tk)
    a_spec = (pl.BlockSpec((tk, tm), lambda i, j, k: (k, i)) if ta else pl.BlockSpec((tm, tk), lambda i, j, k: (i, k)))
    b_spec = (pl.BlockSpec((tn, tk), lambda i, j, k: (j, k)) if tb else pl.BlockSpec((tk, tn), lambda i, j, k: (k, j)))
    o_spec = pl.BlockSpec((tm, tn), lambda i, j, k: (i, j))
    dn = (((0 if ta else 1,), (1 if tb else 0,)), ((), ()))
    has_acc = acc is not None

    def body(*refs):
        a_ref, b_ref = refs[0], refs[1]
        o_ref = refs[-1]
        k = pl.program_id(2)
        p = lax.dot_general(a_ref[...].astype(BF16), b_ref[...].astype(BF16), dn,
                            preferred_element_type=F32)

        @pl.when(k == 0)
        def _():
            o_ref[...] = p + refs[2][...] if has_acc else p

        @pl.when(k > 0)
        def _():
            o_ref[...] += p

    return pl.pallas_call(
        body, name=name,
        out_shape=jax.ShapeDtypeStruct((m_dim, n_dim), F32),
        grid=(m_dim // tm, n_dim // tn, k_dim // tk),
        in_specs=[a_spec, b_spec] + ([o_spec] if has_acc else []),
        out_specs=o_spec,
        compiler_params=_params(("parallel", "parallel", "arbitrary")),
    )(*([a, b] + ([acc] if has_acc else [])))


def _mm_relu2(a, b, *, name, tm=1024, tn=1024):
    m_dim, k_dim = a.shape
    n_dim = b.shape[1]
    tm, tn = _tile(m_dim, tm), _tile(n_dim, tn)

    def body(a_ref, b_ref, h_ref, act_ref):
        h = jnp.dot(a_ref[...].astype(BF16), b_ref[...].astype(BF16), preferred_element_type=F32)
        h_ref[...] = h
        r = jnp.maximum(h, 0.0)
        act_ref[...] = (r * r).astype(BF16)

    o_spec = pl.BlockSpec((tm, tn), lambda i, j: (i, j))
    return pl.pallas_call(
        body, name=name,
        out_shape=(jax.ShapeDtypeStruct((m_dim, n_dim), F32), jax.ShapeDtypeStruct((m_dim, n_dim), BF16)),
        grid=(m_dim // tm, n_dim // tn),
        in_specs=[pl.BlockSpec((tm, k_dim), lambda i, j: (i, 0)), pl.BlockSpec((k_dim, tn), lambda i, j: (0, j))],
        out_specs=(o_spec, o_spec),
        compiler_params=_params(("parallel", "parallel")),
    )(a, b)


def _mm_relu2_grad(d, b, h, *, name, tm=1024, tn=1024):
    m_dim, k_dim = d.shape
    n_dim = b.shape[0]
    tm, tn = _tile(m_dim, tm), _tile(n_dim, tn)

    def body(d_ref, b_ref, h_ref, o_ref):
        p = lax.dot_general(d_ref[...].astype(BF16), b_ref[...].astype(BF16), ((NT), ((), ())),
                            preferred_element_type=F32)
        o_ref[...] = (p * (2.0 * jnp.maximum(h_ref[...], 0.0))).astype(BF16)

    o_spec = pl.BlockSpec((tm, tn), lambda i, j: (i, j))
    return pl.pallas_call(
        body, name=name,
        out_shape=jax.ShapeDtypeStruct((m_dim, n_dim), BF16),
        grid=(m_dim // tm, n_dim // tn),
        in_specs=[pl.BlockSpec((tm, k_dim), lambda i, j: (i, 0)), pl.BlockSpec((tn, k_dim), lambda i, j: (j, 0)), o_spec],
        out_specs=o_spec,
        compiler_params=_params(("parallel", "parallel")),
    )(d, b, h)


def _rowwise(f, rows, params, row_out, acc_out, *, tm, name):
    length = rows[0].shape[0]
    tm = _row_tile(length, tm)
    nr, npar, nro = len(rows), len(params), len(row_out)

    def body(*refs):
        ins = [r[...] for r in refs[:nr + npar]]
        outs = refs[nr + npar:]
        r_o, a_o = f(*ins)
        for ref, val in zip(outs[:nro], r_o):
            ref[...] = val.astype(ref.dtype)
        i = pl.program_id(0)
        for ref, val in zip(outs[nro:], a_o):
            @pl.when(i == 0)
            def _(ref=ref, val=val):
                ref[...] = val.astype(ref.dtype)

            @pl.when(i > 0)
            def _(ref=ref, val=val):
                ref[...] += val.astype(ref.dtype)

    in_specs = ([pl.BlockSpec((tm, r.shape[1]), lambda i: (i, 0)) for r in rows]
                + [pl.BlockSpec(p.shape, lambda i: (0, 0)) for p in params])
    out_specs = ([pl.BlockSpec((tm, w), lambda i: (i, 0)) for w, _ in row_out]
                 + [pl.BlockSpec(s, lambda i: (0, 0)) for s, _ in acc_out])
    out_shape = ([jax.ShapeDtypeStruct((length, w), dt) for w, dt in row_out]
                 + [jax.ShapeDtypeStruct(s, dt) for s, dt in acc_out])
    res = pl.pallas_call(
        body, name=name, out_shape=out_shape, grid=(length // tm,),
        in_specs=in_specs, out_specs=out_specs,
        compiler_params=_params(("arbitrary",) if acc_out else ("parallel",)),
    )(*rows, *params)
    return res[:nro], res[nro:]


def _rw_fwd(f, rows, params, *, tm, name):
    tm_ = _row_tile(rows[0].shape[0], tm)
    shapes = jax.eval_shape(f, *[jax.ShapeDtypeStruct((tm_, r.shape[1]), r.dtype) for r in rows],
                            *[jax.ShapeDtypeStruct(p.shape, p.dtype) for p in params])
    row_out = [(s.shape[1], s.dtype) for s in shapes]
    outs, _ = _rowwise(lambda *v: (f(*v), ()), rows, params, row_out, [], tm=tm, name=name)
    return outs


def _rw_bwd(f, rows, params, cots, *, row_grad, param_grad, tm, name):
    nr, npar, nct = len(rows), len(params), len(cots)

    def g(*vals):
        prim = vals[:nr] + vals[nr + nct:]
        ct = vals[nr:nr + nct]
        _, vjp = jax.vjp(f, *prim)
        grads = vjp(tuple(ct))
        return (tuple(grads[i] for i in range(nr) if row_grad[i]),
                tuple(grads[nr + i] for i in range(npar) if param_grad[i]))

    row_out = [(rows[i].shape[1], F32) for i in range(nr) if row_grad[i]]
    acc_out = [(params[i].shape, F32) for i in range(npar) if param_grad[i]]
    return _rowwise(g, list(rows) + list(cots), params, row_out, acc_out, tm=tm, name=name)


def _f_ln_res(x, h, g, b):
    pre = DN_ALPHA * x + h
    mu = jnp.mean(pre, axis=-1, keepdims=True)
    d = pre - mu
    var = jnp.mean(d * d, axis=-1, keepdims=True)
    return (d * lax.rsqrt(var + LN_EPS) * g + b,)


def _silu(t):
    return t * jax.nn.sigmoid(t)


def _f_gdn_qkv(c):
    a = _silu(c)
    outs = []
    for part, scale in ((0, HEAD_DIM ** -0.5), (1, 1.0)):
        heads = []
        for h in range(GDN_HEADS):
            t = a[:, part * D_MODEL + h * HEAD_DIM: part * D_MODEL + (h + 1) * HEAD_DIM]
            t = t * lax.rsqrt(jnp.sum(t * t, axis=-1, keepdims=True) + 1e-6)
            heads.append(t * scale if scale != 1.0 else t)
        outs.append(jnp.concatenate(heads, axis=-1))
    outs.append(a[:, 2 * D_MODEL:])
    return tuple(outs)


def _f_gdn_out(o, z, norm_g):
    heads = []
    for h in range(GDN_HEADS):
        t = o[:, h * HEAD_DIM:(h + 1) * HEAD_DIM]
        t = t * lax.rsqrt(jnp.mean(t * t, axis=-1, keepdims=True) + RMS_EPS) * norm_g
        heads.append(t)
    return (jnp.concatenate(heads, axis=-1) * _silu(z),)


def _f_attn(xq, kmem, vmem):
    heads = []
    for h in range(XA_HEADS):
        sl = slice(h * HEAD_DIM, (h + 1) * HEAD_DIM)
        s = lax.dot_general(xq[:, sl].astype(BF16), kmem[:, sl].astype(BF16),
                            (((1,), (1,)), ((), ())), preferred_element_type=F32) * (HEAD_DIM ** -0.5)
        m = lax.stop_gradient(jnp.max(s, axis=-1, keepdims=True))
        e = jnp.exp(s - m)
        p = e / jnp.sum(e, axis=-1, keepdims=True)
        heads.append(jnp.dot(p.astype(BF16), vmem[:, sl].astype(BF16), preferred_element_type=F32))
    return (jnp.concatenate(heads, axis=-1),)


def _f_s5_gelu(y, u, d):
    return (jax.nn.gelu(y + d * u),)


def _f_s5_gate(zg, t, b):
    return (zg * jax.nn.sigmoid(t + b),)


def _f_add(a, b):
    return (a + b,)


def _f_add_and_narrow(a, b):
    s = a + b
    return s, s.astype(BF16)


def _f_add4(a, b, c, d):
    return (((a + b.astype(F32)) + c.astype(F32)) + d.astype(F32),)


def _f_adamw(w, g, m, v):
    m = ADAM_B1 * m + (1.0 - ADAM_B1) * g
    v = ADAM_B2 * v + (1.0 - ADAM_B2) * jnp.square(g)
    m_hat = m / (1.0 - ADAM_B1 ** ADAM_STEP)
    v_hat = v / (1.0 - ADAM_B2 ** ADAM_STEP)
    delta = -ADAM_LR * (m_hat / (jnp.sqrt(v_hat) + ADAM_EPS) + ADAM_WD * w)
    return delta, m, v


def _conv_fwd(u, w, *, tm, name):
    length, chans = u.shape
    tm = min(tm, length)
    tc = _tile(chans, 1024)
    hb = tm // SUBLANES

    def body(cur_ref, prev_ref, w_ref, o_ref, buf):
        i = pl.program_id(1)
        buf[0:SUBLANES, :] = jnp.where(i > 0, prev_ref[...], 0.0)
        buf[SUBLANES:, :] = cur_ref[...]
        acc = buf[pl.ds(SUBLANES - 3, tm), :] * w_ref[0:1, :]
        for k in range(1, GDN_CONV):
            acc = acc + buf[pl.ds(SUBLANES - 3 + k, tm), :] * w_ref[k:k + 1, :]
        o_ref[...] = acc

    return pl.pallas_call(
        body, name=name, out_shape=jax.ShapeDtypeStruct(u.shape, F32),
        grid=(chans // tc, length // tm),
        in_specs=[pl.BlockSpec((tm, tc), lambda j, i: (i, j)),
                  pl.BlockSpec((SUBLANES, tc), lambda j, i: (jnp.maximum(i * hb - 1, 0), j)),
                  pl.BlockSpec((GDN_CONV, tc), lambda j, i: (0, j))],
        out_specs=pl.BlockSpec((tm, tc), lambda j, i: (i, j)),
        scratch_shapes=[pltpu.VMEM((tm + SUBLANES, tc), F32)],
        compiler_params=_params(("parallel", "parallel")),
    )(u, u, w)


def _conv_bwd(u, w, dc, *, tm, name):
    length, chans = u.shape
    tm = min(tm, length)
    tc = _tile(chans, 1024)
    hb = tm // SUBLANES
    last = length // tm - 1

    def body(u_ref, uprev_ref, dc_ref, dcnext_ref, w_ref, du_ref, dw_ref, ubuf, dbuf):
        i = pl.program_id(1)
        ubuf[0:SUBLANES, :] = jnp.where(i > 0, uprev_ref[...], 0.0)
        ubuf[SUBLANES:, :] = u_ref[...]
        dbuf[0:tm, :] = dc_ref[...]
        dbuf[tm:, :] = jnp.where(i < last, dcnext_ref[...], 0.0)
        dcv = dc_ref[...]
        du = dbuf[pl.ds(3, tm), :] * w_ref[0:1, :]
        rows = [jnp.sum(dcv * ubuf[pl.ds(SUBLANES - 3, tm), :], axis=0, keepdims=True)]
        for k in range(1, GDN_CONV):
            du = du + dbuf[pl.ds(3 - k, tm), :] * w_ref[k:k + 1, :]
            rows.append(jnp.sum(dcv * ubuf[pl.ds(SUBLANES - 3 + k, tm), :], axis=0, keepdims=True))
        du_ref[...] = du
        dwv = jnp.concatenate(rows, axis=0)

        @pl.when(i == 0)
        def _():
            dw_ref[...] = dwv

        @pl.when(i > 0)
        def _():
            dw_ref[...] += dwv

    return pl.pallas_call(
        body, name=name,
        out_shape=(jax.ShapeDtypeStruct(u.shape, F32), jax.ShapeDtypeStruct((GDN_CONV, chans), F32)),
        grid=(chans // tc, length // tm),
        in_specs=[pl.BlockSpec((tm, tc), lambda j, i: (i, j)),
                  pl.BlockSpec((SUBLANES, tc), lambda j, i: (jnp.maximum(i * hb - 1, 0), j)),
                  pl.BlockSpec((tm, tc), lambda j, i: (i, j)),
                  pl.BlockSpec((SUBLANES, tc), lambda j, i: (jnp.minimum((i + 1) * hb, (last + 1) * hb - 1), j)),
                  pl.BlockSpec((GDN_CONV, tc), lambda j, i: (0, j))],
        out_specs=(pl.BlockSpec((tm, tc), lambda j, i: (i, j)),
                   pl.BlockSpec((GDN_CONV, tc), lambda j, i: (0, j))),
        scratch_shapes=[pltpu.VMEM((tm + SUBLANES, tc), F32), pltpu.VMEM((tm + SUBLANES, tc), F32)],
        compiler_params=_params(("parallel", "arbitrary")),
    )(u, u, dc, dc, w)


def _dot(a, b, dims, precision=None):
    if precision is None:
        a, b = a.astype(BF16), b.astype(BF16)
    return lax.dot_general(a, b, (dims, ((), ())), preferred_element_type=F32, precision=precision)


NN = ((1,), (0,))
NT = ((1,), (1,))
TN = ((0,), (0,))
HI = lax.Precision.HIGHEST


def _hmap(f, *lists):
    return [f(*t) for t in zip(*lists)]


@jax.custom_vjp
def _unit_lower_inverse(a):
    c = a[0].shape[0]
    eye = (lax.broadcasted_iota(jnp.int32, (c, c), 0) == lax.broadcasted_iota(jnp.int32, (c, c), 1)).astype(F32)
    p = _hmap(lambda x: -x, a)
    t = _hmap(lambda x: eye + x, p)
    for _ in range(int(math.log2(c)) - 1):
        p = _hmap(lambda x: _dot(x, x, NN, HI), p)
        t = _hmap(lambda x, y: x + _dot(x, y, NN, HI), t, p)
    return t


def _uli_fwd(a):
    t = _unit_lower_inverse(a)
    return t, t


def _uli_bwd(t, dt):
    left = _hmap(lambda x, y: _dot(x, y, TN, HI), t, dt)
    return (_hmap(lambda x, y: -_dot(x, y, NT, HI), left, t),)


_unit_lower_inverse.defvjp(_uli_fwd, _uli_bwd)


def _gdn_chunk(q, k, v, bl, al, a_log, dt_bias, state):
    c = q[0].shape[0]
    row = lax.broadcasted_iota(jnp.int32, (c, c), 0)
    col = lax.broadcasted_iota(jnp.int32, (c, c), 1)
    causal = row >= col
    strict = row > col
    eye = (row == col).astype(F32)
    beta = _hmap(jax.nn.sigmoid, bl)
    g = _hmap(lambda a_, l_, d_: -jnp.exp(a_) * jax.nn.softplus(l_ + d_), a_log, al, dt_bias)
    g_r = _hmap(lambda x: jnp.sum(eye * x, axis=0, keepdims=True), g)
    gc = _hmap(lambda x: jnp.sum(jnp.where(causal, x, 0.0), axis=1, keepdims=True), g_r)
    gc_r = _hmap(lambda x: jnp.sum(jnp.where(row <= col, x, 0.0), axis=0, keepdims=True), g)
    decay = _hmap(lambda x, y: jnp.where(causal, jnp.exp(jnp.where(causal, x - y, 0.0)), 0.0), gc, gc_r)
    e_gc = _hmap(jnp.exp, gc)
    kb = _hmap(jnp.multiply, k, beta)
    vb = _hmap(jnp.multiply, v, beta)
    a_mat = _hmap(lambda x, y, d: jnp.where(strict, _dot(x, y, NT) * d, 0.0), kb, k, decay)
    t_inv = _unit_lower_inverse(a_mat)
    u_blk = _hmap(lambda t, x: _dot(t, x, NN), t_inv, vb)
    w_blk = _hmap(lambda t, x, e: _dot(t, x * e, NN), t_inv, kb, e_gc)
    v_new = _hmap(lambda u, w, s: u - _dot(w, s, NN), u_blk, w_blk, state)
    attn = _hmap(lambda x, y, d: _dot(x, y, NT) * d, q, k, decay)
    o_state = _hmap(lambda x, e, s: _dot(x * e, s, NN), q, e_gc, state)
    o = _hmap(lambda base, at, vn: base + _dot(at, vn, NN), o_state, attn, v_new)
    g_last = _hmap(lambda x: jnp.sum(x, axis=0, keepdims=True), g)
    k_dec = _hmap(lambda x, gl, c_: x * jnp.exp(gl - c_), k, g_last, gc)
    new_state = _hmap(lambda s, gl, kd, vn: s * jnp.exp(gl) + _dot(kd, vn, TN), state, g_last, k_dec, v_new)
    return o, new_state


def _gdn_operands(q_ref, k_ref, v_ref, bav, alog_ref, dtb_ref):
    hs = range(GDN_HEADS)
    cols = [slice(h * HEAD_DIM, (h + 1) * HEAD_DIM) for h in hs]
    return ([q_ref[:, sl] for sl in cols], [k_ref[:, sl] for sl in cols], [v_ref[:, sl] for sl in cols],
            [bav[:, h:h + 1] for h in hs], [bav[:, h + GDN_HEADS:h + GDN_HEADS + 1] for h in hs],
            [alog_ref[h:h + 1, 0:1] for h in hs], [dtb_ref[h:h + 1, 0:1] for h in hs])


def _gdn_scan_fwd(q, k, v, ba, a_log, dt_bias, *, name):
    length = q.shape[0]
    n = length // GDN_CHUNK
    c = GDN_CHUNK

    def body(q_ref, k_ref, v_ref, ba_ref, alog_ref, dtb_ref, o_ref, s_ref, state):
        i = pl.program_id(0)

        @pl.when(i == 0)
        def _():
            state[...] = jnp.zeros_like(state)

        bav = ba_ref[...]
        heads = [slice(h * HEAD_DIM, (h + 1) * HEAD_DIM) for h in range(GDN_HEADS)]
        s_in = [state[h] for h in range(GDN_HEADS)]
        o, s_out = _gdn_chunk(*_gdn_operands(q_ref, k_ref, v_ref, bav, alog_ref, dtb_ref), s_in)
        for h, sl in enumerate(heads):
            s_ref[h] = s_in[h]
            o_ref[:, sl] = o[h]
            state[h] = s_out[h]

    row_spec = pl.BlockSpec((c, D_MODEL), lambda i: (i, 0))
    small = pl.BlockSpec((GDN_HEADS, LANES), lambda i: (0, 0))
    return pl.pallas_call(
        body, name=name,
        out_shape=(jax.ShapeDtypeStruct((length, D_MODEL), F32),
                   jax.ShapeDtypeStruct((n, GDN_HEADS, HEAD_DIM, HEAD_DIM), F32)),
        grid=(n,),
        in_specs=[row_spec, row_spec, row_spec, pl.BlockSpec((c, LANES), lambda i: (i, 0)), small, small],
        out_specs=(row_spec, pl.BlockSpec((None, GDN_HEADS, HEAD_DIM, HEAD_DIM), lambda i: (i, 0, 0, 0))),
        scratch_shapes=[pltpu.VMEM((GDN_HEADS, HEAD_DIM, HEAD_DIM), F32)],
        compiler_params=_params(("arbitrary",)),
    )(q, k, v, ba, a_log, dt_bias)


def _gdn_scan_bwd(q, k, v, ba, a_log, dt_bias, states, do, *, name):
    length = q.shape[0]
    n = length // GDN_CHUNK
    c = GDN_CHUNK

    def body(q_ref, k_ref, v_ref, ba_ref, alog_ref, dtb_ref, s_ref, do_ref,
             dq_ref, dk_ref, dv_ref, dba_ref, dalog_ref, ddtb_ref, dstate):
        i = pl.program_id(0)

        @pl.when(i == 0)
        def _():
            dstate[...] = jnp.zeros_like(dstate)
            dalog_ref[...] = jnp.zeros_like(dalog_ref)
            ddtb_ref[...] = jnp.zeros_like(ddtb_ref)

        bav = ba_ref[...]
        lane = lax.broadcasted_iota(jnp.int32, (c, LANES), 1)
        sub8 = lax.broadcasted_iota(jnp.int32, (GDN_HEADS, LANES), 0)
        lane8 = lax.broadcasted_iota(jnp.int32, (GDN_HEADS, LANES), 1)
        slab = jnp.zeros((c, LANES), F32)
        dalog_all = jnp.zeros((GDN_HEADS, LANES), F32)
        ddtb_all = jnp.zeros((GDN_HEADS, LANES), F32)
        heads = [slice(h * HEAD_DIM, (h + 1) * HEAD_DIM) for h in range(GDN_HEADS)]
        ds_in = [dstate[h] for h in range(GDN_HEADS)]
        s_in = [s_ref[h] for h in range(GDN_HEADS)]
        _, vjp = jax.vjp(_gdn_chunk, *_gdn_operands(q_ref, k_ref, v_ref, bav, alog_ref, dtb_ref), s_in)
        dq, dk, dv, dbl, dal, dalog, ddtb, ds = vjp(([do_ref[:, sl] for sl in heads], ds_in))
        for h, sl in enumerate(heads):
            dq_ref[:, sl] = dq[h]
            dk_ref[:, sl] = dk[h]
            dv_ref[:, sl] = dv[h]
            dstate[h] = ds[h]
            slab = slab + jnp.where(lane == h, dbl[h], 0.0) + jnp.where(lane == h + GDN_HEADS, dal[h], 0.0)
            here = (sub8 == h) & (lane8 == 0)
            dalog_all = dalog_all + jnp.where(here, dalog[h], 0.0)
            ddtb_all = ddtb_all + jnp.where(here, ddtb[h], 0.0)
        dba_ref[...] = slab
        dalog_ref[...] += dalog_all
        ddtb_ref[...] += ddtb_all

    row_spec = pl.BlockSpec((c, D_MODEL), lambda i: (n - 1 - i, 0))
    small = pl.BlockSpec((GDN_HEADS, LANES), lambda i: (0, 0))
    return pl.pallas_call(
        body, name=name,
        out_shape=(jax.ShapeDtypeStruct((length, D_MODEL), F32),) * 3
        + (jax.ShapeDtypeStruct((length, LANES), F32),
           jax.ShapeDtypeStruct((GDN_HEADS, LANES), F32), jax.ShapeDtypeStruct((GDN_HEADS, LANES), F32)),
        grid=(n,),
        in_specs=[row_spec, row_spec, row_spec,
                  pl.BlockSpec((c, LANES), lambda i: (n - 1 - i, 0)), small, small,
                  pl.BlockSpec((None, GDN_HEADS, HEAD_DIM, HEAD_DIM), lambda i: (n - 1 - i, 0, 0, 0)),
                  row_spec],
        out_specs=(row_spec, row_spec, row_spec,
                   pl.BlockSpec((c, LANES), lambda i: (n - 1 - i, 0)), small, small),
        scratch_shapes=[pltpu.VMEM((GDN_HEADS, HEAD_DIM, HEAD_DIM), F32)],
        compiler_params=_params(("arbitrary",)),
    )(q, k, v, ba, a_log, dt_bias, states, do)


S5_W = S5_T * LANES
S5_S = 2 * 8 * S5_STATE
S5_SH = S5_S // 2


def _iota2(shape):
    return lax.broadcasted_iota(jnp.int32, shape, 0), lax.broadcasted_iota(jnp.int32, shape, 1)


def _s5_rep_t(t, dtype):
    row, col = _iota2((S5_T * S5_GROUP, LANES))
    return ((jnp.right_shift(row, 4) == t) & (jnp.bitwise_and(row, 15) == jnp.bitwise_and(col, 15))).astype(dtype)


def _s5_rep_state(dtype):
    row, col = _iota2((2 * S5_STATE, S5_S))
    return ((jnp.right_shift(row, 6) == jnp.right_shift(col, 9))
            & (jnp.bitwise_and(row, 63) == jnp.bitwise_and(col, 63))).astype(dtype)


def _s5_masks():
    row, col = _iota2((LANES, LANES))
    m_ab = jnp.right_shift(row, 4) == jnp.right_shift(col, 4)
    row, col = _iota2((S5_S, LANES))
    m_e = jnp.bitwise_and(jnp.right_shift(row, 6), 7) == jnp.right_shift(col, 4)
    row, col = _iota2((LANES, S5_S))
    m_f = jnp.right_shift(row, 4) == jnp.bitwise_and(jnp.right_shift(col, 6), 7)
    return m_ab, m_e, m_f


def _s5_expand(kx_ref, ec_ref, fc_ref, kb_scr, e_scr, f_scr):
    m_ab, m_e, m_f = _s5_masks()
    kx = kx_ref[...].astype(BF16)
    ec = ec_ref[...].astype(BF16)
    rep_state = _s5_rep_state(BF16)
    for t in range(S5_T):
        rep = _s5_rep_t(t, BF16)
        cols = slice(t * LANES, (t + 1) * LANES)
        kb_scr[t] = jnp.where(m_ab, jnp.dot(kx, rep, preferred_element_type=F32), 0.0).astype(BF16)
        e_scr[:, cols] = jnp.where(m_e, jnp.dot(ec, rep, preferred_element_type=F32), 0.0).astype(BF16)
        f_scr[cols, :] = jnp.where(m_f, jnp.dot(fc_ref[t].astype(BF16), rep_state, preferred_element_type=F32),
                                   0.0).astype(BF16)


def _s5_token_rows(ref, n):
    return [ref[pl.ds(t, n, stride=S5_T), :].astype(BF16) for t in range(S5_T)]


def _s5_scan_fwd(u, kx, ec, fc, at, *, name):
    length = u.shape[0]
    n = length // S5_T
    assert n % SUBLANES == 0

    def body(u_ref, kx_ref, ec_ref, fc_ref, at_ref, y_ref, h_ref, kb_scr, e_scr, f_scr, g_scr):
        _s5_expand(kx_ref, ec_ref, fc_ref, kb_scr, e_scr, f_scr)
        us = _s5_token_rows(u_ref, n)
        g_scr[...] = jnp.dot(jnp.concatenate(us, axis=1), f_scr[...], preferred_element_type=F32)
        ar, ai = at_ref[:, :S5_SH], at_ref[:, S5_SH:]

        def step(blk, h):
            base = pl.multiple_of(blk * SUBLANES, SUBLANES)
            g8 = g_scr[pl.ds(base, SUBLANES), :]
            rows = []
            for r in range(SUBLANES):
                rows.append(h)
                hr, hi = h[:, :S5_SH], h[:, S5_SH:]
                h = jnp.concatenate([ar * hr - ai * hi, ar * hi + ai * hr], axis=1) + g8[r:r + 1, :]
            h_ref[pl.ds(base, SUBLANES), :] = jnp.concatenate(rows, axis=0)
            return h

        lax.fori_loop(0, n // SUBLANES, step, jnp.zeros((1, S5_S), F32))
        hb = h_ref[...].astype(BF16)
        for t in range(S5_T):
            acc = jnp.dot(hb, e_scr[:, t * LANES:(t + 1) * LANES], preferred_element_type=F32)
            for s in range(t + 1):
                acc = acc + jnp.dot(us[s], kb_scr[t - s], preferred_element_type=F32)
            y_ref[pl.ds(t, n, stride=S5_T), :] = acc

    return pl.pallas_call(
        body, name=name,
        out_shape=(jax.ShapeDtypeStruct((length, D_MODEL), F32), jax.ShapeDtypeStruct((S5_TILES, n, S5_S), F32)),
        grid=(S5_TILES,),
        in_specs=[pl.BlockSpec((length, LANES), lambda k: (0, k)), _s5_spec(LANES, S5_T * S5_GROUP),
                  _s5_spec(S5_S, S5_T * S5_GROUP), _s5_spec(S5_T, LANES, LANES), _s5_spec(1, S5_S)],
        out_specs=(pl.BlockSpec((length, LANES), lambda k: (0, k)), _s5_spec(n, S5_S)),
        scratch_shapes=[pltpu.VMEM((S5_T, LANES, LANES), BF16), pltpu.VMEM((S5_S, S5_W), BF16),
                        pltpu.VMEM((S5_W, S5_S), BF16), pltpu.VMEM((n, S5_S), F32)],
        compiler_params=_params(("parallel",)),
    )(u, kx, ec, fc, at)


def _s5_spec(*tail):
    return pl.BlockSpec((None,) + tail, lambda k: (k,) + (0,) * len(tail))


def _s5_scan_bwd(dy, kx, ec, fc, at, hs, *, name):
    length = dy.shape[0]
    n = length // S5_T

    def body(dy_ref, kx_ref, ec_ref, fc_ref, at_ref, h_ref, du_ref, dg_ref, dat_ref, kb_scr, e_scr, f_scr, dh_scr):
        _s5_expand(kx_ref, ec_ref, fc_ref, kb_scr, e_scr, f_scr)
        dys = _s5_token_rows(dy_ref, n)
        dh_scr[...] = _dot(jnp.concatenate(dys, axis=1), e_scr[...], NT)
        ar, ai = at_ref[:, :S5_SH], at_ref[:, S5_SH:]

        def step(it, carry):
            cy, dat = carry
            base = pl.multiple_of((n // SUBLANES - 1 - it) * SUBLANES, SUBLANES)
            dh8 = dh_scr[pl.ds(base, SUBLANES), :]
            h8 = h_ref[pl.ds(base, SUBLANES), :]
            rows = [None] * SUBLANES
            for r in reversed(range(SUBLANES)):
                rows[r] = cy
                cr, ci = cy[:, :S5_SH], cy[:, S5_SH:]
                hr, hi = h8[r:r + 1, :S5_SH], h8[r:r + 1, S5_SH:]
                dat = dat + jnp.concatenate([cr * hr + ci * hi, ci * hr - cr * hi], axis=1)
                cy = dh8[r:r + 1, :] + jnp.concatenate([ar * cr + ai * ci, ar * ci - ai * cr], axis=1)
            dg_ref[pl.ds(base, SUBLANES), :] = jnp.concatenate(rows, axis=0)
            return cy, dat

        zero = jnp.zeros((1, S5_S), F32)
        _, dat = lax.fori_loop(0, n // SUBLANES, step, (zero, zero))
        dat_ref[...] = dat
        dgb = dg_ref[...].astype(BF16)
        for s in range(S5_T):
            acc = _dot(dgb, f_scr[s * LANES:(s + 1) * LANES, :], NT)
            for t in range(s, S5_T):
                acc = acc + _dot(dys[t], kb_scr[t - s], NT)
            du_ref[pl.ds(s, n, stride=S5_T), :] = acc

    row_spec = pl.BlockSpec((length, LANES), lambda k: (0, k))
    return pl.pallas_call(
        body, name=name,
        out_shape=(jax.ShapeDtypeStruct((length, D_MODEL), F32), jax.ShapeDtypeStruct((S5_TILES, n, S5_S), F32),
                   jax.ShapeDtypeStruct((S5_TILES, 1, S5_S), F32)),
        grid=(S5_TILES,),
        in_specs=[row_spec, _s5_spec(LANES, S5_T * S5_GROUP), _s5_spec(S5_S, S5_T * S5_GROUP),
                  _s5_spec(S5_T, LANES, LANES), _s5_spec(1, S5_S), _s5_spec(n, S5_S)],
        out_specs=(row_spec, _s5_spec(n, S5_S), _s5_spec(1, S5_S)),
        scratch_shapes=[pltpu.VMEM((S5_T, LANES, LANES), BF16), pltpu.VMEM((S5_S, S5_W), BF16),
                        pltpu.VMEM((S5_W, S5_S), BF16), pltpu.VMEM((n, S5_S), F32)],
        compiler_params=_params(("parallel",)),
    )(dy, kx, ec, fc, at, hs)


def _s5_operator_grads(dy, u, hs, dg, *, name):
    length = u.shape[0]
    n = length // S5_T

    def body(dy_ref, u_ref, h_ref, dg_ref, dkx_ref, dec_ref, dfc_ref):
        dys = _s5_token_rows(dy_ref, n)
        us = _s5_token_rows(u_ref, n)
        m_ab, m_e, m_f = _s5_masks()
        hb = h_ref[...].astype(BF16)
        dgb = dg_ref[...].astype(BF16)
        rep_state = _s5_rep_state(F32)
        dkx = jnp.zeros((LANES, S5_T * S5_GROUP), F32)
        dec = jnp.zeros((S5_S, S5_T * S5_GROUP), F32)
        for t in range(S5_T):
            rep = _s5_rep_t(t, F32)
            dkb = _dot(us[0], dys[t], TN)
            for s in range(1, S5_T - t):
                dkb = dkb + _dot(us[s], dys[s + t], TN)
            dkx = dkx + _dot(jnp.where(m_ab, dkb, 0.0), rep, NT, HI)
            dec = dec + _dot(jnp.where(m_e, _dot(hb, dys[t], TN), 0.0), rep, NT, HI)
            dfc_ref[t] = _dot(jnp.where(m_f, _dot(us[t], dgb, TN), 0.0), rep_state, NT, HI)
        dkx_ref[...] = dkx
        dec_ref[...] = dec

    row_spec = pl.BlockSpec((length, LANES), lambda k: (0, k))
    outs = (_s5_spec(LANES, S5_T * S5_GROUP), _s5_spec(S5_S, S5_T * S5_GROUP), _s5_spec(S5_T, LANES, LANES))
    return pl.pallas_call(
        body, name=name,
        out_shape=(jax.ShapeDtypeStruct((S5_TILES, LANES, S5_T * S5_GROUP), F32),
                   jax.ShapeDtypeStruct((S5_TILES, S5_S, S5_T * S5_GROUP), F32),
                   jax.ShapeDtypeStruct((S5_TILES, S5_T, LANES, LANES), F32)),
        grid=(S5_TILES,),
        in_specs=[row_spec, row_spec, _s5_spec(n, S5_S), _s5_spec(n, S5_S)],
        out_specs=outs,
        compiler_params=_params(("parallel",)),
    )(dy, u, hs, dg)


def _s5_prep(a_re, a_im, b_re, b_im, c_re, c_im, log_dt):
    t_len, tiles = S5_T, S5_TILES
    dt = jnp.exp(log_dt)[:, None]
    mag = jnp.exp(a_re * dt)
    ab_re, ab_im = mag * jnp.cos(a_im * dt), mag * jnp.sin(a_im * dt)
    den = jnp.square(a_re) + jnp.square(a_im)
    n_re, n_im = ab_re - 1.0, ab_im
    f_re = (n_re * a_re + n_im * a_im) / den
    f_im = (n_im * a_re - n_re * a_im) / den
    bb_re = f_re[..., None] * b_re - f_im[..., None] * b_im
    bb_im = f_re[..., None] * b_im + f_im[..., None] * b_re
    p_re, p_im = [jnp.ones_like(ab_re)], [jnp.zeros_like(ab_re)]
    for _ in range(t_len):
        p_re, p_im = (p_re + [p_re[-1] * ab_re - p_im[-1] * ab_im],
                      p_im + [p_re[-1] * ab_im + p_im[-1] * ab_re])
    rev_re, rev_im = jnp.stack(p_re[t_len - 1::-1]), jnp.stack(p_im[t_len - 1::-1])
    p_re, p_im = jnp.stack(p_re), jnp.stack(p_im)
    ca_re = c_re[None] * p_re[:, :, None, :] - c_im[None] * p_im[:, :, None, :]
    ca_im = c_re[None] * p_im[:, :, None, :] + c_im[None] * p_re[:, :, None, :]
    lag = (jnp.einsum('tgip,gpj->tgij', ca_re[:t_len], bb_re, precision=HI)
           - jnp.einsum('tgip,gpj->tgij', ca_im[:t_len], bb_im, precision=HI))
    kx = lag.reshape(t_len, tiles, 8, S5_GROUP, S5_GROUP).transpose(1, 2, 4, 0, 3)
    kx = kx.reshape(tiles, LANES, t_len * S5_GROUP)
    e_st = jnp.stack([ca_re[1:], -ca_im[1:]])
    e_st = e_st.reshape(2, t_len, tiles, 8, S5_GROUP, S5_STATE).transpose(2, 0, 3, 5, 1, 4)
    ec = e_st.reshape(tiles, S5_S, t_len * S5_GROUP)
    ab_b = jnp.stack([rev_re[..., None] * bb_re[None] - rev_im[..., None] * bb_im[None],
                      rev_re[..., None] * bb_im[None] + rev_im[..., None] * bb_re[None]])
    ab_b = ab_b.reshape(2, t_len, tiles, 8, S5_STATE, S5_GROUP).transpose(2, 1, 3, 5, 0, 4)
    fc = ab_b.reshape(tiles, t_len, LANES, 2 * S5_STATE)
    a_t = jnp.stack([p_re[t_len], p_im[t_len]]).reshape(2, tiles, 8 * S5_STATE).transpose(1, 0, 2)
    return kx, ec, fc, a_t.reshape(tiles, 1, S5_S)


TM_ROW = 256


def _gdn_fwd(x, w, tag):
    qkv = _mm(x, w["wqkv"], name="gdn_proj_qkv")
    z = _mm(x, w["wz"], name="gdn_proj_z")
    ba = _mm(x, w["wba"], name="gdn_proj_ba")
    cv = _conv_fwd(qkv, w["conv_w"], tm=TM_ROW, name="gdn_conv")
    q, k, v = _rw_fwd(_f_gdn_qkv, [cv], [], tm=TM_ROW, name="gdn_qkv")
    o, states = _gdn_scan_fwd(q, k, v, ba, w["a_log8"], w["dt_bias8"], name="gdn_scan")
    (mix,) = _rw_fwd(_f_gdn_out, [o, z], [w["norm_g"]], tm=TM_ROW, name="gdn_out")
    return mix, (qkv, z, ba, cv, q, k, v, states, o)


def _gdn_bwd(x, w, saved, dmix, dx_acc):
    qkv, z, ba, cv, q, k, v, states, o = saved
    (do, dz), (dnorm_g,) = _rw_bwd(_f_gdn_out, [o, z], [w["norm_g"]], [dmix], row_grad=[1, 1], param_grad=[1],
                                   tm=TM_ROW, name="gdn_out_bwd")
    dq, dk, dv, dba, dalog, ddtb = _gdn_scan_bwd(q, k, v, ba, w["a_log8"], w["dt_bias8"], states, do,
                                                  name="gdn_scan_bwd")
    (dcv,), _ = _rw_bwd(_f_gdn_qkv, [cv], [], [dq, dk, dv], row_grad=[1], param_grad=[], tm=TM_ROW,
                        name="gdn_qkv_bwd")
    dqkv, dconv_w = _conv_bwd(qkv, w["conv_w"], dcv, tm=TM_ROW, name="gdn_conv_bwd")
    dx = _mm(dqkv, w["wqkv"], tb=True, acc=dx_acc, name="gdn_dx_qkv")
    dx = _mm(dz, w["wz"], tb=True, acc=dx, name="gdn_dx_z")
    dx = _mm(dba, w["wba"], tb=True, acc=dx, name="gdn_dx_ba")
    grads = dict(wqkv=_mm(x, dqkv, ta=True, name="gdn_dw_qkv"), wz=_mm(x, dz, ta=True, name="gdn_dw_z"),
                 wba=_mm(x, dba, ta=True, name="gdn_dw_ba"), conv_w=dconv_w,
                 a_log=dalog[:, 0], dt_bias=ddtb[:, 0], norm_g=dnorm_g[0])
    return dx, grads


def _s5_fwd(x, w, tag):
    u = _mm(x, w["wu"], name="s5_proj_u")
    y, hs = _s5_scan_fwd(u, w["kx"], w["ec"], w["fc"], w["a_t"], name="s5_scan")
    (zg,) = _rw_fwd(_f_s5_gelu, [y, u], [w["d"]], tm=TM_ROW, name="s5_gelu")
    t = _mm(zg, w["w_glu"], name="s5_glu")
    (mix,) = _rw_fwd(_f_s5_gate, [zg, t], [w["b_glu"]], tm=TM_ROW, name="s5_gate")
    return mix, (u, hs, y, zg, t)


def _s5_bwd(x, w, saved, dmix, dx_acc):
    u, hs, y, zg, t = saved
    (dzg, dt), (db_glu,) = _rw_bwd(_f_s5_gate, [zg, t], [w["b_glu"]], [dmix], row_grad=[1, 1], param_grad=[1],
                                   tm=TM_ROW, name="s5_gate_bwd")
    dzg = _mm(dt, w["w_glu"], tb=True, acc=dzg, name="s5_dzg")
    dw_glu = _mm(zg, dt, ta=True, name="s5_dw_glu")
    (dy, du), (dd,) = _rw_bwd(_f_s5_gelu, [y, u], [w["d"]], [dzg], row_grad=[1, 1], param_grad=[1],
                              tm=TM_ROW, name="s5_gelu_bwd")
    du_scan, dg, dat = _s5_scan_bwd(dy, w["kx"], w["ec"], w["fc"], w["a_t"], hs, name="s5_scan_bwd")
    dkx, dec, dfc = _s5_operator_grads(dy, u, hs, dg, name="s5_operator_grads")
    (du,) = _rw_fwd(_f_add, [du, du_scan], [], tm=TM_ROW, name="s5_du_add")
    d_a_re, d_a_im, d_b_re, d_b_im, d_c_re, d_c_im, d_log_dt = w["prep_vjp"]((dkx, dec, dfc, dat))
    dx = _mm(du, w["wu"], tb=True, acc=dx_acc, name="s5_dx_u")
    grads = dict(wu=_mm(x, du, ta=True, name="s5_dw_u"), w_glu=dw_glu, b_glu=db_glu[0], d=dd[0],
                 a_re=d_a_re, a_im=d_a_im, b_re=d_b_re, b_im=d_b_im, c_re=d_c_re, c_im=d_c_im, log_dt=d_log_dt)
    return dx, grads


def _layer_fwd(x, mem, w, is_gdn):
    mix, msave = (_gdn_fwd if is_gdn else _s5_fwd)(x, w, "")
    xq = _mm(x, w["wxq"], name="proj_xq")
    kv = _mm(mem, w["wkv"], name="mem_kv")
    kmem, vmem = kv[:, :XA_DIM], kv[:, XA_DIM:]
    (cross,) = _rw_fwd(_f_attn, [xq], [kmem, vmem], tm=TM_ROW, name="attn")
    h = _mm(mix, w["wo_mix"], name="wo_mix")
    h = _mm(cross, w["wo_cross"], acc=h, name="wo_cross")
    (x1,) = _rw_fwd(_f_ln_res, [x, h], [w["ln1_g"], w["ln1_b"]], tm=TM_ROW, name="ln_res")
    hm, act = _mm_relu2(x1, w["w1"], name="mlp_up")
    f = _mm(act, w["w2"], name="mlp_down")
    (x2,) = _rw_fwd(_f_ln_res, [x1, f], [w["ln2_g"], w["ln2_b"]], tm=TM_ROW, name="ln_res")
    return x2, (x, msave, xq, kmem, vmem, mix, cross, h, x1, hm, act, f)


def _layer_bwd(mem, w, is_gdn, saved, dx2):
    x, msave, xq, kmem, vmem, mix, cross, h, x1, hm, act, f = saved
    (dx1, df), (dg2, db2) = _rw_bwd(_f_ln_res, [x1, f], [w["ln2_g"], w["ln2_b"]], [dx2], row_grad=[1, 1],
                                    param_grad=[1, 1], tm=TM_ROW, name="ln_res_bwd")
    dhm = _mm_relu2_grad(df, w["w2"], hm, name="mlp_dhm")
    dw2 = _mm(act, df, ta=True, name="mlp_dw2")
    dx1 = _mm(dhm, w["w1"], tb=True, acc=dx1, name="mlp_dx")
    dw1 = _mm(x1, dhm, ta=True, name="mlp_dw1")
    (dx, dh), (dg1, db1) = _rw_bwd(_f_ln_res, [x, h], [w["ln1_g"], w["ln1_b"]], [dx1], row_grad=[1, 1],
                                   param_grad=[1, 1], tm=TM_ROW, name="ln_res_bwd")
    dmix = _mm(dh, w["wo_mix"], tb=True, name="wo_dmix")
    dcross = _mm(dh, w["wo_cross"], tb=True, name="wo_dcross")
    dwo = jnp.concatenate([_mm(mix, dh, ta=True, name="wo_dw_mix"), _mm(cross, dh, ta=True, name="wo_dw_cross")], 0)
    (dxq,), (dkmem, dvmem) = _rw_bwd(_f_attn, [xq], [kmem, vmem], [dcross], row_grad=[1], param_grad=[1, 1],
                                     tm=TM_ROW, name="attn_bwd")
    dwkv = _mm(mem, jnp.concatenate([dkmem, dvmem], axis=1), ta=True, name="mem_dw_kv")
    dx = _mm(dxq, w["wxq"], tb=True, acc=dx, name="dx_xq")
    dwxq = _mm(x, dxq, ta=True, name="dw_xq")
    dx, mg = (_gdn_bwd if is_gdn else _s5_bwd)(x, w, msave, dmix, dx)
    grads = dict(mixer=mg, wxq=dwxq, wkv=dwkv, wo=dwo, w1=dw1, w2=dw2,
                 ln1_g=dg1[0], ln1_b=db1[0], ln2_g=dg2[0], ln2_b=db2[0])
    return dx, grads


def _loss_and_grad(y, target):
    def f(yv, tv):
        err = yv - tv
        return (err * (1.0 / D_MODEL),), (0.5 / D_MODEL * jnp.sum(err * err, axis=0, keepdims=True),)

    (dy,), (part,) = _rowwise(f, [y, target], [], [(D_MODEL, F32)], [((1, D_MODEL), F32)], tm=512, name="loss")
    return jnp.sum(part), dy


def _layer_weights(full, i):
    j = i // 2
    w = dict(wkv=full["w_kv_mem"][i].astype(BF16),
             wo_mix=full["w_o"][i, :D_MODEL].astype(BF16), wo_cross=full["w_o"][i, D_MODEL:].astype(BF16),
             ln1_g=full["ln1_g"][i][None], ln1_b=full["ln1_b"][i][None],
             ln2_g=full["ln2_g"][i][None], ln2_b=full["ln2_b"][i][None],
             w1=full["mlp_w1"][i].astype(BF16), w2=full["mlp_w2"][i].astype(BF16))
    if i % 2 == 0:
        w_in = full["gdn_w_in"][j]
        gd = 3 * D_MODEL
        w.update(wqkv=w_in[:, :gd].astype(BF16), wz=w_in[:, gd:gd + D_MODEL].astype(BF16),
                 wba=jnp.pad(w_in[:, gd + D_MODEL:gd + D_MODEL + 2 * GDN_HEADS],
                             ((0, 0), (0, LANES - 2 * GDN_HEADS))).astype(BF16),
                 wxq=w_in[:, gd + D_MODEL + 2 * GDN_HEADS:].astype(BF16),
                 conv_w=full["gdn_conv_w"][j],
                 a_log8=jnp.broadcast_to(full["gdn_a_log"][j][:, None], (GDN_HEADS, LANES)),
                 dt_bias8=jnp.broadcast_to(full["gdn_dt_bias"][j][:, None], (GDN_HEADS, LANES)),
                 norm_g=full["gdn_norm_g"][j][None])
    else:
        w_in = full["s5_w_in"][j]
        (kx, ec, fc, a_t), prep_vjp = jax.vjp(
            _s5_prep, full["s5_a_re"][j], full["s5_a_im"][j], full["s5_b_re"][j], full["s5_b_im"][j],
            full["s5_c_re"][j], full["s5_c_im"][j], full["s5_log_dt"][j])
        w.update(wu=w_in[:, :D_MODEL].astype(BF16), wxq=w_in[:, D_MODEL:].astype(BF16),
                 kx=kx, ec=ec, fc=fc, a_t=a_t, prep_vjp=prep_vjp,
                 d=full["s5_d"][j][None], w_glu=full["s5_w_glu"][j].astype(BF16), b_glu=full["s5_b_glu"][j][None])
    return w


def _full_grads(layer_grads):
    g = layer_grads
    gdn = [g[i] for i in range(DEPTH) if i % 2 == 0]
    s5 = [g[i] for i in range(DEPTH) if i % 2 == 1]
    out = dict(
        w_kv_mem=jnp.stack([l["wkv"] for l in g]), w_o=jnp.stack([l["wo"] for l in g]),
        ln1_g=jnp.stack([l["ln1_g"] for l in g]), ln1_b=jnp.stack([l["ln1_b"] for l in g]),
        ln2_g=jnp.stack([l["ln2_g"] for l in g]), ln2_b=jnp.stack([l["ln2_b"] for l in g]),
        mlp_w1=jnp.stack([l["w1"] for l in g]), mlp_w2=jnp.stack([l["w2"] for l in g]),
        gdn_w_in=jnp.stack([jnp.concatenate([l["mixer"]["wqkv"], l["mixer"]["wz"],
                                             l["mixer"]["wba"][:, :2 * GDN_HEADS], l["wxq"]], axis=1) for l in gdn]),
        gdn_conv_w=jnp.stack([l["mixer"]["conv_w"] for l in gdn]),
        gdn_a_log=jnp.stack([l["mixer"]["a_log"] for l in gdn]),
        gdn_dt_bias=jnp.stack([l["mixer"]["dt_bias"] for l in gdn]),
        gdn_norm_g=jnp.stack([l["mixer"]["norm_g"] for l in gdn]),
        s5_w_in=jnp.stack([jnp.concatenate([l["mixer"]["wu"], l["wxq"]], axis=1) for l in s5]),
        s5_d=jnp.stack([l["mixer"]["d"] for l in s5]),
        s5_w_glu=jnp.stack([l["mixer"]["w_glu"] for l in s5]),
        s5_b_glu=jnp.stack([l["mixer"]["b_glu"] for l in s5]))
    for n in ("a_re", "a_im", "b_re", "b_im", "c_re", "c_im", "log_dt"):
        out["s5_" + n] = jnp.stack([l["mixer"][n] for l in s5])
    return out


def _local_step(x, mem, target, full):
    lw = [_layer_weights(full, i) for i in range(DEPTH)]
    saves = []
    h = x
    for i in range(DEPTH):
        h, s = _layer_fwd(h, mem, lw[i], i % 2 == 0)
        saves.append(s)
    loss, d = _loss_and_grad(h, target)
    grads = [None] * DEPTH
    for i in reversed(range(DEPTH)):
        d, grads[i] = _layer_bwd(mem, lw[i], i % 2 == 0, saves[i], d)
    return loss, d, _full_grads(grads)


ANY = pl.BlockSpec(memory_space=pl.ANY)
SHARD_ROWS = 1024
SMALL_ROWS = 128


def _place():
    return lax.axis_index("x"), lax.axis_index("y"), lax.axis_index("c")


def _other_chips(x, y):
    return [(1 - x, y), (x, 1 - y), (1 - x, 1 - y)]


def _all_gather_chips(wpack, *, name):
    rows = wpack.shape[0]
    half = rows // 2

    def body(w_ref, out_ref, send_sems, recv_sems, local_sem):
        x, y, c = _place()
        sibling = (x, y, 1 - c)
        chips = _other_chips(x, y)

        def blk(cx, cy, cc):
            return out_ref.at[2 * cx + cy, pl.ds(cc * half, half), :]

        def copy(k, src, dst, to):
            return pltpu.make_async_remote_copy(src_ref=src, dst_ref=dst, send_sem=send_sems.at[k],
                                                recv_sem=recv_sems.at[k], device_id=to, device_id_type=MESH)

        mine = pltpu.make_async_copy(w_ref, out_ref.at[2 * x + y], local_sem)
        mine.start()
        first = [copy(j, w_ref.at[pl.ds(c * half, half), :], blk(x, y, c), (cx, cy, c))
                 for j, (cx, cy) in enumerate(chips)]
        for cp in first:
            cp.start()
        passed = [copy(3 + j, blk(cx, cy, c), blk(cx, cy, c), sibling) for j, (cx, cy) in enumerate(chips)]
        for j, (cx, cy) in enumerate(chips):
            copy(j, blk(cx, cy, c), blk(cx, cy, c), (cx, cy, c)).wait_recv()
            passed[j].start()
        for j, (cx, cy) in enumerate(chips):
            copy(3 + j, blk(cx, cy, 1 - c), blk(cx, cy, 1 - c), sibling).wait_recv()
        for cp in first + passed:
            cp.wait_send()
        mine.wait()

    return pl.pallas_call(
        body, name=name, out_shape=jax.ShapeDtypeStruct((N_CHIPS, rows, D_MODEL), wpack.dtype),
        in_specs=[ANY], out_specs=ANY,
        scratch_shapes=[pltpu.SemaphoreType.DMA((6,)), pltpu.SemaphoreType.DMA((6,)), pltpu.SemaphoreType.DMA],
    )(wpack)


def _sibling_swap(buf, *, name):
    def body(in_ref, out_ref, send_sem, recv_sem):
        x, y, c = _place()
        cp = pltpu.make_async_remote_copy(src_ref=in_ref, dst_ref=out_ref, send_sem=send_sem, recv_sem=recv_sem,
                                          device_id=(x, y, 1 - c), device_id_type=MESH)
        cp.start()
        cp.wait()

    return pl.pallas_call(
        body, name=name, out_shape=jax.ShapeDtypeStruct(buf.shape, buf.dtype), in_specs=[ANY], out_specs=ANY,
        scratch_shapes=[pltpu.SemaphoreType.DMA, pltpu.SemaphoreType.DMA],
    )(buf)


def _pair_exchange(gpack, *, name):
    pieces, rows, width = gpack.shape
    half = rows // 2

    def body(in_ref, own_ref, got_ref, send_sem, recv_sem, local_sem):
        x, y, c = _place()
        keep = pltpu.make_async_copy(in_ref.at[:, pl.ds(c * half, half), :], own_ref, local_sem)
        keep.start()
        cp = pltpu.make_async_remote_copy(src_ref=in_ref.at[:, pl.ds((1 - c) * half, half), :], dst_ref=got_ref,
                                          send_sem=send_sem, recv_sem=recv_sem,
                                          device_id=(x, y, 1 - c), device_id_type=MESH)
        cp.start()
        cp.wait()
        keep.wait()

    shape = jax.ShapeDtypeStruct((pieces, half, width), gpack.dtype)
    return pl.pallas_call(
        body, name=name, out_shape=(shape, shape), in_specs=[ANY], out_specs=(ANY, ANY),
        scratch_shapes=[pltpu.SemaphoreType.DMA, pltpu.SemaphoreType.DMA, pltpu.SemaphoreType.DMA],
    )(gpack)


def _chip_exchange(pieces, *, name):
    _, rows, width = pieces.shape

    def body(in_ref, out_ref, send_sems, recv_sems):
        x, y, c = _place()
        cps = [pltpu.make_async_remote_copy(src_ref=in_ref.at[2 * cx + cy], dst_ref=out_ref.at[j],
                                            send_sem=send_sems.at[j], recv_sem=recv_sems.at[j],
                                            device_id=(cx, cy, c), device_id_type=MESH)
               for j, (cx, cy) in enumerate(_other_chips(x, y))]
        for cp in cps:
            cp.start()
        for cp in cps:
            cp.wait()

    return pl.pallas_call(
        body, name=name, out_shape=jax.ShapeDtypeStruct((3, rows, width), pieces.dtype), in_specs=[ANY], out_specs=ANY,
        scratch_shapes=[pltpu.SemaphoreType.DMA((3,)), pltpu.SemaphoreType.DMA((3,))],
    )(pieces)


def _all_reduce_small(v, *, name):
    rows, width = v.shape

    def body(in_ref, out_ref, gath, send_sems, recv_sems):
        x, y, c = _place()
        me = 4 * x + 2 * y + c
        gath[me] = in_ref[...]
        peers = []
        for m in range(1, N_DEV):
            px = 1 - x if m & 4 else x
            py = 1 - y if m & 2 else y
            pc = 1 - c if m & 1 else c
            peers.append((m - 1, (px, py, pc), 4 * px + 2 * py + pc))
        for k, peer, _ in peers:
            pltpu.make_async_remote_copy(src_ref=in_ref, dst_ref=gath.at[me], send_sem=send_sems.at[k],
                                         recv_sem=recv_sems.at[k], device_id=peer, device_id_type=MESH).start()
        for k, peer, plin in peers:
            cp = pltpu.make_async_remote_copy(src_ref=in_ref, dst_ref=gath.at[plin], send_sem=send_sems.at[k],
                                              recv_sem=recv_sems.at[k], device_id=peer, device_id_type=MESH)
            cp.wait_send()
            cp.wait_recv()
        acc = gath[0]
        for d in range(1, N_DEV):
            acc = acc + gath[d]
        out_ref[...] = acc

    vmem = pl.BlockSpec(memory_space=pltpu.VMEM)
    return pl.pallas_call(
        body, name=name, out_shape=jax.ShapeDtypeStruct(v.shape, v.dtype), in_specs=[vmem], out_specs=vmem,
        scratch_shapes=[pltpu.VMEM((N_DEV, rows, width), v.dtype),
                        pltpu.SemaphoreType.DMA((N_DEV - 1,)), pltpu.SemaphoreType.DMA((N_DEV - 1,))],
        compiler_params=pltpu.CompilerParams(vmem_limit_bytes=VMEM_LIMIT_V7X),
    )(v)


def _reduce_scatter(gpack):
    x, y, c = _place()
    half = gpack.shape[1] // 2
    own, got = _pair_exchange(gpack, name="rs_pair_swap")
    pair, pair16 = _rw_fwd(_f_add_and_narrow, [own.reshape(N_CHIPS * half, D_MODEL),
                                               got.reshape(N_CHIPS * half, D_MODEL)], [], tm=512, name="rs_pair_add")
    pair = pair.reshape(N_CHIPS, half, D_MODEL)
    recv = _chip_exchange(pair16.reshape(N_CHIPS, half, D_MODEL), name="rs_chip_exchange")
    mine = lax.dynamic_index_in_dim(pair, 2 * x + y, axis=0, keepdims=False)
    (total,) = _rw_fwd(_f_add4, [mine, recv[0], recv[1], recv[2]], [], tm=512, name="rs_chip_add")
    theirs = _sibling_swap(total, name="rs_share_swap")
    return jnp.concatenate([jnp.where(c == 0, total, theirs), jnp.where(c == 0, theirs, total)], axis=0)


_SHARDED = (("w_kv_mem", 1), ("w_o", 1), ("mlp_w1", 2), ("mlp_w2", 1), ("gdn_w_in", 2), ("gdn_conv_w", 2),
            ("s5_w_in", 2), ("s5_d", 1), ("s5_w_glu", 1), ("s5_b_glu", 1))
_MATMUL_ONLY = ("w_kv_mem", "w_o", "mlp_w1", "mlp_w2", "gdn_w_in", "s5_w_in", "s5_w_glu")
_REPLICATED = ("ln1_g", "ln1_b", "ln2_g", "ln2_b", "gdn_a_log", "gdn_dt_bias", "gdn_norm_g", "s5_a_re", "s5_a_im",
               "s5_b_re", "s5_b_im", "s5_c_re", "s5_c_im", "s5_log_dt")
_WEIGHTS = ("w_kv_mem", "w_o", "ln1_g", "ln1_b", "ln2_g", "ln2_b", "mlp_w1", "mlp_w2", "gdn_w_in", "gdn_conv_w",
            "gdn_a_log", "gdn_dt_bias", "gdn_norm_g", "s5_w_in", "s5_a_re", "s5_a_im", "s5_b_re", "s5_b_im",
            "s5_c_re", "s5_c_im", "s5_log_dt", "s5_d", "s5_w_glu", "s5_b_glu")


def _pack(arrs, lead=(), unit_rows=SHARD_ROWS):
    nl = len(lead)
    flat = jnp.concatenate([a.reshape(lead + (-1,)) for a in arrs], axis=nl)
    unit = unit_rows * D_MODEL
    pad = -flat.shape[nl] % unit
    flat = jnp.pad(flat, ((0, 0),) * nl + ((0, pad),))
    return flat.reshape(lead + (-1, D_MODEL))


def _unpack(packed, shapes, lead=()):
    nl = len(lead)
    flat = packed.reshape(lead + (-1,))
    out, off = [], 0
    for s in shapes:
        n = math.prod(s)
        out.append(lax.slice_in_dim(flat, off, off + n, axis=nl).reshape(lead + tuple(s)))
        off += n
    return out


def _split3(t):
    hi = t.astype(BF16)
    r1 = t - hi.astype(F32)
    mid = r1.astype(BF16)
    lo = (r1 - mid.astype(F32)).astype(BF16)
    return jnp.stack([hi, mid, lo], axis=-1)


def _join3(t):
    return (t[..., 0].astype(F32) + t[..., 1].astype(F32)) + t[..., 2].astype(F32)


def _merge_chips(blocks, axis):
    return jnp.concatenate([blocks[s] for s in range(N_CHIPS)], axis=axis)


def _pack_for_chips(arrs):
    per_chip = sum(math.prod(a.shape) for a, _ in arrs) // N_CHIPS
    pad = -per_chip % (SHARD_ROWS * D_MODEL)
    pieces = []
    for s in range(N_CHIPS):
        for a, axis in arrs:
            n = a.shape[axis] // N_CHIPS
            pieces.append(lax.slice_in_dim(a, s * n, (s + 1) * n, axis=axis).reshape(-1))
        pieces.append(jnp.zeros((pad,), F32))
    return jnp.concatenate(pieces).reshape(N_CHIPS, -1, D_MODEL)


def kernel(x, mem, w_kv_mem, w_o, ln1_g, ln1_b, ln2_g, ln2_b, mlp_w1, mlp_w2, gdn_w_in, gdn_conv_w, gdn_a_log, gdn_dt_bias, gdn_norm_g, s5_w_in, s5_a_re, s5_a_im, s5_b_re, s5_b_im, s5_c_re, s5_c_im, s5_log_dt, s5_d, s5_w_glu, s5_b_glu, loss_target, m_w_kv_mem, m_w_o, m_ln1_g, m_ln1_b, m_ln2_g, m_ln2_b, m_mlp_w1, m_mlp_w2, m_gdn_w_in, m_gdn_conv_w, m_gdn_a_log, m_gdn_dt_bias, m_gdn_norm_g, m_s5_w_in, m_s5_a_re, m_s5_a_im, m_s5_b_re, m_s5_b_im, m_s5_c_re, m_s5_c_im, m_s5_log_dt, m_s5_d, m_s5_w_glu, m_s5_b_glu, v_w_kv_mem, v_w_o, v_ln1_g, v_ln1_b, v_ln2_g, v_ln2_b, v_mlp_w1, v_mlp_w2, v_gdn_w_in, v_gdn_conv_w, v_gdn_a_log, v_gdn_dt_bias, v_gdn_norm_g, v_s5_w_in, v_s5_a_re, v_s5_a_im, v_s5_b_re, v_s5_b_im, v_s5_c_re, v_s5_c_im, v_s5_log_dt, v_s5_d, v_s5_w_glu, v_s5_b_glu):
    given = dict(locals())
    w = {n: given[n] for n in _WEIGHTS}
    mom = {n: given["m_" + n] for n in _WEIGHTS}
    var = {n: given["v_" + n] for n in _WEIGHTS}
    shard_names = [n for n, _ in _SHARDED]
    shard_shapes = [w[n].shape for n in shard_names]
    rep_shapes = [w[n].shape for n in _REPLICATED]

    wire = [w[n].astype(BF16) if n in _MATMUL_ONLY else _split3(w[n]) for n in shard_names]
    gathered = _all_gather_chips(_pack(wire), name="gather_weights")
    blocks = _unpack(gathered, [a.shape for a in wire], lead=(N_CHIPS,))
    full = {n: _merge_chips(blk if n in _MATMUL_ONLY else _join3(blk), ax) for (n, ax), blk in zip(_SHARDED, blocks)}
    full.update({n: w[n] for n in _REPLICATED})

    loss, grad_x, grads = _local_step(x[0], mem[0], loss_target[0], full)
    loss = lax.psum(loss, ("x", "y", "c"))

    g_shard = _reduce_scatter(_pack_for_chips([(grads[n], ax) for n, ax in _SHARDED]))
    def pack_small(d):
        return _pack([d[n] for n in _REPLICATED], unit_rows=SMALL_ROWS)

    g_rep = _all_reduce_small(pack_small(grads), name="reduce_replicated")

    def adamw(wp, gp, mp, vp, name):
        return _rw_fwd(_f_adamw, [wp, gp, mp, vp], [], tm=256, name=name)

    outs = {}
    for n, g in zip(shard_names, _unpack(g_shard, shard_shapes)):
        flat = (-1, w[n].shape[-1])
        res = adamw(w[n].reshape(flat), g.reshape(flat), mom[n].reshape(flat), var[n].reshape(flat), "adamw_" + n)
        outs[("grad", n)] = g
        outs.update({(kind, n): a.reshape(w[n].shape) for kind, a in zip(("delta", "new_m", "new_v"), res)})
    packed = (g_rep,) + tuple(adamw(pack_small(w), g_rep, pack_small(mom), pack_small(var), "adamw_replicated"))
    for kind, pr in zip(("grad", "delta", "new_m", "new_v"), packed):
        outs.update({(kind, n): a for n, a in zip(_REPLICATED, _unpack(pr, rep_shapes))})
    return (loss, grad_x[None]) + tuple(outs[(kind, n)] for kind in ("grad", "delta", "new_m", "new_v")
                                        for n in _WEIGHTS)
```

```python
import functools
import math

import jax
import jax.numpy as jnp
from jax import lax
from jax.experimental import pallas as pl
from jax.experimental.pallas import tpu as pltpu

F32 = jnp.float32
BF16 = jnp.bfloat16
MESH = pl.DeviceIdType.MESH

D_MODEL = 1024
DEPTH = 4
GDN_HEADS = 8
HEAD_DIM = 128
GDN_CONV = 4
GDN_CHUNK = 64
S5_GROUPS = 64
S5_GROUP = 16
S5_STATE = 64
XA_HEADS = 4
XA_DIM = 512
D_FF = 4096
DN_ALPHA = (2 * DEPTH) ** 0.25
LN_EPS = 1e-5
RMS_EPS = 1e-6
ADAM_LR, ADAM_B1, ADAM_B2, ADAM_EPS, ADAM_WD, ADAM_STEP = 0.001, 0.9, 0.999, 1e-08, 0.01, 10

VMEM_LIMIT_V7X = 56 * 1024 * 1024
LANES = 128
SUBLANES = 8
S5_T = 16
S5_TILES = D_MODEL // LANES
N_CHIPS = 4
N_DEV = 8


def _params(sem):
    return pltpu.CompilerParams(dimension_semantics=sem, vmem_limit_bytes=VMEM_LIMIT_V7X)


def _tile(n, pref):
    if n <= pref:
        return n
    t = (pref // LANES) * LANES
    while n % t:
        t -= LANES
    return t


def _row_tile(n, pref):
    if n % SUBLANES:
        return n
    t = min(pref, n) // SUBLANES * SUBLANES
    while n % t:
        t -= SUBLANES
    return t


def _col_blocked_spec(rows_tile, cols_tile, block_cols, rows_axis, cols_axis):
    r = block_cols // cols_tile

    def index(*ijk):
        c = ijk[cols_axis]
        return (c, ijk[rows_axis], 0) if r == 1 else (c // r, ijk[rows_axis], c % r)

    return pl.BlockSpec((None, rows_tile, cols_tile), index)


def _mm(a, b, *, ta=False, tb=False, acc=None, name, tm=1024, tn=1024, tk=1024, out_blocks=0):
    k_dim, m_dim = a.shape if ta else a.shape[::-1]
    b_rows, b_cols = (b.shape[0], b.shape[1]) if b.ndim == 2 else (b.shape[1], b.shape[0] * b.shape[2])
    n_dim = b_rows if tb else b_cols
    assert (b_cols if tb else b_rows) == k_dim, (a.shape, b.shape, ta, tb)
    limit_n = n_dim // out_blocks if out_blocks else (n_dim if b.ndim == 2 or tb else b.shape[2])
    limit_k = b.shape[2] if (b.ndim == 3 and tb) else k_dim
    tm, tn, tk = _tile(m_dim, tm), _tile(limit_n, min(tn, limit_n)), _tile(limit_k, min(tk, limit_k))
    a_spec = (pl.BlockSpec((tk, tm), lambda i, j, k: (k, i)) if ta else pl.BlockSpec((tm, tk), lambda i, j, k: (i, k)))
    if b.ndim == 3:
        b_spec = (_col_blocked_spec(tn, tk, b.shape[2], 1, 2) if tb else _col_blocked_spec(tk, tn, b.shape[2], 2, 1))
    else:
        b_spec = (pl.BlockSpec((tn, tk), lambda i, j, k: (j, k)) if tb
                  else pl.BlockSpec((tk, tn), lambda i, j, k: (k, j)))
    o_spec = (_col_blocked_spec(tm, tn, n_dim // out_blocks, 0, 1) if out_blocks
              else pl.BlockSpec((tm, tn), lambda i, j, k: (i, j)))
    o_shape = (out_blocks, m_dim, n_dim // out_blocks) if out_blocks else (m_dim, n_dim)
    dn = (((0 if ta else 1,), (1 if tb else 0,)), ((), ()))
    has_acc = acc is not None

    def body(*refs):
        a_ref, b_ref = refs[0], refs[1]
        o_ref = refs[-1]
        k = pl.program_id(2)
        p = lax.dot_general(a_ref[...].astype(BF16), b_ref[...].astype(BF16), dn,
                            preferred_element_type=F32)

        @pl.when(k == 0)
        def _():
            o_ref[...] = p + refs[2][...] if has_acc else p

        @pl.when(k > 0)
        def _():
            o_ref[...] += p

    return pl.pallas_call(
        body, name=name,
        out_shape=jax.ShapeDtypeStruct(o_shape, F32),
        grid=(m_dim // tm, n_dim // tn, k_dim // tk),
        in_specs=[a_spec, b_spec] + ([o_spec] if has_acc else []),
        out_specs=o_spec,
        compiler_params=_params(("parallel", "parallel", "arbitrary")),
    )(*([a, b] + ([acc] if has_acc else [])))


def _mm_relu2(a, b, *, name, tm=1024):
    m_dim, k_dim = a.shape
    n_blocks, _, tn = b.shape
    n_dim = n_blocks * tn
    tm = _tile(m_dim, tm)

    def body(a_ref, b_ref, h_ref, act_ref):
        h = jnp.dot(a_ref[...].astype(BF16), b_ref[...].astype(BF16), preferred_element_type=F32)
        h_ref[...] = h
        r = jnp.maximum(h, 0.0)
        act_ref[...] = (r * r).astype(BF16)

    o_spec = pl.BlockSpec((tm, tn), lambda i, j: (i, j))
    return pl.pallas_call(
        body, name=name,
        out_shape=(jax.ShapeDtypeStruct((m_dim, n_dim), F32), jax.ShapeDtypeStruct((m_dim, n_dim), BF16)),
        grid=(m_dim // tm, n_dim // tn),
        in_specs=[pl.BlockSpec((tm, k_dim), lambda i, j: (i, 0)),
                  pl.BlockSpec((None, k_dim, tn), lambda i, j: (j, 0, 0))],
        out_specs=(o_spec, o_spec),
        compiler_params=_params(("parallel", "parallel")),
    )(a, b)


def _mm_relu2_grad(d, b, h, *, name, tm=1024, tn=1024):
    m_dim, k_dim = d.shape
    n_dim = b.shape[0]
    tm, tn = _tile(m_dim, tm), _tile(n_dim, tn)

    def body(d_ref, b_ref, h_ref, o_ref):
        p = lax.dot_general(d_ref[...].astype(BF16), b_ref[...].astype(BF16), ((NT), ((), ())),
                            preferred_element_type=F32)
        o_ref[...] = (p * (2.0 * jnp.maximum(h_ref[...], 0.0))).astype(BF16)

    o_spec = pl.BlockSpec((tm, tn), lambda i, j: (i, j))
    return pl.pallas_call(
        body, name=name,
        out_shape=jax.ShapeDtypeStruct((m_dim, n_dim), BF16),
        grid=(m_dim // tm, n_dim // tn),
        in_specs=[pl.BlockSpec((tm, k_dim), lambda i, j: (i, 0)), pl.BlockSpec((tn, k_dim), lambda i, j: (j, 0)), o_spec],
        out_specs=o_spec,
        compiler_params=_params(("parallel", "parallel")),
    )(d, b, h)


def _rowwise(f, rows, params, row_out, acc_out, *, tm, name):
    length = rows[0].shape[0]
    tm = _row_tile(length, tm)
    nr, npar, nro = len(rows), len(params), len(row_out)

    def body(*refs):
        ins = [r[...] for r in refs[:nr + npar]]
        outs = refs[nr + npar:]
        r_o, a_o = f(*ins)
        for ref, val in zip(outs[:nro], r_o):
            ref[...] = val.astype(ref.dtype)
        i = pl.program_id(0)
        for ref, val in zip(outs[nro:], a_o):
            @pl.when(i == 0)
            def _(ref=ref, val=val):
                ref[...] = val.astype(ref.dtype)

            @pl.when(i > 0)
            def _(ref=ref, val=val):
                ref[...] += val.astype(ref.dtype)

    in_specs = ([pl.BlockSpec((tm, r.shape[1]), lambda i: (i, 0)) for r in rows]
                + [pl.BlockSpec(p.shape, lambda i: (0, 0)) for p in params])
    out_specs = ([pl.BlockSpec((tm, w), lambda i: (i, 0)) for w, _ in row_out]
                 + [pl.BlockSpec(s, lambda i: (0, 0)) for s, _ in acc_out])
    out_shape = ([jax.ShapeDtypeStruct((length, w), dt) for w, dt in row_out]
                 + [jax.ShapeDtypeStruct(s, dt) for s, dt in acc_out])
    res = pl.pallas_call(
        body, name=name, out_shape=out_shape, grid=(length // tm,),
        in_specs=in_specs, out_specs=out_specs,
        compiler_params=_params(("arbitrary",) if acc_out else ("parallel",)),
    )(*rows, *params)
    return res[:nro], res[nro:]


def _rw_fwd(f, rows, params, *, tm, name):
    tm_ = _row_tile(rows[0].shape[0], tm)
    shapes = jax.eval_shape(f, *[jax.ShapeDtypeStruct((tm_, r.shape[1]), r.dtype) for r in rows],
                            *[jax.ShapeDtypeStruct(p.shape, p.dtype) for p in params])
    row_out = [(s.shape[1], s.dtype) for s in shapes]
    outs, _ = _rowwise(lambda *v: (f(*v), ()), rows, params, row_out, [], tm=tm, name=name)
    return outs


def _rw_bwd(f, rows, params, cots, *, row_grad, param_grad, tm, name):
    nr, npar, nct = len(rows), len(params), len(cots)

    def g(*vals):
        prim = vals[:nr] + vals[nr + nct:]
        ct = vals[nr:nr + nct]
        _, vjp = jax.vjp(f, *prim)
        grads = vjp(tuple(ct))
        return (tuple(grads[i] for i in range(nr) if row_grad[i]),
                tuple(grads[nr + i] for i in range(npar) if param_grad[i]))

    row_out = [(rows[i].shape[1], F32) for i in range(nr) if row_grad[i]]
    acc_out = [(params[i].shape, F32) for i in range(npar) if param_grad[i]]
    return _rowwise(g, list(rows) + list(cots), params, row_out, acc_out, tm=tm, name=name)


def _f_ln_res(x, h, g, b):
    pre = DN_ALPHA * x + h
    mu = jnp.mean(pre, axis=-1, keepdims=True)
    d = pre - mu
    var = jnp.mean(d * d, axis=-1, keepdims=True)
    return (d * lax.rsqrt(var + LN_EPS) * g + b,)


def _silu(t):
    return t * jax.nn.sigmoid(t)


def _f_gdn_qkv(c):
    a = _silu(c)
    outs = []
    for part, scale in ((0, HEAD_DIM ** -0.5), (1, 1.0)):
        heads = []
        for h in range(GDN_HEADS):
            t = a[:, part * D_MODEL + h * HEAD_DIM: part * D_MODEL + (h + 1) * HEAD_DIM]
            t = t * lax.rsqrt(jnp.sum(t * t, axis=-1, keepdims=True) + 1e-6)
            heads.append(t * scale if scale != 1.0 else t)
        outs.append(jnp.concatenate(heads, axis=-1))
    outs.append(a[:, 2 * D_MODEL:])
    return tuple(outs)


def _f_gdn_out(o, z, norm_g):
    heads = []
    for h in range(GDN_HEADS):
        t = o[:, h * HEAD_DIM:(h + 1) * HEAD_DIM]
        t = t * lax.rsqrt(jnp.mean(t * t, axis=-1, keepdims=True) + RMS_EPS) * norm_g
        heads.append(t)
    return (jnp.concatenate(heads, axis=-1) * _silu(z),)


def _f_attn(xq, kmem, vmem):
    heads = []
    for h in range(XA_HEADS):
        sl = slice(h * HEAD_DIM, (h + 1) * HEAD_DIM)
        s = lax.dot_general(xq[:, sl].astype(BF16), kmem[:, sl].astype(BF16),
                            (((1,), (1,)), ((), ())), preferred_element_type=F32) * (HEAD_DIM ** -0.5)
        m = lax.stop_gradient(jnp.max(s, axis=-1, keepdims=True))
        e = jnp.exp(s - m)
        p = e / jnp.sum(e, axis=-1, keepdims=True)
        heads.append(jnp.dot(p.astype(BF16), vmem[:, sl].astype(BF16), preferred_element_type=F32))
    return (jnp.concatenate(heads, axis=-1),)


def _f_s5_gelu(y, u, d):
    return (jax.nn.gelu(y + d * u),)


def _f_s5_gate(zg, t, b):
    return (zg * jax.nn.sigmoid(t + b),)


def _f_add(a, b):
    return (a + b,)


def _f_add4(a, b, c, d):
    return (((a + b.astype(F32)) + c.astype(F32)) + d.astype(F32),)


def _f_adamw(w, g, m, v):
    m = ADAM_B1 * m + (1.0 - ADAM_B1) * g
    v = ADAM_B2 * v + (1.0 - ADAM_B2) * jnp.square(g)
    m_hat = m / (1.0 - ADAM_B1 ** ADAM_STEP)
    v_hat = v / (1.0 - ADAM_B2 ** ADAM_STEP)
    delta = -ADAM_LR * (m_hat / (jnp.sqrt(v_hat) + ADAM_EPS) + ADAM_WD * w)
    return delta, m, v


def _conv_fwd(u, w, *, tm, name):
    length, chans = u.shape
    tm = min(tm, length)
    tc = _tile(chans, 1024)
    hb = tm // SUBLANES

    def body(cur_ref, prev_ref, w_ref, o_ref, buf):
        i = pl.program_id(1)
        buf[0:SUBLANES, :] = jnp.where(i > 0, prev_ref[...], 0.0)
        buf[SUBLANES:, :] = cur_ref[...]
        acc = buf[pl.ds(SUBLANES - 3, tm), :] * w_ref[0:1, :]
        for k in range(1, GDN_CONV):
            acc = acc + buf[pl.ds(SUBLANES - 3 + k, tm), :] * w_ref[k:k + 1, :]
        o_ref[...] = acc

    return pl.pallas_call(
        body, name=name, out_shape=jax.ShapeDtypeStruct(u.shape, F32),
        grid=(chans // tc, length // tm),
        in_specs=[pl.BlockSpec((tm, tc), lambda j, i: (i, j)),
                  pl.BlockSpec((SUBLANES, tc), lambda j, i: (jnp.maximum(i * hb - 1, 0), j)),
                  pl.BlockSpec((GDN_CONV, tc), lambda j, i: (0, j))],
        out_specs=pl.BlockSpec((tm, tc), lambda j, i: (i, j)),
        scratch_shapes=[pltpu.VMEM((tm + SUBLANES, tc), F32)],
        compiler_params=_params(("parallel", "parallel")),
    )(u, u, w)


def _conv_bwd(u, w, dc, *, tm, name):
    length, chans = u.shape
    tm = min(tm, length)
    tc = _tile(chans, 1024)
    hb = tm // SUBLANES
    last = length // tm - 1

    def body(u_ref, uprev_ref, dc_ref, dcnext_ref, w_ref, du_ref, dw_ref, ubuf, dbuf):
        i = pl.program_id(1)
        ubuf[0:SUBLANES, :] = jnp.where(i > 0, uprev_ref[...], 0.0)
        ubuf[SUBLANES:, :] = u_ref[...]
        dbuf[0:tm, :] = dc_ref[...]
        dbuf[tm:, :] = jnp.where(i < last, dcnext_ref[...], 0.0)
        dcv = dc_ref[...]
        du = dbuf[pl.ds(3, tm), :] * w_ref[0:1, :]
        rows = [jnp.sum(dcv * ubuf[pl.ds(SUBLANES - 3, tm), :], axis=0, keepdims=True)]
        for k in range(1, GDN_CONV):
            du = du + dbuf[pl.ds(3 - k, tm), :] * w_ref[k:k + 1, :]
            rows.append(jnp.sum(dcv * ubuf[pl.ds(SUBLANES - 3 + k, tm), :], axis=0, keepdims=True))
        du_ref[...] = du
        dwv = jnp.concatenate(rows, axis=0)

        @pl.when(i == 0)
        def _():
            dw_ref[...] = dwv

        @pl.when(i > 0)
        def _():
            dw_ref[...] += dwv

    return pl.pallas_call(
        body, name=name,
        out_shape=(jax.ShapeDtypeStruct(u.shape, F32), jax.ShapeDtypeStruct((GDN_CONV, chans), F32)),
        grid=(chans // tc, length // tm),
        in_specs=[pl.BlockSpec((tm, tc), lambda j, i: (i, j)),
                  pl.BlockSpec((SUBLANES, tc), lambda j, i: (jnp.maximum(i * hb - 1, 0), j)),
                  pl.BlockSpec((tm, tc), lambda j, i: (i, j)),
                  pl.BlockSpec((SUBLANES, tc), lambda j, i: (jnp.minimum((i + 1) * hb, (last + 1) * hb - 1), j)),
                  pl.BlockSpec((GDN_CONV, tc), lambda j, i: (0, j))],
        out_specs=(pl.BlockSpec((tm, tc), lambda j, i: (i, j)),
                   pl.BlockSpec((GDN_CONV, tc), lambda j, i: (0, j))),
        scratch_shapes=[pltpu.VMEM((tm + SUBLANES, tc), F32), pltpu.VMEM((tm + SUBLANES, tc), F32)],
        compiler_params=_params(("parallel", "arbitrary")),
    )(u, u, dc, dc, w)


def _dot(a, b, dims, precision=None):
    if precision is None:
        a, b = a.astype(BF16), b.astype(BF16)
    return lax.dot_general(a, b, (dims, ((), ())), preferred_element_type=F32, precision=precision)


NN = ((1,), (0,))
NT = ((1,), (1,))
TN = ((0,), (0,))
HI = lax.Precision.HIGHEST


def _hmap(f, *lists):
    return [f(*t) for t in zip(*lists)]


@jax.custom_vjp
def _unit_lower_inverse(a):
    c = a[0].shape[0]
    eye = (lax.broadcasted_iota(jnp.int32, (c, c), 0) == lax.broadcasted_iota(jnp.int32, (c, c), 1)).astype(F32)
    p = _hmap(lambda x: -x, a)
    t = _hmap(lambda x: eye + x, p)
    for _ in range(int(math.log2(c)) - 1):
        p = _hmap(lambda x: _dot(x, x, NN, HI), p)
        t = _hmap(lambda x, y: x + _dot(x, y, NN, HI), t, p)
    return t


def _uli_fwd(a):
    t = _unit_lower_inverse(a)
    return t, t


def _uli_bwd(t, dt):
    left = _hmap(lambda x, y: _dot(x, y, TN, HI), t, dt)
    return (_hmap(lambda x, y: -_dot(x, y, NT, HI), left, t),)


_unit_lower_inverse.defvjp(_uli_fwd, _uli_bwd)


def _gdn_chunk(q, k, v, bl, al, a_log, dt_bias, state):
    c = q[0].shape[0]
    row = lax.broadcasted_iota(jnp.int32, (c, c), 0)
    col = lax.broadcasted_iota(jnp.int32, (c, c), 1)
    causal = row >= col
    strict = row > col
    eye = (row == col).astype(F32)
    beta = _hmap(jax.nn.sigmoid, bl)
    g = _hmap(lambda a_, l_, d_: -jnp.exp(a_) * jax.nn.softplus(l_ + d_), a_log, al, dt_bias)
    g_r = _hmap(lambda x: jnp.sum(eye * x, axis=0, keepdims=True), g)
    gc = _hmap(lambda x: jnp.sum(jnp.where(causal, x, 0.0), axis=1, keepdims=True), g_r)
    gc_r = _hmap(lambda x: jnp.sum(jnp.where(row <= col, x, 0.0), axis=0, keepdims=True), g)
    decay = _hmap(lambda x, y: jnp.where(causal, jnp.exp(jnp.where(causal, x - y, 0.0)), 0.0), gc, gc_r)
    e_gc = _hmap(jnp.exp, gc)
    kb = _hmap(jnp.multiply, k, beta)
    vb = _hmap(jnp.multiply, v, beta)
    a_mat = _hmap(lambda x, y, d: jnp.where(strict, _dot(x, y, NT) * d, 0.0), kb, k, decay)
    t_inv = _unit_lower_inverse(a_mat)
    u_blk = _hmap(lambda t, x: _dot(t, x, NN), t_inv, vb)
    w_blk = _hmap(lambda t, x, e: _dot(t, x * e, NN), t_inv, kb, e_gc)
    v_new = _hmap(lambda u, w, s: u - _dot(w, s, NN), u_blk, w_blk, state)
    attn = _hmap(lambda x, y, d: _dot(x, y, NT) * d, q, k, decay)
    o_state = _hmap(lambda x, e, s: _dot(x * e, s, NN), q, e_gc, state)
    o = _hmap(lambda base, at, vn: base + _dot(at, vn, NN), o_state, attn, v_new)
    g_last = _hmap(lambda x: jnp.sum(x, axis=0, keepdims=True), g)
    k_dec = _hmap(lambda x, gl, c_: x * jnp.exp(gl - c_), k, g_last, gc)
    new_state = _hmap(lambda s, gl, kd, vn: s * jnp.exp(gl) + _dot(kd, vn, TN), state, g_last, k_dec, v_new)
    return o, new_state


def _gdn_operands(q_ref, k_ref, v_ref, bav, alog_ref, dtb_ref):
    hs = range(GDN_HEADS)
    cols = [slice(h * HEAD_DIM, (h + 1) * HEAD_DIM) for h in hs]
    return ([q_ref[:, sl] for sl in cols], [k_ref[:, sl] for sl in cols], [v_ref[:, sl] for sl in cols],
            [bav[:, h:h + 1] for h in hs], [bav[:, h + GDN_HEADS:h + GDN_HEADS + 1] for h in hs],
            [alog_ref[h:h + 1, 0:1] for h in hs], [dtb_ref[h:h + 1, 0:1] for h in hs])


def _gdn_scan_fwd(q, k, v, ba, a_log, dt_bias, *, name):
    length = q.shape[0]
    n = length // GDN_CHUNK
    c = GDN_CHUNK

    def body(q_ref, k_ref, v_ref, ba_ref, alog_ref, dtb_ref, o_ref, s_ref, state):
        i = pl.program_id(0)

        @pl.when(i == 0)
        def _():
            state[...] = jnp.zeros_like(state)

        bav = ba_ref[...]
        heads = [slice(h * HEAD_DIM, (h + 1) * HEAD_DIM) for h in range(GDN_HEADS)]
        s_in = [state[h] for h in range(GDN_HEADS)]
        o, s_out = _gdn_chunk(*_gdn_operands(q_ref, k_ref, v_ref, bav, alog_ref, dtb_ref), s_in)
        for h, sl in enumerate(heads):
            s_ref[h] = s_in[h]
            o_ref[:, sl] = o[h]
            state[h] = s_out[h]

    row_spec = pl.BlockSpec((c, D_MODEL), lambda i: (i, 0))
    small = pl.BlockSpec((GDN_HEADS, LANES), lambda i: (0, 0))
    return pl.pallas_call(
        body, name=name,
        out_shape=(jax.ShapeDtypeStruct((length, D_MODEL), F32),
                   jax.ShapeDtypeStruct((n, GDN_HEADS, HEAD_DIM, HEAD_DIM), F32)),
        grid=(n,),
        in_specs=[row_spec, row_spec, row_spec, pl.BlockSpec((c, LANES), lambda i: (i, 0)), small, small],
        out_specs=(row_spec, pl.BlockSpec((None, GDN_HEADS, HEAD_DIM, HEAD_DIM), lambda i: (i, 0, 0, 0))),
        scratch_shapes=[pltpu.VMEM((GDN_HEADS, HEAD_DIM, HEAD_DIM), F32)],
        compiler_params=_params(("arbitrary",)),
    )(q, k, v, ba, a_log, dt_bias)


def _gdn_scan_bwd(q, k, v, ba, a_log, dt_bias, states, do, *, name):
    length = q.shape[0]
    n = length // GDN_CHUNK
    c = GDN_CHUNK

    def body(q_ref, k_ref, v_ref, ba_ref, alog_ref, dtb_ref, s_ref, do_ref,
             dq_ref, dk_ref, dv_ref, dba_ref, dalog_ref, ddtb_ref, dstate):
        i = pl.program_id(0)

        @pl.when(i == 0)
        def _():
            dstate[...] = jnp.zeros_like(dstate)
            dalog_ref[...] = jnp.zeros_like(dalog_ref)
            ddtb_ref[...] = jnp.zeros_like(ddtb_ref)

        bav = ba_ref[...]
        lane = lax.broadcasted_iota(jnp.int32, (c, LANES), 1)
        sub8 = lax.broadcasted_iota(jnp.int32, (GDN_HEADS, LANES), 0)
        lane8 = lax.broadcasted_iota(jnp.int32, (GDN_HEADS, LANES), 1)
        slab = jnp.zeros((c, LANES), F32)
        dalog_all = jnp.zeros((GDN_HEADS, LANES), F32)
        ddtb_all = jnp.zeros((GDN_HEADS, LANES), F32)
        heads = [slice(h * HEAD_DIM, (h + 1) * HEAD_DIM) for h in range(GDN_HEADS)]
        ds_in = [dstate[h] for h in range(GDN_HEADS)]
        s_in = [s_ref[h] for h in range(GDN_HEADS)]
        _, vjp = jax.vjp(_gdn_chunk, *_gdn_operands(q_ref, k_ref, v_ref, bav, alog_ref, dtb_ref), s_in)
        dq, dk, dv, dbl, dal, dalog, ddtb, ds = vjp(([do_ref[:, sl] for sl in heads], ds_in))
        for h, sl in enumerate(heads):
            dq_ref[:, sl] = dq[h]
            dk_ref[:, sl] = dk[h]
            dv_ref[:, sl] = dv[h]
            dstate[h] = ds[h]
            slab = slab + jnp.where(lane == h, dbl[h], 0.0) + jnp.where(lane == h + GDN_HEADS, dal[h], 0.0)
            here = (sub8 == h) & (lane8 == 0)
            dalog_all = dalog_all + jnp.where(here, dalog[h], 0.0)
            ddtb_all = ddtb_all + jnp.where(here, ddtb[h], 0.0)
        dba_ref[...] = slab
        dalog_ref[...] += dalog_all
        ddtb_ref[...] += ddtb_all

    row_spec = pl.BlockSpec((c, D_MODEL), lambda i: (n - 1 - i, 0))
    small = pl.BlockSpec((GDN_HEADS, LANES), lambda i: (0, 0))
    return pl.pallas_call(
        body, name=name,
        out_shape=(jax.ShapeDtypeStruct((length, D_MODEL), F32),) * 3
        + (jax.ShapeDtypeStruct((length, LANES), F32),
           jax.ShapeDtypeStruct((GDN_HEADS, LANES), F32), jax.ShapeDtypeStruct((GDN_HEADS, LANES), F32)),
        grid=(n,),
        in_specs=[row_spec, row_spec, row_spec,
                  pl.BlockSpec((c, LANES), lambda i: (n - 1 - i, 0)), small, small,
                  pl.BlockSpec((None, GDN_HEADS, HEAD_DIM, HEAD_DIM), lambda i: (n - 1 - i, 0, 0, 0)),
                  row_spec],
        out_specs=(row_spec, row_spec, row_spec,
                   pl.BlockSpec((c, LANES), lambda i: (n - 1 - i, 0)), small, small),
        scratch_shapes=[pltpu.VMEM((GDN_HEADS, HEAD_DIM, HEAD_DIM), F32)],
        compiler_params=_params(("arbitrary",)),
    )(q, k, v, ba, a_log, dt_bias, states, do)


S5_W = S5_T * LANES
S5_S = 2 * 8 * S5_STATE
S5_SH = S5_S // 2


def _iota2(shape):
    return lax.broadcasted_iota(jnp.int32, shape, 0), lax.broadcasted_iota(jnp.int32, shape, 1)


def _s5_rep_t(t, dtype):
    row, col = _iota2((S5_T * S5_GROUP, LANES))
    return ((jnp.right_shift(row, 4) == t) & (jnp.bitwise_and(row, 15) == jnp.bitwise_and(col, 15))).astype(dtype)


def _s5_rep_state(dtype):
    row, col = _iota2((2 * S5_STATE, S5_S))
    return ((jnp.right_shift(row, 6) == jnp.right_shift(col, 9))
            & (jnp.bitwise_and(row, 63) == jnp.bitwise_and(col, 63))).astype(dtype)


def _s5_masks():
    row, col = _iota2((LANES, LANES))
    m_ab = jnp.right_shift(row, 4) == jnp.right_shift(col, 4)
    row, col = _iota2((S5_S, LANES))
    m_e = jnp.bitwise_and(jnp.right_shift(row, 6), 7) == jnp.right_shift(col, 4)
    row, col = _iota2((LANES, S5_S))
    m_f = jnp.right_shift(row, 4) == jnp.bitwise_and(jnp.right_shift(col, 6), 7)
    return m_ab, m_e, m_f


def _s5_expand(kx_ref, ec_ref, fc_ref, kb_scr, e_scr, f_scr):
    m_ab, m_e, m_f = _s5_masks()
    kx = kx_ref[...].astype(BF16)
    ec = ec_ref[...].astype(BF16)
    rep_state = _s5_rep_state(BF16)
    for t in range(S5_T):
        rep = _s5_rep_t(t, BF16)
        cols = slice(t * LANES, (t + 1) * LANES)
        kb_scr[t] = jnp.where(m_ab, jnp.dot(kx, rep, preferred_element_type=F32), 0.0).astype(BF16)
        e_scr[:, cols] = jnp.where(m_e, jnp.dot(ec, rep, preferred_element_type=F32), 0.0).astype(BF16)
        f_scr[cols, :] = jnp.where(m_f, jnp.dot(fc_ref[t].astype(BF16), rep_state, preferred_element_type=F32),
                                   0.0).astype(BF16)


def _s5_token_rows(ref, n):
    return [ref[pl.ds(t, n, stride=S5_T), :].astype(BF16) for t in range(S5_T)]


def _s5_scan_fwd(u, kx, ec, fc, at, *, name):
    length = u.shape[0]
    n = length // S5_T
    assert n % SUBLANES == 0

    def body(u_ref, kx_ref, ec_ref, fc_ref, at_ref, y_ref, h_ref, kb_scr, e_scr, f_scr, g_scr):
        _s5_expand(kx_ref, ec_ref, fc_ref, kb_scr, e_scr, f_scr)
        us = _s5_token_rows(u_ref, n)
        g_scr[...] = jnp.dot(jnp.concatenate(us, axis=1), f_scr[...], preferred_element_type=F32)
        ar, ai = at_ref[:, :S5_SH], at_ref[:, S5_SH:]

        def step(blk, h):
            base = pl.multiple_of(blk * SUBLANES, SUBLANES)
            g8 = g_scr[pl.ds(base, SUBLANES), :]
            rows = []
            for r in range(SUBLANES):
                rows.append(h)
                hr, hi = h[:, :S5_SH], h[:, S5_SH:]
                h = jnp.concatenate([ar * hr - ai * hi, ar * hi + ai * hr], axis=1) + g8[r:r + 1, :]
            h_ref[pl.ds(base, SUBLANES), :] = jnp.concatenate(rows, axis=0)
            return h

        lax.fori_loop(0, n // SUBLANES, step, jnp.zeros((1, S5_S), F32))
        hb = h_ref[...].astype(BF16)
        for t in range(S5_T):
            acc = jnp.dot(hb, e_scr[:, t * LANES:(t + 1) * LANES], preferred_element_type=F32)
            for s in range(t + 1):
                acc = acc + jnp.dot(us[s], kb_scr[t - s], preferred_element_type=F32)
            y_ref[pl.ds(t, n, stride=S5_T), :] = acc

    return pl.pallas_call(
        body, name=name,
        out_shape=(jax.ShapeDtypeStruct((length, D_MODEL), F32), jax.ShapeDtypeStruct((S5_TILES, n, S5_S), F32)),
        grid=(S5_TILES,),
        in_specs=[pl.BlockSpec((length, LANES), lambda k: (0, k)), _s5_spec(LANES, S5_T * S5_GROUP),
                  _s5_spec(S5_S, S5_T * S5_GROUP), _s5_spec(S5_T, LANES, LANES), _s5_spec(1, S5_S)],
        out_specs=(pl.BlockSpec((length, LANES), lambda k: (0, k)), _s5_spec(n, S5_S)),
        scratch_shapes=[pltpu.VMEM((S5_T, LANES, LANES), BF16), pltpu.VMEM((S5_S, S5_W), BF16),
                        pltpu.VMEM((S5_W, S5_S), BF16), pltpu.VMEM((n, S5_S), F32)],
        compiler_params=_params(("parallel",)),
    )(u, kx, ec, fc, at)


def _s5_spec(*tail):
    return pl.BlockSpec((None,) + tail, lambda k: (k,) + (0,) * len(tail))


def _s5_scan_bwd(dy, kx, ec, fc, at, hs, *, name):
    length = dy.shape[0]
    n = length // S5_T

    def body(dy_ref, kx_ref, ec_ref, fc_ref, at_ref, h_ref, du_ref, dg_ref, dat_ref, kb_scr, e_scr, f_scr, dh_scr):
        _s5_expand(kx_ref, ec_ref, fc_ref, kb_scr, e_scr, f_scr)
        dys = _s5_token_rows(dy_ref, n)
        dh_scr[...] = _dot(jnp.concatenate(dys, axis=1), e_scr[...], NT)
        ar, ai = at_ref[:, :S5_SH], at_ref[:, S5_SH:]

        def step(it, carry):
            cy, dat = carry
            base = pl.multiple_of((n // SUBLANES - 1 - it) * SUBLANES, SUBLANES)
            dh8 = dh_scr[pl.ds(base, SUBLANES), :]
            h8 = h_ref[pl.ds(base, SUBLANES), :]
            rows = [None] * SUBLANES
            for r in reversed(range(SUBLANES)):
                rows[r] = cy
                cr, ci = cy[:, :S5_SH], cy[:, S5_SH:]
                hr, hi = h8[r:r + 1, :S5_SH], h8[r:r + 1, S5_SH:]
                dat = dat + jnp.concatenate([cr * hr + ci * hi, ci * hr - cr * hi], axis=1)
                cy = dh8[r:r + 1, :] + jnp.concatenate([ar * cr + ai * ci, ar * ci - ai * cr], axis=1)
            dg_ref[pl.ds(base, SUBLANES), :] = jnp.concatenate(rows, axis=0)
            return cy, dat

        zero = jnp.zeros((1, S5_S), F32)
        _, dat = lax.fori_loop(0, n // SUBLANES, step, (zero, zero))
        dat_ref[...] = dat
        dgb = dg_ref[...].astype(BF16)
        for s in range(S5_T):
            acc = _dot(dgb, f_scr[s * LANES:(s + 1) * LANES, :], NT)
            for t in range(s, S5_T):
                acc = acc + _dot(dys[t], kb_scr[t - s], NT)
            du_ref[pl.ds(s, n, stride=S5_T), :] = acc

    row_spec = pl.BlockSpec((length, LANES), lambda k: (0, k))
    return pl.pallas_call(
        body, name=name,
        out_shape=(jax.ShapeDtypeStruct((length, D_MODEL), F32), jax.ShapeDtypeStruct((S5_TILES, n, S5_S), F32),
                   jax.ShapeDtypeStruct((S5_TILES, 1, S5_S), F32)),
        grid=(S5_TILES,),
        in_specs=[row_spec, _s5_spec(LANES, S5_T * S5_GROUP), _s5_spec(S5_S, S5_T * S5_GROUP),
                  _s5_spec(S5_T, LANES, LANES), _s5_spec(1, S5_S), _s5_spec(n, S5_S)],
        out_specs=(row_spec, _s5_spec(n, S5_S), _s5_spec(1, S5_S)),
        scratch_shapes=[pltpu.VMEM((S5_T, LANES, LANES), BF16), pltpu.VMEM((S5_S, S5_W), BF16),
                        pltpu.VMEM((S5_W, S5_S), BF16), pltpu.VMEM((n, S5_S), F32)],
        compiler_params=_params(("parallel",)),
    )(dy, kx, ec, fc, at, hs)


def _s5_operator_grads(dy, u, hs, dg, *, name):
    length = u.shape[0]
    n = length // S5_T

    def body(dy_ref, u_ref, h_ref, dg_ref, dkx_ref, dec_ref, dfc_ref):
        dys = _s5_token_rows(dy_ref, n)
        us = _s5_token_rows(u_ref, n)
        ucat = jnp.concatenate(us, axis=1)
        m_ab, m_e, m_f = _s5_masks()
        hb = h_ref[...].astype(BF16)
        dgb = dg_ref[...].astype(BF16)
        lane = lax.broadcasted_iota(jnp.int32, (1, LANES), 1)
        lane_group = jnp.right_shift(lane, 4)

        def own_block(x, mask):
            x = jnp.where(mask, x, 0.0)
            for shift in (64, 32, 16):
                x = x + pltpu.roll(x, shift, 1)
            return x

        def place(halves, t, x):
            halves[t // 8] = jnp.where(lane_group == t % 8, x, halves[t // 8])

        dkb = [jnp.zeros((LANES, LANES), F32) for _ in range(S5_T)]
        dec = [jnp.zeros((S5_S, LANES), F32) for _ in range(2)]
        for t in range(S5_T):
            d_t = _dot(ucat, dys[t], TN)
            for s in range(t + 1):
                dkb[t - s] = dkb[t - s] + d_t[s * LANES:(s + 1) * LANES, :]
            place(dec, t, own_block(_dot(hb, dys[t], TN), m_e))
            wide = jnp.where(m_f, _dot(us[t], dgb, TN), 0.0)
            parts = []
            for r in range(2):
                acc = wide[:, r * S5_SH:r * S5_SH + LANES]
                for q in range(1, S5_SH // LANES):
                    acc = acc + wide[:, r * S5_SH + q * LANES:r * S5_SH + (q + 1) * LANES]
                parts.append(acc + pltpu.roll(acc, S5_STATE, 1))
            dfc_ref[t] = jnp.where(lane < S5_STATE, parts[0], parts[1])
        dkx = [jnp.zeros((LANES, LANES), F32) for _ in range(2)]
        for t in range(S5_T):
            place(dkx, t, own_block(dkb[t], m_ab))
        dkx_ref[...] = jnp.concatenate(dkx, axis=1)
        dec_ref[...] = jnp.concatenate(dec, axis=1)

    row_spec = pl.BlockSpec((length, LANES), lambda k: (0, k))
    outs = (_s5_spec(LANES, S5_T * S5_GROUP), _s5_spec(S5_S, S5_T * S5_GROUP), _s5_spec(S5_T, LANES, LANES))
    return pl.pallas_call(
        body, name=name,
        out_shape=(jax.ShapeDtypeStruct((S5_TILES, LANES, S5_T * S5_GROUP), F32),
                   jax.ShapeDtypeStruct((S5_TILES, S5_S, S5_T * S5_GROUP), F32),
                   jax.ShapeDtypeStruct((S5_TILES, S5_T, LANES, LANES), F32)),
        grid=(S5_TILES,),
        in_specs=[row_spec, row_spec, _s5_spec(n, S5_S), _s5_spec(n, S5_S)],
        out_specs=outs,
        compiler_params=_params(("parallel",)),
    )(dy, u, hs, dg)


def _s5_prep(a_re, a_im, b_re, b_im, c_re, c_im, log_dt):
    t_len, tiles = S5_T, S5_TILES
    dt = jnp.exp(log_dt)[:, None]
    mag = jnp.exp(a_re * dt)
    ab_re, ab_im = mag * jnp.cos(a_im * dt), mag * jnp.sin(a_im * dt)
    den = jnp.square(a_re) + jnp.square(a_im)
    n_re, n_im = ab_re - 1.0, ab_im
    f_re = (n_re * a_re + n_im * a_im) / den
    f_im = (n_im * a_re - n_re * a_im) / den
    bb_re = f_re[..., None] * b_re - f_im[..., None] * b_im
    bb_im = f_re[..., None] * b_im + f_im[..., None] * b_re
    p_re, p_im = [jnp.ones_like(ab_re)], [jnp.zeros_like(ab_re)]
    for _ in range(t_len):
        p_re, p_im = (p_re + [p_re[-1] * ab_re - p_im[-1] * ab_im],
                      p_im + [p_re[-1] * ab_im + p_im[-1] * ab_re])
    rev_re, rev_im = jnp.stack(p_re[t_len - 1::-1]), jnp.stack(p_im[t_len - 1::-1])
    p_re, p_im = jnp.stack(p_re), jnp.stack(p_im)
    ca_re = c_re[None] * p_re[:, :, None, :] - c_im[None] * p_im[:, :, None, :]
    ca_im = c_re[None] * p_im[:, :, None, :] + c_im[None] * p_re[:, :, None, :]
    lag = (jnp.einsum('tgip,gpj->tgij', ca_re[:t_len], bb_re, precision=HI)
           - jnp.einsum('tgip,gpj->tgij', ca_im[:t_len], bb_im, precision=HI))
    kx = lag.reshape(t_len, tiles, 8, S5_GROUP, S5_GROUP).transpose(1, 2, 4, 0, 3)
    kx = kx.reshape(tiles, LANES, t_len * S5_GROUP)
    e_st = jnp.stack([ca_re[1:], -ca_im[1:]])
    e_st = e_st.reshape(2, t_len, tiles, 8, S5_GROUP, S5_STATE).transpose(2, 0, 3, 5, 1, 4)
    ec = e_st.reshape(tiles, S5_S, t_len * S5_GROUP)
    ab_b = jnp.stack([rev_re[..., None] * bb_re[None] - rev_im[..., None] * bb_im[None],
                      rev_re[..., None] * bb_im[None] + rev_im[..., None] * bb_re[None]])
    ab_b = ab_b.reshape(2, t_len, tiles, 8, S5_STATE, S5_GROUP).transpose(2, 1, 3, 5, 0, 4)
    fc = ab_b.reshape(tiles, t_len, LANES, 2 * S5_STATE)
    a_t = jnp.stack([p_re[t_len], p_im[t_len]]).reshape(2, tiles, 8 * S5_STATE).transpose(1, 0, 2)
    return kx, ec, fc, a_t.reshape(tiles, 1, S5_S)


TM_ROW = 256


def _gdn_fwd(x, w, tag):
    qkv = _mm(x, w["wqkv"], name="gdn_proj_qkv")
    z = _mm(x, w["wz"], name="gdn_proj_z")
    ba = _mm(x, w["wba"], name="gdn_proj_ba")
    cv = _conv_fwd(qkv, w["conv_w"], tm=TM_ROW, name="gdn_conv")
    q, k, v = _rw_fwd(_f_gdn_qkv, [cv], [], tm=TM_ROW, name="gdn_qkv")
    o, states = _gdn_scan_fwd(q, k, v, ba, w["a_log8"], w["dt_bias8"], name="gdn_scan")
    (mix,) = _rw_fwd(_f_gdn_out, [o, z], [w["norm_g"]], tm=TM_ROW, name="gdn_out")
    return mix, (qkv, z, ba, cv, q, k, v, states, o)


def _gdn_bwd(x, w, saved, dmix, dx_acc):
    qkv, z, ba, cv, q, k, v, states, o = saved
    (do, dz), (dnorm_g,) = _rw_bwd(_f_gdn_out, [o, z], [w["norm_g"]], [dmix], row_grad=[1, 1], param_grad=[1],
                                   tm=TM_ROW, name="gdn_out_bwd")
    dq, dk, dv, dba, dalog, ddtb = _gdn_scan_bwd(q, k, v, ba, w["a_log8"], w["dt_bias8"], states, do,
                                                  name="gdn_scan_bwd")
    (dcv,), _ = _rw_bwd(_f_gdn_qkv, [cv], [], [dq, dk, dv], row_grad=[1], param_grad=[], tm=TM_ROW,
                        name="gdn_qkv_bwd")
    dqkv, dconv_w = _conv_bwd(qkv, w["conv_w"], dcv, tm=TM_ROW, name="gdn_conv_bwd")
    dx = _mm(dqkv, w["wqkv"], tb=True, acc=dx_acc, name="gdn_dx_qkv")
    dx = _mm(dz, w["wz"], tb=True, acc=dx, name="gdn_dx_z")
    dx = _mm(dba, w["wba"], tb=True, acc=dx, name="gdn_dx_ba")
    grads = dict(wqkv=_mm(x, dqkv, ta=True, name="gdn_dw_qkv"), wz=_mm(x, dz, ta=True, name="gdn_dw_z"),
                 wba=_mm(x, dba, ta=True, name="gdn_dw_ba"), conv_w=dconv_w,
                 a_log=dalog[:, 0], dt_bias=ddtb[:, 0], norm_g=dnorm_g[0])
    return dx, grads


def _s5_fwd(x, w, tag):
    u = _mm(x, w["wu"], name="s5_proj_u")
    y, hs = _s5_scan_fwd(u, w["kx"], w["ec"], w["fc"], w["a_t"], name="s5_scan")
    (zg,) = _rw_fwd(_f_s5_gelu, [y, u], [w["d"]], tm=TM_ROW, name="s5_gelu")
    t = _mm(zg, w["w_glu"], name="s5_glu")
    (mix,) = _rw_fwd(_f_s5_gate, [zg, t], [w["b_glu"]], tm=TM_ROW, name="s5_gate")
    return mix, (u, hs, y, zg, t)


def _s5_bwd(x, w, saved, dmix, dx_acc):
    u, hs, y, zg, t = saved
    (dzg, dt), (db_glu,) = _rw_bwd(_f_s5_gate, [zg, t], [w["b_glu"]], [dmix], row_grad=[1, 1], param_grad=[1],
                                   tm=TM_ROW, name="s5_gate_bwd")
    dzg = _mm(dt, w["w_glu"], tb=True, acc=dzg, name="s5_dzg")
    dw_glu = _mm(zg, dt, ta=True, name="s5_dw_glu")
    (dy, du), (dd,) = _rw_bwd(_f_s5_gelu, [y, u], [w["d"]], [dzg], row_grad=[1, 1], param_grad=[1],
                              tm=TM_ROW, name="s5_gelu_bwd")
    du_scan, dg, dat = _s5_scan_bwd(dy, w["kx"], w["ec"], w["fc"], w["a_t"], hs, name="s5_scan_bwd")
    dkx, dec, dfc = _s5_operator_grads(dy, u, hs, dg, name="s5_operator_grads")
    (du,) = _rw_fwd(_f_add, [du, du_scan], [], tm=TM_ROW, name="s5_du_add")
    d_a_re, d_a_im, d_b_re, d_b_im, d_c_re, d_c_im, d_log_dt = w["prep_vjp"]((dkx, dec, dfc, dat))
    dx = _mm(du, w["wu"], tb=True, acc=dx_acc, name="s5_dx_u")
    grads = dict(wu=_mm(x, du, ta=True, name="s5_dw_u"), w_glu=dw_glu, b_glu=db_glu[0], d=dd[0],
                 a_re=d_a_re, a_im=d_a_im, b_re=d_b_re, b_im=d_b_im, c_re=d_c_re, c_im=d_c_im, log_dt=d_log_dt)
    return dx, grads


def _layer_fwd(x, mem, w, is_gdn):
    mix, msave = (_gdn_fwd if is_gdn else _s5_fwd)(x, w, "")
    xq = _mm(x, w["wxq"], name="proj_xq")
    kv = _mm(mem, w["wkv"], name="mem_kv")
    kmem, vmem = kv[:, :XA_DIM], kv[:, XA_DIM:]
    (cross,) = _rw_fwd(_f_attn, [xq], [kmem, vmem], tm=TM_ROW, name="attn")
    h = _mm(mix, w["wo_mix"], name="wo_mix")
    h = _mm(cross, w["wo_cross"], acc=h, name="wo_cross")
    (x1,) = _rw_fwd(_f_ln_res, [x, h], [w["ln1_g"], w["ln1_b"]], tm=TM_ROW, name="ln_res")
    hm, act = _mm_relu2(x1, w["w1"], name="mlp_up")
    f = _mm(act, w["w2"], name="mlp_down")
    (x2,) = _rw_fwd(_f_ln_res, [x1, f], [w["ln2_g"], w["ln2_b"]], tm=TM_ROW, name="ln_res")
    return x2, (x, msave, xq, kmem, vmem, mix, cross, h, x1, hm, act, f)


def _layer_bwd(mem, w, is_gdn, saved, dx2):
    x, msave, xq, kmem, vmem, mix, cross, h, x1, hm, act, f = saved
    (dx1, df), (dg2, db2) = _rw_bwd(_f_ln_res, [x1, f], [w["ln2_g"], w["ln2_b"]], [dx2], row_grad=[1, 1],
                                    param_grad=[1, 1], tm=TM_ROW, name="ln_res_bwd")
    dhm = _mm_relu2_grad(df, w["w2"], hm, name="mlp_dhm")
    dw2 = _mm(act, df, ta=True, name="mlp_dw2")
    dx1 = _mm(dhm, w["w1"], tb=True, acc=dx1, name="mlp_dx")
    dw1 = _mm(x1, dhm, ta=True, out_blocks=N_CHIPS, name="mlp_dw1")
    (dx, dh), (dg1, db1) = _rw_bwd(_f_ln_res, [x, h], [w["ln1_g"], w["ln1_b"]], [dx1], row_grad=[1, 1],
                                   param_grad=[1, 1], tm=TM_ROW, name="ln_res_bwd")
    dmix = _mm(dh, w["wo_mix"], tb=True, name="wo_dmix")
    dcross = _mm(dh, w["wo_cross"], tb=True, name="wo_dcross")
    dwo = jnp.concatenate([_mm(mix, dh, ta=True, name="wo_dw_mix"), _mm(cross, dh, ta=True, name="wo_dw_cross")], 0)
    (dxq,), (dkmem, dvmem) = _rw_bwd(_f_attn, [xq], [kmem, vmem], [dcross], row_grad=[1], param_grad=[1, 1],
                                     tm=TM_ROW, name="attn_bwd")
    dwkv = _mm(mem, jnp.concatenate([dkmem, dvmem], axis=1), ta=True, name="mem_dw_kv")
    dx = _mm(dxq, w["wxq"], tb=True, acc=dx, name="dx_xq")
    dwxq = _mm(x, dxq, ta=True, name="dw_xq")
    dx, mg = (_gdn_bwd if is_gdn else _s5_bwd)(x, w, msave, dmix, dx)
    grads = dict(mixer=mg, wxq=dwxq, wkv=dwkv, wo=dwo, w1=dw1, w2=dw2,
                 ln1_g=dg1[0], ln1_b=db1[0], ln2_g=dg2[0], ln2_b=db2[0])
    return dx, grads


def _loss_and_grad(y, target):
    def f(yv, tv):
        err = yv - tv
        return (err * (1.0 / D_MODEL),), (0.5 / D_MODEL * jnp.sum(err * err, axis=0, keepdims=True),)

    (dy,), (part,) = _rowwise(f, [y, target], [], [(D_MODEL, F32)], [((1, D_MODEL), F32)], tm=512, name="loss")
    return jnp.sum(part), dy


def _layer_weights(full, i):
    j = i // 2
    w = dict(wkv=full["w_kv_mem"][i].astype(BF16),
             wo_mix=full["w_o"][i, :D_MODEL].astype(BF16), wo_cross=full["w_o"][i, D_MODEL:].astype(BF16),
             ln1_g=full["ln1_g"][i][None], ln1_b=full["ln1_b"][i][None],
             ln2_g=full["ln2_g"][i][None], ln2_b=full["ln2_b"][i][None],
             w1=full["mlp_w1"][:, i].astype(BF16), w2=full["mlp_w2"][i].astype(BF16))
    if i % 2 == 0:
        w_in = full["gdn_w_in"][j]
        gd = 3 * D_MODEL
        w.update(wqkv=w_in[:, :gd].astype(BF16), wz=w_in[:, gd:gd + D_MODEL].astype(BF16),
                 wba=jnp.pad(w_in[:, gd + D_MODEL:gd + D_MODEL + 2 * GDN_HEADS],
                             ((0, 0), (0, LANES - 2 * GDN_HEADS))).astype(BF16),
                 wxq=w_in[:, gd + D_MODEL + 2 * GDN_HEADS:].astype(BF16),
                 conv_w=full["gdn_conv_w"][j],
                 a_log8=jnp.broadcast_to(full["gdn_a_log"][j][:, None], (GDN_HEADS, LANES)),
                 dt_bias8=jnp.broadcast_to(full["gdn_dt_bias"][j][:, None], (GDN_HEADS, LANES)),
                 norm_g=full["gdn_norm_g"][j][None])
    else:
        w_in = full["s5_w_in"][j]
        (kx, ec, fc, a_t), prep_vjp = jax.vjp(
            _s5_prep, full["s5_a_re"][j], full["s5_a_im"][j], full["s5_b_re"][j], full["s5_b_im"][j],
            full["s5_c_re"][j], full["s5_c_im"][j], full["s5_log_dt"][j])
        w.update(wu=w_in[:, :D_MODEL].astype(BF16), wxq=w_in[:, D_MODEL:].astype(BF16),
                 kx=kx, ec=ec, fc=fc, a_t=a_t, prep_vjp=prep_vjp,
                 d=full["s5_d"][j][None], w_glu=full["s5_w_glu"][j].astype(BF16), b_glu=full["s5_b_glu"][j][None])
    return w


def _full_grads(layer_grads):
    g = layer_grads
    gdn = [g[i] for i in range(DEPTH) if i % 2 == 0]
    s5 = [g[i] for i in range(DEPTH) if i % 2 == 1]
    out = dict(
        w_kv_mem=jnp.stack([l["wkv"] for l in g]), w_o=jnp.stack([l["wo"] for l in g]),
        ln1_g=jnp.stack([l["ln1_g"] for l in g]), ln1_b=jnp.stack([l["ln1_b"] for l in g]),
        ln2_g=jnp.stack([l["ln2_g"] for l in g]), ln2_b=jnp.stack([l["ln2_b"] for l in g]),
        mlp_w1=jnp.stack([l["w1"] for l in g], axis=1), mlp_w2=jnp.stack([l["w2"] for l in g]),
        gdn_w_in=jnp.stack([jnp.concatenate([l["mixer"]["wqkv"], l["mixer"]["wz"],
                                             l["mixer"]["wba"][:, :2 * GDN_HEADS], l["wxq"]], axis=1) for l in gdn]),
        gdn_conv_w=jnp.stack([l["mixer"]["conv_w"] for l in gdn]),
        gdn_a_log=jnp.stack([l["mixer"]["a_log"] for l in gdn]),
        gdn_dt_bias=jnp.stack([l["mixer"]["dt_bias"] for l in gdn]),
        gdn_norm_g=jnp.stack([l["mixer"]["norm_g"] for l in gdn]),
        s5_w_in=jnp.stack([jnp.concatenate([l["mixer"]["wu"], l["wxq"]], axis=1) for l in s5]),
        s5_d=jnp.stack([l["mixer"]["d"] for l in s5]),
        s5_w_glu=jnp.stack([l["mixer"]["w_glu"] for l in s5]),
        s5_b_glu=jnp.stack([l["mixer"]["b_glu"] for l in s5]))
    for n in ("a_re", "a_im", "b_re", "b_im", "c_re", "c_im", "log_dt"):
        out["s5_" + n] = jnp.stack([l["mixer"][n] for l in s5])
    return out


def _local_step(x, mem, target, full):
    lw = [_layer_weights(full, i) for i in range(DEPTH)]
    saves = []
    h = x
    for i in range(DEPTH):
        h, s = _layer_fwd(h, mem, lw[i], i % 2 == 0)
        saves.append(s)
    loss, d = _loss_and_grad(h, target)
    grads = [None] * DEPTH
    for i in reversed(range(DEPTH)):
        d, grads[i] = _layer_bwd(mem, lw[i], i % 2 == 0, saves[i], d)
    return loss, d, _full_grads(grads)


ANY = pl.BlockSpec(memory_space=pl.ANY)
SHARD_ROWS = 1024
SMALL_ROWS = 128


def _place():
    return lax.axis_index("x"), lax.axis_index("y"), lax.axis_index("c")


def _other_chips(x, y):
    return [(1 - x, y), (x, 1 - y), (1 - x, 1 - y)]


def _all_gather_chips(wpack, *, name):
    rows = wpack.shape[0]
    half = rows // 2

    def body(w_ref, out_ref, send_sems, recv_sems):
        x, y, c = _place()
        sibling = (x, y, 1 - c)
        chips = _other_chips(x, y)

        def blk(cx, cy, cc):
            return out_ref.at[2 * cx + cy, pl.ds(cc * half, half), :]

        def copy(k, src, dst, to):
            return pltpu.make_async_remote_copy(src_ref=src, dst_ref=dst, send_sem=send_sems.at[k],
                                                recv_sem=recv_sems.at[k], device_id=to, device_id_type=MESH)

        first = [copy(j, w_ref.at[pl.ds(c * half, half), :], blk(x, y, c), (cx, cy, c))
                 for j, (cx, cy) in enumerate(chips)]
        for cp in first:
            cp.start()
        passed = [copy(3 + j, blk(cx, cy, c), blk(cx, cy, c), sibling) for j, (cx, cy) in enumerate(chips)]
        for j, (cx, cy) in enumerate(chips):
            copy(j, blk(cx, cy, c), blk(cx, cy, c), (cx, cy, c)).wait_recv()
            passed[j].start()
        for j, (cx, cy) in enumerate(chips):
            copy(3 + j, blk(cx, cy, 1 - c), blk(cx, cy, 1 - c), sibling).wait_recv()
        for cp in first + passed:
            cp.wait_send()

    return pl.pallas_call(
        body, name=name, out_shape=jax.ShapeDtypeStruct((N_CHIPS, rows, D_MODEL), wpack.dtype),
        in_specs=[ANY], out_specs=ANY,
        scratch_shapes=[pltpu.SemaphoreType.DMA((6,)), pltpu.SemaphoreType.DMA((6,))],
    )(wpack)


def _sibling_swap(buf, *, name):
    def body(in_ref, out_ref, send_sem, recv_sem):
        x, y, c = _place()
        cp = pltpu.make_async_remote_copy(src_ref=in_ref, dst_ref=out_ref, send_sem=send_sem, recv_sem=recv_sem,
                                          device_id=(x, y, 1 - c), device_id_type=MESH)
        cp.start()
        cp.wait()

    return pl.pallas_call(
        body, name=name, out_shape=jax.ShapeDtypeStruct(buf.shape, buf.dtype), in_specs=[ANY], out_specs=ANY,
        scratch_shapes=[pltpu.SemaphoreType.DMA, pltpu.SemaphoreType.DMA],
    )(buf)


def _pair_exchange(gpack, *, name):
    pieces, rows, width = gpack.shape
    half = rows // 2

    def body(in_ref, got_ref, send_sems, recv_sems):
        x, y, c = _place()
        sends = [pltpu.make_async_remote_copy(src_ref=in_ref.at[p, pl.ds((1 - c) * half, half), :],
                                              dst_ref=got_ref.at[p], send_sem=send_sems.at[p],
                                              recv_sem=recv_sems.at[p], device_id=(x, y, 1 - c), device_id_type=MESH)
                 for p in range(pieces)]
        for cp in sends:
            cp.start()
        for cp in sends:
            cp.wait()

    return pl.pallas_call(
        body, name=name, out_shape=jax.ShapeDtypeStruct((pieces, half, width), gpack.dtype),
        in_specs=[ANY], out_specs=ANY,
        scratch_shapes=[pltpu.SemaphoreType.DMA((pieces,)), pltpu.SemaphoreType.DMA((pieces,))],
    )(gpack)


def _pair_add(gpack, got, c, *, name, tm=512):
    pieces, rows, width = gpack.shape
    half = rows // 2
    nb = half // tm

    def body(c_ref, a_ref, b_ref, sum_ref, narrow_ref):
        s = a_ref[...] + b_ref[...]
        sum_ref[...] = s
        narrow_ref[...] = s.astype(BF16)

    blk = pl.BlockSpec((None, tm, width), lambda p, i, c_ref: (p, i, 0))
    return pl.pallas_call(
        body, name=name,
        out_shape=(jax.ShapeDtypeStruct((pieces, half, width), F32), jax.ShapeDtypeStruct((pieces, half, width), BF16)),
        grid_spec=pltpu.PrefetchScalarGridSpec(
            num_scalar_prefetch=1, grid=(pieces, nb),
            in_specs=[pl.BlockSpec((None, tm, width), lambda p, i, c_ref: (p, c_ref[0] * nb + i, 0)), blk],
            out_specs=(blk, blk)),
        compiler_params=_params(("parallel", "parallel")),
    )(c, gpack, got)


def _chip_exchange(pieces, *, name):
    _, rows, width = pieces.shape

    def body(in_ref, out_ref, send_sems, recv_sems):
        x, y, c = _place()
        cps = [pltpu.make_async_remote_copy(src_ref=in_ref.at[2 * cx + cy], dst_ref=out_ref.at[j],
                                            send_sem=send_sems.at[j], recv_sem=recv_sems.at[j],
                                            device_id=(cx, cy, c), device_id_type=MESH)
               for j, (cx, cy) in enumerate(_other_chips(x, y))]
        for cp in cps:
            cp.start()
        for cp in cps:
            cp.wait()

    return pl.pallas_call(
        body, name=name, out_shape=jax.ShapeDtypeStruct((3, rows, width), pieces.dtype), in_specs=[ANY], out_specs=ANY,
        scratch_shapes=[pltpu.SemaphoreType.DMA((3,)), pltpu.SemaphoreType.DMA((3,))],
    )(pieces)


def _all_reduce_small(v, *, name):
    rows, width = v.shape

    def body(in_ref, out_ref, gath, send_sems, recv_sems):
        x, y, c = _place()
        me = 4 * x + 2 * y + c
        gath[me] = in_ref[...]
        peers = []
        for m in range(1, N_DEV):
            px = 1 - x if m & 4 else x
            py = 1 - y if m & 2 else y
            pc = 1 - c if m & 1 else c
            peers.append((m - 1, (px, py, pc), 4 * px + 2 * py + pc))
        for k, peer, _ in peers:
            pltpu.make_async_remote_copy(src_ref=in_ref, dst_ref=gath.at[me], send_sem=send_sems.at[k],
                                         recv_sem=recv_sems.at[k], device_id=peer, device_id_type=MESH).start()
        for k, peer, plin in peers:
            cp = pltpu.make_async_remote_copy(src_ref=in_ref, dst_ref=gath.at[plin], send_sem=send_sems.at[k],
                                              recv_sem=recv_sems.at[k], device_id=peer, device_id_type=MESH)
            cp.wait_send()
            cp.wait_recv()
        acc = gath[0]
        for d in range(1, N_DEV):
            acc = acc + gath[d]
        out_ref[...] = acc

    vmem = pl.BlockSpec(memory_space=pltpu.VMEM)
    return pl.pallas_call(
        body, name=name, out_shape=jax.ShapeDtypeStruct(v.shape, v.dtype), in_specs=[vmem], out_specs=vmem,
        scratch_shapes=[pltpu.VMEM((N_DEV, rows, width), v.dtype),
                        pltpu.SemaphoreType.DMA((N_DEV - 1,)), pltpu.SemaphoreType.DMA((N_DEV - 1,))],
        compiler_params=pltpu.CompilerParams(vmem_limit_bytes=VMEM_LIMIT_V7X),
    )(v)


def _reduce_scatter(gpack):
    x, y, c = _place()
    half = gpack.shape[1] // 2
    got = _pair_exchange(gpack, name="rs_pair_swap")
    pair, pair16 = _pair_add(gpack, got, c.astype(jnp.int32).reshape(1), name="rs_pair_add")
    recv = _chip_exchange(pair16, name="rs_chip_exchange")
    mine = lax.dynamic_index_in_dim(pair, 2 * x + y, axis=0, keepdims=False)
    (total,) = _rw_fwd(_f_add4, [mine, recv[0], recv[1], recv[2]], [], tm=512, name="rs_chip_add")
    theirs = _sibling_swap(total, name="rs_share_swap")
    return jnp.concatenate([jnp.where(c == 0, total, theirs), jnp.where(c == 0, theirs, total)], axis=0)


_SHARDED = (("w_kv_mem", 1), ("w_o", 1), ("mlp_w1", 2), ("mlp_w2", 1), ("gdn_w_in", 2), ("gdn_conv_w", 2),
            ("s5_w_in", 2), ("s5_d", 1), ("s5_w_glu", 1), ("s5_b_glu", 1))
_MATMUL_ONLY = ("w_kv_mem", "w_o", "mlp_w1", "mlp_w2", "gdn_w_in", "s5_w_in", "s5_w_glu")
_KEPT_BLOCKED = ("mlp_w1",)
_REPLICATED = ("ln1_g", "ln1_b", "ln2_g", "ln2_b", "gdn_a_log", "gdn_dt_bias", "gdn_norm_g", "s5_a_re", "s5_a_im",
               "s5_b_re", "s5_b_im", "s5_c_re", "s5_c_im", "s5_log_dt")
_WEIGHTS = ("w_kv_mem", "w_o", "ln1_g", "ln1_b", "ln2_g", "ln2_b", "mlp_w1", "mlp_w2", "gdn_w_in", "gdn_conv_w",
            "gdn_a_log", "gdn_dt_bias", "gdn_norm_g", "s5_w_in", "s5_a_re", "s5_a_im", "s5_b_re", "s5_b_im",
            "s5_c_re", "s5_c_im", "s5_log_dt", "s5_d", "s5_w_glu", "s5_b_glu")


def _pack(arrs, lead=(), unit_rows=SHARD_ROWS):
    nl = len(lead)
    flat = jnp.concatenate([a.reshape(lead + (-1,)) for a in arrs], axis=nl)
    unit = unit_rows * D_MODEL
    pad = -flat.shape[nl] % unit
    flat = jnp.pad(flat, ((0, 0),) * nl + ((0, pad),))
    return flat.reshape(lead + (-1, D_MODEL))


def _unpack(packed, shapes, lead=()):
    nl = len(lead)
    flat = packed.reshape(lead + (-1,))
    out, off = [], 0
    for s in shapes:
        n = math.prod(s)
        out.append(lax.slice_in_dim(flat, off, off + n, axis=nl).reshape(lead + tuple(s)))
        off += n
    return out


def _split3(t):
    hi = t.astype(BF16)
    r1 = t - hi.astype(F32)
    mid = r1.astype(BF16)
    lo = (r1 - mid.astype(F32)).astype(BF16)
    return jnp.stack([hi, mid, lo], axis=-1)


def _join3(t):
    return (t[..., 0].astype(F32) + t[..., 1].astype(F32)) + t[..., 2].astype(F32)


def _merge_chips(blocks, axis):
    return jnp.concatenate([blocks[s] for s in range(N_CHIPS)], axis=axis)


def _pack_for_chips(arrs):
    per_chip = sum(math.prod(a.shape) for a, _ in arrs) // N_CHIPS
    pad = -per_chip % (SHARD_ROWS * D_MODEL)
    pieces = []
    for s in range(N_CHIPS):
        for a, axis in arrs:
            if axis is None:
                pieces.append(a[s].reshape(-1))
                continue
            n = a.shape[axis] // N_CHIPS
            pieces.append(lax.slice_in_dim(a, s * n, (s + 1) * n, axis=axis).reshape(-1))
        pieces.append(jnp.zeros((pad,), F32))
    return jnp.concatenate(pieces).reshape(N_CHIPS, -1, D_MODEL)


def kernel(x, mem, w_kv_mem, w_o, ln1_g, ln1_b, ln2_g, ln2_b, mlp_w1, mlp_w2, gdn_w_in, gdn_conv_w, gdn_a_log, gdn_dt_bias, gdn_norm_g, s5_w_in, s5_a_re, s5_a_im, s5_b_re, s5_b_im, s5_c_re, s5_c_im, s5_log_dt, s5_d, s5_w_glu, s5_b_glu, loss_target, m_w_kv_mem, m_w_o, m_ln1_g, m_ln1_b, m_ln2_g, m_ln2_b, m_mlp_w1, m_mlp_w2, m_gdn_w_in, m_gdn_conv_w, m_gdn_a_log, m_gdn_dt_bias, m_gdn_norm_g, m_s5_w_in, m_s5_a_re, m_s5_a_im, m_s5_b_re, m_s5_b_im, m_s5_c_re, m_s5_c_im, m_s5_log_dt, m_s5_d, m_s5_w_glu, m_s5_b_glu, v_w_kv_mem, v_w_o, v_ln1_g, v_ln1_b, v_ln2_g, v_ln2_b, v_mlp_w1, v_mlp_w2, v_gdn_w_in, v_gdn_conv_w, v_gdn_a_log, v_gdn_dt_bias, v_gdn_norm_g, v_s5_w_in, v_s5_a_re, v_s5_a_im, v_s5_b_re, v_s5_b_im, v_s5_c_re, v_s5_c_im, v_s5_log_dt, v_s5_d, v_s5_w_glu, v_s5_b_glu):
    given = dict(locals())
    w = {n: given[n] for n in _WEIGHTS}
    mom = {n: given["m_" + n] for n in _WEIGHTS}
    var = {n: given["v_" + n] for n in _WEIGHTS}
    shard_names = [n for n, _ in _SHARDED]
    shard_shapes = [w[n].shape for n in shard_names]
    rep_shapes = [w[n].shape for n in _REPLICATED]

    wire = [w[n].astype(BF16) if n in _MATMUL_ONLY else _split3(w[n]) for n in shard_names]
    wire_pack = _pack(wire)
    gathered = _all_gather_chips(wire_pack, name="gather_weights")
    me_chip = 2 * lax.axis_index("x") + lax.axis_index("y")
    gathered = lax.dynamic_update_index_in_dim(gathered, wire_pack, me_chip, axis=0)
    blocks = _unpack(gathered, [a.shape for a in wire], lead=(N_CHIPS,))
    full = {n: blk if n in _KEPT_BLOCKED else _merge_chips(blk if n in _MATMUL_ONLY else _join3(blk), ax)
            for (n, ax), blk in zip(_SHARDED, blocks)}
    full.update({n: w[n] for n in _REPLICATED})

    loss, grad_x, grads = _local_step(x[0], mem[0], loss_target[0], full)
    loss = lax.psum(loss, ("x", "y", "c"))

    g_shard = _reduce_scatter(_pack_for_chips([(grads[n], None if n in _KEPT_BLOCKED else ax) for n, ax in _SHARDED]))
    def pack_small(d):
        return _pack([d[n] for n in _REPLICATED], unit_rows=SMALL_ROWS)

    g_rep = _all_reduce_small(pack_small(grads), name="reduce_replicated")

    def adamw(wp, gp, mp, vp, name):
        return _rw_fwd(_f_adamw, [wp, gp, mp, vp], [], tm=256, name=name)

    outs = {}
    for n, g in zip(shard_names, _unpack(g_shard, shard_shapes)):
        flat = (-1, w[n].shape[-1])
        res = adamw(w[n].reshape(flat), g.reshape(flat), mom[n].reshape(flat), var[n].reshape(flat), "adamw_" + n)
        outs[("grad", n)] = g
        outs.update({(kind, n): a.reshape(w[n].shape) for kind, a in zip(("delta", "new_m", "new_v"), res)})
    packed = (g_rep,) + tuple(adamw(pack_small(w), g_rep, pack_small(mom), pack_small(var), "adamw_replicated"))
    for kind, pr in zip(("grad", "delta", "new_m", "new_v"), packed):
        outs.update({(kind, n): a for n, a in zip(_REPLICATED, _unpack(pr, rep_shapes))})
    return (loss, grad_x[None]) + tuple(outs[(kind, n)] for kind in ("grad", "delta", "new_m", "new_v")
                                        for n in _WEIGHTS)
```

```python
import functools
import math

import jax
import jax.numpy as jnp
from jax import lax
from jax.experimental import pallas as pl
from jax.experimental.pallas import tpu as pltpu

F32 = jnp.float32
BF16 = jnp.bfloat16
MESH = pl.DeviceIdType.MESH

D_MODEL = 1024
DEPTH = 4
GDN_HEADS = 8
HEAD_DIM = 128
GDN_CONV = 4
GDN_CHUNK = 64
S5_GROUPS = 64
S5_GROUP = 16
S5_STATE = 64
XA_HEADS = 4
XA_DIM = 512
D_FF = 4096
DN_ALPHA = (2 * DEPTH) ** 0.25
LN_EPS = 1e-5
RMS_EPS = 1e-6
ADAM_LR, ADAM_B1, ADAM_B2, ADAM_EPS, ADAM_WD, ADAM_STEP = 0.001, 0.9, 0.999, 1e-08, 0.01, 10

VMEM_LIMIT_V7X = 56 * 1024 * 1024
LANES = 128
SUBLANES = 8
S5_T = 16
S5_TILES = D_MODEL // LANES
N_CHIPS = 4
N_DEV = 8


def _params(sem):
    return pltpu.CompilerParams(dimension_semantics=sem, vmem_limit_bytes=VMEM_LIMIT_V7X)


def _tile(n, pref):
    if n <= pref:
        return n
    t = (pref // LANES) * LANES
    while n % t:
        t -= LANES
    return t


def _row_tile(n, pref):
    if n % SUBLANES:
        return n
    t = min(pref, n) // SUBLANES * SUBLANES
    while n % t:
        t -= SUBLANES
    return t


def _col_blocked_spec(rows_tile, cols_tile, block_cols, rows_axis, cols_axis):
    r = block_cols // cols_tile

    def index(*ijk):
        c = ijk[cols_axis]
        return (c, ijk[rows_axis], 0) if r == 1 else (c // r, ijk[rows_axis], c % r)

    return pl.BlockSpec((None, rows_tile, cols_tile), index)


def _mm(a, b, *, ta=False, tb=False, acc=None, name, tm=1024, tn=1024, tk=1024, out_blocks=0):
    k_dim, m_dim = a.shape if ta else a.shape[::-1]
    b_rows, b_cols = (b.shape[0], b.shape[1]) if b.ndim == 2 else (b.shape[1], b.shape[0] * b.shape[2])
    n_dim = b_rows if tb else b_cols
    assert (b_cols if tb else b_rows) == k_dim, (a.shape, b.shape, ta, tb)
    limit_n = n_dim // out_blocks if out_blocks else (n_dim if b.ndim == 2 or tb else b.shape[2])
    limit_k = b.shape[2] if (b.ndim == 3 and tb) else k_dim
    tm, tn, tk = _tile(m_dim, tm), _tile(limit_n, min(tn, limit_n)), _tile(limit_k, min(tk, limit_k))
    a_spec = (pl.BlockSpec((tk, tm), lambda i, j, k: (k, i)) if ta else pl.BlockSpec((tm, tk), lambda i, j, k: (i, k)))
    if b.ndim == 3:
        b_spec = (_col_blocked_spec(tn, tk, b.shape[2], 1, 2) if tb else _col_blocked_spec(tk, tn, b.shape[2], 2, 1))
    else:
        b_spec = (pl.BlockSpec((tn, tk), lambda i, j, k: (j, k)) if tb
                  else pl.BlockSpec((tk, tn), lambda i, j, k: (k, j)))
    o_spec = (_col_blocked_spec(tm, tn, n_dim // out_blocks, 0, 1) if out_blocks
              else pl.BlockSpec((tm, tn), lambda i, j, k: (i, j)))
    o_shape = (out_blocks, m_dim, n_dim // out_blocks) if out_blocks else (m_dim, n_dim)
    dn = (((0 if ta else 1,), (1 if tb else 0,)), ((), ()))
    has_acc = acc is not None

    def body(*refs):
        a_ref, b_ref = refs[0], refs[1]
        o_ref = refs[-1]
        k = pl.program_id(2)
        p = lax.dot_general(a_ref[...].astype(BF16), b_ref[...].astype(BF16), dn,
                            preferred_element_type=F32)

        @pl.when(k == 0)
        def _():
            o_ref[...] = p + refs[2][...] if has_acc else p

        @pl.when(k > 0)
        def _():
            o_ref[...] += p

    return pl.pallas_call(
        body, name=name,
        out_shape=jax.ShapeDtypeStruct(o_shape, F32),
        grid=(m_dim // tm, n_dim // tn, k_dim // tk),
        in_specs=[a_spec, b_spec] + ([o_spec] if has_acc else []),
        out_specs=o_spec,
        compiler_params=_params(("parallel", "parallel", "arbitrary")),
    )(*([a, b] + ([acc] if has_acc else [])))


def _mm_relu2(a, b, *, name, tm=1024):
    m_dim, k_dim = a.shape
    n_blocks, _, tn = b.shape
    n_dim = n_blocks * tn
    tm = _tile(m_dim, tm)

    def body(a_ref, b_ref, h_ref, act_ref):
        h = jnp.dot(a_ref[...].astype(BF16), b_ref[...].astype(BF16), preferred_element_type=F32)
        h_ref[...] = h
        r = jnp.maximum(h, 0.0)
        act_ref[...] = (r * r).astype(BF16)

    o_spec = pl.BlockSpec((tm, tn), lambda i, j: (i, j))
    return pl.pallas_call(
        body, name=name,
        out_shape=(jax.ShapeDtypeStruct((m_dim, n_dim), F32), jax.ShapeDtypeStruct((m_dim, n_dim), BF16)),
        grid=(m_dim // tm, n_dim // tn),
        in_specs=[pl.BlockSpec((tm, k_dim), lambda i, j: (i, 0)),
                  pl.BlockSpec((None, k_dim, tn), lambda i, j: (j, 0, 0))],
        out_specs=(o_spec, o_spec),
        compiler_params=_params(("parallel", "parallel")),
    )(a, b)


def _mm_relu2_grad(d, b, h, *, name, tm=1024, tn=1024):
    m_dim, k_dim = d.shape
    n_dim = b.shape[0]
    tm, tn = _tile(m_dim, tm), _tile(n_dim, tn)

    def body(d_ref, b_ref, h_ref, o_ref):
        p = lax.dot_general(d_ref[...].astype(BF16), b_ref[...].astype(BF16), ((NT), ((), ())),
                            preferred_element_type=F32)
        o_ref[...] = (p * (2.0 * jnp.maximum(h_ref[...], 0.0))).astype(BF16)

    o_spec = pl.BlockSpec((tm, tn), lambda i, j: (i, j))
    return pl.pallas_call(
        body, name=name,
        out_shape=jax.ShapeDtypeStruct((m_dim, n_dim), BF16),
        grid=(m_dim // tm, n_dim // tn),
        in_specs=[pl.BlockSpec((tm, k_dim), lambda i, j: (i, 0)), pl.BlockSpec((tn, k_dim), lambda i, j: (j, 0)), o_spec],
        out_specs=o_spec,
        compiler_params=_params(("parallel", "parallel")),
    )(d, b, h)


def _rowwise(f, rows, params, row_out, acc_out, *, tm, name):
    length = rows[0].shape[0]
    tm = _row_tile(length, tm)
    nr, npar, nro = len(rows), len(params), len(row_out)

    def body(*refs):
        ins = [r[...] for r in refs[:nr + npar]]
        outs = refs[nr + npar:]
        r_o, a_o = f(*ins)
        for ref, val in zip(outs[:nro], r_o):
            ref[...] = val.astype(ref.dtype)
        i = pl.program_id(0)
        for ref, val in zip(outs[nro:], a_o):
            @pl.when(i == 0)
            def _(ref=ref, val=val):
                ref[...] = val.astype(ref.dtype)

            @pl.when(i > 0)
            def _(ref=ref, val=val):
                ref[...] += val.astype(ref.dtype)

    in_specs = ([pl.BlockSpec((tm, r.shape[1]), lambda i: (i, 0)) for r in rows]
                + [pl.BlockSpec(p.shape, lambda i: (0, 0)) for p in params])
    out_specs = ([pl.BlockSpec((tm, w), lambda i: (i, 0)) for w, _ in row_out]
                 + [pl.BlockSpec(s, lambda i: (0, 0)) for s, _ in acc_out])
    out_shape = ([jax.ShapeDtypeStruct((length, w), dt) for w, dt in row_out]
                 + [jax.ShapeDtypeStruct(s, dt) for s, dt in acc_out])
    res = pl.pallas_call(
        body, name=name, out_shape=out_shape, grid=(length // tm,),
        in_specs=in_specs, out_specs=out_specs,
        compiler_params=_params(("arbitrary",) if acc_out else ("parallel",)),
    )(*rows, *params)
    return res[:nro], res[nro:]


def _rw_fwd(f, rows, params, *, tm, name):
    tm_ = _row_tile(rows[0].shape[0], tm)
    shapes = jax.eval_shape(f, *[jax.ShapeDtypeStruct((tm_, r.shape[1]), r.dtype) for r in rows],
                            *[jax.ShapeDtypeStruct(p.shape, p.dtype) for p in params])
    row_out = [(s.shape[1], s.dtype) for s in shapes]
    outs, _ = _rowwise(lambda *v: (f(*v), ()), rows, params, row_out, [], tm=tm, name=name)
    return outs


def _rw_bwd(f, rows, params, cots, *, row_grad, param_grad, tm, name):
    nr, npar, nct = len(rows), len(params), len(cots)

    def g(*vals):
        prim = vals[:nr] + vals[nr + nct:]
        ct = vals[nr:nr + nct]
        _, vjp = jax.vjp(f, *prim)
        grads = vjp(tuple(ct))
        return (tuple(grads[i] for i in range(nr) if row_grad[i]),
                tuple(grads[nr + i] for i in range(npar) if param_grad[i]))

    row_out = [(rows[i].shape[1], F32) for i in range(nr) if row_grad[i]]
    acc_out = [(params[i].shape, F32) for i in range(npar) if param_grad[i]]
    return _rowwise(g, list(rows) + list(cots), params, row_out, acc_out, tm=tm, name=name)


def _f_ln_res(x, h, g, b):
    pre = DN_ALPHA * x + h
    mu = jnp.mean(pre, axis=-1, keepdims=True)
    d = pre - mu
    var = jnp.mean(d * d, axis=-1, keepdims=True)
    return (d * lax.rsqrt(var + LN_EPS) * g + b,)


def _silu(t):
    return t * jax.nn.sigmoid(t)


def _f_gdn_qkv(c):
    a = _silu(c)
    outs = []
    for part, scale in ((0, HEAD_DIM ** -0.5), (1, 1.0)):
        heads = []
        for h in range(GDN_HEADS):
            t = a[:, part * D_MODEL + h * HEAD_DIM: part * D_MODEL + (h + 1) * HEAD_DIM]
            t = t * lax.rsqrt(jnp.sum(t * t, axis=-1, keepdims=True) + 1e-6)
            heads.append(t * scale if scale != 1.0 else t)
        outs.append(jnp.concatenate(heads, axis=-1))
    outs.append(a[:, 2 * D_MODEL:])
    return tuple(outs)


def _f_gdn_out(o, z, norm_g):
    heads = []
    for h in range(GDN_HEADS):
        t = o[:, h * HEAD_DIM:(h + 1) * HEAD_DIM]
        t = t * lax.rsqrt(jnp.mean(t * t, axis=-1, keepdims=True) + RMS_EPS) * norm_g
        heads.append(t)
    return (jnp.concatenate(heads, axis=-1) * _silu(z),)


def _f_attn(xq, kmem, vmem):
    heads = []
    for h in range(XA_HEADS):
        sl = slice(h * HEAD_DIM, (h + 1) * HEAD_DIM)
        s = lax.dot_general(xq[:, sl].astype(BF16), kmem[:, sl].astype(BF16),
                            (((1,), (1,)), ((), ())), preferred_element_type=F32) * (HEAD_DIM ** -0.5)
        m = lax.stop_gradient(jnp.max(s, axis=-1, keepdims=True))
        e = jnp.exp(s - m)
        p = e / jnp.sum(e, axis=-1, keepdims=True)
        heads.append(jnp.dot(p.astype(BF16), vmem[:, sl].astype(BF16), preferred_element_type=F32))
    return (jnp.concatenate(heads, axis=-1),)


def _f_s5_gelu(y, u, d):
    return (jax.nn.gelu(y + d * u),)


def _f_s5_gate(zg, t, b):
    return (zg * jax.nn.sigmoid(t + b),)


def _f_add(a, b):
    return (a + b,)


def _f_add4(a, b, c, d):
    return (((a + b.astype(F32)) + c.astype(F32)) + d.astype(F32),)


def _f_adamw(w, g, m, v):
    m = ADAM_B1 * m + (1.0 - ADAM_B1) * g
    v = ADAM_B2 * v + (1.0 - ADAM_B2) * jnp.square(g)
    m_hat = m / (1.0 - ADAM_B1 ** ADAM_STEP)
    v_hat = v / (1.0 - ADAM_B2 ** ADAM_STEP)
    delta = -ADAM_LR * (m_hat / (jnp.sqrt(v_hat) + ADAM_EPS) + ADAM_WD * w)
    return delta, m, v


def _conv_fwd(u, w, *, tm, name):
    length, chans = u.shape
    tm = min(tm, length)
    tc = _tile(chans, 1024)
    hb = tm // SUBLANES

    def body(cur_ref, prev_ref, w_ref, o_ref, buf):
        i = pl.program_id(1)
        buf[0:SUBLANES, :] = jnp.where(i > 0, prev_ref[...], 0.0)
        buf[SUBLANES:, :] = cur_ref[...]
        acc = buf[pl.ds(SUBLANES - 3, tm), :] * w_ref[0:1, :]
        for k in range(1, GDN_CONV):
            acc = acc + buf[pl.ds(SUBLANES - 3 + k, tm), :] * w_ref[k:k + 1, :]
        o_ref[...] = acc

    return pl.pallas_call(
        body, name=name, out_shape=jax.ShapeDtypeStruct(u.shape, F32),
        grid=(chans // tc, length // tm),
        in_specs=[pl.BlockSpec((tm, tc), lambda j, i: (i, j)),
                  pl.BlockSpec((SUBLANES, tc), lambda j, i: (jnp.maximum(i * hb - 1, 0), j)),
                  pl.BlockSpec((GDN_CONV, tc), lambda j, i: (0, j))],
        out_specs=pl.BlockSpec((tm, tc), lambda j, i: (i, j)),
        scratch_shapes=[pltpu.VMEM((tm + SUBLANES, tc), F32)],
        compiler_params=_params(("parallel", "parallel")),
    )(u, u, w)


def _conv_bwd(u, w, dc, *, tm, name):
    length, chans = u.shape
    tm = min(tm, length)
    tc = _tile(chans, 1024)
    hb = tm // SUBLANES
    last = length // tm - 1

    def body(u_ref, uprev_ref, dc_ref, dcnext_ref, w_ref, du_ref, dw_ref, ubuf, dbuf):
        i = pl.program_id(1)
        ubuf[0:SUBLANES, :] = jnp.where(i > 0, uprev_ref[...], 0.0)
        ubuf[SUBLANES:, :] = u_ref[...]
        dbuf[0:tm, :] = dc_ref[...]
        dbuf[tm:, :] = jnp.where(i < last, dcnext_ref[...], 0.0)
        dcv = dc_ref[...]
        du = dbuf[pl.ds(3, tm), :] * w_ref[0:1, :]
        rows = [jnp.sum(dcv * ubuf[pl.ds(SUBLANES - 3, tm), :], axis=0, keepdims=True)]
        for k in range(1, GDN_CONV):
            du = du + dbuf[pl.ds(3 - k, tm), :] * w_ref[k:k + 1, :]
            rows.append(jnp.sum(dcv * ubuf[pl.ds(SUBLANES - 3 + k, tm), :], axis=0, keepdims=True))
        du_ref[...] = du
        dwv = jnp.concatenate(rows, axis=0)

        @pl.when(i == 0)
        def _():
            dw_ref[...] = dwv

        @pl.when(i > 0)
        def _():
            dw_ref[...] += dwv

    return pl.pallas_call(
        body, name=name,
        out_shape=(jax.ShapeDtypeStruct(u.shape, F32), jax.ShapeDtypeStruct((GDN_CONV, chans), F32)),
        grid=(chans // tc, length // tm),
        in_specs=[pl.BlockSpec((tm, tc), lambda j, i: (i, j)),
                  pl.BlockSpec((SUBLANES, tc), lambda j, i: (jnp.maximum(i * hb - 1, 0), j)),
                  pl.BlockSpec((tm, tc), lambda j, i: (i, j)),
                  pl.BlockSpec((SUBLANES, tc), lambda j, i: (jnp.minimum((i + 1) * hb, (last + 1) * hb - 1), j)),
                  pl.BlockSpec((GDN_CONV, tc), lambda j, i: (0, j))],
        out_specs=(pl.BlockSpec((tm, tc), lambda j, i: (i, j)),
                   pl.BlockSpec((GDN_CONV, tc), lambda j, i: (0, j))),
        scratch_shapes=[pltpu.VMEM((tm + SUBLANES, tc), F32), pltpu.VMEM((tm + SUBLANES, tc), F32)],
        compiler_params=_params(("parallel", "arbitrary")),
    )(u, u, dc, dc, w)


def _dot(a, b, dims, precision=None):
    if precision is None:
        a, b = a.astype(BF16), b.astype(BF16)
    return lax.dot_general(a, b, (dims, ((), ())), preferred_element_type=F32, precision=precision)


def _dot3(a, b, dims):
    ah, bh = a.astype(BF16), b.astype(BF16)
    al, bl = (a - ah.astype(F32)).astype(BF16), (b - bh.astype(F32)).astype(BF16)

    def d(x, y):
        return lax.dot_general(x, y, (dims, ((), ())), preferred_element_type=F32)

    return d(ah, bh) + (d(ah, bl) + d(al, bh))


NN = ((1,), (0,))
NT = ((1,), (1,))
TN = ((0,), (0,))
HI = lax.Precision.HIGHEST


def _hmap(f, *lists):
    return [f(*t) for t in zip(*lists)]


@jax.custom_vjp
def _unit_lower_inverse(a):
    c = a[0].shape[0]
    eye = (lax.broadcasted_iota(jnp.int32, (c, c), 0) == lax.broadcasted_iota(jnp.int32, (c, c), 1)).astype(F32)
    p = _hmap(lambda x: -x, a)
    t = _hmap(lambda x: eye + x, p)
    for _ in range(int(math.log2(c)) - 1):
        p = _hmap(lambda x: _dot3(x, x, NN), p)
        t = _hmap(lambda x, y: x + _dot3(x, y, NN), t, p)
    return t


def _uli_fwd(a):
    t = _unit_lower_inverse(a)
    return t, t


def _uli_bwd(t, dt):
    left = _hmap(lambda x, y: _dot3(x, y, TN), t, dt)
    return (_hmap(lambda x, y: -_dot3(x, y, NT), left, t),)


_unit_lower_inverse.defvjp(_uli_fwd, _uli_bwd)


def _gdn_chunk(q, k, v, bl, al, a_log, dt_bias, state):
    c = q[0].shape[0]
    row = lax.broadcasted_iota(jnp.int32, (c, c), 0)
    col = lax.broadcasted_iota(jnp.int32, (c, c), 1)
    causal = row >= col
    strict = row > col
    eye = (row == col).astype(F32)
    beta = _hmap(jax.nn.sigmoid, bl)
    g = _hmap(lambda a_, l_, d_: -jnp.exp(a_) * jax.nn.softplus(l_ + d_), a_log, al, dt_bias)
    g_r = _hmap(lambda x: jnp.sum(eye * x, axis=0, keepdims=True), g)
    gc = _hmap(lambda x: jnp.sum(jnp.where(causal, x, 0.0), axis=1, keepdims=True), g_r)
    gc_r = _hmap(lambda x: jnp.sum(jnp.where(row <= col, x, 0.0), axis=0, keepdims=True), g)
    decay = _hmap(lambda x, y: jnp.where(causal, jnp.exp(jnp.where(causal, x - y, 0.0)), 0.0), gc, gc_r)
    e_gc = _hmap(jnp.exp, gc)
    kb = _hmap(jnp.multiply, k, beta)
    vb = _hmap(jnp.multiply, v, beta)
    a_mat = _hmap(lambda x, y, d: jnp.where(strict, _dot(x, y, NT) * d, 0.0), kb, k, decay)
    t_inv = _unit_lower_inverse(a_mat)
    u_blk = _hmap(lambda t, x: _dot(t, x, NN), t_inv, vb)
    w_blk = _hmap(lambda t, x, e: _dot(t, x * e, NN), t_inv, kb, e_gc)
    v_new = _hmap(lambda u, w, s: u - _dot(w, s, NN), u_blk, w_blk, state)
    attn = _hmap(lambda x, y, d: _dot(x, y, NT) * d, q, k, decay)
    o_state = _hmap(lambda x, e, s: _dot(x * e, s, NN), q, e_gc, state)
    o = _hmap(lambda base, at, vn: base + _dot(at, vn, NN), o_state, attn, v_new)
    g_last = _hmap(lambda x: jnp.sum(x, axis=0, keepdims=True), g)
    k_dec = _hmap(lambda x, gl, c_: x * jnp.exp(gl - c_), k, g_last, gc)
    new_state = _hmap(lambda s, gl, kd, vn: s * jnp.exp(gl) + _dot(kd, vn, TN), state, g_last, k_dec, v_new)
    return o, new_state


def _gdn_operands(q_ref, k_ref, v_ref, bav, alog_ref, dtb_ref):
    hs = range(GDN_HEADS)
    cols = [slice(h * HEAD_DIM, (h + 1) * HEAD_DIM) for h in hs]
    return ([q_ref[:, sl] for sl in cols], [k_ref[:, sl] for sl in cols], [v_ref[:, sl] for sl in cols],
            [bav[:, h:h + 1] for h in hs], [bav[:, h + GDN_HEADS:h + GDN_HEADS + 1] for h in hs],
            [alog_ref[h:h + 1, 0:1] for h in hs], [dtb_ref[h:h + 1, 0:1] for h in hs])


def _gdn_scan_fwd(q, k, v, ba, a_log, dt_bias, *, name):
    length = q.shape[0]
    n = length // GDN_CHUNK
    c = GDN_CHUNK

    def body(q_ref, k_ref, v_ref, ba_ref, alog_ref, dtb_ref, o_ref, s_ref, state):
        i = pl.program_id(0)

        @pl.when(i == 0)
        def _():
            state[...] = jnp.zeros_like(state)

        bav = ba_ref[...]
        heads = [slice(h * HEAD_DIM, (h + 1) * HEAD_DIM) for h in range(GDN_HEADS)]
        s_in = [state[h] for h in range(GDN_HEADS)]
        o, s_out = _gdn_chunk(*_gdn_operands(q_ref, k_ref, v_ref, bav, alog_ref, dtb_ref), s_in)
        for h, sl in enumerate(heads):
            s_ref[h] = s_in[h]
            o_ref[:, sl] = o[h]
            state[h] = s_out[h]

    row_spec = pl.BlockSpec((c, D_MODEL), lambda i: (i, 0))
    small = pl.BlockSpec((GDN_HEADS, LANES), lambda i: (0, 0))
    return pl.pallas_call(
        body, name=name,
        out_shape=(jax.ShapeDtypeStruct((length, D_MODEL), F32),
                   jax.ShapeDtypeStruct((n, GDN_HEADS, HEAD_DIM, HEAD_DIM), F32)),
        grid=(n,),
        in_specs=[row_spec, row_spec, row_spec, pl.BlockSpec((c, LANES), lambda i: (i, 0)), small, small],
        out_specs=(row_spec, pl.BlockSpec((None, GDN_HEADS, HEAD_DIM, HEAD_DIM), lambda i: (i, 0, 0, 0))),
        scratch_shapes=[pltpu.VMEM((GDN_HEADS, HEAD_DIM, HEAD_DIM), F32)],
        compiler_params=_params(("arbitrary",)),
    )(q, k, v, ba, a_log, dt_bias)


def _gdn_scan_bwd(q, k, v, ba, a_log, dt_bias, states, do, *, name):
    length = q.shape[0]
    n = length // GDN_CHUNK
    c = GDN_CHUNK

    def body(q_ref, k_ref, v_ref, ba_ref, alog_ref, dtb_ref, s_ref, do_ref,
             dq_ref, dk_ref, dv_ref, dba_ref, dalog_ref, ddtb_ref, dstate):
        i = pl.program_id(0)

        @pl.when(i == 0)
        def _():
            dstate[...] = jnp.zeros_like(dstate)
            dalog_ref[...] = jnp.zeros_like(dalog_ref)
            ddtb_ref[...] = jnp.zeros_like(ddtb_ref)

        bav = ba_ref[...]
        lane = lax.broadcasted_iota(jnp.int32, (c, LANES), 1)
        sub8 = lax.broadcasted_iota(jnp.int32, (GDN_HEADS, LANES), 0)
        lane8 = lax.broadcasted_iota(jnp.int32, (GDN_HEADS, LANES), 1)
        slab = jnp.zeros((c, LANES), F32)
        dalog_all = jnp.zeros((GDN_HEADS, LANES), F32)
        ddtb_all = jnp.zeros((GDN_HEADS, LANES), F32)
        heads = [slice(h * HEAD_DIM, (h + 1) * HEAD_DIM) for h in range(GDN_HEADS)]
        ds_in = [dstate[h] for h in range(GDN_HEADS)]
        s_in = [s_ref[h] for h in range(GDN_HEADS)]
        _, vjp = jax.vjp(_gdn_chunk, *_gdn_operands(q_ref, k_ref, v_ref, bav, alog_ref, dtb_ref), s_in)
        dq, dk, dv, dbl, dal, dalog, ddtb, ds = vjp(([do_ref[:, sl] for sl in heads], ds_in))
        for h, sl in enumerate(heads):
            dq_ref[:, sl] = dq[h]
            dk_ref[:, sl] = dk[h]
            dv_ref[:, sl] = dv[h]
            dstate[h] = ds[h]
            slab = slab + jnp.where(lane == h, dbl[h], 0.0) + jnp.where(lane == h + GDN_HEADS, dal[h], 0.0)
            here = (sub8 == h) & (lane8 == 0)
            dalog_all = dalog_all + jnp.where(here, dalog[h], 0.0)
            ddtb_all = ddtb_all + jnp.where(here, ddtb[h], 0.0)
        dba_ref[...] = slab
        dalog_ref[...] += dalog_all
        ddtb_ref[...] += ddtb_all

    row_spec = pl.BlockSpec((c, D_MODEL), lambda i: (n - 1 - i, 0))
    small = pl.BlockSpec((GDN_HEADS, LANES), lambda i: (0, 0))
    return pl.pallas_call(
        body, name=name,
        out_shape=(jax.ShapeDtypeStruct((length, D_MODEL), F32),) * 3
        + (jax.ShapeDtypeStruct((length, LANES), F32),
           jax.ShapeDtypeStruct((GDN_HEADS, LANES), F32), jax.ShapeDtypeStruct((GDN_HEADS, LANES), F32)),
        grid=(n,),
        in_specs=[row_spec, row_spec, row_spec,
                  pl.BlockSpec((c, LANES), lambda i: (n - 1 - i, 0)), small, small,
                  pl.BlockSpec((None, GDN_HEADS, HEAD_DIM, HEAD_DIM), lambda i: (n - 1 - i, 0, 0, 0)),
                  row_spec],
        out_specs=(row_spec, row_spec, row_spec,
                   pl.BlockSpec((c, LANES), lambda i: (n - 1 - i, 0)), small, small),
        scratch_shapes=[pltpu.VMEM((GDN_HEADS, HEAD_DIM, HEAD_DIM), F32)],
        compiler_params=_params(("arbitrary",)),
    )(q, k, v, ba, a_log, dt_bias, states, do)


S5_W = S5_T * LANES
S5_S = 2 * 8 * S5_STATE
S5_SH = S5_S // 2


def _iota2(shape):
    return lax.broadcasted_iota(jnp.int32, shape, 0), lax.broadcasted_iota(jnp.int32, shape, 1)


def _s5_rep_t(t, dtype):
    row, col = _iota2((S5_T * S5_GROUP, LANES))
    return ((jnp.right_shift(row, 4) == t) & (jnp.bitwise_and(row, 15) == jnp.bitwise_and(col, 15))).astype(dtype)


def _s5_rep_state(dtype):
    row, col = _iota2((2 * S5_STATE, S5_S))
    return ((jnp.right_shift(row, 6) == jnp.right_shift(col, 9))
            & (jnp.bitwise_and(row, 63) == jnp.bitwise_and(col, 63))).astype(dtype)


def _s5_masks():
    row, col = _iota2((LANES, LANES))
    m_ab = jnp.right_shift(row, 4) == jnp.right_shift(col, 4)
    row, col = _iota2((S5_S, LANES))
    m_e = jnp.bitwise_and(jnp.right_shift(row, 6), 7) == jnp.right_shift(col, 4)
    row, col = _iota2((LANES, S5_S))
    m_f = jnp.right_shift(row, 4) == jnp.bitwise_and(jnp.right_shift(col, 6), 7)
    return m_ab, m_e, m_f


def _s5_expand(kx_ref, ec_ref, fc_ref, kb_scr, e_scr, f_scr):
    m_ab, m_e, m_f = _s5_masks()
    kx = kx_ref[...].astype(BF16)
    ec = ec_ref[...].astype(BF16)
    rep_state = _s5_rep_state(BF16)
    for t in range(S5_T):
        rep = _s5_rep_t(t, BF16)
        cols = slice(t * LANES, (t + 1) * LANES)
        kb_scr[t] = jnp.where(m_ab, jnp.dot(kx, rep, preferred_element_type=F32), 0.0).astype(BF16)
        e_scr[:, cols] = jnp.where(m_e, jnp.dot(ec, rep, preferred_element_type=F32), 0.0).astype(BF16)
        f_scr[cols, :] = jnp.where(m_f, jnp.dot(fc_ref[t].astype(BF16), rep_state, preferred_element_type=F32),
                                   0.0).astype(BF16)


def _s5_token_rows(ref, n):
    return [ref[pl.ds(t, n, stride=S5_T), :].astype(BF16) for t in range(S5_T)]


def _s5_scan_fwd(u, kx, ec, fc, at, *, name):
    length = u.shape[0]
    n = length // S5_T
    assert n % SUBLANES == 0

    def body(u_ref, kx_ref, ec_ref, fc_ref, at_ref, y_ref, h_ref, kb_scr, e_scr, f_scr, g_scr):
        _s5_expand(kx_ref, ec_ref, fc_ref, kb_scr, e_scr, f_scr)
        us = _s5_token_rows(u_ref, n)
        g_scr[...] = jnp.dot(jnp.concatenate(us, axis=1), f_scr[...], preferred_element_type=F32)
        ar, ai = at_ref[:, :S5_SH], at_ref[:, S5_SH:]

        def step(blk, h):
            base = pl.multiple_of(blk * SUBLANES, SUBLANES)
            g8 = g_scr[pl.ds(base, SUBLANES), :]
            rows = []
            for r in range(SUBLANES):
                rows.append(h)
                hr, hi = h[:, :S5_SH], h[:, S5_SH:]
                h = jnp.concatenate([ar * hr - ai * hi, ar * hi + ai * hr], axis=1) + g8[r:r + 1, :]
            h_ref[pl.ds(base, SUBLANES), :] = jnp.concatenate(rows, axis=0)
            return h

        lax.fori_loop(0, n // SUBLANES, step, jnp.zeros((1, S5_S), F32))
        hb = h_ref[...].astype(BF16)
        for t in range(S5_T):
            acc = jnp.dot(hb, e_scr[:, t * LANES:(t + 1) * LANES], preferred_element_type=F32)
            for s in range(t + 1):
                acc = acc + jnp.dot(us[s], kb_scr[t - s], preferred_element_type=F32)
            y_ref[pl.ds(t, n, stride=S5_T), :] = acc

    return pl.pallas_call(
        body, name=name,
        out_shape=(jax.ShapeDtypeStruct((length, D_MODEL), F32), jax.ShapeDtypeStruct((S5_TILES, n, S5_S), F32)),
        grid=(S5_TILES,),
        in_specs=[pl.BlockSpec((length, LANES), lambda k: (0, k)), _s5_spec(LANES, S5_T * S5_GROUP),
                  _s5_spec(S5_S, S5_T * S5_GROUP), _s5_spec(S5_T, LANES, LANES), _s5_spec(1, S5_S)],
        out_specs=(pl.BlockSpec((length, LANES), lambda k: (0, k)), _s5_spec(n, S5_S)),
        scratch_shapes=[pltpu.VMEM((S5_T, LANES, LANES), BF16), pltpu.VMEM((S5_S, S5_W), BF16),
                        pltpu.VMEM((S5_W, S5_S), BF16), pltpu.VMEM((n, S5_S), F32)],
        compiler_params=_params(("parallel",)),
    )(u, kx, ec, fc, at)


def _s5_spec(*tail):
    return pl.BlockSpec((None,) + tail, lambda k: (k,) + (0,) * len(tail))


def _s5_scan_bwd(dy, kx, ec, fc, at, hs, *, name):
    length = dy.shape[0]
    n = length // S5_T

    def body(dy_ref, kx_ref, ec_ref, fc_ref, at_ref, h_ref, du_ref, dg_ref, dat_ref, kb_scr, e_scr, f_scr, dh_scr):
        _s5_expand(kx_ref, ec_ref, fc_ref, kb_scr, e_scr, f_scr)
        dys = _s5_token_rows(dy_ref, n)
        dh_scr[...] = _dot(jnp.concatenate(dys, axis=1), e_scr[...], NT)
        ar, ai = at_ref[:, :S5_SH], at_ref[:, S5_SH:]

        def step(it, carry):
            cy, dat = carry
            base = pl.multiple_of((n // SUBLANES - 1 - it) * SUBLANES, SUBLANES)
            dh8 = dh_scr[pl.ds(base, SUBLANES), :]
            h8 = h_ref[pl.ds(base, SUBLANES), :]
            rows = [None] * SUBLANES
            for r in reversed(range(SUBLANES)):
                rows[r] = cy
                cr, ci = cy[:, :S5_SH], cy[:, S5_SH:]
                hr, hi = h8[r:r + 1, :S5_SH], h8[r:r + 1, S5_SH:]
                dat = dat + jnp.concatenate([cr * hr + ci * hi, ci * hr - cr * hi], axis=1)
                cy = dh8[r:r + 1, :] + jnp.concatenate([ar * cr + ai * ci, ar * ci - ai * cr], axis=1)
            dg_ref[pl.ds(base, SUBLANES), :] = jnp.concatenate(rows, axis=0)
            return cy, dat

        zero = jnp.zeros((1, S5_S), F32)
        _, dat = lax.fori_loop(0, n // SUBLANES, step, (zero, zero))
        dat_ref[...] = dat
        dgb = dg_ref[...].astype(BF16)
        for s in range(S5_T):
            acc = _dot(dgb, f_scr[s * LANES:(s + 1) * LANES, :], NT)
            for t in range(s, S5_T):
                acc = acc + _dot(dys[t], kb_scr[t - s], NT)
            du_ref[pl.ds(s, n, stride=S5_T), :] = acc

    row_spec = pl.BlockSpec((length, LANES), lambda k: (0, k))
    return pl.pallas_call(
        body, name=name,
        out_shape=(jax.ShapeDtypeStruct((length, D_MODEL), F32), jax.ShapeDtypeStruct((S5_TILES, n, S5_S), F32),
                   jax.ShapeDtypeStruct((S5_TILES, 1, S5_S), F32)),
        grid=(S5_TILES,),
        in_specs=[row_spec, _s5_spec(LANES, S5_T * S5_GROUP), _s5_spec(S5_S, S5_T * S5_GROUP),
                  _s5_spec(S5_T, LANES, LANES), _s5_spec(1, S5_S), _s5_spec(n, S5_S)],
        out_specs=(row_spec, _s5_spec(n, S5_S), _s5_spec(1, S5_S)),
        scratch_shapes=[pltpu.VMEM((S5_T, LANES, LANES), BF16), pltpu.VMEM((S5_S, S5_W), BF16),
                        pltpu.VMEM((S5_W, S5_S), BF16), pltpu.VMEM((n, S5_S), F32)],
        compiler_params=_params(("parallel",)),
    )(dy, kx, ec, fc, at, hs)


def _s5_operator_grads(dy, u, hs, dg, *, name):
    length = u.shape[0]
    n = length // S5_T

    def body(dy_ref, u_ref, h_ref, dg_ref, dkx_ref, dec_ref, dfc_ref):
        dys = _s5_token_rows(dy_ref, n)
        us = _s5_token_rows(u_ref, n)
        ucat = jnp.concatenate(us, axis=1)
        m_ab, m_e, m_f = _s5_masks()
        hb = h_ref[...].astype(BF16)
        dgb = dg_ref[...].astype(BF16)
        lane = lax.broadcasted_iota(jnp.int32, (1, LANES), 1)
        lane_group = jnp.right_shift(lane, 4)

        def own_block(x, mask):
            x = jnp.where(mask, x, 0.0)
            for shift in (64, 32, 16):
                x = x + pltpu.roll(x, shift, 1)
            return x

        def place(halves, t, x):
            halves[t // 8] = jnp.where(lane_group == t % 8, x, halves[t // 8])

        dkb = [jnp.zeros((LANES, LANES), F32) for _ in range(S5_T)]
        dec = [jnp.zeros((S5_S, LANES), F32) for _ in range(2)]
        for t in range(S5_T):
            d_t = _dot(ucat, dys[t], TN)
            for s in range(t + 1):
                dkb[t - s] = dkb[t - s] + d_t[s * LANES:(s + 1) * LANES, :]
            place(dec, t, own_block(_dot(hb, dys[t], TN), m_e))
            wide = jnp.where(m_f, _dot(us[t], dgb, TN), 0.0)
            parts = []
            for r in range(2):
                acc = wide[:, r * S5_SH:r * S5_SH + LANES]
                for q in range(1, S5_SH // LANES):
                    acc = acc + wide[:, r * S5_SH + q * LANES:r * S5_SH + (q + 1) * LANES]
                parts.append(acc + pltpu.roll(acc, S5_STATE, 1))
            dfc_ref[t] = jnp.where(lane < S5_STATE, parts[0], parts[1])
        dkx = [jnp.zeros((LANES, LANES), F32) for _ in range(2)]
        for t in range(S5_T):
            place(dkx, t, own_block(dkb[t], m_ab))
        dkx_ref[...] = jnp.concatenate(dkx, axis=1)
        dec_ref[...] = jnp.concatenate(dec, axis=1)

    row_spec = pl.BlockSpec((length, LANES), lambda k: (0, k))
    outs = (_s5_spec(LANES, S5_T * S5_GROUP), _s5_spec(S5_S, S5_T * S5_GROUP), _s5_spec(S5_T, LANES, LANES))
    return pl.pallas_call(
        body, name=name,
        out_shape=(jax.ShapeDtypeStruct((S5_TILES, LANES, S5_T * S5_GROUP), F32),
                   jax.ShapeDtypeStruct((S5_TILES, S5_S, S5_T * S5_GROUP), F32),
                   jax.ShapeDtypeStruct((S5_TILES, S5_T, LANES, LANES), F32)),
        grid=(S5_TILES,),
        in_specs=[row_spec, row_spec, _s5_spec(n, S5_S), _s5_spec(n, S5_S)],
        out_specs=outs,
        compiler_params=_params(("parallel",)),
    )(dy, u, hs, dg)


def _s5_prep(a_re, a_im, b_re, b_im, c_re, c_im, log_dt):
    t_len, tiles = S5_T, S5_TILES
    dt = jnp.exp(log_dt)[:, None]
    mag = jnp.exp(a_re * dt)
    ab_re, ab_im = mag * jnp.cos(a_im * dt), mag * jnp.sin(a_im * dt)
    den = jnp.square(a_re) + jnp.square(a_im)
    n_re, n_im = ab_re - 1.0, ab_im
    f_re = (n_re * a_re + n_im * a_im) / den
    f_im = (n_im * a_re - n_re * a_im) / den
    bb_re = f_re[..., None] * b_re - f_im[..., None] * b_im
    bb_im = f_re[..., None] * b_im + f_im[..., None] * b_re
    def powers(exponents):
        e = exponents[:, None, None]
        m = jnp.exp(e * (a_re * dt))
        return m * jnp.cos(e * (a_im * dt)), m * jnp.sin(e * (a_im * dt))

    p_re, p_im = powers(jnp.arange(t_len + 1, dtype=F32))
    rev_re, rev_im = powers((t_len - 1) - jnp.arange(t_len, dtype=F32))
    ca_re = c_re[None] * p_re[:, :, None, :] - c_im[None] * p_im[:, :, None, :]
    ca_im = c_re[None] * p_im[:, :, None, :] + c_im[None] * p_re[:, :, None, :]
    lag = (jnp.einsum('tgip,gpj->tgij', ca_re[:t_len], bb_re, precision=HI)
           - jnp.einsum('tgip,gpj->tgij', ca_im[:t_len], bb_im, precision=HI))
    kx = lag.reshape(t_len, tiles, 8, S5_GROUP, S5_GROUP).transpose(1, 2, 4, 0, 3)
    kx = kx.reshape(tiles, LANES, t_len * S5_GROUP)
    e_st = jnp.stack([ca_re[1:], -ca_im[1:]])
    e_st = e_st.reshape(2, t_len, tiles, 8, S5_GROUP, S5_STATE).transpose(2, 0, 3, 5, 1, 4)
    ec = e_st.reshape(tiles, S5_S, t_len * S5_GROUP)
    ab_b = jnp.stack([rev_re[..., None] * bb_re[None] - rev_im[..., None] * bb_im[None],
                      rev_re[..., None] * bb_im[None] + rev_im[..., None] * bb_re[None]])
    ab_b = ab_b.reshape(2, t_len, tiles, 8, S5_STATE, S5_GROUP).transpose(2, 1, 3, 5, 0, 4)
    fc = ab_b.reshape(tiles, t_len, LANES, 2 * S5_STATE)
    a_t = jnp.stack([p_re[t_len], p_im[t_len]]).reshape(2, tiles, 8 * S5_STATE).transpose(1, 0, 2)
    return kx, ec, fc, a_t.reshape(tiles, 1, S5_S)


TM_ROW = 256


def _gdn_fwd(x, w, tag):
    qkv = _mm(x, w["wqkv"], name="gdn_proj_qkv")
    z = _mm(x, w["wz"], name="gdn_proj_z")
    ba = _mm(x, w["wba"], name="gdn_proj_ba")
    cv = _conv_fwd(qkv, w["conv_w"], tm=TM_ROW, name="gdn_conv")
    q, k, v = _rw_fwd(_f_gdn_qkv, [cv], [], tm=TM_ROW, name="gdn_qkv")
    o, states = _gdn_scan_fwd(q, k, v, ba, w["a_log8"], w["dt_bias8"], name="gdn_scan")
    (mix,) = _rw_fwd(_f_gdn_out, [o, z], [w["norm_g"]], tm=TM_ROW, name="gdn_out")
    return mix, (qkv, z, ba, cv, q, k, v, states, o)


def _gdn_bwd(x, w, saved, dmix, dx_acc):
    qkv, z, ba, cv, q, k, v, states, o = saved
    (do, dz), (dnorm_g,) = _rw_bwd(_f_gdn_out, [o, z], [w["norm_g"]], [dmix], row_grad=[1, 1], param_grad=[1],
                                   tm=TM_ROW, name="gdn_out_bwd")
    dq, dk, dv, dba, dalog, ddtb = _gdn_scan_bwd(q, k, v, ba, w["a_log8"], w["dt_bias8"], states, do,
                                                  name="gdn_scan_bwd")
    (dcv,), _ = _rw_bwd(_f_gdn_qkv, [cv], [], [dq, dk, dv], row_grad=[1], param_grad=[], tm=TM_ROW,
                        name="gdn_qkv_bwd")
    dqkv, dconv_w = _conv_bwd(qkv, w["conv_w"], dcv, tm=TM_ROW, name="gdn_conv_bwd")
    dx = _mm(dqkv, w["wqkv"], tb=True, acc=dx_acc, name="gdn_dx_qkv")
    dx = _mm(dz, w["wz"], tb=True, acc=dx, name="gdn_dx_z")
    dx = _mm(dba, w["wba"], tb=True, acc=dx, name="gdn_dx_ba")
    grads = dict(wqkv=_mm(x, dqkv, ta=True, name="gdn_dw_qkv"), wz=_mm(x, dz, ta=True, name="gdn_dw_z"),
                 wba=_mm(x, dba, ta=True, name="gdn_dw_ba"), conv_w=dconv_w,
                 a_log=dalog[:, 0], dt_bias=ddtb[:, 0], norm_g=dnorm_g[0])
    return dx, grads


def _s5_fwd(x, w, tag):
    u = _mm(x, w["wu"], name="s5_proj_u")
    y, hs = _s5_scan_fwd(u, w["kx"], w["ec"], w["fc"], w["a_t"], name="s5_scan")
    (zg,) = _rw_fwd(_f_s5_gelu, [y, u], [w["d"]], tm=TM_ROW, name="s5_gelu")
    t = _mm(zg, w["w_glu"], name="s5_glu")
    (mix,) = _rw_fwd(_f_s5_gate, [zg, t], [w["b_glu"]], tm=TM_ROW, name="s5_gate")
    return mix, (u, hs, y, zg, t)


def _s5_bwd(x, w, saved, dmix, dx_acc):
    u, hs, y, zg, t = saved
    (dzg, dt), (db_glu,) = _rw_bwd(_f_s5_gate, [zg, t], [w["b_glu"]], [dmix], row_grad=[1, 1], param_grad=[1],
                                   tm=TM_ROW, name="s5_gate_bwd")
    dzg = _mm(dt, w["w_glu"], tb=True, acc=dzg, name="s5_dzg")
    dw_glu = _mm(zg, dt, ta=True, name="s5_dw_glu")
    (dy, du), (dd,) = _rw_bwd(_f_s5_gelu, [y, u], [w["d"]], [dzg], row_grad=[1, 1], param_grad=[1],
                              tm=TM_ROW, name="s5_gelu_bwd")
    du_scan, dg, dat = _s5_scan_bwd(dy, w["kx"], w["ec"], w["fc"], w["a_t"], hs, name="s5_scan_bwd")
    dkx, dec, dfc = _s5_operator_grads(dy, u, hs, dg, name="s5_operator_grads")
    (du,) = _rw_fwd(_f_add, [du, du_scan], [], tm=TM_ROW, name="s5_du_add")
    d_a_re, d_a_im, d_b_re, d_b_im, d_c_re, d_c_im, d_log_dt = w["prep_vjp"]((dkx, dec, dfc, dat))
    dx = _mm(du, w["wu"], tb=True, acc=dx_acc, name="s5_dx_u")
    grads = dict(wu=_mm(x, du, ta=True, name="s5_dw_u"), w_glu=dw_glu, b_glu=db_glu[0], d=dd[0],
                 a_re=d_a_re, a_im=d_a_im, b_re=d_b_re, b_im=d_b_im, c_re=d_c_re, c_im=d_c_im, log_dt=d_log_dt)
    return dx, grads


def _layer_fwd(x, mem, w, is_gdn):
    mix, msave = (_gdn_fwd if is_gdn else _s5_fwd)(x, w, "")
    xq = _mm(x, w["wxq"], name="proj_xq")
    kv = _mm(mem, w["wkv"], name="mem_kv")
    kmem, vmem = kv[:, :XA_DIM], kv[:, XA_DIM:]
    (cross,) = _rw_fwd(_f_attn, [xq], [kmem, vmem], tm=TM_ROW, name="attn")
    h = _mm(mix, w["wo_mix"], name="wo_mix")
    h = _mm(cross, w["wo_cross"], acc=h, name="wo_cross")
    (x1,) = _rw_fwd(_f_ln_res, [x, h], [w["ln1_g"], w["ln1_b"]], tm=TM_ROW, name="ln_res")
    hm, act = _mm_relu2(x1, w["w1"], name="mlp_up")
    f = _mm(act, w["w2"], name="mlp_down")
    (x2,) = _rw_fwd(_f_ln_res, [x1, f], [w["ln2_g"], w["ln2_b"]], tm=TM_ROW, name="ln_res")
    return x2, (x, msave, xq, kmem, vmem, mix, cross, h, x1, hm, act, f)


def _layer_bwd(mem, w, is_gdn, saved, dx2):
    x, msave, xq, kmem, vmem, mix, cross, h, x1, hm, act, f = saved
    (dx1, df), (dg2, db2) = _rw_bwd(_f_ln_res, [x1, f], [w["ln2_g"], w["ln2_b"]], [dx2], row_grad=[1, 1],
                                    param_grad=[1, 1], tm=TM_ROW, name="ln_res_bwd")
    dhm = _mm_relu2_grad(df, w["w2"], hm, name="mlp_dhm")
    dw2 = _mm(act, df, ta=True, name="mlp_dw2")
    dx1 = _mm(dhm, w["w1"], tb=True, acc=dx1, name="mlp_dx")
    dw1 = _mm(x1, dhm, ta=True, out_blocks=N_CHIPS, name="mlp_dw1")
    (dx, dh), (dg1, db1) = _rw_bwd(_f_ln_res, [x, h], [w["ln1_g"], w["ln1_b"]], [dx1], row_grad=[1, 1],
                                   param_grad=[1, 1], tm=TM_ROW, name="ln_res_bwd")
    dmix = _mm(dh, w["wo_mix"], tb=True, name="wo_dmix")
    dcross = _mm(dh, w["wo_cross"], tb=True, name="wo_dcross")
    dwo = jnp.concatenate([_mm(mix, dh, ta=True, name="wo_dw_mix"), _mm(cross, dh, ta=True, name="wo_dw_cross")], 0)
    (dxq,), (dkmem, dvmem) = _rw_bwd(_f_attn, [xq], [kmem, vmem], [dcross], row_grad=[1], param_grad=[1, 1],
                                     tm=TM_ROW, name="attn_bwd")
    dwkv = _mm(mem, jnp.concatenate([dkmem, dvmem], axis=1), ta=True, name="mem_dw_kv")
    dx = _mm(dxq, w["wxq"], tb=True, acc=dx, name="dx_xq")
    dwxq = _mm(x, dxq, ta=True, name="dw_xq")
    dx, mg = (_gdn_bwd if is_gdn else _s5_bwd)(x, w, msave, dmix, dx)
    grads = dict(mixer=mg, wxq=dwxq, wkv=dwkv, wo=dwo, w1=dw1, w2=dw2,
                 ln1_g=dg1[0], ln1_b=db1[0], ln2_g=dg2[0], ln2_b=db2[0])
    return dx, grads


def _loss_and_grad(y, target):
    def f(yv, tv):
        err = yv - tv
        return (err * (1.0 / D_MODEL),), (0.5 / D_MODEL * jnp.sum(err * err, axis=0, keepdims=True),)

    (dy,), (part,) = _rowwise(f, [y, target], [], [(D_MODEL, F32)], [((1, D_MODEL), F32)], tm=512, name="loss")
    return jnp.sum(part), dy


def _layer_weights(full, i):
    j = i // 2
    w = dict(wkv=full["w_kv_mem"][i].astype(BF16),
             wo_mix=full["w_o"][i, :D_MODEL].astype(BF16), wo_cross=full["w_o"][i, D_MODEL:].astype(BF16),
             ln1_g=full["ln1_g"][i][None], ln1_b=full["ln1_b"][i][None],
             ln2_g=full["ln2_g"][i][None], ln2_b=full["ln2_b"][i][None],
             w1=full["mlp_w1"][:, i].astype(BF16), w2=full["mlp_w2"][i].astype(BF16))
    if i % 2 == 0:
        w_in = full["gdn_w_in"][j]
        gd = 3 * D_MODEL
        w.update(wqkv=w_in[:, :gd].astype(BF16), wz=w_in[:, gd:gd + D_MODEL].astype(BF16),
                 wba=jnp.pad(w_in[:, gd + D_MODEL:gd + D_MODEL + 2 * GDN_HEADS],
                             ((0, 0), (0, LANES - 2 * GDN_HEADS))).astype(BF16),
                 wxq=w_in[:, gd + D_MODEL + 2 * GDN_HEADS:].astype(BF16),
                 conv_w=full["gdn_conv_w"][j],
                 a_log8=jnp.broadcast_to(full["gdn_a_log"][j][:, None], (GDN_HEADS, LANES)),
                 dt_bias8=jnp.broadcast_to(full["gdn_dt_bias"][j][:, None], (GDN_HEADS, LANES)),
                 norm_g=full["gdn_norm_g"][j][None])
    else:
        w_in = full["s5_w_in"][j]
        (kx, ec, fc, a_t), prep_vjp = jax.vjp(
            _s5_prep, full["s5_a_re"][j], full["s5_a_im"][j], full["s5_b_re"][j], full["s5_b_im"][j],
            full["s5_c_re"][j], full["s5_c_im"][j], full["s5_log_dt"][j])
        w.update(wu=w_in[:, :D_MODEL].astype(BF16), wxq=w_in[:, D_MODEL:].astype(BF16),
                 kx=kx, ec=ec, fc=fc, a_t=a_t, prep_vjp=prep_vjp,
                 d=full["s5_d"][j][None], w_glu=full["s5_w_glu"][j].astype(BF16), b_glu=full["s5_b_glu"][j][None])
    return w


def _grads_by_weight(layer_grads):
    g = layer_grads
    gdn = [g[i] for i in range(DEPTH) if i % 2 == 0]
    s5 = [g[i] for i in range(DEPTH) if i % 2 == 1]
    out = dict(
        w_kv_mem=[l["wkv"] for l in g], w_o=[l["wo"] for l in g],
        ln1_g=jnp.stack([l["ln1_g"] for l in g]), ln1_b=jnp.stack([l["ln1_b"] for l in g]),
        ln2_g=jnp.stack([l["ln2_g"] for l in g]), ln2_b=jnp.stack([l["ln2_b"] for l in g]),
        mlp_w1=[l["w1"] for l in g], mlp_w2=[l["w2"] for l in g],
        gdn_w_in=[jnp.concatenate([l["mixer"]["wqkv"], l["mixer"]["wz"],
                                   l["mixer"]["wba"][:, :2 * GDN_HEADS], l["wxq"]], axis=1) for l in gdn],
        gdn_conv_w=[l["mixer"]["conv_w"] for l in gdn],
        gdn_a_log=jnp.stack([l["mixer"]["a_log"] for l in gdn]),
        gdn_dt_bias=jnp.stack([l["mixer"]["dt_bias"] for l in gdn]),
        gdn_norm_g=jnp.stack([l["mixer"]["norm_g"] for l in gdn]),
        s5_w_in=[jnp.concatenate([l["mixer"]["wu"], l["wxq"]], axis=1) for l in s5],
        s5_d=[l["mixer"]["d"] for l in s5],
        s5_w_glu=[l["mixer"]["w_glu"] for l in s5],
        s5_b_glu=[l["mixer"]["b_glu"] for l in s5])
    for n in ("a_re", "a_im", "b_re", "b_im", "c_re", "c_im", "log_dt"):
        out["s5_" + n] = jnp.stack([l["mixer"][n] for l in s5])
    return out


def _local_step(x, mem, target, full):
    lw = [_layer_weights(full, i) for i in range(DEPTH)]
    saves = []
    h = x
    for i in range(DEPTH):
        h, s = _layer_fwd(h, mem, lw[i], i % 2 == 0)
        saves.append(s)
    loss, d = _loss_and_grad(h, target)
    grads = [None] * DEPTH
    for i in reversed(range(DEPTH)):
        d, grads[i] = _layer_bwd(mem, lw[i], i % 2 == 0, saves[i], d)
    return loss, d, _grads_by_weight(grads)


ANY = pl.BlockSpec(memory_space=pl.ANY)
SHARD_ROWS = 1024
SMALL_ROWS = 128


def _place():
    return lax.axis_index("x"), lax.axis_index("y"), lax.axis_index("c")


def _other_chips(x, y):
    return [(1 - x, y), (x, 1 - y), (1 - x, 1 - y)]


def _all_gather_chips(wpack, *, name):
    rows = wpack.shape[0]
    half = rows // 2

    def body(w_ref, out_ref, send_sems, recv_sems):
        x, y, c = _place()
        sibling = (x, y, 1 - c)
        chips = _other_chips(x, y)

        def blk(cx, cy, cc):
            return out_ref.at[2 * cx + cy, pl.ds(cc * half, half), :]

        def copy(k, src, dst, to):
            return pltpu.make_async_remote_copy(src_ref=src, dst_ref=dst, send_sem=send_sems.at[k],
                                                recv_sem=recv_sems.at[k], device_id=to, device_id_type=MESH)

        first = [copy(j, w_ref.at[pl.ds(c * half, half), :], blk(x, y, c), (cx, cy, c))
                 for j, (cx, cy) in enumerate(chips)]
        for cp in first:
            cp.start()
        passed = [copy(3 + j, blk(cx, cy, c), blk(cx, cy, c), sibling) for j, (cx, cy) in enumerate(chips)]
        for j, (cx, cy) in enumerate(chips):
            copy(j, blk(cx, cy, c), blk(cx, cy, c), (cx, cy, c)).wait_recv()
            passed[j].start()
        for j, (cx, cy) in enumerate(chips):
            copy(3 + j, blk(cx, cy, 1 - c), blk(cx, cy, 1 - c), sibling).wait_recv()
        for cp in first + passed:
            cp.wait_send()

    return pl.pallas_call(
        body, name=name, out_shape=jax.ShapeDtypeStruct((N_CHIPS, rows, D_MODEL), wpack.dtype),
        in_specs=[ANY], out_specs=ANY,
        scratch_shapes=[pltpu.SemaphoreType.DMA((6,)), pltpu.SemaphoreType.DMA((6,))],
    )(wpack)


def _sibling_swap(buf, *, name):
    def body(in_ref, out_ref, send_sem, recv_sem):
        x, y, c = _place()
        cp = pltpu.make_async_remote_copy(src_ref=in_ref, dst_ref=out_ref, send_sem=send_sem, recv_sem=recv_sem,
                                          device_id=(x, y, 1 - c), device_id_type=MESH)
        cp.start()
        cp.wait()

    return pl.pallas_call(
        body, name=name, out_shape=jax.ShapeDtypeStruct(buf.shape, buf.dtype), in_specs=[ANY], out_specs=ANY,
        scratch_shapes=[pltpu.SemaphoreType.DMA, pltpu.SemaphoreType.DMA],
    )(buf)


def _pair_exchange(gpack, *, name):
    pieces, rows, width = gpack.shape
    half = rows // 2

    def body(in_ref, got_ref, send_sems, recv_sems):
        x, y, c = _place()
        sends = [pltpu.make_async_remote_copy(src_ref=in_ref.at[p, pl.ds((1 - c) * half, half), :],
                                              dst_ref=got_ref.at[p], send_sem=send_sems.at[p],
                                              recv_sem=recv_sems.at[p], device_id=(x, y, 1 - c), device_id_type=MESH)
                 for p in range(pieces)]
        for cp in sends:
            cp.start()
        for cp in sends:
            cp.wait()

    return pl.pallas_call(
        body, name=name, out_shape=jax.ShapeDtypeStruct((pieces, half, width), gpack.dtype),
        in_specs=[ANY], out_specs=ANY,
        scratch_shapes=[pltpu.SemaphoreType.DMA((pieces,)), pltpu.SemaphoreType.DMA((pieces,))],
    )(gpack)


def _pair_add(gpack, got, c, *, name, tm=512):
    pieces, rows, width = gpack.shape
    half = rows // 2
    nb = half // tm

    def body(c_ref, a_ref, b_ref, sum_ref, narrow_ref):
        s = a_ref[...] + b_ref[...]
        sum_ref[...] = s
        narrow_ref[...] = s.astype(BF16)

    blk = pl.BlockSpec((None, tm, width), lambda p, i, c_ref: (p, i, 0))
    return pl.pallas_call(
        body, name=name,
        out_shape=(jax.ShapeDtypeStruct((pieces, half, width), F32), jax.ShapeDtypeStruct((pieces, half, width), BF16)),
        grid_spec=pltpu.PrefetchScalarGridSpec(
            num_scalar_prefetch=1, grid=(pieces, nb),
            in_specs=[pl.BlockSpec((None, tm, width), lambda p, i, c_ref: (p, c_ref[0] * nb + i, 0)), blk],
            out_specs=(blk, blk)),
        compiler_params=_params(("parallel", "parallel")),
    )(c, gpack, got)


def _chip_exchange(pieces, *, name):
    _, rows, width = pieces.shape

    def body(in_ref, out_ref, send_sems, recv_sems):
        x, y, c = _place()
        cps = [pltpu.make_async_remote_copy(src_ref=in_ref.at[2 * cx + cy], dst_ref=out_ref.at[j],
                                            send_sem=send_sems.at[j], recv_sem=recv_sems.at[j],
                                            device_id=(cx, cy, c), device_id_type=MESH)
               for j, (cx, cy) in enumerate(_other_chips(x, y))]
        for cp in cps:
            cp.start()
        for cp in cps:
            cp.wait()

    return pl.pallas_call(
        body, name=name, out_shape=jax.ShapeDtypeStruct((3, rows, width), pieces.dtype), in_specs=[ANY], out_specs=ANY,
        scratch_shapes=[pltpu.SemaphoreType.DMA((3,)), pltpu.SemaphoreType.DMA((3,))],
    )(pieces)


def _all_reduce_small(v, *, name):
    rows, width = v.shape

    def body(in_ref, out_ref, gath, send_sems, recv_sems):
        x, y, c = _place()
        me = 4 * x + 2 * y + c
        gath[me] = in_ref[...]
        peers = []
        for m in range(1, N_DEV):
            px = 1 - x if m & 4 else x
            py = 1 - y if m & 2 else y
            pc = 1 - c if m & 1 else c
            peers.append((m - 1, (px, py, pc), 4 * px + 2 * py + pc))
        for k, peer, _ in peers:
            pltpu.make_async_remote_copy(src_ref=in_ref, dst_ref=gath.at[me], send_sem=send_sems.at[k],
                                         recv_sem=recv_sems.at[k], device_id=peer, device_id_type=MESH).start()
        for k, peer, plin in peers:
            cp = pltpu.make_async_remote_copy(src_ref=in_ref, dst_ref=gath.at[plin], send_sem=send_sems.at[k],
                                              recv_sem=recv_sems.at[k], device_id=peer, device_id_type=MESH)
            cp.wait_send()
            cp.wait_recv()
        acc = gath[0]
        for d in range(1, N_DEV):
            acc = acc + gath[d]
        out_ref[...] = acc

    vmem = pl.BlockSpec(memory_space=pltpu.VMEM)
    return pl.pallas_call(
        body, name=name, out_shape=jax.ShapeDtypeStruct(v.shape, v.dtype), in_specs=[vmem], out_specs=vmem,
        scratch_shapes=[pltpu.VMEM((N_DEV, rows, width), v.dtype),
                        pltpu.SemaphoreType.DMA((N_DEV - 1,)), pltpu.SemaphoreType.DMA((N_DEV - 1,))],
        compiler_params=pltpu.CompilerParams(vmem_limit_bytes=VMEM_LIMIT_V7X),
    )(v)


def _reduce_scatter(gpack):
    x, y, c = _place()
    half = gpack.shape[1] // 2
    got = _pair_exchange(gpack, name="rs_pair_swap")
    pair, pair16 = _pair_add(gpack, got, c.astype(jnp.int32).reshape(1), name="rs_pair_add")
    recv = _chip_exchange(pair16, name="rs_chip_exchange")
    mine = lax.dynamic_index_in_dim(pair, 2 * x + y, axis=0, keepdims=False)
    (total,) = _rw_fwd(_f_add4, [mine, recv[0], recv[1], recv[2]], [], tm=512, name="rs_chip_add")
    theirs = _sibling_swap(total, name="rs_share_swap")
    return jnp.concatenate([jnp.where(c == 0, total, theirs), jnp.where(c == 0, theirs, total)], axis=0)


_SHARDED = (("w_kv_mem", 1), ("w_o", 1), ("mlp_w1", 2), ("mlp_w2", 1), ("gdn_w_in", 2), ("gdn_conv_w", 2),
            ("s5_w_in", 2), ("s5_d", 1), ("s5_w_glu", 1), ("s5_b_glu", 1))
_MATMUL_ONLY = ("w_kv_mem", "w_o", "mlp_w1", "mlp_w2", "gdn_w_in", "s5_w_in", "s5_w_glu")
_KEPT_BLOCKED = ("mlp_w1",)
_REPLICATED = ("ln1_g", "ln1_b", "ln2_g", "ln2_b", "gdn_a_log", "gdn_dt_bias", "gdn_norm_g", "s5_a_re", "s5_a_im",
               "s5_b_re", "s5_b_im", "s5_c_re", "s5_c_im", "s5_log_dt")
_WEIGHTS = ("w_kv_mem", "w_o", "ln1_g", "ln1_b", "ln2_g", "ln2_b", "mlp_w1", "mlp_w2", "gdn_w_in", "gdn_conv_w",
            "gdn_a_log", "gdn_dt_bias", "gdn_norm_g", "s5_w_in", "s5_a_re", "s5_a_im", "s5_b_re", "s5_b_im",
            "s5_c_re", "s5_c_im", "s5_log_dt", "s5_d", "s5_w_glu", "s5_b_glu")


ROW_ALIGN = 16


def _n_rows(shape):
    return -(-math.prod(shape) // (ROW_ALIGN * D_MODEL)) * ROW_ALIGN


def _as_rows(a):
    rows = _n_rows(a.shape)
    if a.shape[-1] == D_MODEL and a.size == rows * D_MODEL:
        return a.reshape(-1, D_MODEL)
    flat = a.reshape(-1)
    return jnp.pad(flat, (0, rows * D_MODEL - flat.size)).reshape(rows, D_MODEL)


def _pack(arrs, unit_rows=SHARD_ROWS):
    rows = [_as_rows(a) for a in arrs]
    pad = -sum(r.shape[0] for r in rows) % unit_rows
    if pad:
        rows.append(jnp.zeros((pad, D_MODEL), rows[0].dtype))
    return jnp.concatenate(rows, axis=0)


def _unpack(packed, shapes):
    lead = packed.shape[:-2]
    out, off = [], 0
    for s in shapes:
        r = _n_rows(s)
        seg = lax.slice_in_dim(packed, off, off + r, axis=len(lead))
        if s[-1] != D_MODEL or math.prod(s) != r * D_MODEL:
            seg = lax.slice_in_dim(seg.reshape(lead + (-1,)), 0, math.prod(s), axis=len(lead))
        out.append(seg.reshape(lead + tuple(s)))
        off += r
    return out


def _split3(t):
    hi = t.astype(BF16)
    r1 = t - hi.astype(F32)
    mid = r1.astype(BF16)
    lo = (r1 - mid.astype(F32)).astype(BF16)
    return jnp.stack([hi, mid, lo], axis=-1)


def _join3(t):
    return (t[..., 0].astype(F32) + t[..., 1].astype(F32)) + t[..., 2].astype(F32)


def _merge_chips(blocks, axis):
    return jnp.concatenate([blocks[s] for s in range(N_CHIPS)], axis=axis)


def _pack_for_chips(weights):
    rows = []
    for s in range(N_CHIPS):
        chip = []
        for layers, axis in weights:
            if axis is None:
                blocks = [g[s] for g in layers]
            else:
                n = layers[0].shape[axis] // N_CHIPS
                blocks = [lax.slice_in_dim(g, s * n, (s + 1) * n, axis=axis) for g in layers]
            if math.prod(blocks[0].shape) % (ROW_ALIGN * D_MODEL) == 0:
                chip += [_as_rows(b) for b in blocks]
            else:
                chip.append(_as_rows(jnp.stack(blocks)))
        pad = -sum(r.shape[0] for r in chip) % SHARD_ROWS
        rows += chip + ([jnp.zeros((pad, D_MODEL), F32)] if pad else [])
    return jnp.concatenate(rows, axis=0).reshape(N_CHIPS, -1, D_MODEL)


def kernel(x, mem, w_kv_mem, w_o, ln1_g, ln1_b, ln2_g, ln2_b, mlp_w1, mlp_w2, gdn_w_in, gdn_conv_w, gdn_a_log, gdn_dt_bias, gdn_norm_g, s5_w_in, s5_a_re, s5_a_im, s5_b_re, s5_b_im, s5_c_re, s5_c_im, s5_log_dt, s5_d, s5_w_glu, s5_b_glu, loss_target, m_w_kv_mem, m_w_o, m_ln1_g, m_ln1_b, m_ln2_g, m_ln2_b, m_mlp_w1, m_mlp_w2, m_gdn_w_in, m_gdn_conv_w, m_gdn_a_log, m_gdn_dt_bias, m_gdn_norm_g, m_s5_w_in, m_s5_a_re, m_s5_a_im, m_s5_b_re, m_s5_b_im, m_s5_c_re, m_s5_c_im, m_s5_log_dt, m_s5_d, m_s5_w_glu, m_s5_b_glu, v_w_kv_mem, v_w_o, v_ln1_g, v_ln1_b, v_ln2_g, v_ln2_b, v_mlp_w1, v_mlp_w2, v_gdn_w_in, v_gdn_conv_w, v_gdn_a_log, v_gdn_dt_bias, v_gdn_norm_g, v_s5_w_in, v_s5_a_re, v_s5_a_im, v_s5_b_re, v_s5_b_im, v_s5_c_re, v_s5_c_im, v_s5_log_dt, v_s5_d, v_s5_w_glu, v_s5_b_glu):
    given = dict(locals())
    w = {n: given[n] for n in _WEIGHTS}
    mom = {n: given["m_" + n] for n in _WEIGHTS}
    var = {n: given["v_" + n] for n in _WEIGHTS}
    shard_names = [n for n, _ in _SHARDED]
    shard_shapes = [w[n].shape for n in shard_names]
    rep_shapes = [w[n].shape for n in _REPLICATED]

    wire = [w[n].astype(BF16) if n in _MATMUL_ONLY else _split3(w[n]) for n in shard_names]
    wire_pack = _pack(wire)
    gathered = _all_gather_chips(wire_pack, name="gather_weights")
    me_chip = 2 * lax.axis_index("x") + lax.axis_index("y")
    gathered = lax.dynamic_update_index_in_dim(gathered, wire_pack, me_chip, axis=0)
    blocks = _unpack(gathered, [a.shape for a in wire])
    full = {n: blk if n in _KEPT_BLOCKED else _merge_chips(blk if n in _MATMUL_ONLY else _join3(blk), ax)
            for (n, ax), blk in zip(_SHARDED, blocks)}
    full.update({n: w[n] for n in _REPLICATED})

    loss, grad_x, grads = _local_step(x[0], mem[0], loss_target[0], full)
    loss = lax.psum(loss, ("x", "y", "c"))

    g_shard = _reduce_scatter(_pack_for_chips([(grads[n], None if n in _KEPT_BLOCKED else ax - 1)
                                               for n, ax in _SHARDED]))
    def pack_small(d):
        return _pack([d[n] for n in _REPLICATED], unit_rows=SMALL_ROWS)

    g_rep = _all_reduce_small(pack_small(grads), name="reduce_replicated")

    def adamw(wp, gp, mp, vp, name):
        return _rw_fwd(_f_adamw, [wp, gp, mp, vp], [], tm=256, name=name)

    outs = {}
    for n, g in zip(shard_names, _unpack(g_shard, shard_shapes)):
        flat = (-1, w[n].shape[-1])
        res = adamw(w[n].reshape(flat), g.reshape(flat), mom[n].reshape(flat), var[n].reshape(flat), "adamw_" + n)
        outs[("grad", n)] = g
        outs.update({(kind, n): a.reshape(w[n].shape) for kind, a in zip(("delta", "new_m", "new_v"), res)})
    packed = (g_rep,) + tuple(adamw(pack_small(w), g_rep, pack_small(mom), pack_small(var), "adamw_replicated"))
    for kind, pr in zip(("grad", "delta", "new_m", "new_v"), packed):
        outs.update({(kind, n): a for n, a in zip(_REPLICATED, _unpack(pr, rep_shapes))})
    return (loss, grad_x[None]) + tuple(outs[(kind, n)] for kind in ("grad", "delta", "new_m", "new_v")
                                        for n in _WEIGHTS)
```

```python
import functools
import math

import jax
import jax.numpy as jnp
from jax import lax
from jax.experimental import pallas as pl
from jax.experimental.pallas import tpu as pltpu

F32 = jnp.float32
BF16 = jnp.bfloat16
MESH = pl.DeviceIdType.MESH

D_MODEL = 1024
DEPTH = 4
GDN_HEADS = 8
HEAD_DIM = 128
GDN_CONV = 4
GDN_CHUNK = 64
S5_GROUPS = 64
S5_GROUP = 16
S5_STATE = 64
XA_HEADS = 4
XA_DIM = 512
D_FF = 4096
DN_ALPHA = (2 * DEPTH) ** 0.25
LN_EPS = 1e-5
RMS_EPS = 1e-6
ADAM_LR, ADAM_B1, ADAM_B2, ADAM_EPS, ADAM_WD, ADAM_STEP = 0.001, 0.9, 0.999, 1e-08, 0.01, 10

VMEM_LIMIT_V7X = 56 * 1024 * 1024
LANES = 128
SUBLANES = 8
S5_T = 16
S5_TILES = D_MODEL // LANES
N_CHIPS = 4
N_DEV = 8


def _params(sem):
    return pltpu.CompilerParams(dimension_semantics=sem, vmem_limit_bytes=VMEM_LIMIT_V7X)


def _tile(n, pref):
    if n <= pref:
        return n
    t = (pref // LANES) * LANES
    while n % t:
        t -= LANES
    return t


def _row_tile(n, pref):
    if n % SUBLANES:
        return n
    t = min(pref, n) // SUBLANES * SUBLANES
    while n % t:
        t -= SUBLANES
    return t


def _col_blocked_spec(rows_tile, cols_tile, block_cols, rows_axis, cols_axis):
    r = block_cols // cols_tile

    def index(*ijk):
        c = ijk[cols_axis]
        return (c, ijk[rows_axis], 0) if r == 1 else (c // r, ijk[rows_axis], c % r)

    return pl.BlockSpec((None, rows_tile, cols_tile), index)


def _mm(a, b, *, ta=False, tb=False, acc=None, name, tm=1024, tn=1024, tk=1024, out_blocks=0):
    k_dim, m_dim = a.shape if ta else a.shape[::-1]
    b_rows, b_cols = (b.shape[0], b.shape[1]) if b.ndim == 2 else (b.shape[1], b.shape[0] * b.shape[2])
    n_dim = b_rows if tb else b_cols
    assert (b_cols if tb else b_rows) == k_dim, (a.shape, b.shape, ta, tb)
    limit_n = n_dim // out_blocks if out_blocks else (n_dim if b.ndim == 2 or tb else b.shape[2])
    limit_k = b.shape[2] if (b.ndim == 3 and tb) else k_dim
    tm, tn, tk = _tile(m_dim, tm), _tile(limit_n, min(tn, limit_n)), _tile(limit_k, min(tk, limit_k))
    a_spec = (pl.BlockSpec((tk, tm), lambda i, j, k: (k, i)) if ta else pl.BlockSpec((tm, tk), lambda i, j, k: (i, k)))
    if b.ndim == 3:
        b_spec = (_col_blocked_spec(tn, tk, b.shape[2], 1, 2) if tb else _col_blocked_spec(tk, tn, b.shape[2], 2, 1))
    else:
        b_spec = (pl.BlockSpec((tn, tk), lambda i, j, k: (j, k)) if tb
                  else pl.BlockSpec((tk, tn), lambda i, j, k: (k, j)))
    o_spec = (_col_blocked_spec(tm, tn, n_dim // out_blocks, 0, 1) if out_blocks
              else pl.BlockSpec((tm, tn), lambda i, j, k: (i, j)))
    o_shape = (out_blocks, m_dim, n_dim // out_blocks) if out_blocks else (m_dim, n_dim)
    dn = (((0 if ta else 1,), (1 if tb else 0,)), ((), ()))
    has_acc = acc is not None

    def body(*refs):
        a_ref, b_ref = refs[0], refs[1]
        o_ref = refs[-1]
        k = pl.program_id(2)
        p = lax.dot_general(a_ref[...].astype(BF16), b_ref[...].astype(BF16), dn,
                            preferred_element_type=F32)

        @pl.when(k == 0)
        def _():
            o_ref[...] = p + refs[2][...] if has_acc else p

        @pl.when(k > 0)
        def _():
            o_ref[...] += p

    return pl.pallas_call(
        body, name=name,
        out_shape=jax.ShapeDtypeStruct(o_shape, F32),
        grid=(m_dim // tm, n_dim // tn, k_dim // tk),
        in_specs=[a_spec, b_spec] + ([o_spec] if has_acc else []),
        out_specs=o_spec,
        compiler_params=_params(("parallel", "parallel", "arbitrary")),
    )(*([a, b] + ([acc] if has_acc else [])))


def _mm_relu2(a, b, *, name, tm=1024):
    m_dim, k_dim = a.shape
    n_blocks, _, tn = b.shape
    n_dim = n_blocks * tn
    tm = _tile(m_dim, tm)

    def body(a_ref, b_ref, h_ref, act_ref):
        h = jnp.dot(a_ref[...].astype(BF16), b_ref[...].astype(BF16), preferred_element_type=F32)
        h_ref[...] = h
        r = jnp.maximum(h, 0.0)
        act_ref[...] = (r * r).astype(BF16)

    o_spec = pl.BlockSpec((tm, tn), lambda i, j: (i, j))
    return pl.pallas_call(
        body, name=name,
        out_shape=(jax.ShapeDtypeStruct((m_dim, n_dim), F32), jax.ShapeDtypeStruct((m_dim, n_dim), BF16)),
        grid=(m_dim // tm, n_dim // tn),
        in_specs=[pl.BlockSpec((tm, k_dim), lambda i, j: (i, 0)),
                  pl.BlockSpec((None, k_dim, tn), lambda i, j: (j, 0, 0))],
        out_specs=(o_spec, o_spec),
        compiler_params=_params(("parallel", "parallel")),
    )(a, b)


def _mm_relu2_grad(d, b, h, *, name, tm=1024, tn=1024):
    m_dim, k_dim = d.shape
    n_dim = b.shape[0]
    tm, tn = _tile(m_dim, tm), _tile(n_dim, tn)

    def body(d_ref, b_ref, h_ref, o_ref):
        p = lax.dot_general(d_ref[...].astype(BF16), b_ref[...].astype(BF16), ((NT), ((), ())),
                            preferred_element_type=F32)
        o_ref[...] = (p * (2.0 * jnp.maximum(h_ref[...], 0.0))).astype(BF16)

    o_spec = pl.BlockSpec((tm, tn), lambda i, j: (i, j))
    return pl.pallas_call(
        body, name=name,
        out_shape=jax.ShapeDtypeStruct((m_dim, n_dim), BF16),
        grid=(m_dim // tm, n_dim // tn),
        in_specs=[pl.BlockSpec((tm, k_dim), lambda i, j: (i, 0)), pl.BlockSpec((tn, k_dim), lambda i, j: (j, 0)), o_spec],
        out_specs=o_spec,
        compiler_params=_params(("parallel", "parallel")),
    )(d, b, h)


def _rowwise(f, rows, params, row_out, acc_out, *, tm, name):
    length = rows[0].shape[0]
    tm = _row_tile(length, tm)
    nr, npar, nro = len(rows), len(params), len(row_out)

    def body(*refs):
        ins = [r[...] for r in refs[:nr + npar]]
        outs = refs[nr + npar:]
        r_o, a_o = f(*ins)
        for ref, val in zip(outs[:nro], r_o):
            ref[...] = val.astype(ref.dtype)
        i = pl.program_id(0)
        for ref, val in zip(outs[nro:], a_o):
            @pl.when(i == 0)
            def _(ref=ref, val=val):
                ref[...] = val.astype(ref.dtype)

            @pl.when(i > 0)
            def _(ref=ref, val=val):
                ref[...] += val.astype(ref.dtype)

    in_specs = ([pl.BlockSpec((tm, r.shape[1]), lambda i: (i, 0)) for r in rows]
                + [pl.BlockSpec(p.shape, lambda i: (0, 0)) for p in params])
    out_specs = ([pl.BlockSpec((tm, w), lambda i: (i, 0)) for w, _ in row_out]
                 + [pl.BlockSpec(s, lambda i: (0, 0)) for s, _ in acc_out])
    out_shape = ([jax.ShapeDtypeStruct((length, w), dt) for w, dt in row_out]
                 + [jax.ShapeDtypeStruct(s, dt) for s, dt in acc_out])
    res = pl.pallas_call(
        body, name=name, out_shape=out_shape, grid=(length // tm,),
        in_specs=in_specs, out_specs=out_specs,
        compiler_params=_params(("arbitrary",) if acc_out else ("parallel",)),
    )(*rows, *params)
    return res[:nro], res[nro:]


def _rw_fwd(f, rows, params, *, tm, name):
    tm_ = _row_tile(rows[0].shape[0], tm)
    shapes = jax.eval_shape(f, *[jax.ShapeDtypeStruct((tm_, r.shape[1]), r.dtype) for r in rows],
                            *[jax.ShapeDtypeStruct(p.shape, p.dtype) for p in params])
    row_out = [(s.shape[1], s.dtype) for s in shapes]
    outs, _ = _rowwise(lambda *v: (f(*v), ()), rows, params, row_out, [], tm=tm, name=name)
    return outs


def _rw_bwd(f, rows, params, cots, *, row_grad, param_grad, tm, name):
    nr, npar, nct = len(rows), len(params), len(cots)

    def g(*vals):
        prim = vals[:nr] + vals[nr + nct:]
        ct = vals[nr:nr + nct]
        _, vjp = jax.vjp(f, *prim)
        grads = vjp(tuple(ct))
        return (tuple(grads[i] for i in range(nr) if row_grad[i]),
                tuple(grads[nr + i] for i in range(npar) if param_grad[i]))

    row_out = [(rows[i].shape[1], F32) for i in range(nr) if row_grad[i]]
    acc_out = [(params[i].shape, F32) for i in range(npar) if param_grad[i]]
    return _rowwise(g, list(rows) + list(cots), params, row_out, acc_out, tm=tm, name=name)


def _f_ln_res(x, h, g, b):
    pre = DN_ALPHA * x + h
    mu = jnp.mean(pre, axis=-1, keepdims=True)
    d = pre - mu
    var = jnp.mean(d * d, axis=-1, keepdims=True)
    return (d * lax.rsqrt(var + LN_EPS) * g + b,)


def _silu(t):
    return t * jax.nn.sigmoid(t)


def _f_gdn_qkv(c):
    a = _silu(c)
    outs = []
    for part, scale in ((0, HEAD_DIM ** -0.5), (1, 1.0)):
        heads = []
        for h in range(GDN_HEADS):
            t = a[:, part * D_MODEL + h * HEAD_DIM: part * D_MODEL + (h + 1) * HEAD_DIM]
            t = t * lax.rsqrt(jnp.sum(t * t, axis=-1, keepdims=True) + 1e-6)
            heads.append(t * scale if scale != 1.0 else t)
        outs.append(jnp.concatenate(heads, axis=-1))
    outs.append(a[:, 2 * D_MODEL:])
    return tuple(outs)


def _f_gdn_out(o, z, norm_g):
    heads = []
    for h in range(GDN_HEADS):
        t = o[:, h * HEAD_DIM:(h + 1) * HEAD_DIM]
        t = t * lax.rsqrt(jnp.mean(t * t, axis=-1, keepdims=True) + RMS_EPS) * norm_g
        heads.append(t)
    return (jnp.concatenate(heads, axis=-1) * _silu(z),)


def _f_attn(xq, kmem, vmem):
    heads = []
    for h in range(XA_HEADS):
        sl = slice(h * HEAD_DIM, (h + 1) * HEAD_DIM)
        s = lax.dot_general(xq[:, sl].astype(BF16), kmem[:, sl].astype(BF16),
                            (((1,), (1,)), ((), ())), preferred_element_type=F32) * (HEAD_DIM ** -0.5)
        m = lax.stop_gradient(jnp.max(s, axis=-1, keepdims=True))
        e = jnp.exp(s - m)
        p = e / jnp.sum(e, axis=-1, keepdims=True)
        heads.append(jnp.dot(p.astype(BF16), vmem[:, sl].astype(BF16), preferred_element_type=F32))
    return (jnp.concatenate(heads, axis=-1),)


def _f_s5_gelu(y, u, d):
    return (jax.nn.gelu(y + d * u),)


def _f_s5_gate(zg, t, b):
    return (zg * jax.nn.sigmoid(t + b),)


def _f_add(a, b):
    return (a + b,)


def _f_add4(a, b, c, d):
    return (((a + b.astype(F32)) + c.astype(F32)) + d.astype(F32),)


def _f_adamw(w, g, m, v):
    m = ADAM_B1 * m + (1.0 - ADAM_B1) * g
    v = ADAM_B2 * v + (1.0 - ADAM_B2) * jnp.square(g)
    m_hat = m / (1.0 - ADAM_B1 ** ADAM_STEP)
    v_hat = v / (1.0 - ADAM_B2 ** ADAM_STEP)
    delta = -ADAM_LR * (m_hat / (jnp.sqrt(v_hat) + ADAM_EPS) + ADAM_WD * w)
    return delta, m, v


def _conv_fwd(u, w, *, tm, name):
    length, chans = u.shape
    tm = min(tm, length)
    tc = _tile(chans, 1024)
    hb = tm // SUBLANES

    def body(cur_ref, prev_ref, w_ref, o_ref, buf):
        i = pl.program_id(1)
        buf[0:SUBLANES, :] = jnp.where(i > 0, prev_ref[...], 0.0)
        buf[SUBLANES:, :] = cur_ref[...]
        acc = buf[pl.ds(SUBLANES - 3, tm), :] * w_ref[0:1, :]
        for k in range(1, GDN_CONV):
            acc = acc + buf[pl.ds(SUBLANES - 3 + k, tm), :] * w_ref[k:k + 1, :]
        o_ref[...] = acc

    return pl.pallas_call(
        body, name=name, out_shape=jax.ShapeDtypeStruct(u.shape, F32),
        grid=(chans // tc, length // tm),
        in_specs=[pl.BlockSpec((tm, tc), lambda j, i: (i, j)),
                  pl.BlockSpec((SUBLANES, tc), lambda j, i: (jnp.maximum(i * hb - 1, 0), j)),
                  pl.BlockSpec((GDN_CONV, tc), lambda j, i: (0, j))],
        out_specs=pl.BlockSpec((tm, tc), lambda j, i: (i, j)),
        scratch_shapes=[pltpu.VMEM((tm + SUBLANES, tc), F32)],
        compiler_params=_params(("parallel", "parallel")),
    )(u, u, w)


def _conv_bwd(u, w, dc, *, tm, name):
    length, chans = u.shape
    tm = min(tm, length)
    tc = _tile(chans, 1024)
    hb = tm // SUBLANES
    last = length // tm - 1

    def body(u_ref, uprev_ref, dc_ref, dcnext_ref, w_ref, du_ref, dw_ref, ubuf, dbuf):
        i = pl.program_id(1)
        ubuf[0:SUBLANES, :] = jnp.where(i > 0, uprev_ref[...], 0.0)
        ubuf[SUBLANES:, :] = u_ref[...]
        dbuf[0:tm, :] = dc_ref[...]
        dbuf[tm:, :] = jnp.where(i < last, dcnext_ref[...], 0.0)
        dcv = dc_ref[...]
        du = dbuf[pl.ds(3, tm), :] * w_ref[0:1, :]
        rows = [jnp.sum(dcv * ubuf[pl.ds(SUBLANES - 3, tm), :], axis=0, keepdims=True)]
        for k in range(1, GDN_CONV):
            du = du + dbuf[pl.ds(3 - k, tm), :] * w_ref[k:k + 1, :]
            rows.append(jnp.sum(dcv * ubuf[pl.ds(SUBLANES - 3 + k, tm), :], axis=0, keepdims=True))
        du_ref[...] = du
        dwv = jnp.concatenate(rows, axis=0)

        @pl.when(i == 0)
        def _():
            dw_ref[...] = dwv

        @pl.when(i > 0)
        def _():
            dw_ref[...] += dwv

    return pl.pallas_call(
        body, name=name,
        out_shape=(jax.ShapeDtypeStruct(u.shape, F32), jax.ShapeDtypeStruct((GDN_CONV, chans), F32)),
        grid=(chans // tc, length // tm),
        in_specs=[pl.BlockSpec((tm, tc), lambda j, i: (i, j)),
                  pl.BlockSpec((SUBLANES, tc), lambda j, i: (jnp.maximum(i * hb - 1, 0), j)),
                  pl.BlockSpec((tm, tc), lambda j, i: (i, j)),
                  pl.BlockSpec((SUBLANES, tc), lambda j, i: (jnp.minimum((i + 1) * hb, (last + 1) * hb - 1), j)),
                  pl.BlockSpec((GDN_CONV, tc), lambda j, i: (0, j))],
        out_specs=(pl.BlockSpec((tm, tc), lambda j, i: (i, j)),
                   pl.BlockSpec((GDN_CONV, tc), lambda j, i: (0, j))),
        scratch_shapes=[pltpu.VMEM((tm + SUBLANES, tc), F32), pltpu.VMEM((tm + SUBLANES, tc), F32)],
        compiler_params=_params(("parallel", "arbitrary")),
    )(u, u, dc, dc, w)


def _dot(a, b, dims, precision=None):
    if precision is None:
        a, b = a.astype(BF16), b.astype(BF16)
    return lax.dot_general(a, b, (dims, ((), ())), preferred_element_type=F32, precision=precision)


def _dot3(a, b, dims):
    ah, bh = a.astype(BF16), b.astype(BF16)
    al, bl = (a - ah.astype(F32)).astype(BF16), (b - bh.astype(F32)).astype(BF16)

    def d(x, y):
        return lax.dot_general(x, y, (dims, ((), ())), preferred_element_type=F32)

    return d(ah, bh) + (d(ah, bl) + d(al, bh))


NN = ((1,), (0,))
NT = ((1,), (1,))
TN = ((0,), (0,))
HI = lax.Precision.HIGHEST


def _hmap(f, *lists):
    return [f(*t) for t in zip(*lists)]


@jax.custom_vjp
def _unit_lower_inverse(a):
    c = a[0].shape[0]
    eye = (lax.broadcasted_iota(jnp.int32, (c, c), 0) == lax.broadcasted_iota(jnp.int32, (c, c), 1)).astype(F32)
    p = _hmap(lambda x: -x, a)
    t = _hmap(lambda x: eye + x, p)
    for _ in range(int(math.log2(c)) - 1):
        p = _hmap(lambda x: _dot3(x, x, NN), p)
        t = _hmap(lambda x, y: x + _dot3(x, y, NN), t, p)
    return t


def _uli_fwd(a):
    t = _unit_lower_inverse(a)
    return t, t


def _uli_bwd(t, dt):
    left = _hmap(lambda x, y: _dot3(x, y, TN), t, dt)
    return (_hmap(lambda x, y: -_dot3(x, y, NT), left, t),)


_unit_lower_inverse.defvjp(_uli_fwd, _uli_bwd)


def _gdn_chunk(q, k, v, bl, al, a_log, dt_bias, state):
    c = q[0].shape[0]
    row = lax.broadcasted_iota(jnp.int32, (c, c), 0)
    col = lax.broadcasted_iota(jnp.int32, (c, c), 1)
    causal = row >= col
    strict = row > col
    eye = (row == col).astype(F32)
    beta = _hmap(jax.nn.sigmoid, bl)
    g = _hmap(lambda a_, l_, d_: -jnp.exp(a_) * jax.nn.softplus(l_ + d_), a_log, al, dt_bias)
    g_r = _hmap(lambda x: jnp.sum(eye * x, axis=0, keepdims=True), g)
    gc = _hmap(lambda x: jnp.sum(jnp.where(causal, x, 0.0), axis=1, keepdims=True), g_r)
    gc_r = _hmap(lambda x: jnp.sum(jnp.where(row <= col, x, 0.0), axis=0, keepdims=True), g)
    decay = _hmap(lambda x, y: jnp.where(causal, jnp.exp(jnp.where(causal, x - y, 0.0)), 0.0), gc, gc_r)
    e_gc = _hmap(jnp.exp, gc)
    kb = _hmap(jnp.multiply, k, beta)
    vb = _hmap(jnp.multiply, v, beta)
    a_mat = _hmap(lambda x, y, d: jnp.where(strict, _dot(x, y, NT) * d, 0.0), kb, k, decay)
    t_inv = _unit_lower_inverse(a_mat)
    u_blk = _hmap(lambda t, x: _dot(t, x, NN), t_inv, vb)
    w_blk = _hmap(lambda t, x, e: _dot(t, x * e, NN), t_inv, kb, e_gc)
    v_new = _hmap(lambda u, w, s: u - _dot(w, s, NN), u_blk, w_blk, state)
    attn = _hmap(lambda x, y, d: _dot(x, y, NT) * d, q, k, decay)
    o_state = _hmap(lambda x, e, s: _dot(x * e, s, NN), q, e_gc, state)
    o = _hmap(lambda base, at, vn: base + _dot(at, vn, NN), o_state, attn, v_new)
    g_last = _hmap(lambda x: jnp.sum(x, axis=0, keepdims=True), g)
    k_dec = _hmap(lambda x, gl, c_: x * jnp.exp(gl - c_), k, g_last, gc)
    new_state = _hmap(lambda s, gl, kd, vn: s * jnp.exp(gl) + _dot(kd, vn, TN), state, g_last, k_dec, v_new)
    return o, new_state


def _gdn_operands(q_ref, k_ref, v_ref, bav, alog_ref, dtb_ref):
    hs = range(GDN_HEADS)
    cols = [slice(h * HEAD_DIM, (h + 1) * HEAD_DIM) for h in hs]
    return ([q_ref[:, sl] for sl in cols], [k_ref[:, sl] for sl in cols], [v_ref[:, sl] for sl in cols],
            [bav[:, h:h + 1] for h in hs], [bav[:, h + GDN_HEADS:h + GDN_HEADS + 1] for h in hs],
            [alog_ref[h:h + 1, 0:1] for h in hs], [dtb_ref[h:h + 1, 0:1] for h in hs])


def _gdn_scan_fwd(q, k, v, ba, a_log, dt_bias, *, name):
    length = q.shape[0]
    n = length // GDN_CHUNK
    c = GDN_CHUNK

    def body(q_ref, k_ref, v_ref, ba_ref, alog_ref, dtb_ref, o_ref, s_ref, state):
        i = pl.program_id(0)

        @pl.when(i == 0)
        def _():
            state[...] = jnp.zeros_like(state)

        bav = ba_ref[...]
        heads = [slice(h * HEAD_DIM, (h + 1) * HEAD_DIM) for h in range(GDN_HEADS)]
        s_in = [state[h] for h in range(GDN_HEADS)]
        o, s_out = _gdn_chunk(*_gdn_operands(q_ref, k_ref, v_ref, bav, alog_ref, dtb_ref), s_in)
        for h, sl in enumerate(heads):
            s_ref[h] = s_in[h]
            o_ref[:, sl] = o[h]
            state[h] = s_out[h]

    row_spec = pl.BlockSpec((c, D_MODEL), lambda i: (i, 0))
    small = pl.BlockSpec((GDN_HEADS, LANES), lambda i: (0, 0))
    return pl.pallas_call(
        body, name=name,
        out_shape=(jax.ShapeDtypeStruct((length, D_MODEL), F32),
                   jax.ShapeDtypeStruct((n, GDN_HEADS, HEAD_DIM, HEAD_DIM), F32)),
        grid=(n,),
        in_specs=[row_spec, row_spec, row_spec, pl.BlockSpec((c, LANES), lambda i: (i, 0)), small, small],
        out_specs=(row_spec, pl.BlockSpec((None, GDN_HEADS, HEAD_DIM, HEAD_DIM), lambda i: (i, 0, 0, 0))),
        scratch_shapes=[pltpu.VMEM((GDN_HEADS, HEAD_DIM, HEAD_DIM), F32)],
        compiler_params=_params(("arbitrary",)),
    )(q, k, v, ba, a_log, dt_bias)


def _gdn_scan_bwd(q, k, v, ba, a_log, dt_bias, states, do, *, name):
    length = q.shape[0]
    n = length // GDN_CHUNK
    c = GDN_CHUNK

    def body(q_ref, k_ref, v_ref, ba_ref, alog_ref, dtb_ref, s_ref, do_ref,
             dq_ref, dk_ref, dv_ref, dba_ref, dalog_ref, ddtb_ref, dstate):
        i = pl.program_id(0)

        @pl.when(i == 0)
        def _():
            dstate[...] = jnp.zeros_like(dstate)
            dalog_ref[...] = jnp.zeros_like(dalog_ref)
            ddtb_ref[...] = jnp.zeros_like(ddtb_ref)

        bav = ba_ref[...]
        lane = lax.broadcasted_iota(jnp.int32, (c, LANES), 1)
        sub8 = lax.broadcasted_iota(jnp.int32, (GDN_HEADS, LANES), 0)
        lane8 = lax.broadcasted_iota(jnp.int32, (GDN_HEADS, LANES), 1)
        slab = jnp.zeros((c, LANES), F32)
        dalog_all = jnp.zeros((GDN_HEADS, LANES), F32)
        ddtb_all = jnp.zeros((GDN_HEADS, LANES), F32)
        heads = [slice(h * HEAD_DIM, (h + 1) * HEAD_DIM) for h in range(GDN_HEADS)]
        ds_in = [dstate[h] for h in range(GDN_HEADS)]
        s_in = [s_ref[h] for h in range(GDN_HEADS)]
        _, vjp = jax.vjp(_gdn_chunk, *_gdn_operands(q_ref, k_ref, v_ref, bav, alog_ref, dtb_ref), s_in)
        dq, dk, dv, dbl, dal, dalog, ddtb, ds = vjp(([do_ref[:, sl] for sl in heads], ds_in))
        for h, sl in enumerate(heads):
            dq_ref[:, sl] = dq[h]
            dk_ref[:, sl] = dk[h]
            dv_ref[:, sl] = dv[h]
            dstate[h] = ds[h]
            slab = slab + jnp.where(lane == h, dbl[h], 0.0) + jnp.where(lane == h + GDN_HEADS, dal[h], 0.0)
            here = (sub8 == h) & (lane8 == 0)
            dalog_all = dalog_all + jnp.where(here, dalog[h], 0.0)
            ddtb_all = ddtb_all + jnp.where(here, ddtb[h], 0.0)
        dba_ref[...] = slab
        dalog_ref[...] += dalog_all
        ddtb_ref[...] += ddtb_all

    row_spec = pl.BlockSpec((c, D_MODEL), lambda i: (n - 1 - i, 0))
    small = pl.BlockSpec((GDN_HEADS, LANES), lambda i: (0, 0))
    return pl.pallas_call(
        body, name=name,
        out_shape=(jax.ShapeDtypeStruct((length, D_MODEL), F32),) * 3
        + (jax.ShapeDtypeStruct((length, LANES), F32),
           jax.ShapeDtypeStruct((GDN_HEADS, LANES), F32), jax.ShapeDtypeStruct((GDN_HEADS, LANES), F32)),
        grid=(n,),
        in_specs=[row_spec, row_spec, row_spec,
                  pl.BlockSpec((c, LANES), lambda i: (n - 1 - i, 0)), small, small,
                  pl.BlockSpec((None, GDN_HEADS, HEAD_DIM, HEAD_DIM), lambda i: (n - 1 - i, 0, 0, 0)),
                  row_spec],
        out_specs=(row_spec, row_spec, row_spec,
                   pl.BlockSpec((c, LANES), lambda i: (n - 1 - i, 0)), small, small),
        scratch_shapes=[pltpu.VMEM((GDN_HEADS, HEAD_DIM, HEAD_DIM), F32)],
        compiler_params=_params(("arbitrary",)),
    )(q, k, v, ba, a_log, dt_bias, states, do)


S5_W = S5_T * LANES
S5_S = 2 * 8 * S5_STATE
S5_SH = S5_S // 2


def _iota2(shape):
    return lax.broadcasted_iota(jnp.int32, shape, 0), lax.broadcasted_iota(jnp.int32, shape, 1)


def _s5_rep_t(t, dtype):
    row, col = _iota2((S5_T * S5_GROUP, LANES))
    return ((jnp.right_shift(row, 4) == t) & (jnp.bitwise_and(row, 15) == jnp.bitwise_and(col, 15))).astype(dtype)


def _s5_rep_state(dtype):
    row, col = _iota2((2 * S5_STATE, S5_S))
    return ((jnp.right_shift(row, 6) == jnp.right_shift(col, 9))
            & (jnp.bitwise_and(row, 63) == jnp.bitwise_and(col, 63))).astype(dtype)


def _s5_masks():
    row, col = _iota2((LANES, LANES))
    m_ab = jnp.right_shift(row, 4) == jnp.right_shift(col, 4)
    row, col = _iota2((S5_S, LANES))
    m_e = jnp.bitwise_and(jnp.right_shift(row, 6), 7) == jnp.right_shift(col, 4)
    row, col = _iota2((LANES, S5_S))
    m_f = jnp.right_shift(row, 4) == jnp.bitwise_and(jnp.right_shift(col, 6), 7)
    return m_ab, m_e, m_f


def _s5_expand(kx_ref, ec_ref, fc_ref, kb_scr, e_scr, f_scr):
    m_ab, m_e, m_f = _s5_masks()
    kx = kx_ref[...].astype(BF16)
    ec = ec_ref[...].astype(BF16)
    rep_state = _s5_rep_state(BF16)
    for t in range(S5_T):
        rep = _s5_rep_t(t, BF16)
        cols = slice(t * LANES, (t + 1) * LANES)
        kb_scr[t] = jnp.where(m_ab, jnp.dot(kx, rep, preferred_element_type=F32), 0.0).astype(BF16)
        e_scr[:, cols] = jnp.where(m_e, jnp.dot(ec, rep, preferred_element_type=F32), 0.0).astype(BF16)
        f_scr[cols, :] = jnp.where(m_f, jnp.dot(fc_ref[t].astype(BF16), rep_state, preferred_element_type=F32),
                                   0.0).astype(BF16)


def _s5_token_rows(ref, n):
    return [ref[pl.ds(t, n, stride=S5_T), :].astype(BF16) for t in range(S5_T)]


def _s5_scan_fwd(u, kx, ec, fc, at, *, name):
    length = u.shape[0]
    n = length // S5_T
    assert n % SUBLANES == 0

    def body(u_ref, kx_ref, ec_ref, fc_ref, at_ref, y_ref, h_ref, kb_scr, e_scr, f_scr, g_scr):
        _s5_expand(kx_ref, ec_ref, fc_ref, kb_scr, e_scr, f_scr)
        us = _s5_token_rows(u_ref, n)
        g_scr[...] = jnp.dot(jnp.concatenate(us, axis=1), f_scr[...], preferred_element_type=F32)
        ar, ai = at_ref[:, :S5_SH], at_ref[:, S5_SH:]

        def step(blk, h):
            base = pl.multiple_of(blk * SUBLANES, SUBLANES)
            g8 = g_scr[pl.ds(base, SUBLANES), :]
            rows = []
            for r in range(SUBLANES):
                rows.append(h)
                hr, hi = h[:, :S5_SH], h[:, S5_SH:]
                h = jnp.concatenate([ar * hr - ai * hi, ar * hi + ai * hr], axis=1) + g8[r:r + 1, :]
            h_ref[pl.ds(base, SUBLANES), :] = jnp.concatenate(rows, axis=0)
            return h

        lax.fori_loop(0, n // SUBLANES, step, jnp.zeros((1, S5_S), F32))
        hb = h_ref[...].astype(BF16)
        for t in range(S5_T):
            acc = jnp.dot(hb, e_scr[:, t * LANES:(t + 1) * LANES], preferred_element_type=F32)
            for s in range(t + 1):
                acc = acc + jnp.dot(us[s], kb_scr[t - s], preferred_element_type=F32)
            y_ref[pl.ds(t, n, stride=S5_T), :] = acc

    return pl.pallas_call(
        body, name=name,
        out_shape=(jax.ShapeDtypeStruct((length, D_MODEL), F32), jax.ShapeDtypeStruct((S5_TILES, n, S5_S), F32)),
        grid=(S5_TILES,),
        in_specs=[pl.BlockSpec((length, LANES), lambda k: (0, k)), _s5_spec(LANES, S5_T * S5_GROUP),
                  _s5_spec(S5_S, S5_T * S5_GROUP), _s5_spec(S5_T, LANES, LANES), _s5_spec(1, S5_S)],
        out_specs=(pl.BlockSpec((length, LANES), lambda k: (0, k)), _s5_spec(n, S5_S)),
        scratch_shapes=[pltpu.VMEM((S5_T, LANES, LANES), BF16), pltpu.VMEM((S5_S, S5_W), BF16),
                        pltpu.VMEM((S5_W, S5_S), BF16), pltpu.VMEM((n, S5_S), F32)],
        compiler_params=_params(("parallel",)),
    )(u, kx, ec, fc, at)


def _s5_spec(*tail):
    return pl.BlockSpec((None,) + tail, lambda k: (k,) + (0,) * len(tail))


def _s5_scan_bwd(dy, kx, ec, fc, at, hs, *, name):
    length = dy.shape[0]
    n = length // S5_T

    def body(dy_ref, kx_ref, ec_ref, fc_ref, at_ref, h_ref, du_ref, dg_ref, dat_ref, kb_scr, e_scr, f_scr, dh_scr):
        _s5_expand(kx_ref, ec_ref, fc_ref, kb_scr, e_scr, f_scr)
        dys = _s5_token_rows(dy_ref, n)
        dh_scr[...] = _dot(jnp.concatenate(dys, axis=1), e_scr[...], NT)
        ar, ai = at_ref[:, :S5_SH], at_ref[:, S5_SH:]

        def step(it, carry):
            cy, dat = carry
            base = pl.multiple_of((n // SUBLANES - 1 - it) * SUBLANES, SUBLANES)
            dh8 = dh_scr[pl.ds(base, SUBLANES), :]
            h8 = h_ref[pl.ds(base, SUBLANES), :]
            rows = [None] * SUBLANES
            for r in reversed(range(SUBLANES)):
                rows[r] = cy
                cr, ci = cy[:, :S5_SH], cy[:, S5_SH:]
                hr, hi = h8[r:r + 1, :S5_SH], h8[r:r + 1, S5_SH:]
                dat = dat + jnp.concatenate([cr * hr + ci * hi, ci * hr - cr * hi], axis=1)
                cy = dh8[r:r + 1, :] + jnp.concatenate([ar * cr + ai * ci, ar * ci - ai * cr], axis=1)
            dg_ref[pl.ds(base, SUBLANES), :] = jnp.concatenate(rows, axis=0)
            return cy, dat

        zero = jnp.zeros((1, S5_S), F32)
        _, dat = lax.fori_loop(0, n // SUBLANES, step, (zero, zero))
        dat_ref[...] = dat
        dgb = dg_ref[...].astype(BF16)
        for s in range(S5_T):
            acc = _dot(dgb, f_scr[s * LANES:(s + 1) * LANES, :], NT)
            for t in range(s, S5_T):
                acc = acc + _dot(dys[t], kb_scr[t - s], NT)
            du_ref[pl.ds(s, n, stride=S5_T), :] = acc

    row_spec = pl.BlockSpec((length, LANES), lambda k: (0, k))
    return pl.pallas_call(
        body, name=name,
        out_shape=(jax.ShapeDtypeStruct((length, D_MODEL), F32), jax.ShapeDtypeStruct((S5_TILES, n, S5_S), F32),
                   jax.ShapeDtypeStruct((S5_TILES, 1, S5_S), F32)),
        grid=(S5_TILES,),
        in_specs=[row_spec, _s5_spec(LANES, S5_T * S5_GROUP), _s5_spec(S5_S, S5_T * S5_GROUP),
                  _s5_spec(S5_T, LANES, LANES), _s5_spec(1, S5_S), _s5_spec(n, S5_S)],
        out_specs=(row_spec, _s5_spec(n, S5_S), _s5_spec(1, S5_S)),
        scratch_shapes=[pltpu.VMEM((S5_T, LANES, LANES), BF16), pltpu.VMEM((S5_S, S5_W), BF16),
                        pltpu.VMEM((S5_W, S5_S), BF16), pltpu.VMEM((n, S5_S), F32)],
        compiler_params=_params(("parallel",)),
    )(dy, kx, ec, fc, at, hs)


def _s5_operator_grads(dy, u, hs, dg, *, name):
    length = u.shape[0]
    n = length // S5_T

    def body(dy_ref, u_ref, h_ref, dg_ref, dkx_ref, dec_ref, dfc_ref):
        dys = _s5_token_rows(dy_ref, n)
        us = _s5_token_rows(u_ref, n)
        ucat = jnp.concatenate(us, axis=1)
        m_ab, m_e, m_f = _s5_masks()
        hb = h_ref[...].astype(BF16)
        dgb = dg_ref[...].astype(BF16)
        lane = lax.broadcasted_iota(jnp.int32, (1, LANES), 1)
        lane_group = jnp.right_shift(lane, 4)

        def own_block(x, mask):
            x = jnp.where(mask, x, 0.0)
            for shift in (64, 32, 16):
                x = x + pltpu.roll(x, shift, 1)
            return x

        def place(halves, t, x):
            halves[t // 8] = jnp.where(lane_group == t % 8, x, halves[t // 8])

        dkb = [jnp.zeros((LANES, LANES), F32) for _ in range(S5_T)]
        dec = [jnp.zeros((S5_S, LANES), F32) for _ in range(2)]
        for t in range(S5_T):
            d_t = _dot(ucat, dys[t], TN)
            for s in range(t + 1):
                dkb[t - s] = dkb[t - s] + d_t[s * LANES:(s + 1) * LANES, :]
            place(dec, t, own_block(_dot(hb, dys[t], TN), m_e))
            wide = jnp.where(m_f, _dot(us[t], dgb, TN), 0.0)
            parts = []
            for r in range(2):
                acc = wide[:, r * S5_SH:r * S5_SH + LANES]
                for q in range(1, S5_SH // LANES):
                    acc = acc + wide[:, r * S5_SH + q * LANES:r * S5_SH + (q + 1) * LANES]
                parts.append(acc + pltpu.roll(acc, S5_STATE, 1))
            dfc_ref[t] = jnp.where(lane < S5_STATE, parts[0], parts[1])
        dkx = [jnp.zeros((LANES, LANES), F32) for _ in range(2)]
        for t in range(S5_T):
            place(dkx, t, own_block(dkb[t], m_ab))
        dkx_ref[...] = jnp.concatenate(dkx, axis=1)
        dec_ref[...] = jnp.concatenate(dec, axis=1)

    row_spec = pl.BlockSpec((length, LANES), lambda k: (0, k))
    outs = (_s5_spec(LANES, S5_T * S5_GROUP), _s5_spec(S5_S, S5_T * S5_GROUP), _s5_spec(S5_T, LANES, LANES))
    return pl.pallas_call(
        body, name=name,
        out_shape=(jax.ShapeDtypeStruct((S5_TILES, LANES, S5_T * S5_GROUP), F32),
                   jax.ShapeDtypeStruct((S5_TILES, S5_S, S5_T * S5_GROUP), F32),
                   jax.ShapeDtypeStruct((S5_TILES, S5_T, LANES, LANES), F32)),
        grid=(S5_TILES,),
        in_specs=[row_spec, row_spec, _s5_spec(n, S5_S), _s5_spec(n, S5_S)],
        out_specs=outs,
        compiler_params=_params(("parallel",)),
    )(dy, u, hs, dg)


def _s5_prep(a_re, a_im, b_re, b_im, c_re, c_im, log_dt):
    t_len, tiles = S5_T, S5_TILES
    dt = jnp.exp(log_dt)[:, None]
    mag = jnp.exp(a_re * dt)
    ab_re, ab_im = mag * jnp.cos(a_im * dt), mag * jnp.sin(a_im * dt)
    den = jnp.square(a_re) + jnp.square(a_im)
    n_re, n_im = ab_re - 1.0, ab_im
    f_re = (n_re * a_re + n_im * a_im) / den
    f_im = (n_im * a_re - n_re * a_im) / den
    bb_re = f_re[..., None] * b_re - f_im[..., None] * b_im
    bb_im = f_re[..., None] * b_im + f_im[..., None] * b_re
    def powers(exponents):
        e = exponents[:, None, None]
        m = jnp.exp(e * (a_re * dt))
        return m * jnp.cos(e * (a_im * dt)), m * jnp.sin(e * (a_im * dt))

    p_re, p_im = powers(jnp.arange(t_len + 1, dtype=F32))
    rev_re, rev_im = powers((t_len - 1) - jnp.arange(t_len, dtype=F32))
    ca_re = c_re[None] * p_re[:, :, None, :] - c_im[None] * p_im[:, :, None, :]
    ca_im = c_re[None] * p_im[:, :, None, :] + c_im[None] * p_re[:, :, None, :]
    lag = (jnp.einsum('tgip,gpj->tgij', ca_re[:t_len], bb_re, precision=HI)
           - jnp.einsum('tgip,gpj->tgij', ca_im[:t_len], bb_im, precision=HI))
    kx = lag.reshape(t_len, tiles, 8, S5_GROUP, S5_GROUP).transpose(1, 2, 4, 0, 3)
    kx = kx.reshape(tiles, LANES, t_len * S5_GROUP)
    e_st = jnp.stack([ca_re[1:], -ca_im[1:]])
    e_st = e_st.reshape(2, t_len, tiles, 8, S5_GROUP, S5_STATE).transpose(2, 0, 3, 5, 1, 4)
    ec = e_st.reshape(tiles, S5_S, t_len * S5_GROUP)
    ab_b = jnp.stack([rev_re[..., None] * bb_re[None] - rev_im[..., None] * bb_im[None],
                      rev_re[..., None] * bb_im[None] + rev_im[..., None] * bb_re[None]])
    ab_b = ab_b.reshape(2, t_len, tiles, 8, S5_STATE, S5_GROUP).transpose(2, 1, 3, 5, 0, 4)
    fc = ab_b.reshape(tiles, t_len, LANES, 2 * S5_STATE)
    a_t = jnp.stack([p_re[t_len], p_im[t_len]]).reshape(2, tiles, 8 * S5_STATE).transpose(1, 0, 2)
    return kx, ec, fc, a_t.reshape(tiles, 1, S5_S)


TM_ROW = 256


def _gdn_fwd(x, w, tag):
    qkv = _mm(x, w["wqkv"], name="gdn_proj_qkv")
    z = _mm(x, w["wz"], name="gdn_proj_z")
    ba = _mm(x, w["wba"], name="gdn_proj_ba")
    cv = _conv_fwd(qkv, w["conv_w"], tm=TM_ROW, name="gdn_conv")
    q, k, v = _rw_fwd(_f_gdn_qkv, [cv], [], tm=TM_ROW, name="gdn_qkv")
    o, states = _gdn_scan_fwd(q, k, v, ba, w["a_log8"], w["dt_bias8"], name="gdn_scan")
    (mix,) = _rw_fwd(_f_gdn_out, [o, z], [w["norm_g"]], tm=TM_ROW, name="gdn_out")
    return mix, (qkv, z, ba, cv, q, k, v, states, o)


def _gdn_bwd(x, w, saved, dmix, dx_acc):
    qkv, z, ba, cv, q, k, v, states, o = saved
    (do, dz), (dnorm_g,) = _rw_bwd(_f_gdn_out, [o, z], [w["norm_g"]], [dmix], row_grad=[1, 1], param_grad=[1],
                                   tm=TM_ROW, name="gdn_out_bwd")
    dq, dk, dv, dba, dalog, ddtb = _gdn_scan_bwd(q, k, v, ba, w["a_log8"], w["dt_bias8"], states, do,
                                                  name="gdn_scan_bwd")
    (dcv,), _ = _rw_bwd(_f_gdn_qkv, [cv], [], [dq, dk, dv], row_grad=[1], param_grad=[], tm=TM_ROW,
                        name="gdn_qkv_bwd")
    dqkv, dconv_w = _conv_bwd(qkv, w["conv_w"], dcv, tm=TM_ROW, name="gdn_conv_bwd")
    dx = _mm(dqkv, w["wqkv"], tb=True, acc=dx_acc, name="gdn_dx_qkv")
    dx = _mm(dz, w["wz"], tb=True, acc=dx, name="gdn_dx_z")
    dx = _mm(dba, w["wba"], tb=True, acc=dx, name="gdn_dx_ba")
    grads = dict(wqkv=_mm(x, dqkv, ta=True, name="gdn_dw_qkv"), wz=_mm(x, dz, ta=True, name="gdn_dw_z"),
                 wba=_mm(x, dba, ta=True, name="gdn_dw_ba"), conv_w=dconv_w,
                 a_log=dalog[:, 0], dt_bias=ddtb[:, 0], norm_g=dnorm_g[0])
    return dx, grads


def _s5_fwd(x, w, tag):
    u = _mm(x, w["wu"], name="s5_proj_u")
    y, hs = _s5_scan_fwd(u, w["kx"], w["ec"], w["fc"], w["a_t"], name="s5_scan")
    (zg,) = _rw_fwd(_f_s5_gelu, [y, u], [w["d"]], tm=TM_ROW, name="s5_gelu")
    t = _mm(zg, w["w_glu"], name="s5_glu")
    (mix,) = _rw_fwd(_f_s5_gate, [zg, t], [w["b_glu"]], tm=TM_ROW, name="s5_gate")
    return mix, (u, hs, y, zg, t)


def _s5_bwd(x, w, saved, dmix, dx_acc):
    u, hs, y, zg, t = saved
    (dzg, dt), (db_glu,) = _rw_bwd(_f_s5_gate, [zg, t], [w["b_glu"]], [dmix], row_grad=[1, 1], param_grad=[1],
                                   tm=TM_ROW, name="s5_gate_bwd")
    dzg = _mm(dt, w["w_glu"], tb=True, acc=dzg, name="s5_dzg")
    dw_glu = _mm(zg, dt, ta=True, name="s5_dw_glu")
    (dy, du), (dd,) = _rw_bwd(_f_s5_gelu, [y, u], [w["d"]], [dzg], row_grad=[1, 1], param_grad=[1],
                              tm=TM_ROW, name="s5_gelu_bwd")
    du_scan, dg, dat = _s5_scan_bwd(dy, w["kx"], w["ec"], w["fc"], w["a_t"], hs, name="s5_scan_bwd")
    dkx, dec, dfc = _s5_operator_grads(dy, u, hs, dg, name="s5_operator_grads")
    (du,) = _rw_fwd(_f_add, [du, du_scan], [], tm=TM_ROW, name="s5_du_add")
    d_a_re, d_a_im, d_b_re, d_b_im, d_c_re, d_c_im, d_log_dt = w["prep_vjp"]((dkx, dec, dfc, dat))
    dx = _mm(du, w["wu"], tb=True, acc=dx_acc, name="s5_dx_u")
    grads = dict(wu=_mm(x, du, ta=True, name="s5_dw_u"), w_glu=dw_glu, b_glu=db_glu[0], d=dd[0],
                 a_re=d_a_re, a_im=d_a_im, b_re=d_b_re, b_im=d_b_im, c_re=d_c_re, c_im=d_c_im, log_dt=d_log_dt)
    return dx, grads


def _layer_fwd(x, mem, w, is_gdn):
    mix, msave = (_gdn_fwd if is_gdn else _s5_fwd)(x, w, "")
    xq = _mm(x, w["wxq"], name="proj_xq")
    kv = _mm(mem, w["wkv"], name="mem_kv")
    kmem, vmem = kv[:, :XA_DIM], kv[:, XA_DIM:]
    (cross,) = _rw_fwd(_f_attn, [xq], [kmem, vmem], tm=TM_ROW, name="attn")
    h = _mm(mix, w["wo_mix"], name="wo_mix")
    h = _mm(cross, w["wo_cross"], acc=h, name="wo_cross")
    (x1,) = _rw_fwd(_f_ln_res, [x, h], [w["ln1_g"], w["ln1_b"]], tm=TM_ROW, name="ln_res")
    hm, act = _mm_relu2(x1, w["w1"], name="mlp_up")
    f = _mm(act, w["w2"], name="mlp_down")
    (x2,) = _rw_fwd(_f_ln_res, [x1, f], [w["ln2_g"], w["ln2_b"]], tm=TM_ROW, name="ln_res")
    return x2, (x, msave, xq, kmem, vmem, mix, cross, h, x1, hm, act, f)


def _layer_bwd(mem, w, is_gdn, saved, dx2):
    x, msave, xq, kmem, vmem, mix, cross, h, x1, hm, act, f = saved
    (dx1, df), (dg2, db2) = _rw_bwd(_f_ln_res, [x1, f], [w["ln2_g"], w["ln2_b"]], [dx2], row_grad=[1, 1],
                                    param_grad=[1, 1], tm=TM_ROW, name="ln_res_bwd")
    dhm = _mm_relu2_grad(df, w["w2"], hm, name="mlp_dhm")
    dw2 = _mm(act, df, ta=True, name="mlp_dw2")
    dx1 = _mm(dhm, w["w1"], tb=True, acc=dx1, name="mlp_dx")
    dw1 = _mm(x1, dhm, ta=True, out_blocks=N_CHIPS, name="mlp_dw1")
    (dx, dh), (dg1, db1) = _rw_bwd(_f_ln_res, [x, h], [w["ln1_g"], w["ln1_b"]], [dx1], row_grad=[1, 1],
                                   param_grad=[1, 1], tm=TM_ROW, name="ln_res_bwd")
    dmix = _mm(dh, w["wo_mix"], tb=True, name="wo_dmix")
    dcross = _mm(dh, w["wo_cross"], tb=True, name="wo_dcross")
    dwo = jnp.concatenate([_mm(mix, dh, ta=True, name="wo_dw_mix"), _mm(cross, dh, ta=True, name="wo_dw_cross")], 0)
    (dxq,), (dkmem, dvmem) = _rw_bwd(_f_attn, [xq], [kmem, vmem], [dcross], row_grad=[1], param_grad=[1, 1],
                                     tm=TM_ROW, name="attn_bwd")
    dwkv = _mm(mem, jnp.concatenate([dkmem, dvmem], axis=1), ta=True, name="mem_dw_kv")
    dx = _mm(dxq, w["wxq"], tb=True, acc=dx, name="dx_xq")
    dwxq = _mm(x, dxq, ta=True, name="dw_xq")
    dx, mg = (_gdn_bwd if is_gdn else _s5_bwd)(x, w, msave, dmix, dx)
    grads = dict(mixer=mg, wxq=dwxq, wkv=dwkv, wo=dwo, w1=dw1, w2=dw2,
                 ln1_g=dg1[0], ln1_b=db1[0], ln2_g=dg2[0], ln2_b=db2[0])
    return dx, grads


def _loss_and_grad(y, target):
    def f(yv, tv):
        err = yv - tv
        return (err * (1.0 / D_MODEL),), (0.5 / D_MODEL * jnp.sum(err * err, axis=0, keepdims=True),)

    (dy,), (part,) = _rowwise(f, [y, target], [], [(D_MODEL, F32)], [((1, D_MODEL), F32)], tm=512, name="loss")
    return jnp.sum(part), dy


def _layer_weights(full, i):
    j = i // 2
    w = dict(wkv=full["w_kv_mem"][i].astype(BF16),
             wo_mix=full["w_o"][i][:D_MODEL].astype(BF16), wo_cross=full["w_o"][i][D_MODEL:].astype(BF16),
             ln1_g=full["ln1_g"][i][None], ln1_b=full["ln1_b"][i][None],
             ln2_g=full["ln2_g"][i][None], ln2_b=full["ln2_b"][i][None],
             w1=full["mlp_w1"][i].astype(BF16), w2=full["mlp_w2"][i].astype(BF16))
    if i % 2 == 0:
        w_in = full["gdn_w_in"][j]
        gd = 3 * D_MODEL
        w.update(wqkv=w_in[:, :gd].astype(BF16), wz=w_in[:, gd:gd + D_MODEL].astype(BF16),
                 wba=jnp.pad(w_in[:, gd + D_MODEL:gd + D_MODEL + 2 * GDN_HEADS],
                             ((0, 0), (0, LANES - 2 * GDN_HEADS))).astype(BF16),
                 wxq=w_in[:, gd + D_MODEL + 2 * GDN_HEADS:].astype(BF16),
                 conv_w=full["gdn_conv_w"][j],
                 a_log8=jnp.broadcast_to(full["gdn_a_log"][j][:, None], (GDN_HEADS, LANES)),
                 dt_bias8=jnp.broadcast_to(full["gdn_dt_bias"][j][:, None], (GDN_HEADS, LANES)),
                 norm_g=full["gdn_norm_g"][j][None])
    else:
        w_in = full["s5_w_in"][j]
        (kx, ec, fc, a_t), prep_vjp = jax.vjp(
            _s5_prep, full["s5_a_re"][j], full["s5_a_im"][j], full["s5_b_re"][j], full["s5_b_im"][j],
            full["s5_c_re"][j], full["s5_c_im"][j], full["s5_log_dt"][j])
        w.update(wu=w_in[:, :D_MODEL].astype(BF16), wxq=w_in[:, D_MODEL:].astype(BF16),
                 kx=kx, ec=ec, fc=fc, a_t=a_t, prep_vjp=prep_vjp,
                 d=full["s5_d"][j][None], w_glu=full["s5_w_glu"][j].astype(BF16), b_glu=full["s5_b_glu"][j][None])
    return w


def _grads_by_weight(layer_grads):
    g = layer_grads
    gdn = [g[i] for i in range(DEPTH) if i % 2 == 0]
    s5 = [g[i] for i in range(DEPTH) if i % 2 == 1]
    out = dict(
        w_kv_mem=[l["wkv"] for l in g], w_o=[l["wo"] for l in g],
        ln1_g=jnp.stack([l["ln1_g"] for l in g]), ln1_b=jnp.stack([l["ln1_b"] for l in g]),
        ln2_g=jnp.stack([l["ln2_g"] for l in g]), ln2_b=jnp.stack([l["ln2_b"] for l in g]),
        mlp_w1=[l["w1"] for l in g], mlp_w2=[l["w2"] for l in g],
        gdn_w_in=[jnp.concatenate([l["mixer"]["wqkv"], l["mixer"]["wz"],
                                   l["mixer"]["wba"][:, :2 * GDN_HEADS], l["wxq"]], axis=1) for l in gdn],
        gdn_conv_w=[l["mixer"]["conv_w"] for l in gdn],
        gdn_a_log=jnp.stack([l["mixer"]["a_log"] for l in gdn]),
        gdn_dt_bias=jnp.stack([l["mixer"]["dt_bias"] for l in gdn]),
        gdn_norm_g=jnp.stack([l["mixer"]["norm_g"] for l in gdn]),
        s5_w_in=[jnp.concatenate([l["mixer"]["wu"], l["wxq"]], axis=1) for l in s5],
        s5_d=[l["mixer"]["d"] for l in s5],
        s5_w_glu=[l["mixer"]["w_glu"] for l in s5],
        s5_b_glu=[l["mixer"]["b_glu"] for l in s5])
    for n in ("a_re", "a_im", "b_re", "b_im", "c_re", "c_im", "log_dt"):
        out["s5_" + n] = jnp.stack([l["mixer"][n] for l in s5])
    return out


def _local_step(x, mem, target, weights_of):
    lw, saves = [], []
    h = x
    for i in range(DEPTH):
        lw.append(weights_of(i, h))
        h, s = _layer_fwd(h, mem, lw[i], i % 2 == 0)
        saves.append(s)
    loss, d = _loss_and_grad(h, target)
    grads = [None] * DEPTH
    for i in reversed(range(DEPTH)):
        d, grads[i] = _layer_bwd(mem, lw[i], i % 2 == 0, saves[i], d)
    return loss, d, _grads_by_weight(grads)


ANY = pl.BlockSpec(memory_space=pl.ANY)
SHARD_ROWS = 1024
SMALL_ROWS = 128


def _place():
    return lax.axis_index("x"), lax.axis_index("y"), lax.axis_index("c")


def _other_chips(x, y):
    return [(1 - x, y), (x, 1 - y), (1 - x, 1 - y)]


def _all_gather_chips(wpack, *, name):
    rows = wpack.shape[0]
    half = rows // 2

    def body(w_ref, out_ref, send_sems, recv_sems):
        x, y, c = _place()
        sibling = (x, y, 1 - c)
        chips = _other_chips(x, y)

        def blk(cx, cy, cc):
            return out_ref.at[2 * cx + cy, pl.ds(cc * half, half), :]

        def copy(k, src, dst, to):
            return pltpu.make_async_remote_copy(src_ref=src, dst_ref=dst, send_sem=send_sems.at[k],
                                                recv_sem=recv_sems.at[k], device_id=to, device_id_type=MESH)

        first = [copy(j, w_ref.at[pl.ds(c * half, half), :], blk(x, y, c), (cx, cy, c))
                 for j, (cx, cy) in enumerate(chips)]
        for cp in first:
            cp.start()
        passed = [copy(3 + j, blk(cx, cy, c), blk(cx, cy, c), sibling) for j, (cx, cy) in enumerate(chips)]
        for j, (cx, cy) in enumerate(chips):
            copy(j, blk(cx, cy, c), blk(cx, cy, c), (cx, cy, c)).wait_recv()
            passed[j].start()
        for j, (cx, cy) in enumerate(chips):
            copy(3 + j, blk(cx, cy, 1 - c), blk(cx, cy, 1 - c), sibling).wait_recv()
        for cp in first + passed:
            cp.wait_send()

    return pl.pallas_call(
        body, name=name, out_shape=jax.ShapeDtypeStruct((N_CHIPS, rows, D_MODEL), wpack.dtype),
        in_specs=[ANY], out_specs=ANY,
        scratch_shapes=[pltpu.SemaphoreType.DMA((6,)), pltpu.SemaphoreType.DMA((6,))],
    )(wpack)


HBM = pl.BlockSpec(memory_space=pltpu.HBM)
SEM = pl.BlockSpec(memory_space=pltpu.SEMAPHORE)
DATAFLOW = pltpu.SideEffectType.DATAFLOW_SIDE_EFFECTING


def _gather_ici_copies(w_ref, land_ref, send_sems, recv_sems, outgoing):
    x, y, c = _place()
    half = w_ref.shape[0] // 2
    mine = pl.ds(c * half, half)
    return [pltpu.make_async_remote_copy(
        src_ref=w_ref.at[mine, :], dst_ref=land_ref.at[2 * x + y if outgoing else 2 * cx + cy, mine, :],
        send_sem=send_sems.at[j], recv_sem=recv_sems.at[j], device_id=(cx, cy, c), device_id_type=MESH)
        for j, (cx, cy) in enumerate(_other_chips(x, y))]


def _gather_start(wpack, after):
    rows = wpack.shape[0]

    def body(w_ref, land_ref, after_ref, send_sems, recv_sems, w_thru, land_thru, token):
        for cp in _gather_ici_copies(w_ref, land_ref, send_sems, recv_sems, outgoing=True):
            cp.start()
        token[...] = jnp.zeros_like(token)

    land = pltpu.with_memory_space_constraint(lax.empty((N_CHIPS, rows, D_MODEL), wpack.dtype), pltpu.HBM)
    return pl.pallas_call(
        body, name="gather_start",
        out_shape=(pltpu.SemaphoreType.DMA((3,)), pltpu.SemaphoreType.DMA((3,)), pltpu.HBM(wpack.shape, wpack.dtype),
                   pltpu.HBM(land.shape, land.dtype), jax.ShapeDtypeStruct((SUBLANES, LANES), F32)),
        in_specs=(HBM, HBM, ANY), out_specs=(SEM, SEM, HBM, HBM, pl.BlockSpec(memory_space=pltpu.VMEM)),
        input_output_aliases={0: 2, 1: 3},
        compiler_params=pltpu.CompilerParams(has_side_effects=DATAFLOW),
    )(pltpu.with_memory_space_constraint(wpack, pltpu.HBM), land, after)


def _gather_wait(send_sems, recv_sems, w_thru, land_thru, after):
    def body(w_ref, land_ref, send_sems, recv_sems, after_ref, w_dead, land_out):
        for cp in _gather_ici_copies(w_ref, land_ref, send_sems, recv_sems, outgoing=False):
            cp.wait_send()
            cp.wait_recv()

    return pl.pallas_call(
        body, name="gather_wait",
        out_shape=(pltpu.HBM(w_thru.shape, w_thru.dtype), pltpu.HBM(land_thru.shape, land_thru.dtype)),
        in_specs=(HBM, HBM, SEM, SEM, ANY), out_specs=(HBM, HBM), input_output_aliases={0: 0, 1: 1},
        compiler_params=pltpu.CompilerParams(has_side_effects=DATAFLOW),
    )(w_thru, land_thru, send_sems, recv_sems, after)[1]


def _gather_forward(land, *, name):
    rows = land.shape[1]
    half = rows // 2

    def body(in_ref, out_ref, send_sems, recv_sems):
        x, y, c = _place()

        def copy(j, cx, cy, cc):
            rows_of = out_ref.at[2 * cx + cy, pl.ds(cc * half, half), :]
            return pltpu.make_async_remote_copy(src_ref=rows_of, dst_ref=rows_of, send_sem=send_sems.at[j],
                                                recv_sem=recv_sems.at[j], device_id=(x, y, 1 - c), device_id_type=MESH)

        sends = [copy(j, cx, cy, c) for j, (cx, cy) in enumerate(_other_chips(x, y))]
        for cp in sends:
            cp.start()
        for j, (cx, cy) in enumerate(_other_chips(x, y)):
            copy(j, cx, cy, 1 - c).wait_recv()
        for cp in sends:
            cp.wait_send()

    return pl.pallas_call(
        body, name=name, out_shape=jax.ShapeDtypeStruct(land.shape, land.dtype), in_specs=[ANY], out_specs=ANY,
        input_output_aliases={0: 0},
        scratch_shapes=[pltpu.SemaphoreType.DMA((3,)), pltpu.SemaphoreType.DMA((3,))],
    )(land)


def _sibling_swap(buf, *, name):
    def body(in_ref, out_ref, send_sem, recv_sem):
        x, y, c = _place()
        cp = pltpu.make_async_remote_copy(src_ref=in_ref, dst_ref=out_ref, send_sem=send_sem, recv_sem=recv_sem,
                                          device_id=(x, y, 1 - c), device_id_type=MESH)
        cp.start()
        cp.wait()

    return pl.pallas_call(
        body, name=name, out_shape=jax.ShapeDtypeStruct(buf.shape, buf.dtype), in_specs=[ANY], out_specs=ANY,
        scratch_shapes=[pltpu.SemaphoreType.DMA, pltpu.SemaphoreType.DMA],
    )(buf)


def _pair_exchange(gpack, *, name):
    pieces, rows, width = gpack.shape
    half = rows // 2

    def body(in_ref, got_ref, send_sems, recv_sems):
        x, y, c = _place()
        sends = [pltpu.make_async_remote_copy(src_ref=in_ref.at[p, pl.ds((1 - c) * half, half), :],
                                              dst_ref=got_ref.at[p], send_sem=send_sems.at[p],
                                              recv_sem=recv_sems.at[p], device_id=(x, y, 1 - c), device_id_type=MESH)
                 for p in range(pieces)]
        for cp in sends:
            cp.start()
        for cp in sends:
            cp.wait()

    return pl.pallas_call(
        body, name=name, out_shape=jax.ShapeDtypeStruct((pieces, half, width), gpack.dtype),
        in_specs=[ANY], out_specs=ANY,
        scratch_shapes=[pltpu.SemaphoreType.DMA((pieces,)), pltpu.SemaphoreType.DMA((pieces,))],
    )(gpack)


def _pair_add(gpack, got, c, *, name, tm=512):
    pieces, rows, width = gpack.shape
    half = rows // 2
    nb = half // tm

    def body(c_ref, a_ref, b_ref, sum_ref, narrow_ref):
        s = a_ref[...] + b_ref[...]
        sum_ref[...] = s
        narrow_ref[...] = s.astype(BF16)

    blk = pl.BlockSpec((None, tm, width), lambda p, i, c_ref: (p, i, 0))
    return pl.pallas_call(
        body, name=name,
        out_shape=(jax.ShapeDtypeStruct((pieces, half, width), F32), jax.ShapeDtypeStruct((pieces, half, width), BF16)),
        grid_spec=pltpu.PrefetchScalarGridSpec(
            num_scalar_prefetch=1, grid=(pieces, nb),
            in_specs=[pl.BlockSpec((None, tm, width), lambda p, i, c_ref: (p, c_ref[0] * nb + i, 0)), blk],
            out_specs=(blk, blk)),
        compiler_params=_params(("parallel", "parallel")),
    )(c, gpack, got)


def _chip_exchange(pieces, *, name):
    _, rows, width = pieces.shape

    def body(in_ref, out_ref, send_sems, recv_sems):
        x, y, c = _place()
        cps = [pltpu.make_async_remote_copy(src_ref=in_ref.at[2 * cx + cy], dst_ref=out_ref.at[j],
                                            send_sem=send_sems.at[j], recv_sem=recv_sems.at[j],
                                            device_id=(cx, cy, c), device_id_type=MESH)
               for j, (cx, cy) in enumerate(_other_chips(x, y))]
        for cp in cps:
            cp.start()
        for cp in cps:
            cp.wait()

    return pl.pallas_call(
        body, name=name, out_shape=jax.ShapeDtypeStruct((3, rows, width), pieces.dtype), in_specs=[ANY], out_specs=ANY,
        scratch_shapes=[pltpu.SemaphoreType.DMA((3,)), pltpu.SemaphoreType.DMA((3,))],
    )(pieces)


def _all_reduce_small(v, *, name):
    rows, width = v.shape

    def body(in_ref, out_ref, gath, send_sems, recv_sems):
        x, y, c = _place()
        me = 4 * x + 2 * y + c
        gath[me] = in_ref[...]
        peers = []
        for m in range(1, N_DEV):
            px = 1 - x if m & 4 else x
            py = 1 - y if m & 2 else y
            pc = 1 - c if m & 1 else c
            peers.append((m - 1, (px, py, pc), 4 * px + 2 * py + pc))
        for k, peer, _ in peers:
            pltpu.make_async_remote_copy(src_ref=in_ref, dst_ref=gath.at[me], send_sem=send_sems.at[k],
                                         recv_sem=recv_sems.at[k], device_id=peer, device_id_type=MESH).start()
        for k, peer, plin in peers:
            cp = pltpu.make_async_remote_copy(src_ref=in_ref, dst_ref=gath.at[plin], send_sem=send_sems.at[k],
                                              recv_sem=recv_sems.at[k], device_id=peer, device_id_type=MESH)
            cp.wait_send()
            cp.wait_recv()
        acc = gath[0]
        for d in range(1, N_DEV):
            acc = acc + gath[d]
        out_ref[...] = acc

    vmem = pl.BlockSpec(memory_space=pltpu.VMEM)
    return pl.pallas_call(
        body, name=name, out_shape=jax.ShapeDtypeStruct(v.shape, v.dtype), in_specs=[vmem], out_specs=vmem,
        scratch_shapes=[pltpu.VMEM((N_DEV, rows, width), v.dtype),
                        pltpu.SemaphoreType.DMA((N_DEV - 1,)), pltpu.SemaphoreType.DMA((N_DEV - 1,))],
        compiler_params=pltpu.CompilerParams(vmem_limit_bytes=VMEM_LIMIT_V7X),
    )(v)


def _reduce_scatter(gpack):
    x, y, c = _place()
    half = gpack.shape[1] // 2
    got = _pair_exchange(gpack, name="rs_pair_swap")
    pair, pair16 = _pair_add(gpack, got, c.astype(jnp.int32).reshape(1), name="rs_pair_add")
    recv = _chip_exchange(pair16, name="rs_chip_exchange")
    mine = lax.dynamic_index_in_dim(pair, 2 * x + y, axis=0, keepdims=False)
    (total,) = _rw_fwd(_f_add4, [mine, recv[0], recv[1], recv[2]], [], tm=512, name="rs_chip_add")
    theirs = _sibling_swap(total, name="rs_share_swap")
    return jnp.concatenate([jnp.where(c == 0, total, theirs), jnp.where(c == 0, theirs, total)], axis=0)


_SHARDED = (("w_kv_mem", 1), ("w_o", 1), ("mlp_w1", 2), ("mlp_w2", 1), ("gdn_w_in", 2), ("gdn_conv_w", 2),
            ("s5_w_in", 2), ("s5_d", 1), ("s5_w_glu", 1), ("s5_b_glu", 1))
_MATMUL_ONLY = ("w_kv_mem", "w_o", "mlp_w1", "mlp_w2", "gdn_w_in", "s5_w_in", "s5_w_glu")
_KEPT_BLOCKED = ("mlp_w1",)
_REPLICATED = ("ln1_g", "ln1_b", "ln2_g", "ln2_b", "gdn_a_log", "gdn_dt_bias", "gdn_norm_g", "s5_a_re", "s5_a_im",
               "s5_b_re", "s5_b_im", "s5_c_re", "s5_c_im", "s5_log_dt")
_WEIGHTS = ("w_kv_mem", "w_o", "ln1_g", "ln1_b", "ln2_g", "ln2_b", "mlp_w1", "mlp_w2", "gdn_w_in", "gdn_conv_w",
            "gdn_a_log", "gdn_dt_bias", "gdn_norm_g", "s5_w_in", "s5_a_re", "s5_a_im", "s5_b_re", "s5_b_im",
            "s5_c_re", "s5_c_im", "s5_log_dt", "s5_d", "s5_w_glu", "s5_b_glu")


ROW_ALIGN = 16


def _n_rows(shape):
    return -(-math.prod(shape) // (ROW_ALIGN * D_MODEL)) * ROW_ALIGN


def _as_rows(a):
    rows = _n_rows(a.shape)
    if a.shape[-1] == D_MODEL and a.size == rows * D_MODEL:
        return a.reshape(-1, D_MODEL)
    flat = a.reshape(-1)
    return jnp.pad(flat, (0, rows * D_MODEL - flat.size)).reshape(rows, D_MODEL)


def _pack(arrs, unit_rows=SHARD_ROWS):
    rows = [_as_rows(a) for a in arrs]
    pad = -sum(r.shape[0] for r in rows) % unit_rows
    if pad:
        rows.append(jnp.zeros((pad, D_MODEL), rows[0].dtype))
    return jnp.concatenate(rows, axis=0)


def _unpack(packed, shapes):
    lead = packed.shape[:-2]
    out, off = [], 0
    for s in shapes:
        r = _n_rows(s)
        seg = lax.slice_in_dim(packed, off, off + r, axis=len(lead))
        if s[-1] != D_MODEL or math.prod(s) != r * D_MODEL:
            seg = lax.slice_in_dim(seg.reshape(lead + (-1,)), 0, math.prod(s), axis=len(lead))
        out.append(seg.reshape(lead + tuple(s)))
        off += r
    return out


def _split3(t):
    hi = t.astype(BF16)
    r1 = t - hi.astype(F32)
    mid = r1.astype(BF16)
    lo = (r1 - mid.astype(F32)).astype(BF16)
    return jnp.stack([hi, mid, lo], axis=-1)


def _join3(t):
    return (t[..., 0].astype(F32) + t[..., 1].astype(F32)) + t[..., 2].astype(F32)


def _merge_chips(blocks, axis):
    return jnp.concatenate([blocks[s] for s in range(N_CHIPS)], axis=axis)


def _pack_for_chips(weights):
    rows = []
    for s in range(N_CHIPS):
        chip = []
        for layers, axis in weights:
            if axis is None:
                blocks = [g[s] for g in layers]
            else:
                n = layers[0].shape[axis] // N_CHIPS
                blocks = [lax.slice_in_dim(g, s * n, (s + 1) * n, axis=axis) for g in layers]
            if math.prod(blocks[0].shape) % (ROW_ALIGN * D_MODEL) == 0:
                chip += [_as_rows(b) for b in blocks]
            else:
                chip.append(_as_rows(jnp.stack(blocks)))
        pad = -sum(r.shape[0] for r in chip) % SHARD_ROWS
        rows += chip + ([jnp.zeros((pad, D_MODEL), F32)] if pad else [])
    return jnp.concatenate(rows, axis=0).reshape(N_CHIPS, -1, D_MODEL)


def kernel(x, mem, w_kv_mem, w_o, ln1_g, ln1_b, ln2_g, ln2_b, mlp_w1, mlp_w2, gdn_w_in, gdn_conv_w, gdn_a_log, gdn_dt_bias, gdn_norm_g, s5_w_in, s5_a_re, s5_a_im, s5_b_re, s5_b_im, s5_c_re, s5_c_im, s5_log_dt, s5_d, s5_w_glu, s5_b_glu, loss_target, m_w_kv_mem, m_w_o, m_ln1_g, m_ln1_b, m_ln2_g, m_ln2_b, m_mlp_w1, m_mlp_w2, m_gdn_w_in, m_gdn_conv_w, m_gdn_a_log, m_gdn_dt_bias, m_gdn_norm_g, m_s5_w_in, m_s5_a_re, m_s5_a_im, m_s5_b_re, m_s5_b_im, m_s5_c_re, m_s5_c_im, m_s5_log_dt, m_s5_d, m_s5_w_glu, m_s5_b_glu, v_w_kv_mem, v_w_o, v_ln1_g, v_ln1_b, v_ln2_g, v_ln2_b, v_mlp_w1, v_mlp_w2, v_gdn_w_in, v_gdn_conv_w, v_gdn_a_log, v_gdn_dt_bias, v_gdn_norm_g, v_s5_w_in, v_s5_a_re, v_s5_a_im, v_s5_b_re, v_s5_b_im, v_s5_c_re, v_s5_c_im, v_s5_log_dt, v_s5_d, v_s5_w_glu, v_s5_b_glu):
    given = dict(locals())
    w = {n: given[n] for n in _WEIGHTS}
    mom = {n: given["m_" + n] for n in _WEIGHTS}
    var = {n: given["v_" + n] for n in _WEIGHTS}
    shard_names = [n for n, _ in _SHARDED]
    shard_shapes = [w[n].shape for n in shard_names]
    rep_shapes = [w[n].shape for n in _REPLICATED]

    wire = {n: w[n].astype(BF16) if n in _MATMUL_ONLY else _split3(w[n]) for n in shard_names}
    first = {n: 0 if n.startswith("s5_") else 1 for n in shard_names}
    me_chip = 2 * lax.axis_index("x") + lax.axis_index("y")
    early = [wire[n][:first[n]] for n in shard_names if first[n]]
    late = [wire[n][first[n]:] for n in shard_names]
    early_pack, late_pack = _pack(early), _pack(late)
    landed = _all_gather_chips(early_pack, name="gather_first_layer")
    landed = lax.dynamic_update_index_in_dim(landed, early_pack, me_chip, axis=0)
    early_blocks = dict(zip([n for n in shard_names if first[n]], _unpack(landed, [a.shape for a in early])))
    send_sems, recv_sems, pack_thru, land_thru, token = _gather_start(late_pack, after=landed)
    axis_of = dict(_SHARDED)

    def merged(n, blk):
        if n in _KEPT_BLOCKED:
            return blk
        return _merge_chips(blk if n in _MATMUL_ONLY else _join3(blk), axis_of[n] - 1)

    late_full = {}

    def weights_of(i, h):
        if i == 0:
            full = {n: [merged(n, blk[:, 0])] for n, blk in early_blocks.items()}
            full["gdn_w_in"][0] = full["gdn_w_in"][0] + token[0, 0].astype(BF16)
        else:
            if not late_full:
                land = _gather_wait(send_sems, recv_sems, pack_thru, land_thru, after=h)
                land = _gather_forward(land, name="gather_forward")
                land = lax.dynamic_update_index_in_dim(land, late_pack, me_chip, axis=0)
                for n, blk in zip(shard_names, _unpack(land, [a.shape for a in late])):
                    late_full[n] = [None] * first[n] + [merged(n, blk[:, t]) for t in range(blk.shape[1])]
            full = dict(late_full)
        full.update({n: w[n] for n in _REPLICATED})
        return _layer_weights(full, i)

    loss, grad_x, grads = _local_step(x[0], mem[0], loss_target[0], weights_of)
    loss = lax.psum(loss, ("x", "y", "c"))

    g_shard = _reduce_scatter(_pack_for_chips([(grads[n], None if n in _KEPT_BLOCKED else ax - 1)
                                               for n, ax in _SHARDED]))
    def pack_small(d):
        return _pack([d[n] for n in _REPLICATED], unit_rows=SMALL_ROWS)

    g_rep = _all_reduce_small(pack_small(grads), name="reduce_replicated")

    def adamw(wp, gp, mp, vp, name):
        return _rw_fwd(_f_adamw, [wp, gp, mp, vp], [], tm=256, name=name)

    outs = {}
    for n, g in zip(shard_names, _unpack(g_shard, shard_shapes)):
        flat = (-1, w[n].shape[-1])
        res = adamw(w[n].reshape(flat), g.reshape(flat), mom[n].reshape(flat), var[n].reshape(flat), "adamw_" + n)
        outs[("grad", n)] = g
        outs.update({(kind, n): a.reshape(w[n].shape) for kind, a in zip(("delta", "new_m", "new_v"), res)})
    packed = (g_rep,) + tuple(adamw(pack_small(w), g_rep, pack_small(mom), pack_small(var), "adamw_replicated"))
    for kind, pr in zip(("grad", "delta", "new_m", "new_v"), packed):
        outs.update({(kind, n): a for n, a in zip(_REPLICATED, _unpack(pr, rep_shapes))})
    return (loss, grad_x[None]) + tuple(outs[(kind, n)] for kind in ("grad", "delta", "new_m", "new_v")
                                        for n in _WEIGHTS)
```

```python
import functools
import math

import jax
import jax.numpy as jnp
from jax import lax
from jax.experimental import pallas as pl
from jax.experimental.pallas import tpu as pltpu

F32 = jnp.float32
BF16 = jnp.bfloat16
MESH = pl.DeviceIdType.MESH

D_MODEL = 1024
DEPTH = 4
GDN_HEADS = 8
HEAD_DIM = 128
GDN_CONV = 4
GDN_CHUNK = 64
S5_GROUPS = 64
S5_GROUP = 16
S5_STATE = 64
XA_HEADS = 4
XA_DIM = 512
D_FF = 4096
DN_ALPHA = (2 * DEPTH) ** 0.25
LN_EPS = 1e-5
RMS_EPS = 1e-6
ADAM_LR, ADAM_B1, ADAM_B2, ADAM_EPS, ADAM_WD, ADAM_STEP = 0.001, 0.9, 0.999, 1e-08, 0.01, 10

VMEM_LIMIT_V7X = 56 * 1024 * 1024
LANES = 128
SUBLANES = 8
S5_T = 16
S5_TILES = D_MODEL // LANES
N_CHIPS = 4
N_DEV = 8


def _params(sem):
    return pltpu.CompilerParams(dimension_semantics=sem, vmem_limit_bytes=VMEM_LIMIT_V7X)


def _tile(n, pref):
    if n <= pref:
        return n
    t = (pref // LANES) * LANES
    while n % t:
        t -= LANES
    return t


def _row_tile(n, pref):
    if n % SUBLANES:
        return n
    t = min(pref, n) // SUBLANES * SUBLANES
    while n % t:
        t -= SUBLANES
    return t


def _col_blocked_spec(rows_tile, cols_tile, block_cols, rows_axis, cols_axis):
    r = block_cols // cols_tile

    def index(*ijk):
        c = ijk[cols_axis]
        return (c, ijk[rows_axis], 0) if r == 1 else (c // r, ijk[rows_axis], c % r)

    return pl.BlockSpec((None, rows_tile, cols_tile), index)


def _mm(a, b, *, ta=False, tb=False, acc=None, name, tm=1024, tn=1024, tk=1024, out_blocks=0):
    k_dim, m_dim = a.shape if ta else a.shape[::-1]
    b_rows, b_cols = (b.shape[0], b.shape[1]) if b.ndim == 2 else (b.shape[1], b.shape[0] * b.shape[2])
    n_dim = b_rows if tb else b_cols
    assert (b_cols if tb else b_rows) == k_dim, (a.shape, b.shape, ta, tb)
    limit_n = n_dim // out_blocks if out_blocks else (n_dim if b.ndim == 2 or tb else b.shape[2])
    limit_k = b.shape[2] if (b.ndim == 3 and tb) else k_dim
    tm, tn, tk = _tile(m_dim, tm), _tile(limit_n, min(tn, limit_n)), _tile(limit_k, min(tk, limit_k))
    a_spec = (pl.BlockSpec((tk, tm), lambda i, j, k: (k, i)) if ta else pl.BlockSpec((tm, tk), lambda i, j, k: (i, k)))
    if b.ndim == 3:
        b_spec = (_col_blocked_spec(tn, tk, b.shape[2], 1, 2) if tb else _col_blocked_spec(tk, tn, b.shape[2], 2, 1))
    else:
        b_spec = (pl.BlockSpec((tn, tk), lambda i, j, k: (j, k)) if tb
                  else pl.BlockSpec((tk, tn), lambda i, j, k: (k, j)))
    o_spec = (_col_blocked_spec(tm, tn, n_dim // out_blocks, 0, 1) if out_blocks
              else pl.BlockSpec((tm, tn), lambda i, j, k: (i, j)))
    o_shape = (out_blocks, m_dim, n_dim // out_blocks) if out_blocks else (m_dim, n_dim)
    dn = (((0 if ta else 1,), (1 if tb else 0,)), ((), ()))
    has_acc = acc is not None

    def body(*refs):
        a_ref, b_ref = refs[0], refs[1]
        o_ref = refs[-1]
        k = pl.program_id(2)
        p = lax.dot_general(a_ref[...].astype(BF16), b_ref[...].astype(BF16), dn,
                            preferred_element_type=F32)

        @pl.when(k == 0)
        def _():
            o_ref[...] = p + refs[2][...] if has_acc else p

        @pl.when(k > 0)
        def _():
            o_ref[...] += p

    return pl.pallas_call(
        body, name=name,
        out_shape=jax.ShapeDtypeStruct(o_shape, F32),
        grid=(m_dim // tm, n_dim // tn, k_dim // tk),
        in_specs=[a_spec, b_spec] + ([o_spec] if has_acc else []),
        out_specs=o_spec,
        compiler_params=_params(("parallel", "parallel", "arbitrary")),
    )(*([a, b] + ([acc] if has_acc else [])))


def _mm_relu2(a, b, *, name, tm=1024):
    m_dim, k_dim = a.shape
    n_blocks, _, tn = b.shape
    n_dim = n_blocks * tn
    tm = _tile(m_dim, tm)

    def body(a_ref, b_ref, h_ref, act_ref):
        h = jnp.dot(a_ref[...].astype(BF16), b_ref[...].astype(BF16), preferred_element_type=F32)
        h_ref[...] = h
        r = jnp.maximum(h, 0.0)
        act_ref[...] = (r * r).astype(BF16)

    o_spec = pl.BlockSpec((tm, tn), lambda i, j: (i, j))
    return pl.pallas_call(
        body, name=name,
        out_shape=(jax.ShapeDtypeStruct((m_dim, n_dim), F32), jax.ShapeDtypeStruct((m_dim, n_dim), BF16)),
        grid=(m_dim // tm, n_dim // tn),
        in_specs=[pl.BlockSpec((tm, k_dim), lambda i, j: (i, 0)),
                  pl.BlockSpec((None, k_dim, tn), lambda i, j: (j, 0, 0))],
        out_specs=(o_spec, o_spec),
        compiler_params=_params(("parallel", "parallel")),
    )(a, b)


def _mm_relu2_grad(d, b, h, *, name, tm=1024, tn=1024):
    m_dim, k_dim = d.shape
    n_dim = b.shape[0]
    tm, tn = _tile(m_dim, tm), _tile(n_dim, tn)

    def body(d_ref, b_ref, h_ref, o_ref):
        p = lax.dot_general(d_ref[...].astype(BF16), b_ref[...].astype(BF16), ((NT), ((), ())),
                            preferred_element_type=F32)
        o_ref[...] = (p * (2.0 * jnp.maximum(h_ref[...], 0.0))).astype(BF16)

    o_spec = pl.BlockSpec((tm, tn), lambda i, j: (i, j))
    return pl.pallas_call(
        body, name=name,
        out_shape=jax.ShapeDtypeStruct((m_dim, n_dim), BF16),
        grid=(m_dim // tm, n_dim // tn),
        in_specs=[pl.BlockSpec((tm, k_dim), lambda i, j: (i, 0)), pl.BlockSpec((tn, k_dim), lambda i, j: (j, 0)), o_spec],
        out_specs=o_spec,
        compiler_params=_params(("parallel", "parallel")),
    )(d, b, h)


def _rowwise(f, rows, params, row_out, acc_out, *, tm, name):
    length = rows[0].shape[0]
    tm = _row_tile(length, tm)
    nr, npar, nro = len(rows), len(params), len(row_out)

    def body(*refs):
        ins = [r[...] for r in refs[:nr + npar]]
        outs = refs[nr + npar:]
        r_o, a_o = f(*ins)
        for ref, val in zip(outs[:nro], r_o):
            ref[...] = val.astype(ref.dtype)
        i = pl.program_id(0)
        for ref, val in zip(outs[nro:], a_o):
            @pl.when(i == 0)
            def _(ref=ref, val=val):
                ref[...] = val.astype(ref.dtype)

            @pl.when(i > 0)
            def _(ref=ref, val=val):
                ref[...] += val.astype(ref.dtype)

    in_specs = ([pl.BlockSpec((tm, r.shape[1]), lambda i: (i, 0)) for r in rows]
                + [pl.BlockSpec(p.shape, lambda i: (0, 0)) for p in params])
    out_specs = ([pl.BlockSpec((tm, w), lambda i: (i, 0)) for w, _ in row_out]
                 + [pl.BlockSpec(s, lambda i: (0, 0)) for s, _ in acc_out])
    out_shape = ([jax.ShapeDtypeStruct((length, w), dt) for w, dt in row_out]
                 + [jax.ShapeDtypeStruct(s, dt) for s, dt in acc_out])
    res = pl.pallas_call(
        body, name=name, out_shape=out_shape, grid=(length // tm,),
        in_specs=in_specs, out_specs=out_specs,
        compiler_params=_params(("arbitrary",) if acc_out else ("parallel",)),
    )(*rows, *params)
    return res[:nro], res[nro:]


def _rw_fwd(f, rows, params, *, tm, name):
    tm_ = _row_tile(rows[0].shape[0], tm)
    shapes = jax.eval_shape(f, *[jax.ShapeDtypeStruct((tm_, r.shape[1]), r.dtype) for r in rows],
                            *[jax.ShapeDtypeStruct(p.shape, p.dtype) for p in params])
    row_out = [(s.shape[1], s.dtype) for s in shapes]
    outs, _ = _rowwise(lambda *v: (f(*v), ()), rows, params, row_out, [], tm=tm, name=name)
    return outs


def _rw_bwd(f, rows, params, cots, *, row_grad, param_grad, tm, name):
    nr, npar, nct = len(rows), len(params), len(cots)

    def g(*vals):
        prim = vals[:nr] + vals[nr + nct:]
        ct = vals[nr:nr + nct]
        _, vjp = jax.vjp(f, *prim)
        grads = vjp(tuple(ct))
        return (tuple(grads[i] for i in range(nr) if row_grad[i]),
                tuple(grads[nr + i] for i in range(npar) if param_grad[i]))

    row_out = [(rows[i].shape[1], F32) for i in range(nr) if row_grad[i]]
    acc_out = [(params[i].shape, F32) for i in range(npar) if param_grad[i]]
    return _rowwise(g, list(rows) + list(cots), params, row_out, acc_out, tm=tm, name=name)


def _f_ln_res(x, h, g, b):
    pre = DN_ALPHA * x + h
    mu = jnp.mean(pre, axis=-1, keepdims=True)
    d = pre - mu
    var = jnp.mean(d * d, axis=-1, keepdims=True)
    return (d * lax.rsqrt(var + LN_EPS) * g + b,)


def _silu(t):
    return t * jax.nn.sigmoid(t)


def _f_gdn_qkv(c):
    a = _silu(c)
    outs = []
    for part, scale in ((0, HEAD_DIM ** -0.5), (1, 1.0)):
        heads = []
        for h in range(GDN_HEADS):
            t = a[:, part * D_MODEL + h * HEAD_DIM: part * D_MODEL + (h + 1) * HEAD_DIM]
            t = t * lax.rsqrt(jnp.sum(t * t, axis=-1, keepdims=True) + 1e-6)
            heads.append(t * scale if scale != 1.0 else t)
        outs.append(jnp.concatenate(heads, axis=-1))
    outs.append(a[:, 2 * D_MODEL:])
    return tuple(outs)


def _f_gdn_out(o, z, norm_g):
    heads = []
    for h in range(GDN_HEADS):
        t = o[:, h * HEAD_DIM:(h + 1) * HEAD_DIM]
        t = t * lax.rsqrt(jnp.mean(t * t, axis=-1, keepdims=True) + RMS_EPS) * norm_g
        heads.append(t)
    return (jnp.concatenate(heads, axis=-1) * _silu(z),)


def _f_attn(xq, kmem, vmem):
    heads = []
    for h in range(XA_HEADS):
        sl = slice(h * HEAD_DIM, (h + 1) * HEAD_DIM)
        s = lax.dot_general(xq[:, sl].astype(BF16), kmem[:, sl].astype(BF16),
                            (((1,), (1,)), ((), ())), preferred_element_type=F32) * (HEAD_DIM ** -0.5)
        m = lax.stop_gradient(jnp.max(s, axis=-1, keepdims=True))
        e = jnp.exp(s - m)
        p = e / jnp.sum(e, axis=-1, keepdims=True)
        heads.append(jnp.dot(p.astype(BF16), vmem[:, sl].astype(BF16), preferred_element_type=F32))
    return (jnp.concatenate(heads, axis=-1),)


def _f_s5_gelu(y, u, d):
    return (jax.nn.gelu(y + d * u),)


def _f_s5_gate(zg, t, b):
    return (zg * jax.nn.sigmoid(t + b),)


def _f_add(a, b):
    return (a + b,)


def _f_add4(a, b, c, d):
    return (((a + b.astype(F32)) + c.astype(F32)) + d.astype(F32),)


def _f_adamw(w, g, m, v):
    m = ADAM_B1 * m + (1.0 - ADAM_B1) * g
    v = ADAM_B2 * v + (1.0 - ADAM_B2) * jnp.square(g)
    m_hat = m / (1.0 - ADAM_B1 ** ADAM_STEP)
    v_hat = v / (1.0 - ADAM_B2 ** ADAM_STEP)
    delta = -ADAM_LR * (m_hat / (jnp.sqrt(v_hat) + ADAM_EPS) + ADAM_WD * w)
    return delta, m, v


def _conv_fwd(u, w, *, tm, name):
    length, chans = u.shape
    tm = min(tm, length)
    tc = _tile(chans, 1024)
    hb = tm // SUBLANES

    def body(cur_ref, prev_ref, w_ref, o_ref, buf):
        i = pl.program_id(1)
        buf[0:SUBLANES, :] = jnp.where(i > 0, prev_ref[...], 0.0)
        buf[SUBLANES:, :] = cur_ref[...]
        acc = buf[pl.ds(SUBLANES - 3, tm), :] * w_ref[0:1, :]
        for k in range(1, GDN_CONV):
            acc = acc + buf[pl.ds(SUBLANES - 3 + k, tm), :] * w_ref[k:k + 1, :]
        o_ref[...] = acc

    return pl.pallas_call(
        body, name=name, out_shape=jax.ShapeDtypeStruct(u.shape, F32),
        grid=(chans // tc, length // tm),
        in_specs=[pl.BlockSpec((tm, tc), lambda j, i: (i, j)),
                  pl.BlockSpec((SUBLANES, tc), lambda j, i: (jnp.maximum(i * hb - 1, 0), j)),
                  pl.BlockSpec((GDN_CONV, tc), lambda j, i: (0, j))],
        out_specs=pl.BlockSpec((tm, tc), lambda j, i: (i, j)),
        scratch_shapes=[pltpu.VMEM((tm + SUBLANES, tc), F32)],
        compiler_params=_params(("parallel", "parallel")),
    )(u, u, w)


def _conv_bwd(u, w, dc, *, tm, name):
    length, chans = u.shape
    tm = min(tm, length)
    tc = _tile(chans, 1024)
    hb = tm // SUBLANES
    last = length // tm - 1

    def body(u_ref, uprev_ref, dc_ref, dcnext_ref, w_ref, du_ref, dw_ref, ubuf, dbuf):
        i = pl.program_id(1)
        ubuf[0:SUBLANES, :] = jnp.where(i > 0, uprev_ref[...], 0.0)
        ubuf[SUBLANES:, :] = u_ref[...]
        dbuf[0:tm, :] = dc_ref[...]
        dbuf[tm:, :] = jnp.where(i < last, dcnext_ref[...], 0.0)
        dcv = dc_ref[...]
        du = dbuf[pl.ds(3, tm), :] * w_ref[0:1, :]
        rows = [jnp.sum(dcv * ubuf[pl.ds(SUBLANES - 3, tm), :], axis=0, keepdims=True)]
        for k in range(1, GDN_CONV):
            du = du + dbuf[pl.ds(3 - k, tm), :] * w_ref[k:k + 1, :]
            rows.append(jnp.sum(dcv * ubuf[pl.ds(SUBLANES - 3 + k, tm), :], axis=0, keepdims=True))
        du_ref[...] = du
        dwv = jnp.concatenate(rows, axis=0)

        @pl.when(i == 0)
        def _():
            dw_ref[...] = dwv

        @pl.when(i > 0)
        def _():
            dw_ref[...] += dwv

    return pl.pallas_call(
        body, name=name,
        out_shape=(jax.ShapeDtypeStruct(u.shape, F32), jax.ShapeDtypeStruct((GDN_CONV, chans), F32)),
        grid=(chans // tc, length // tm),
        in_specs=[pl.BlockSpec((tm, tc), lambda j, i: (i, j)),
                  pl.BlockSpec((SUBLANES, tc), lambda j, i: (jnp.maximum(i * hb - 1, 0), j)),
                  pl.BlockSpec((tm, tc), lambda j, i: (i, j)),
                  pl.BlockSpec((SUBLANES, tc), lambda j, i: (jnp.minimum((i + 1) * hb, (last + 1) * hb - 1), j)),
                  pl.BlockSpec((GDN_CONV, tc), lambda j, i: (0, j))],
        out_specs=(pl.BlockSpec((tm, tc), lambda j, i: (i, j)),
                   pl.BlockSpec((GDN_CONV, tc), lambda j, i: (0, j))),
        scratch_shapes=[pltpu.VMEM((tm + SUBLANES, tc), F32), pltpu.VMEM((tm + SUBLANES, tc), F32)],
        compiler_params=_params(("parallel", "arbitrary")),
    )(u, u, dc, dc, w)


def _dot(a, b, dims, precision=None):
    if precision is None:
        a, b = a.astype(BF16), b.astype(BF16)
    return lax.dot_general(a, b, (dims, ((), ())), preferred_element_type=F32, precision=precision)


def _dot3(a, b, dims):
    ah, bh = a.astype(BF16), b.astype(BF16)
    al, bl = (a - ah.astype(F32)).astype(BF16), (b - bh.astype(F32)).astype(BF16)

    def d(x, y):
        return lax.dot_general(x, y, (dims, ((), ())), preferred_element_type=F32)

    return d(ah, bh) + (d(ah, bl) + d(al, bh))


NN = ((1,), (0,))
NT = ((1,), (1,))
TN = ((0,), (0,))
HI = lax.Precision.HIGHEST


def _hmap(f, *lists):
    return [f(*t) for t in zip(*lists)]


@jax.custom_vjp
def _unit_lower_inverse(a):
    c = a[0].shape[0]
    eye = (lax.broadcasted_iota(jnp.int32, (c, c), 0) == lax.broadcasted_iota(jnp.int32, (c, c), 1)).astype(F32)
    p = _hmap(lambda x: -x, a)
    t = _hmap(lambda x: eye + x, p)
    for _ in range(int(math.log2(c)) - 1):
        p = _hmap(lambda x: _dot3(x, x, NN), p)
        t = _hmap(lambda x, y: x + _dot3(x, y, NN), t, p)
    return t


def _uli_fwd(a):
    t = _unit_lower_inverse(a)
    return t, t


def _uli_bwd(t, dt):
    left = _hmap(lambda x, y: _dot3(x, y, TN), t, dt)
    return (_hmap(lambda x, y: -_dot3(x, y, NT), left, t),)


_unit_lower_inverse.defvjp(_uli_fwd, _uli_bwd)


def _gdn_chunk(q, k, v, bl, al, a_log, dt_bias, state):
    c = q[0].shape[0]
    row = lax.broadcasted_iota(jnp.int32, (c, c), 0)
    col = lax.broadcasted_iota(jnp.int32, (c, c), 1)
    causal = row >= col
    strict = row > col
    eye = (row == col).astype(F32)
    beta = _hmap(jax.nn.sigmoid, bl)
    g = _hmap(lambda a_, l_, d_: -jnp.exp(a_) * jax.nn.softplus(l_ + d_), a_log, al, dt_bias)
    g_r = _hmap(lambda x: jnp.sum(eye * x, axis=0, keepdims=True), g)
    gc = _hmap(lambda x: jnp.sum(jnp.where(causal, x, 0.0), axis=1, keepdims=True), g_r)
    gc_r = _hmap(lambda x: jnp.sum(jnp.where(row <= col, x, 0.0), axis=0, keepdims=True), g)
    decay = _hmap(lambda x, y: jnp.where(causal, jnp.exp(jnp.where(causal, x - y, 0.0)), 0.0), gc, gc_r)
    e_gc = _hmap(jnp.exp, gc)
    kb = _hmap(jnp.multiply, k, beta)
    vb = _hmap(jnp.multiply, v, beta)
    a_mat = _hmap(lambda x, y, d: jnp.where(strict, _dot(x, y, NT) * d, 0.0), kb, k, decay)
    t_inv = _unit_lower_inverse(a_mat)
    u_blk = _hmap(lambda t, x: _dot(t, x, NN), t_inv, vb)
    w_blk = _hmap(lambda t, x, e: _dot(t, x * e, NN), t_inv, kb, e_gc)
    v_new = _hmap(lambda u, w, s: u - _dot(w, s, NN), u_blk, w_blk, state)
    attn = _hmap(lambda x, y, d: _dot(x, y, NT) * d, q, k, decay)
    o_state = _hmap(lambda x, e, s: _dot(x * e, s, NN), q, e_gc, state)
    o = _hmap(lambda base, at, vn: base + _dot(at, vn, NN), o_state, attn, v_new)
    g_last = _hmap(lambda x: jnp.sum(x, axis=0, keepdims=True), g)
    k_dec = _hmap(lambda x, gl, c_: x * jnp.exp(gl - c_), k, g_last, gc)
    new_state = _hmap(lambda s, gl, kd, vn: s * jnp.exp(gl) + _dot(kd, vn, TN), state, g_last, k_dec, v_new)
    return o, new_state


def _gdn_operands(q_ref, k_ref, v_ref, bav, alog_ref, dtb_ref):
    hs = range(GDN_HEADS)
    cols = [slice(h * HEAD_DIM, (h + 1) * HEAD_DIM) for h in hs]
    return ([q_ref[:, sl] for sl in cols], [k_ref[:, sl] for sl in cols], [v_ref[:, sl] for sl in cols],
            [bav[:, h:h + 1] for h in hs], [bav[:, h + GDN_HEADS:h + GDN_HEADS + 1] for h in hs],
            [alog_ref[h:h + 1, 0:1] for h in hs], [dtb_ref[h:h + 1, 0:1] for h in hs])


def _gdn_scan_fwd(q, k, v, ba, a_log, dt_bias, *, name):
    length = q.shape[0]
    n = length // GDN_CHUNK
    c = GDN_CHUNK

    def body(q_ref, k_ref, v_ref, ba_ref, alog_ref, dtb_ref, o_ref, s_ref, state):
        i = pl.program_id(0)

        @pl.when(i == 0)
        def _():
            state[...] = jnp.zeros_like(state)

        bav = ba_ref[...]
        heads = [slice(h * HEAD_DIM, (h + 1) * HEAD_DIM) for h in range(GDN_HEADS)]
        s_in = [state[h] for h in range(GDN_HEADS)]
        o, s_out = _gdn_chunk(*_gdn_operands(q_ref, k_ref, v_ref, bav, alog_ref, dtb_ref), s_in)
        for h, sl in enumerate(heads):
            s_ref[h] = s_in[h]
            o_ref[:, sl] = o[h]
            state[h] = s_out[h]

    row_spec = pl.BlockSpec((c, D_MODEL), lambda i: (i, 0))
    small = pl.BlockSpec((GDN_HEADS, LANES), lambda i: (0, 0))
    return pl.pallas_call(
        body, name=name,
        out_shape=(jax.ShapeDtypeStruct((length, D_MODEL), F32),
                   jax.ShapeDtypeStruct((n, GDN_HEADS, HEAD_DIM, HEAD_DIM), F32)),
        grid=(n,),
        in_specs=[row_spec, row_spec, row_spec, pl.BlockSpec((c, LANES), lambda i: (i, 0)), small, small],
        out_specs=(row_spec, pl.BlockSpec((None, GDN_HEADS, HEAD_DIM, HEAD_DIM), lambda i: (i, 0, 0, 0))),
        scratch_shapes=[pltpu.VMEM((GDN_HEADS, HEAD_DIM, HEAD_DIM), F32)],
        compiler_params=_params(("arbitrary",)),
    )(q, k, v, ba, a_log, dt_bias)


def _gdn_scan_bwd(q, k, v, ba, a_log, dt_bias, states, do, *, name):
    length = q.shape[0]
    n = length // GDN_CHUNK
    c = GDN_CHUNK

    def body(q_ref, k_ref, v_ref, ba_ref, alog_ref, dtb_ref, s_ref, do_ref,
             dq_ref, dk_ref, dv_ref, dba_ref, dalog_ref, ddtb_ref, dstate):
        i = pl.program_id(0)

        @pl.when(i == 0)
        def _():
            dstate[...] = jnp.zeros_like(dstate)
            dalog_ref[...] = jnp.zeros_like(dalog_ref)
            ddtb_ref[...] = jnp.zeros_like(ddtb_ref)

        bav = ba_ref[...]
        lane = lax.broadcasted_iota(jnp.int32, (c, LANES), 1)
        sub8 = lax.broadcasted_iota(jnp.int32, (GDN_HEADS, LANES), 0)
        lane8 = lax.broadcasted_iota(jnp.int32, (GDN_HEADS, LANES), 1)
        slab = jnp.zeros((c, LANES), F32)
        dalog_all = jnp.zeros((GDN_HEADS, LANES), F32)
        ddtb_all = jnp.zeros((GDN_HEADS, LANES), F32)
        heads = [slice(h * HEAD_DIM, (h + 1) * HEAD_DIM) for h in range(GDN_HEADS)]
        ds_in = [dstate[h] for h in range(GDN_HEADS)]
        s_in = [s_ref[h] for h in range(GDN_HEADS)]
        _, vjp = jax.vjp(_gdn_chunk, *_gdn_operands(q_ref, k_ref, v_ref, bav, alog_ref, dtb_ref), s_in)
        dq, dk, dv, dbl, dal, dalog, ddtb, ds = vjp(([do_ref[:, sl] for sl in heads], ds_in))
        for h, sl in enumerate(heads):
            dq_ref[:, sl] = dq[h]
            dk_ref[:, sl] = dk[h]
            dv_ref[:, sl] = dv[h]
            dstate[h] = ds[h]
            slab = slab + jnp.where(lane == h, dbl[h], 0.0) + jnp.where(lane == h + GDN_HEADS, dal[h], 0.0)
            here = (sub8 == h) & (lane8 == 0)
            dalog_all = dalog_all + jnp.where(here, dalog[h], 0.0)
            ddtb_all = ddtb_all + jnp.where(here, ddtb[h], 0.0)
        dba_ref[...] = slab
        dalog_ref[...] += dalog_all
        ddtb_ref[...] += ddtb_all

    row_spec = pl.BlockSpec((c, D_MODEL), lambda i: (n - 1 - i, 0))
    small = pl.BlockSpec((GDN_HEADS, LANES), lambda i: (0, 0))
    return pl.pallas_call(
        body, name=name,
        out_shape=(jax.ShapeDtypeStruct((length, D_MODEL), F32),) * 3
        + (jax.ShapeDtypeStruct((length, LANES), F32),
           jax.ShapeDtypeStruct((GDN_HEADS, LANES), F32), jax.ShapeDtypeStruct((GDN_HEADS, LANES), F32)),
        grid=(n,),
        in_specs=[row_spec, row_spec, row_spec,
                  pl.BlockSpec((c, LANES), lambda i: (n - 1 - i, 0)), small, small,
                  pl.BlockSpec((None, GDN_HEADS, HEAD_DIM, HEAD_DIM), lambda i: (n - 1 - i, 0, 0, 0)),
                  row_spec],
        out_specs=(row_spec, row_spec, row_spec,
                   pl.BlockSpec((c, LANES), lambda i: (n - 1 - i, 0)), small, small),
        scratch_shapes=[pltpu.VMEM((GDN_HEADS, HEAD_DIM, HEAD_DIM), F32)],
        compiler_params=_params(("arbitrary",)),
    )(q, k, v, ba, a_log, dt_bias, states, do)


S5_W = S5_T * LANES
S5_S = 2 * 8 * S5_STATE
S5_SH = S5_S // 2


def _iota2(shape):
    return lax.broadcasted_iota(jnp.int32, shape, 0), lax.broadcasted_iota(jnp.int32, shape, 1)


def _s5_rep_t(t, dtype):
    row, col = _iota2((S5_T * S5_GROUP, LANES))
    return ((jnp.right_shift(row, 4) == t) & (jnp.bitwise_and(row, 15) == jnp.bitwise_and(col, 15))).astype(dtype)


def _s5_rep_state(dtype):
    row, col = _iota2((2 * S5_STATE, S5_S))
    return ((jnp.right_shift(row, 6) == jnp.right_shift(col, 9))
            & (jnp.bitwise_and(row, 63) == jnp.bitwise_and(col, 63))).astype(dtype)


def _s5_masks():
    row, col = _iota2((LANES, LANES))
    m_ab = jnp.right_shift(row, 4) == jnp.right_shift(col, 4)
    row, col = _iota2((S5_S, LANES))
    m_e = jnp.bitwise_and(jnp.right_shift(row, 6), 7) == jnp.right_shift(col, 4)
    row, col = _iota2((LANES, S5_S))
    m_f = jnp.right_shift(row, 4) == jnp.bitwise_and(jnp.right_shift(col, 6), 7)
    return m_ab, m_e, m_f


def _s5_expand(kx_ref, ec_ref, fc_ref, kb_scr, e_scr, f_scr):
    m_ab, m_e, m_f = _s5_masks()
    kx = kx_ref[...].astype(BF16)
    ec = ec_ref[...].astype(BF16)
    rep_state = _s5_rep_state(BF16)
    for t in range(S5_T):
        rep = _s5_rep_t(t, BF16)
        cols = slice(t * LANES, (t + 1) * LANES)
        kb_scr[t] = jnp.where(m_ab, jnp.dot(kx, rep, preferred_element_type=F32), 0.0).astype(BF16)
        e_scr[:, cols] = jnp.where(m_e, jnp.dot(ec, rep, preferred_element_type=F32), 0.0).astype(BF16)
        f_scr[cols, :] = jnp.where(m_f, jnp.dot(fc_ref[t].astype(BF16), rep_state, preferred_element_type=F32),
                                   0.0).astype(BF16)


def _s5_token_rows(ref, n):
    return [ref[pl.ds(t, n, stride=S5_T), :].astype(BF16) for t in range(S5_T)]


def _s5_scan_fwd(u, kx, ec, fc, at, *, name):
    length = u.shape[0]
    n = length // S5_T
    assert n % SUBLANES == 0

    def body(u_ref, kx_ref, ec_ref, fc_ref, at_ref, y_ref, h_ref, kb_scr, e_scr, f_scr, g_scr):
        _s5_expand(kx_ref, ec_ref, fc_ref, kb_scr, e_scr, f_scr)
        us = _s5_token_rows(u_ref, n)
        g_scr[...] = jnp.dot(jnp.concatenate(us, axis=1), f_scr[...], preferred_element_type=F32)
        ar, ai = at_ref[:, :S5_SH], at_ref[:, S5_SH:]

        def step(blk, h):
            base = pl.multiple_of(blk * SUBLANES, SUBLANES)
            g8 = g_scr[pl.ds(base, SUBLANES), :]
            rows = []
            for r in range(SUBLANES):
                rows.append(h)
                hr, hi = h[:, :S5_SH], h[:, S5_SH:]
                h = jnp.concatenate([ar * hr - ai * hi, ar * hi + ai * hr], axis=1) + g8[r:r + 1, :]
            h_ref[pl.ds(base, SUBLANES), :] = jnp.concatenate(rows, axis=0)
            return h

        lax.fori_loop(0, n // SUBLANES, step, jnp.zeros((1, S5_S), F32))
        hb = h_ref[...].astype(BF16)
        for t in range(S5_T):
            acc = jnp.dot(hb, e_scr[:, t * LANES:(t + 1) * LANES], preferred_element_type=F32)
            for s in range(t + 1):
                acc = acc + jnp.dot(us[s], kb_scr[t - s], preferred_element_type=F32)
            y_ref[pl.ds(t, n, stride=S5_T), :] = acc

    return pl.pallas_call(
        body, name=name,
        out_shape=(jax.ShapeDtypeStruct((length, D_MODEL), F32), jax.ShapeDtypeStruct((S5_TILES, n, S5_S), F32)),
        grid=(S5_TILES,),
        in_specs=[pl.BlockSpec((length, LANES), lambda k: (0, k)), _s5_spec(LANES, S5_T * S5_GROUP),
                  _s5_spec(S5_S, S5_T * S5_GROUP), _s5_spec(S5_T, LANES, LANES), _s5_spec(1, S5_S)],
        out_specs=(pl.BlockSpec((length, LANES), lambda k: (0, k)), _s5_spec(n, S5_S)),
        scratch_shapes=[pltpu.VMEM((S5_T, LANES, LANES), BF16), pltpu.VMEM((S5_S, S5_W), BF16),
                        pltpu.VMEM((S5_W, S5_S), BF16), pltpu.VMEM((n, S5_S), F32)],
        compiler_params=_params(("parallel",)),
    )(u, kx, ec, fc, at)


def _s5_spec(*tail):
    return pl.BlockSpec((None,) + tail, lambda k: (k,) + (0,) * len(tail))


def _s5_scan_bwd(dy, kx, ec, fc, at, hs, *, name):
    length = dy.shape[0]
    n = length // S5_T

    def body(dy_ref, kx_ref, ec_ref, fc_ref, at_ref, h_ref, du_ref, dg_ref, dat_ref, kb_scr, e_scr, f_scr, dh_scr):
        _s5_expand(kx_ref, ec_ref, fc_ref, kb_scr, e_scr, f_scr)
        dys = _s5_token_rows(dy_ref, n)
        dh_scr[...] = _dot(jnp.concatenate(dys, axis=1), e_scr[...], NT)
        ar, ai = at_ref[:, :S5_SH], at_ref[:, S5_SH:]

        def step(it, carry):
            cy, dat = carry
            base = pl.multiple_of((n // SUBLANES - 1 - it) * SUBLANES, SUBLANES)
            dh8 = dh_scr[pl.ds(base, SUBLANES), :]
            h8 = h_ref[pl.ds(base, SUBLANES), :]
            rows = [None] * SUBLANES
            for r in reversed(range(SUBLANES)):
                rows[r] = cy
                cr, ci = cy[:, :S5_SH], cy[:, S5_SH:]
                hr, hi = h8[r:r + 1, :S5_SH], h8[r:r + 1, S5_SH:]
                dat = dat + jnp.concatenate([cr * hr + ci * hi, ci * hr - cr * hi], axis=1)
                cy = dh8[r:r + 1, :] + jnp.concatenate([ar * cr + ai * ci, ar * ci - ai * cr], axis=1)
            dg_ref[pl.ds(base, SUBLANES), :] = jnp.concatenate(rows, axis=0)
            return cy, dat

        zero = jnp.zeros((1, S5_S), F32)
        _, dat = lax.fori_loop(0, n // SUBLANES, step, (zero, zero))
        dat_ref[...] = dat
        dgb = dg_ref[...].astype(BF16)
        for s in range(S5_T):
            acc = _dot(dgb, f_scr[s * LANES:(s + 1) * LANES, :], NT)
            for t in range(s, S5_T):
                acc = acc + _dot(dys[t], kb_scr[t - s], NT)
            du_ref[pl.ds(s, n, stride=S5_T), :] = acc

    row_spec = pl.BlockSpec((length, LANES), lambda k: (0, k))
    return pl.pallas_call(
        body, name=name,
        out_shape=(jax.ShapeDtypeStruct((length, D_MODEL), F32), jax.ShapeDtypeStruct((S5_TILES, n, S5_S), F32),
                   jax.ShapeDtypeStruct((S5_TILES, 1, S5_S), F32)),
        grid=(S5_TILES,),
        in_specs=[row_spec, _s5_spec(LANES, S5_T * S5_GROUP), _s5_spec(S5_S, S5_T * S5_GROUP),
                  _s5_spec(S5_T, LANES, LANES), _s5_spec(1, S5_S), _s5_spec(n, S5_S)],
        out_specs=(row_spec, _s5_spec(n, S5_S), _s5_spec(1, S5_S)),
        scratch_shapes=[pltpu.VMEM((S5_T, LANES, LANES), BF16), pltpu.VMEM((S5_S, S5_W), BF16),
                        pltpu.VMEM((S5_W, S5_S), BF16), pltpu.VMEM((n, S5_S), F32)],
        compiler_params=_params(("parallel",)),
    )(dy, kx, ec, fc, at, hs)


def _s5_operator_grads(dy, u, hs, dg, *, name):
    length = u.shape[0]
    n = length // S5_T

    def body(dy_ref, u_ref, h_ref, dg_ref, dkx_ref, dec_ref, dfc_ref):
        dys = _s5_token_rows(dy_ref, n)
        us = _s5_token_rows(u_ref, n)
        ucat = jnp.concatenate(us, axis=1)
        m_ab, m_e, m_f = _s5_masks()
        hb = h_ref[...].astype(BF16)
        dgb = dg_ref[...].astype(BF16)
        lane = lax.broadcasted_iota(jnp.int32, (1, LANES), 1)
        lane_group = jnp.right_shift(lane, 4)

        def own_block(x, mask):
            x = jnp.where(mask, x, 0.0)
            for shift in (64, 32, 16):
                x = x + pltpu.roll(x, shift, 1)
            return x

        def place(halves, t, x):
            halves[t // 8] = jnp.where(lane_group == t % 8, x, halves[t // 8])

        dkb = [jnp.zeros((LANES, LANES), F32) for _ in range(S5_T)]
        dec = [jnp.zeros((S5_S, LANES), F32) for _ in range(2)]
        for t in range(S5_T):
            d_t = _dot(ucat, dys[t], TN)
            for s in range(t + 1):
                dkb[t - s] = dkb[t - s] + d_t[s * LANES:(s + 1) * LANES, :]
            place(dec, t, own_block(_dot(hb, dys[t], TN), m_e))
            wide = jnp.where(m_f, _dot(us[t], dgb, TN), 0.0)
            parts = []
            for r in range(2):
                acc = wide[:, r * S5_SH:r * S5_SH + LANES]
                for q in range(1, S5_SH // LANES):
                    acc = acc + wide[:, r * S5_SH + q * LANES:r * S5_SH + (q + 1) * LANES]
                parts.append(acc + pltpu.roll(acc, S5_STATE, 1))
            dfc_ref[t] = jnp.where(lane < S5_STATE, parts[0], parts[1])
        dkx = [jnp.zeros((LANES, LANES), F32) for _ in range(2)]
        for t in range(S5_T):
            place(dkx, t, own_block(dkb[t], m_ab))
        dkx_ref[...] = jnp.concatenate(dkx, axis=1)
        dec_ref[...] = jnp.concatenate(dec, axis=1)

    row_spec = pl.BlockSpec((length, LANES), lambda k: (0, k))
    outs = (_s5_spec(LANES, S5_T * S5_GROUP), _s5_spec(S5_S, S5_T * S5_GROUP), _s5_spec(S5_T, LANES, LANES))
    return pl.pallas_call(
        body, name=name,
        out_shape=(jax.ShapeDtypeStruct((S5_TILES, LANES, S5_T * S5_GROUP), F32),
                   jax.ShapeDtypeStruct((S5_TILES, S5_S, S5_T * S5_GROUP), F32),
                   jax.ShapeDtypeStruct((S5_TILES, S5_T, LANES, LANES), F32)),
        grid=(S5_TILES,),
        in_specs=[row_spec, row_spec, _s5_spec(n, S5_S), _s5_spec(n, S5_S)],
        out_specs=outs,
        compiler_params=_params(("parallel",)),
    )(dy, u, hs, dg)


def _s5_prep(a_re, a_im, b_re, b_im, c_re, c_im, log_dt):
    t_len, tiles = S5_T, S5_TILES
    dt = jnp.exp(log_dt)[:, None]
    mag = jnp.exp(a_re * dt)
    ab_re, ab_im = mag * jnp.cos(a_im * dt), mag * jnp.sin(a_im * dt)
    den = jnp.square(a_re) + jnp.square(a_im)
    n_re, n_im = ab_re - 1.0, ab_im
    f_re = (n_re * a_re + n_im * a_im) / den
    f_im = (n_im * a_re - n_re * a_im) / den
    bb_re = f_re[..., None] * b_re - f_im[..., None] * b_im
    bb_im = f_re[..., None] * b_im + f_im[..., None] * b_re
    def powers(exponents):
        e = exponents[:, None, None]
        m = jnp.exp(e * (a_re * dt))
        return m * jnp.cos(e * (a_im * dt)), m * jnp.sin(e * (a_im * dt))

    p_re, p_im = powers(jnp.arange(t_len + 1, dtype=F32))
    rev_re, rev_im = powers((t_len - 1) - jnp.arange(t_len, dtype=F32))
    ca_re = c_re[None] * p_re[:, :, None, :] - c_im[None] * p_im[:, :, None, :]
    ca_im = c_re[None] * p_im[:, :, None, :] + c_im[None] * p_re[:, :, None, :]
    lag = (jnp.einsum('tgip,gpj->tgij', ca_re[:t_len], bb_re, precision=HI)
           - jnp.einsum('tgip,gpj->tgij', ca_im[:t_len], bb_im, precision=HI))
    kx = lag.reshape(t_len, tiles, 8, S5_GROUP, S5_GROUP).transpose(1, 2, 4, 0, 3)
    kx = kx.reshape(tiles, LANES, t_len * S5_GROUP)
    e_st = jnp.stack([ca_re[1:], -ca_im[1:]])
    e_st = e_st.reshape(2, t_len, tiles, 8, S5_GROUP, S5_STATE).transpose(2, 0, 3, 5, 1, 4)
    ec = e_st.reshape(tiles, S5_S, t_len * S5_GROUP)
    ab_b = jnp.stack([rev_re[..., None] * bb_re[None] - rev_im[..., None] * bb_im[None],
                      rev_re[..., None] * bb_im[None] + rev_im[..., None] * bb_re[None]])
    ab_b = ab_b.reshape(2, t_len, tiles, 8, S5_STATE, S5_GROUP).transpose(2, 1, 3, 5, 0, 4)
    fc = ab_b.reshape(tiles, t_len, LANES, 2 * S5_STATE)
    a_t = jnp.stack([p_re[t_len], p_im[t_len]]).reshape(2, tiles, 8 * S5_STATE).transpose(1, 0, 2)
    return kx, ec, fc, a_t.reshape(tiles, 1, S5_S)


TM_ROW = 256


def _gdn_fwd(x, w, tag):
    qkv = _mm(x, w["wqkv"], name="gdn_proj_qkv")
    z = _mm(x, w["wz"], name="gdn_proj_z")
    ba = _mm(x, w["wba"], name="gdn_proj_ba")
    cv = _conv_fwd(qkv, w["conv_w"], tm=TM_ROW, name="gdn_conv")
    q, k, v = _rw_fwd(_f_gdn_qkv, [cv], [], tm=TM_ROW, name="gdn_qkv")
    o, states = _gdn_scan_fwd(q, k, v, ba, w["a_log8"], w["dt_bias8"], name="gdn_scan")
    (mix,) = _rw_fwd(_f_gdn_out, [o, z], [w["norm_g"]], tm=TM_ROW, name="gdn_out")
    return mix, (qkv, z, ba, cv, q, k, v, states, o)


def _gdn_bwd(x, w, saved, dmix, dx_acc):
    qkv, z, ba, cv, q, k, v, states, o = saved
    (do, dz), (dnorm_g,) = _rw_bwd(_f_gdn_out, [o, z], [w["norm_g"]], [dmix], row_grad=[1, 1], param_grad=[1],
                                   tm=TM_ROW, name="gdn_out_bwd")
    dq, dk, dv, dba, dalog, ddtb = _gdn_scan_bwd(q, k, v, ba, w["a_log8"], w["dt_bias8"], states, do,
                                                  name="gdn_scan_bwd")
    (dcv,), _ = _rw_bwd(_f_gdn_qkv, [cv], [], [dq, dk, dv], row_grad=[1], param_grad=[], tm=TM_ROW,
                        name="gdn_qkv_bwd")
    dqkv, dconv_w = _conv_bwd(qkv, w["conv_w"], dcv, tm=TM_ROW, name="gdn_conv_bwd")
    dx = _mm(dqkv, w["wqkv"], tb=True, acc=dx_acc, name="gdn_dx_qkv")
    dx = _mm(dz, w["wz"], tb=True, acc=dx, name="gdn_dx_z")
    dx = _mm(dba, w["wba"], tb=True, acc=dx, name="gdn_dx_ba")
    grads = dict(wqkv=_mm(x, dqkv, ta=True, name="gdn_dw_qkv"), wz=_mm(x, dz, ta=True, name="gdn_dw_z"),
                 wba=_mm(x, dba, ta=True, name="gdn_dw_ba"), conv_w=dconv_w,
                 a_log=dalog[:, 0], dt_bias=ddtb[:, 0], norm_g=dnorm_g[0])
    return dx, grads


def _s5_fwd(x, w, tag):
    u = _mm(x, w["wu"], name="s5_proj_u")
    y, hs = _s5_scan_fwd(u, w["kx"], w["ec"], w["fc"], w["a_t"], name="s5_scan")
    (zg,) = _rw_fwd(_f_s5_gelu, [y, u], [w["d"]], tm=TM_ROW, name="s5_gelu")
    t = _mm(zg, w["w_glu"], name="s5_glu")
    (mix,) = _rw_fwd(_f_s5_gate, [zg, t], [w["b_glu"]], tm=TM_ROW, name="s5_gate")
    return mix, (u, hs, y, zg, t)


def _s5_bwd(x, w, saved, dmix, dx_acc):
    u, hs, y, zg, t = saved
    (dzg, dt), (db_glu,) = _rw_bwd(_f_s5_gate, [zg, t], [w["b_glu"]], [dmix], row_grad=[1, 1], param_grad=[1],
                                   tm=TM_ROW, name="s5_gate_bwd")
    dzg = _mm(dt, w["w_glu"], tb=True, acc=dzg, name="s5_dzg")
    dw_glu = _mm(zg, dt, ta=True, name="s5_dw_glu")
    (dy, du), (dd,) = _rw_bwd(_f_s5_gelu, [y, u], [w["d"]], [dzg], row_grad=[1, 1], param_grad=[1],
                              tm=TM_ROW, name="s5_gelu_bwd")
    du_scan, dg, dat = _s5_scan_bwd(dy, w["kx"], w["ec"], w["fc"], w["a_t"], hs, name="s5_scan_bwd")
    dkx, dec, dfc = _s5_operator_grads(dy, u, hs, dg, name="s5_operator_grads")
    (du,) = _rw_fwd(_f_add, [du, du_scan], [], tm=TM_ROW, name="s5_du_add")
    d_a_re, d_a_im, d_b_re, d_b_im, d_c_re, d_c_im, d_log_dt = w["prep_vjp"]((dkx, dec, dfc, dat))
    dx = _mm(du, w["wu"], tb=True, acc=dx_acc, name="s5_dx_u")
    grads = dict(wu=_mm(x, du, ta=True, name="s5_dw_u"), w_glu=dw_glu, b_glu=db_glu[0], d=dd[0],
                 a_re=d_a_re, a_im=d_a_im, b_re=d_b_re, b_im=d_b_im, c_re=d_c_re, c_im=d_c_im, log_dt=d_log_dt)
    return dx, grads


def _layer_fwd(x, mem, w, is_gdn):
    mix, msave = (_gdn_fwd if is_gdn else _s5_fwd)(x, w, "")
    xq = _mm(x, w["wxq"], name="proj_xq")
    kv = _mm(mem, w["wkv"], name="mem_kv")
    kmem, vmem = kv[:, :XA_DIM], kv[:, XA_DIM:]
    (cross,) = _rw_fwd(_f_attn, [xq], [kmem, vmem], tm=TM_ROW, name="attn")
    h = _mm(mix, w["wo_mix"], name="wo_mix")
    h = _mm(cross, w["wo_cross"], acc=h, name="wo_cross")
    (x1,) = _rw_fwd(_f_ln_res, [x, h], [w["ln1_g"], w["ln1_b"]], tm=TM_ROW, name="ln_res")
    hm, act = _mm_relu2(x1, w["w1"], name="mlp_up")
    f = _mm(act, w["w2"], name="mlp_down")
    (x2,) = _rw_fwd(_f_ln_res, [x1, f], [w["ln2_g"], w["ln2_b"]], tm=TM_ROW, name="ln_res")
    return x2, (x, msave, xq, kmem, vmem, mix, cross, h, x1, hm, act, f)


def _layer_bwd(mem, w, is_gdn, saved, dx2, token=None):
    x, msave, xq, kmem, vmem, mix, cross, h, x1, hm, act, f = saved
    ln2_g = w["ln2_g"] if token is None else w["ln2_g"] + token[0, 0]
    (dx1, df), (dg2, db2) = _rw_bwd(_f_ln_res, [x1, f], [ln2_g, w["ln2_b"]], [dx2], row_grad=[1, 1],
                                    param_grad=[1, 1], tm=TM_ROW, name="ln_res_bwd")
    dhm = _mm_relu2_grad(df, w["w2"], hm, name="mlp_dhm")
    dw2 = _mm(act, df, ta=True, name="mlp_dw2")
    dx1 = _mm(dhm, w["w1"], tb=True, acc=dx1, name="mlp_dx")
    dw1 = _mm(x1, dhm, ta=True, out_blocks=N_CHIPS, name="mlp_dw1")
    (dx, dh), (dg1, db1) = _rw_bwd(_f_ln_res, [x, h], [w["ln1_g"], w["ln1_b"]], [dx1], row_grad=[1, 1],
                                   param_grad=[1, 1], tm=TM_ROW, name="ln_res_bwd")
    dmix = _mm(dh, w["wo_mix"], tb=True, name="wo_dmix")
    dcross = _mm(dh, w["wo_cross"], tb=True, name="wo_dcross")
    dwo = jnp.concatenate([_mm(mix, dh, ta=True, name="wo_dw_mix"), _mm(cross, dh, ta=True, name="wo_dw_cross")], 0)
    (dxq,), (dkmem, dvmem) = _rw_bwd(_f_attn, [xq], [kmem, vmem], [dcross], row_grad=[1], param_grad=[1, 1],
                                     tm=TM_ROW, name="attn_bwd")
    dwkv = _mm(mem, jnp.concatenate([dkmem, dvmem], axis=1), ta=True, name="mem_dw_kv")
    dx = _mm(dxq, w["wxq"], tb=True, acc=dx, name="dx_xq")
    dwxq = _mm(x, dxq, ta=True, name="dw_xq")
    dx, mg = (_gdn_bwd if is_gdn else _s5_bwd)(x, w, msave, dmix, dx)
    grads = dict(mixer=mg, wxq=dwxq, wkv=dwkv, wo=dwo, w1=dw1, w2=dw2,
                 ln1_g=dg1[0], ln1_b=db1[0], ln2_g=dg2[0], ln2_b=db2[0])
    return dx, grads


def _loss_and_grad(y, target):
    def f(yv, tv):
        err = yv - tv
        return (err * (1.0 / D_MODEL),), (0.5 / D_MODEL * jnp.sum(err * err, axis=0, keepdims=True),)

    (dy,), (part,) = _rowwise(f, [y, target], [], [(D_MODEL, F32)], [((1, D_MODEL), F32)], tm=512, name="loss")
    return jnp.sum(part), dy


def _layer_weights(full, i):
    j = i // 2
    w = dict(wkv=full["w_kv_mem"][i].astype(BF16),
             wo_mix=full["w_o"][i][:D_MODEL].astype(BF16), wo_cross=full["w_o"][i][D_MODEL:].astype(BF16),
             ln1_g=full["ln1_g"][i][None], ln1_b=full["ln1_b"][i][None],
             ln2_g=full["ln2_g"][i][None], ln2_b=full["ln2_b"][i][None],
             w1=full["mlp_w1"][i].astype(BF16), w2=full["mlp_w2"][i].astype(BF16))
    if i % 2 == 0:
        w_in = full["gdn_w_in"][j]
        gd = 3 * D_MODEL
        w.update(wqkv=w_in[:, :gd].astype(BF16), wz=w_in[:, gd:gd + D_MODEL].astype(BF16),
                 wba=jnp.pad(w_in[:, gd + D_MODEL:gd + D_MODEL + 2 * GDN_HEADS],
                             ((0, 0), (0, LANES - 2 * GDN_HEADS))).astype(BF16),
                 wxq=w_in[:, gd + D_MODEL + 2 * GDN_HEADS:].astype(BF16),
                 conv_w=full["gdn_conv_w"][j],
                 a_log8=jnp.broadcast_to(full["gdn_a_log"][j][:, None], (GDN_HEADS, LANES)),
                 dt_bias8=jnp.broadcast_to(full["gdn_dt_bias"][j][:, None], (GDN_HEADS, LANES)),
                 norm_g=full["gdn_norm_g"][j][None])
    else:
        w_in = full["s5_w_in"][j]
        (kx, ec, fc, a_t), prep_vjp = jax.vjp(
            _s5_prep, full["s5_a_re"][j], full["s5_a_im"][j], full["s5_b_re"][j], full["s5_b_im"][j],
            full["s5_c_re"][j], full["s5_c_im"][j], full["s5_log_dt"][j])
        w.update(wu=w_in[:, :D_MODEL].astype(BF16), wxq=w_in[:, D_MODEL:].astype(BF16),
                 kx=kx, ec=ec, fc=fc, a_t=a_t, prep_vjp=prep_vjp,
                 d=full["s5_d"][j][None], w_glu=full["s5_w_glu"][j].astype(BF16), b_glu=full["s5_b_glu"][j][None])
    return w


def _sharded_grads(l, i):
    m = l["mixer"]
    out = dict(w_kv_mem=l["wkv"], w_o=l["wo"], mlp_w1=l["w1"], mlp_w2=l["w2"])
    if i % 2 == 0:
        out.update(gdn_w_in=jnp.concatenate([m["wqkv"], m["wz"], m["wba"][:, :2 * GDN_HEADS], l["wxq"]], axis=1),
                   gdn_conv_w=m["conv_w"])
    else:
        out.update(s5_w_in=jnp.concatenate([m["wu"], l["wxq"]], axis=1), s5_d=m["d"], s5_w_glu=m["w_glu"],
                   s5_b_glu=m["b_glu"])
    return out


def _replicated_grads(layer_grads):
    g = layer_grads
    gdn = [g[i]["mixer"] for i in range(DEPTH) if i % 2 == 0]
    s5 = [g[i]["mixer"] for i in range(DEPTH) if i % 2 == 1]
    out = {n: jnp.stack([l[n] for l in g]) for n in ("ln1_g", "ln1_b", "ln2_g", "ln2_b")}
    out.update({"gdn_" + n: jnp.stack([m[n] for m in gdn]) for n in ("a_log", "dt_bias", "norm_g")})
    out.update({"s5_" + n: jnp.stack([m[n] for m in s5])
                for n in ("a_re", "a_im", "b_re", "b_im", "c_re", "c_im", "log_dt")})
    return out


def _local_step(x, mem, target, weights_of, grads_ready):
    lw, saves = [], []
    h = x
    for i in range(DEPTH):
        lw.append(weights_of(i, h))
        h, s = _layer_fwd(h, mem, lw[i], i % 2 == 0)
        saves.append(s)
    loss, d = _loss_and_grad(h, target)
    grads = [None] * DEPTH
    token = None
    for i in reversed(range(DEPTH)):
        d, grads[i] = _layer_bwd(mem, lw[i], i % 2 == 0, saves[i], d, token)
        token = grads_ready(i, grads[i])
    return loss, d, grads


ANY = pl.BlockSpec(memory_space=pl.ANY)
SHARD_ROWS = 1024
SMALL_ROWS = 128


def _place():
    return lax.axis_index("x"), lax.axis_index("y"), lax.axis_index("c")


def _other_chips(x, y):
    return [(1 - x, y), (x, 1 - y), (1 - x, 1 - y)]


def _all_gather_chips(wpack, *, name):
    rows = wpack.shape[0]
    half = rows // 2

    def body(w_ref, out_ref, send_sems, recv_sems):
        x, y, c = _place()
        sibling = (x, y, 1 - c)
        chips = _other_chips(x, y)

        def blk(cx, cy, cc):
            return out_ref.at[2 * cx + cy, pl.ds(cc * half, half), :]

        def copy(k, src, dst, to):
            return pltpu.make_async_remote_copy(src_ref=src, dst_ref=dst, send_sem=send_sems.at[k],
                                                recv_sem=recv_sems.at[k], device_id=to, device_id_type=MESH)

        first = [copy(j, w_ref.at[pl.ds(c * half, half), :], blk(x, y, c), (cx, cy, c))
                 for j, (cx, cy) in enumerate(chips)]
        for cp in first:
            cp.start()
        passed = [copy(3 + j, blk(cx, cy, c), blk(cx, cy, c), sibling) for j, (cx, cy) in enumerate(chips)]
        for j, (cx, cy) in enumerate(chips):
            copy(j, blk(cx, cy, c), blk(cx, cy, c), (cx, cy, c)).wait_recv()
            passed[j].start()
        for j, (cx, cy) in enumerate(chips):
            copy(3 + j, blk(cx, cy, 1 - c), blk(cx, cy, 1 - c), sibling).wait_recv()
        for cp in first + passed:
            cp.wait_send()

    return pl.pallas_call(
        body, name=name, out_shape=jax.ShapeDtypeStruct((N_CHIPS, rows, D_MODEL), wpack.dtype),
        in_specs=[ANY], out_specs=ANY,
        scratch_shapes=[pltpu.SemaphoreType.DMA((6,)), pltpu.SemaphoreType.DMA((6,))],
    )(wpack)


HBM = pl.BlockSpec(memory_space=pltpu.HBM)
SEM = pl.BlockSpec(memory_space=pltpu.SEMAPHORE)
DATAFLOW = pltpu.SideEffectType.DATAFLOW_SIDE_EFFECTING


def _gather_ici_copies(w_ref, land_ref, send_sems, recv_sems, outgoing):
    x, y, c = _place()
    half = w_ref.shape[0] // 2
    mine = pl.ds(c * half, half)
    return [pltpu.make_async_remote_copy(
        src_ref=w_ref.at[mine, :], dst_ref=land_ref.at[2 * x + y if outgoing else 2 * cx + cy, mine, :],
        send_sem=send_sems.at[j], recv_sem=recv_sems.at[j], device_id=(cx, cy, c), device_id_type=MESH)
        for j, (cx, cy) in enumerate(_other_chips(x, y))]


def _gather_start(wpack, after):
    rows = wpack.shape[0]

    def body(w_ref, land_ref, after_ref, send_sems, recv_sems, w_thru, land_thru, token):
        for cp in _gather_ici_copies(w_ref, land_ref, send_sems, recv_sems, outgoing=True):
            cp.start()
        token[...] = jnp.zeros_like(token)

    land = pltpu.with_memory_space_constraint(lax.empty((N_CHIPS, rows, D_MODEL), wpack.dtype), pltpu.HBM)
    return pl.pallas_call(
        body, name="gather_start",
        out_shape=(pltpu.SemaphoreType.DMA((3,)), pltpu.SemaphoreType.DMA((3,)), pltpu.HBM(wpack.shape, wpack.dtype),
                   pltpu.HBM(land.shape, land.dtype), jax.ShapeDtypeStruct((SUBLANES, LANES), F32)),
        in_specs=(HBM, HBM, ANY), out_specs=(SEM, SEM, HBM, HBM, pl.BlockSpec(memory_space=pltpu.VMEM)),
        input_output_aliases={0: 2, 1: 3},
        compiler_params=pltpu.CompilerParams(has_side_effects=DATAFLOW),
    )(pltpu.with_memory_space_constraint(wpack, pltpu.HBM), land, after)


def _gather_wait(send_sems, recv_sems, w_thru, land_thru, after):
    def body(w_ref, land_ref, send_sems, recv_sems, after_ref, w_dead, land_out):
        for cp in _gather_ici_copies(w_ref, land_ref, send_sems, recv_sems, outgoing=False):
            cp.wait_send()
            cp.wait_recv()

    return pl.pallas_call(
        body, name="gather_wait",
        out_shape=(pltpu.HBM(w_thru.shape, w_thru.dtype), pltpu.HBM(land_thru.shape, land_thru.dtype)),
        in_specs=(HBM, HBM, SEM, SEM, ANY), out_specs=(HBM, HBM), input_output_aliases={0: 0, 1: 1},
        compiler_params=pltpu.CompilerParams(has_side_effects=DATAFLOW),
    )(w_thru, land_thru, send_sems, recv_sems, after)[1]


def _gather_forward(land, *, name):
    rows = land.shape[1]
    half = rows // 2

    def body(in_ref, out_ref, send_sems, recv_sems):
        x, y, c = _place()

        def copy(j, cx, cy, cc):
            rows_of = out_ref.at[2 * cx + cy, pl.ds(cc * half, half), :]
            return pltpu.make_async_remote_copy(src_ref=rows_of, dst_ref=rows_of, send_sem=send_sems.at[j],
                                                recv_sem=recv_sems.at[j], device_id=(x, y, 1 - c), device_id_type=MESH)

        sends = [copy(j, cx, cy, c) for j, (cx, cy) in enumerate(_other_chips(x, y))]
        for cp in sends:
            cp.start()
        for j, (cx, cy) in enumerate(_other_chips(x, y)):
            copy(j, cx, cy, 1 - c).wait_recv()
        for cp in sends:
            cp.wait_send()

    return pl.pallas_call(
        body, name=name, out_shape=jax.ShapeDtypeStruct(land.shape, land.dtype), in_specs=[ANY], out_specs=ANY,
        input_output_aliases={0: 0},
        scratch_shapes=[pltpu.SemaphoreType.DMA((3,)), pltpu.SemaphoreType.DMA((3,))],
    )(land)


def _sibling_swap(buf, *, name):
    def body(in_ref, out_ref, send_sem, recv_sem):
        x, y, c = _place()
        cp = pltpu.make_async_remote_copy(src_ref=in_ref, dst_ref=out_ref, send_sem=send_sem, recv_sem=recv_sem,
                                          device_id=(x, y, 1 - c), device_id_type=MESH)
        cp.start()
        cp.wait()

    return pl.pallas_call(
        body, name=name, out_shape=jax.ShapeDtypeStruct(buf.shape, buf.dtype), in_specs=[ANY], out_specs=ANY,
        scratch_shapes=[pltpu.SemaphoreType.DMA, pltpu.SemaphoreType.DMA],
    )(buf)


def _pair_exchange(gpack, *, name):
    pieces, rows, width = gpack.shape
    half = rows // 2

    def body(in_ref, got_ref, send_sems, recv_sems):
        x, y, c = _place()
        sends = [pltpu.make_async_remote_copy(src_ref=in_ref.at[p, pl.ds((1 - c) * half, half), :],
                                              dst_ref=got_ref.at[p], send_sem=send_sems.at[p],
                                              recv_sem=recv_sems.at[p], device_id=(x, y, 1 - c), device_id_type=MESH)
                 for p in range(pieces)]
        for cp in sends:
            cp.start()
        for cp in sends:
            cp.wait()

    return pl.pallas_call(
        body, name=name, out_shape=jax.ShapeDtypeStruct((pieces, half, width), gpack.dtype),
        in_specs=[ANY], out_specs=ANY,
        scratch_shapes=[pltpu.SemaphoreType.DMA((pieces,)), pltpu.SemaphoreType.DMA((pieces,))],
    )(gpack)


def _pair_add(gpack, got, c, *, name, tm=512):
    pieces, rows, width = gpack.shape
    half = rows // 2
    nb = half // tm

    def body(c_ref, a_ref, b_ref, sum_ref, narrow_ref):
        s = a_ref[...] + b_ref[...]
        sum_ref[...] = s
        narrow_ref[...] = s.astype(BF16)

    blk = pl.BlockSpec((None, tm, width), lambda p, i, c_ref: (p, i, 0))
    return pl.pallas_call(
        body, name=name,
        out_shape=(jax.ShapeDtypeStruct((pieces, half, width), F32), jax.ShapeDtypeStruct((pieces, half, width), BF16)),
        grid_spec=pltpu.PrefetchScalarGridSpec(
            num_scalar_prefetch=1, grid=(pieces, nb),
            in_specs=[pl.BlockSpec((None, tm, width), lambda p, i, c_ref: (p, c_ref[0] * nb + i, 0)), blk],
            out_specs=(blk, blk)),
        compiler_params=_params(("parallel", "parallel")),
    )(c, gpack, got)


def _chip_exchange(pieces, *, name):
    _, rows, width = pieces.shape

    def body(in_ref, out_ref, send_sems, recv_sems):
        x, y, c = _place()
        cps = [pltpu.make_async_remote_copy(src_ref=in_ref.at[2 * cx + cy], dst_ref=out_ref.at[j],
                                            send_sem=send_sems.at[j], recv_sem=recv_sems.at[j],
                                            device_id=(cx, cy, c), device_id_type=MESH)
               for j, (cx, cy) in enumerate(_other_chips(x, y))]
        for cp in cps:
            cp.start()
        for cp in cps:
            cp.wait()

    return pl.pallas_call(
        body, name=name, out_shape=jax.ShapeDtypeStruct((3, rows, width), pieces.dtype), in_specs=[ANY], out_specs=ANY,
        scratch_shapes=[pltpu.SemaphoreType.DMA((3,)), pltpu.SemaphoreType.DMA((3,))],
    )(pieces)


def _chip_exchange_copies(in_ref, land_ref, send_sems, recv_sems):
    x, y, c = _place()
    return [pltpu.make_async_remote_copy(src_ref=in_ref.at[2 * cx + cy], dst_ref=land_ref.at[j],
                                         send_sem=send_sems.at[j], recv_sem=recv_sems.at[j],
                                         device_id=(cx, cy, c), device_id_type=MESH)
            for j, (cx, cy) in enumerate(_other_chips(x, y))]


def _chip_exchange_start(pieces):
    _, rows, width = pieces.shape

    def body(in_ref, land_ref, send_sems, recv_sems, in_thru, land_thru, token):
        for cp in _chip_exchange_copies(in_ref, land_ref, send_sems, recv_sems):
            cp.start()
        token[...] = jnp.zeros_like(token)

    land = pltpu.with_memory_space_constraint(lax.empty((3, rows, width), pieces.dtype), pltpu.HBM)
    return pl.pallas_call(
        body, name="rs_chip_start",
        out_shape=(pltpu.SemaphoreType.DMA((3,)), pltpu.SemaphoreType.DMA((3,)), pltpu.HBM(pieces.shape, pieces.dtype),
                   pltpu.HBM(land.shape, land.dtype), jax.ShapeDtypeStruct((SUBLANES, LANES), F32)),
        in_specs=(HBM, HBM), out_specs=(SEM, SEM, HBM, HBM, pl.BlockSpec(memory_space=pltpu.VMEM)),
        input_output_aliases={0: 2, 1: 3},
        compiler_params=pltpu.CompilerParams(has_side_effects=DATAFLOW),
    )(pltpu.with_memory_space_constraint(pieces, pltpu.HBM), land)


def _chip_exchange_wait(send_sems, recv_sems, in_thru, land_thru, after):
    def body(in_ref, land_ref, send_sems, recv_sems, after_ref, in_dead, land_out):
        for cp in _chip_exchange_copies(in_ref, land_ref, send_sems, recv_sems):
            cp.wait_send()
            cp.wait_recv()

    return pl.pallas_call(
        body, name="rs_chip_wait",
        out_shape=(pltpu.HBM(in_thru.shape, in_thru.dtype), pltpu.HBM(land_thru.shape, land_thru.dtype)),
        in_specs=(HBM, HBM, SEM, SEM, ANY), out_specs=(HBM, HBM), input_output_aliases={0: 0, 1: 1},
        compiler_params=pltpu.CompilerParams(has_side_effects=DATAFLOW),
    )(in_thru, land_thru, send_sems, recv_sems, after)[1]


def _all_reduce_small(v, *, name):
    rows, width = v.shape

    def body(in_ref, out_ref, gath, send_sems, recv_sems):
        x, y, c = _place()
        me = 4 * x + 2 * y + c
        gath[me] = in_ref[...]
        peers = []
        for m in range(1, N_DEV):
            px = 1 - x if m & 4 else x
            py = 1 - y if m & 2 else y
            pc = 1 - c if m & 1 else c
            peers.append((m - 1, (px, py, pc), 4 * px + 2 * py + pc))
        for k, peer, _ in peers:
            pltpu.make_async_remote_copy(src_ref=in_ref, dst_ref=gath.at[me], send_sem=send_sems.at[k],
                                         recv_sem=recv_sems.at[k], device_id=peer, device_id_type=MESH).start()
        for k, peer, plin in peers:
            cp = pltpu.make_async_remote_copy(src_ref=in_ref, dst_ref=gath.at[plin], send_sem=send_sems.at[k],
                                              recv_sem=recv_sems.at[k], device_id=peer, device_id_type=MESH)
            cp.wait_send()
            cp.wait_recv()
        acc = gath[0]
        for d in range(1, N_DEV):
            acc = acc + gath[d]
        out_ref[...] = acc

    vmem = pl.BlockSpec(memory_space=pltpu.VMEM)
    return pl.pallas_call(
        body, name=name, out_shape=jax.ShapeDtypeStruct(v.shape, v.dtype), in_specs=[vmem], out_specs=vmem,
        scratch_shapes=[pltpu.VMEM((N_DEV, rows, width), v.dtype),
                        pltpu.SemaphoreType.DMA((N_DEV - 1,)), pltpu.SemaphoreType.DMA((N_DEV - 1,))],
        compiler_params=pltpu.CompilerParams(vmem_limit_bytes=VMEM_LIMIT_V7X),
    )(v)


def _reduce_scatter_begin(gpack, behind):
    x, y, c = _place()
    got = _pair_exchange(gpack, name="rs_pair_swap")
    pair, pair16 = _pair_add(gpack, got, c.astype(jnp.int32).reshape(1), name="rs_pair_add")
    mine = lax.dynamic_index_in_dim(pair, 2 * x + y, axis=0, keepdims=False)
    if behind:
        *in_flight, token = _chip_exchange_start(pair16)
        return dict(mine=mine, in_flight=in_flight), token
    return dict(mine=mine, recv=_chip_exchange(pair16, name="rs_chip_exchange")), None


def _reduce_scatter_end(state, after=None):
    c = lax.axis_index("c")
    recv = state["recv"] if "recv" in state else _chip_exchange_wait(*state["in_flight"], after=after)
    (total,) = _rw_fwd(_f_add4, [state["mine"], recv[0], recv[1], recv[2]], [], tm=512, name="rs_chip_add")
    theirs = _sibling_swap(total, name="rs_share_swap")
    return jnp.concatenate([jnp.where(c == 0, total, theirs), jnp.where(c == 0, theirs, total)], axis=0)


_SHARDED = (("w_kv_mem", 1), ("w_o", 1), ("mlp_w1", 2), ("mlp_w2", 1), ("gdn_w_in", 2), ("gdn_conv_w", 2),
            ("s5_w_in", 2), ("s5_d", 1), ("s5_w_glu", 1), ("s5_b_glu", 1))
_MATMUL_ONLY = ("w_kv_mem", "w_o", "mlp_w1", "mlp_w2", "gdn_w_in", "s5_w_in", "s5_w_glu")
REDUCED_FIRST = (2, 3)
_KEPT_BLOCKED = ("mlp_w1",)
_REPLICATED = ("ln1_g", "ln1_b", "ln2_g", "ln2_b", "gdn_a_log", "gdn_dt_bias", "gdn_norm_g", "s5_a_re", "s5_a_im",
               "s5_b_re", "s5_b_im", "s5_c_re", "s5_c_im", "s5_log_dt")
_WEIGHTS = ("w_kv_mem", "w_o", "ln1_g", "ln1_b", "ln2_g", "ln2_b", "mlp_w1", "mlp_w2", "gdn_w_in", "gdn_conv_w",
            "gdn_a_log", "gdn_dt_bias", "gdn_norm_g", "s5_w_in", "s5_a_re", "s5_a_im", "s5_b_re", "s5_b_im",
            "s5_c_re", "s5_c_im", "s5_log_dt", "s5_d", "s5_w_glu", "s5_b_glu")


ROW_ALIGN = 16


def _n_rows(shape):
    return -(-math.prod(shape) // (ROW_ALIGN * D_MODEL)) * ROW_ALIGN


def _as_rows(a):
    rows = _n_rows(a.shape)
    if a.shape[-1] == D_MODEL and a.size == rows * D_MODEL:
        return a.reshape(-1, D_MODEL)
    flat = a.reshape(-1)
    return jnp.pad(flat, (0, rows * D_MODEL - flat.size)).reshape(rows, D_MODEL)


def _pack(arrs, unit_rows=SHARD_ROWS):
    rows = [_as_rows(a) for a in arrs]
    pad = -sum(r.shape[0] for r in rows) % unit_rows
    if pad:
        rows.append(jnp.zeros((pad, D_MODEL), rows[0].dtype))
    return jnp.concatenate(rows, axis=0)


def _unpack(packed, shapes):
    lead = packed.shape[:-2]
    out, off = [], 0
    for s in shapes:
        r = _n_rows(s)
        seg = lax.slice_in_dim(packed, off, off + r, axis=len(lead))
        if s[-1] != D_MODEL or math.prod(s) != r * D_MODEL:
            seg = lax.slice_in_dim(seg.reshape(lead + (-1,)), 0, math.prod(s), axis=len(lead))
        out.append(seg.reshape(lead + tuple(s)))
        off += r
    return out


def _split3(t):
    hi = t.astype(BF16)
    r1 = t - hi.astype(F32)
    mid = r1.astype(BF16)
    lo = (r1 - mid.astype(F32)).astype(BF16)
    return jnp.stack([hi, mid, lo], axis=-1)


def _join3(t):
    return (t[..., 0].astype(F32) + t[..., 1].astype(F32)) + t[..., 2].astype(F32)


def _merge_chips(blocks, axis):
    return jnp.concatenate([blocks[s] for s in range(N_CHIPS)], axis=axis)


def _pack_for_chips(weights):
    rows = []
    for s in range(N_CHIPS):
        chip = []
        for layers, axis in weights:
            if axis is None:
                blocks = [g[s] for g in layers]
            else:
                n = layers[0].shape[axis] // N_CHIPS
                blocks = [lax.slice_in_dim(g, s * n, (s + 1) * n, axis=axis) for g in layers]
            if math.prod(blocks[0].shape) % (ROW_ALIGN * D_MODEL) == 0:
                chip += [_as_rows(b) for b in blocks]
            else:
                chip.append(_as_rows(jnp.stack(blocks)))
        pad = -sum(r.shape[0] for r in chip) % SHARD_ROWS
        rows += chip + ([jnp.zeros((pad, D_MODEL), F32)] if pad else [])
    return jnp.concatenate(rows, axis=0).reshape(N_CHIPS, -1, D_MODEL)


def kernel(x, mem, w_kv_mem, w_o, ln1_g, ln1_b, ln2_g, ln2_b, mlp_w1, mlp_w2, gdn_w_in, gdn_conv_w, gdn_a_log, gdn_dt_bias, gdn_norm_g, s5_w_in, s5_a_re, s5_a_im, s5_b_re, s5_b_im, s5_c_re, s5_c_im, s5_log_dt, s5_d, s5_w_glu, s5_b_glu, loss_target, m_w_kv_mem, m_w_o, m_ln1_g, m_ln1_b, m_ln2_g, m_ln2_b, m_mlp_w1, m_mlp_w2, m_gdn_w_in, m_gdn_conv_w, m_gdn_a_log, m_gdn_dt_bias, m_gdn_norm_g, m_s5_w_in, m_s5_a_re, m_s5_a_im, m_s5_b_re, m_s5_b_im, m_s5_c_re, m_s5_c_im, m_s5_log_dt, m_s5_d, m_s5_w_glu, m_s5_b_glu, v_w_kv_mem, v_w_o, v_ln1_g, v_ln1_b, v_ln2_g, v_ln2_b, v_mlp_w1, v_mlp_w2, v_gdn_w_in, v_gdn_conv_w, v_gdn_a_log, v_gdn_dt_bias, v_gdn_norm_g, v_s5_w_in, v_s5_a_re, v_s5_a_im, v_s5_b_re, v_s5_b_im, v_s5_c_re, v_s5_c_im, v_s5_log_dt, v_s5_d, v_s5_w_glu, v_s5_b_glu):
    given = dict(locals())
    w = {n: given[n] for n in _WEIGHTS}
    mom = {n: given["m_" + n] for n in _WEIGHTS}
    var = {n: given["v_" + n] for n in _WEIGHTS}
    shard_names = [n for n, _ in _SHARDED]
    shard_shapes = [w[n].shape for n in shard_names]
    rep_shapes = [w[n].shape for n in _REPLICATED]

    wire = {n: w[n].astype(BF16) if n in _MATMUL_ONLY else _split3(w[n]) for n in shard_names}
    first = {n: 0 if n.startswith("s5_") else 1 for n in shard_names}
    me_chip = 2 * lax.axis_index("x") + lax.axis_index("y")
    early = [wire[n][:first[n]] for n in shard_names if first[n]]
    late = [wire[n][first[n]:] for n in shard_names]
    early_pack, late_pack = _pack(early), _pack(late)
    landed = _all_gather_chips(early_pack, name="gather_first_layer")
    landed = lax.dynamic_update_index_in_dim(landed, early_pack, me_chip, axis=0)
    early_blocks = dict(zip([n for n in shard_names if first[n]], _unpack(landed, [a.shape for a in early])))
    send_sems, recv_sems, pack_thru, land_thru, token = _gather_start(late_pack, after=landed)
    axis_of = dict(_SHARDED)

    def merged(n, blk):
        if n in _KEPT_BLOCKED:
            return blk
        return _merge_chips(blk if n in _MATMUL_ONLY else _join3(blk), axis_of[n] - 1)

    late_full = {}

    def weights_of(i, h):
        if i == 0:
            full = {n: [merged(n, blk[:, 0])] for n, blk in early_blocks.items()}
            full["gdn_w_in"][0] = full["gdn_w_in"][0] + token[0, 0].astype(BF16)
        else:
            if not late_full:
                land = _gather_wait(send_sems, recv_sems, pack_thru, land_thru, after=h)
                land = _gather_forward(land, name="gather_forward")
                land = lax.dynamic_update_index_in_dim(land, late_pack, me_chip, axis=0)
                for n, blk in zip(shard_names, _unpack(land, [a.shape for a in late])):
                    late_full[n] = [None] * first[n] + [merged(n, blk[:, t]) for t in range(blk.shape[1])]
            full = dict(late_full)
        full.update({n: w[n] for n in _REPLICATED})
        return _layer_weights(full, i)

    sharded = {}
    in_flight = {}

    def group_pack(layers):
        names = [n for n in shard_names if any(n in sharded[i] for i in layers)]
        per_weight = [[sharded[i][n] for i in layers if n in sharded[i]] for n in names]
        pack = _pack_for_chips([(g, None if n in _KEPT_BLOCKED else axis_of[n] - 1) for n, g in zip(names, per_weight)])
        return pack, names, [(len(g),) + w[n].shape[1:] for n, g in zip(names, per_weight)]

    def grads_ready(i, g):
        sharded[i] = _sharded_grads(g, i)
        if i != REDUCED_FIRST[0]:
            return None
        pack, names, shapes = group_pack(REDUCED_FIRST)
        state, token = _reduce_scatter_begin(pack, behind=True)
        in_flight.update(state=state, names=names, shapes=shapes)
        return token

    loss, grad_x, layer_grads = _local_step(x[0], mem[0], loss_target[0], weights_of, grads_ready)
    loss = lax.psum(loss, ("x", "y", "c"))
    rest = [i for i in range(DEPTH) if i not in REDUCED_FIRST]
    pack, names, shapes = group_pack(rest)
    state, _ = _reduce_scatter_begin(pack, behind=False)
    pieces = {n: [] for n in shard_names}
    for n, g in zip(names, _unpack(_reduce_scatter_end(state), shapes)):
        pieces[n].append(g)
    late = _reduce_scatter_end(in_flight["state"], after=grad_x)
    for n, g in zip(in_flight["names"], _unpack(late, in_flight["shapes"])):
        pieces[n].append(g)
    g_shards = [p[0] if len(p) == 1 else jnp.concatenate(p, axis=0) for p in (pieces[n] for n in shard_names)]

    def pack_small(d):
        return _pack([d[n] for n in _REPLICATED], unit_rows=SMALL_ROWS)

    g_rep = _all_reduce_small(pack_small(_replicated_grads(layer_grads)), name="reduce_replicated")

    def adamw(wp, gp, mp, vp, name):
        return _rw_fwd(_f_adamw, [wp, gp, mp, vp], [], tm=256, name=name)

    outs = {}
    for n, g in zip(shard_names, g_shards):
        flat = (-1, w[n].shape[-1])
        res = adamw(w[n].reshape(flat), g.reshape(flat), mom[n].reshape(flat), var[n].reshape(flat), "adamw_" + n)
        outs[("grad", n)] = g
        outs.update({(kind, n): a.reshape(w[n].shape) for kind, a in zip(("delta", "new_m", "new_v"), res)})
    packed = (g_rep,) + tuple(adamw(pack_small(w), g_rep, pack_small(mom), pack_small(var), "adamw_replicated"))
    for kind, pr in zip(("grad", "delta", "new_m", "new_v"), packed):
        outs.update({(kind, n): a for n, a in zip(_REPLICATED, _unpack(pr, rep_shapes))})
    return (loss, grad_x[None]) + tuple(outs[(kind, n)] for kind in ("grad", "delta", "new_m", "new_v")
                                        for n in _WEIGHTS)
```

```python
import functools
import math

import jax
import jax.numpy as jnp
from jax import lax
from jax.experimental import pallas as pl
from jax.experimental.pallas import tpu as pltpu

F32 = jnp.float32
BF16 = jnp.bfloat16
MESH = pl.DeviceIdType.MESH

D_MODEL = 1024
DEPTH = 4
GDN_HEADS = 8
HEAD_DIM = 128
GDN_CONV = 4
GDN_CHUNK = 64
S5_GROUPS = 64
S5_GROUP = 16
S5_STATE = 64
XA_HEADS = 4
XA_DIM = 512
D_FF = 4096
DN_ALPHA = (2 * DEPTH) ** 0.25
LN_EPS = 1e-5
RMS_EPS = 1e-6
ADAM_LR, ADAM_B1, ADAM_B2, ADAM_EPS, ADAM_WD, ADAM_STEP = 0.001, 0.9, 0.999, 1e-08, 0.01, 10

VMEM_LIMIT_V7X = 56 * 1024 * 1024
LANES = 128
SUBLANES = 8
S5_T = 16
S5_TILES = D_MODEL // LANES
N_CHIPS = 4
N_DEV = 8


def _params(sem):
    return pltpu.CompilerParams(dimension_semantics=sem, vmem_limit_bytes=VMEM_LIMIT_V7X)


def _tile(n, pref):
    if n <= pref:
        return n
    t = (pref // LANES) * LANES
    while n % t:
        t -= LANES
    return t


def _row_tile(n, pref):
    if n % SUBLANES:
        return n
    t = min(pref, n) // SUBLANES * SUBLANES
    while n % t:
        t -= SUBLANES
    return t


def _col_blocked_spec(rows_tile, cols_tile, block_cols, rows_axis, cols_axis):
    r = block_cols // cols_tile

    def index(*ijk):
        c = ijk[cols_axis]
        return (c, ijk[rows_axis], 0) if r == 1 else (c // r, ijk[rows_axis], c % r)

    return pl.BlockSpec((None, rows_tile, cols_tile), index)


def _mm(a, b, *, ta=False, tb=False, acc=None, name, tm=1024, tn=1024, tk=1024, out_blocks=0):
    k_dim, m_dim = a.shape if ta else a.shape[::-1]
    b_rows, b_cols = (b.shape[0], b.shape[1]) if b.ndim == 2 else (b.shape[1], b.shape[0] * b.shape[2])
    n_dim = b_rows if tb else b_cols
    assert (b_cols if tb else b_rows) == k_dim, (a.shape, b.shape, ta, tb)
    limit_n = n_dim // out_blocks if out_blocks else (n_dim if b.ndim == 2 or tb else b.shape[2])
    limit_k = b.shape[2] if (b.ndim == 3 and tb) else k_dim
    tm, tn, tk = _tile(m_dim, tm), _tile(limit_n, min(tn, limit_n)), _tile(limit_k, min(tk, limit_k))
    a_spec = (pl.BlockSpec((tk, tm), lambda i, j, k: (k, i)) if ta else pl.BlockSpec((tm, tk), lambda i, j, k: (i, k)))
    if b.ndim == 3:
        b_spec = (_col_blocked_spec(tn, tk, b.shape[2], 1, 2) if tb else _col_blocked_spec(tk, tn, b.shape[2], 2, 1))
    else:
        b_spec = (pl.BlockSpec((tn, tk), lambda i, j, k: (j, k)) if tb
                  else pl.BlockSpec((tk, tn), lambda i, j, k: (k, j)))
    o_spec = (_col_blocked_spec(tm, tn, n_dim // out_blocks, 0, 1) if out_blocks
              else pl.BlockSpec((tm, tn), lambda i, j, k: (i, j)))
    o_shape = (out_blocks, m_dim, n_dim // out_blocks) if out_blocks else (m_dim, n_dim)
    dn = (((0 if ta else 1,), (1 if tb else 0,)), ((), ()))
    has_acc = acc is not None

    def body(*refs):
        a_ref, b_ref = refs[0], refs[1]
        o_ref = refs[-1]
        k = pl.program_id(2)
        p = lax.dot_general(a_ref[...].astype(BF16), b_ref[...].astype(BF16), dn,
                            preferred_element_type=F32)

        @pl.when(k == 0)
        def _():
            o_ref[...] = p + refs[2][...] if has_acc else p

        @pl.when(k > 0)
        def _():
            o_ref[...] += p

    return pl.pallas_call(
        body, name=name,
        out_shape=jax.ShapeDtypeStruct(o_shape, F32),
        grid=(m_dim // tm, n_dim // tn, k_dim // tk),
        in_specs=[a_spec, b_spec] + ([o_spec] if has_acc else []),
        out_specs=o_spec,
        compiler_params=_params(("parallel", "parallel", "arbitrary")),
    )(*([a, b] + ([acc] if has_acc else [])))


def _mm_relu2(a, b, *, name, tm=1024):
    m_dim, k_dim = a.shape
    n_blocks, _, tn = b.shape
    n_dim = n_blocks * tn
    tm = _tile(m_dim, tm)

    def body(a_ref, b_ref, h_ref, act_ref):
        h = jnp.dot(a_ref[...].astype(BF16), b_ref[...].astype(BF16), preferred_element_type=F32)
        h_ref[...] = h
        r = jnp.maximum(h, 0.0)
        act_ref[...] = (r * r).astype(BF16)

    o_spec = pl.BlockSpec((tm, tn), lambda i, j: (i, j))
    return pl.pallas_call(
        body, name=name,
        out_shape=(jax.ShapeDtypeStruct((m_dim, n_dim), F32), jax.ShapeDtypeStruct((m_dim, n_dim), BF16)),
        grid=(m_dim // tm, n_dim // tn),
        in_specs=[pl.BlockSpec((tm, k_dim), lambda i, j: (i, 0)),
                  pl.BlockSpec((None, k_dim, tn), lambda i, j: (j, 0, 0))],
        out_specs=(o_spec, o_spec),
        compiler_params=_params(("parallel", "parallel")),
    )(a, b)


def _mm_relu2_grad(d, b, h, *, name, tm=1024, tn=1024):
    m_dim, k_dim = d.shape
    n_dim = b.shape[0]
    tm, tn = _tile(m_dim, tm), _tile(n_dim, tn)

    def body(d_ref, b_ref, h_ref, o_ref):
        p = lax.dot_general(d_ref[...].astype(BF16), b_ref[...].astype(BF16), ((NT), ((), ())),
                            preferred_element_type=F32)
        o_ref[...] = (p * (2.0 * jnp.maximum(h_ref[...], 0.0))).astype(BF16)

    o_spec = pl.BlockSpec((tm, tn), lambda i, j: (i, j))
    return pl.pallas_call(
        body, name=name,
        out_shape=jax.ShapeDtypeStruct((m_dim, n_dim), BF16),
        grid=(m_dim // tm, n_dim // tn),
        in_specs=[pl.BlockSpec((tm, k_dim), lambda i, j: (i, 0)), pl.BlockSpec((tn, k_dim), lambda i, j: (j, 0)), o_spec],
        out_specs=o_spec,
        compiler_params=_params(("parallel", "parallel")),
    )(d, b, h)


def _rowwise(f, rows, params, row_out, acc_out, *, tm, name):
    length = rows[0].shape[0]
    tm = _row_tile(length, tm)
    nr, npar, nro = len(rows), len(params), len(row_out)

    def body(*refs):
        ins = [r[...] for r in refs[:nr + npar]]
        outs = refs[nr + npar:]
        r_o, a_o = f(*ins)
        for ref, val in zip(outs[:nro], r_o):
            ref[...] = val.astype(ref.dtype)
        i = pl.program_id(0)
        for ref, val in zip(outs[nro:], a_o):
            @pl.when(i == 0)
            def _(ref=ref, val=val):
                ref[...] = val.astype(ref.dtype)

            @pl.when(i > 0)
            def _(ref=ref, val=val):
                ref[...] += val.astype(ref.dtype)

    in_specs = ([pl.BlockSpec((tm, r.shape[1]), lambda i: (i, 0)) for r in rows]
                + [pl.BlockSpec(p.shape, lambda i: (0, 0)) for p in params])
    out_specs = ([pl.BlockSpec((tm, w), lambda i: (i, 0)) for w, _ in row_out]
                 + [pl.BlockSpec(s, lambda i: (0, 0)) for s, _ in acc_out])
    out_shape = ([jax.ShapeDtypeStruct((length, w), dt) for w, dt in row_out]
                 + [jax.ShapeDtypeStruct(s, dt) for s, dt in acc_out])
    res = pl.pallas_call(
        body, name=name, out_shape=out_shape, grid=(length // tm,),
        in_specs=in_specs, out_specs=out_specs,
        compiler_params=_params(("arbitrary",) if acc_out else ("parallel",)),
    )(*rows, *params)
    return res[:nro], res[nro:]


def _rw_fwd(f, rows, params, *, tm, name, out_dtypes=None):
    tm_ = _row_tile(rows[0].shape[0], tm)
    shapes = jax.eval_shape(f, *[jax.ShapeDtypeStruct((tm_, r.shape[1]), r.dtype) for r in rows],
                            *[jax.ShapeDtypeStruct(p.shape, p.dtype) for p in params])
    row_out = [(s.shape[1], s.dtype if out_dtypes is None else dt)
               for s, dt in zip(shapes, out_dtypes or shapes)]
    outs, _ = _rowwise(lambda *v: (f(*v), ()), rows, params, row_out, [], tm=tm, name=name)
    return outs


def _rw_bwd(f, rows, params, cots, *, row_grad, param_grad, tm, name, row_dtypes=None):
    nr, npar, nct = len(rows), len(params), len(cots)

    def g(*vals):
        prim = vals[:nr] + vals[nr + nct:]
        ct = vals[nr:nr + nct]
        _, vjp = jax.vjp(f, *prim)
        grads = vjp(tuple(ct))
        return (tuple(grads[i] for i in range(nr) if row_grad[i]),
                tuple(grads[nr + i] for i in range(npar) if param_grad[i]))

    widths = [rows[i].shape[1] for i in range(nr) if row_grad[i]]
    row_out = list(zip(widths, row_dtypes or [F32] * len(widths)))
    acc_out = [(params[i].shape, F32) for i in range(npar) if param_grad[i]]
    return _rowwise(g, list(rows) + list(cots), params, row_out, acc_out, tm=tm, name=name)


def _f_ln_res(x, h, g, b):
    pre = DN_ALPHA * x + h
    mu = jnp.mean(pre, axis=-1, keepdims=True)
    d = pre - mu
    var = jnp.mean(d * d, axis=-1, keepdims=True)
    return (d * lax.rsqrt(var + LN_EPS) * g + b,)


def _silu(t):
    return t * jax.nn.sigmoid(t)


def _f_gdn_qkv(c):
    a = _silu(c)
    outs = []
    for part, scale in ((0, HEAD_DIM ** -0.5), (1, 1.0)):
        heads = []
        for h in range(GDN_HEADS):
            t = a[:, part * D_MODEL + h * HEAD_DIM: part * D_MODEL + (h + 1) * HEAD_DIM]
            t = t * lax.rsqrt(jnp.sum(t * t, axis=-1, keepdims=True) + 1e-6)
            heads.append(t * scale if scale != 1.0 else t)
        outs.append(jnp.concatenate(heads, axis=-1))
    outs.append(a[:, 2 * D_MODEL:])
    return tuple(outs)


def _f_gdn_out(o, z, norm_g):
    heads = []
    for h in range(GDN_HEADS):
        t = o[:, h * HEAD_DIM:(h + 1) * HEAD_DIM]
        t = t * lax.rsqrt(jnp.mean(t * t, axis=-1, keepdims=True) + RMS_EPS) * norm_g
        heads.append(t)
    return (jnp.concatenate(heads, axis=-1) * _silu(z),)


def _f_attn(xq, kmem, vmem):
    heads = []
    for h in range(XA_HEADS):
        sl = slice(h * HEAD_DIM, (h + 1) * HEAD_DIM)
        s = lax.dot_general(xq[:, sl].astype(BF16), kmem[:, sl].astype(BF16),
                            (((1,), (1,)), ((), ())), preferred_element_type=F32) * (HEAD_DIM ** -0.5)
        m = lax.stop_gradient(jnp.max(s, axis=-1, keepdims=True))
        e = jnp.exp(s - m)
        p = e / jnp.sum(e, axis=-1, keepdims=True)
        heads.append(jnp.dot(p.astype(BF16), vmem[:, sl].astype(BF16), preferred_element_type=F32))
    return (jnp.concatenate(heads, axis=-1),)


def _f_s5_gelu(y, u, d):
    return (jax.nn.gelu(y + d * u),)


def _f_s5_gate(zg, t, b):
    return (zg * jax.nn.sigmoid(t + b),)


def _f_add(a, b):
    return (a + b,)


def _f_add4(a, b, c, d):
    return (((a + b.astype(F32)) + c.astype(F32)) + d.astype(F32),)


def _f_adamw(w, g, m, v):
    m = ADAM_B1 * m + (1.0 - ADAM_B1) * g
    v = ADAM_B2 * v + (1.0 - ADAM_B2) * jnp.square(g)
    m_hat = m / (1.0 - ADAM_B1 ** ADAM_STEP)
    v_hat = v / (1.0 - ADAM_B2 ** ADAM_STEP)
    delta = -ADAM_LR * (m_hat / (jnp.sqrt(v_hat) + ADAM_EPS) + ADAM_WD * w)
    return delta, m, v


def _conv_fwd(u, w, *, tm, name):
    length, chans = u.shape
    tm = min(tm, length)
    tc = _tile(chans, 1024)
    hb = tm // SUBLANES

    def body(cur_ref, prev_ref, w_ref, o_ref, buf):
        i = pl.program_id(1)
        buf[0:SUBLANES, :] = jnp.where(i > 0, prev_ref[...], 0.0)
        buf[SUBLANES:, :] = cur_ref[...]
        acc = buf[pl.ds(SUBLANES - 3, tm), :] * w_ref[0:1, :]
        for k in range(1, GDN_CONV):
            acc = acc + buf[pl.ds(SUBLANES - 3 + k, tm), :] * w_ref[k:k + 1, :]
        o_ref[...] = acc

    return pl.pallas_call(
        body, name=name, out_shape=jax.ShapeDtypeStruct(u.shape, F32),
        grid=(chans // tc, length // tm),
        in_specs=[pl.BlockSpec((tm, tc), lambda j, i: (i, j)),
                  pl.BlockSpec((SUBLANES, tc), lambda j, i: (jnp.maximum(i * hb - 1, 0), j)),
                  pl.BlockSpec((GDN_CONV, tc), lambda j, i: (0, j))],
        out_specs=pl.BlockSpec((tm, tc), lambda j, i: (i, j)),
        scratch_shapes=[pltpu.VMEM((tm + SUBLANES, tc), F32)],
        compiler_params=_params(("parallel", "parallel")),
    )(u, u, w)


def _conv_bwd(u, w, dc, *, tm, name):
    length, chans = u.shape
    tm = min(tm, length)
    tc = _tile(chans, 1024)
    hb = tm // SUBLANES
    last = length // tm - 1

    def body(u_ref, uprev_ref, dc_ref, dcnext_ref, w_ref, du_ref, dw_ref, ubuf, dbuf):
        i = pl.program_id(1)
        ubuf[0:SUBLANES, :] = jnp.where(i > 0, uprev_ref[...], 0.0)
        ubuf[SUBLANES:, :] = u_ref[...]
        dbuf[0:tm, :] = dc_ref[...]
        dbuf[tm:, :] = jnp.where(i < last, dcnext_ref[...], 0.0)
        dcv = dc_ref[...]
        du = dbuf[pl.ds(3, tm), :] * w_ref[0:1, :]
        rows = [jnp.sum(dcv * ubuf[pl.ds(SUBLANES - 3, tm), :], axis=0, keepdims=True)]
        for k in range(1, GDN_CONV):
            du = du + dbuf[pl.ds(3 - k, tm), :] * w_ref[k:k + 1, :]
            rows.append(jnp.sum(dcv * ubuf[pl.ds(SUBLANES - 3 + k, tm), :], axis=0, keepdims=True))
        du_ref[...] = du.astype(du_ref.dtype)
        dwv = jnp.concatenate(rows, axis=0)

        @pl.when(i == 0)
        def _():
            dw_ref[...] = dwv

        @pl.when(i > 0)
        def _():
            dw_ref[...] += dwv

    return pl.pallas_call(
        body, name=name,
        out_shape=(jax.ShapeDtypeStruct(u.shape, BF16), jax.ShapeDtypeStruct((GDN_CONV, chans), F32)),
        grid=(chans // tc, length // tm),
        in_specs=[pl.BlockSpec((tm, tc), lambda j, i: (i, j)),
                  pl.BlockSpec((SUBLANES, tc), lambda j, i: (jnp.maximum(i * hb - 1, 0), j)),
                  pl.BlockSpec((tm, tc), lambda j, i: (i, j)),
                  pl.BlockSpec((SUBLANES, tc), lambda j, i: (jnp.minimum((i + 1) * hb, (last + 1) * hb - 1), j)),
                  pl.BlockSpec((GDN_CONV, tc), lambda j, i: (0, j))],
        out_specs=(pl.BlockSpec((tm, tc), lambda j, i: (i, j)),
                   pl.BlockSpec((GDN_CONV, tc), lambda j, i: (0, j))),
        scratch_shapes=[pltpu.VMEM((tm + SUBLANES, tc), F32), pltpu.VMEM((tm + SUBLANES, tc), F32)],
        compiler_params=_params(("parallel", "arbitrary")),
    )(u, u, dc, dc, w)


def _dot(a, b, dims, precision=None):
    if precision is None:
        a, b = a.astype(BF16), b.astype(BF16)
    return lax.dot_general(a, b, (dims, ((), ())), preferred_element_type=F32, precision=precision)


def _dot3(a, b, dims):
    ah, bh = a.astype(BF16), b.astype(BF16)
    al, bl = (a - ah.astype(F32)).astype(BF16), (b - bh.astype(F32)).astype(BF16)

    def d(x, y):
        return lax.dot_general(x, y, (dims, ((), ())), preferred_element_type=F32)

    return d(ah, bh) + (d(ah, bl) + d(al, bh))


NN = ((1,), (0,))
NT = ((1,), (1,))
TN = ((0,), (0,))
HI = lax.Precision.HIGHEST


def _hmap(f, *lists):
    return [f(*t) for t in zip(*lists)]


@jax.custom_vjp
def _unit_lower_inverse(a):
    c = a[0].shape[0]
    eye = (lax.broadcasted_iota(jnp.int32, (c, c), 0) == lax.broadcasted_iota(jnp.int32, (c, c), 1)).astype(F32)
    p = _hmap(lambda x: -x, a)
    t = _hmap(lambda x: eye + x, p)
    for _ in range(int(math.log2(c)) - 1):
        p = _hmap(lambda x: _dot3(x, x, NN), p)
        t = _hmap(lambda x, y: x + _dot3(x, y, NN), t, p)
    return t


def _uli_fwd(a):
    t = _unit_lower_inverse(a)
    return t, t


def _uli_bwd(t, dt):
    left = _hmap(lambda x, y: _dot3(x, y, TN), t, dt)
    return (_hmap(lambda x, y: -_dot3(x, y, NT), left, t),)


_unit_lower_inverse.defvjp(_uli_fwd, _uli_bwd)


def _gdn_chunk(q, k, v, bl, al, a_log, dt_bias, state):
    c = q[0].shape[0]
    row = lax.broadcasted_iota(jnp.int32, (c, c), 0)
    col = lax.broadcasted_iota(jnp.int32, (c, c), 1)
    causal = row >= col
    strict = row > col
    eye = (row == col).astype(F32)
    beta = _hmap(jax.nn.sigmoid, bl)
    g = _hmap(lambda a_, l_, d_: -jnp.exp(a_) * jax.nn.softplus(l_ + d_), a_log, al, dt_bias)
    g_r = _hmap(lambda x: jnp.sum(eye * x, axis=0, keepdims=True), g)
    gc = _hmap(lambda x: jnp.sum(jnp.where(causal, x, 0.0), axis=1, keepdims=True), g_r)
    gc_r = _hmap(lambda x: jnp.sum(jnp.where(row <= col, x, 0.0), axis=0, keepdims=True), g)
    decay = _hmap(lambda x, y: jnp.where(causal, jnp.exp(jnp.where(causal, x - y, 0.0)), 0.0), gc, gc_r)
    e_gc = _hmap(jnp.exp, gc)
    kb = _hmap(jnp.multiply, k, beta)
    vb = _hmap(jnp.multiply, v, beta)
    a_mat = _hmap(lambda x, y, d: jnp.where(strict, _dot(x, y, NT) * d, 0.0), kb, k, decay)
    t_inv = _unit_lower_inverse(a_mat)
    u_blk = _hmap(lambda t, x: _dot(t, x, NN), t_inv, vb)
    w_blk = _hmap(lambda t, x, e: _dot(t, x * e, NN), t_inv, kb, e_gc)
    v_new = _hmap(lambda u, w, s: u - _dot(w, s, NN), u_blk, w_blk, state)
    attn = _hmap(lambda x, y, d: _dot(x, y, NT) * d, q, k, decay)
    o_state = _hmap(lambda x, e, s: _dot(x * e, s, NN), q, e_gc, state)
    o = _hmap(lambda base, at, vn: base + _dot(at, vn, NN), o_state, attn, v_new)
    g_last = _hmap(lambda x: jnp.sum(x, axis=0, keepdims=True), g)
    k_dec = _hmap(lambda x, gl, c_: x * jnp.exp(gl - c_), k, g_last, gc)
    new_state = _hmap(lambda s, gl, kd, vn: s * jnp.exp(gl) + _dot(kd, vn, TN), state, g_last, k_dec, v_new)
    return o, new_state


def _gdn_operands(q_ref, k_ref, v_ref, bav, alog_ref, dtb_ref):
    hs = range(GDN_HEADS)
    cols = [slice(h * HEAD_DIM, (h + 1) * HEAD_DIM) for h in hs]
    return ([q_ref[:, sl] for sl in cols], [k_ref[:, sl] for sl in cols], [v_ref[:, sl] for sl in cols],
            [bav[:, h:h + 1] for h in hs], [bav[:, h + GDN_HEADS:h + GDN_HEADS + 1] for h in hs],
            [alog_ref[h:h + 1, 0:1] for h in hs], [dtb_ref[h:h + 1, 0:1] for h in hs])


def _gdn_scan_fwd(q, k, v, ba, a_log, dt_bias, *, name):
    length = q.shape[0]
    n = length // GDN_CHUNK
    c = GDN_CHUNK

    def body(q_ref, k_ref, v_ref, ba_ref, alog_ref, dtb_ref, o_ref, s_ref, state):
        i = pl.program_id(0)

        @pl.when(i == 0)
        def _():
            state[...] = jnp.zeros_like(state)

        bav = ba_ref[...]
        heads = [slice(h * HEAD_DIM, (h + 1) * HEAD_DIM) for h in range(GDN_HEADS)]
        s_in = [state[h] for h in range(GDN_HEADS)]
        o, s_out = _gdn_chunk(*_gdn_operands(q_ref, k_ref, v_ref, bav, alog_ref, dtb_ref), s_in)
        for h, sl in enumerate(heads):
            s_ref[h] = s_in[h]
            o_ref[:, sl] = o[h]
            state[h] = s_out[h]

    row_spec = pl.BlockSpec((c, D_MODEL), lambda i: (i, 0))
    small = pl.BlockSpec((GDN_HEADS, LANES), lambda i: (0, 0))
    return pl.pallas_call(
        body, name=name,
        out_shape=(jax.ShapeDtypeStruct((length, D_MODEL), F32),
                   jax.ShapeDtypeStruct((n, GDN_HEADS, HEAD_DIM, HEAD_DIM), F32)),
        grid=(n,),
        in_specs=[row_spec, row_spec, row_spec, pl.BlockSpec((c, LANES), lambda i: (i, 0)), small, small],
        out_specs=(row_spec, pl.BlockSpec((None, GDN_HEADS, HEAD_DIM, HEAD_DIM), lambda i: (i, 0, 0, 0))),
        scratch_shapes=[pltpu.VMEM((GDN_HEADS, HEAD_DIM, HEAD_DIM), F32)],
        compiler_params=_params(("arbitrary",)),
    )(q, k, v, ba, a_log, dt_bias)


def _gdn_scan_bwd(q, k, v, ba, a_log, dt_bias, states, do, *, name):
    length = q.shape[0]
    n = length // GDN_CHUNK
    c = GDN_CHUNK

    def body(q_ref, k_ref, v_ref, ba_ref, alog_ref, dtb_ref, s_ref, do_ref,
             dq_ref, dk_ref, dv_ref, dba_ref, dalog_ref, ddtb_ref, dstate):
        i = pl.program_id(0)

        @pl.when(i == 0)
        def _():
            dstate[...] = jnp.zeros_like(dstate)
            dalog_ref[...] = jnp.zeros_like(dalog_ref)
            ddtb_ref[...] = jnp.zeros_like(ddtb_ref)

        bav = ba_ref[...]
        lane = lax.broadcasted_iota(jnp.int32, (c, LANES), 1)
        sub8 = lax.broadcasted_iota(jnp.int32, (GDN_HEADS, LANES), 0)
        lane8 = lax.broadcasted_iota(jnp.int32, (GDN_HEADS, LANES), 1)
        slab = jnp.zeros((c, LANES), F32)
        dalog_all = jnp.zeros((GDN_HEADS, LANES), F32)
        ddtb_all = jnp.zeros((GDN_HEADS, LANES), F32)
        heads = [slice(h * HEAD_DIM, (h + 1) * HEAD_DIM) for h in range(GDN_HEADS)]
        ds_in = [dstate[h] for h in range(GDN_HEADS)]
        s_in = [s_ref[h] for h in range(GDN_HEADS)]
        _, vjp = jax.vjp(_gdn_chunk, *_gdn_operands(q_ref, k_ref, v_ref, bav, alog_ref, dtb_ref), s_in)
        dq, dk, dv, dbl, dal, dalog, ddtb, ds = vjp(([do_ref[:, sl] for sl in heads], ds_in))
        for h, sl in enumerate(heads):
            dq_ref[:, sl] = dq[h]
            dk_ref[:, sl] = dk[h]
            dv_ref[:, sl] = dv[h]
            dstate[h] = ds[h]
            slab = slab + jnp.where(lane == h, dbl[h], 0.0) + jnp.where(lane == h + GDN_HEADS, dal[h], 0.0)
            here = (sub8 == h) & (lane8 == 0)
            dalog_all = dalog_all + jnp.where(here, dalog[h], 0.0)
            ddtb_all = ddtb_all + jnp.where(here, ddtb[h], 0.0)
        dba_ref[...] = slab
        dalog_ref[...] += dalog_all
        ddtb_ref[...] += ddtb_all

    row_spec = pl.BlockSpec((c, D_MODEL), lambda i: (n - 1 - i, 0))
    small = pl.BlockSpec((GDN_HEADS, LANES), lambda i: (0, 0))
    return pl.pallas_call(
        body, name=name,
        out_shape=(jax.ShapeDtypeStruct((length, D_MODEL), F32),) * 3
        + (jax.ShapeDtypeStruct((length, LANES), F32),
           jax.ShapeDtypeStruct((GDN_HEADS, LANES), F32), jax.ShapeDtypeStruct((GDN_HEADS, LANES), F32)),
        grid=(n,),
        in_specs=[row_spec, row_spec, row_spec,
                  pl.BlockSpec((c, LANES), lambda i: (n - 1 - i, 0)), small, small,
                  pl.BlockSpec((None, GDN_HEADS, HEAD_DIM, HEAD_DIM), lambda i: (n - 1 - i, 0, 0, 0)),
                  row_spec],
        out_specs=(row_spec, row_spec, row_spec,
                   pl.BlockSpec((c, LANES), lambda i: (n - 1 - i, 0)), small, small),
        scratch_shapes=[pltpu.VMEM((GDN_HEADS, HEAD_DIM, HEAD_DIM), F32)],
        compiler_params=_params(("arbitrary",)),
    )(q, k, v, ba, a_log, dt_bias, states, do)


S5_W = S5_T * LANES
S5_S = 2 * 8 * S5_STATE
S5_SH = S5_S // 2


def _iota2(shape):
    return lax.broadcasted_iota(jnp.int32, shape, 0), lax.broadcasted_iota(jnp.int32, shape, 1)


def _s5_rep_t(t, dtype):
    row, col = _iota2((S5_T * S5_GROUP, LANES))
    return ((jnp.right_shift(row, 4) == t) & (jnp.bitwise_and(row, 15) == jnp.bitwise_and(col, 15))).astype(dtype)


def _s5_rep_state(dtype):
    row, col = _iota2((2 * S5_STATE, S5_S))
    return ((jnp.right_shift(row, 6) == jnp.right_shift(col, 9))
            & (jnp.bitwise_and(row, 63) == jnp.bitwise_and(col, 63))).astype(dtype)


def _s5_masks():
    row, col = _iota2((LANES, LANES))
    m_ab = jnp.right_shift(row, 4) == jnp.right_shift(col, 4)
    row, col = _iota2((S5_S, LANES))
    m_e = jnp.bitwise_and(jnp.right_shift(row, 6), 7) == jnp.right_shift(col, 4)
    row, col = _iota2((LANES, S5_S))
    m_f = jnp.right_shift(row, 4) == jnp.bitwise_and(jnp.right_shift(col, 6), 7)
    return m_ab, m_e, m_f


def _s5_expand(kx_ref, ec_ref, fc_ref, kb_scr, e_scr, f_scr):
    m_ab, m_e, m_f = _s5_masks()
    kx = kx_ref[...].astype(BF16)
    ec = ec_ref[...].astype(BF16)
    rep_state = _s5_rep_state(BF16)
    for t in range(S5_T):
        rep = _s5_rep_t(t, BF16)
        cols = slice(t * LANES, (t + 1) * LANES)
        kb_scr[t] = jnp.where(m_ab, jnp.dot(kx, rep, preferred_element_type=F32), 0.0).astype(BF16)
        e_scr[:, cols] = jnp.where(m_e, jnp.dot(ec, rep, preferred_element_type=F32), 0.0).astype(BF16)
        f_scr[cols, :] = jnp.where(m_f, jnp.dot(fc_ref[t].astype(BF16), rep_state, preferred_element_type=F32),
                                   0.0).astype(BF16)


def _s5_token_rows(ref, n):
    return [ref[pl.ds(t, n, stride=S5_T), :].astype(BF16) for t in range(S5_T)]


def _s5_scan_fwd(u, kx, ec, fc, at, *, name):
    length = u.shape[0]
    n = length // S5_T
    assert n % SUBLANES == 0

    def body(u_ref, kx_ref, ec_ref, fc_ref, at_ref, y_ref, h_ref, kb_scr, e_scr, f_scr, g_scr):
        _s5_expand(kx_ref, ec_ref, fc_ref, kb_scr, e_scr, f_scr)
        us = _s5_token_rows(u_ref, n)
        g_scr[...] = jnp.dot(jnp.concatenate(us, axis=1), f_scr[...], preferred_element_type=F32)
        ar, ai = at_ref[:, :S5_SH], at_ref[:, S5_SH:]

        def step(blk, h):
            base = pl.multiple_of(blk * SUBLANES, SUBLANES)
            g8 = g_scr[pl.ds(base, SUBLANES), :]
            rows = []
            for r in range(SUBLANES):
                rows.append(h)
                hr, hi = h[:, :S5_SH], h[:, S5_SH:]
                h = jnp.concatenate([ar * hr - ai * hi, ar * hi + ai * hr], axis=1) + g8[r:r + 1, :]
            h_ref[pl.ds(base, SUBLANES), :] = jnp.concatenate(rows, axis=0)
            return h

        lax.fori_loop(0, n // SUBLANES, step, jnp.zeros((1, S5_S), F32))
        hb = h_ref[...].astype(BF16)
        for t in range(S5_T):
            acc = jnp.dot(hb, e_scr[:, t * LANES:(t + 1) * LANES], preferred_element_type=F32)
            for s in range(t + 1):
                acc = acc + jnp.dot(us[s], kb_scr[t - s], preferred_element_type=F32)
            y_ref[pl.ds(t, n, stride=S5_T), :] = acc

    return pl.pallas_call(
        body, name=name,
        out_shape=(jax.ShapeDtypeStruct((length, D_MODEL), F32), jax.ShapeDtypeStruct((S5_TILES, n, S5_S), F32)),
        grid=(S5_TILES,),
        in_specs=[pl.BlockSpec((length, LANES), lambda k: (0, k)), _s5_spec(LANES, S5_T * S5_GROUP),
                  _s5_spec(S5_S, S5_T * S5_GROUP), _s5_spec(S5_T, LANES, LANES), _s5_spec(1, S5_S)],
        out_specs=(pl.BlockSpec((length, LANES), lambda k: (0, k)), _s5_spec(n, S5_S)),
        scratch_shapes=[pltpu.VMEM((S5_T, LANES, LANES), BF16), pltpu.VMEM((S5_S, S5_W), BF16),
                        pltpu.VMEM((S5_W, S5_S), BF16), pltpu.VMEM((n, S5_S), F32)],
        compiler_params=_params(("parallel",)),
    )(u, kx, ec, fc, at)


def _s5_spec(*tail):
    return pl.BlockSpec((None,) + tail, lambda k: (k,) + (0,) * len(tail))


def _s5_scan_bwd(dy, kx, ec, fc, at, hs, *, name):
    length = dy.shape[0]
    n = length // S5_T

    def body(dy_ref, kx_ref, ec_ref, fc_ref, at_ref, h_ref, du_ref, dg_ref, dat_ref, kb_scr, e_scr, f_scr, dh_scr):
        _s5_expand(kx_ref, ec_ref, fc_ref, kb_scr, e_scr, f_scr)
        dys = _s5_token_rows(dy_ref, n)
        dh_scr[...] = _dot(jnp.concatenate(dys, axis=1), e_scr[...], NT)
        ar, ai = at_ref[:, :S5_SH], at_ref[:, S5_SH:]

        def step(it, carry):
            cy, dat = carry
            base = pl.multiple_of((n // SUBLANES - 1 - it) * SUBLANES, SUBLANES)
            dh8 = dh_scr[pl.ds(base, SUBLANES), :]
            h8 = h_ref[pl.ds(base, SUBLANES), :]
            rows = [None] * SUBLANES
            for r in reversed(range(SUBLANES)):
                rows[r] = cy
                cr, ci = cy[:, :S5_SH], cy[:, S5_SH:]
                hr, hi = h8[r:r + 1, :S5_SH], h8[r:r + 1, S5_SH:]
                dat = dat + jnp.concatenate([cr * hr + ci * hi, ci * hr - cr * hi], axis=1)
                cy = dh8[r:r + 1, :] + jnp.concatenate([ar * cr + ai * ci, ar * ci - ai * cr], axis=1)
            dg_ref[pl.ds(base, SUBLANES), :] = jnp.concatenate(rows, axis=0)
            return cy, dat

        zero = jnp.zeros((1, S5_S), F32)
        _, dat = lax.fori_loop(0, n // SUBLANES, step, (zero, zero))
        dat_ref[...] = dat
        dgb = dg_ref[...].astype(BF16)
        for s in range(S5_T):
            acc = _dot(dgb, f_scr[s * LANES:(s + 1) * LANES, :], NT)
            for t in range(s, S5_T):
                acc = acc + _dot(dys[t], kb_scr[t - s], NT)
            du_ref[pl.ds(s, n, stride=S5_T), :] = acc

    row_spec = pl.BlockSpec((length, LANES), lambda k: (0, k))
    return pl.pallas_call(
        body, name=name,
        out_shape=(jax.ShapeDtypeStruct((length, D_MODEL), F32), jax.ShapeDtypeStruct((S5_TILES, n, S5_S), F32),
                   jax.ShapeDtypeStruct((S5_TILES, 1, S5_S), F32)),
        grid=(S5_TILES,),
        in_specs=[row_spec, _s5_spec(LANES, S5_T * S5_GROUP), _s5_spec(S5_S, S5_T * S5_GROUP),
                  _s5_spec(S5_T, LANES, LANES), _s5_spec(1, S5_S), _s5_spec(n, S5_S)],
        out_specs=(row_spec, _s5_spec(n, S5_S), _s5_spec(1, S5_S)),
        scratch_shapes=[pltpu.VMEM((S5_T, LANES, LANES), BF16), pltpu.VMEM((S5_S, S5_W), BF16),
                        pltpu.VMEM((S5_W, S5_S), BF16), pltpu.VMEM((n, S5_S), F32)],
        compiler_params=_params(("parallel",)),
    )(dy, kx, ec, fc, at, hs)


def _s5_operator_grads(dy, u, hs, dg, *, name):
    length = u.shape[0]
    n = length // S5_T

    def body(dy_ref, u_ref, h_ref, dg_ref, dkx_ref, dec_ref, dfc_ref):
        dys = _s5_token_rows(dy_ref, n)
        us = _s5_token_rows(u_ref, n)
        ucat = jnp.concatenate(us, axis=1)
        m_ab, m_e, m_f = _s5_masks()
        hb = h_ref[...].astype(BF16)
        dgb = dg_ref[...].astype(BF16)
        lane = lax.broadcasted_iota(jnp.int32, (1, LANES), 1)
        lane_group = jnp.right_shift(lane, 4)

        def own_block(x, mask):
            x = jnp.where(mask, x, 0.0)
            for shift in (64, 32, 16):
                x = x + pltpu.roll(x, shift, 1)
            return x

        def place(halves, t, x):
            halves[t // 8] = jnp.where(lane_group == t % 8, x, halves[t // 8])

        dkb = [jnp.zeros((LANES, LANES), F32) for _ in range(S5_T)]
        dec = [jnp.zeros((S5_S, LANES), F32) for _ in range(2)]
        for t in range(S5_T):
            d_t = _dot(ucat, dys[t], TN)
            for s in range(t + 1):
                dkb[t - s] = dkb[t - s] + d_t[s * LANES:(s + 1) * LANES, :]
            place(dec, t, own_block(_dot(hb, dys[t], TN), m_e))
            wide = jnp.where(m_f, _dot(us[t], dgb, TN), 0.0)
            parts = []
            for r in range(2):
                acc = wide[:, r * S5_SH:r * S5_SH + LANES]
                for q in range(1, S5_SH // LANES):
                    acc = acc + wide[:, r * S5_SH + q * LANES:r * S5_SH + (q + 1) * LANES]
                parts.append(acc + pltpu.roll(acc, S5_STATE, 1))
            dfc_ref[t] = jnp.where(lane < S5_STATE, parts[0], parts[1])
        dkx = [jnp.zeros((LANES, LANES), F32) for _ in range(2)]
        for t in range(S5_T):
            place(dkx, t, own_block(dkb[t], m_ab))
        dkx_ref[...] = jnp.concatenate(dkx, axis=1)
        dec_ref[...] = jnp.concatenate(dec, axis=1)

    row_spec = pl.BlockSpec((length, LANES), lambda k: (0, k))
    outs = (_s5_spec(LANES, S5_T * S5_GROUP), _s5_spec(S5_S, S5_T * S5_GROUP), _s5_spec(S5_T, LANES, LANES))
    return pl.pallas_call(
        body, name=name,
        out_shape=(jax.ShapeDtypeStruct((S5_TILES, LANES, S5_T * S5_GROUP), F32),
                   jax.ShapeDtypeStruct((S5_TILES, S5_S, S5_T * S5_GROUP), F32),
                   jax.ShapeDtypeStruct((S5_TILES, S5_T, LANES, LANES), F32)),
        grid=(S5_TILES,),
        in_specs=[row_spec, row_spec, _s5_spec(n, S5_S), _s5_spec(n, S5_S)],
        out_specs=outs,
        compiler_params=_params(("parallel",)),
    )(dy, u, hs, dg)


def _s5_prep(a_re, a_im, b_re, b_im, c_re, c_im, log_dt):
    t_len, tiles = S5_T, S5_TILES
    dt = jnp.exp(log_dt)[:, None]
    mag = jnp.exp(a_re * dt)
    ab_re, ab_im = mag * jnp.cos(a_im * dt), mag * jnp.sin(a_im * dt)
    den = jnp.square(a_re) + jnp.square(a_im)
    n_re, n_im = ab_re - 1.0, ab_im
    f_re = (n_re * a_re + n_im * a_im) / den
    f_im = (n_im * a_re - n_re * a_im) / den
    bb_re = f_re[..., None] * b_re - f_im[..., None] * b_im
    bb_im = f_re[..., None] * b_im + f_im[..., None] * b_re

    def powers(exponents):
        e = exponents[:, None, None]
        m = jnp.exp(e * (a_re * dt))
        return m * jnp.cos(e * (a_im * dt)), m * jnp.sin(e * (a_im * dt))

    p_re, p_im = powers(jnp.arange(t_len + 1, dtype=F32))
    rev_re, rev_im = powers((t_len - 1) - jnp.arange(t_len, dtype=F32))
    ca_re = c_re[None] * p_re[:, :, None, :] - c_im[None] * p_im[:, :, None, :]
    ca_im = c_re[None] * p_im[:, :, None, :] + c_im[None] * p_re[:, :, None, :]
    lag = (jnp.einsum('tgip,gpj->tgij', ca_re[:t_len], bb_re, precision=HI)
           - jnp.einsum('tgip,gpj->tgij', ca_im[:t_len], bb_im, precision=HI))
    kx = lag.reshape(t_len, tiles, 8, S5_GROUP, S5_GROUP).transpose(1, 2, 4, 0, 3)
    kx = kx.reshape(tiles, LANES, t_len * S5_GROUP)
    e_st = jnp.stack([ca_re[1:], -ca_im[1:]])
    e_st = e_st.reshape(2, t_len, tiles, 8, S5_GROUP, S5_STATE).transpose(2, 0, 3, 5, 1, 4)
    ec = e_st.reshape(tiles, S5_S, t_len * S5_GROUP)
    ab_b = jnp.stack([rev_re[..., None] * bb_re[None] - rev_im[..., None] * bb_im[None],
                      rev_re[..., None] * bb_im[None] + rev_im[..., None] * bb_re[None]])
    ab_b = ab_b.reshape(2, t_len, tiles, 8, S5_STATE, S5_GROUP).transpose(2, 1, 3, 5, 0, 4)
    fc = ab_b.reshape(tiles, t_len, LANES, 2 * S5_STATE)
    a_t = jnp.stack([p_re[t_len], p_im[t_len]]).reshape(2, tiles, 8 * S5_STATE).transpose(1, 0, 2)
    return kx, ec, fc, a_t.reshape(tiles, 1, S5_S)


TM_ROW = 256


def _gdn_fwd(x, w, tag):
    qkv = _mm(x, w["wqkv"], name="gdn_proj_qkv")
    z = _mm(x, w["wz"], name="gdn_proj_z")
    ba = _mm(x, w["wba"], name="gdn_proj_ba")
    cv = _conv_fwd(qkv, w["conv_w"], tm=TM_ROW, name="gdn_conv")
    q, k, v = _rw_fwd(_f_gdn_qkv, [cv], [], tm=TM_ROW, name="gdn_qkv")
    o, states = _gdn_scan_fwd(q, k, v, ba, w["a_log8"], w["dt_bias8"], name="gdn_scan")
    (mix,) = _rw_fwd(_f_gdn_out, [o, z], [w["norm_g"]], tm=TM_ROW, name="gdn_out", out_dtypes=[BF16])
    return mix, (qkv, z, ba, cv, q, k, v, states, o)


def _gdn_bwd(x, w, saved, dmix, dx_acc):
    qkv, z, ba, cv, q, k, v, states, o = saved
    (do, dz), (dnorm_g,) = _rw_bwd(_f_gdn_out, [o, z], [w["norm_g"]], [dmix], row_grad=[1, 1], param_grad=[1],
                                   tm=TM_ROW, name="gdn_out_bwd", row_dtypes=[F32, BF16])
    dq, dk, dv, dba, dalog, ddtb = _gdn_scan_bwd(q, k, v, ba, w["a_log8"], w["dt_bias8"], states, do,
                                                  name="gdn_scan_bwd")
    (dcv,), _ = _rw_bwd(_f_gdn_qkv, [cv], [], [dq, dk, dv], row_grad=[1], param_grad=[], tm=TM_ROW,
                        name="gdn_qkv_bwd")
    dqkv, dconv_w = _conv_bwd(qkv, w["conv_w"], dcv, tm=TM_ROW, name="gdn_conv_bwd")
    dx = _mm(dqkv, w["wqkv"], tb=True, acc=dx_acc, name="gdn_dx_qkv")
    dx = _mm(dz, w["wz"], tb=True, acc=dx, name="gdn_dx_z")
    dx = _mm(dba, w["wba"], tb=True, acc=dx, name="gdn_dx_ba")
    grads = dict(wqkv=_mm(x, dqkv, ta=True, name="gdn_dw_qkv"), wz=_mm(x, dz, ta=True, name="gdn_dw_z"),
                 wba=_mm(x, dba, ta=True, name="gdn_dw_ba"), conv_w=dconv_w,
                 a_log=dalog[:, 0], dt_bias=ddtb[:, 0], norm_g=dnorm_g[0])
    return dx, grads


def _s5_fwd(x, w, tag):
    u = _mm(x, w["wu"], name="s5_proj_u")
    y, hs = _s5_scan_fwd(u, w["kx"], w["ec"], w["fc"], w["a_t"], name="s5_scan")
    (zg,) = _rw_fwd(_f_s5_gelu, [y, u], [w["d"]], tm=TM_ROW, name="s5_gelu")
    t = _mm(zg, w["w_glu"], name="s5_glu")
    (mix,) = _rw_fwd(_f_s5_gate, [zg, t], [w["b_glu"]], tm=TM_ROW, name="s5_gate", out_dtypes=[BF16])
    return mix, (u, hs, y, zg, t)


def _s5_bwd(x, w, saved, dmix, dx_acc):
    u, hs, y, zg, t = saved
    (dzg, dt), (db_glu,) = _rw_bwd(_f_s5_gate, [zg, t], [w["b_glu"]], [dmix], row_grad=[1, 1], param_grad=[1],
                                   tm=TM_ROW, name="s5_gate_bwd", row_dtypes=[F32, BF16])
    dzg = _mm(dt, w["w_glu"], tb=True, acc=dzg, name="s5_dzg")
    dw_glu = _mm(zg, dt, ta=True, name="s5_dw_glu")
    (dy, du), (dd,) = _rw_bwd(_f_s5_gelu, [y, u], [w["d"]], [dzg], row_grad=[1, 1], param_grad=[1],
                              tm=TM_ROW, name="s5_gelu_bwd")
    du_scan, dg, dat = _s5_scan_bwd(dy, w["kx"], w["ec"], w["fc"], w["a_t"], hs, name="s5_scan_bwd")
    dkx, dec, dfc = _s5_operator_grads(dy, u, hs, dg, name="s5_operator_grads")
    (du,) = _rw_fwd(_f_add, [du, du_scan], [], tm=TM_ROW, name="s5_du_add", out_dtypes=[BF16])
    d_a_re, d_a_im, d_b_re, d_b_im, d_c_re, d_c_im, d_log_dt = w["prep_vjp"]((dkx, dec, dfc, dat))
    dx = _mm(du, w["wu"], tb=True, acc=dx_acc, name="s5_dx_u")
    grads = dict(wu=_mm(x, du, ta=True, name="s5_dw_u"), w_glu=dw_glu, b_glu=db_glu[0], d=dd[0],
                 a_re=d_a_re, a_im=d_a_im, b_re=d_b_re, b_im=d_b_im, c_re=d_c_re, c_im=d_c_im, log_dt=d_log_dt)
    return dx, grads


def _layer_fwd(x, mem, w, is_gdn):
    mix, msave = (_gdn_fwd if is_gdn else _s5_fwd)(x, w, "")
    xq = _mm(x, w["wxq"], name="proj_xq")
    kv = _mm(mem, w["wkv"], name="mem_kv")
    kmem, vmem = kv[:, :XA_DIM], kv[:, XA_DIM:]
    (cross,) = _rw_fwd(_f_attn, [xq], [kmem, vmem], tm=TM_ROW, name="attn", out_dtypes=[BF16])
    h = _mm(mix, w["wo_mix"], name="wo_mix")
    h = _mm(cross, w["wo_cross"], acc=h, name="wo_cross")
    (x1,) = _rw_fwd(_f_ln_res, [x, h], [w["ln1_g"], w["ln1_b"]], tm=TM_ROW, name="ln_res")
    hm, act = _mm_relu2(x1, w["w1"], name="mlp_up")
    f = _mm(act, w["w2"], name="mlp_down")
    (x2,) = _rw_fwd(_f_ln_res, [x1, f], [w["ln2_g"], w["ln2_b"]], tm=TM_ROW, name="ln_res")
    return x2, (x, msave, xq, kmem, vmem, mix, cross, h, x1, hm, act, f)


def _layer_bwd(mem, w, is_gdn, saved, dx2, token=None):
    x, msave, xq, kmem, vmem, mix, cross, h, x1, hm, act, f = saved
    ln2_g = w["ln2_g"] if token is None else w["ln2_g"] + token[0, 0]
    (dx1, df), (dg2, db2) = _rw_bwd(_f_ln_res, [x1, f], [ln2_g, w["ln2_b"]], [dx2], row_grad=[1, 1],
                                    param_grad=[1, 1], tm=TM_ROW, name="ln_res_bwd", row_dtypes=[F32, BF16])
    dhm = _mm_relu2_grad(df, w["w2"], hm, name="mlp_dhm")
    dw2 = _mm(act, df, ta=True, name="mlp_dw2")
    dx1 = _mm(dhm, w["w1"], tb=True, acc=dx1, name="mlp_dx")
    dw1 = _mm(x1, dhm, ta=True, out_blocks=N_CHIPS, name="mlp_dw1")
    (dx, dh), (dg1, db1) = _rw_bwd(_f_ln_res, [x, h], [w["ln1_g"], w["ln1_b"]], [dx1], row_grad=[1, 1],
                                   param_grad=[1, 1], tm=TM_ROW, name="ln_res_bwd", row_dtypes=[F32, BF16])
    dmix =_mm(dh, w["wo_mix"], tb=True, name="wo_dmix")
    dcross = _mm(dh, w["wo_cross"], tb=True, name="wo_dcross")
    dwo = jnp.concatenate([_mm(mix, dh, ta=True, name="wo_dw_mix"), _mm(cross, dh, ta=True, name="wo_dw_cross")], 0)
    (dxq,), (dkmem, dvmem) = _rw_bwd(_f_attn, [xq], [kmem, vmem], [dcross], row_grad=[1], param_grad=[1, 1],
                                     tm=TM_ROW, name="attn_bwd", row_dtypes=[BF16])
    dwkv = _mm(mem, jnp.concatenate([dkmem, dvmem], axis=1), ta=True, name="mem_dw_kv")
    dx = _mm(dxq, w["wxq"], tb=True, acc=dx, name="dx_xq")
    dwxq = _mm(x, dxq, ta=True, name="dw_xq")
    dx, mg = (_gdn_bwd if is_gdn else _s5_bwd)(x, w, msave, dmix, dx)
    grads = dict(mixer=mg, wxq=dwxq, wkv=dwkv, wo=dwo, w1=dw1, w2=dw2,
                 ln1_g=dg1[0], ln1_b=db1[0], ln2_g=dg2[0], ln2_b=db2[0])
    return dx, grads


def _loss_and_grad(y, target):
    def f(yv, tv):
        err = yv - tv
        return (err * (1.0 / D_MODEL),), (0.5 / D_MODEL * jnp.sum(err * err, axis=0, keepdims=True),)

    (dy,), (part,) = _rowwise(f, [y, target], [], [(D_MODEL, F32)], [((1, D_MODEL), F32)], tm=512, name="loss")
    return jnp.sum(part), dy


def _layer_weights(full, i):
    j = i // 2
    w = dict(wkv=full["w_kv_mem"][i].astype(BF16),
             wo_mix=full["w_o"][i][:D_MODEL].astype(BF16), wo_cross=full["w_o"][i][D_MODEL:].astype(BF16),
             ln1_g=full["ln1_g"][i][None], ln1_b=full["ln1_b"][i][None],
             ln2_g=full["ln2_g"][i][None], ln2_b=full["ln2_b"][i][None],
             w1=full["mlp_w1"][i].astype(BF16), w2=full["mlp_w2"][i].astype(BF16))
    if i % 2 == 0:
        w_in = full["gdn_w_in"][j]
        gd = 3 * D_MODEL
        w.update(wqkv=w_in[:, :gd].astype(BF16), wz=w_in[:, gd:gd + D_MODEL].astype(BF16),
                 wba=jnp.pad(w_in[:, gd + D_MODEL:gd + D_MODEL + 2 * GDN_HEADS],
                             ((0, 0), (0, LANES - 2 * GDN_HEADS))).astype(BF16),
                 wxq=w_in[:, gd + D_MODEL + 2 * GDN_HEADS:].astype(BF16),
                 conv_w=full["gdn_conv_w"][j],
                 a_log8=jnp.broadcast_to(full["gdn_a_log"][j][:, None], (GDN_HEADS, LANES)),
                 dt_bias8=jnp.broadcast_to(full["gdn_dt_bias"][j][:, None], (GDN_HEADS, LANES)),
                 norm_g=full["gdn_norm_g"][j][None])
    else:
        w_in = full["s5_w_in"][j]
        (kx, ec, fc, a_t), prep_vjp = jax.vjp(
            _s5_prep, full["s5_a_re"][j], full["s5_a_im"][j], full["s5_b_re"][j], full["s5_b_im"][j],
            full["s5_c_re"][j], full["s5_c_im"][j], full["s5_log_dt"][j])
        w.update(wu=w_in[:, :D_MODEL].astype(BF16), wxq=w_in[:, D_MODEL:].astype(BF16),
                 kx=kx, ec=ec, fc=fc, a_t=a_t, prep_vjp=prep_vjp,
                 d=full["s5_d"][j][None], w_glu=full["s5_w_glu"][j].astype(BF16), b_glu=full["s5_b_glu"][j][None])
    return w


def _sharded_grads(l, i):
    m = l["mixer"]
    out = dict(w_kv_mem=l["wkv"], w_o=l["wo"], mlp_w1=l["w1"], mlp_w2=l["w2"])
    if i % 2 == 0:
        out.update(gdn_w_in=jnp.concatenate([m["wqkv"], m["wz"], m["wba"][:, :2 * GDN_HEADS], l["wxq"]], axis=1),
                   gdn_conv_w=m["conv_w"])
    else:
        out.update(s5_w_in=jnp.concatenate([m["wu"], l["wxq"]], axis=1), s5_d=m["d"], s5_w_glu=m["w_glu"],
                   s5_b_glu=m["b_glu"])
    return out


def _replicated_grads(layer_grads):
    g = layer_grads
    gdn = [g[i]["mixer"] for i in range(DEPTH) if i % 2 == 0]
    s5 = [g[i]["mixer"] for i in range(DEPTH) if i % 2 == 1]
    out = {n: jnp.stack([l[n] for l in g]) for n in ("ln1_g", "ln1_b", "ln2_g", "ln2_b")}
    out.update({"gdn_" + n: jnp.stack([m[n] for m in gdn]) for n in ("a_log", "dt_bias", "norm_g")})
    out.update({"s5_" + n: jnp.stack([m[n] for m in s5])
                for n in ("a_re", "a_im", "b_re", "b_im", "c_re", "c_im", "log_dt")})
    return out


def _local_step(x, mem, target, weights_of, grads_ready):
    lw, saves = [], []
    h = x
    for i in range(DEPTH):
        lw.append(weights_of(i, h))
        h, s = _layer_fwd(h, mem, lw[i], i % 2 == 0)
        saves.append(s)
    loss, d = _loss_and_grad(h, target)
    grads = [None] * DEPTH
    token = None
    for i in reversed(range(DEPTH)):
        d, grads[i] = _layer_bwd(mem, lw[i], i % 2 == 0, saves[i], d, token)
        token = grads_ready(i, grads[i])
    return loss, d, grads


ANY = pl.BlockSpec(memory_space=pl.ANY)
SHARD_ROWS = 1024
SMALL_ROWS = 128


def _place():
    return lax.axis_index("x"), lax.axis_index("y"), lax.axis_index("c")


def _other_chips(x, y):
    return [(1 - x, y), (x, 1 - y), (1 - x, 1 - y)]


def _all_gather_chips(wpack, *, name):
    rows = wpack.shape[0]
    half = rows // 2

    def body(w_ref, out_ref, send_sems, recv_sems):
        x, y, c = _place()
        sibling = (x, y, 1 - c)
        chips = _other_chips(x, y)

        def blk(cx, cy, cc):
            return out_ref.at[2 * cx + cy, pl.ds(cc * half, half), :]

        def copy(k, src, dst, to):
            return pltpu.make_async_remote_copy(src_ref=src, dst_ref=dst, send_sem=send_sems.at[k],
                                                recv_sem=recv_sems.at[k], device_id=to, device_id_type=MESH)

        first = [copy(j, w_ref.at[pl.ds(c * half, half), :], blk(x, y, c), (cx, cy, c))
                 for j, (cx, cy) in enumerate(chips)]
        for cp in first:
            cp.start()
        passed = [copy(3 + j, blk(cx, cy, c), blk(cx, cy, c), sibling) for j, (cx, cy) in enumerate(chips)]
        for j, (cx, cy) in enumerate(chips):
            copy(j, blk(cx, cy, c), blk(cx, cy, c), (cx, cy, c)).wait_recv()
            passed[j].start()
        for j, (cx, cy) in enumerate(chips):
            copy(3 + j, blk(cx, cy, 1 - c), blk(cx, cy, 1 - c), sibling).wait_recv()
        for cp in first + passed:
            cp.wait_send()

    return pl.pallas_call(
        body, name=name, out_shape=jax.ShapeDtypeStruct((N_CHIPS, rows, D_MODEL), wpack.dtype),
        in_specs=[ANY], out_specs=ANY,
        scratch_shapes=[pltpu.SemaphoreType.DMA((6,)), pltpu.SemaphoreType.DMA((6,))],
    )(wpack)


HBM = pl.BlockSpec(memory_space=pltpu.HBM)
SEM = pl.BlockSpec(memory_space=pltpu.SEMAPHORE)
DATAFLOW = pltpu.SideEffectType.DATAFLOW_SIDE_EFFECTING


def _gather_ici_copies(w_ref, land_ref, send_sems, recv_sems, outgoing):
    x, y, c = _place()
    half = w_ref.shape[0] // 2
    mine = pl.ds(c * half, half)
    return [pltpu.make_async_remote_copy(
        src_ref=w_ref.at[mine, :], dst_ref=land_ref.at[2 * x + y if outgoing else 2 * cx + cy, mine, :],
        send_sem=send_sems.at[j], recv_sem=recv_sems.at[j], device_id=(cx, cy, c), device_id_type=MESH)
        for j, (cx, cy) in enumerate(_other_chips(x, y))]


def _gather_start(wpack, after):
    rows = wpack.shape[0]

    def body(w_ref, land_ref, after_ref, send_sems, recv_sems, w_thru, land_thru, token):
        for cp in _gather_ici_copies(w_ref, land_ref, send_sems, recv_sems, outgoing=True):
            cp.start()
        token[...] = jnp.zeros_like(token)

    land = pltpu.with_memory_space_constraint(lax.empty((N_CHIPS, rows, D_MODEL), wpack.dtype), pltpu.HBM)
    return pl.pallas_call(
        body, name="gather_start",
        out_shape=(pltpu.SemaphoreType.DMA((3,)), pltpu.SemaphoreType.DMA((3,)), pltpu.HBM(wpack.shape, wpack.dtype),
                   pltpu.HBM(land.shape, land.dtype), jax.ShapeDtypeStruct((SUBLANES, LANES), F32)),
        in_specs=(HBM, HBM, ANY), out_specs=(SEM, SEM, HBM, HBM, pl.BlockSpec(memory_space=pltpu.VMEM)),
        input_output_aliases={0: 2, 1: 3},
        compiler_params=pltpu.CompilerParams(has_side_effects=DATAFLOW),
    )(pltpu.with_memory_space_constraint(wpack, pltpu.HBM), land, after)


def _gather_wait(send_sems, recv_sems, w_thru, land_thru, after):
    def body(w_ref, land_ref, send_sems, recv_sems, after_ref, w_dead, land_out):
        for cp in _gather_ici_copies(w_ref, land_ref, send_sems, recv_sems, outgoing=False):
            cp.wait_send()
            cp.wait_recv()

    return pl.pallas_call(
        body, name="gather_wait",
        out_shape=(pltpu.HBM(w_thru.shape, w_thru.dtype), pltpu.HBM(land_thru.shape, land_thru.dtype)),
        in_specs=(HBM, HBM, SEM, SEM, ANY), out_specs=(HBM, HBM), input_output_aliases={0: 0, 1: 1},
        compiler_params=pltpu.CompilerParams(has_side_effects=DATAFLOW),
    )(w_thru, land_thru, send_sems, recv_sems, after)[1]


def _gather_forward(land, *, name):
    rows = land.shape[1]
    half = rows // 2

    def body(in_ref, out_ref, send_sems, recv_sems):
        x, y, c = _place()

        def copy(j, cx, cy, cc):
            rows_of = out_ref.at[2 * cx + cy, pl.ds(cc * half, half), :]
            return pltpu.make_async_remote_copy(src_ref=rows_of, dst_ref=rows_of, send_sem=send_sems.at[j],
                                                recv_sem=recv_sems.at[j], device_id=(x, y, 1 - c), device_id_type=MESH)

        sends = [copy(j, cx, cy, c) for j, (cx, cy) in enumerate(_other_chips(x, y))]
        for cp in sends:
            cp.start()
        for j, (cx, cy) in enumerate(_other_chips(x, y)):
            copy(j, cx, cy, 1 - c).wait_recv()
        for cp in sends:
            cp.wait_send()

    return pl.pallas_call(
        body, name=name, out_shape=jax.ShapeDtypeStruct(land.shape, land.dtype), in_specs=[ANY], out_specs=ANY,
        input_output_aliases={0: 0},
        scratch_shapes=[pltpu.SemaphoreType.DMA((3,)), pltpu.SemaphoreType.DMA((3,))],
    )(land)


def _sibling_swap(buf, *, name):
    def body(in_ref, out_ref, send_sem, recv_sem):
        x, y, c = _place()
        cp = pltpu.make_async_remote_copy(src_ref=in_ref, dst_ref=out_ref, send_sem=send_sem, recv_sem=recv_sem,
                                          device_id=(x, y, 1 - c), device_id_type=MESH)
        cp.start()
        cp.wait()

    return pl.pallas_call(
        body, name=name, out_shape=jax.ShapeDtypeStruct(buf.shape, buf.dtype), in_specs=[ANY], out_specs=ANY,
        scratch_shapes=[pltpu.SemaphoreType.DMA, pltpu.SemaphoreType.DMA],
    )(buf)


def _pair_exchange(gpack, *, name):
    pieces, rows, width = gpack.shape
    half = rows // 2

    def body(in_ref, got_ref, send_sems, recv_sems):
        x, y, c = _place()
        sends = [pltpu.make_async_remote_copy(src_ref=in_ref.at[p, pl.ds((1 - c) * half, half), :],
                                              dst_ref=got_ref.at[p], send_sem=send_sems.at[p],
                                              recv_sem=recv_sems.at[p], device_id=(x, y, 1 - c), device_id_type=MESH)
                 for p in range(pieces)]
        for cp in sends:
            cp.start()
        for cp in sends:
            cp.wait()

    return pl.pallas_call(
        body, name=name, out_shape=jax.ShapeDtypeStruct((pieces, half, width), gpack.dtype),
        in_specs=[ANY], out_specs=ANY,
        scratch_shapes=[pltpu.SemaphoreType.DMA((pieces,)), pltpu.SemaphoreType.DMA((pieces,))],
    )(gpack)


def _pair_add(gpack, got, c, *, name, tm=512):
    pieces, rows, width = gpack.shape
    half = rows // 2
    nb = half // tm

    def body(c_ref, a_ref, b_ref, sum_ref, narrow_ref):
        s = a_ref[...] + b_ref[...]
        sum_ref[...] = s
        narrow_ref[...] = s.astype(BF16)

    blk = pl.BlockSpec((None, tm, width), lambda p, i, c_ref: (p, i, 0))
    return pl.pallas_call(
        body, name=name,
        out_shape=(jax.ShapeDtypeStruct((pieces, half, width), F32), jax.ShapeDtypeStruct((pieces, half, width), BF16)),
        grid_spec=pltpu.PrefetchScalarGridSpec(
            num_scalar_prefetch=1, grid=(pieces, nb),
            in_specs=[pl.BlockSpec((None, tm, width), lambda p, i, c_ref: (p, c_ref[0] * nb + i, 0)), blk],
            out_specs=(blk, blk)),
        compiler_params=_params(("parallel", "parallel")),
    )(c, gpack, got)


def _chip_exchange(pieces, *, name):
    _, rows, width = pieces.shape

    def body(in_ref, out_ref, send_sems, recv_sems):
        x, y, c = _place()
        cps = [pltpu.make_async_remote_copy(src_ref=in_ref.at[2 * cx + cy], dst_ref=out_ref.at[j],
                                            send_sem=send_sems.at[j], recv_sem=recv_sems.at[j],
                                            device_id=(cx, cy, c), device_id_type=MESH)
               for j, (cx, cy) in enumerate(_other_chips(x, y))]
        for cp in cps:
            cp.start()
        for cp in cps:
            cp.wait()

    return pl.pallas_call(
        body, name=name, out_shape=jax.ShapeDtypeStruct((3, rows, width), pieces.dtype), in_specs=[ANY], out_specs=ANY,
        scratch_shapes=[pltpu.SemaphoreType.DMA((3,)), pltpu.SemaphoreType.DMA((3,))],
    )(pieces)


def _chip_exchange_copies(in_ref, land_ref, send_sems, recv_sems):
    x, y, c = _place()
    return [pltpu.make_async_remote_copy(src_ref=in_ref.at[2 * cx + cy], dst_ref=land_ref.at[j],
                                         send_sem=send_sems.at[j], recv_sem=recv_sems.at[j],
                                         device_id=(cx, cy, c), device_id_type=MESH)
            for j, (cx, cy) in enumerate(_other_chips(x, y))]


def _chip_exchange_start(pieces):
    _, rows, width = pieces.shape

    def body(in_ref, land_ref, send_sems, recv_sems, in_thru, land_thru, token):
        for cp in _chip_exchange_copies(in_ref, land_ref, send_sems, recv_sems):
            cp.start()
        token[...] = jnp.zeros_like(token)

    land = pltpu.with_memory_space_constraint(lax.empty((3, rows, width), pieces.dtype), pltpu.HBM)
    return pl.pallas_call(
        body, name="rs_chip_start",
        out_shape=(pltpu.SemaphoreType.DMA((3,)), pltpu.SemaphoreType.DMA((3,)), pltpu.HBM(pieces.shape, pieces.dtype),
                   pltpu.HBM(land.shape, land.dtype), jax.ShapeDtypeStruct((SUBLANES, LANES), F32)),
        in_specs=(HBM, HBM), out_specs=(SEM, SEM, HBM, HBM, pl.BlockSpec(memory_space=pltpu.VMEM)),
        input_output_aliases={0: 2, 1: 3},
        compiler_params=pltpu.CompilerParams(has_side_effects=DATAFLOW),
    )(pltpu.with_memory_space_constraint(pieces, pltpu.HBM), land)


def _chip_exchange_wait(send_sems, recv_sems, in_thru, land_thru, after):
    def body(in_ref, land_ref, send_sems, recv_sems, after_ref, in_dead, land_out):
        for cp in _chip_exchange_copies(in_ref, land_ref, send_sems, recv_sems):
            cp.wait_send()
            cp.wait_recv()

    return pl.pallas_call(
        body, name="rs_chip_wait",
        out_shape=(pltpu.HBM(in_thru.shape, in_thru.dtype), pltpu.HBM(land_thru.shape, land_thru.dtype)),
        in_specs=(HBM, HBM, SEM, SEM, ANY), out_specs=(HBM, HBM), input_output_aliases={0: 0, 1: 1},
        compiler_params=pltpu.CompilerParams(has_side_effects=DATAFLOW),
    )(in_thru, land_thru, send_sems, recv_sems, after)[1]


def _all_reduce_small(v, *, name):
    rows, width = v.shape

    def body(in_ref, out_ref, gath, send_sems, recv_sems):
        x, y, c = _place()
        me = 4 * x + 2 * y + c
        gath[me] = in_ref[...]
        peers = []
        for m in range(1, N_DEV):
            px = 1 - x if m & 4 else x
            py = 1 - y if m & 2 else y
            pc = 1 - c if m & 1 else c
            peers.append((m - 1, (px, py, pc), 4 * px + 2 * py + pc))
        for k, peer, _ in peers:
            pltpu.make_async_remote_copy(src_ref=in_ref, dst_ref=gath.at[me], send_sem=send_sems.at[k],
                                         recv_sem=recv_sems.at[k], device_id=peer, device_id_type=MESH).start()
        for k, peer, plin in peers:
            cp = pltpu.make_async_remote_copy(src_ref=in_ref, dst_ref=gath.at[plin], send_sem=send_sems.at[k],
                                              recv_sem=recv_sems.at[k], device_id=peer, device_id_type=MESH)
            cp.wait_send()
            cp.wait_recv()
        acc = gath[0]
        for d in range(1, N_DEV):
            acc = acc + gath[d]
        out_ref[...] = acc

    vmem = pl.BlockSpec(memory_space=pltpu.VMEM)
    return pl.pallas_call(
        body, name=name, out_shape=jax.ShapeDtypeStruct(v.shape, v.dtype), in_specs=[vmem], out_specs=vmem,
        scratch_shapes=[pltpu.VMEM((N_DEV, rows, width), v.dtype),
                        pltpu.SemaphoreType.DMA((N_DEV - 1,)), pltpu.SemaphoreType.DMA((N_DEV - 1,))],
        compiler_params=pltpu.CompilerParams(vmem_limit_bytes=VMEM_LIMIT_V7X),
    )(v)


def _reduce_scatter_begin(gpack, behind):
    x, y, c = _place()
    got = _pair_exchange(gpack, name="rs_pair_swap")
    pair, pair16 = _pair_add(gpack, got, c.astype(jnp.int32).reshape(1), name="rs_pair_add")
    mine = lax.dynamic_index_in_dim(pair, 2 * x + y, axis=0, keepdims=False)
    if behind:
        *in_flight, token = _chip_exchange_start(pair16)
        return dict(mine=mine, in_flight=in_flight), token
    return dict(mine=mine, recv=_chip_exchange(pair16, name="rs_chip_exchange")), None


def _reduce_scatter_end(state, after=None):
    c = lax.axis_index("c")
    recv = state["recv"] if "recv" in state else _chip_exchange_wait(*state["in_flight"], after=after)
    (total,) = _rw_fwd(_f_add4, [state["mine"], recv[0], recv[1], recv[2]], [], tm=512, name="rs_chip_add")
    theirs = _sibling_swap(total, name="rs_share_swap")
    return jnp.concatenate([jnp.where(c == 0, total, theirs), jnp.where(c == 0, theirs, total)], axis=0)


_SHARDED = (("w_kv_mem", 1), ("w_o", 1), ("mlp_w1", 2), ("mlp_w2", 1), ("gdn_w_in", 2), ("gdn_conv_w", 2),
            ("s5_w_in", 2), ("s5_d", 1), ("s5_w_glu", 1), ("s5_b_glu", 1))
_MATMUL_ONLY = ("w_kv_mem", "w_o", "mlp_w1", "mlp_w2", "gdn_w_in", "s5_w_in", "s5_w_glu")
REDUCED_FIRST = (2, 3)
_KEPT_BLOCKED = ("mlp_w1",)
_REPLICATED = ("ln1_g", "ln1_b", "ln2_g", "ln2_b", "gdn_a_log", "gdn_dt_bias", "gdn_norm_g", "s5_a_re", "s5_a_im",
               "s5_b_re", "s5_b_im", "s5_c_re", "s5_c_im", "s5_log_dt")
_WEIGHTS = ("w_kv_mem", "w_o", "ln1_g", "ln1_b", "ln2_g", "ln2_b", "mlp_w1", "mlp_w2", "gdn_w_in", "gdn_conv_w",
            "gdn_a_log", "gdn_dt_bias", "gdn_norm_g", "s5_w_in", "s5_a_re", "s5_a_im", "s5_b_re", "s5_b_im",
            "s5_c_re", "s5_c_im", "s5_log_dt", "s5_d", "s5_w_glu", "s5_b_glu")


ROW_ALIGN = 16


def _n_rows(shape):
    return -(-math.prod(shape) // (ROW_ALIGN * D_MODEL)) * ROW_ALIGN


def _as_rows(a):
    rows = _n_rows(a.shape)
    if a.shape[-1] == D_MODEL and a.size == rows * D_MODEL:
        return a.reshape(-1, D_MODEL)
    flat = a.reshape(-1)
    return jnp.pad(flat, (0, rows * D_MODEL - flat.size)).reshape(rows, D_MODEL)


def _pack(arrs, unit_rows=SHARD_ROWS):
    rows = [_as_rows(a) for a in arrs]
    pad = -sum(r.shape[0] for r in rows) % unit_rows
    if pad:
        rows.append(jnp.zeros((pad, D_MODEL), rows[0].dtype))
    return jnp.concatenate(rows, axis=0)


def _unpack(packed, shapes):
    lead = packed.shape[:-2]
    out, off = [], 0
    for s in shapes:
        r = _n_rows(s)
        seg = lax.slice_in_dim(packed, off, off + r, axis=len(lead))
        if s[-1] != D_MODEL or math.prod(s) != r * D_MODEL:
            seg = lax.slice_in_dim(seg.reshape(lead + (-1,)), 0, math.prod(s), axis=len(lead))
        out.append(seg.reshape(lead + tuple(s)))
        off += r
    return out


def _split3(t):
    hi = t.astype(BF16)
    r1 = t - hi.astype(F32)
    mid = r1.astype(BF16)
    lo = (r1 - mid.astype(F32)).astype(BF16)
    return jnp.stack([hi, mid, lo], axis=-1)


def _join3(t):
    return (t[..., 0].astype(F32) + t[..., 1].astype(F32)) + t[..., 2].astype(F32)


def _merge_chips(blocks, axis):
    return jnp.concatenate([blocks[s] for s in range(N_CHIPS)], axis=axis)


def _pack_for_chips(weights):
    rows = []
    for s in range(N_CHIPS):
        chip = []
        for layers, axis in weights:
            if axis is None:
                blocks = [g[s] for g in layers]
            else:
                n = layers[0].shape[axis] // N_CHIPS
                blocks = [lax.slice_in_dim(g, s * n, (s + 1) * n, axis=axis) for g in layers]
            if math.prod(blocks[0].shape) % (ROW_ALIGN * D_MODEL) == 0:
                chip += [_as_rows(b) for b in blocks]
            else:
                chip.append(_as_rows(jnp.stack(blocks)))
        pad = -sum(r.shape[0] for r in chip) % SHARD_ROWS
        rows += chip + ([jnp.zeros((pad, D_MODEL), F32)] if pad else [])
    return jnp.concatenate(rows, axis=0).reshape(N_CHIPS, -1, D_MODEL)


def kernel(x, mem, w_kv_mem, w_o, ln1_g, ln1_b, ln2_g, ln2_b, mlp_w1, mlp_w2, gdn_w_in, gdn_conv_w, gdn_a_log, gdn_dt_bias, gdn_norm_g, s5_w_in, s5_a_re, s5_a_im, s5_b_re, s5_b_im, s5_c_re, s5_c_im, s5_log_dt, s5_d, s5_w_glu, s5_b_glu, loss_target, m_w_kv_mem, m_w_o, m_ln1_g, m_ln1_b, m_ln2_g, m_ln2_b, m_mlp_w1, m_mlp_w2, m_gdn_w_in, m_gdn_conv_w, m_gdn_a_log, m_gdn_dt_bias, m_gdn_norm_g, m_s5_w_in, m_s5_a_re, m_s5_a_im, m_s5_b_re, m_s5_b_im, m_s5_c_re, m_s5_c_im, m_s5_log_dt, m_s5_d, m_s5_w_glu, m_s5_b_glu, v_w_kv_mem, v_w_o, v_ln1_g, v_ln1_b, v_ln2_g, v_ln2_b, v_mlp_w1, v_mlp_w2, v_gdn_w_in, v_gdn_conv_w, v_gdn_a_log, v_gdn_dt_bias, v_gdn_norm_g, v_s5_w_in, v_s5_a_re, v_s5_a_im, v_s5_b_re, v_s5_b_im, v_s5_c_re, v_s5_c_im, v_s5_log_dt, v_s5_d, v_s5_w_glu, v_s5_b_glu):
    given = dict(locals())
    w = {n: given[n] for n in _WEIGHTS}
    mom = {n: given["m_" + n] for n in _WEIGHTS}
    var = {n: given["v_" + n] for n in _WEIGHTS}
    shard_names = [n for n, _ in _SHARDED]
    shard_shapes = [w[n].shape for n in shard_names]
    rep_shapes = [w[n].shape for n in _REPLICATED]

    wire = {n: w[n].astype(BF16) if n in _MATMUL_ONLY else _split3(w[n]) for n in shard_names}
    first = {n: 0 if n.startswith("s5_") else 1 for n in shard_names}
    me_chip = 2 * lax.axis_index("x") + lax.axis_index("y")
    early = [wire[n][:first[n]] for n in shard_names if first[n]]
    late = [wire[n][first[n]:] for n in shard_names]
    early_pack, late_pack = _pack(early), _pack(late)
    landed = _all_gather_chips(early_pack, name="gather_first_layer")
    landed = lax.dynamic_update_index_in_dim(landed, early_pack, me_chip, axis=0)
    early_blocks = dict(zip([n for n in shard_names if first[n]], _unpack(landed, [a.shape for a in early])))
    send_sems, recv_sems, pack_thru, land_thru, token = _gather_start(late_pack, after=landed)
    axis_of = dict(_SHARDED)

    def merged(n, blk):
        if n in _KEPT_BLOCKED:
            return blk
        return _merge_chips(blk if n in _MATMUL_ONLY else _join3(blk), axis_of[n] - 1)

    late_full = {}

    def weights_of(i, h):
        if i == 0:
            full = {n: [merged(n, blk[:, 0])] for n, blk in early_blocks.items()}
            full["gdn_w_in"][0] = full["gdn_w_in"][0] + token[0, 0].astype(BF16)
        else:
            if not late_full:
                land = _gather_wait(send_sems, recv_sems, pack_thru, land_thru, after=h)
                land = _gather_forward(land, name="gather_forward")
                land = lax.dynamic_update_index_in_dim(land, late_pack, me_chip, axis=0)
                for n, blk in zip(shard_names, _unpack(land, [a.shape for a in late])):
                    late_full[n] = [None] * first[n] + [merged(n, blk[:, t]) for t in range(blk.shape[1])]
            full = dict(late_full)
        full.update({n: w[n] for n in _REPLICATED})
        return _layer_weights(full, i)

    sharded = {}
    in_flight = {}

    def group_pack(layers):
        names = [n for n in shard_names if any(n in sharded[i] for i in layers)]
        per_weight = [[sharded[i][n] for i in layers if n in sharded[i]] for n in names]
        pack = _pack_for_chips([(g, None if n in _KEPT_BLOCKED else axis_of[n] - 1) for n, g in zip(names, per_weight)])
        return pack, names, [(len(g),) + w[n].shape[1:] for n, g in zip(names, per_weight)]

    def grads_ready(i, g):
        sharded[i] = _sharded_grads(g, i)
        if i != REDUCED_FIRST[0]:
            return None
        pack, names, shapes = group_pack(REDUCED_FIRST)
        state, token = _reduce_scatter_begin(pack, behind=True)
        in_flight.update(state=state, names=names, shapes=shapes)
        return token

    loss, grad_x, layer_grads = _local_step(x[0], mem[0], loss_target[0], weights_of, grads_ready)
    loss = lax.psum(loss, ("x", "y", "c"))
    rest = [i for i in range(DEPTH) if i not in REDUCED_FIRST]
    pack, names, shapes = group_pack(rest)
    state, _ = _reduce_scatter_begin(pack, behind=False)
    pieces = {n: [] for n in shard_names}
    for n, g in zip(names, _unpack(_reduce_scatter_end(state), shapes)):
        pieces[n].append(g)
    late = _reduce_scatter_end(in_flight["state"], after=grad_x)
    for n, g in zip(in_flight["names"], _unpack(late, in_flight["shapes"])):
        pieces[n].append(g)
    g_shards = [p[0] if len(p) == 1 else jnp.concatenate(p, axis=0) for p in (pieces[n] for n in shard_names)]

    def pack_small(d):
        return _pack([d[n] for n in _REPLICATED], unit_rows=SMALL_ROWS)

    g_rep = _all_reduce_small(pack_small(_replicated_grads(layer_grads)), name="reduce_replicated")

    def adamw(wp, gp, mp, vp, name):
        return _rw_fwd(_f_adamw, [wp, gp, mp, vp], [], tm=256, name=name)

    outs = {}
    for n, g in zip(shard_names, g_shards):
        flat = (-1, w[n].shape[-1])
        res = adamw(w[n].reshape(flat), g.reshape(flat), mom[n].reshape(flat), var[n].reshape(flat), "adamw_" + n)
        outs[("grad", n)] = g
        outs.update({(kind, n): a.reshape(w[n].shape) for kind, a in zip(("delta", "new_m", "new_v"), res)})
    packed = (g_rep,) + tuple(adamw(pack_small(w), g_rep, pack_small(mom), pack_small(var), "adamw_replicated"))
    for kind, pr in zip(("grad", "delta", "new_m", "new_v"), packed):
        outs.update({(kind, n): a for n, a in zip(_REPLICATED, _unpack(pr, rep_shapes))})
    return (loss, grad_x[None]) + tuple(outs[(kind, n)] for kind in ("grad", "delta", "new_m", "new_v")
                                        for n in _WEIGHTS)
```

```python
import functools
import math

import jax
import jax.numpy as jnp
from jax import lax
from jax.experimental import pallas as pl
from jax.experimental.pallas import tpu as pltpu

F32 = jnp.float32
BF16 = jnp.bfloat16
MESH = pl.DeviceIdType.MESH

D_MODEL = 1024
DEPTH = 4
GDN_HEADS = 8
HEAD_DIM = 128
GDN_CONV = 4
GDN_CHUNK = 64
S5_GROUPS = 64
S5_GROUP = 16
S5_STATE = 64
XA_HEADS = 4
XA_DIM = 512
D_FF = 4096
DN_ALPHA = (2 * DEPTH) ** 0.25
LN_EPS = 1e-5
RMS_EPS = 1e-6
ADAM_LR, ADAM_B1, ADAM_B2, ADAM_EPS, ADAM_WD, ADAM_STEP = 0.001, 0.9, 0.999, 1e-08, 0.01, 10

VMEM_LIMIT_V7X = 56 * 1024 * 1024
LANES = 128
SUBLANES = 8
S5_T = 16
S5_TILES = D_MODEL // LANES
N_CHIPS = 4
N_DEV = 8


def _params(sem):
    return pltpu.CompilerParams(dimension_semantics=sem, vmem_limit_bytes=VMEM_LIMIT_V7X)


def _tile(n, pref):
    if n <= pref:
        return n
    t = (pref // LANES) * LANES
    while n % t:
        t -= LANES
    return t


def _row_tile(n, pref):
    if n % SUBLANES:
        return n
    t = min(pref, n) // SUBLANES * SUBLANES
    while n % t:
        t -= SUBLANES
    return t


def _col_blocked_spec(rows_tile, cols_tile, block_cols, rows_axis, cols_axis):
    r = block_cols // cols_tile

    def index(*ijk):
        c = ijk[cols_axis]
        return (c, ijk[rows_axis], 0) if r == 1 else (c // r, ijk[rows_axis], c % r)

    return pl.BlockSpec((None, rows_tile, cols_tile), index)


def _mm(a, b, *, ta=False, tb=False, acc=None, name, tm=1024, tn=1024, tk=1024, out_blocks=0):
    k_dim, m_dim = a.shape if ta else a.shape[::-1]
    b_rows, b_cols = (b.shape[0], b.shape[1]) if b.ndim == 2 else (b.shape[1], b.shape[0] * b.shape[2])
    n_dim = b_rows if tb else b_cols
    assert (b_cols if tb else b_rows) == k_dim, (a.shape, b.shape, ta, tb)
    limit_n = n_dim // out_blocks if out_blocks else (n_dim if b.ndim == 2 or tb else b.shape[2])
    limit_k = b.shape[2] if (b.ndim == 3 and tb) else k_dim
    tm, tn, tk = _tile(m_dim, tm), _tile(limit_n, min(tn, limit_n)), _tile(limit_k, min(tk, limit_k))
    a_spec = (pl.BlockSpec((tk, tm), lambda i, j, k: (k, i)) if ta else pl.BlockSpec((tm, tk), lambda i, j, k: (i, k)))
    if b.ndim == 3:
        b_spec = (_col_blocked_spec(tn, tk, b.shape[2], 1, 2) if tb else _col_blocked_spec(tk, tn, b.shape[2], 2, 1))
    else:
        b_spec = (pl.BlockSpec((tn, tk), lambda i, j, k: (j, k)) if tb
                  else pl.BlockSpec((tk, tn), lambda i, j, k: (k, j)))
    o_spec = (_col_blocked_spec(tm, tn, n_dim // out_blocks, 0, 1) if out_blocks
              else pl.BlockSpec((tm, tn), lambda i, j, k: (i, j)))
    o_shape = (out_blocks, m_dim, n_dim // out_blocks) if out_blocks else (m_dim, n_dim)
    dn = (((0 if ta else 1,), (1 if tb else 0,)), ((), ()))
    has_acc = acc is not None

    def body(*refs):
        a_ref, b_ref = refs[0], refs[1]
        o_ref = refs[-1]
        k = pl.program_id(2)
        p = lax.dot_general(a_ref[...].astype(BF16), b_ref[...].astype(BF16), dn,
                            preferred_element_type=F32)

        @pl.when(k == 0)
        def _():
            o_ref[...] = p + refs[2][...] if has_acc else p

        @pl.when(k > 0)
        def _():
            o_ref[...] += p

    return pl.pallas_call(
        body, name=name,
        out_shape=jax.ShapeDtypeStruct(o_shape, F32),
        grid=(m_dim // tm, n_dim // tn, k_dim // tk),
        in_specs=[a_spec, b_spec] + ([o_spec] if has_acc else []),
        out_specs=o_spec,
        compiler_params=_params(("parallel", "parallel", "arbitrary")),
    )(*([a, b] + ([acc] if has_acc else [])))


def _mm_relu2(a, b, *, name, tm=1024):
    m_dim, k_dim = a.shape
    n_blocks, _, tn = b.shape
    n_dim = n_blocks * tn
    tm = _tile(m_dim, tm)

    def body(a_ref, b_ref, h_ref, act_ref):
        h = jnp.dot(a_ref[...].astype(BF16), b_ref[...].astype(BF16), preferred_element_type=F32)
        h_ref[...] = h
        r = jnp.maximum(h, 0.0)
        act_ref[...] = (r * r).astype(BF16)

    o_spec = pl.BlockSpec((tm, tn), lambda i, j: (i, j))
    return pl.pallas_call(
        body, name=name,
        out_shape=(jax.ShapeDtypeStruct((m_dim, n_dim), F32), jax.ShapeDtypeStruct((m_dim, n_dim), BF16)),
        grid=(m_dim // tm, n_dim // tn),
        in_specs=[pl.BlockSpec((tm, k_dim), lambda i, j: (i, 0)),
                  pl.BlockSpec((None, k_dim, tn), lambda i, j: (j, 0, 0))],
        out_specs=(o_spec, o_spec),
        compiler_params=_params(("parallel", "parallel")),
    )(a, b)


def _mm_relu2_grad(d, b, h, *, name, tm=1024, tn=1024):
    m_dim, k_dim = d.shape
    n_dim = b.shape[0]
    tm, tn = _tile(m_dim, tm), _tile(n_dim, tn)

    def body(d_ref, b_ref, h_ref, o_ref):
        p = lax.dot_general(d_ref[...].astype(BF16), b_ref[...].astype(BF16), ((NT), ((), ())),
                            preferred_element_type=F32)
        o_ref[...] = (p * (2.0 * jnp.maximum(h_ref[...], 0.0))).astype(BF16)

    o_spec = pl.BlockSpec((tm, tn), lambda i, j: (i, j))
    return pl.pallas_call(
        body, name=name,
        out_shape=jax.ShapeDtypeStruct((m_dim, n_dim), BF16),
        grid=(m_dim // tm, n_dim // tn),
        in_specs=[pl.BlockSpec((tm, k_dim), lambda i, j: (i, 0)), pl.BlockSpec((tn, k_dim), lambda i, j: (j, 0)), o_spec],
        out_specs=o_spec,
        compiler_params=_params(("parallel", "parallel")),
    )(d, b, h)


def _rowwise(f, rows, params, row_out, acc_out, *, tm, name):
    length = rows[0].shape[0]
    tm = _row_tile(length, tm)
    nr, npar, nro = len(rows), len(params), len(row_out)

    def body(*refs):
        ins = [r[...] for r in refs[:nr + npar]]
        outs = refs[nr + npar:]
        r_o, a_o = f(*ins)
        for ref, val in zip(outs[:nro], r_o):
            ref[...] = val.astype(ref.dtype)
        i = pl.program_id(0)
        for ref, val in zip(outs[nro:], a_o):
            @pl.when(i == 0)
            def _(ref=ref, val=val):
                ref[...] = val.astype(ref.dtype)

            @pl.when(i > 0)
            def _(ref=ref, val=val):
                ref[...] += val.astype(ref.dtype)

    in_specs = ([pl.BlockSpec((tm, r.shape[1]), lambda i: (i, 0)) for r in rows]
                + [pl.BlockSpec(p.shape, lambda i: (0, 0)) for p in params])
    out_specs = ([pl.BlockSpec((tm, w), lambda i: (i, 0)) for w, _ in row_out]
                 + [pl.BlockSpec(s, lambda i: (0, 0)) for s, _ in acc_out])
    out_shape = ([jax.ShapeDtypeStruct((length, w), dt) for w, dt in row_out]
                 + [jax.ShapeDtypeStruct(s, dt) for s, dt in acc_out])
    res = pl.pallas_call(
        body, name=name, out_shape=out_shape, grid=(length // tm,),
        in_specs=in_specs, out_specs=out_specs,
        compiler_params=_params(("arbitrary",) if acc_out else ("parallel",)),
    )(*rows, *params)
    return res[:nro], res[nro:]


def _rw_fwd(f, rows, params, *, tm, name, out_dtypes=None):
    tm_ = _row_tile(rows[0].shape[0], tm)
    shapes = jax.eval_shape(f, *[jax.ShapeDtypeStruct((tm_, r.shape[1]), r.dtype) for r in rows],
                            *[jax.ShapeDtypeStruct(p.shape, p.dtype) for p in params])
    row_out = [(s.shape[1], s.dtype if out_dtypes is None else dt)
               for s, dt in zip(shapes, out_dtypes or shapes)]
    outs, _ = _rowwise(lambda *v: (f(*v), ()), rows, params, row_out, [], tm=tm, name=name)
    return outs


def _rw_bwd(f, rows, params, cots, *, row_grad, param_grad, tm, name, row_dtypes=None):
    nr, npar, nct = len(rows), len(params), len(cots)

    def g(*vals):
        prim = vals[:nr] + vals[nr + nct:]
        ct = vals[nr:nr + nct]
        _, vjp = jax.vjp(f, *prim)
        grads = vjp(tuple(ct))
        return (tuple(grads[i] for i in range(nr) if row_grad[i]),
                tuple(grads[nr + i] for i in range(npar) if param_grad[i]))

    widths = [rows[i].shape[1] for i in range(nr) if row_grad[i]]
    row_out = list(zip(widths, row_dtypes or [F32] * len(widths)))
    acc_out = [(params[i].shape, F32) for i in range(npar) if param_grad[i]]
    return _rowwise(g, list(rows) + list(cots), params, row_out, acc_out, tm=tm, name=name)


def _f_ln_res(x, h, g, b):
    pre = DN_ALPHA * x + h
    mu = jnp.mean(pre, axis=-1, keepdims=True)
    d = pre - mu
    var = jnp.mean(d * d, axis=-1, keepdims=True)
    return (d * lax.rsqrt(var + LN_EPS) * g + b,)


def _silu(t):
    return t * jax.nn.sigmoid(t)


def _f_gdn_qkv(c):
    a = _silu(c)
    outs = []
    for part, scale in ((0, HEAD_DIM ** -0.5), (1, 1.0)):
        heads = []
        for h in range(GDN_HEADS):
            t = a[:, part * D_MODEL + h * HEAD_DIM: part * D_MODEL + (h + 1) * HEAD_DIM]
            t = t * lax.rsqrt(jnp.sum(t * t, axis=-1, keepdims=True) + 1e-6)
            heads.append(t * scale if scale != 1.0 else t)
        outs.append(jnp.concatenate(heads, axis=-1))
    outs.append(a[:, 2 * D_MODEL:])
    return tuple(outs)


def _f_gdn_out(o, z, norm_g):
    heads = []
    for h in range(GDN_HEADS):
        t = o[:, h * HEAD_DIM:(h + 1) * HEAD_DIM]
        t = t * lax.rsqrt(jnp.mean(t * t, axis=-1, keepdims=True) + RMS_EPS) * norm_g
        heads.append(t)
    return (jnp.concatenate(heads, axis=-1) * _silu(z),)


def _f_attn(xq, kmem, vmem):
    heads = []
    for h in range(XA_HEADS):
        sl = slice(h * HEAD_DIM, (h + 1) * HEAD_DIM)
        s = lax.dot_general(xq[:, sl].astype(BF16), kmem[:, sl].astype(BF16),
                            (((1,), (1,)), ((), ())), preferred_element_type=F32) * (HEAD_DIM ** -0.5)
        m = lax.stop_gradient(jnp.max(s, axis=-1, keepdims=True))
        e = jnp.exp(s - m)
        p = e / jnp.sum(e, axis=-1, keepdims=True)
        heads.append(jnp.dot(p.astype(BF16), vmem[:, sl].astype(BF16), preferred_element_type=F32))
    return (jnp.concatenate(heads, axis=-1),)


def _f_s5_gelu(y, u, d):
    return (jax.nn.gelu(y + d * u),)


def _f_s5_gate(zg, t, b):
    return (zg * jax.nn.sigmoid(t + b),)


def _f_add(a, b):
    return (a + b,)


def _f_add4(a, b, c, d):
    return (((a + b.astype(F32)) + c.astype(F32)) + d.astype(F32),)


def _f_adamw(w, g, m, v):
    m = ADAM_B1 * m + (1.0 - ADAM_B1) * g
    v = ADAM_B2 * v + (1.0 - ADAM_B2) * jnp.square(g)
    m_hat = m / (1.0 - ADAM_B1 ** ADAM_STEP)
    v_hat = v / (1.0 - ADAM_B2 ** ADAM_STEP)
    delta = -ADAM_LR * (m_hat / (jnp.sqrt(v_hat) + ADAM_EPS) + ADAM_WD * w)
    return delta, m, v


def _conv_fwd(u, w, *, tm, name):
    length, chans = u.shape
    tm = min(tm, length)
    tc = _tile(chans, 1024)
    hb = tm // SUBLANES

    def body(cur_ref, prev_ref, w_ref, o_ref, buf):
        i = pl.program_id(1)
        buf[0:SUBLANES, :] = jnp.where(i > 0, prev_ref[...], 0.0)
        buf[SUBLANES:, :] = cur_ref[...]
        acc = buf[pl.ds(SUBLANES - 3, tm), :] * w_ref[0:1, :]
        for k in range(1, GDN_CONV):
            acc = acc + buf[pl.ds(SUBLANES - 3 + k, tm), :] * w_ref[k:k + 1, :]
        o_ref[...] = acc

    return pl.pallas_call(
        body, name=name, out_shape=jax.ShapeDtypeStruct(u.shape, F32),
        grid=(chans // tc, length // tm),
        in_specs=[pl.BlockSpec((tm, tc), lambda j, i: (i, j)),
                  pl.BlockSpec((SUBLANES, tc), lambda j, i: (jnp.maximum(i * hb - 1, 0), j)),
                  pl.BlockSpec((GDN_CONV, tc), lambda j, i: (0, j))],
        out_specs=pl.BlockSpec((tm, tc), lambda j, i: (i, j)),
        scratch_shapes=[pltpu.VMEM((tm + SUBLANES, tc), F32)],
        compiler_params=_params(("parallel", "parallel")),
    )(u, u, w)


def _conv_bwd(u, w, dc, *, tm, name):
    length, chans = u.shape
    tm = min(tm, length)
    tc = _tile(chans, 1024)
    hb = tm // SUBLANES
    last = length // tm - 1

    def body(u_ref, uprev_ref, dc_ref, dcnext_ref, w_ref, du_ref, dw_ref, ubuf, dbuf):
        i = pl.program_id(1)
        ubuf[0:SUBLANES, :] = jnp.where(i > 0, uprev_ref[...], 0.0)
        ubuf[SUBLANES:, :] = u_ref[...]
        dbuf[0:tm, :] = dc_ref[...]
        dbuf[tm:, :] = jnp.where(i < last, dcnext_ref[...], 0.0)
        dcv = dc_ref[...]
        du = dbuf[pl.ds(3, tm), :] * w_ref[0:1, :]
        rows = [jnp.sum(dcv * ubuf[pl.ds(SUBLANES - 3, tm), :], axis=0, keepdims=True)]
        for k in range(1, GDN_CONV):
            du = du + dbuf[pl.ds(3 - k, tm), :] * w_ref[k:k + 1, :]
            rows.append(jnp.sum(dcv * ubuf[pl.ds(SUBLANES - 3 + k, tm), :], axis=0, keepdims=True))
        du_ref[...] = du.astype(du_ref.dtype)
        dwv = jnp.concatenate(rows, axis=0)

        @pl.when(i == 0)
        def _():
            dw_ref[...] = dwv

        @pl.when(i > 0)
        def _():
            dw_ref[...] += dwv

    return pl.pallas_call(
        body, name=name,
        out_shape=(jax.ShapeDtypeStruct(u.shape, BF16), jax.ShapeDtypeStruct((GDN_CONV, chans), F32)),
        grid=(chans // tc, length // tm),
        in_specs=[pl.BlockSpec((tm, tc), lambda j, i: (i, j)),
                  pl.BlockSpec((SUBLANES, tc), lambda j, i: (jnp.maximum(i * hb - 1, 0), j)),
                  pl.BlockSpec((tm, tc), lambda j, i: (i, j)),
                  pl.BlockSpec((SUBLANES, tc), lambda j, i: (jnp.minimum((i + 1) * hb, (last + 1) * hb - 1), j)),
                  pl.BlockSpec((GDN_CONV, tc), lambda j, i: (0, j))],
        out_specs=(pl.BlockSpec((tm, tc), lambda j, i: (i, j)),
                   pl.BlockSpec((GDN_CONV, tc), lambda j, i: (0, j))),
        scratch_shapes=[pltpu.VMEM((tm + SUBLANES, tc), F32), pltpu.VMEM((tm + SUBLANES, tc), F32)],
        compiler_params=_params(("parallel", "arbitrary")),
    )(u, u, dc, dc, w)


def _dot(a, b, dims, precision=None):
    if precision is None:
        a, b = a.astype(BF16), b.astype(BF16)
    return lax.dot_general(a, b, (dims, ((), ())), preferred_element_type=F32, precision=precision)


def _dot3(a, b, dims):
    ah, bh = a.astype(BF16), b.astype(BF16)
    al, bl = (a - ah.astype(F32)).astype(BF16), (b - bh.astype(F32)).astype(BF16)

    def d(x, y):
        return lax.dot_general(x, y, (dims, ((), ())), preferred_element_type=F32)

    return d(ah, bh) + (d(ah, bl) + d(al, bh))


NN = ((1,), (0,))
NT = ((1,), (1,))
TN = ((0,), (0,))
HI = lax.Precision.HIGHEST


def _hmap(f, *lists):
    return [f(*t) for t in zip(*lists)]


@jax.custom_vjp
def _unit_lower_inverse(a):
    c = a[0].shape[0]
    eye = (lax.broadcasted_iota(jnp.int32, (c, c), 0) == lax.broadcasted_iota(jnp.int32, (c, c), 1)).astype(F32)
    p = _hmap(lambda x: -x, a)
    t = _hmap(lambda x: eye + x, p)
    for _ in range(int(math.log2(c)) - 1):
        p = _hmap(lambda x: _dot3(x, x, NN), p)
        t = _hmap(lambda x, y: x + _dot3(x, y, NN), t, p)
    return t


def _uli_fwd(a):
    t = _unit_lower_inverse(a)
    return t, t


def _uli_bwd(t, dt):
    left = _hmap(lambda x, y: _dot3(x, y, TN), t, dt)
    return (_hmap(lambda x, y: -_dot3(x, y, NT), left, t),)


_unit_lower_inverse.defvjp(_uli_fwd, _uli_bwd)


def _gdn_chunk(q, k, v, bl, al, a_log, dt_bias, state):
    c = q[0].shape[0]
    row = lax.broadcasted_iota(jnp.int32, (c, c), 0)
    col = lax.broadcasted_iota(jnp.int32, (c, c), 1)
    causal = row >= col
    strict = row > col
    eye = (row == col).astype(F32)
    beta = _hmap(jax.nn.sigmoid, bl)
    g = _hmap(lambda a_, l_, d_: -jnp.exp(a_) * jax.nn.softplus(l_ + d_), a_log, al, dt_bias)
    g_r = _hmap(lambda x: jnp.sum(eye * x, axis=0, keepdims=True), g)
    gc = _hmap(lambda x: jnp.sum(jnp.where(causal, x, 0.0), axis=1, keepdims=True), g_r)
    gc_r = _hmap(lambda x: jnp.sum(jnp.where(row <= col, x, 0.0), axis=0, keepdims=True), g)
    decay = _hmap(lambda x, y: jnp.where(causal, jnp.exp(jnp.where(causal, x - y, 0.0)), 0.0), gc, gc_r)
    e_gc = _hmap(jnp.exp, gc)
    kb = _hmap(jnp.multiply, k, beta)
    vb = _hmap(jnp.multiply, v, beta)
    a_mat = _hmap(lambda x, y, d: jnp.where(strict, _dot(x, y, NT) * d, 0.0), kb, k, decay)
    t_inv = _unit_lower_inverse(a_mat)
    u_blk = _hmap(lambda t, x: _dot(t, x, NN), t_inv, vb)
    w_blk = _hmap(lambda t, x, e: _dot(t, x * e, NN), t_inv, kb, e_gc)
    v_new = _hmap(lambda u, w, s: u - _dot(w, s, NN), u_blk, w_blk, state)
    attn = _hmap(lambda x, y, d: _dot(x, y, NT) * d, q, k, decay)
    o_state = _hmap(lambda x, e, s: _dot(x * e, s, NN), q, e_gc, state)
    o = _hmap(lambda base, at, vn: base + _dot(at, vn, NN), o_state, attn, v_new)
    g_last = _hmap(lambda x: jnp.sum(x, axis=0, keepdims=True), g)
    k_dec = _hmap(lambda x, gl, c_: x * jnp.exp(gl - c_), k, g_last, gc)
    new_state = _hmap(lambda s, gl, kd, vn: s * jnp.exp(gl) + _dot(kd, vn, TN), state, g_last, k_dec, v_new)
    return o, new_state


def _gdn_operands(q_ref, k_ref, v_ref, bav, alog_ref, dtb_ref):
    hs = range(GDN_HEADS)
    cols = [slice(h * HEAD_DIM, (h + 1) * HEAD_DIM) for h in hs]
    return ([q_ref[:, sl] for sl in cols], [k_ref[:, sl] for sl in cols], [v_ref[:, sl] for sl in cols],
            [bav[:, h:h + 1] for h in hs], [bav[:, h + GDN_HEADS:h + GDN_HEADS + 1] for h in hs],
            [alog_ref[h:h + 1, 0:1] for h in hs], [dtb_ref[h:h + 1, 0:1] for h in hs])


def _gdn_scan_fwd(q, k, v, ba, a_log, dt_bias, *, name):
    length = q.shape[0]
    n = length // GDN_CHUNK
    c = GDN_CHUNK

    def body(q_ref, k_ref, v_ref, ba_ref, alog_ref, dtb_ref, o_ref, s_ref, state):
        i = pl.program_id(0)

        @pl.when(i == 0)
        def _():
            state[...] = jnp.zeros_like(state)

        bav = ba_ref[...]
        heads = [slice(h * HEAD_DIM, (h + 1) * HEAD_DIM) for h in range(GDN_HEADS)]
        s_in = [state[h] for h in range(GDN_HEADS)]
        o, s_out = _gdn_chunk(*_gdn_operands(q_ref, k_ref, v_ref, bav, alog_ref, dtb_ref), s_in)
        for h, sl in enumerate(heads):
            s_ref[h] = s_in[h]
            o_ref[:, sl] = o[h]
            state[h] = s_out[h]

    row_spec = pl.BlockSpec((c, D_MODEL), lambda i: (i, 0))
    small = pl.BlockSpec((GDN_HEADS, LANES), lambda i: (0, 0))
    return pl.pallas_call(
        body, name=name,
        out_shape=(jax.ShapeDtypeStruct((length, D_MODEL), F32),
                   jax.ShapeDtypeStruct((n, GDN_HEADS, HEAD_DIM, HEAD_DIM), F32)),
        grid=(n,),
        in_specs=[row_spec, row_spec, row_spec, pl.BlockSpec((c, LANES), lambda i: (i, 0)), small, small],
        out_specs=(row_spec, pl.BlockSpec((None, GDN_HEADS, HEAD_DIM, HEAD_DIM), lambda i: (i, 0, 0, 0))),
        scratch_shapes=[pltpu.VMEM((GDN_HEADS, HEAD_DIM, HEAD_DIM), F32)],
        compiler_params=_params(("arbitrary",)),
    )(q, k, v, ba, a_log, dt_bias)


def _gdn_scan_bwd(q, k, v, ba, a_log, dt_bias, states, do, *, name):
    length = q.shape[0]
    n = length // GDN_CHUNK
    c = GDN_CHUNK

    def body(q_ref, k_ref, v_ref, ba_ref, alog_ref, dtb_ref, s_ref, do_ref,
             dq_ref, dk_ref, dv_ref, dba_ref, dalog_ref, ddtb_ref, dstate):
        i = pl.program_id(0)

        @pl.when(i == 0)
        def _():
            dstate[...] = jnp.zeros_like(dstate)
            dalog_ref[...] = jnp.zeros_like(dalog_ref)
            ddtb_ref[...] = jnp.zeros_like(ddtb_ref)

        bav = ba_ref[...]
        lane = lax.broadcasted_iota(jnp.int32, (c, LANES), 1)
        sub8 = lax.broadcasted_iota(jnp.int32, (GDN_HEADS, LANES), 0)
        lane8 = lax.broadcasted_iota(jnp.int32, (GDN_HEADS, LANES), 1)
        slab = jnp.zeros((c, LANES), F32)
        dalog_all = jnp.zeros((GDN_HEADS, LANES), F32)
        ddtb_all = jnp.zeros((GDN_HEADS, LANES), F32)
        heads = [slice(h * HEAD_DIM, (h + 1) * HEAD_DIM) for h in range(GDN_HEADS)]
        ds_in = [dstate[h] for h in range(GDN_HEADS)]
        s_in = [s_ref[h] for h in range(GDN_HEADS)]
        _, vjp = jax.vjp(_gdn_chunk, *_gdn_operands(q_ref, k_ref, v_ref, bav, alog_ref, dtb_ref), s_in)
        dq, dk, dv, dbl, dal, dalog, ddtb, ds = vjp(([do_ref[:, sl] for sl in heads], ds_in))
        for h, sl in enumerate(heads):
            dq_ref[:, sl] = dq[h]
            dk_ref[:, sl] = dk[h]
            dv_ref[:, sl] = dv[h]
            dstate[h] = ds[h]
            slab = slab + jnp.where(lane == h, dbl[h], 0.0) + jnp.where(lane == h + GDN_HEADS, dal[h], 0.0)
            here = (sub8 == h) & (lane8 == 0)
            dalog_all = dalog_all + jnp.where(here, dalog[h], 0.0)
            ddtb_all = ddtb_all + jnp.where(here, ddtb[h], 0.0)
        dba_ref[...] = slab
        dalog_ref[...] += dalog_all
        ddtb_ref[...] += ddtb_all

    row_spec = pl.BlockSpec((c, D_MODEL), lambda i: (n - 1 - i, 0))
    small = pl.BlockSpec((GDN_HEADS, LANES), lambda i: (0, 0))
    return pl.pallas_call(
        body, name=name,
        out_shape=(jax.ShapeDtypeStruct((length, D_MODEL), F32),) * 3
        + (jax.ShapeDtypeStruct((length, LANES), F32),
           jax.ShapeDtypeStruct((GDN_HEADS, LANES), F32), jax.ShapeDtypeStruct((GDN_HEADS, LANES), F32)),
        grid=(n,),
        in_specs=[row_spec, row_spec, row_spec,
                  pl.BlockSpec((c, LANES), lambda i: (n - 1 - i, 0)), small, small,
                  pl.BlockSpec((None, GDN_HEADS, HEAD_DIM, HEAD_DIM), lambda i: (n - 1 - i, 0, 0, 0)),
                  row_spec],
        out_specs=(row_spec, row_spec, row_spec,
                   pl.BlockSpec((c, LANES), lambda i: (n - 1 - i, 0)), small, small),
        scratch_shapes=[pltpu.VMEM((GDN_HEADS, HEAD_DIM, HEAD_DIM), F32)],
        compiler_params=_params(("arbitrary",)),
    )(q, k, v, ba, a_log, dt_bias, states, do)


S5_W = S5_T * LANES
S5_S = 2 * 8 * S5_STATE
S5_SH = S5_S // 2


def _iota2(shape):
    return lax.broadcasted_iota(jnp.int32, shape, 0), lax.broadcasted_iota(jnp.int32, shape, 1)


def _s5_rep_t(t, dtype):
    row, col = _iota2((S5_T * S5_GROUP, LANES))
    return ((jnp.right_shift(row, 4) == t) & (jnp.bitwise_and(row, 15) == jnp.bitwise_and(col, 15))).astype(dtype)


def _s5_rep_state(dtype):
    row, col = _iota2((2 * S5_STATE, S5_S))
    return ((jnp.right_shift(row, 6) == jnp.right_shift(col, 9))
            & (jnp.bitwise_and(row, 63) == jnp.bitwise_and(col, 63))).astype(dtype)


def _s5_masks():
    row, col = _iota2((LANES, LANES))
    m_ab = jnp.right_shift(row, 4) == jnp.right_shift(col, 4)
    row, col = _iota2((S5_S, LANES))
    m_e = jnp.bitwise_and(jnp.right_shift(row, 6), 7) == jnp.right_shift(col, 4)
    row, col = _iota2((LANES, S5_S))
    m_f = jnp.right_shift(row, 4) == jnp.bitwise_and(jnp.right_shift(col, 6), 7)
    return m_ab, m_e, m_f


def _s5_expand(kx_ref, ec_ref, fc_ref, kb_scr, e_scr, f_scr):
    m_ab, m_e, m_f = _s5_masks()
    kx = kx_ref[...].astype(BF16)
    ec = ec_ref[...].astype(BF16)
    rep_state = _s5_rep_state(BF16)
    for t in range(S5_T):
        rep = _s5_rep_t(t, BF16)
        cols = slice(t * LANES, (t + 1) * LANES)
        kb_scr[t] = jnp.where(m_ab, jnp.dot(kx, rep, preferred_element_type=F32), 0.0).astype(BF16)
        e_scr[:, cols] = jnp.where(m_e, jnp.dot(ec, rep, preferred_element_type=F32), 0.0).astype(BF16)
        f_scr[cols, :] = jnp.where(m_f, jnp.dot(fc_ref[t].astype(BF16), rep_state, preferred_element_type=F32),
                                   0.0).astype(BF16)


def _s5_token_rows(ref, n):
    return [ref[pl.ds(t, n, stride=S5_T), :].astype(BF16) for t in range(S5_T)]


def _s5_scan_fwd(u, kx, ec, fc, at, *, name):
    length = u.shape[0]
    n = length // S5_T
    assert n % SUBLANES == 0

    def body(u_ref, kx_ref, ec_ref, fc_ref, at_ref, y_ref, h_ref, kb_scr, e_scr, f_scr, g_scr):
        _s5_expand(kx_ref, ec_ref, fc_ref, kb_scr, e_scr, f_scr)
        us = _s5_token_rows(u_ref, n)
        g_scr[...] = jnp.dot(jnp.concatenate(us, axis=1), f_scr[...], preferred_element_type=F32)
        ar, ai = at_ref[:, :S5_SH], at_ref[:, S5_SH:]

        def step(blk, h):
            base = pl.multiple_of(blk * SUBLANES, SUBLANES)
            g8 = g_scr[pl.ds(base, SUBLANES), :]
            rows = []
            for r in range(SUBLANES):
                rows.append(h)
                hr, hi = h[:, :S5_SH], h[:, S5_SH:]
                h = jnp.concatenate([ar * hr - ai * hi, ar * hi + ai * hr], axis=1) + g8[r:r + 1, :]
            h_ref[pl.ds(base, SUBLANES), :] = jnp.concatenate(rows, axis=0)
            return h

        lax.fori_loop(0, n // SUBLANES, step, jnp.zeros((1, S5_S), F32))
        hb = h_ref[...].astype(BF16)
        for t in range(S5_T):
            acc = jnp.dot(hb, e_scr[:, t * LANES:(t + 1) * LANES], preferred_element_type=F32)
            for s in range(t + 1):
                acc = acc + jnp.dot(us[s], kb_scr[t - s], preferred_element_type=F32)
            y_ref[pl.ds(t, n, stride=S5_T), :] = acc

    return pl.pallas_call(
        body, name=name,
        out_shape=(jax.ShapeDtypeStruct((length, D_MODEL), F32), jax.ShapeDtypeStruct((S5_TILES, n, S5_S), F32)),
        grid=(S5_TILES,),
        in_specs=[pl.BlockSpec((length, LANES), lambda k: (0, k)), _s5_spec(LANES, S5_T * S5_GROUP),
                  _s5_spec(S5_S, S5_T * S5_GROUP), _s5_spec(S5_T, LANES, LANES), _s5_spec(1, S5_S)],
        out_specs=(pl.BlockSpec((length, LANES), lambda k: (0, k)), _s5_spec(n, S5_S)),
        scratch_shapes=[pltpu.VMEM((S5_T, LANES, LANES), BF16), pltpu.VMEM((S5_S, S5_W), BF16),
                        pltpu.VMEM((S5_W, S5_S), BF16), pltpu.VMEM((n, S5_S), F32)],
        compiler_params=_params(("parallel",)),
    )(u, kx, ec, fc, at)


def _s5_spec(*tail):
    return pl.BlockSpec((None,) + tail, lambda k: (k,) + (0,) * len(tail))


def _s5_scan_bwd(dy, kx, ec, fc, at, hs, *, name):
    length = dy.shape[0]
    n = length // S5_T

    def body(dy_ref, kx_ref, ec_ref, fc_ref, at_ref, h_ref, du_ref, dg_ref, dat_ref, kb_scr, e_scr, f_scr, dh_scr):
        _s5_expand(kx_ref, ec_ref, fc_ref, kb_scr, e_scr, f_scr)
        dys = _s5_token_rows(dy_ref, n)
        dh_scr[...] = _dot(jnp.concatenate(dys, axis=1), e_scr[...], NT)
        ar, ai = at_ref[:, :S5_SH], at_ref[:, S5_SH:]

        def step(it, carry):
            cy, dat = carry
            base = pl.multiple_of((n // SUBLANES - 1 - it) * SUBLANES, SUBLANES)
            dh8 = dh_scr[pl.ds(base, SUBLANES), :]
            h8 = h_ref[pl.ds(base, SUBLANES), :]
            rows = [None] * SUBLANES
            for r in reversed(range(SUBLANES)):
                rows[r] = cy
                cr, ci = cy[:, :S5_SH], cy[:, S5_SH:]
                hr, hi = h8[r:r + 1, :S5_SH], h8[r:r + 1, S5_SH:]
                dat = dat + jnp.concatenate([cr * hr + ci * hi, ci * hr - cr * hi], axis=1)
                cy = dh8[r:r + 1, :] + jnp.concatenate([ar * cr + ai * ci, ar * ci - ai * cr], axis=1)
            dg_ref[pl.ds(base, SUBLANES), :] = jnp.concatenate(rows, axis=0)
            return cy, dat

        zero = jnp.zeros((1, S5_S), F32)
        _, dat = lax.fori_loop(0, n // SUBLANES, step, (zero, zero))
        dat_ref[...] = dat
        dgb = dg_ref[...].astype(BF16)
        for s in range(S5_T):
            acc = _dot(dgb, f_scr[s * LANES:(s + 1) * LANES, :], NT)
            for t in range(s, S5_T):
                acc = acc + _dot(dys[t], kb_scr[t - s], NT)
            du_ref[pl.ds(s, n, stride=S5_T), :] = acc

    row_spec = pl.BlockSpec((length, LANES), lambda k: (0, k))
    return pl.pallas_call(
        body, name=name,
        out_shape=(jax.ShapeDtypeStruct((length, D_MODEL), F32), jax.ShapeDtypeStruct((S5_TILES, n, S5_S), F32),
                   jax.ShapeDtypeStruct((S5_TILES, 1, S5_S), F32)),
        grid=(S5_TILES,),
        in_specs=[row_spec, _s5_spec(LANES, S5_T * S5_GROUP), _s5_spec(S5_S, S5_T * S5_GROUP),
                  _s5_spec(S5_T, LANES, LANES), _s5_spec(1, S5_S), _s5_spec(n, S5_S)],
        out_specs=(row_spec, _s5_spec(n, S5_S), _s5_spec(1, S5_S)),
        scratch_shapes=[pltpu.VMEM((S5_T, LANES, LANES), BF16), pltpu.VMEM((S5_S, S5_W), BF16),
                        pltpu.VMEM((S5_W, S5_S), BF16), pltpu.VMEM((n, S5_S), F32)],
        compiler_params=_params(("parallel",)),
    )(dy, kx, ec, fc, at, hs)


def _s5_operator_grads(dy, u, hs, dg, *, name):
    length = u.shape[0]
    n = length // S5_T

    def body(dy_ref, u_ref, h_ref, dg_ref, dkx_ref, dec_ref, dfc_ref):
        dys = _s5_token_rows(dy_ref, n)
        us = _s5_token_rows(u_ref, n)
        ucat = jnp.concatenate(us, axis=1)
        m_ab, m_e, m_f = _s5_masks()
        hb = h_ref[...].astype(BF16)
        dgb = dg_ref[...].astype(BF16)
        lane = lax.broadcasted_iota(jnp.int32, (1, LANES), 1)
        lane_group = jnp.right_shift(lane, 4)

        def own_block(x, mask):
            x = jnp.where(mask, x, 0.0)
            for shift in (64, 32, 16):
                x = x + pltpu.roll(x, shift, 1)
            return x

        def place(halves, t, x):
            halves[t // 8] = jnp.where(lane_group == t % 8, x, halves[t // 8])

        dkb = [jnp.zeros((LANES, LANES), F32) for _ in range(S5_T)]
        dec = [jnp.zeros((S5_S, LANES), F32) for _ in range(2)]
        for t in range(S5_T):
            d_t = _dot(ucat, dys[t], TN)
            for s in range(t + 1):
                dkb[t - s] = dkb[t - s] + d_t[s * LANES:(s + 1) * LANES, :]
            place(dec, t, own_block(_dot(hb, dys[t], TN), m_e))
            wide = jnp.where(m_f, _dot(us[t], dgb, TN), 0.0)
            parts = []
            for r in range(2):
                acc = wide[:, r * S5_SH:r * S5_SH + LANES]
                for q in range(1, S5_SH // LANES):
                    acc = acc + wide[:, r * S5_SH + q * LANES:r * S5_SH + (q + 1) * LANES]
                parts.append(acc + pltpu.roll(acc, S5_STATE, 1))
            dfc_ref[t] = jnp.where(lane < S5_STATE, parts[0], parts[1])
        dkx = [jnp.zeros((LANES, LANES), F32) for _ in range(2)]
        for t in range(S5_T):
            place(dkx, t, own_block(dkb[t], m_ab))
        dkx_ref[...] = jnp.concatenate(dkx, axis=1)
        dec_ref[...] = jnp.concatenate(dec, axis=1)

    row_spec = pl.BlockSpec((length, LANES), lambda k: (0, k))
    outs = (_s5_spec(LANES, S5_T * S5_GROUP), _s5_spec(S5_S, S5_T * S5_GROUP), _s5_spec(S5_T, LANES, LANES))
    return pl.pallas_call(
        body, name=name,
        out_shape=(jax.ShapeDtypeStruct((S5_TILES, LANES, S5_T * S5_GROUP), F32),
                   jax.ShapeDtypeStruct((S5_TILES, S5_S, S5_T * S5_GROUP), F32),
                   jax.ShapeDtypeStruct((S5_TILES, S5_T, LANES, LANES), F32)),
        grid=(S5_TILES,),
        in_specs=[row_spec, row_spec, _s5_spec(n, S5_S), _s5_spec(n, S5_S)],
        out_specs=outs,
        compiler_params=_params(("parallel",)),
    )(dy, u, hs, dg)


def _s5_prep(a_re, a_im, b_re, b_im, c_re, c_im, log_dt):
    t_len, tiles = S5_T, S5_TILES
    dt = jnp.exp(log_dt)[:, None]
    mag = jnp.exp(a_re * dt)
    ab_re, ab_im = mag * jnp.cos(a_im * dt), mag * jnp.sin(a_im * dt)
    den = jnp.square(a_re) + jnp.square(a_im)
    n_re, n_im = ab_re - 1.0, ab_im
    f_re = (n_re * a_re + n_im * a_im) / den
    f_im = (n_im * a_re - n_re * a_im) / den
    bb_re = f_re[..., None] * b_re - f_im[..., None] * b_im
    bb_im = f_re[..., None] * b_im + f_im[..., None] * b_re

    def powers(exponents):
        e = exponents[:, None, None]
        m = jnp.exp(e * (a_re * dt))
        return m * jnp.cos(e * (a_im * dt)), m * jnp.sin(e * (a_im * dt))

    p_re, p_im = powers(jnp.arange(t_len + 1, dtype=F32))
    rev_re, rev_im = powers((t_len - 1) - jnp.arange(t_len, dtype=F32))
    ca_re = c_re[None] * p_re[:, :, None, :] - c_im[None] * p_im[:, :, None, :]
    ca_im = c_re[None] * p_im[:, :, None, :] + c_im[None] * p_re[:, :, None, :]
    lag = (jnp.einsum('tgip,gpj->tgij', ca_re[:t_len], bb_re, precision=HI)
           - jnp.einsum('tgip,gpj->tgij', ca_im[:t_len], bb_im, precision=HI))
    kx = lag.reshape(t_len, tiles, 8, S5_GROUP, S5_GROUP).transpose(1, 2, 4, 0, 3)
    kx = kx.reshape(tiles, LANES, t_len * S5_GROUP)
    e_st = jnp.stack([ca_re[1:], -ca_im[1:]])
    e_st = e_st.reshape(2, t_len, tiles, 8, S5_GROUP, S5_STATE).transpose(2, 0, 3, 5, 1, 4)
    ec = e_st.reshape(tiles, S5_S, t_len * S5_GROUP)
    ab_b = jnp.stack([rev_re[..., None] * bb_re[None] - rev_im[..., None] * bb_im[None],
                      rev_re[..., None] * bb_im[None] + rev_im[..., None] * bb_re[None]])
    ab_b = ab_b.reshape(2, t_len, tiles, 8, S5_STATE, S5_GROUP).transpose(2, 1, 3, 5, 0, 4)
    fc = ab_b.reshape(tiles, t_len, LANES, 2 * S5_STATE)
    a_t = jnp.stack([p_re[t_len], p_im[t_len]]).reshape(2, tiles, 8 * S5_STATE).transpose(1, 0, 2)
    return kx, ec, fc, a_t.reshape(tiles, 1, S5_S)


TM_ROW = 256


def _gdn_fwd(x, w, tag):
    qkv = _mm(x, w["wqkv"], name="gdn_proj_qkv")
    z = _mm(x, w["wz"], name="gdn_proj_z")
    ba = _mm(x, w["wba"], name="gdn_proj_ba")
    cv = _conv_fwd(qkv, w["conv_w"], tm=TM_ROW, name="gdn_conv")
    q, k, v = _rw_fwd(_f_gdn_qkv, [cv], [], tm=TM_ROW, name="gdn_qkv")
    o, states = _gdn_scan_fwd(q, k, v, ba, w["a_log8"], w["dt_bias8"], name="gdn_scan")
    (mix,) = _rw_fwd(_f_gdn_out, [o, z], [w["norm_g"]], tm=TM_ROW, name="gdn_out", out_dtypes=[BF16])
    return mix, (qkv, z, ba, cv, q, k, v, states, o)


def _gdn_bwd(x, w, saved, dmix, dx_acc):
    qkv, z, ba, cv, q, k, v, states, o = saved
    (do, dz), (dnorm_g,) = _rw_bwd(_f_gdn_out, [o, z], [w["norm_g"]], [dmix], row_grad=[1, 1], param_grad=[1],
                                   tm=TM_ROW, name="gdn_out_bwd", row_dtypes=[F32, BF16])
    dq, dk, dv, dba, dalog, ddtb = _gdn_scan_bwd(q, k, v, ba, w["a_log8"], w["dt_bias8"], states, do,
                                                  name="gdn_scan_bwd")
    (dcv,), _ = _rw_bwd(_f_gdn_qkv, [cv], [], [dq, dk, dv], row_grad=[1], param_grad=[], tm=TM_ROW,
                        name="gdn_qkv_bwd")
    dqkv, dconv_w = _conv_bwd(qkv, w["conv_w"], dcv, tm=TM_ROW, name="gdn_conv_bwd")
    dx = _mm(dqkv, w["wqkv"], tb=True, acc=dx_acc, name="gdn_dx_qkv")
    dx = _mm(dz, w["wz"], tb=True, acc=dx, name="gdn_dx_z")
    dx = _mm(dba, w["wba"], tb=True, acc=dx, name="gdn_dx_ba")
    grads = dict(wqkv=_mm(x, dqkv, ta=True, name="gdn_dw_qkv"), wz=_mm(x, dz, ta=True, name="gdn_dw_z"),
                 wba=_mm(x, dba, ta=True, name="gdn_dw_ba"), conv_w=dconv_w,
                 a_log=dalog[:, 0], dt_bias=ddtb[:, 0], norm_g=dnorm_g[0])
    return dx, grads


def _s5_fwd(x, w, tag):
    u = _mm(x, w["wu"], name="s5_proj_u")
    y, hs = _s5_scan_fwd(u, w["kx"], w["ec"], w["fc"], w["a_t"], name="s5_scan")
    (zg,) = _rw_fwd(_f_s5_gelu, [y, u], [w["d"]], tm=TM_ROW, name="s5_gelu")
    t = _mm(zg, w["w_glu"], name="s5_glu")
    (mix,) = _rw_fwd(_f_s5_gate, [zg, t], [w["b_glu"]], tm=TM_ROW, name="s5_gate", out_dtypes=[BF16])
    return mix, (u, hs, y, zg, t)


def _s5_bwd(x, w, saved, dmix, dx_acc):
    u, hs, y, zg, t = saved
    (dzg, dt), (db_glu,) = _rw_bwd(_f_s5_gate, [zg, t], [w["b_glu"]], [dmix], row_grad=[1, 1], param_grad=[1],
                                   tm=TM_ROW, name="s5_gate_bwd", row_dtypes=[F32, BF16])
    dzg = _mm(dt, w["w_glu"], tb=True, acc=dzg, name="s5_dzg")
    dw_glu = _mm(zg, dt, ta=True, name="s5_dw_glu")
    (dy, du), (dd,) = _rw_bwd(_f_s5_gelu, [y, u], [w["d"]], [dzg], row_grad=[1, 1], param_grad=[1],
                              tm=TM_ROW, name="s5_gelu_bwd")
    du_scan, dg, dat = _s5_scan_bwd(dy, w["kx"], w["ec"], w["fc"], w["a_t"], hs, name="s5_scan_bwd")
    dkx, dec, dfc = _s5_operator_grads(dy, u, hs, dg, name="s5_operator_grads")
    (du,) = _rw_fwd(_f_add, [du, du_scan], [], tm=TM_ROW, name="s5_du_add", out_dtypes=[BF16])
    d_a_re, d_a_im, d_b_re, d_b_im, d_c_re, d_c_im, d_log_dt = w["prep_vjp"]((dkx, dec, dfc, dat))
    dx = _mm(du, w["wu"], tb=True, acc=dx_acc, name="s5_dx_u")
    grads = dict(wu=_mm(x, du, ta=True, name="s5_dw_u"), w_glu=dw_glu, b_glu=db_glu[0], d=dd[0],
                 a_re=d_a_re, a_im=d_a_im, b_re=d_b_re, b_im=d_b_im, c_re=d_c_re, c_im=d_c_im, log_dt=d_log_dt)
    return dx, grads


def _layer_fwd(x, mem, w, is_gdn):
    mix, msave = (_gdn_fwd if is_gdn else _s5_fwd)(x, w, "")
    xq = _mm(x, w["wxq"], name="proj_xq")
    kv = _mm(mem, w["wkv"], name="mem_kv")
    kmem, vmem = kv[:, :XA_DIM], kv[:, XA_DIM:]
    (cross,) = _rw_fwd(_f_attn, [xq], [kmem, vmem], tm=TM_ROW, name="attn", out_dtypes=[BF16])
    h = _mm(mix, w["wo_mix"], name="wo_mix")
    h = _mm(cross, w["wo_cross"], acc=h, name="wo_cross")
    (x1,) = _rw_fwd(_f_ln_res, [x, h], [w["ln1_g"], w["ln1_b"]], tm=TM_ROW, name="ln_res")
    hm, act = _mm_relu2(x1, w["w1"], name="mlp_up")
    f = _mm(act, w["w2"], name="mlp_down")
    (x2,) = _rw_fwd(_f_ln_res, [x1, f], [w["ln2_g"], w["ln2_b"]], tm=TM_ROW, name="ln_res")
    return x2, (x, msave, xq, kmem, vmem, mix, cross, h, x1, hm, act, f)


def _layer_bwd(mem, w, is_gdn, saved, dx2, token=None):
    x, msave, xq, kmem, vmem, mix, cross, h, x1, hm, act, f = saved
    ln2_g = w["ln2_g"] if token is None else w["ln2_g"] + token[0, 0]
    (dx1, df), (dg2, db2) = _rw_bwd(_f_ln_res, [x1, f], [ln2_g, w["ln2_b"]], [dx2], row_grad=[1, 1],
                                    param_grad=[1, 1], tm=TM_ROW, name="ln_res_bwd", row_dtypes=[F32, BF16])
    dhm = _mm_relu2_grad(df, w["w2"], hm, name="mlp_dhm")
    dw2 = _mm(act, df, ta=True, name="mlp_dw2")
    dx1 = _mm(dhm, w["w1"], tb=True, acc=dx1, name="mlp_dx")
    dw1 = _mm(x1, dhm, ta=True, out_blocks=N_CHIPS, name="mlp_dw1")
    (dx, dh), (dg1, db1) = _rw_bwd(_f_ln_res, [x, h], [w["ln1_g"], w["ln1_b"]], [dx1], row_grad=[1, 1],
                                   param_grad=[1, 1], tm=TM_ROW, name="ln_res_bwd", row_dtypes=[F32, BF16])
    dmix =_mm(dh, w["wo_mix"], tb=True, name="wo_dmix")
    dcross = _mm(dh, w["wo_cross"], tb=True, name="wo_dcross")
    dwo = jnp.concatenate([_mm(mix, dh, ta=True, name="wo_dw_mix"), _mm(cross, dh, ta=True, name="wo_dw_cross")], 0)
    (dxq,), (dkmem, dvmem) = _rw_bwd(_f_attn, [xq], [kmem, vmem], [dcross], row_grad=[1], param_grad=[1, 1],
                                     tm=TM_ROW, name="attn_bwd", row_dtypes=[BF16])
    dwkv = _mm(mem, jnp.concatenate([dkmem, dvmem], axis=1), ta=True, name="mem_dw_kv")
    dx = _mm(dxq, w["wxq"], tb=True, acc=dx, name="dx_xq")
    dwxq = _mm(x, dxq, ta=True, name="dw_xq")
    dx, mg = (_gdn_bwd if is_gdn else _s5_bwd)(x, w, msave, dmix, dx)
    grads = dict(mixer=mg, wxq=dwxq, wkv=dwkv, wo=dwo, w1=dw1, w2=dw2,
                 ln1_g=dg1[0], ln1_b=db1[0], ln2_g=dg2[0], ln2_b=db2[0])
    return dx, grads


def _loss_and_grad(y, target):
    def f(yv, tv):
        err = yv - tv
        return (err * (1.0 / D_MODEL),), (0.5 / D_MODEL * jnp.sum(err * err, axis=0, keepdims=True),)

    (dy,), (part,) = _rowwise(f, [y, target], [], [(D_MODEL, F32)], [((1, D_MODEL), F32)], tm=512, name="loss")
    return jnp.sum(part), dy


def _layer_weights(full, i):
    j = i // 2
    w = dict(wkv=full["w_kv_mem"][i].astype(BF16),
             wo_mix=full["w_o"][i][:D_MODEL].astype(BF16), wo_cross=full["w_o"][i][D_MODEL:].astype(BF16),
             ln1_g=full["ln1_g"][i][None], ln1_b=full["ln1_b"][i][None],
             ln2_g=full["ln2_g"][i][None], ln2_b=full["ln2_b"][i][None],
             w1=full["mlp_w1"][i].astype(BF16), w2=full["mlp_w2"][i].astype(BF16))
    if i % 2 == 0:
        w_in = full["gdn_w_in"][j]
        gd = 3 * D_MODEL
        w.update(wqkv=w_in[:, :gd].astype(BF16), wz=w_in[:, gd:gd + D_MODEL].astype(BF16),
                 wba=jnp.pad(w_in[:, gd + D_MODEL:gd + D_MODEL + 2 * GDN_HEADS],
                             ((0, 0), (0, LANES - 2 * GDN_HEADS))).astype(BF16),
                 wxq=w_in[:, gd + D_MODEL + 2 * GDN_HEADS:].astype(BF16),
                 conv_w=full["gdn_conv_w"][j],
                 a_log8=jnp.broadcast_to(full["gdn_a_log"][j][:, None], (GDN_HEADS, LANES)),
                 dt_bias8=jnp.broadcast_to(full["gdn_dt_bias"][j][:, None], (GDN_HEADS, LANES)),
                 norm_g=full["gdn_norm_g"][j][None])
    else:
        w_in = full["s5_w_in"][j]
        (kx, ec, fc, a_t), prep_vjp = jax.vjp(
            _s5_prep, full["s5_a_re"][j], full["s5_a_im"][j], full["s5_b_re"][j], full["s5_b_im"][j],
            full["s5_c_re"][j], full["s5_c_im"][j], full["s5_log_dt"][j])
        w.update(wu=w_in[:, :D_MODEL].astype(BF16), wxq=w_in[:, D_MODEL:].astype(BF16),
                 kx=kx, ec=ec, fc=fc, a_t=a_t, prep_vjp=prep_vjp,
                 d=full["s5_d"][j][None], w_glu=full["s5_w_glu"][j].astype(BF16), b_glu=full["s5_b_glu"][j][None])
    return w


def _sharded_grads(l, i):
    m = l["mixer"]
    out = dict(w_kv_mem=l["wkv"], w_o=l["wo"], mlp_w1=l["w1"], mlp_w2=l["w2"])
    if i % 2 == 0:
        out.update(gdn_w_in=jnp.concatenate([m["wqkv"], m["wz"], m["wba"][:, :2 * GDN_HEADS], l["wxq"]], axis=1),
                   gdn_conv_w=m["conv_w"])
    else:
        out.update(s5_w_in=jnp.concatenate([m["wu"], l["wxq"]], axis=1), s5_d=m["d"], s5_w_glu=m["w_glu"],
                   s5_b_glu=m["b_glu"])
    return out


def _replicated_grads(layer_grads):
    g = layer_grads
    gdn = [g[i]["mixer"] for i in range(DEPTH) if i % 2 == 0]
    s5 = [g[i]["mixer"] for i in range(DEPTH) if i % 2 == 1]
    out = {n: jnp.stack([l[n] for l in g]) for n in ("ln1_g", "ln1_b", "ln2_g", "ln2_b")}
    out.update({"gdn_" + n: jnp.stack([m[n] for m in gdn]) for n in ("a_log", "dt_bias", "norm_g")})
    out.update({"s5_" + n: jnp.stack([m[n] for m in s5])
                for n in ("a_re", "a_im", "b_re", "b_im", "c_re", "c_im", "log_dt")})
    return out


def _local_step(x, mem, target, weights_of, grads_ready):
    lw, saves = [], []
    h = x
    for i in range(DEPTH):
        lw.append(weights_of(i, h))
        h, s = _layer_fwd(h, mem, lw[i], i % 2 == 0)
        saves.append(s)
    loss, d = _loss_and_grad(h, target)
    grads = [None] * DEPTH
    token = None
    for i in reversed(range(DEPTH)):
        d, grads[i] = _layer_bwd(mem, lw[i], i % 2 == 0, saves[i], d, token)
        token = grads_ready(i, grads[i])
    return loss, d, grads


ANY = pl.BlockSpec(memory_space=pl.ANY)
SHARD_ROWS = 1024
SMALL_ROWS = 128


def _place():
    return lax.axis_index("x"), lax.axis_index("y"), lax.axis_index("c")


def _other_chips(x, y):
    return [(1 - x, y), (x, 1 - y), (1 - x, 1 - y)]


def _all_gather_chips(wpack, *, name):
    rows = wpack.shape[0]
    half = rows // 2

    def body(w_ref, out_ref, send_sems, recv_sems):
        x, y, c = _place()
        sibling = (x, y, 1 - c)
        chips = _other_chips(x, y)

        def blk(cx, cy, cc):
            return out_ref.at[2 * cx + cy, pl.ds(cc * half, half), :]

        def copy(k, src, dst, to):
            return pltpu.make_async_remote_copy(src_ref=src, dst_ref=dst, send_sem=send_sems.at[k],
                                                recv_sem=recv_sems.at[k], device_id=to, device_id_type=MESH)

        first = [copy(j, w_ref.at[pl.ds(c * half, half), :], blk(x, y, c), (cx, cy, c))
                 for j, (cx, cy) in enumerate(chips)]
        for cp in first:
            cp.start()
        passed = [copy(3 + j, blk(cx, cy, c), blk(cx, cy, c), sibling) for j, (cx, cy) in enumerate(chips)]
        for j, (cx, cy) in enumerate(chips):
            copy(j, blk(cx, cy, c), blk(cx, cy, c), (cx, cy, c)).wait_recv()
            passed[j].start()
        for j, (cx, cy) in enumerate(chips):
            copy(3 + j, blk(cx, cy, 1 - c), blk(cx, cy, 1 - c), sibling).wait_recv()
        for cp in first + passed:
            cp.wait_send()

    return pl.pallas_call(
        body, name=name, out_shape=jax.ShapeDtypeStruct((N_CHIPS, rows, D_MODEL), wpack.dtype),
        in_specs=[ANY], out_specs=ANY,
        scratch_shapes=[pltpu.SemaphoreType.DMA((6,)), pltpu.SemaphoreType.DMA((6,))],
    )(wpack)


HBM = pl.BlockSpec(memory_space=pltpu.HBM)
SEM = pl.BlockSpec(memory_space=pltpu.SEMAPHORE)
DATAFLOW = pltpu.SideEffectType.DATAFLOW_SIDE_EFFECTING


def _gather_ici_copies(w_ref, land_ref, send_sems, recv_sems, outgoing):
    x, y, c = _place()
    half = w_ref.shape[0] // 2
    mine = pl.ds(c * half, half)
    return [pltpu.make_async_remote_copy(
        src_ref=w_ref.at[mine, :], dst_ref=land_ref.at[2 * x + y if outgoing else 2 * cx + cy, mine, :],
        send_sem=send_sems.at[j], recv_sem=recv_sems.at[j], device_id=(cx, cy, c), device_id_type=MESH)
        for j, (cx, cy) in enumerate(_other_chips(x, y))]


def _gather_start(wpack, after):
    rows = wpack.shape[0]

    def body(w_ref, land_ref, after_ref, send_sems, recv_sems, w_thru, land_thru, token):
        for cp in _gather_ici_copies(w_ref, land_ref, send_sems, recv_sems, outgoing=True):
            cp.start()
        token[...] = jnp.zeros_like(token)

    land = pltpu.with_memory_space_constraint(lax.empty((N_CHIPS, rows, D_MODEL), wpack.dtype), pltpu.HBM)
    return pl.pallas_call(
        body, name="gather_start",
        out_shape=(pltpu.SemaphoreType.DMA((3,)), pltpu.SemaphoreType.DMA((3,)), pltpu.HBM(wpack.shape, wpack.dtype),
                   pltpu.HBM(land.shape, land.dtype), jax.ShapeDtypeStruct((SUBLANES, LANES), F32)),
        in_specs=(HBM, HBM, ANY), out_specs=(SEM, SEM, HBM, HBM, pl.BlockSpec(memory_space=pltpu.VMEM)),
        input_output_aliases={0: 2, 1: 3},
        compiler_params=pltpu.CompilerParams(has_side_effects=DATAFLOW),
    )(pltpu.with_memory_space_constraint(wpack, pltpu.HBM), land, after)


def _gather_wait(send_sems, recv_sems, w_thru, land_thru, after):
    def body(w_ref, land_ref, send_sems, recv_sems, after_ref, w_dead, land_out):
        for cp in _gather_ici_copies(w_ref, land_ref, send_sems, recv_sems, outgoing=False):
            cp.wait_send()
            cp.wait_recv()

    return pl.pallas_call(
        body, name="gather_wait",
        out_shape=(pltpu.HBM(w_thru.shape, w_thru.dtype), pltpu.HBM(land_thru.shape, land_thru.dtype)),
        in_specs=(HBM, HBM, SEM, SEM, ANY), out_specs=(HBM, HBM), input_output_aliases={0: 0, 1: 1},
        compiler_params=pltpu.CompilerParams(has_side_effects=DATAFLOW),
    )(w_thru, land_thru, send_sems, recv_sems, after)[1]


def _gather_forward(land, *, name):
    rows = land.shape[1]
    half = rows // 2

    def body(in_ref, out_ref, send_sems, recv_sems):
        x, y, c = _place()

        def copy(j, cx, cy, cc):
            rows_of = out_ref.at[2 * cx + cy, pl.ds(cc * half, half), :]
            return pltpu.make_async_remote_copy(src_ref=rows_of, dst_ref=rows_of, send_sem=send_sems.at[j],
                                                recv_sem=recv_sems.at[j], device_id=(x, y, 1 - c), device_id_type=MESH)

        sends = [copy(j, cx, cy, c) for j, (cx, cy) in enumerate(_other_chips(x, y))]
        for cp in sends:
            cp.start()
        for j, (cx, cy) in enumerate(_other_chips(x, y)):
            copy(j, cx, cy, 1 - c).wait_recv()
        for cp in sends:
            cp.wait_send()

    return pl.pallas_call(
        body, name=name, out_shape=jax.ShapeDtypeStruct(land.shape, land.dtype), in_specs=[ANY], out_specs=ANY,
        input_output_aliases={0: 0},
        scratch_shapes=[pltpu.SemaphoreType.DMA((3,)), pltpu.SemaphoreType.DMA((3,))],
    )(land)


def _sibling_swap(buf, *, name):
    def body(in_ref, out_ref, send_sem, recv_sem):
        x, y, c = _place()
        cp = pltpu.make_async_remote_copy(src_ref=in_ref, dst_ref=out_ref, send_sem=send_sem, recv_sem=recv_sem,
                                          device_id=(x, y, 1 - c), device_id_type=MESH)
        cp.start()
        cp.wait()

    return pl.pallas_call(
        body, name=name, out_shape=jax.ShapeDtypeStruct(buf.shape, buf.dtype), in_specs=[ANY], out_specs=ANY,
        scratch_shapes=[pltpu.SemaphoreType.DMA, pltpu.SemaphoreType.DMA],
    )(buf)


def _pair_exchange(gpack, *, name):
    pieces, rows, width = gpack.shape
    half = rows // 2

    def body(in_ref, got_ref, send_sems, recv_sems):
        x, y, c = _place()
        sends = [pltpu.make_async_remote_copy(src_ref=in_ref.at[p, pl.ds((1 - c) * half, half), :],
                                              dst_ref=got_ref.at[p], send_sem=send_sems.at[p],
                                              recv_sem=recv_sems.at[p], device_id=(x, y, 1 - c), device_id_type=MESH)
                 for p in range(pieces)]
        for cp in sends:
            cp.start()
        for cp in sends:
            cp.wait()

    return pl.pallas_call(
        body, name=name, out_shape=jax.ShapeDtypeStruct((pieces, half, width), gpack.dtype),
        in_specs=[ANY], out_specs=ANY,
        scratch_shapes=[pltpu.SemaphoreType.DMA((pieces,)), pltpu.SemaphoreType.DMA((pieces,))],
    )(gpack)


def _pair_add(gpack, got, c, *, name, tm=512):
    pieces, rows, width = gpack.shape
    half = rows // 2
    nb = half // tm

    def body(c_ref, a_ref, b_ref, sum_ref, narrow_ref):
        s = a_ref[...] + b_ref[...]
        sum_ref[...] = s
        narrow_ref[...] = s.astype(BF16)

    blk = pl.BlockSpec((None, tm, width), lambda p, i, c_ref: (p, i, 0))
    return pl.pallas_call(
        body, name=name,
        out_shape=(jax.ShapeDtypeStruct((pieces, half, width), F32), jax.ShapeDtypeStruct((pieces, half, width), BF16)),
        grid_spec=pltpu.PrefetchScalarGridSpec(
            num_scalar_prefetch=1, grid=(pieces, nb),
            in_specs=[pl.BlockSpec((None, tm, width), lambda p, i, c_ref: (p, c_ref[0] * nb + i, 0)), blk],
            out_specs=(blk, blk)),
        compiler_params=_params(("parallel", "parallel")),
    )(c, gpack, got)


def _chip_exchange(pieces, *, name):
    _, rows, width = pieces.shape

    def body(in_ref, out_ref, send_sems, recv_sems):
        x, y, c = _place()
        cps = [pltpu.make_async_remote_copy(src_ref=in_ref.at[2 * cx + cy], dst_ref=out_ref.at[j],
                                            send_sem=send_sems.at[j], recv_sem=recv_sems.at[j],
                                            device_id=(cx, cy, c), device_id_type=MESH)
               for j, (cx, cy) in enumerate(_other_chips(x, y))]
        for cp in cps:
            cp.start()
        for cp in cps:
            cp.wait()

    return pl.pallas_call(
        body, name=name, out_shape=jax.ShapeDtypeStruct((3, rows, width), pieces.dtype), in_specs=[ANY], out_specs=ANY,
        scratch_shapes=[pltpu.SemaphoreType.DMA((3,)), pltpu.SemaphoreType.DMA((3,))],
    )(pieces)


def _chip_exchange_copies(in_ref, land_ref, send_sems, recv_sems):
    x, y, c = _place()
    return [pltpu.make_async_remote_copy(src_ref=in_ref.at[2 * cx + cy], dst_ref=land_ref.at[j],
                                         send_sem=send_sems.at[j], recv_sem=recv_sems.at[j],
                                         device_id=(cx, cy, c), device_id_type=MESH)
            for j, (cx, cy) in enumerate(_other_chips(x, y))]


def _chip_exchange_start(pieces):
    _, rows, width = pieces.shape

    def body(in_ref, land_ref, send_sems, recv_sems, in_thru, land_thru, token):
        for cp in _chip_exchange_copies(in_ref, land_ref, send_sems, recv_sems):
            cp.start()
        token[...] = jnp.zeros_like(token)

    land = pltpu.with_memory_space_constraint(lax.empty((3, rows, width), pieces.dtype), pltpu.HBM)
    return pl.pallas_call(
        body, name="rs_chip_start",
        out_shape=(pltpu.SemaphoreType.DMA((3,)), pltpu.SemaphoreType.DMA((3,)), pltpu.HBM(pieces.shape, pieces.dtype),
                   pltpu.HBM(land.shape, land.dtype), jax.ShapeDtypeStruct((SUBLANES, LANES), F32)),
        in_specs=(HBM, HBM), out_specs=(SEM, SEM, HBM, HBM, pl.BlockSpec(memory_space=pltpu.VMEM)),
        input_output_aliases={0: 2, 1: 3},
        compiler_params=pltpu.CompilerParams(has_side_effects=DATAFLOW),
    )(pltpu.with_memory_space_constraint(pieces, pltpu.HBM), land)


def _chip_exchange_wait(send_sems, recv_sems, in_thru, land_thru, after):
    def body(in_ref, land_ref, send_sems, recv_sems, after_ref, in_dead, land_out):
        for cp in _chip_exchange_copies(in_ref, land_ref, send_sems, recv_sems):
            cp.wait_send()
            cp.wait_recv()

    return pl.pallas_call(
        body, name="rs_chip_wait",
        out_shape=(pltpu.HBM(in_thru.shape, in_thru.dtype), pltpu.HBM(land_thru.shape, land_thru.dtype)),
        in_specs=(HBM, HBM, SEM, SEM, ANY), out_specs=(HBM, HBM), input_output_aliases={0: 0, 1: 1},
        compiler_params=pltpu.CompilerParams(has_side_effects=DATAFLOW),
    )(in_thru, land_thru, send_sems, recv_sems, after)[1]


def _all_reduce_small(v, *, name):
    rows, width = v.shape
    half = rows // 2
    assert half % SUBLANES == 0

    def body(in_ref, out_ref, pair_buf, chip_buf, send_sems, recv_sems):
        x, y, c = _place()
        sibling = (x, y, 1 - c)
        me = 2 * x + y
        mine = pl.ds(pl.multiple_of(c * half, SUBLANES), half)
        other = pl.ds(pl.multiple_of((1 - c) * half, SUBLANES), half)

        def copy(k, src, dst, to):
            return pltpu.make_async_remote_copy(src_ref=src, dst_ref=dst, send_sem=send_sems.at[k],
                                                recv_sem=recv_sems.at[k], device_id=to, device_id_type=MESH)

        swap = copy(0, in_ref.at[other, :], pair_buf, sibling)
        swap.start()
        swap.wait()
        chip_buf[me] = in_ref[mine, :] + pair_buf[...]
        chips = _other_chips(x, y)
        for j, (cx, cy) in enumerate(chips):
            copy(1 + j, chip_buf.at[me], chip_buf.at[me], (cx, cy, c)).start()
        for j, (cx, cy) in enumerate(chips):
            got = copy(1 + j, chip_buf.at[me], chip_buf.at[2 * cx + cy], (cx, cy, c))
            got.wait_send()
            got.wait_recv()
        out_ref[mine, :] = ((chip_buf[0] + chip_buf[1]) + chip_buf[2]) + chip_buf[3]
        share = copy(1 + len(chips), out_ref.at[mine, :], out_ref.at[mine, :], sibling)
        share.start()
        share.wait_send()
        copy(1 + len(chips), out_ref.at[other, :], out_ref.at[other, :], sibling).wait_recv()

    vmem = pl.BlockSpec(memory_space=pltpu.VMEM)
    return pl.pallas_call(
        body, name=name, out_shape=jax.ShapeDtypeStruct(v.shape, v.dtype), in_specs=[vmem], out_specs=vmem,
        scratch_shapes=[pltpu.VMEM((half, width), v.dtype), pltpu.VMEM((N_CHIPS, half, width), v.dtype),
                        pltpu.SemaphoreType.DMA((5,)), pltpu.SemaphoreType.DMA((5,))],
        compiler_params=pltpu.CompilerParams(vmem_limit_bytes=VMEM_LIMIT_V7X),
    )(v)


def _reduce_scatter_begin(gpack, behind):
    x, y, c = _place()
    got = _pair_exchange(gpack, name="rs_pair_swap")
    pair, pair16 = _pair_add(gpack, got, c.astype(jnp.int32).reshape(1), name="rs_pair_add")
    mine = lax.dynamic_index_in_dim(pair, 2 * x + y, axis=0, keepdims=False)
    if behind:
        *in_flight, token = _chip_exchange_start(pair16)
        return dict(mine=mine, in_flight=in_flight), token
    return dict(mine=mine, recv=_chip_exchange(pair16, name="rs_chip_exchange")), None


def _reduce_scatter_end(state, after=None):
    c = lax.axis_index("c")
    recv = state["recv"] if "recv" in state else _chip_exchange_wait(*state["in_flight"], after=after)
    (total,) = _rw_fwd(_f_add4, [state["mine"], recv[0], recv[1], recv[2]], [], tm=512, name="rs_chip_add")
    theirs = _sibling_swap(total, name="rs_share_swap")
    return jnp.concatenate([jnp.where(c == 0, total, theirs), jnp.where(c == 0, theirs, total)], axis=0)


_SHARDED = (("w_kv_mem", 1), ("w_o", 1), ("mlp_w1", 2), ("mlp_w2", 1), ("gdn_w_in", 2), ("gdn_conv_w", 2),
            ("s5_w_in", 2), ("s5_d", 1), ("s5_w_glu", 1), ("s5_b_glu", 1))
_MATMUL_ONLY = ("w_kv_mem", "w_o", "mlp_w1", "mlp_w2", "gdn_w_in", "s5_w_in", "s5_w_glu")
REDUCED_FIRST = (1, 2, 3)
_KEPT_BLOCKED = ("mlp_w1",)
_REPLICATED = ("ln1_g", "ln1_b", "ln2_g", "ln2_b", "gdn_a_log", "gdn_dt_bias", "gdn_norm_g", "s5_a_re", "s5_a_im",
               "s5_b_re", "s5_b_im", "s5_c_re", "s5_c_im", "s5_log_dt")
_WEIGHTS = ("w_kv_mem", "w_o", "ln1_g", "ln1_b", "ln2_g", "ln2_b", "mlp_w1", "mlp_w2", "gdn_w_in", "gdn_conv_w",
            "gdn_a_log", "gdn_dt_bias", "gdn_norm_g", "s5_w_in", "s5_a_re", "s5_a_im", "s5_b_re", "s5_b_im",
            "s5_c_re", "s5_c_im", "s5_log_dt", "s5_d", "s5_w_glu", "s5_b_glu")


ROW_ALIGN = 16


def _n_rows(shape):
    return -(-math.prod(shape) // (ROW_ALIGN * D_MODEL)) * ROW_ALIGN


def _as_rows(a):
    rows = _n_rows(a.shape)
    if a.shape[-1] == D_MODEL and a.size == rows * D_MODEL:
        return a.reshape(-1, D_MODEL)
    flat = a.reshape(-1)
    return jnp.pad(flat, (0, rows * D_MODEL - flat.size)).reshape(rows, D_MODEL)


def _pack(arrs, unit_rows=SHARD_ROWS):
    rows = [_as_rows(a) for a in arrs]
    pad = -sum(r.shape[0] for r in rows) % unit_rows
    if pad:
        rows.append(jnp.zeros((pad, D_MODEL), rows[0].dtype))
    return jnp.concatenate(rows, axis=0)


def _unpack(packed, shapes):
    lead = packed.shape[:-2]
    out, off = [], 0
    for s in shapes:
        r = _n_rows(s)
        seg = lax.slice_in_dim(packed, off, off + r, axis=len(lead))
        if s[-1] != D_MODEL or math.prod(s) != r * D_MODEL:
            seg = lax.slice_in_dim(seg.reshape(lead + (-1,)), 0, math.prod(s), axis=len(lead))
        out.append(seg.reshape(lead + tuple(s)))
        off += r
    return out


def _split3(t):
    hi = t.astype(BF16)
    r1 = t - hi.astype(F32)
    mid = r1.astype(BF16)
    lo = (r1 - mid.astype(F32)).astype(BF16)
    return jnp.stack([hi, mid, lo], axis=-1)


def _join3(t):
    return (t[..., 0].astype(F32) + t[..., 1].astype(F32)) + t[..., 2].astype(F32)


def _merge_chips(blocks, axis):
    return jnp.concatenate([blocks[s] for s in range(N_CHIPS)], axis=axis)


def _pack_for_chips(weights):
    rows = []
    for s in range(N_CHIPS):
        chip = []
        for layers, axis in weights:
            if axis is None:
                blocks = [g[s] for g in layers]
            else:
                n = layers[0].shape[axis] // N_CHIPS
                blocks = [lax.slice_in_dim(g, s * n, (s + 1) * n, axis=axis) for g in layers]
            if math.prod(blocks[0].shape) % (ROW_ALIGN * D_MODEL) == 0:
                chip += [_as_rows(b) for b in blocks]
            else:
                chip.append(_as_rows(jnp.stack(blocks)))
        pad = -sum(r.shape[0] for r in chip) % SHARD_ROWS
        rows += chip + ([jnp.zeros((pad, D_MODEL), F32)] if pad else [])
    return jnp.concatenate(rows, axis=0).reshape(N_CHIPS, -1, D_MODEL)


def kernel(x, mem, w_kv_mem, w_o, ln1_g, ln1_b, ln2_g, ln2_b, mlp_w1, mlp_w2, gdn_w_in, gdn_conv_w, gdn_a_log, gdn_dt_bias, gdn_norm_g, s5_w_in, s5_a_re, s5_a_im, s5_b_re, s5_b_im, s5_c_re, s5_c_im, s5_log_dt, s5_d, s5_w_glu, s5_b_glu, loss_target, m_w_kv_mem, m_w_o, m_ln1_g, m_ln1_b, m_ln2_g, m_ln2_b, m_mlp_w1, m_mlp_w2, m_gdn_w_in, m_gdn_conv_w, m_gdn_a_log, m_gdn_dt_bias, m_gdn_norm_g, m_s5_w_in, m_s5_a_re, m_s5_a_im, m_s5_b_re, m_s5_b_im, m_s5_c_re, m_s5_c_im, m_s5_log_dt, m_s5_d, m_s5_w_glu, m_s5_b_glu, v_w_kv_mem, v_w_o, v_ln1_g, v_ln1_b, v_ln2_g, v_ln2_b, v_mlp_w1, v_mlp_w2, v_gdn_w_in, v_gdn_conv_w, v_gdn_a_log, v_gdn_dt_bias, v_gdn_norm_g, v_s5_w_in, v_s5_a_re, v_s5_a_im, v_s5_b_re, v_s5_b_im, v_s5_c_re, v_s5_c_im, v_s5_log_dt, v_s5_d, v_s5_w_glu, v_s5_b_glu):
    given = dict(locals())
    w = {n: given[n] for n in _WEIGHTS}
    mom = {n: given["m_" + n] for n in _WEIGHTS}
    var = {n: given["v_" + n] for n in _WEIGHTS}
    shard_names = [n for n, _ in _SHARDED]
    shard_shapes = [w[n].shape for n in shard_names]
    rep_shapes = [w[n].shape for n in _REPLICATED]

    wire = {n: w[n].astype(BF16) if n in _MATMUL_ONLY else _split3(w[n]) for n in shard_names}
    first = {n: 0 if n.startswith("s5_") else 1 for n in shard_names}
    me_chip = 2 * lax.axis_index("x") + lax.axis_index("y")
    early = [wire[n][:first[n]] for n in shard_names if first[n]]
    late = [wire[n][first[n]:] for n in shard_names]
    early_pack, late_pack = _pack(early), _pack(late)
    landed = _all_gather_chips(early_pack, name="gather_first_layer")
    landed = lax.dynamic_update_index_in_dim(landed, early_pack, me_chip, axis=0)
    early_blocks = dict(zip([n for n in shard_names if first[n]], _unpack(landed, [a.shape for a in early])))
    send_sems, recv_sems, pack_thru, land_thru, token = _gather_start(late_pack, after=landed)
    axis_of = dict(_SHARDED)

    def merged(n, blk):
        if n in _KEPT_BLOCKED:
            return blk
        return _merge_chips(blk if n in _MATMUL_ONLY else _join3(blk), axis_of[n] - 1)

    late_full = {}

    def weights_of(i, h):
        if i == 0:
            full = {n: [merged(n, blk[:, 0])] for n, blk in early_blocks.items()}
            full["gdn_w_in"][0] = full["gdn_w_in"][0] + token[0, 0].astype(BF16)
        else:
            if not late_full:
                land = _gather_wait(send_sems, recv_sems, pack_thru, land_thru, after=h)
                land = _gather_forward(land, name="gather_forward")
                land = lax.dynamic_update_index_in_dim(land, late_pack, me_chip, axis=0)
                for n, blk in zip(shard_names, _unpack(land, [a.shape for a in late])):
                    late_full[n] = [None] * first[n] + [merged(n, blk[:, t]) for t in range(blk.shape[1])]
            full = dict(late_full)
        full.update({n: w[n] for n in _REPLICATED})
        return _layer_weights(full, i)

    sharded = {}
    in_flight = {}

    def group_pack(layers):
        names = [n for n in shard_names if any(n in sharded[i] for i in layers)]
        per_weight = [[sharded[i][n] for i in layers if n in sharded[i]] for n in names]
        pack = _pack_for_chips([(g, None if n in _KEPT_BLOCKED else axis_of[n] - 1) for n, g in zip(names, per_weight)])
        return pack, names, [(len(g),) + w[n].shape[1:] for n, g in zip(names, per_weight)]

    def grads_ready(i, g):
        sharded[i] = _sharded_grads(g, i)
        if i != REDUCED_FIRST[0]:
            return None
        pack, names, shapes = group_pack(REDUCED_FIRST)
        state, token = _reduce_scatter_begin(pack, behind=True)
        in_flight.update(state=state, names=names, shapes=shapes)
        return token

    loss, grad_x, layer_grads = _local_step(x[0], mem[0], loss_target[0], weights_of, grads_ready)
    loss = lax.psum(loss, ("x", "y", "c"))
    rest = [i for i in range(DEPTH) if i not in REDUCED_FIRST]
    pack, names, shapes = group_pack(rest)
    state, _ = _reduce_scatter_begin(pack, behind=False)
    pieces = {n: [] for n in shard_names}
    for n, g in zip(names, _unpack(_reduce_scatter_end(state), shapes)):
        pieces[n].append(g)
    late = _reduce_scatter_end(in_flight["state"], after=grad_x)
    for n, g in zip(in_flight["names"], _unpack(late, in_flight["shapes"])):
        pieces[n].append(g)
    g_shards = [p[0] if len(p) == 1 else jnp.concatenate(p, axis=0) for p in (pieces[n] for n in shard_names)]

    def pack_small(d):
        return _pack([d[n] for n in _REPLICATED], unit_rows=SMALL_ROWS)

    g_rep = _all_reduce_small(pack_small(_replicated_grads(layer_grads)), name="reduce_replicated")

    def adamw(wp, gp, mp, vp, name):
        return _rw_fwd(_f_adamw, [wp, gp, mp, vp], [], tm=256, name=name)

    outs = {}
    for n, g in zip(shard_names, g_shards):
        flat = (-1, w[n].shape[-1])
        res = adamw(w[n].reshape(flat), g.reshape(flat), mom[n].reshape(flat), var[n].reshape(flat), "adamw_" + n)
        outs[("grad", n)] = g
        outs.update({(kind, n): a.reshape(w[n].shape) for kind, a in zip(("delta", "new_m", "new_v"), res)})
    packed = (g_rep,) + tuple(adamw(pack_small(w), g_rep, pack_small(mom), pack_small(var), "adamw_replicated"))
    for kind, pr in zip(("grad", "delta", "new_m", "new_v"), packed):
        outs.update({(kind, n): a for n, a in zip(_REPLICATED, _unpack(pr, rep_shapes))})
    return (loss, grad_x[None]) + tuple(outs[(kind, n)] for kind in ("grad", "delta", "new_m", "new_v")
                                        for n in _WEIGHTS)
```

```python
import functools
import math

import jax
import jax.numpy as jnp
from jax import lax
from jax.experimental import pallas as pl
from jax.experimental.pallas import tpu as pltpu

F32 = jnp.float32
BF16 = jnp.bfloat16
MESH = pl.DeviceIdType.MESH

D_MODEL = 1024
DEPTH = 4
GDN_HEADS = 8
HEAD_DIM = 128
GDN_CONV = 4
GDN_CHUNK = 64
S5_GROUPS = 64
S5_GROUP = 16
S5_STATE = 64
XA_HEADS = 4
XA_DIM = 512
D_FF = 4096
DN_ALPHA = (2 * DEPTH) ** 0.25
LN_EPS = 1e-5
RMS_EPS = 1e-6
ADAM_LR, ADAM_B1, ADAM_B2, ADAM_EPS, ADAM_WD, ADAM_STEP = 0.001, 0.9, 0.999, 1e-08, 0.01, 10

VMEM_LIMIT_V7X = 56 * 1024 * 1024
LANES = 128
SUBLANES = 8
S5_T = 16
S5_TILES = D_MODEL // LANES
N_CHIPS = 4
N_DEV = 8


def _params(sem):
    return pltpu.CompilerParams(dimension_semantics=sem, vmem_limit_bytes=VMEM_LIMIT_V7X)


def _tile(n, pref):
    if n <= pref:
        return n
    t = (pref // LANES) * LANES
    while n % t:
        t -= LANES
    return t


def _row_tile(n, pref):
    if n % SUBLANES:
        return n
    t = min(pref, n) // SUBLANES * SUBLANES
    while n % t:
        t -= SUBLANES
    return t


def _col_blocked_spec(rows_tile, cols_tile, block_cols, rows_axis, cols_axis):
    r = block_cols // cols_tile

    def index(*ijk):
        c = ijk[cols_axis]
        return (c, ijk[rows_axis], 0) if r == 1 else (c // r, ijk[rows_axis], c % r)

    return pl.BlockSpec((None, rows_tile, cols_tile), index)


def _mm(a, b, *, ta=False, tb=False, acc=None, name, tm=1024, tn=1024, tk=None, out_blocks=0):
    if tk is None:
        tk = 4096 if a.dtype == BF16 and b.dtype == BF16 else 2048
    k_dim, m_dim = a.shape if ta else a.shape[::-1]
    b_rows, b_cols = (b.shape[0], b.shape[1]) if b.ndim == 2 else (b.shape[1], b.shape[0] * b.shape[2])
    n_dim = b_rows if tb else b_cols
    assert (b_cols if tb else b_rows) == k_dim, (a.shape, b.shape, ta, tb)
    limit_n = n_dim // out_blocks if out_blocks else (n_dim if b.ndim == 2 or tb else b.shape[2])
    limit_k = b.shape[2] if (b.ndim == 3 and tb) else k_dim
    tm, tn, tk = _tile(m_dim, tm), _tile(limit_n, min(tn, limit_n)), _tile(limit_k, min(tk, limit_k))
    a_spec = (pl.BlockSpec((tk, tm), lambda i, j, k: (k, i)) if ta else pl.BlockSpec((tm, tk), lambda i, j, k: (i, k)))
    if b.ndim == 3:
        b_spec = (_col_blocked_spec(tn, tk, b.shape[2], 1, 2) if tb else _col_blocked_spec(tk, tn, b.shape[2], 2, 1))
    else:
        b_spec = (pl.BlockSpec((tn, tk), lambda i, j, k: (j, k)) if tb
                  else pl.BlockSpec((tk, tn), lambda i, j, k: (k, j)))
    o_spec = (_col_blocked_spec(tm, tn, n_dim // out_blocks, 0, 1) if out_blocks
              else pl.BlockSpec((tm, tn), lambda i, j, k: (i, j)))
    o_shape = (out_blocks, m_dim, n_dim // out_blocks) if out_blocks else (m_dim, n_dim)
    dn = (((0 if ta else 1,), (1 if tb else 0,)), ((), ()))
    has_acc = acc is not None

    def body(*refs):
        a_ref, b_ref = refs[0], refs[1]
        o_ref = refs[-1]
        k = pl.program_id(2)
        p = lax.dot_general(a_ref[...].astype(BF16), b_ref[...].astype(BF16), dn,
                            preferred_element_type=F32)

        @pl.when(k == 0)
        def _():
            o_ref[...] = p + refs[2][...] if has_acc else p

        @pl.when(k > 0)
        def _():
            o_ref[...] += p

    return pl.pallas_call(
        body, name=name,
        out_shape=jax.ShapeDtypeStruct(o_shape, F32),
        grid=(m_dim // tm, n_dim // tn, k_dim // tk),
        in_specs=[a_spec, b_spec] + ([o_spec] if has_acc else []),
        out_specs=o_spec,
        compiler_params=_params(("parallel", "parallel", "arbitrary")),
    )(*([a, b] + ([acc] if has_acc else [])))


def _mm_relu2(a, b, *, name, tm=1024):
    m_dim, k_dim = a.shape
    n_blocks, _, tn = b.shape
    n_dim = n_blocks * tn
    tm = _tile(m_dim, tm)

    def body(a_ref, b_ref, h_ref, act_ref):
        h = jnp.dot(a_ref[...].astype(BF16), b_ref[...].astype(BF16), preferred_element_type=F32)
        h_ref[...] = h
        r = jnp.maximum(h, 0.0)
        act_ref[...] = (r * r).astype(BF16)

    o_spec = pl.BlockSpec((tm, tn), lambda i, j: (i, j))
    return pl.pallas_call(
        body, name=name,
        out_shape=(jax.ShapeDtypeStruct((m_dim, n_dim), F32), jax.ShapeDtypeStruct((m_dim, n_dim), BF16)),
        grid=(m_dim // tm, n_dim // tn),
        in_specs=[pl.BlockSpec((tm, k_dim), lambda i, j: (i, 0)),
                  pl.BlockSpec((None, k_dim, tn), lambda i, j: (j, 0, 0))],
        out_specs=(o_spec, o_spec),
        compiler_params=_params(("parallel", "parallel")),
    )(a, b)


def _mm_relu2_grad(d, b, h, *, name, tm=1024, tn=1024):
    m_dim, k_dim = d.shape
    n_dim = b.shape[0]
    tm, tn = _tile(m_dim, tm), _tile(n_dim, tn)

    def body(d_ref, b_ref, h_ref, o_ref):
        p = lax.dot_general(d_ref[...].astype(BF16), b_ref[...].astype(BF16), ((NT), ((), ())),
                            preferred_element_type=F32)
        o_ref[...] = (p * (2.0 * jnp.maximum(h_ref[...], 0.0))).astype(BF16)

    o_spec = pl.BlockSpec((tm, tn), lambda i, j: (i, j))
    return pl.pallas_call(
        body, name=name,
        out_shape=jax.ShapeDtypeStruct((m_dim, n_dim), BF16),
        grid=(m_dim // tm, n_dim // tn),
        in_specs=[pl.BlockSpec((tm, k_dim), lambda i, j: (i, 0)), pl.BlockSpec((tn, k_dim), lambda i, j: (j, 0)), o_spec],
        out_specs=o_spec,
        compiler_params=_params(("parallel", "parallel")),
    )(d, b, h)


def _rowwise(f, rows, params, row_out, acc_out, *, tm, name):
    length = rows[0].shape[0]
    tm = _row_tile(length, tm)
    nr, npar, nro = len(rows), len(params), len(row_out)

    def body(*refs):
        ins = [r[...] for r in refs[:nr + npar]]
        outs = refs[nr + npar:]
        r_o, a_o = f(*ins)
        for ref, val in zip(outs[:nro], r_o):
            ref[...] = val.astype(ref.dtype)
        i = pl.program_id(0)
        for ref, val in zip(outs[nro:], a_o):
            @pl.when(i == 0)
            def _(ref=ref, val=val):
                ref[...] = val.astype(ref.dtype)

            @pl.when(i > 0)
            def _(ref=ref, val=val):
                ref[...] += val.astype(ref.dtype)

    in_specs = ([pl.BlockSpec((tm, r.shape[1]), lambda i: (i, 0)) for r in rows]
                + [pl.BlockSpec(p.shape, lambda i: (0, 0)) for p in params])
    out_specs = ([pl.BlockSpec((tm, w), lambda i: (i, 0)) for w, _ in row_out]
                 + [pl.BlockSpec(s, lambda i: (0, 0)) for s, _ in acc_out])
    out_shape = ([jax.ShapeDtypeStruct((length, w), dt) for w, dt in row_out]
                 + [jax.ShapeDtypeStruct(s, dt) for s, dt in acc_out])
    res = pl.pallas_call(
        body, name=name, out_shape=out_shape, grid=(length // tm,),
        in_specs=in_specs, out_specs=out_specs,
        compiler_params=_params(("arbitrary",) if acc_out else ("parallel",)),
    )(*rows, *params)
    return res[:nro], res[nro:]


def _rw_fwd(f, rows, params, *, tm, name, out_dtypes=None):
    tm_ = _row_tile(rows[0].shape[0], tm)
    shapes = jax.eval_shape(f, *[jax.ShapeDtypeStruct((tm_, r.shape[1]), r.dtype) for r in rows],
                            *[jax.ShapeDtypeStruct(p.shape, p.dtype) for p in params])
    row_out = [(s.shape[1], s.dtype if out_dtypes is None else dt)
               for s, dt in zip(shapes, out_dtypes or shapes)]
    outs, _ = _rowwise(lambda *v: (f(*v), ()), rows, params, row_out, [], tm=tm, name=name)
    return outs


def _rw_bwd(f, rows, params, cots, *, row_grad, param_grad, tm, name, row_dtypes=None):
    nr, npar, nct = len(rows), len(params), len(cots)

    def g(*vals):
        prim = vals[:nr] + vals[nr + nct:]
        ct = vals[nr:nr + nct]
        _, vjp = jax.vjp(f, *prim)
        grads = vjp(tuple(ct))
        return (tuple(grads[i] for i in range(nr) if row_grad[i]),
                tuple(grads[nr + i] for i in range(npar) if param_grad[i]))

    widths = [rows[i].shape[1] for i in range(nr) if row_grad[i]]
    row_out = list(zip(widths, row_dtypes or [F32] * len(widths)))
    acc_out = [(params[i].shape, F32) for i in range(npar) if param_grad[i]]
    return _rowwise(g, list(rows) + list(cots), params, row_out, acc_out, tm=tm, name=name)


def _f_ln_res(x, h, g, b):
    pre = DN_ALPHA * x + h
    mu = jnp.mean(pre, axis=-1, keepdims=True)
    d = pre - mu
    var = jnp.mean(d * d, axis=-1, keepdims=True)
    return (d * lax.rsqrt(var + LN_EPS) * g + b,)


def _silu(t):
    return t * jax.nn.sigmoid(t)


def _f_gdn_qkv(c):
    a = _silu(c)
    outs = []
    for part, scale in ((0, HEAD_DIM ** -0.5), (1, 1.0)):
        heads = []
        for h in range(GDN_HEADS):
            t = a[:, part * D_MODEL + h * HEAD_DIM: part * D_MODEL + (h + 1) * HEAD_DIM]
            t = t * lax.rsqrt(jnp.sum(t * t, axis=-1, keepdims=True) + 1e-6)
            heads.append(t * scale if scale != 1.0 else t)
        outs.append(jnp.concatenate(heads, axis=-1))
    outs.append(a[:, 2 * D_MODEL:])
    return tuple(outs)


def _f_gdn_out(o, z, norm_g):
    heads = []
    for h in range(GDN_HEADS):
        t = o[:, h * HEAD_DIM:(h + 1) * HEAD_DIM]
        t = t * lax.rsqrt(jnp.mean(t * t, axis=-1, keepdims=True) + RMS_EPS) * norm_g
        heads.append(t)
    return (jnp.concatenate(heads, axis=-1) * _silu(z),)


def _f_attn(xq, kmem, vmem):
    heads = []
    for h in range(XA_HEADS):
        sl = slice(h * HEAD_DIM, (h + 1) * HEAD_DIM)
        s = lax.dot_general(xq[:, sl].astype(BF16), kmem[:, sl].astype(BF16),
                            (((1,), (1,)), ((), ())), preferred_element_type=F32) * (HEAD_DIM ** -0.5)
        m = lax.stop_gradient(jnp.max(s, axis=-1, keepdims=True))
        e = jnp.exp(s - m)
        p = e / jnp.sum(e, axis=-1, keepdims=True)
        heads.append(jnp.dot(p.astype(BF16), vmem[:, sl].astype(BF16), preferred_element_type=F32))
    return (jnp.concatenate(heads, axis=-1),)


def _f_s5_gelu(y, u, d):
    return (jax.nn.gelu(y + d * u),)


def _f_s5_gate(zg, t, b):
    return (zg * jax.nn.sigmoid(t + b),)


def _f_add(a, b):
    return (a + b,)


def _f_add4(a, b, c, d):
    return (((a + b.astype(F32)) + c.astype(F32)) + d.astype(F32),)


def _f_adamw(w, g, m, v):
    m = ADAM_B1 * m + (1.0 - ADAM_B1) * g
    v = ADAM_B2 * v + (1.0 - ADAM_B2) * jnp.square(g)
    m_hat = m / (1.0 - ADAM_B1 ** ADAM_STEP)
    v_hat = v / (1.0 - ADAM_B2 ** ADAM_STEP)
    delta = -ADAM_LR * (m_hat / (jnp.sqrt(v_hat) + ADAM_EPS) + ADAM_WD * w)
    return delta, m, v


def _conv_fwd(u, w, *, tm, name):
    length, chans = u.shape
    tm = min(tm, length)
    tc = _tile(chans, 1024)
    hb = tm // SUBLANES

    def body(cur_ref, prev_ref, w_ref, o_ref, buf):
        i = pl.program_id(1)
        buf[0:SUBLANES, :] = jnp.where(i > 0, prev_ref[...], 0.0)
        buf[SUBLANES:, :] = cur_ref[...]
        acc = buf[pl.ds(SUBLANES - 3, tm), :] * w_ref[0:1, :]
        for k in range(1, GDN_CONV):
            acc = acc + buf[pl.ds(SUBLANES - 3 + k, tm), :] * w_ref[k:k + 1, :]
        o_ref[...] = acc

    return pl.pallas_call(
        body, name=name, out_shape=jax.ShapeDtypeStruct(u.shape, F32),
        grid=(chans // tc, length // tm),
        in_specs=[pl.BlockSpec((tm, tc), lambda j, i: (i, j)),
                  pl.BlockSpec((SUBLANES, tc), lambda j, i: (jnp.maximum(i * hb - 1, 0), j)),
                  pl.BlockSpec((GDN_CONV, tc), lambda j, i: (0, j))],
        out_specs=pl.BlockSpec((tm, tc), lambda j, i: (i, j)),
        scratch_shapes=[pltpu.VMEM((tm + SUBLANES, tc), F32)],
        compiler_params=_params(("parallel", "parallel")),
    )(u, u, w)


def _conv_bwd(u, w, dc, *, tm, name):
    length, chans = u.shape
    tm = min(tm, length)
    tc = _tile(chans, 1024)
    hb = tm // SUBLANES
    last = length // tm - 1

    def body(u_ref, uprev_ref, dc_ref, dcnext_ref, w_ref, du_ref, dw_ref, ubuf, dbuf):
        i = pl.program_id(1)
        ubuf[0:SUBLANES, :] = jnp.where(i > 0, uprev_ref[...], 0.0)
        ubuf[SUBLANES:, :] = u_ref[...]
        dbuf[0:tm, :] = dc_ref[...]
        dbuf[tm:, :] = jnp.where(i < last, dcnext_ref[...], 0.0)
        dcv = dc_ref[...]
        du = dbuf[pl.ds(3, tm), :] * w_ref[0:1, :]
        rows = [jnp.sum(dcv * ubuf[pl.ds(SUBLANES - 3, tm), :], axis=0, keepdims=True)]
        for k in range(1, GDN_CONV):
            du = du + dbuf[pl.ds(3 - k, tm), :] * w_ref[k:k + 1, :]
            rows.append(jnp.sum(dcv * ubuf[pl.ds(SUBLANES - 3 + k, tm), :], axis=0, keepdims=True))
        du_ref[...] = du.astype(du_ref.dtype)
        dwv = jnp.concatenate(rows, axis=0)

        @pl.when(i == 0)
        def _():
            dw_ref[...] = dwv

        @pl.when(i > 0)
        def _():
            dw_ref[...] += dwv

    return pl.pallas_call(
        body, name=name,
        out_shape=(jax.ShapeDtypeStruct(u.shape, BF16), jax.ShapeDtypeStruct((GDN_CONV, chans), F32)),
        grid=(chans // tc, length // tm),
        in_specs=[pl.BlockSpec((tm, tc), lambda j, i: (i, j)),
                  pl.BlockSpec((SUBLANES, tc), lambda j, i: (jnp.maximum(i * hb - 1, 0), j)),
                  pl.BlockSpec((tm, tc), lambda j, i: (i, j)),
                  pl.BlockSpec((SUBLANES, tc), lambda j, i: (jnp.minimum((i + 1) * hb, (last + 1) * hb - 1), j)),
                  pl.BlockSpec((GDN_CONV, tc), lambda j, i: (0, j))],
        out_specs=(pl.BlockSpec((tm, tc), lambda j, i: (i, j)),
                   pl.BlockSpec((GDN_CONV, tc), lambda j, i: (0, j))),
        scratch_shapes=[pltpu.VMEM((tm + SUBLANES, tc), F32), pltpu.VMEM((tm + SUBLANES, tc), F32)],
        compiler_params=_params(("parallel", "arbitrary")),
    )(u, u, dc, dc, w)


def _dot(a, b, dims, precision=None):
    if precision is None:
        a, b = a.astype(BF16), b.astype(BF16)
    return lax.dot_general(a, b, (dims, ((), ())), preferred_element_type=F32, precision=precision)


def _dot3(a, b, dims):
    ah, bh = a.astype(BF16), b.astype(BF16)
    al, bl = (a - ah.astype(F32)).astype(BF16), (b - bh.astype(F32)).astype(BF16)

    def d(x, y):
        return lax.dot_general(x, y, (dims, ((), ())), preferred_element_type=F32)

    return d(ah, bh) + (d(ah, bl) + d(al, bh))


NN = ((1,), (0,))
NT = ((1,), (1,))
TN = ((0,), (0,))
HI = lax.Precision.HIGHEST


def _hmap(f, *lists):
    return [f(*t) for t in zip(*lists)]


@jax.custom_vjp
def _unit_lower_inverse(a):
    c = a[0].shape[0]
    eye = (lax.broadcasted_iota(jnp.int32, (c, c), 0) == lax.broadcasted_iota(jnp.int32, (c, c), 1)).astype(F32)
    p = _hmap(lambda x: -x, a)
    t = _hmap(lambda x: eye + x, p)
    for _ in range(int(math.log2(c)) - 1):
        p = _hmap(lambda x: _dot3(x, x, NN), p)
        t = _hmap(lambda x, y: x + _dot3(x, y, NN), t, p)
    return t


def _uli_fwd(a):
    t = _unit_lower_inverse(a)
    return t, t


def _uli_bwd(t, dt):
    left = _hmap(lambda x, y: _dot3(x, y, TN), t, dt)
    return (_hmap(lambda x, y: -_dot3(x, y, NT), left, t),)


_unit_lower_inverse.defvjp(_uli_fwd, _uli_bwd)


def _gdn_chunk(q, k, v, bl, al, a_log, dt_bias, state):
    c = q[0].shape[0]
    row = lax.broadcasted_iota(jnp.int32, (c, c), 0)
    col = lax.broadcasted_iota(jnp.int32, (c, c), 1)
    causal = row >= col
    strict = row > col
    eye = (row == col).astype(F32)
    beta = _hmap(jax.nn.sigmoid, bl)
    g = _hmap(lambda a_, l_, d_: -jnp.exp(a_) * jax.nn.softplus(l_ + d_), a_log, al, dt_bias)
    g_r = _hmap(lambda x: jnp.sum(eye * x, axis=0, keepdims=True), g)
    gc = _hmap(lambda x: jnp.sum(jnp.where(causal, x, 0.0), axis=1, keepdims=True), g_r)
    gc_r = _hmap(lambda x: jnp.sum(jnp.where(row <= col, x, 0.0), axis=0, keepdims=True), g)
    decay = _hmap(lambda x, y: jnp.where(causal, jnp.exp(jnp.where(causal, x - y, 0.0)), 0.0), gc, gc_r)
    e_gc = _hmap(jnp.exp, gc)
    kb = _hmap(jnp.multiply, k, beta)
    vb = _hmap(jnp.multiply, v, beta)
    a_mat = _hmap(lambda x, y, d: jnp.where(strict, _dot(x, y, NT) * d, 0.0), kb, k, decay)
    t_inv = _unit_lower_inverse(a_mat)
    u_blk = _hmap(lambda t, x: _dot(t, x, NN), t_inv, vb)
    w_blk = _hmap(lambda t, x, e: _dot(t, x * e, NN), t_inv, kb, e_gc)
    v_new = _hmap(lambda u, w, s: u - _dot(w, s, NN), u_blk, w_blk, state)
    attn = _hmap(lambda x, y, d: _dot(x, y, NT) * d, q, k, decay)
    o_state = _hmap(lambda x, e, s: _dot(x * e, s, NN), q, e_gc, state)
    o = _hmap(lambda base, at, vn: base + _dot(at, vn, NN), o_state, attn, v_new)
    g_last = _hmap(lambda x: jnp.sum(x, axis=0, keepdims=True), g)
    k_dec = _hmap(lambda x, gl, c_: x * jnp.exp(gl - c_), k, g_last, gc)
    new_state = _hmap(lambda s, gl, kd, vn: s * jnp.exp(gl) + _dot(kd, vn, TN), state, g_last, k_dec, v_new)
    return o, new_state


def _gdn_operands(q_ref, k_ref, v_ref, bav, alog_ref, dtb_ref):
    hs = range(GDN_HEADS)
    cols = [slice(h * HEAD_DIM, (h + 1) * HEAD_DIM) for h in hs]
    return ([q_ref[:, sl] for sl in cols], [k_ref[:, sl] for sl in cols], [v_ref[:, sl] for sl in cols],
            [bav[:, h:h + 1] for h in hs], [bav[:, h + GDN_HEADS:h + GDN_HEADS + 1] for h in hs],
            [alog_ref[h:h + 1, 0:1] for h in hs], [dtb_ref[h:h + 1, 0:1] for h in hs])


def _gdn_scan_fwd(q, k, v, ba, a_log, dt_bias, *, name):
    length = q.shape[0]
    n = length // GDN_CHUNK
    c = GDN_CHUNK

    def body(q_ref, k_ref, v_ref, ba_ref, alog_ref, dtb_ref, o_ref, s_ref, state):
        i = pl.program_id(0)

        @pl.when(i == 0)
        def _():
            state[...] = jnp.zeros_like(state)

        bav = ba_ref[...]
        heads = [slice(h * HEAD_DIM, (h + 1) * HEAD_DIM) for h in range(GDN_HEADS)]
        s_in = [state[h] for h in range(GDN_HEADS)]
        o, s_out = _gdn_chunk(*_gdn_operands(q_ref, k_ref, v_ref, bav, alog_ref, dtb_ref), s_in)
        for h, sl in enumerate(heads):
            s_ref[h] = s_in[h]
            o_ref[:, sl] = o[h]
            state[h] = s_out[h]

    row_spec = pl.BlockSpec((c, D_MODEL), lambda i: (i, 0))
    small = pl.BlockSpec((GDN_HEADS, LANES), lambda i: (0, 0))
    return pl.pallas_call(
        body, name=name,
        out_shape=(jax.ShapeDtypeStruct((length, D_MODEL), F32),
                   jax.ShapeDtypeStruct((n, GDN_HEADS, HEAD_DIM, HEAD_DIM), F32)),
        grid=(n,),
        in_specs=[row_spec, row_spec, row_spec, pl.BlockSpec((c, LANES), lambda i: (i, 0)), small, small],
        out_specs=(row_spec, pl.BlockSpec((None, GDN_HEADS, HEAD_DIM, HEAD_DIM), lambda i: (i, 0, 0, 0))),
        scratch_shapes=[pltpu.VMEM((GDN_HEADS, HEAD_DIM, HEAD_DIM), F32)],
        compiler_params=_params(("arbitrary",)),
    )(q, k, v, ba, a_log, dt_bias)


def _gdn_scan_bwd(q, k, v, ba, a_log, dt_bias, states, do, *, name):
    length = q.shape[0]
    n = length // GDN_CHUNK
    c = GDN_CHUNK

    def body(q_ref, k_ref, v_ref, ba_ref, alog_ref, dtb_ref, s_ref, do_ref,
             dq_ref, dk_ref, dv_ref, dba_ref, dalog_ref, ddtb_ref, dstate):
        i = pl.program_id(0)

        @pl.when(i == 0)
        def _():
            dstate[...] = jnp.zeros_like(dstate)
            dalog_ref[...] = jnp.zeros_like(dalog_ref)
            ddtb_ref[...] = jnp.zeros_like(ddtb_ref)

        bav = ba_ref[...]
        lane = lax.broadcasted_iota(jnp.int32, (c, LANES), 1)
        sub8 = lax.broadcasted_iota(jnp.int32, (GDN_HEADS, LANES), 0)
        lane8 = lax.broadcasted_iota(jnp.int32, (GDN_HEADS, LANES), 1)
        slab = jnp.zeros((c, LANES), F32)
        dalog_all = jnp.zeros((GDN_HEADS, LANES), F32)
        ddtb_all = jnp.zeros((GDN_HEADS, LANES), F32)
        heads = [slice(h * HEAD_DIM, (h + 1) * HEAD_DIM) for h in range(GDN_HEADS)]
        ds_in = [dstate[h] for h in range(GDN_HEADS)]
        s_in = [s_ref[h] for h in range(GDN_HEADS)]
        _, vjp = jax.vjp(_gdn_chunk, *_gdn_operands(q_ref, k_ref, v_ref, bav, alog_ref, dtb_ref), s_in)
        dq, dk, dv, dbl, dal, dalog, ddtb, ds = vjp(([do_ref[:, sl] for sl in heads], ds_in))
        for h, sl in enumerate(heads):
            dq_ref[:, sl] = dq[h]
            dk_ref[:, sl] = dk[h]
            dv_ref[:, sl] = dv[h]
            dstate[h] = ds[h]
            slab = slab + jnp.where(lane == h, dbl[h], 0.0) + jnp.where(lane == h + GDN_HEADS, dal[h], 0.0)
            here = (sub8 == h) & (lane8 == 0)
            dalog_all = dalog_all + jnp.where(here, dalog[h], 0.0)
            ddtb_all = ddtb_all + jnp.where(here, ddtb[h], 0.0)
        dba_ref[...] = slab
        dalog_ref[...] += dalog_all
        ddtb_ref[...] += ddtb_all

    row_spec = pl.BlockSpec((c, D_MODEL), lambda i: (n - 1 - i, 0))
    small = pl.BlockSpec((GDN_HEADS, LANES), lambda i: (0, 0))
    return pl.pallas_call(
        body, name=name,
        out_shape=(jax.ShapeDtypeStruct((length, D_MODEL), F32),) * 3
        + (jax.ShapeDtypeStruct((length, LANES), F32),
           jax.ShapeDtypeStruct((GDN_HEADS, LANES), F32), jax.ShapeDtypeStruct((GDN_HEADS, LANES), F32)),
        grid=(n,),
        in_specs=[row_spec, row_spec, row_spec,
                  pl.BlockSpec((c, LANES), lambda i: (n - 1 - i, 0)), small, small,
                  pl.BlockSpec((None, GDN_HEADS, HEAD_DIM, HEAD_DIM), lambda i: (n - 1 - i, 0, 0, 0)),
                  row_spec],
        out_specs=(row_spec, row_spec, row_spec,
                   pl.BlockSpec((c, LANES), lambda i: (n - 1 - i, 0)), small, small),
        scratch_shapes=[pltpu.VMEM((GDN_HEADS, HEAD_DIM, HEAD_DIM), F32)],
        compiler_params=_params(("arbitrary",)),
    )(q, k, v, ba, a_log, dt_bias, states, do)


S5_W = S5_T * LANES
S5_S = 2 * 8 * S5_STATE
S5_SH = S5_S // 2


def _iota2(shape):
    return lax.broadcasted_iota(jnp.int32, shape, 0), lax.broadcasted_iota(jnp.int32, shape, 1)


def _s5_rep_t(t, dtype):
    row, col = _iota2((S5_T * S5_GROUP, LANES))
    return ((jnp.right_shift(row, 4) == t) & (jnp.bitwise_and(row, 15) == jnp.bitwise_and(col, 15))).astype(dtype)


def _s5_rep_state(dtype):
    row, col = _iota2((2 * S5_STATE, S5_S))
    return ((jnp.right_shift(row, 6) == jnp.right_shift(col, 9))
            & (jnp.bitwise_and(row, 63) == jnp.bitwise_and(col, 63))).astype(dtype)


def _s5_masks():
    row, col = _iota2((LANES, LANES))
    m_ab = jnp.right_shift(row, 4) == jnp.right_shift(col, 4)
    row, col = _iota2((S5_S, LANES))
    m_e = jnp.bitwise_and(jnp.right_shift(row, 6), 7) == jnp.right_shift(col, 4)
    row, col = _iota2((LANES, S5_S))
    m_f = jnp.right_shift(row, 4) == jnp.bitwise_and(jnp.right_shift(col, 6), 7)
    return m_ab, m_e, m_f


def _s5_expand(kx_ref, ec_ref, fc_ref, kb_scr, e_scr, f_scr):
    m_ab, m_e, m_f = _s5_masks()
    kx = kx_ref[...].astype(BF16)
    ec = ec_ref[...].astype(BF16)
    rep_state = _s5_rep_state(BF16)
    for t in range(S5_T):
        rep = _s5_rep_t(t, BF16)
        cols = slice(t * LANES, (t + 1) * LANES)
        kb_scr[t] = jnp.where(m_ab, jnp.dot(kx, rep, preferred_element_type=F32), 0.0).astype(BF16)
        e_scr[:, cols] = jnp.where(m_e, jnp.dot(ec, rep, preferred_element_type=F32), 0.0).astype(BF16)
        f_scr[cols, :] = jnp.where(m_f, jnp.dot(fc_ref[t].astype(BF16), rep_state, preferred_element_type=F32),
                                   0.0).astype(BF16)


def _s5_token_rows(ref, n):
    return [ref[pl.ds(t, n, stride=S5_T), :].astype(BF16) for t in range(S5_T)]


def _s5_scan_fwd(u, kx, ec, fc, at, *, name):
    length = u.shape[0]
    n = length // S5_T
    assert n % SUBLANES == 0

    def body(u_ref, kx_ref, ec_ref, fc_ref, at_ref, y_ref, h_ref, kb_scr, e_scr, f_scr, g_scr):
        _s5_expand(kx_ref, ec_ref, fc_ref, kb_scr, e_scr, f_scr)
        us = _s5_token_rows(u_ref, n)
        g_scr[...] = jnp.dot(jnp.concatenate(us, axis=1), f_scr[...], preferred_element_type=F32)
        ar, ai = at_ref[:, :S5_SH], at_ref[:, S5_SH:]

        def step(blk, h):
            base = pl.multiple_of(blk * SUBLANES, SUBLANES)
            g8 = g_scr[pl.ds(base, SUBLANES), :]
            rows = []
            for r in range(SUBLANES):
                rows.append(h)
                hr, hi = h[:, :S5_SH], h[:, S5_SH:]
                h = jnp.concatenate([ar * hr - ai * hi, ar * hi + ai * hr], axis=1) + g8[r:r + 1, :]
            h_ref[pl.ds(base, SUBLANES), :] = jnp.concatenate(rows, axis=0)
            return h

        lax.fori_loop(0, n // SUBLANES, step, jnp.zeros((1, S5_S), F32))
        hb = h_ref[...].astype(BF16)
        for t in range(S5_T):
            acc = jnp.dot(hb, e_scr[:, t * LANES:(t + 1) * LANES], preferred_element_type=F32)
            for s in range(t + 1):
                acc = acc + jnp.dot(us[s], kb_scr[t - s], preferred_element_type=F32)
            y_ref[pl.ds(t, n, stride=S5_T), :] = acc

    return pl.pallas_call(
        body, name=name,
        out_shape=(jax.ShapeDtypeStruct((length, D_MODEL), F32), jax.ShapeDtypeStruct((S5_TILES, n, S5_S), F32)),
        grid=(S5_TILES,),
        in_specs=[pl.BlockSpec((length, LANES), lambda k: (0, k)), _s5_spec(LANES, S5_T * S5_GROUP),
                  _s5_spec(S5_S, S5_T * S5_GROUP), _s5_spec(S5_T, LANES, LANES), _s5_spec(1, S5_S)],
        out_specs=(pl.BlockSpec((length, LANES), lambda k: (0, k)), _s5_spec(n, S5_S)),
        scratch_shapes=[pltpu.VMEM((S5_T, LANES, LANES), BF16), pltpu.VMEM((S5_S, S5_W), BF16),
                        pltpu.VMEM((S5_W, S5_S), BF16), pltpu.VMEM((n, S5_S), F32)],
        compiler_params=_params(("parallel",)),
    )(u, kx, ec, fc, at)


def _s5_spec(*tail):
    return pl.BlockSpec((None,) + tail, lambda k: (k,) + (0,) * len(tail))


def _s5_scan_bwd(dy, kx, ec, fc, at, hs, *, name):
    length = dy.shape[0]
    n = length // S5_T

    def body(dy_ref, kx_ref, ec_ref, fc_ref, at_ref, h_ref, du_ref, dg_ref, dat_ref, kb_scr, e_scr, f_scr, dh_scr):
        _s5_expand(kx_ref, ec_ref, fc_ref, kb_scr, e_scr, f_scr)
        dys = _s5_token_rows(dy_ref, n)
        dh_scr[...] = _dot(jnp.concatenate(dys, axis=1), e_scr[...], NT)
        ar, ai = at_ref[:, :S5_SH], at_ref[:, S5_SH:]

        def step(it, carry):
            cy, dat = carry
            base = pl.multiple_of((n // SUBLANES - 1 - it) * SUBLANES, SUBLANES)
            dh8 = dh_scr[pl.ds(base, SUBLANES), :]
            h8 = h_ref[pl.ds(base, SUBLANES), :]
            rows = [None] * SUBLANES
            for r in reversed(range(SUBLANES)):
                rows[r] = cy
                cr, ci = cy[:, :S5_SH], cy[:, S5_SH:]
                hr, hi = h8[r:r + 1, :S5_SH], h8[r:r + 1, S5_SH:]
                dat = dat + jnp.concatenate([cr * hr + ci * hi, ci * hr - cr * hi], axis=1)
                cy = dh8[r:r + 1, :] + jnp.concatenate([ar * cr + ai * ci, ar * ci - ai * cr], axis=1)
            dg_ref[pl.ds(base, SUBLANES), :] = jnp.concatenate(rows, axis=0)
            return cy, dat

        zero = jnp.zeros((1, S5_S), F32)
        _, dat = lax.fori_loop(0, n // SUBLANES, step, (zero, zero))
        dat_ref[...] = dat
        dgb = dg_ref[...].astype(BF16)
        for s in range(S5_T):
            acc = _dot(dgb, f_scr[s * LANES:(s + 1) * LANES, :], NT)
            for t in range(s, S5_T):
                acc = acc + _dot(dys[t], kb_scr[t - s], NT)
            du_ref[pl.ds(s, n, stride=S5_T), :] = acc

    row_spec = pl.BlockSpec((length, LANES), lambda k: (0, k))
    return pl.pallas_call(
        body, name=name,
        out_shape=(jax.ShapeDtypeStruct((length, D_MODEL), F32), jax.ShapeDtypeStruct((S5_TILES, n, S5_S), F32),
                   jax.ShapeDtypeStruct((S5_TILES, 1, S5_S), F32)),
        grid=(S5_TILES,),
        in_specs=[row_spec, _s5_spec(LANES, S5_T * S5_GROUP), _s5_spec(S5_S, S5_T * S5_GROUP),
                  _s5_spec(S5_T, LANES, LANES), _s5_spec(1, S5_S), _s5_spec(n, S5_S)],
        out_specs=(row_spec, _s5_spec(n, S5_S), _s5_spec(1, S5_S)),
        scratch_shapes=[pltpu.VMEM((S5_T, LANES, LANES), BF16), pltpu.VMEM((S5_S, S5_W), BF16),
                        pltpu.VMEM((S5_W, S5_S), BF16), pltpu.VMEM((n, S5_S), F32)],
        compiler_params=_params(("parallel",)),
    )(dy, kx, ec, fc, at, hs)


def _s5_operator_grads(dy, u, hs, dg, *, name):
    length = u.shape[0]
    n = length // S5_T

    def body(dy_ref, u_ref, h_ref, dg_ref, dkx_ref, dec_ref, dfc_ref):
        dys = _s5_token_rows(dy_ref, n)
        us = _s5_token_rows(u_ref, n)
        ucat = jnp.concatenate(us, axis=1)
        m_ab, m_e, m_f = _s5_masks()
        hb = h_ref[...].astype(BF16)
        dgb = dg_ref[...].astype(BF16)
        lane = lax.broadcasted_iota(jnp.int32, (1, LANES), 1)
        lane_group = jnp.right_shift(lane, 4)

        def own_block(x, mask):
            x = jnp.where(mask, x, 0.0)
            for shift in (64, 32, 16):
                x = x + pltpu.roll(x, shift, 1)
            return x

        def place(halves, t, x):
            halves[t // 8] = jnp.where(lane_group == t % 8, x, halves[t // 8])

        dkb = [jnp.zeros((LANES, LANES), F32) for _ in range(S5_T)]
        dec = [jnp.zeros((S5_S, LANES), F32) for _ in range(2)]
        for t in range(S5_T):
            d_t = _dot(ucat, dys[t], TN)
            for s in range(t + 1):
                dkb[t - s] = dkb[t - s] + d_t[s * LANES:(s + 1) * LANES, :]
            place(dec, t, own_block(_dot(hb, dys[t], TN), m_e))
            wide = jnp.where(m_f, _dot(us[t], dgb, TN), 0.0)
            parts = []
            for r in range(2):
                acc = wide[:, r * S5_SH:r * S5_SH + LANES]
                for q in range(1, S5_SH // LANES):
                    acc = acc + wide[:, r * S5_SH + q * LANES:r * S5_SH + (q + 1) * LANES]
                parts.append(acc + pltpu.roll(acc, S5_STATE, 1))
            dfc_ref[t] = jnp.where(lane < S5_STATE, parts[0], parts[1])
        dkx = [jnp.zeros((LANES, LANES), F32) for _ in range(2)]
        for t in range(S5_T):
            place(dkx, t, own_block(dkb[t], m_ab))
        dkx_ref[...] = jnp.concatenate(dkx, axis=1)
        dec_ref[...] = jnp.concatenate(dec, axis=1)

    row_spec = pl.BlockSpec((length, LANES), lambda k: (0, k))
    outs = (_s5_spec(LANES, S5_T * S5_GROUP), _s5_spec(S5_S, S5_T * S5_GROUP), _s5_spec(S5_T, LANES, LANES))
    return pl.pallas_call(
        body, name=name,
        out_shape=(jax.ShapeDtypeStruct((S5_TILES, LANES, S5_T * S5_GROUP), F32),
                   jax.ShapeDtypeStruct((S5_TILES, S5_S, S5_T * S5_GROUP), F32),
                   jax.ShapeDtypeStruct((S5_TILES, S5_T, LANES, LANES), F32)),
        grid=(S5_TILES,),
        in_specs=[row_spec, row_spec, _s5_spec(n, S5_S), _s5_spec(n, S5_S)],
        out_specs=outs,
        compiler_params=_params(("parallel",)),
    )(dy, u, hs, dg)


def _s5_prep(a_re, a_im, b_re, b_im, c_re, c_im, log_dt):
    t_len, tiles = S5_T, S5_TILES
    dt = jnp.exp(log_dt)[:, None]
    mag = jnp.exp(a_re * dt)
    ab_re, ab_im = mag * jnp.cos(a_im * dt), mag * jnp.sin(a_im * dt)
    den = jnp.square(a_re) + jnp.square(a_im)
    n_re, n_im = ab_re - 1.0, ab_im
    f_re = (n_re * a_re + n_im * a_im) / den
    f_im = (n_im * a_re - n_re * a_im) / den
    bb_re = f_re[..., None] * b_re - f_im[..., None] * b_im
    bb_im = f_re[..., None] * b_im + f_im[..., None] * b_re

    def powers(exponents):
        e = exponents[:, None, None]
        m = jnp.exp(e * (a_re * dt))
        return m * jnp.cos(e * (a_im * dt)), m * jnp.sin(e * (a_im * dt))

    p_re, p_im = powers(jnp.arange(t_len + 1, dtype=F32))
    rev_re, rev_im = powers((t_len - 1) - jnp.arange(t_len, dtype=F32))
    ca_re = c_re[None] * p_re[:, :, None, :] - c_im[None] * p_im[:, :, None, :]
    ca_im = c_re[None] * p_im[:, :, None, :] + c_im[None] * p_re[:, :, None, :]
    lag = (jnp.einsum('tgip,gpj->tgij', ca_re[:t_len], bb_re, precision=HI)
           - jnp.einsum('tgip,gpj->tgij', ca_im[:t_len], bb_im, precision=HI))
    kx = lag.reshape(t_len, tiles, 8, S5_GROUP, S5_GROUP).transpose(1, 2, 4, 0, 3)
    kx = kx.reshape(tiles, LANES, t_len * S5_GROUP)
    e_st = jnp.stack([ca_re[1:], -ca_im[1:]])
    e_st = e_st.reshape(2, t_len, tiles, 8, S5_GROUP, S5_STATE).transpose(2, 0, 3, 5, 1, 4)
    ec = e_st.reshape(tiles, S5_S, t_len * S5_GROUP)
    ab_b = jnp.stack([rev_re[..., None] * bb_re[None] - rev_im[..., None] * bb_im[None],
                      rev_re[..., None] * bb_im[None] + rev_im[..., None] * bb_re[None]])
    ab_b = ab_b.reshape(2, t_len, tiles, 8, S5_STATE, S5_GROUP).transpose(2, 1, 3, 5, 0, 4)
    fc = ab_b.reshape(tiles, t_len, LANES, 2 * S5_STATE)
    a_t = jnp.stack([p_re[t_len], p_im[t_len]]).reshape(2, tiles, 8 * S5_STATE).transpose(1, 0, 2)
    return kx, ec, fc, a_t.reshape(tiles, 1, S5_S)


TM_ROW = 256


def _gdn_fwd(x, w, tag):
    qkv = _mm(x, w["wqkv"], name="gdn_proj_qkv")
    z = _mm(x, w["wz"], name="gdn_proj_z")
    ba = _mm(x, w["wba"], name="gdn_proj_ba")
    cv = _conv_fwd(qkv, w["conv_w"], tm=TM_ROW, name="gdn_conv")
    q, k, v = _rw_fwd(_f_gdn_qkv, [cv], [], tm=TM_ROW, name="gdn_qkv")
    o, states = _gdn_scan_fwd(q, k, v, ba, w["a_log8"], w["dt_bias8"], name="gdn_scan")
    (mix,) = _rw_fwd(_f_gdn_out, [o, z], [w["norm_g"]], tm=TM_ROW, name="gdn_out", out_dtypes=[BF16])
    return mix, (qkv, z, ba, cv, q, k, v, states, o)


def _gdn_bwd(x, w, saved, dmix, dx_acc):
    qkv, z, ba, cv, q, k, v, states, o = saved
    (do, dz), (dnorm_g,) = _rw_bwd(_f_gdn_out, [o, z], [w["norm_g"]], [dmix], row_grad=[1, 1], param_grad=[1],
                                   tm=TM_ROW, name="gdn_out_bwd", row_dtypes=[F32, BF16])
    dq, dk, dv, dba, dalog, ddtb = _gdn_scan_bwd(q, k, v, ba, w["a_log8"], w["dt_bias8"], states, do,
                                                  name="gdn_scan_bwd")
    (dcv,), _ = _rw_bwd(_f_gdn_qkv, [cv], [], [dq, dk, dv], row_grad=[1], param_grad=[], tm=TM_ROW,
                        name="gdn_qkv_bwd")
    dqkv, dconv_w = _conv_bwd(qkv, w["conv_w"], dcv, tm=TM_ROW, name="gdn_conv_bwd")
    dx = _mm(dqkv, w["wqkv"], tb=True, acc=dx_acc, name="gdn_dx_qkv")
    dx = _mm(dz, w["wz"], tb=True, acc=dx, name="gdn_dx_z")
    dx = _mm(dba, w["wba"], tb=True, acc=dx, name="gdn_dx_ba")
    grads = dict(wqkv=_mm(x, dqkv, ta=True, name="gdn_dw_qkv"), wz=_mm(x, dz, ta=True, name="gdn_dw_z"),
                 wba=_mm(x, dba, ta=True, name="gdn_dw_ba"), conv_w=dconv_w,
                 a_log=dalog[:, 0], dt_bias=ddtb[:, 0], norm_g=dnorm_g[0])
    return dx, grads


def _s5_fwd(x, w, tag):
    u = _mm(x, w["wu"], name="s5_proj_u")
    y, hs = _s5_scan_fwd(u, w["kx"], w["ec"], w["fc"], w["a_t"], name="s5_scan")
    (zg,) = _rw_fwd(_f_s5_gelu, [y, u], [w["d"]], tm=TM_ROW, name="s5_gelu")
    t = _mm(zg, w["w_glu"], name="s5_glu")
    (mix,) = _rw_fwd(_f_s5_gate, [zg, t], [w["b_glu"]], tm=TM_ROW, name="s5_gate", out_dtypes=[BF16])
    return mix, (u, hs, y, zg, t)


def _s5_bwd(x, w, saved, dmix, dx_acc):
    u, hs, y, zg, t = saved
    (dzg, dt), (db_glu,) = _rw_bwd(_f_s5_gate, [zg, t], [w["b_glu"]], [dmix], row_grad=[1, 1], param_grad=[1],
                                   tm=TM_ROW, name="s5_gate_bwd", row_dtypes=[F32, BF16])
    dzg = _mm(dt, w["w_glu"], tb=True, acc=dzg, name="s5_dzg")
    dw_glu = _mm(zg, dt, ta=True, name="s5_dw_glu")
    (dy, du), (dd,) = _rw_bwd(_f_s5_gelu, [y, u], [w["d"]], [dzg], row_grad=[1, 1], param_grad=[1],
                              tm=TM_ROW, name="s5_gelu_bwd")
    du_scan, dg, dat = _s5_scan_bwd(dy, w["kx"], w["ec"], w["fc"], w["a_t"], hs, name="s5_scan_bwd")
    dkx, dec, dfc = _s5_operator_grads(dy, u, hs, dg, name="s5_operator_grads")
    (du,) = _rw_fwd(_f_add, [du, du_scan], [], tm=TM_ROW, name="s5_du_add", out_dtypes=[BF16])
    d_a_re, d_a_im, d_b_re, d_b_im, d_c_re, d_c_im, d_log_dt = w["prep_vjp"]((dkx, dec, dfc, dat))
    dx = _mm(du, w["wu"], tb=True, acc=dx_acc, name="s5_dx_u")
    grads = dict(wu=_mm(x, du, ta=True, name="s5_dw_u"), w_glu=dw_glu, b_glu=db_glu[0], d=dd[0],
                 a_re=d_a_re, a_im=d_a_im, b_re=d_b_re, b_im=d_b_im, c_re=d_c_re, c_im=d_c_im, log_dt=d_log_dt)
    return dx, grads


def _layer_fwd(x, mem, w, is_gdn):
    mix, msave = (_gdn_fwd if is_gdn else _s5_fwd)(x, w, "")
    xq = _mm(x, w["wxq"], name="proj_xq")
    kv = _mm(mem, w["wkv"], name="mem_kv")
    kmem, vmem = kv[:, :XA_DIM], kv[:, XA_DIM:]
    (cross,) = _rw_fwd(_f_attn, [xq], [kmem, vmem], tm=TM_ROW, name="attn", out_dtypes=[BF16])
    h = _mm(mix, w["wo_mix"], name="wo_mix")
    h = _mm(cross, w["wo_cross"], acc=h, name="wo_cross")
    (x1,) = _rw_fwd(_f_ln_res, [x, h], [w["ln1_g"], w["ln1_b"]], tm=TM_ROW, name="ln_res")
    hm, act = _mm_relu2(x1, w["w1"], name="mlp_up")
    f = _mm(act, w["w2"], name="mlp_down")
    (x2,) = _rw_fwd(_f_ln_res, [x1, f], [w["ln2_g"], w["ln2_b"]], tm=TM_ROW, name="ln_res")
    return x2, (x, msave, xq, kmem, vmem, mix, cross, h, x1, hm, act, f)


def _layer_bwd(mem, w, is_gdn, saved, dx2, token=None):
    x, msave, xq, kmem, vmem, mix, cross, h, x1, hm, act, f = saved
    ln2_g = w["ln2_g"] if token is None else w["ln2_g"] + token[0, 0]
    (dx1, df), (dg2, db2) = _rw_bwd(_f_ln_res, [x1, f], [ln2_g, w["ln2_b"]], [dx2], row_grad=[1, 1],
                                    param_grad=[1, 1], tm=TM_ROW, name="ln_res_bwd", row_dtypes=[F32, BF16])
    dhm = _mm_relu2_grad(df, w["w2"], hm, name="mlp_dhm")
    dw2 = _mm(act, df, ta=True, name="mlp_dw2")
    dx1 = _mm(dhm, w["w1"], tb=True, acc=dx1, name="mlp_dx")
    dw1 = _mm(x1, dhm, ta=True, out_blocks=N_CHIPS, name="mlp_dw1")
    (dx, dh), (dg1, db1) = _rw_bwd(_f_ln_res, [x, h], [w["ln1_g"], w["ln1_b"]], [dx1], row_grad=[1, 1],
                                   param_grad=[1, 1], tm=TM_ROW, name="ln_res_bwd", row_dtypes=[F32, BF16])
    dmix =_mm(dh, w["wo_mix"], tb=True, name="wo_dmix")
    dcross = _mm(dh, w["wo_cross"], tb=True, name="wo_dcross")
    dwo = jnp.concatenate([_mm(mix, dh, ta=True, name="wo_dw_mix"), _mm(cross, dh, ta=True, name="wo_dw_cross")], 0)
    (dxq,), (dkmem, dvmem) = _rw_bwd(_f_attn, [xq], [kmem, vmem], [dcross], row_grad=[1], param_grad=[1, 1],
                                     tm=TM_ROW, name="attn_bwd", row_dtypes=[BF16])
    dwkv = _mm(mem, jnp.concatenate([dkmem, dvmem], axis=1), ta=True, name="mem_dw_kv")
    dx = _mm(dxq, w["wxq"], tb=True, acc=dx, name="dx_xq")
    dwxq = _mm(x, dxq, ta=True, name="dw_xq")
    dx, mg = (_gdn_bwd if is_gdn else _s5_bwd)(x, w, msave, dmix, dx)
    grads = dict(mixer=mg, wxq=dwxq, wkv=dwkv, wo=dwo, w1=dw1, w2=dw2,
                 ln1_g=dg1[0], ln1_b=db1[0], ln2_g=dg2[0], ln2_b=db2[0])
    return dx, grads


def _loss_and_grad(y, target):
    def f(yv, tv):
        err = yv - tv
        return (err * (1.0 / D_MODEL),), (0.5 / D_MODEL * jnp.sum(err * err, axis=0, keepdims=True),)

    (dy,), (part,) = _rowwise(f, [y, target], [], [(D_MODEL, F32)], [((1, D_MODEL), F32)], tm=512, name="loss")
    return jnp.sum(part), dy


def _layer_weights(full, i):
    j = i // 2
    w = dict(wkv=full["w_kv_mem"][i].astype(BF16),
             wo_mix=full["w_o"][i][:D_MODEL].astype(BF16), wo_cross=full["w_o"][i][D_MODEL:].astype(BF16),
             ln1_g=full["ln1_g"][i][None], ln1_b=full["ln1_b"][i][None],
             ln2_g=full["ln2_g"][i][None], ln2_b=full["ln2_b"][i][None],
             w1=full["mlp_w1"][i].astype(BF16), w2=full["mlp_w2"][i].astype(BF16))
    if i % 2 == 0:
        w_in = full["gdn_w_in"][j]
        gd = 3 * D_MODEL
        w.update(wqkv=w_in[:, :gd].astype(BF16), wz=w_in[:, gd:gd + D_MODEL].astype(BF16),
                 wba=jnp.pad(w_in[:, gd + D_MODEL:gd + D_MODEL + 2 * GDN_HEADS],
                             ((0, 0), (0, LANES - 2 * GDN_HEADS))).astype(BF16),
                 wxq=w_in[:, gd + D_MODEL + 2 * GDN_HEADS:].astype(BF16),
                 conv_w=full["gdn_conv_w"][j],
                 a_log8=jnp.broadcast_to(full["gdn_a_log"][j][:, None], (GDN_HEADS, LANES)),
                 dt_bias8=jnp.broadcast_to(full["gdn_dt_bias"][j][:, None], (GDN_HEADS, LANES)),
                 norm_g=full["gdn_norm_g"][j][None])
    else:
        w_in = full["s5_w_in"][j]
        (kx, ec, fc, a_t), prep_vjp = jax.vjp(
            _s5_prep, full["s5_a_re"][j], full["s5_a_im"][j], full["s5_b_re"][j], full["s5_b_im"][j],
            full["s5_c_re"][j], full["s5_c_im"][j], full["s5_log_dt"][j])
        w.update(wu=w_in[:, :D_MODEL].astype(BF16), wxq=w_in[:, D_MODEL:].astype(BF16),
                 kx=kx, ec=ec, fc=fc, a_t=a_t, prep_vjp=prep_vjp,
                 d=full["s5_d"][j][None], w_glu=full["s5_w_glu"][j].astype(BF16), b_glu=full["s5_b_glu"][j][None])
    return w


def _sharded_grads(l, i):
    m = l["mixer"]
    out = dict(w_kv_mem=l["wkv"], w_o=l["wo"], mlp_w1=l["w1"], mlp_w2=l["w2"])
    if i % 2 == 0:
        out.update(gdn_w_in=jnp.concatenate([m["wqkv"], m["wz"], m["wba"][:, :2 * GDN_HEADS], l["wxq"]], axis=1),
                   gdn_conv_w=m["conv_w"])
    else:
        out.update(s5_w_in=jnp.concatenate([m["wu"], l["wxq"]], axis=1), s5_d=m["d"], s5_w_glu=m["w_glu"],
                   s5_b_glu=m["b_glu"])
    return out


def _replicated_grads(layer_grads):
    g = layer_grads
    gdn = [g[i]["mixer"] for i in range(DEPTH) if i % 2 == 0]
    s5 = [g[i]["mixer"] for i in range(DEPTH) if i % 2 == 1]
    out = {n: jnp.stack([l[n] for l in g]) for n in ("ln1_g", "ln1_b", "ln2_g", "ln2_b")}
    out.update({"gdn_" + n: jnp.stack([m[n] for m in gdn]) for n in ("a_log", "dt_bias", "norm_g")})
    out.update({"s5_" + n: jnp.stack([m[n] for m in s5])
                for n in ("a_re", "a_im", "b_re", "b_im", "c_re", "c_im", "log_dt")})
    return out


def _local_step(x, mem, target, weights_of, grads_ready):
    lw, saves = [], []
    h = x
    for i in range(DEPTH):
        lw.append(weights_of(i, h))
        h, s = _layer_fwd(h, mem, lw[i], i % 2 == 0)
        saves.append(s)
    loss, d = _loss_and_grad(h, target)
    grads = [None] * DEPTH
    token = None
    for i in reversed(range(DEPTH)):
        d, grads[i] = _layer_bwd(mem, lw[i], i % 2 == 0, saves[i], d, token)
        token = grads_ready(i, grads[i])
    return loss, d, grads


ANY = pl.BlockSpec(memory_space=pl.ANY)
SHARD_ROWS = 1024
SMALL_ROWS = 128


def _place():
    return lax.axis_index("x"), lax.axis_index("y"), lax.axis_index("c")


def _other_chips(x, y):
    return [(1 - x, y), (x, 1 - y), (1 - x, 1 - y)]


def _all_gather_chips(wpack, *, name):
    rows = wpack.shape[0]
    half = rows // 2

    def body(w_ref, out_ref, send_sems, recv_sems):
        x, y, c = _place()
        sibling = (x, y, 1 - c)
        chips = _other_chips(x, y)

        def blk(cx, cy, cc):
            return out_ref.at[2 * cx + cy, pl.ds(cc * half, half), :]

        def copy(k, src, dst, to):
            return pltpu.make_async_remote_copy(src_ref=src, dst_ref=dst, send_sem=send_sems.at[k],
                                                recv_sem=recv_sems.at[k], device_id=to, device_id_type=MESH)

        first = [copy(j, w_ref.at[pl.ds(c * half, half), :], blk(x, y, c), (cx, cy, c))
                 for j, (cx, cy) in enumerate(chips)]
        for cp in first:
            cp.start()
        passed = [copy(3 + j, blk(cx, cy, c), blk(cx, cy, c), sibling) for j, (cx, cy) in enumerate(chips)]
        for j, (cx, cy) in enumerate(chips):
            copy(j, blk(cx, cy, c), blk(cx, cy, c), (cx, cy, c)).wait_recv()
            passed[j].start()
        for j, (cx, cy) in enumerate(chips):
            copy(3 + j, blk(cx, cy, 1 - c), blk(cx, cy, 1 - c), sibling).wait_recv()
        for cp in first + passed:
            cp.wait_send()

    return pl.pallas_call(
        body, name=name, out_shape=jax.ShapeDtypeStruct((N_CHIPS, rows, D_MODEL), wpack.dtype),
        in_specs=[ANY], out_specs=ANY,
        scratch_shapes=[pltpu.SemaphoreType.DMA((6,)), pltpu.SemaphoreType.DMA((6,))],
    )(wpack)


HBM = pl.BlockSpec(memory_space=pltpu.HBM)
SEM = pl.BlockSpec(memory_space=pltpu.SEMAPHORE)
DATAFLOW = pltpu.SideEffectType.DATAFLOW_SIDE_EFFECTING


def _gather_ici_copies(w_ref, land_ref, send_sems, recv_sems, outgoing):
    x, y, c = _place()
    half = w_ref.shape[0] // 2
    mine = pl.ds(c * half, half)
    return [pltpu.make_async_remote_copy(
        src_ref=w_ref.at[mine, :], dst_ref=land_ref.at[2 * x + y if outgoing else 2 * cx + cy, mine, :],
        send_sem=send_sems.at[j], recv_sem=recv_sems.at[j], device_id=(cx, cy, c), device_id_type=MESH)
        for j, (cx, cy) in enumerate(_other_chips(x, y))]


def _gather_start(wpack, after):
    rows = wpack.shape[0]

    def body(w_ref, land_ref, after_ref, send_sems, recv_sems, w_thru, land_thru, token):
        for cp in _gather_ici_copies(w_ref, land_ref, send_sems, recv_sems, outgoing=True):
            cp.start()
        token[...] = jnp.zeros_like(token)

    land = pltpu.with_memory_space_constraint(lax.empty((N_CHIPS, rows, D_MODEL), wpack.dtype), pltpu.HBM)
    return pl.pallas_call(
        body, name="gather_start",
        out_shape=(pltpu.SemaphoreType.DMA((3,)), pltpu.SemaphoreType.DMA((3,)), pltpu.HBM(wpack.shape, wpack.dtype),
                   pltpu.HBM(land.shape, land.dtype), jax.ShapeDtypeStruct((SUBLANES, LANES), F32)),
        in_specs=(HBM, HBM, ANY), out_specs=(SEM, SEM, HBM, HBM, pl.BlockSpec(memory_space=pltpu.VMEM)),
        input_output_aliases={0: 2, 1: 3},
        compiler_params=pltpu.CompilerParams(has_side_effects=DATAFLOW),
    )(pltpu.with_memory_space_constraint(wpack, pltpu.HBM), land, after)


def _gather_wait(send_sems, recv_sems, w_thru, land_thru, after):
    def body(w_ref, land_ref, send_sems, recv_sems, after_ref, w_dead, land_out):
        for cp in _gather_ici_copies(w_ref, land_ref, send_sems, recv_sems, outgoing=False):
            cp.wait_send()
            cp.wait_recv()

    return pl.pallas_call(
        body, name="gather_wait",
        out_shape=(pltpu.HBM(w_thru.shape, w_thru.dtype), pltpu.HBM(land_thru.shape, land_thru.dtype)),
        in_specs=(HBM, HBM, SEM, SEM, ANY), out_specs=(HBM, HBM), input_output_aliases={0: 0, 1: 1},
        compiler_params=pltpu.CompilerParams(has_side_effects=DATAFLOW),
    )(w_thru, land_thru, send_sems, recv_sems, after)[1]


def _gather_forward(land, *, name):
    rows = land.shape[1]
    half = rows // 2

    def body(in_ref, out_ref, send_sems, recv_sems):
        x, y, c = _place()

        def copy(j, cx, cy, cc):
            rows_of = out_ref.at[2 * cx + cy, pl.ds(cc * half, half), :]
            return pltpu.make_async_remote_copy(src_ref=rows_of, dst_ref=rows_of, send_sem=send_sems.at[j],
                                                recv_sem=recv_sems.at[j], device_id=(x, y, 1 - c), device_id_type=MESH)

        sends = [copy(j, cx, cy, c) for j, (cx, cy) in enumerate(_other_chips(x, y))]
        for cp in sends:
            cp.start()
        for j, (cx, cy) in enumerate(_other_chips(x, y)):
            copy(j, cx, cy, 1 - c).wait_recv()
        for cp in sends:
            cp.wait_send()

    return pl.pallas_call(
        body, name=name, out_shape=jax.ShapeDtypeStruct(land.shape, land.dtype), in_specs=[ANY], out_specs=ANY,
        input_output_aliases={0: 0},
        scratch_shapes=[pltpu.SemaphoreType.DMA((3,)), pltpu.SemaphoreType.DMA((3,))],
    )(land)


def _sibling_swap(buf, *, name):
    def body(in_ref, out_ref, send_sem, recv_sem):
        x, y, c = _place()
        cp = pltpu.make_async_remote_copy(src_ref=in_ref, dst_ref=out_ref, send_sem=send_sem, recv_sem=recv_sem,
                                          device_id=(x, y, 1 - c), device_id_type=MESH)
        cp.start()
        cp.wait()

    return pl.pallas_call(
        body, name=name, out_shape=jax.ShapeDtypeStruct(buf.shape, buf.dtype), in_specs=[ANY], out_specs=ANY,
        scratch_shapes=[pltpu.SemaphoreType.DMA, pltpu.SemaphoreType.DMA],
    )(buf)


def _pair_exchange(gpack, *, name):
    pieces, rows, width = gpack.shape
    half = rows // 2

    def body(in_ref, got_ref, send_sems, recv_sems):
        x, y, c = _place()
        sends = [pltpu.make_async_remote_copy(src_ref=in_ref.at[p, pl.ds((1 - c) * half, half), :],
                                              dst_ref=got_ref.at[p], send_sem=send_sems.at[p],
                                              recv_sem=recv_sems.at[p], device_id=(x, y, 1 - c), device_id_type=MESH)
                 for p in range(pieces)]
        for cp in sends:
            cp.start()
        for cp in sends:
            cp.wait()

    return pl.pallas_call(
        body, name=name, out_shape=jax.ShapeDtypeStruct((pieces, half, width), gpack.dtype),
        in_specs=[ANY], out_specs=ANY,
        scratch_shapes=[pltpu.SemaphoreType.DMA((pieces,)), pltpu.SemaphoreType.DMA((pieces,))],
    )(gpack)


def _pair_add(gpack, got, c, *, name, tm=512):
    pieces, rows, width = gpack.shape
    half = rows // 2
    nb = half // tm

    def body(c_ref, a_ref, b_ref, sum_ref, narrow_ref):
        s = a_ref[...] + b_ref[...]
        sum_ref[...] = s
        narrow_ref[...] = s.astype(BF16)

    blk = pl.BlockSpec((None, tm, width), lambda p, i, c_ref: (p, i, 0))
    return pl.pallas_call(
        body, name=name,
        out_shape=(jax.ShapeDtypeStruct((pieces, half, width), F32), jax.ShapeDtypeStruct((pieces, half, width), BF16)),
        grid_spec=pltpu.PrefetchScalarGridSpec(
            num_scalar_prefetch=1, grid=(pieces, nb),
            in_specs=[pl.BlockSpec((None, tm, width), lambda p, i, c_ref: (p, c_ref[0] * nb + i, 0)), blk],
            out_specs=(blk, blk)),
        compiler_params=_params(("parallel", "parallel")),
    )(c, gpack, got)


def _chip_exchange(pieces, *, name):
    _, rows, width = pieces.shape

    def body(in_ref, out_ref, send_sems, recv_sems):
        x, y, c = _place()
        cps = [pltpu.make_async_remote_copy(src_ref=in_ref.at[2 * cx + cy], dst_ref=out_ref.at[j],
                                            send_sem=send_sems.at[j], recv_sem=recv_sems.at[j],
                                            device_id=(cx, cy, c), device_id_type=MESH)
               for j, (cx, cy) in enumerate(_other_chips(x, y))]
        for cp in cps:
            cp.start()
        for cp in cps:
            cp.wait()

    return pl.pallas_call(
        body, name=name, out_shape=jax.ShapeDtypeStruct((3, rows, width), pieces.dtype), in_specs=[ANY], out_specs=ANY,
        scratch_shapes=[pltpu.SemaphoreType.DMA((3,)), pltpu.SemaphoreType.DMA((3,))],
    )(pieces)


def _chip_exchange_copies(in_ref, land_ref, send_sems, recv_sems):
    x, y, c = _place()
    return [pltpu.make_async_remote_copy(src_ref=in_ref.at[2 * cx + cy], dst_ref=land_ref.at[j],
                                         send_sem=send_sems.at[j], recv_sem=recv_sems.at[j],
                                         device_id=(cx, cy, c), device_id_type=MESH)
            for j, (cx, cy) in enumerate(_other_chips(x, y))]


def _chip_exchange_start(pieces):
    _, rows, width = pieces.shape

    def body(in_ref, land_ref, send_sems, recv_sems, in_thru, land_thru, token):
        for cp in _chip_exchange_copies(in_ref, land_ref, send_sems, recv_sems):
            cp.start()
        token[...] = jnp.zeros_like(token)

    land = pltpu.with_memory_space_constraint(lax.empty((3, rows, width), pieces.dtype), pltpu.HBM)
    return pl.pallas_call(
        body, name="rs_chip_start",
        out_shape=(pltpu.SemaphoreType.DMA((3,)), pltpu.SemaphoreType.DMA((3,)), pltpu.HBM(pieces.shape, pieces.dtype),
                   pltpu.HBM(land.shape, land.dtype), jax.ShapeDtypeStruct((SUBLANES, LANES), F32)),
        in_specs=(HBM, HBM), out_specs=(SEM, SEM, HBM, HBM, pl.BlockSpec(memory_space=pltpu.VMEM)),
        input_output_aliases={0: 2, 1: 3},
        compiler_params=pltpu.CompilerParams(has_side_effects=DATAFLOW),
    )(pltpu.with_memory_space_constraint(pieces, pltpu.HBM), land)


def _chip_exchange_wait(send_sems, recv_sems, in_thru, land_thru, after):
    def body(in_ref, land_ref, send_sems, recv_sems, after_ref, in_dead, land_out):
        for cp in _chip_exchange_copies(in_ref, land_ref, send_sems, recv_sems):
            cp.wait_send()
            cp.wait_recv()

    return pl.pallas_call(
        body, name="rs_chip_wait",
        out_shape=(pltpu.HBM(in_thru.shape, in_thru.dtype), pltpu.HBM(land_thru.shape, land_thru.dtype)),
        in_specs=(HBM, HBM, SEM, SEM, ANY), out_specs=(HBM, HBM), input_output_aliases={0: 0, 1: 1},
        compiler_params=pltpu.CompilerParams(has_side_effects=DATAFLOW),
    )(in_thru, land_thru, send_sems, recv_sems, after)[1]


def _all_reduce_small(v, *, name):
    rows, width = v.shape
    half = rows // 2
    assert half % SUBLANES == 0

    def body(in_ref, out_ref, pair_buf, chip_buf, send_sems, recv_sems):
        x, y, c = _place()
        sibling = (x, y, 1 - c)
        me = 2 * x + y
        mine = pl.ds(pl.multiple_of(c * half, SUBLANES), half)
        other = pl.ds(pl.multiple_of((1 - c) * half, SUBLANES), half)

        def copy(k, src, dst, to):
            return pltpu.make_async_remote_copy(src_ref=src, dst_ref=dst, send_sem=send_sems.at[k],
                                                recv_sem=recv_sems.at[k], device_id=to, device_id_type=MESH)

        swap = copy(0, in_ref.at[other, :], pair_buf, sibling)
        swap.start()
        swap.wait()
        chip_buf[me] = in_ref[mine, :] + pair_buf[...]
        chips = _other_chips(x, y)
        for j, (cx, cy) in enumerate(chips):
            copy(1 + j, chip_buf.at[me], chip_buf.at[me], (cx, cy, c)).start()
        for j, (cx, cy) in enumerate(chips):
            got = copy(1 + j, chip_buf.at[me], chip_buf.at[2 * cx + cy], (cx, cy, c))
            got.wait_send()
            got.wait_recv()
        out_ref[mine, :] = ((chip_buf[0] + chip_buf[1]) + chip_buf[2]) + chip_buf[3]
        share = copy(1 + len(chips), out_ref.at[mine, :], out_ref.at[mine, :], sibling)
        share.start()
        share.wait_send()
        copy(1 + len(chips), out_ref.at[other, :], out_ref.at[other, :], sibling).wait_recv()

    vmem = pl.BlockSpec(memory_space=pltpu.VMEM)
    return pl.pallas_call(
        body, name=name, out_shape=jax.ShapeDtypeStruct(v.shape, v.dtype), in_specs=[vmem], out_specs=vmem,
        scratch_shapes=[pltpu.VMEM((half, width), v.dtype), pltpu.VMEM((N_CHIPS, half, width), v.dtype),
                        pltpu.SemaphoreType.DMA((5,)), pltpu.SemaphoreType.DMA((5,))],
        compiler_params=pltpu.CompilerParams(vmem_limit_bytes=VMEM_LIMIT_V7X),
    )(v)


def _reduce_scatter_begin(gpack, behind):
    x, y, c = _place()
    got = _pair_exchange(gpack, name="rs_pair_swap")
    pair, pair16 = _pair_add(gpack, got, c.astype(jnp.int32).reshape(1), name="rs_pair_add")
    mine = lax.dynamic_index_in_dim(pair, 2 * x + y, axis=0, keepdims=False)
    if behind:
        *in_flight, token = _chip_exchange_start(pair16)
        return dict(mine=mine, in_flight=in_flight), token
    return dict(mine=mine, recv=_chip_exchange(pair16, name="rs_chip_exchange")), None


def _reduce_scatter_end(state, after=None):
    c = lax.axis_index("c")
    recv = state["recv"] if "recv" in state else _chip_exchange_wait(*state["in_flight"], after=after)
    (total,) = _rw_fwd(_f_add4, [state["mine"], recv[0], recv[1], recv[2]], [], tm=512, name="rs_chip_add")
    theirs = _sibling_swap(total, name="rs_share_swap")
    return jnp.concatenate([jnp.where(c == 0, total, theirs), jnp.where(c == 0, theirs, total)], axis=0)


_SHARDED = (("w_kv_mem", 1), ("w_o", 1), ("mlp_w1", 2), ("mlp_w2", 1), ("gdn_w_in", 2), ("gdn_conv_w", 2),
            ("s5_w_in", 2), ("s5_d", 1), ("s5_w_glu", 1), ("s5_b_glu", 1))
_MATMUL_ONLY = ("w_kv_mem", "w_o", "mlp_w1", "mlp_w2", "gdn_w_in", "s5_w_in", "s5_w_glu")
REDUCED_FIRST = (1, 2, 3)
_KEPT_BLOCKED = ("mlp_w1",)
_REPLICATED = ("ln1_g", "ln1_b", "ln2_g", "ln2_b", "gdn_a_log", "gdn_dt_bias", "gdn_norm_g", "s5_a_re", "s5_a_im",
               "s5_b_re", "s5_b_im", "s5_c_re", "s5_c_im", "s5_log_dt")
_WEIGHTS = ("w_kv_mem", "w_o", "ln1_g", "ln1_b", "ln2_g", "ln2_b", "mlp_w1", "mlp_w2", "gdn_w_in", "gdn_conv_w",
            "gdn_a_log", "gdn_dt_bias", "gdn_norm_g", "s5_w_in", "s5_a_re", "s5_a_im", "s5_b_re", "s5_b_im",
            "s5_c_re", "s5_c_im", "s5_log_dt", "s5_d", "s5_w_glu", "s5_b_glu")


ROW_ALIGN = 16


def _n_rows(shape):
    return -(-math.prod(shape) // (ROW_ALIGN * D_MODEL)) * ROW_ALIGN


def _as_rows(a):
    rows = _n_rows(a.shape)
    if a.shape[-1] == D_MODEL and a.size == rows * D_MODEL:
        return a.reshape(-1, D_MODEL)
    flat = a.reshape(-1)
    return jnp.pad(flat, (0, rows * D_MODEL - flat.size)).reshape(rows, D_MODEL)


def _pack(arrs, unit_rows=SHARD_ROWS):
    rows = [_as_rows(a) for a in arrs]
    pad = -sum(r.shape[0] for r in rows) % unit_rows
    if pad:
        rows.append(jnp.zeros((pad, D_MODEL), rows[0].dtype))
    return jnp.concatenate(rows, axis=0)


def _unpack(packed, shapes):
    lead = packed.shape[:-2]
    out, off = [], 0
    for s in shapes:
        r = _n_rows(s)
        seg = lax.slice_in_dim(packed, off, off + r, axis=len(lead))
        if s[-1] != D_MODEL or math.prod(s) != r * D_MODEL:
            seg = lax.slice_in_dim(seg.reshape(lead + (-1,)), 0, math.prod(s), axis=len(lead))
        out.append(seg.reshape(lead + tuple(s)))
        off += r
    return out


def _split3(t):
    hi = t.astype(BF16)
    r1 = t - hi.astype(F32)
    mid = r1.astype(BF16)
    lo = (r1 - mid.astype(F32)).astype(BF16)
    return jnp.stack([hi, mid, lo], axis=-1)


def _join3(t):
    return (t[..., 0].astype(F32) + t[..., 1].astype(F32)) + t[..., 2].astype(F32)


def _merge_chips(blocks, axis):
    return jnp.concatenate([blocks[s] for s in range(N_CHIPS)], axis=axis)


def _pack_for_chips(weights):
    rows = []
    for s in range(N_CHIPS):
        chip = []
        for layers, axis in weights:
            if axis is None:
                blocks = [g[s] for g in layers]
            else:
                n = layers[0].shape[axis] // N_CHIPS
                blocks = [lax.slice_in_dim(g, s * n, (s + 1) * n, axis=axis) for g in layers]
            if math.prod(blocks[0].shape) % (ROW_ALIGN * D_MODEL) == 0:
                chip += [_as_rows(b) for b in blocks]
            else:
                chip.append(_as_rows(jnp.stack(blocks)))
        pad = -sum(r.shape[0] for r in chip) % SHARD_ROWS
        rows += chip + ([jnp.zeros((pad, D_MODEL), F32)] if pad else [])
    return jnp.concatenate(rows, axis=0).reshape(N_CHIPS, -1, D_MODEL)


def kernel(x, mem, w_kv_mem, w_o, ln1_g, ln1_b, ln2_g, ln2_b, mlp_w1, mlp_w2, gdn_w_in, gdn_conv_w, gdn_a_log, gdn_dt_bias, gdn_norm_g, s5_w_in, s5_a_re, s5_a_im, s5_b_re, s5_b_im, s5_c_re, s5_c_im, s5_log_dt, s5_d, s5_w_glu, s5_b_glu, loss_target, m_w_kv_mem, m_w_o, m_ln1_g, m_ln1_b, m_ln2_g, m_ln2_b, m_mlp_w1, m_mlp_w2, m_gdn_w_in, m_gdn_conv_w, m_gdn_a_log, m_gdn_dt_bias, m_gdn_norm_g, m_s5_w_in, m_s5_a_re, m_s5_a_im, m_s5_b_re, m_s5_b_im, m_s5_c_re, m_s5_c_im, m_s5_log_dt, m_s5_d, m_s5_w_glu, m_s5_b_glu, v_w_kv_mem, v_w_o, v_ln1_g, v_ln1_b, v_ln2_g, v_ln2_b, v_mlp_w1, v_mlp_w2, v_gdn_w_in, v_gdn_conv_w, v_gdn_a_log, v_gdn_dt_bias, v_gdn_norm_g, v_s5_w_in, v_s5_a_re, v_s5_a_im, v_s5_b_re, v_s5_b_im, v_s5_c_re, v_s5_c_im, v_s5_log_dt, v_s5_d, v_s5_w_glu, v_s5_b_glu):
    given = dict(locals())
    w = {n: given[n] for n in _WEIGHTS}
    mom = {n: given["m_" + n] for n in _WEIGHTS}
    var = {n: given["v_" + n] for n in _WEIGHTS}
    shard_names = [n for n, _ in _SHARDED]
    shard_shapes = [w[n].shape for n in shard_names]
    rep_shapes = [w[n].shape for n in _REPLICATED]

    wire = {n: w[n].astype(BF16) if n in _MATMUL_ONLY else _split3(w[n]) for n in shard_names}
    first = {n: 0 if n.startswith("s5_") else 1 for n in shard_names}
    me_chip = 2 * lax.axis_index("x") + lax.axis_index("y")
    early = [wire[n][:first[n]] for n in shard_names if first[n]]
    late = [wire[n][first[n]:] for n in shard_names]
    early_pack, late_pack = _pack(early), _pack(late)
    landed = _all_gather_chips(early_pack, name="gather_first_layer")
    landed = lax.dynamic_update_index_in_dim(landed, early_pack, me_chip, axis=0)
    early_blocks = dict(zip([n for n in shard_names if first[n]], _unpack(landed, [a.shape for a in early])))
    send_sems, recv_sems, pack_thru, land_thru, token = _gather_start(late_pack, after=landed)
    axis_of = dict(_SHARDED)

    def merged(n, blk):
        if n in _KEPT_BLOCKED:
            return blk
        return _merge_chips(blk if n in _MATMUL_ONLY else _join3(blk), axis_of[n] - 1)

    late_full = {}

    def weights_of(i, h):
        if i == 0:
            full = {n: [merged(n, blk[:, 0])] for n, blk in early_blocks.items()}
            full["gdn_w_in"][0] = full["gdn_w_in"][0] + token[0, 0].astype(BF16)
        else:
            if not late_full:
                land = _gather_wait(send_sems, recv_sems, pack_thru, land_thru, after=h)
                land = _gather_forward(land, name="gather_forward")
                land = lax.dynamic_update_index_in_dim(land, late_pack, me_chip, axis=0)
                for n, blk in zip(shard_names, _unpack(land, [a.shape for a in late])):
                    late_full[n] = [None] * first[n] + [merged(n, blk[:, t]) for t in range(blk.shape[1])]
            full = dict(late_full)
        full.update({n: w[n] for n in _REPLICATED})
        return _layer_weights(full, i)

    sharded = {}
    in_flight = {}

    def group_pack(layers):
        names = [n for n in shard_names if any(n in sharded[i] for i in layers)]
        per_weight = [[sharded[i][n] for i in layers if n in sharded[i]] for n in names]
        pack = _pack_for_chips([(g, None if n in _KEPT_BLOCKED else axis_of[n] - 1) for n, g in zip(names, per_weight)])
        return pack, names, [(len(g),) + w[n].shape[1:] for n, g in zip(names, per_weight)]

    def grads_ready(i, g):
        sharded[i] = _sharded_grads(g, i)
        if i != REDUCED_FIRST[0]:
            return None
        pack, names, shapes = group_pack(REDUCED_FIRST)
        state, token = _reduce_scatter_begin(pack, behind=True)
        in_flight.update(state=state, names=names, shapes=shapes)
        return token

    loss, grad_x, layer_grads = _local_step(x[0], mem[0], loss_target[0], weights_of, grads_ready)
    loss = lax.psum(loss, ("x", "y", "c"))
    rest = [i for i in range(DEPTH) if i not in REDUCED_FIRST]
    pack, names, shapes = group_pack(rest)
    state, _ = _reduce_scatter_begin(pack, behind=False)
    pieces = {n: [] for n in shard_names}
    for n, g in zip(names, _unpack(_reduce_scatter_end(state), shapes)):
        pieces[n].append(g)
    late = _reduce_scatter_end(in_flight["state"], after=grad_x)
    for n, g in zip(in_flight["names"], _unpack(late, in_flight["shapes"])):
        pieces[n].append(g)
    g_shards = [p[0] if len(p) == 1 else jnp.concatenate(p, axis=0) for p in (pieces[n] for n in shard_names)]

    def pack_small(d):
        return _pack([d[n] for n in _REPLICATED], unit_rows=SMALL_ROWS)

    g_rep = _all_reduce_small(pack_small(_replicated_grads(layer_grads)), name="reduce_replicated")

    def adamw(wp, gp, mp, vp, name):
        return _rw_fwd(_f_adamw, [wp, gp, mp, vp], [], tm=256, name=name)

    outs = {}
    for n, g in zip(shard_names, g_shards):
        flat = (-1, w[n].shape[-1])
        res = adamw(w[n].reshape(flat), g.reshape(flat), mom[n].reshape(flat), var[n].reshape(flat), "adamw_" + n)
        outs[("grad", n)] = g
        outs.update({(kind, n): a.reshape(w[n].shape) for kind, a in zip(("delta", "new_m", "new_v"), res)})
    packed = (g_rep,) + tuple(adamw(pack_small(w), g_rep, pack_small(mom), pack_small(var), "adamw_replicated"))
    for kind, pr in zip(("grad", "delta", "new_m", "new_v"), packed):
        outs.update({(kind, n): a for n, a in zip(_REPLICATED, _unpack(pr, rep_shapes))})
    return (loss, grad_x[None]) + tuple(outs[(kind, n)] for kind in ("grad", "delta", "new_m", "new_v")
                                        for n in _WEIGHTS)
```

```python
import functools
import math

import jax
import jax.numpy as jnp
from jax import lax
from jax.experimental import pallas as pl
from jax.experimental.pallas import tpu as pltpu

F32 = jnp.float32
BF16 = jnp.bfloat16
MESH = pl.DeviceIdType.MESH

D_MODEL = 1024
DEPTH = 4
GDN_HEADS = 8
HEAD_DIM = 128
GDN_CONV = 4
GDN_CHUNK = 64
S5_GROUPS = 64
S5_GROUP = 16
S5_STATE = 64
XA_HEADS = 4
XA_DIM = 512
D_FF = 4096
DN_ALPHA = (2 * DEPTH) ** 0.25
LN_EPS = 1e-5
RMS_EPS = 1e-6
ADAM_LR, ADAM_B1, ADAM_B2, ADAM_EPS, ADAM_WD, ADAM_STEP = 0.001, 0.9, 0.999, 1e-08, 0.01, 10

VMEM_LIMIT_V7X = 56 * 1024 * 1024
LANES = 128
SUBLANES = 8
S5_T = 16
S5_TILES = D_MODEL // LANES
N_CHIPS = 4
N_DEV = 8


def _params(sem):
    return pltpu.CompilerParams(dimension_semantics=sem, vmem_limit_bytes=VMEM_LIMIT_V7X)


def _tile(n, pref):
    if n <= pref:
        return n
    t = (pref // LANES) * LANES
    while n % t:
        t -= LANES
    return t


def _row_tile(n, pref):
    if n % SUBLANES:
        return n
    t = min(pref, n) // SUBLANES * SUBLANES
    while n % t:
        t -= SUBLANES
    return t


def _col_blocked_spec(rows_tile, cols_tile, block_cols, rows_axis, cols_axis):
    r = block_cols // cols_tile

    def index(*ijk):
        c = ijk[cols_axis]
        return (c, ijk[rows_axis], 0) if r == 1 else (c // r, ijk[rows_axis], c % r)

    return pl.BlockSpec((None, rows_tile, cols_tile), index)


def _mm(a, b, *, ta=False, tb=False, acc=None, name, tm=1024, tn=1024, tk=None, out_blocks=0):
    if tk is None:
        tk = 4096 if a.dtype == BF16 and b.dtype == BF16 else 2048
    k_dim, m_dim = a.shape if ta else a.shape[::-1]
    b_rows, b_cols = (b.shape[0], b.shape[1]) if b.ndim == 2 else (b.shape[1], b.shape[0] * b.shape[2])
    n_dim = b_rows if tb else b_cols
    assert (b_cols if tb else b_rows) == k_dim, (a.shape, b.shape, ta, tb)
    limit_n = n_dim // out_blocks if out_blocks else (n_dim if b.ndim == 2 or tb else b.shape[2])
    limit_k = b.shape[2] if (b.ndim == 3 and tb) else k_dim
    tm, tn, tk = _tile(m_dim, tm), _tile(limit_n, min(tn, limit_n)), _tile(limit_k, min(tk, limit_k))
    a_spec = (pl.BlockSpec((tk, tm), lambda i, j, k: (k, i)) if ta else pl.BlockSpec((tm, tk), lambda i, j, k: (i, k)))
    if b.ndim == 3:
        b_spec = (_col_blocked_spec(tn, tk, b.shape[2], 1, 2) if tb else _col_blocked_spec(tk, tn, b.shape[2], 2, 1))
    else:
        b_spec = (pl.BlockSpec((tn, tk), lambda i, j, k: (j, k)) if tb
                  else pl.BlockSpec((tk, tn), lambda i, j, k: (k, j)))
    o_spec = (_col_blocked_spec(tm, tn, n_dim // out_blocks, 0, 1) if out_blocks
              else pl.BlockSpec((tm, tn), lambda i, j, k: (i, j)))
    o_shape = (out_blocks, m_dim, n_dim // out_blocks) if out_blocks else (m_dim, n_dim)
    dn = (((0 if ta else 1,), (1 if tb else 0,)), ((), ()))
    has_acc = acc is not None

    def body(*refs):
        a_ref, b_ref = refs[0], refs[1]
        o_ref = refs[-1]
        k = pl.program_id(2)
        p = lax.dot_general(a_ref[...].astype(BF16), b_ref[...].astype(BF16), dn,
                            preferred_element_type=F32)

        @pl.when(k == 0)
        def _():
            o_ref[...] = p + refs[2][...] if has_acc else p

        @pl.when(k > 0)
        def _():
            o_ref[...] += p

    return pl.pallas_call(
        body, name=name,
        out_shape=jax.ShapeDtypeStruct(o_shape, F32),
        grid=(m_dim // tm, n_dim // tn, k_dim // tk),
        in_specs=[a_spec, b_spec] + ([o_spec] if has_acc else []),
        out_specs=o_spec,
        compiler_params=_params(("parallel", "parallel", "arbitrary")),
    )(*([a, b] + ([acc] if has_acc else [])))


def _mm_relu2(a, b, *, name, tm=1024):
    m_dim, k_dim = a.shape
    n_blocks, _, tn = b.shape
    n_dim = n_blocks * tn
    tm = _tile(m_dim, tm)

    def body(a_ref, b_ref, h_ref, act_ref):
        h = jnp.dot(a_ref[...].astype(BF16), b_ref[...].astype(BF16), preferred_element_type=F32)
        h_ref[...] = h.astype(h_ref.dtype)
        r = jnp.maximum(h, 0.0)
        act_ref[...] = (r * r).astype(BF16)

    o_spec = pl.BlockSpec((tm, tn), lambda i, j: (i, j))
    return pl.pallas_call(
        body, name=name,
        out_shape=(jax.ShapeDtypeStruct((m_dim, n_dim), BF16), jax.ShapeDtypeStruct((m_dim, n_dim), BF16)),
        grid=(m_dim // tm, n_dim // tn),
        in_specs=[pl.BlockSpec((tm, k_dim), lambda i, j: (i, 0)),
                  pl.BlockSpec((None, k_dim, tn), lambda i, j: (j, 0, 0))],
        out_specs=(o_spec, o_spec),
        compiler_params=_params(("parallel", "parallel")),
    )(a, b)


def _mm_relu2_grad(d, b, h, *, name, tm=1024, tn=1024):
    m_dim, k_dim = d.shape
    n_dim = b.shape[0]
    tm, tn = _tile(m_dim, tm), _tile(n_dim, tn)

    def body(d_ref, b_ref, h_ref, o_ref):
        p = lax.dot_general(d_ref[...].astype(BF16), b_ref[...].astype(BF16), ((NT), ((), ())),
                            preferred_element_type=F32)
        o_ref[...] = (p * (2.0 * jnp.maximum(h_ref[...].astype(F32), 0.0))).astype(BF16)

    o_spec = pl.BlockSpec((tm, tn), lambda i, j: (i, j))
    return pl.pallas_call(
        body, name=name,
        out_shape=jax.ShapeDtypeStruct((m_dim, n_dim), BF16),
        grid=(m_dim // tm, n_dim // tn),
        in_specs=[pl.BlockSpec((tm, k_dim), lambda i, j: (i, 0)), pl.BlockSpec((tn, k_dim), lambda i, j: (j, 0)), o_spec],
        out_specs=o_spec,
        compiler_params=_params(("parallel", "parallel")),
    )(d, b, h)


def _rowwise(f, rows, params, row_out, acc_out, *, tm, name):
    length = rows[0].shape[0]
    tm = _row_tile(length, tm)
    nr, npar, nro = len(rows), len(params), len(row_out)

    def body(*refs):
        ins = [r[...] for r in refs[:nr + npar]]
        outs = refs[nr + npar:]
        r_o, a_o = f(*ins)
        for ref, val in zip(outs[:nro], r_o):
            ref[...] = val.astype(ref.dtype)
        i = pl.program_id(0)
        for ref, val in zip(outs[nro:], a_o):
            @pl.when(i == 0)
            def _(ref=ref, val=val):
                ref[...] = val.astype(ref.dtype)

            @pl.when(i > 0)
            def _(ref=ref, val=val):
                ref[...] += val.astype(ref.dtype)

    in_specs = ([pl.BlockSpec((tm, r.shape[1]), lambda i: (i, 0)) for r in rows]
                + [pl.BlockSpec(p.shape, lambda i: (0, 0)) for p in params])
    out_specs = ([pl.BlockSpec((tm, w), lambda i: (i, 0)) for w, _ in row_out]
                 + [pl.BlockSpec(s, lambda i: (0, 0)) for s, _ in acc_out])
    out_shape = ([jax.ShapeDtypeStruct((length, w), dt) for w, dt in row_out]
                 + [jax.ShapeDtypeStruct(s, dt) for s, dt in acc_out])
    res = pl.pallas_call(
        body, name=name, out_shape=out_shape, grid=(length // tm,),
        in_specs=in_specs, out_specs=out_specs,
        compiler_params=_params(("arbitrary",) if acc_out else ("parallel",)),
    )(*rows, *params)
    return res[:nro], res[nro:]


def _rw_fwd(f, rows, params, *, tm, name, out_dtypes=None):
    tm_ = _row_tile(rows[0].shape[0], tm)
    shapes = jax.eval_shape(f, *[jax.ShapeDtypeStruct((tm_, r.shape[1]), r.dtype) for r in rows],
                            *[jax.ShapeDtypeStruct(p.shape, p.dtype) for p in params])
    row_out = [(s.shape[1], s.dtype if out_dtypes is None else dt)
               for s, dt in zip(shapes, out_dtypes or shapes)]
    outs, _ = _rowwise(lambda *v: (f(*v), ()), rows, params, row_out, [], tm=tm, name=name)
    return outs


def _rw_bwd(f, rows, params, cots, *, row_grad, param_grad, tm, name, row_dtypes=None):
    nr, npar, nct = len(rows), len(params), len(cots)

    def g(*vals):
        prim = vals[:nr] + vals[nr + nct:]
        ct = vals[nr:nr + nct]
        _, vjp = jax.vjp(f, *prim)
        grads = vjp(tuple(ct))
        return (tuple(grads[i] for i in range(nr) if row_grad[i]),
                tuple(grads[nr + i] for i in range(npar) if param_grad[i]))

    widths = [rows[i].shape[1] for i in range(nr) if row_grad[i]]
    row_out = list(zip(widths, row_dtypes or [F32] * len(widths)))
    acc_out = [(params[i].shape, F32) for i in range(npar) if param_grad[i]]
    return _rowwise(g, list(rows) + list(cots), params, row_out, acc_out, tm=tm, name=name)


def _f_ln_res(x, h, g, b):
    pre = DN_ALPHA * x + h
    mu = jnp.mean(pre, axis=-1, keepdims=True)
    d = pre - mu
    var = jnp.mean(d * d, axis=-1, keepdims=True)
    return (d * lax.rsqrt(var + LN_EPS) * g + b,)


def _silu(t):
    return t * jax.nn.sigmoid(t)


def _f_gdn_qkv(c):
    a = _silu(c)
    outs = []
    for part, scale in ((0, HEAD_DIM ** -0.5), (1, 1.0)):
        heads = []
        for h in range(GDN_HEADS):
            t = a[:, part * D_MODEL + h * HEAD_DIM: part * D_MODEL + (h + 1) * HEAD_DIM]
            t = t * lax.rsqrt(jnp.sum(t * t, axis=-1, keepdims=True) + 1e-6)
            heads.append(t * scale if scale != 1.0 else t)
        outs.append(jnp.concatenate(heads, axis=-1))
    outs.append(a[:, 2 * D_MODEL:])
    return tuple(outs)


def _f_gdn_out(o, z, norm_g):
    heads = []
    for h in range(GDN_HEADS):
        t = o[:, h * HEAD_DIM:(h + 1) * HEAD_DIM]
        t = t * lax.rsqrt(jnp.mean(t * t, axis=-1, keepdims=True) + RMS_EPS) * norm_g
        heads.append(t)
    return (jnp.concatenate(heads, axis=-1) * _silu(z),)


def _f_attn(xq, kmem, vmem):
    heads = []
    for h in range(XA_HEADS):
        sl = slice(h * HEAD_DIM, (h + 1) * HEAD_DIM)
        s = lax.dot_general(xq[:, sl].astype(BF16), kmem[:, sl].astype(BF16),
                            (((1,), (1,)), ((), ())), preferred_element_type=F32) * (HEAD_DIM ** -0.5)
        m = lax.stop_gradient(jnp.max(s, axis=-1, keepdims=True))
        e = jnp.exp(s - m)
        p = e / jnp.sum(e, axis=-1, keepdims=True)
        heads.append(jnp.dot(p.astype(BF16), vmem[:, sl].astype(BF16), preferred_element_type=F32))
    return (jnp.concatenate(heads, axis=-1),)


def _f_s5_gelu(y, u, d):
    return (jax.nn.gelu(y + d * u),)


def _f_s5_gate(zg, t, b):
    return (zg * jax.nn.sigmoid(t + b),)


def _f_add(a, b):
    return (a + b,)


def _f_add4(a, b, c, d):
    return (((a + b.astype(F32)) + c.astype(F32)) + d.astype(F32),)


def _f_adamw(w, g, m, v):
    m = ADAM_B1 * m + (1.0 - ADAM_B1) * g
    v = ADAM_B2 * v + (1.0 - ADAM_B2) * jnp.square(g)
    m_hat = m / (1.0 - ADAM_B1 ** ADAM_STEP)
    v_hat = v / (1.0 - ADAM_B2 ** ADAM_STEP)
    delta = -ADAM_LR * (m_hat / (jnp.sqrt(v_hat) + ADAM_EPS) + ADAM_WD * w)
    return delta, m, v


def _conv_fwd(u, w, *, tm, name):
    length, chans = u.shape
    tm = min(tm, length)
    tc = _tile(chans, 1024)
    hb = tm // SUBLANES

    def body(cur_ref, prev_ref, w_ref, o_ref, buf):
        i = pl.program_id(1)
        buf[0:SUBLANES, :] = jnp.where(i > 0, prev_ref[...], 0.0)
        buf[SUBLANES:, :] = cur_ref[...]
        acc = buf[pl.ds(SUBLANES - 3, tm), :] * w_ref[0:1, :]
        for k in range(1, GDN_CONV):
            acc = acc + buf[pl.ds(SUBLANES - 3 + k, tm), :] * w_ref[k:k + 1, :]
        o_ref[...] = acc

    return pl.pallas_call(
        body, name=name, out_shape=jax.ShapeDtypeStruct(u.shape, F32),
        grid=(chans // tc, length // tm),
        in_specs=[pl.BlockSpec((tm, tc), lambda j, i: (i, j)),
                  pl.BlockSpec((SUBLANES, tc), lambda j, i: (jnp.maximum(i * hb - 1, 0), j)),
                  pl.BlockSpec((GDN_CONV, tc), lambda j, i: (0, j))],
        out_specs=pl.BlockSpec((tm, tc), lambda j, i: (i, j)),
        scratch_shapes=[pltpu.VMEM((tm + SUBLANES, tc), F32)],
        compiler_params=_params(("parallel", "parallel")),
    )(u, u, w)


def _conv_bwd(u, w, dc, *, tm, name):
    length, chans = u.shape
    tm = min(tm, length)
    tc = _tile(chans, 1024)
    hb = tm // SUBLANES
    last = length // tm - 1

    def body(u_ref, uprev_ref, dc_ref, dcnext_ref, w_ref, du_ref, dw_ref, ubuf, dbuf):
        i = pl.program_id(1)
        ubuf[0:SUBLANES, :] = jnp.where(i > 0, uprev_ref[...], 0.0)
        ubuf[SUBLANES:, :] = u_ref[...]
        dbuf[0:tm, :] = dc_ref[...]
        dbuf[tm:, :] = jnp.where(i < last, dcnext_ref[...], 0.0)
        dcv = dc_ref[...]
        du = dbuf[pl.ds(3, tm), :] * w_ref[0:1, :]
        rows = [jnp.sum(dcv * ubuf[pl.ds(SUBLANES - 3, tm), :], axis=0, keepdims=True)]
        for k in range(1, GDN_CONV):
            du = du + dbuf[pl.ds(3 - k, tm), :] * w_ref[k:k + 1, :]
            rows.append(jnp.sum(dcv * ubuf[pl.ds(SUBLANES - 3 + k, tm), :], axis=0, keepdims=True))
        du_ref[...] = du.astype(du_ref.dtype)
        dwv = jnp.concatenate(rows, axis=0)

        @pl.when(i == 0)
        def _():
            dw_ref[...] = dwv

        @pl.when(i > 0)
        def _():
            dw_ref[...] += dwv

    return pl.pallas_call(
        body, name=name,
        out_shape=(jax.ShapeDtypeStruct(u.shape, BF16), jax.ShapeDtypeStruct((GDN_CONV, chans), F32)),
        grid=(chans // tc, length // tm),
        in_specs=[pl.BlockSpec((tm, tc), lambda j, i: (i, j)),
                  pl.BlockSpec((SUBLANES, tc), lambda j, i: (jnp.maximum(i * hb - 1, 0), j)),
                  pl.BlockSpec((tm, tc), lambda j, i: (i, j)),
                  pl.BlockSpec((SUBLANES, tc), lambda j, i: (jnp.minimum((i + 1) * hb, (last + 1) * hb - 1), j)),
                  pl.BlockSpec((GDN_CONV, tc), lambda j, i: (0, j))],
        out_specs=(pl.BlockSpec((tm, tc), lambda j, i: (i, j)),
                   pl.BlockSpec((GDN_CONV, tc), lambda j, i: (0, j))),
        scratch_shapes=[pltpu.VMEM((tm + SUBLANES, tc), F32), pltpu.VMEM((tm + SUBLANES, tc), F32)],
        compiler_params=_params(("parallel", "arbitrary")),
    )(u, u, dc, dc, w)


def _dot(a, b, dims, precision=None):
    if precision is None:
        a, b = a.astype(BF16), b.astype(BF16)
    return lax.dot_general(a, b, (dims, ((), ())), preferred_element_type=F32, precision=precision)


def _dot3(a, b, dims):
    ah, bh = a.astype(BF16), b.astype(BF16)
    al, bl = (a - ah.astype(F32)).astype(BF16), (b - bh.astype(F32)).astype(BF16)

    def d(x, y):
        return lax.dot_general(x, y, (dims, ((), ())), preferred_element_type=F32)

    return d(ah, bh) + (d(ah, bl) + d(al, bh))


NN = ((1,), (0,))
NT = ((1,), (1,))
TN = ((0,), (0,))
HI = lax.Precision.HIGHEST


def _hmap(f, *lists):
    return [f(*t) for t in zip(*lists)]


@jax.custom_vjp
def _unit_lower_inverse(a):
    c = a[0].shape[0]
    eye = (lax.broadcasted_iota(jnp.int32, (c, c), 0) == lax.broadcasted_iota(jnp.int32, (c, c), 1)).astype(F32)
    p = _hmap(lambda x: -x, a)
    t = _hmap(lambda x: eye + x, p)
    for _ in range(int(math.log2(c)) - 1):
        p = _hmap(lambda x: _dot3(x, x, NN), p)
        t = _hmap(lambda x, y: x + _dot3(x, y, NN), t, p)
    return t


def _uli_fwd(a):
    t = _unit_lower_inverse(a)
    return t, t


def _uli_bwd(t, dt):
    left = _hmap(lambda x, y: _dot3(x, y, TN), t, dt)
    return (_hmap(lambda x, y: -_dot3(x, y, NT), left, t),)


_unit_lower_inverse.defvjp(_uli_fwd, _uli_bwd)


def _gdn_chunk(q, k, v, bl, al, a_log, dt_bias, state):
    c = q[0].shape[0]
    row = lax.broadcasted_iota(jnp.int32, (c, c), 0)
    col = lax.broadcasted_iota(jnp.int32, (c, c), 1)
    causal = row >= col
    strict = row > col
    eye = (row == col).astype(F32)
    beta = _hmap(jax.nn.sigmoid, bl)
    g = _hmap(lambda a_, l_, d_: -jnp.exp(a_) * jax.nn.softplus(l_ + d_), a_log, al, dt_bias)
    g_r = _hmap(lambda x: jnp.sum(eye * x, axis=0, keepdims=True), g)
    gc = _hmap(lambda x: jnp.sum(jnp.where(causal, x, 0.0), axis=1, keepdims=True), g_r)
    gc_r = _hmap(lambda x: jnp.sum(jnp.where(row <= col, x, 0.0), axis=0, keepdims=True), g)
    decay = _hmap(lambda x, y: jnp.where(causal, jnp.exp(jnp.where(causal, x - y, 0.0)), 0.0), gc, gc_r)
    e_gc = _hmap(jnp.exp, gc)
    kb = _hmap(jnp.multiply, k, beta)
    vb = _hmap(jnp.multiply, v, beta)
    a_mat = _hmap(lambda x, y, d: jnp.where(strict, _dot(x, y, NT) * d, 0.0), kb, k, decay)
    t_inv = _unit_lower_inverse(a_mat)
    u_blk = _hmap(lambda t, x: _dot(t, x, NN), t_inv, vb)
    w_blk = _hmap(lambda t, x, e: _dot(t, x * e, NN), t_inv, kb, e_gc)
    v_new = _hmap(lambda u, w, s: u - _dot(w, s, NN), u_blk, w_blk, state)
    attn = _hmap(lambda x, y, d: _dot(x, y, NT) * d, q, k, decay)
    o_state = _hmap(lambda x, e, s: _dot(x * e, s, NN), q, e_gc, state)
    o = _hmap(lambda base, at, vn: base + _dot(at, vn, NN), o_state, attn, v_new)
    g_last = _hmap(lambda x: jnp.sum(x, axis=0, keepdims=True), g)
    k_dec = _hmap(lambda x, gl, c_: x * jnp.exp(gl - c_), k, g_last, gc)
    new_state = _hmap(lambda s, gl, kd, vn: s * jnp.exp(gl) + _dot(kd, vn, TN), state, g_last, k_dec, v_new)
    return o, new_state


def _gdn_operands(q_ref, k_ref, v_ref, bav, alog_ref, dtb_ref):
    hs = range(GDN_HEADS)
    cols = [slice(h * HEAD_DIM, (h + 1) * HEAD_DIM) for h in hs]
    return ([q_ref[:, sl] for sl in cols], [k_ref[:, sl] for sl in cols], [v_ref[:, sl] for sl in cols],
            [bav[:, h:h + 1] for h in hs], [bav[:, h + GDN_HEADS:h + GDN_HEADS + 1] for h in hs],
            [alog_ref[h:h + 1, 0:1] for h in hs], [dtb_ref[h:h + 1, 0:1] for h in hs])


def _gdn_scan_fwd(q, k, v, ba, a_log, dt_bias, *, name):
    length = q.shape[0]
    n = length // GDN_CHUNK
    c = GDN_CHUNK

    def body(q_ref, k_ref, v_ref, ba_ref, alog_ref, dtb_ref, o_ref, s_ref, state):
        i = pl.program_id(0)

        @pl.when(i == 0)
        def _():
            state[...] = jnp.zeros_like(state)

        bav = ba_ref[...]
        heads = [slice(h * HEAD_DIM, (h + 1) * HEAD_DIM) for h in range(GDN_HEADS)]
        s_in = [state[h] for h in range(GDN_HEADS)]
        o, s_out = _gdn_chunk(*_gdn_operands(q_ref, k_ref, v_ref, bav, alog_ref, dtb_ref), s_in)
        for h, sl in enumerate(heads):
            s_ref[h] = s_in[h]
            o_ref[:, sl] = o[h]
            state[h] = s_out[h]

    row_spec = pl.BlockSpec((c, D_MODEL), lambda i: (i, 0))
    small = pl.BlockSpec((GDN_HEADS, LANES), lambda i: (0, 0))
    return pl.pallas_call(
        body, name=name,
        out_shape=(jax.ShapeDtypeStruct((length, D_MODEL), F32),
                   jax.ShapeDtypeStruct((n, GDN_HEADS, HEAD_DIM, HEAD_DIM), F32)),
        grid=(n,),
        in_specs=[row_spec, row_spec, row_spec, pl.BlockSpec((c, LANES), lambda i: (i, 0)), small, small],
        out_specs=(row_spec, pl.BlockSpec((None, GDN_HEADS, HEAD_DIM, HEAD_DIM), lambda i: (i, 0, 0, 0))),
        scratch_shapes=[pltpu.VMEM((GDN_HEADS, HEAD_DIM, HEAD_DIM), F32)],
        compiler_params=_params(("arbitrary",)),
    )(q, k, v, ba, a_log, dt_bias)


def _gdn_scan_bwd(q, k, v, ba, a_log, dt_bias, states, do, *, name):
    length = q.shape[0]
    n = length // GDN_CHUNK
    c = GDN_CHUNK

    def body(q_ref, k_ref, v_ref, ba_ref, alog_ref, dtb_ref, s_ref, do_ref,
             dq_ref, dk_ref, dv_ref, dba_ref, dalog_ref, ddtb_ref, dstate):
        i = pl.program_id(0)

        @pl.when(i == 0)
        def _():
            dstate[...] = jnp.zeros_like(dstate)
            dalog_ref[...] = jnp.zeros_like(dalog_ref)
            ddtb_ref[...] = jnp.zeros_like(ddtb_ref)

        bav = ba_ref[...]
        lane = lax.broadcasted_iota(jnp.int32, (c, LANES), 1)
        sub8 = lax.broadcasted_iota(jnp.int32, (GDN_HEADS, LANES), 0)
        lane8 = lax.broadcasted_iota(jnp.int32, (GDN_HEADS, LANES), 1)
        slab = jnp.zeros((c, LANES), F32)
        dalog_all = jnp.zeros((GDN_HEADS, LANES), F32)
        ddtb_all = jnp.zeros((GDN_HEADS, LANES), F32)
        heads = [slice(h * HEAD_DIM, (h + 1) * HEAD_DIM) for h in range(GDN_HEADS)]
        ds_in = [dstate[h] for h in range(GDN_HEADS)]
        s_in = [s_ref[h] for h in range(GDN_HEADS)]
        _, vjp = jax.vjp(_gdn_chunk, *_gdn_operands(q_ref, k_ref, v_ref, bav, alog_ref, dtb_ref), s_in)
        dq, dk, dv, dbl, dal, dalog, ddtb, ds = vjp(([do_ref[:, sl] for sl in heads], ds_in))
        for h, sl in enumerate(heads):
            dq_ref[:, sl] = dq[h]
            dk_ref[:, sl] = dk[h]
            dv_ref[:, sl] = dv[h]
            dstate[h] = ds[h]
            slab = slab + jnp.where(lane == h, dbl[h], 0.0) + jnp.where(lane == h + GDN_HEADS, dal[h], 0.0)
            here = (sub8 == h) & (lane8 == 0)
            dalog_all = dalog_all + jnp.where(here, dalog[h], 0.0)
            ddtb_all = ddtb_all + jnp.where(here, ddtb[h], 0.0)
        dba_ref[...] = slab
        dalog_ref[...] += dalog_all
        ddtb_ref[...] += ddtb_all

    row_spec = pl.BlockSpec((c, D_MODEL), lambda i: (n - 1 - i, 0))
    small = pl.BlockSpec((GDN_HEADS, LANES), lambda i: (0, 0))
    return pl.pallas_call(
        body, name=name,
        out_shape=(jax.ShapeDtypeStruct((length, D_MODEL), F32),) * 3
        + (jax.ShapeDtypeStruct((length, LANES), F32),
           jax.ShapeDtypeStruct((GDN_HEADS, LANES), F32), jax.ShapeDtypeStruct((GDN_HEADS, LANES), F32)),
        grid=(n,),
        in_specs=[row_spec, row_spec, row_spec,
                  pl.BlockSpec((c, LANES), lambda i: (n - 1 - i, 0)), small, small,
                  pl.BlockSpec((None, GDN_HEADS, HEAD_DIM, HEAD_DIM), lambda i: (n - 1 - i, 0, 0, 0)),
                  row_spec],
        out_specs=(row_spec, row_spec, row_spec,
                   pl.BlockSpec((c, LANES), lambda i: (n - 1 - i, 0)), small, small),
        scratch_shapes=[pltpu.VMEM((GDN_HEADS, HEAD_DIM, HEAD_DIM), F32)],
        compiler_params=_params(("arbitrary",)),
    )(q, k, v, ba, a_log, dt_bias, states, do)


S5_W = S5_T * LANES
S5_S = 2 * 8 * S5_STATE
S5_SH = S5_S // 2


def _iota2(shape):
    return lax.broadcasted_iota(jnp.int32, shape, 0), lax.broadcasted_iota(jnp.int32, shape, 1)


def _s5_rep_t(t, dtype):
    row, col = _iota2((S5_T * S5_GROUP, LANES))
    return ((jnp.right_shift(row, 4) == t) & (jnp.bitwise_and(row, 15) == jnp.bitwise_and(col, 15))).astype(dtype)


def _s5_rep_state(dtype):
    row, col = _iota2((2 * S5_STATE, S5_S))
    return ((jnp.right_shift(row, 6) == jnp.right_shift(col, 9))
            & (jnp.bitwise_and(row, 63) == jnp.bitwise_and(col, 63))).astype(dtype)


def _s5_masks():
    row, col = _iota2((LANES, LANES))
    m_ab = jnp.right_shift(row, 4) == jnp.right_shift(col, 4)
    row, col = _iota2((S5_S, LANES))
    m_e = jnp.bitwise_and(jnp.right_shift(row, 6), 7) == jnp.right_shift(col, 4)
    row, col = _iota2((LANES, S5_S))
    m_f = jnp.right_shift(row, 4) == jnp.bitwise_and(jnp.right_shift(col, 6), 7)
    return m_ab, m_e, m_f


def _s5_expand(kx_ref, ec_ref, fc_ref, kb_scr, e_scr, f_scr):
    m_ab, m_e, m_f = _s5_masks()
    kx = kx_ref[...].astype(BF16)
    ec = ec_ref[...].astype(BF16)
    rep_state = _s5_rep_state(BF16)
    for t in range(S5_T):
        rep = _s5_rep_t(t, BF16)
        cols = slice(t * LANES, (t + 1) * LANES)
        kb_scr[t] = jnp.where(m_ab, jnp.dot(kx, rep, preferred_element_type=F32), 0.0).astype(BF16)
        e_scr[:, cols] = jnp.where(m_e, jnp.dot(ec, rep, preferred_element_type=F32), 0.0).astype(BF16)
        f_scr[cols, :] = jnp.where(m_f, jnp.dot(fc_ref[t].astype(BF16), rep_state, preferred_element_type=F32),
                                   0.0).astype(BF16)


def _s5_token_rows(ref, n):
    return [ref[pl.ds(t, n, stride=S5_T), :].astype(BF16) for t in range(S5_T)]


def _s5_scan_fwd(u, kx, ec, fc, at, *, name):
    length = u.shape[0]
    n = length // S5_T
    assert n % SUBLANES == 0

    def body(u_ref, kx_ref, ec_ref, fc_ref, at_ref, y_ref, h_ref, kb_scr, e_scr, f_scr, g_scr):
        _s5_expand(kx_ref, ec_ref, fc_ref, kb_scr, e_scr, f_scr)
        us = _s5_token_rows(u_ref, n)
        g_scr[...] = jnp.dot(jnp.concatenate(us, axis=1), f_scr[...], preferred_element_type=F32)
        ar, ai = at_ref[:, :S5_SH], at_ref[:, S5_SH:]

        def step(blk, h):
            base = pl.multiple_of(blk * SUBLANES, SUBLANES)
            g8 = g_scr[pl.ds(base, SUBLANES), :]
            rows = []
            for r in range(SUBLANES):
                rows.append(h)
                hr, hi = h[:, :S5_SH], h[:, S5_SH:]
                h = jnp.concatenate([ar * hr - ai * hi, ar * hi + ai * hr], axis=1) + g8[r:r + 1, :]
            h_ref[pl.ds(base, SUBLANES), :] = jnp.concatenate(rows, axis=0)
            return h

        lax.fori_loop(0, n // SUBLANES, step, jnp.zeros((1, S5_S), F32))
        hb = h_ref[...].astype(BF16)
        for t in range(S5_T):
            acc = jnp.dot(hb, e_scr[:, t * LANES:(t + 1) * LANES], preferred_element_type=F32)
            for s in range(t + 1):
                acc = acc + jnp.dot(us[s], kb_scr[t - s], preferred_element_type=F32)
            y_ref[pl.ds(t, n, stride=S5_T), :] = acc

    return pl.pallas_call(
        body, name=name,
        out_shape=(jax.ShapeDtypeStruct((length, D_MODEL), F32), jax.ShapeDtypeStruct((S5_TILES, n, S5_S), F32)),
        grid=(S5_TILES,),
        in_specs=[pl.BlockSpec((length, LANES), lambda k: (0, k)), _s5_spec(LANES, S5_T * S5_GROUP),
                  _s5_spec(S5_S, S5_T * S5_GROUP), _s5_spec(S5_T, LANES, LANES), _s5_spec(1, S5_S)],
        out_specs=(pl.BlockSpec((length, LANES), lambda k: (0, k)), _s5_spec(n, S5_S)),
        scratch_shapes=[pltpu.VMEM((S5_T, LANES, LANES), BF16), pltpu.VMEM((S5_S, S5_W), BF16),
                        pltpu.VMEM((S5_W, S5_S), BF16), pltpu.VMEM((n, S5_S), F32)],
        compiler_params=_params(("parallel",)),
    )(u, kx, ec, fc, at)


def _s5_spec(*tail):
    return pl.BlockSpec((None,) + tail, lambda k: (k,) + (0,) * len(tail))


def _s5_scan_bwd(dy, kx, ec, fc, at, hs, *, name):
    length = dy.shape[0]
    n = length // S5_T

    def body(dy_ref, kx_ref, ec_ref, fc_ref, at_ref, h_ref, du_ref, dg_ref, dat_ref, kb_scr, e_scr, f_scr, dh_scr):
        _s5_expand(kx_ref, ec_ref, fc_ref, kb_scr, e_scr, f_scr)
        dys = _s5_token_rows(dy_ref, n)
        dh_scr[...] = _dot(jnp.concatenate(dys, axis=1), e_scr[...], NT)
        ar, ai = at_ref[:, :S5_SH], at_ref[:, S5_SH:]

        def step(it, carry):
            cy, dat = carry
            base = pl.multiple_of((n // SUBLANES - 1 - it) * SUBLANES, SUBLANES)
            dh8 = dh_scr[pl.ds(base, SUBLANES), :]
            h8 = h_ref[pl.ds(base, SUBLANES), :]
            rows = [None] * SUBLANES
            for r in reversed(range(SUBLANES)):
                rows[r] = cy
                cr, ci = cy[:, :S5_SH], cy[:, S5_SH:]
                hr, hi = h8[r:r + 1, :S5_SH], h8[r:r + 1, S5_SH:]
                dat = dat + jnp.concatenate([cr * hr + ci * hi, ci * hr - cr * hi], axis=1)
                cy = dh8[r:r + 1, :] + jnp.concatenate([ar * cr + ai * ci, ar * ci - ai * cr], axis=1)
            dg_ref[pl.ds(base, SUBLANES), :] = jnp.concatenate(rows, axis=0)
            return cy, dat

        zero = jnp.zeros((1, S5_S), F32)
        _, dat = lax.fori_loop(0, n // SUBLANES, step, (zero, zero))
        dat_ref[...] = dat
        dgb = dg_ref[...].astype(BF16)
        for s in range(S5_T):
            acc = _dot(dgb, f_scr[s * LANES:(s + 1) * LANES, :], NT)
            for t in range(s, S5_T):
                acc = acc + _dot(dys[t], kb_scr[t - s], NT)
            du_ref[pl.ds(s, n, stride=S5_T), :] = acc

    row_spec = pl.BlockSpec((length, LANES), lambda k: (0, k))
    return pl.pallas_call(
        body, name=name,
        out_shape=(jax.ShapeDtypeStruct((length, D_MODEL), F32), jax.ShapeDtypeStruct((S5_TILES, n, S5_S), F32),
                   jax.ShapeDtypeStruct((S5_TILES, 1, S5_S), F32)),
        grid=(S5_TILES,),
        in_specs=[row_spec, _s5_spec(LANES, S5_T * S5_GROUP), _s5_spec(S5_S, S5_T * S5_GROUP),
                  _s5_spec(S5_T, LANES, LANES), _s5_spec(1, S5_S), _s5_spec(n, S5_S)],
        out_specs=(row_spec, _s5_spec(n, S5_S), _s5_spec(1, S5_S)),
        scratch_shapes=[pltpu.VMEM((S5_T, LANES, LANES), BF16), pltpu.VMEM((S5_S, S5_W), BF16),
                        pltpu.VMEM((S5_W, S5_S), BF16), pltpu.VMEM((n, S5_S), F32)],
        compiler_params=_params(("parallel",)),
    )(dy, kx, ec, fc, at, hs)


def _s5_operator_grads(dy, u, hs, dg, *, name):
    length = u.shape[0]
    n = length // S5_T

    def body(dy_ref, u_ref, h_ref, dg_ref, dkx_ref, dec_ref, dfc_ref):
        dys = _s5_token_rows(dy_ref, n)
        us = _s5_token_rows(u_ref, n)
        ucat = jnp.concatenate(us, axis=1)
        m_ab, m_e, m_f = _s5_masks()
        hb = h_ref[...].astype(BF16)
        dgb = dg_ref[...].astype(BF16)
        lane = lax.broadcasted_iota(jnp.int32, (1, LANES), 1)
        lane_group = jnp.right_shift(lane, 4)

        def own_block(x, mask):
            x = jnp.where(mask, x, 0.0)
            for shift in (64, 32, 16):
                x = x + pltpu.roll(x, shift, 1)
            return x

        def place(halves, t, x):
            halves[t // 8] = jnp.where(lane_group == t % 8, x, halves[t // 8])

        dkb = [jnp.zeros((LANES, LANES), F32) for _ in range(S5_T)]
        dec = [jnp.zeros((S5_S, LANES), F32) for _ in range(2)]
        for t in range(S5_T):
            d_t = _dot(ucat, dys[t], TN)
            for s in range(t + 1):
                dkb[t - s] = dkb[t - s] + d_t[s * LANES:(s + 1) * LANES, :]
            place(dec, t, own_block(_dot(hb, dys[t], TN), m_e))
            wide = jnp.where(m_f, _dot(us[t], dgb, TN), 0.0)
            parts = []
            for r in range(2):
                acc = wide[:, r * S5_SH:r * S5_SH + LANES]
                for q in range(1, S5_SH // LANES):
                    acc = acc + wide[:, r * S5_SH + q * LANES:r * S5_SH + (q + 1) * LANES]
                parts.append(acc + pltpu.roll(acc, S5_STATE, 1))
            dfc_ref[t] = jnp.where(lane < S5_STATE, parts[0], parts[1])
        dkx = [jnp.zeros((LANES, LANES), F32) for _ in range(2)]
        for t in range(S5_T):
            place(dkx, t, own_block(dkb[t], m_ab))
        dkx_ref[...] = jnp.concatenate(dkx, axis=1)
        dec_ref[...] = jnp.concatenate(dec, axis=1)

    row_spec = pl.BlockSpec((length, LANES), lambda k: (0, k))
    outs = (_s5_spec(LANES, S5_T * S5_GROUP), _s5_spec(S5_S, S5_T * S5_GROUP), _s5_spec(S5_T, LANES, LANES))
    return pl.pallas_call(
        body, name=name,
        out_shape=(jax.ShapeDtypeStruct((S5_TILES, LANES, S5_T * S5_GROUP), F32),
                   jax.ShapeDtypeStruct((S5_TILES, S5_S, S5_T * S5_GROUP), F32),
                   jax.ShapeDtypeStruct((S5_TILES, S5_T, LANES, LANES), F32)),
        grid=(S5_TILES,),
        in_specs=[row_spec, row_spec, _s5_spec(n, S5_S), _s5_spec(n, S5_S)],
        out_specs=outs,
        compiler_params=_params(("parallel",)),
    )(dy, u, hs, dg)


def _s5_prep(a_re, a_im, b_re, b_im, c_re, c_im, log_dt):
    t_len, tiles = S5_T, S5_TILES
    dt = jnp.exp(log_dt)[:, None]
    mag = jnp.exp(a_re * dt)
    ab_re, ab_im = mag * jnp.cos(a_im * dt), mag * jnp.sin(a_im * dt)
    den = jnp.square(a_re) + jnp.square(a_im)
    n_re, n_im = ab_re - 1.0, ab_im
    f_re = (n_re * a_re + n_im * a_im) / den
    f_im = (n_im * a_re - n_re * a_im) / den
    bb_re = f_re[..., None] * b_re - f_im[..., None] * b_im
    bb_im = f_re[..., None] * b_im + f_im[..., None] * b_re

    def powers(exponents):
        e = exponents[:, None, None]
        m = jnp.exp(e * (a_re * dt))
        return m * jnp.cos(e * (a_im * dt)), m * jnp.sin(e * (a_im * dt))

    p_re, p_im = powers(jnp.arange(t_len + 1, dtype=F32))
    rev_re, rev_im = powers((t_len - 1) - jnp.arange(t_len, dtype=F32))
    ca_re = c_re[None] * p_re[:, :, None, :] - c_im[None] * p_im[:, :, None, :]
    ca_im = c_re[None] * p_im[:, :, None, :] + c_im[None] * p_re[:, :, None, :]
    lag = (jnp.einsum('tgip,gpj->tgij', ca_re[:t_len], bb_re, precision=HI)
           - jnp.einsum('tgip,gpj->tgij', ca_im[:t_len], bb_im, precision=HI))
    kx = lag.reshape(t_len, tiles, 8, S5_GROUP, S5_GROUP).transpose(1, 2, 4, 0, 3)
    kx = kx.reshape(tiles, LANES, t_len * S5_GROUP)
    e_st = jnp.stack([ca_re[1:], -ca_im[1:]])
    e_st = e_st.reshape(2, t_len, tiles, 8, S5_GROUP, S5_STATE).transpose(2, 0, 3, 5, 1, 4)
    ec = e_st.reshape(tiles, S5_S, t_len * S5_GROUP)
    ab_b = jnp.stack([rev_re[..., None] * bb_re[None] - rev_im[..., None] * bb_im[None],
                      rev_re[..., None] * bb_im[None] + rev_im[..., None] * bb_re[None]])
    ab_b = ab_b.reshape(2, t_len, tiles, 8, S5_STATE, S5_GROUP).transpose(2, 1, 3, 5, 0, 4)
    fc = ab_b.reshape(tiles, t_len, LANES, 2 * S5_STATE)
    a_t = jnp.stack([p_re[t_len], p_im[t_len]]).reshape(2, tiles, 8 * S5_STATE).transpose(1, 0, 2)
    return kx, ec, fc, a_t.reshape(tiles, 1, S5_S)


TM_ROW = 256


def _gdn_fwd(x, w, tag):
    qkv = _mm(x, w["wqkv"], name="gdn_proj_qkv")
    z = _mm(x, w["wz"], name="gdn_proj_z")
    ba = _mm(x, w["wba"], name="gdn_proj_ba")
    cv = _conv_fwd(qkv, w["conv_w"], tm=TM_ROW, name="gdn_conv")
    q, k, v = _rw_fwd(_f_gdn_qkv, [cv], [], tm=TM_ROW, name="gdn_qkv")
    o, states = _gdn_scan_fwd(q, k, v, ba, w["a_log8"], w["dt_bias8"], name="gdn_scan")
    (mix,) = _rw_fwd(_f_gdn_out, [o, z], [w["norm_g"]], tm=TM_ROW, name="gdn_out", out_dtypes=[BF16])
    return mix, (qkv, z, ba, cv, q, k, v, states, o)


def _gdn_bwd(x, w, saved, dmix, dx_acc):
    qkv, z, ba, cv, q, k, v, states, o = saved
    (do, dz), (dnorm_g,) = _rw_bwd(_f_gdn_out, [o, z], [w["norm_g"]], [dmix], row_grad=[1, 1], param_grad=[1],
                                   tm=TM_ROW, name="gdn_out_bwd", row_dtypes=[F32, BF16])
    dq, dk, dv, dba, dalog, ddtb = _gdn_scan_bwd(q, k, v, ba, w["a_log8"], w["dt_bias8"], states, do,
                                                  name="gdn_scan_bwd")
    (dcv,), _ = _rw_bwd(_f_gdn_qkv, [cv], [], [dq, dk, dv], row_grad=[1], param_grad=[], tm=TM_ROW,
                        name="gdn_qkv_bwd")
    dqkv, dconv_w = _conv_bwd(qkv, w["conv_w"], dcv, tm=TM_ROW, name="gdn_conv_bwd")
    dx = _mm(dqkv, w["wqkv"], tb=True, acc=dx_acc, name="gdn_dx_qkv")
    dx = _mm(dz, w["wz"], tb=True, acc=dx, name="gdn_dx_z")
    dx = _mm(dba, w["wba"], tb=True, acc=dx, name="gdn_dx_ba")
    grads = dict(wqkv=_mm(x, dqkv, ta=True, name="gdn_dw_qkv"), wz=_mm(x, dz, ta=True, name="gdn_dw_z"),
                 wba=_mm(x, dba, ta=True, name="gdn_dw_ba"), conv_w=dconv_w,
                 a_log=dalog[:, 0], dt_bias=ddtb[:, 0], norm_g=dnorm_g[0])
    return dx, grads


def _s5_fwd(x, w, tag):
    u = _mm(x, w["wu"], name="s5_proj_u")
    y, hs = _s5_scan_fwd(u, w["kx"], w["ec"], w["fc"], w["a_t"], name="s5_scan")
    (zg,) = _rw_fwd(_f_s5_gelu, [y, u], [w["d"]], tm=TM_ROW, name="s5_gelu")
    t = _mm(zg, w["w_glu"], name="s5_glu")
    (mix,) = _rw_fwd(_f_s5_gate, [zg, t], [w["b_glu"]], tm=TM_ROW, name="s5_gate", out_dtypes=[BF16])
    return mix, (u, hs, y, zg, t)


def _s5_bwd(x, w, saved, dmix, dx_acc):
    u, hs, y, zg, t = saved
    (dzg, dt), (db_glu,) = _rw_bwd(_f_s5_gate, [zg, t], [w["b_glu"]], [dmix], row_grad=[1, 1], param_grad=[1],
                                   tm=TM_ROW, name="s5_gate_bwd", row_dtypes=[F32, BF16])
    dzg = _mm(dt, w["w_glu"], tb=True, acc=dzg, name="s5_dzg")
    dw_glu = _mm(zg, dt, ta=True, name="s5_dw_glu")
    (dy, du), (dd,) = _rw_bwd(_f_s5_gelu, [y, u], [w["d"]], [dzg], row_grad=[1, 1], param_grad=[1],
                              tm=TM_ROW, name="s5_gelu_bwd")
    du_scan, dg, dat = _s5_scan_bwd(dy, w["kx"], w["ec"], w["fc"], w["a_t"], hs, name="s5_scan_bwd")
    dkx, dec, dfc = _s5_operator_grads(dy, u, hs, dg, name="s5_operator_grads")
    (du,) = _rw_fwd(_f_add, [du, du_scan], [], tm=TM_ROW, name="s5_du_add", out_dtypes=[BF16])
    d_a_re, d_a_im, d_b_re, d_b_im, d_c_re, d_c_im, d_log_dt = w["prep_vjp"]((dkx, dec, dfc, dat))
    dx = _mm(du, w["wu"], tb=True, acc=dx_acc, name="s5_dx_u")
    grads = dict(wu=_mm(x, du, ta=True, name="s5_dw_u"), w_glu=dw_glu, b_glu=db_glu[0], d=dd[0],
                 a_re=d_a_re, a_im=d_a_im, b_re=d_b_re, b_im=d_b_im, c_re=d_c_re, c_im=d_c_im, log_dt=d_log_dt)
    return dx, grads


def _ln_res_both(x, h, g, b):
    (y,) = _f_ln_res(x, h, g, b)
    return y, y


def _layer_fwd(x, xb, mem, w, is_gdn):
    mix, msave = (_gdn_fwd if is_gdn else _s5_fwd)(xb, w, "")
    xq = _mm(xb, w["wxq"], name="proj_xq")
    kv = _mm(mem, w["wkv"], name="mem_kv")
    kmem, vmem = kv[:, :XA_DIM], kv[:, XA_DIM:]
    (cross,) = _rw_fwd(_f_attn, [xq], [kmem, vmem], tm=TM_ROW, name="attn", out_dtypes=[BF16])
    h = _mm(mix, w["wo_mix"], name="wo_mix")
    h = _mm(cross, w["wo_cross"], acc=h, name="wo_cross")
    x1, x1b = _rw_fwd(_ln_res_both, [x, h], [w["ln1_g"], w["ln1_b"]], tm=TM_ROW, name="ln_res",
                      out_dtypes=[F32, BF16])
    hm, act = _mm_relu2(x1b, w["w1"], name="mlp_up")
    f = _mm(act, w["w2"], name="mlp_down")
    x2, x2b = _rw_fwd(_ln_res_both, [x1, f], [w["ln2_g"], w["ln2_b"]], tm=TM_ROW, name="ln_res",
                      out_dtypes=[F32, BF16])
    return x2, x2b, (x, xb, msave, xq, kmem, vmem, mix, cross, h, x1, x1b, hm, act, f)


def _layer_bwd(mem, w, is_gdn, saved, dx2, token=None):
    x, xb, msave, xq, kmem, vmem, mix, cross, h, x1, x1b, hm, act, f = saved
    ln2_g = w["ln2_g"] if token is None else w["ln2_g"] + token[0, 0]
    (dx1, df), (dg2, db2) = _rw_bwd(_f_ln_res, [x1, f], [ln2_g, w["ln2_b"]], [dx2], row_grad=[1, 1],
                                    param_grad=[1, 1], tm=TM_ROW, name="ln_res_bwd", row_dtypes=[F32, BF16])
    dhm = _mm_relu2_grad(df, w["w2"], hm, name="mlp_dhm")
    dw2 = _mm(act, df, ta=True, name="mlp_dw2")
    dx1 = _mm(dhm, w["w1"], tb=True, acc=dx1, name="mlp_dx")
    dw1 = _mm(x1b, dhm, ta=True, out_blocks=N_CHIPS, name="mlp_dw1")
    (dx, dh), (dg1, db1) = _rw_bwd(_f_ln_res, [x, h], [w["ln1_g"], w["ln1_b"]], [dx1], row_grad=[1, 1],
                                   param_grad=[1, 1], tm=TM_ROW, name="ln_res_bwd", row_dtypes=[F32, BF16])
    dmix =_mm(dh, w["wo_mix"], tb=True, name="wo_dmix")
    dcross = _mm(dh, w["wo_cross"], tb=True, name="wo_dcross")
    dwo = jnp.concatenate([_mm(mix, dh, ta=True, name="wo_dw_mix"), _mm(cross, dh, ta=True, name="wo_dw_cross")], 0)
    (dxq,), (dkmem, dvmem) = _rw_bwd(_f_attn, [xq], [kmem, vmem], [dcross], row_grad=[1], param_grad=[1, 1],
                                     tm=TM_ROW, name="attn_bwd", row_dtypes=[BF16])
    dwkv = _mm(mem, jnp.concatenate([dkmem, dvmem], axis=1), ta=True, name="mem_dw_kv")
    dx = _mm(dxq, w["wxq"], tb=True, acc=dx, name="dx_xq")
    dwxq = _mm(xb, dxq, ta=True, name="dw_xq")
    dx, mg = (_gdn_bwd if is_gdn else _s5_bwd)(xb, w, msave, dmix, dx)
    grads = dict(mixer=mg, wxq=dwxq, wkv=dwkv, wo=dwo, w1=dw1, w2=dw2,
                 ln1_g=dg1[0], ln1_b=db1[0], ln2_g=dg2[0], ln2_b=db2[0])
    return dx, grads


def _loss_and_grad(y, target):
    def f(yv, tv):
        err = yv - tv
        return (err * (1.0 / D_MODEL),), (0.5 / D_MODEL * jnp.sum(err * err, axis=0, keepdims=True),)

    (dy,), (part,) = _rowwise(f, [y, target], [], [(D_MODEL, F32)], [((1, D_MODEL), F32)], tm=512, name="loss")
    return jnp.sum(part), dy


def _layer_weights(full, i):
    j = i // 2
    w = dict(wkv=full["w_kv_mem"][i].astype(BF16),
             wo_mix=full["w_o"][i][:D_MODEL].astype(BF16), wo_cross=full["w_o"][i][D_MODEL:].astype(BF16),
             ln1_g=full["ln1_g"][i][None], ln1_b=full["ln1_b"][i][None],
             ln2_g=full["ln2_g"][i][None], ln2_b=full["ln2_b"][i][None],
             w1=full["mlp_w1"][i].astype(BF16), w2=full["mlp_w2"][i].astype(BF16))
    if i % 2 == 0:
        w_in = full["gdn_w_in"][j]
        gd = 3 * D_MODEL
        w.update(wqkv=w_in[:, :gd].astype(BF16), wz=w_in[:, gd:gd + D_MODEL].astype(BF16),
                 wba=jnp.pad(w_in[:, gd + D_MODEL:gd + D_MODEL + 2 * GDN_HEADS],
                             ((0, 0), (0, LANES - 2 * GDN_HEADS))).astype(BF16),
                 wxq=w_in[:, gd + D_MODEL + 2 * GDN_HEADS:].astype(BF16),
                 conv_w=full["gdn_conv_w"][j],
                 a_log8=jnp.broadcast_to(full["gdn_a_log"][j][:, None], (GDN_HEADS, LANES)),
                 dt_bias8=jnp.broadcast_to(full["gdn_dt_bias"][j][:, None], (GDN_HEADS, LANES)),
                 norm_g=full["gdn_norm_g"][j][None])
    else:
        w_in = full["s5_w_in"][j]
        (kx, ec, fc, a_t), prep_vjp = jax.vjp(
            _s5_prep, full["s5_a_re"][j], full["s5_a_im"][j], full["s5_b_re"][j], full["s5_b_im"][j],
            full["s5_c_re"][j], full["s5_c_im"][j], full["s5_log_dt"][j])
        w.update(wu=w_in[:, :D_MODEL].astype(BF16), wxq=w_in[:, D_MODEL:].astype(BF16),
                 kx=kx, ec=ec, fc=fc, a_t=a_t, prep_vjp=prep_vjp,
                 d=full["s5_d"][j][None], w_glu=full["s5_w_glu"][j].astype(BF16), b_glu=full["s5_b_glu"][j][None])
    return w


def _sharded_grads(l, i):
    m = l["mixer"]
    out = dict(w_kv_mem=l["wkv"], w_o=l["wo"], mlp_w1=l["w1"], mlp_w2=l["w2"])
    if i % 2 == 0:
        out.update(gdn_w_in=jnp.concatenate([m["wqkv"], m["wz"], m["wba"][:, :2 * GDN_HEADS], l["wxq"]], axis=1),
                   gdn_conv_w=m["conv_w"])
    else:
        out.update(s5_w_in=jnp.concatenate([m["wu"], l["wxq"]], axis=1), s5_d=m["d"], s5_w_glu=m["w_glu"],
                   s5_b_glu=m["b_glu"])
    return out


def _replicated_grads(layer_grads):
    g = layer_grads
    gdn = [g[i]["mixer"] for i in range(DEPTH) if i % 2 == 0]
    s5 = [g[i]["mixer"] for i in range(DEPTH) if i % 2 == 1]
    out = {n: jnp.stack([l[n] for l in g]) for n in ("ln1_g", "ln1_b", "ln2_g", "ln2_b")}
    out.update({"gdn_" + n: jnp.stack([m[n] for m in gdn]) for n in ("a_log", "dt_bias", "norm_g")})
    out.update({"s5_" + n: jnp.stack([m[n] for m in s5])
                for n in ("a_re", "a_im", "b_re", "b_im", "c_re", "c_im", "log_dt")})
    return out


def _local_step(x, mem, target, weights_of, grads_ready):
    lw, saves = [], []
    h, hb = x, x.astype(BF16)
    for i in range(DEPTH):
        lw.append(weights_of(i, h))
        h, hb, s = _layer_fwd(h, hb, mem, lw[i], i % 2 == 0)
        saves.append(s)
    loss, d = _loss_and_grad(h, target)
    grads = [None] * DEPTH
    token = None
    for i in reversed(range(DEPTH)):
        d, grads[i] = _layer_bwd(mem, lw[i], i % 2 == 0, saves[i], d, token)
        token = grads_ready(i, grads[i])
    return loss, d, grads


ANY = pl.BlockSpec(memory_space=pl.ANY)
SHARD_ROWS = 1024
SMALL_ROWS = 128


def _place():
    return lax.axis_index("x"), lax.axis_index("y"), lax.axis_index("c")


def _other_chips(x, y):
    return [(1 - x, y), (x, 1 - y), (1 - x, 1 - y)]


def _all_gather_chips(wpack, *, name):
    rows = wpack.shape[0]
    half = rows // 2

    def body(w_ref, out_ref, send_sems, recv_sems):
        x, y, c = _place()
        sibling = (x, y, 1 - c)
        chips = _other_chips(x, y)

        def blk(cx, cy, cc):
            return out_ref.at[2 * cx + cy, pl.ds(cc * half, half), :]

        def copy(k, src, dst, to):
            return pltpu.make_async_remote_copy(src_ref=src, dst_ref=dst, send_sem=send_sems.at[k],
                                                recv_sem=recv_sems.at[k], device_id=to, device_id_type=MESH)

        first = [copy(j, w_ref.at[pl.ds(c * half, half), :], blk(x, y, c), (cx, cy, c))
                 for j, (cx, cy) in enumerate(chips)]
        for cp in first:
            cp.start()
        passed = [copy(3 + j, blk(cx, cy, c), blk(cx, cy, c), sibling) for j, (cx, cy) in enumerate(chips)]
        for j, (cx, cy) in enumerate(chips):
            copy(j, blk(cx, cy, c), blk(cx, cy, c), (cx, cy, c)).wait_recv()
            passed[j].start()
        for j, (cx, cy) in enumerate(chips):
            copy(3 + j, blk(cx, cy, 1 - c), blk(cx, cy, 1 - c), sibling).wait_recv()
        for cp in first + passed:
            cp.wait_send()

    return pl.pallas_call(
        body, name=name, out_shape=jax.ShapeDtypeStruct((N_CHIPS, rows, D_MODEL), wpack.dtype),
        in_specs=[ANY], out_specs=ANY,
        scratch_shapes=[pltpu.SemaphoreType.DMA((6,)), pltpu.SemaphoreType.DMA((6,))],
    )(wpack)


HBM = pl.BlockSpec(memory_space=pltpu.HBM)
SEM = pl.BlockSpec(memory_space=pltpu.SEMAPHORE)
DATAFLOW = pltpu.SideEffectType.DATAFLOW_SIDE_EFFECTING


def _gather_ici_copies(w_ref, land_ref, send_sems, recv_sems, outgoing):
    x, y, c = _place()
    half = w_ref.shape[0] // 2
    mine = pl.ds(c * half, half)
    return [pltpu.make_async_remote_copy(
        src_ref=w_ref.at[mine, :], dst_ref=land_ref.at[2 * x + y if outgoing else 2 * cx + cy, mine, :],
        send_sem=send_sems.at[j], recv_sem=recv_sems.at[j], device_id=(cx, cy, c), device_id_type=MESH)
        for j, (cx, cy) in enumerate(_other_chips(x, y))]


def _gather_start(wpack, after):
    rows = wpack.shape[0]

    def body(w_ref, land_ref, after_ref, send_sems, recv_sems, w_thru, land_thru, token):
        for cp in _gather_ici_copies(w_ref, land_ref, send_sems, recv_sems, outgoing=True):
            cp.start()
        token[...] = jnp.zeros_like(token)

    land = pltpu.with_memory_space_constraint(lax.empty((N_CHIPS, rows, D_MODEL), wpack.dtype), pltpu.HBM)
    return pl.pallas_call(
        body, name="gather_start",
        out_shape=(pltpu.SemaphoreType.DMA((3,)), pltpu.SemaphoreType.DMA((3,)), pltpu.HBM(wpack.shape, wpack.dtype),
                   pltpu.HBM(land.shape, land.dtype), jax.ShapeDtypeStruct((SUBLANES, LANES), F32)),
        in_specs=(HBM, HBM, ANY), out_specs=(SEM, SEM, HBM, HBM, pl.BlockSpec(memory_space=pltpu.VMEM)),
        input_output_aliases={0: 2, 1: 3},
        compiler_params=pltpu.CompilerParams(has_side_effects=DATAFLOW),
    )(pltpu.with_memory_space_constraint(wpack, pltpu.HBM), land, after)


def _gather_wait(send_sems, recv_sems, w_thru, land_thru, after):
    def body(w_ref, land_ref, send_sems, recv_sems, after_ref, w_dead, land_out):
        for cp in _gather_ici_copies(w_ref, land_ref, send_sems, recv_sems, outgoing=False):
            cp.wait_send()
            cp.wait_recv()

    return pl.pallas_call(
        body, name="gather_wait",
        out_shape=(pltpu.HBM(w_thru.shape, w_thru.dtype), pltpu.HBM(land_thru.shape, land_thru.dtype)),
        in_specs=(HBM, HBM, SEM, SEM, ANY), out_specs=(HBM, HBM), input_output_aliases={0: 0, 1: 1},
        compiler_params=pltpu.CompilerParams(has_side_effects=DATAFLOW),
    )(w_thru, land_thru, send_sems, recv_sems, after)[1]


def _gather_forward(land, *, name):
    rows = land.shape[1]
    half = rows // 2

    def body(in_ref, out_ref, send_sems, recv_sems):
        x, y, c = _place()

        def copy(j, cx, cy, cc):
            rows_of = out_ref.at[2 * cx + cy, pl.ds(cc * half, half), :]
            return pltpu.make_async_remote_copy(src_ref=rows_of, dst_ref=rows_of, send_sem=send_sems.at[j],
                                                recv_sem=recv_sems.at[j], device_id=(x, y, 1 - c), device_id_type=MESH)

        sends = [copy(j, cx, cy, c) for j, (cx, cy) in enumerate(_other_chips(x, y))]
        for cp in sends:
            cp.start()
        for j, (cx, cy) in enumerate(_other_chips(x, y)):
            copy(j, cx, cy, 1 - c).wait_recv()
        for cp in sends:
            cp.wait_send()

    return pl.pallas_call(
        body, name=name, out_shape=jax.ShapeDtypeStruct(land.shape, land.dtype), in_specs=[ANY], out_specs=ANY,
        input_output_aliases={0: 0},
        scratch_shapes=[pltpu.SemaphoreType.DMA((3,)), pltpu.SemaphoreType.DMA((3,))],
    )(land)


def _sibling_swap(buf, *, name):
    def body(in_ref, out_ref, send_sem, recv_sem):
        x, y, c = _place()
        cp = pltpu.make_async_remote_copy(src_ref=in_ref, dst_ref=out_ref, send_sem=send_sem, recv_sem=recv_sem,
                                          device_id=(x, y, 1 - c), device_id_type=MESH)
        cp.start()
        cp.wait()

    return pl.pallas_call(
        body, name=name, out_shape=jax.ShapeDtypeStruct(buf.shape, buf.dtype), in_specs=[ANY], out_specs=ANY,
        scratch_shapes=[pltpu.SemaphoreType.DMA, pltpu.SemaphoreType.DMA],
    )(buf)


def _pair_exchange(gpack, *, name):
    pieces, rows, width = gpack.shape
    half = rows // 2

    def body(in_ref, got_ref, send_sems, recv_sems):
        x, y, c = _place()
        sends = [pltpu.make_async_remote_copy(src_ref=in_ref.at[p, pl.ds((1 - c) * half, half), :],
                                              dst_ref=got_ref.at[p], send_sem=send_sems.at[p],
                                              recv_sem=recv_sems.at[p], device_id=(x, y, 1 - c), device_id_type=MESH)
                 for p in range(pieces)]
        for cp in sends:
            cp.start()
        for cp in sends:
            cp.wait()

    return pl.pallas_call(
        body, name=name, out_shape=jax.ShapeDtypeStruct((pieces, half, width), gpack.dtype),
        in_specs=[ANY], out_specs=ANY,
        scratch_shapes=[pltpu.SemaphoreType.DMA((pieces,)), pltpu.SemaphoreType.DMA((pieces,))],
    )(gpack)


def _pair_add(gpack, got, c, *, name, tm=512):
    pieces, rows, width = gpack.shape
    half = rows // 2
    nb = half // tm

    def body(c_ref, a_ref, b_ref, sum_ref, narrow_ref):
        s = a_ref[...] + b_ref[...]
        sum_ref[...] = s
        narrow_ref[...] = s.astype(BF16)

    blk = pl.BlockSpec((None, tm, width), lambda p, i, c_ref: (p, i, 0))
    return pl.pallas_call(
        body, name=name,
        out_shape=(jax.ShapeDtypeStruct((pieces, half, width), F32), jax.ShapeDtypeStruct((pieces, half, width), BF16)),
        grid_spec=pltpu.PrefetchScalarGridSpec(
            num_scalar_prefetch=1, grid=(pieces, nb),
            in_specs=[pl.BlockSpec((None, tm, width), lambda p, i, c_ref: (p, c_ref[0] * nb + i, 0)), blk],
            out_specs=(blk, blk)),
        compiler_params=_params(("parallel", "parallel")),
    )(c, gpack, got)


def _chip_exchange(pieces, *, name):
    _, rows, width = pieces.shape

    def body(in_ref, out_ref, send_sems, recv_sems):
        x, y, c = _place()
        cps = [pltpu.make_async_remote_copy(src_ref=in_ref.at[2 * cx + cy], dst_ref=out_ref.at[j],
                                            send_sem=send_sems.at[j], recv_sem=recv_sems.at[j],
                                            device_id=(cx, cy, c), device_id_type=MESH)
               for j, (cx, cy) in enumerate(_other_chips(x, y))]
        for cp in cps:
            cp.start()
        for cp in cps:
            cp.wait()

    return pl.pallas_call(
        body, name=name, out_shape=jax.ShapeDtypeStruct((3, rows, width), pieces.dtype), in_specs=[ANY], out_specs=ANY,
        scratch_shapes=[pltpu.SemaphoreType.DMA((3,)), pltpu.SemaphoreType.DMA((3,))],
    )(pieces)


def _chip_exchange_copies(in_ref, land_ref, send_sems, recv_sems):
    x, y, c = _place()
    return [pltpu.make_async_remote_copy(src_ref=in_ref.at[2 * cx + cy], dst_ref=land_ref.at[j],
                                         send_sem=send_sems.at[j], recv_sem=recv_sems.at[j],
                                         device_id=(cx, cy, c), device_id_type=MESH)
            for j, (cx, cy) in enumerate(_other_chips(x, y))]


def _chip_exchange_start(pieces):
    _, rows, width = pieces.shape

    def body(in_ref, land_ref, send_sems, recv_sems, in_thru, land_thru, token):
        for cp in _chip_exchange_copies(in_ref, land_ref, send_sems, recv_sems):
            cp.start()
        token[...] = jnp.zeros_like(token)

    land = pltpu.with_memory_space_constraint(lax.empty((3, rows, width), pieces.dtype), pltpu.HBM)
    return pl.pallas_call(
        body, name="rs_chip_start",
        out_shape=(pltpu.SemaphoreType.DMA((3,)), pltpu.SemaphoreType.DMA((3,)), pltpu.HBM(pieces.shape, pieces.dtype),
                   pltpu.HBM(land.shape, land.dtype), jax.ShapeDtypeStruct((SUBLANES, LANES), F32)),
        in_specs=(HBM, HBM), out_specs=(SEM, SEM, HBM, HBM, pl.BlockSpec(memory_space=pltpu.VMEM)),
        input_output_aliases={0: 2, 1: 3},
        compiler_params=pltpu.CompilerParams(has_side_effects=DATAFLOW),
    )(pltpu.with_memory_space_constraint(pieces, pltpu.HBM), land)


def _chip_exchange_wait(send_sems, recv_sems, in_thru, land_thru, after):
    def body(in_ref, land_ref, send_sems, recv_sems, after_ref, in_dead, land_out):
        for cp in _chip_exchange_copies(in_ref, land_ref, send_sems, recv_sems):
            cp.wait_send()
            cp.wait_recv()

    return pl.pallas_call(
        body, name="rs_chip_wait",
        out_shape=(pltpu.HBM(in_thru.shape, in_thru.dtype), pltpu.HBM(land_thru.shape, land_thru.dtype)),
        in_specs=(HBM, HBM, SEM, SEM, ANY), out_specs=(HBM, HBM), input_output_aliases={0: 0, 1: 1},
        compiler_params=pltpu.CompilerParams(has_side_effects=DATAFLOW),
    )(in_thru, land_thru, send_sems, recv_sems, after)[1]


def _all_reduce_small(v, *, name):
    rows, width = v.shape
    half = rows // 2
    assert half % SUBLANES == 0

    def body(in_ref, out_ref, pair_buf, chip_buf, send_sems, recv_sems):
        x, y, c = _place()
        sibling = (x, y, 1 - c)
        me = 2 * x + y
        mine = pl.ds(pl.multiple_of(c * half, SUBLANES), half)
        other = pl.ds(pl.multiple_of((1 - c) * half, SUBLANES), half)

        def copy(k, src, dst, to):
            return pltpu.make_async_remote_copy(src_ref=src, dst_ref=dst, send_sem=send_sems.at[k],
                                                recv_sem=recv_sems.at[k], device_id=to, device_id_type=MESH)

        swap = copy(0, in_ref.at[other, :], pair_buf, sibling)
        swap.start()
        swap.wait()
        chip_buf[me] = in_ref[mine, :] + pair_buf[...]
        chips = _other_chips(x, y)
        for j, (cx, cy) in enumerate(chips):
            copy(1 + j, chip_buf.at[me], chip_buf.at[me], (cx, cy, c)).start()
        for j, (cx, cy) in enumerate(chips):
            got = copy(1 + j, chip_buf.at[me], chip_buf.at[2 * cx + cy], (cx, cy, c))
            got.wait_send()
            got.wait_recv()
        out_ref[mine, :] = ((chip_buf[0] + chip_buf[1]) + chip_buf[2]) + chip_buf[3]
        share = copy(1 + len(chips), out_ref.at[mine, :], out_ref.at[mine, :], sibling)
        share.start()
        share.wait_send()
        copy(1 + len(chips), out_ref.at[other, :], out_ref.at[other, :], sibling).wait_recv()

    vmem = pl.BlockSpec(memory_space=pltpu.VMEM)
    return pl.pallas_call(
        body, name=name, out_shape=jax.ShapeDtypeStruct(v.shape, v.dtype), in_specs=[vmem], out_specs=vmem,
        scratch_shapes=[pltpu.VMEM((half, width), v.dtype), pltpu.VMEM((N_CHIPS, half, width), v.dtype),
                        pltpu.SemaphoreType.DMA((5,)), pltpu.SemaphoreType.DMA((5,))],
        compiler_params=pltpu.CompilerParams(vmem_limit_bytes=VMEM_LIMIT_V7X),
    )(v)


def _reduce_scatter_begin(gpack, behind):
    x, y, c = _place()
    got = _pair_exchange(gpack, name="rs_pair_swap")
    pair, pair16 = _pair_add(gpack, got, c.astype(jnp.int32).reshape(1), name="rs_pair_add")
    mine = lax.dynamic_index_in_dim(pair, 2 * x + y, axis=0, keepdims=False)
    if behind:
        *in_flight, token = _chip_exchange_start(pair16)
        return dict(mine=mine, in_flight=in_flight), token
    return dict(mine=mine, recv=_chip_exchange(pair16, name="rs_chip_exchange")), None


def _reduce_scatter_end(state, after=None):
    c = lax.axis_index("c")
    recv = state["recv"] if "recv" in state else _chip_exchange_wait(*state["in_flight"], after=after)
    (total,) = _rw_fwd(_f_add4, [state["mine"], recv[0], recv[1], recv[2]], [], tm=512, name="rs_chip_add")
    theirs = _sibling_swap(total, name="rs_share_swap")
    return jnp.concatenate([jnp.where(c == 0, total, theirs), jnp.where(c == 0, theirs, total)], axis=0)


_SHARDED = (("w_kv_mem", 1), ("w_o", 1), ("mlp_w1", 2), ("mlp_w2", 1), ("gdn_w_in", 2), ("gdn_conv_w", 2),
            ("s5_w_in", 2), ("s5_d", 1), ("s5_w_glu", 1), ("s5_b_glu", 1))
_MATMUL_ONLY = ("w_kv_mem", "w_o", "mlp_w1", "mlp_w2", "gdn_w_in", "s5_w_in", "s5_w_glu")
REDUCED_FIRST = (1, 2, 3)
_KEPT_BLOCKED = ("mlp_w1",)
_REPLICATED = ("ln1_g", "ln1_b", "ln2_g", "ln2_b", "gdn_a_log", "gdn_dt_bias", "gdn_norm_g", "s5_a_re", "s5_a_im",
               "s5_b_re", "s5_b_im", "s5_c_re", "s5_c_im", "s5_log_dt")
_WEIGHTS = ("w_kv_mem", "w_o", "ln1_g", "ln1_b", "ln2_g", "ln2_b", "mlp_w1", "mlp_w2", "gdn_w_in", "gdn_conv_w",
            "gdn_a_log", "gdn_dt_bias", "gdn_norm_g", "s5_w_in", "s5_a_re", "s5_a_im", "s5_b_re", "s5_b_im",
            "s5_c_re", "s5_c_im", "s5_log_dt", "s5_d", "s5_w_glu", "s5_b_glu")


ROW_ALIGN = 16


def _n_rows(shape):
    return -(-math.prod(shape) // (ROW_ALIGN * D_MODEL)) * ROW_ALIGN


def _as_rows(a):
    rows = _n_rows(a.shape)
    if a.shape[-1] == D_MODEL and a.size == rows * D_MODEL:
        return a.reshape(-1, D_MODEL)
    flat = a.reshape(-1)
    return jnp.pad(flat, (0, rows * D_MODEL - flat.size)).reshape(rows, D_MODEL)


def _pack(arrs, unit_rows=SHARD_ROWS):
    rows = [_as_rows(a) for a in arrs]
    pad = -sum(r.shape[0] for r in rows) % unit_rows
    if pad:
        rows.append(jnp.zeros((pad, D_MODEL), rows[0].dtype))
    return jnp.concatenate(rows, axis=0)


def _unpack(packed, shapes):
    lead = packed.shape[:-2]
    out, off = [], 0
    for s in shapes:
        r = _n_rows(s)
        seg = lax.slice_in_dim(packed, off, off + r, axis=len(lead))
        if s[-1] != D_MODEL or math.prod(s) != r * D_MODEL:
            seg = lax.slice_in_dim(seg.reshape(lead + (-1,)), 0, math.prod(s), axis=len(lead))
        out.append(seg.reshape(lead + tuple(s)))
        off += r
    return out


def _split3(t):
    hi = t.astype(BF16)
    r1 = t - hi.astype(F32)
    mid = r1.astype(BF16)
    lo = (r1 - mid.astype(F32)).astype(BF16)
    return jnp.stack([hi, mid, lo], axis=-1)


def _join3(t):
    return (t[..., 0].astype(F32) + t[..., 1].astype(F32)) + t[..., 2].astype(F32)


def _merge_chips(blocks, axis):
    return jnp.concatenate([blocks[s] for s in range(N_CHIPS)], axis=axis)


def _pack_for_chips(weights):
    rows = []
    for s in range(N_CHIPS):
        chip = []
        for layers, axis in weights:
            if axis is None:
                blocks = [g[s] for g in layers]
            else:
                n = layers[0].shape[axis] // N_CHIPS
                blocks = [lax.slice_in_dim(g, s * n, (s + 1) * n, axis=axis) for g in layers]
            if math.prod(blocks[0].shape) % (ROW_ALIGN * D_MODEL) == 0:
                chip += [_as_rows(b) for b in blocks]
            else:
                chip.append(_as_rows(jnp.stack(blocks)))
        pad = -sum(r.shape[0] for r in chip) % SHARD_ROWS
        rows += chip + ([jnp.zeros((pad, D_MODEL), F32)] if pad else [])
    return jnp.concatenate(rows, axis=0).reshape(N_CHIPS, -1, D_MODEL)


def kernel(x, mem, w_kv_mem, w_o, ln1_g, ln1_b, ln2_g, ln2_b, mlp_w1, mlp_w2, gdn_w_in, gdn_conv_w, gdn_a_log, gdn_dt_bias, gdn_norm_g, s5_w_in, s5_a_re, s5_a_im, s5_b_re, s5_b_im, s5_c_re, s5_c_im, s5_log_dt, s5_d, s5_w_glu, s5_b_glu, loss_target, m_w_kv_mem, m_w_o, m_ln1_g, m_ln1_b, m_ln2_g, m_ln2_b, m_mlp_w1, m_mlp_w2, m_gdn_w_in, m_gdn_conv_w, m_gdn_a_log, m_gdn_dt_bias, m_gdn_norm_g, m_s5_w_in, m_s5_a_re, m_s5_a_im, m_s5_b_re, m_s5_b_im, m_s5_c_re, m_s5_c_im, m_s5_log_dt, m_s5_d, m_s5_w_glu, m_s5_b_glu, v_w_kv_mem, v_w_o, v_ln1_g, v_ln1_b, v_ln2_g, v_ln2_b, v_mlp_w1, v_mlp_w2, v_gdn_w_in, v_gdn_conv_w, v_gdn_a_log, v_gdn_dt_bias, v_gdn_norm_g, v_s5_w_in, v_s5_a_re, v_s5_a_im, v_s5_b_re, v_s5_b_im, v_s5_c_re, v_s5_c_im, v_s5_log_dt, v_s5_d, v_s5_w_glu, v_s5_b_glu):
    given = dict(locals())
    w = {n: given[n] for n in _WEIGHTS}
    mom = {n: given["m_" + n] for n in _WEIGHTS}
    var = {n: given["v_" + n] for n in _WEIGHTS}
    shard_names = [n for n, _ in _SHARDED]
    shard_shapes = [w[n].shape for n in shard_names]
    rep_shapes = [w[n].shape for n in _REPLICATED]

    wire = {n: w[n].astype(BF16) if n in _MATMUL_ONLY else _split3(w[n]) for n in shard_names}
    first = {n: 0 if n.startswith("s5_") else 1 for n in shard_names}
    me_chip = 2 * lax.axis_index("x") + lax.axis_index("y")
    early = [wire[n][:first[n]] for n in shard_names if first[n]]
    late = [wire[n][first[n]:] for n in shard_names]
    early_pack, late_pack = _pack(early), _pack(late)
    landed = _all_gather_chips(early_pack, name="gather_first_layer")
    landed = lax.dynamic_update_index_in_dim(landed, early_pack, me_chip, axis=0)
    early_blocks = dict(zip([n for n in shard_names if first[n]], _unpack(landed, [a.shape for a in early])))
    send_sems, recv_sems, pack_thru, land_thru, token = _gather_start(late_pack, after=landed)
    axis_of = dict(_SHARDED)

    def merged(n, blk):
        if n in _KEPT_BLOCKED:
            return blk
        return _merge_chips(blk if n in _MATMUL_ONLY else _join3(blk), axis_of[n] - 1)

    late_full = {}

    def weights_of(i, h):
        if i == 0:
            full = {n: [merged(n, blk[:, 0])] for n, blk in early_blocks.items()}
            full["gdn_w_in"][0] = full["gdn_w_in"][0] + token[0, 0].astype(BF16)
        else:
            if not late_full:
                land = _gather_wait(send_sems, recv_sems, pack_thru, land_thru, after=h)
                land = _gather_forward(land, name="gather_forward")
                land = lax.dynamic_update_index_in_dim(land, late_pack, me_chip, axis=0)
                for n, blk in zip(shard_names, _unpack(land, [a.shape for a in late])):
                    late_full[n] = [None] * first[n] + [merged(n, blk[:, t]) for t in range(blk.shape[1])]
            full = dict(late_full)
        full.update({n: w[n] for n in _REPLICATED})
        return _layer_weights(full, i)

    sharded = {}
    in_flight = {}

    def group_pack(layers):
        names = [n for n in shard_names if any(n in sharded[i] for i in layers)]
        per_weight = [[sharded[i][n] for i in layers if n in sharded[i]] for n in names]
        pack = _pack_for_chips([(g, None if n in _KEPT_BLOCKED else axis_of[n] - 1) for n, g in zip(names, per_weight)])
        return pack, names, [(len(g),) + w[n].shape[1:] for n, g in zip(names, per_weight)]

    def grads_ready(i, g):
        sharded[i] = _sharded_grads(g, i)
        if i != REDUCED_FIRST[0]:
            return None
        pack, names, shapes = group_pack(REDUCED_FIRST)
        state, token = _reduce_scatter_begin(pack, behind=True)
        in_flight.update(state=state, names=names, shapes=shapes)
        return token

    loss, grad_x, layer_grads = _local_step(x[0], mem[0], loss_target[0], weights_of, grads_ready)
    loss = lax.psum(loss, ("x", "y", "c"))
    rest = [i for i in range(DEPTH) if i not in REDUCED_FIRST]
    pack, names, shapes = group_pack(rest)
    state, _ = _reduce_scatter_begin(pack, behind=False)
    pieces = {n: [] for n in shard_names}
    for n, g in zip(names, _unpack(_reduce_scatter_end(state), shapes)):
        pieces[n].append(g)
    late = _reduce_scatter_end(in_flight["state"], after=grad_x)
    for n, g in zip(in_flight["names"], _unpack(late, in_flight["shapes"])):
        pieces[n].append(g)
    g_shards = [p[0] if len(p) == 1 else jnp.concatenate(p, axis=0) for p in (pieces[n] for n in shard_names)]

    def pack_small(d):
        return _pack([d[n] for n in _REPLICATED], unit_rows=SMALL_ROWS)

    g_rep = _all_reduce_small(pack_small(_replicated_grads(layer_grads)), name="reduce_replicated")

    def adamw(wp, gp, mp, vp, name):
        return _rw_fwd(_f_adamw, [wp, gp, mp, vp], [], tm=256, name=name)

    outs = {}
    for n, g in zip(shard_names, g_shards):
        flat = (-1, w[n].shape[-1])
        res = adamw(w[n].reshape(flat), g.reshape(flat), mom[n].reshape(flat), var[n].reshape(flat), "adamw_" + n)
        outs[("grad", n)] = g
        outs.update({(kind, n): a.reshape(w[n].shape) for kind, a in zip(("delta", "new_m", "new_v"), res)})
    packed = (g_rep,) + tuple(adamw(pack_small(w), g_rep, pack_small(mom), pack_small(var), "adamw_replicated"))
    for kind, pr in zip(("grad", "delta", "new_m", "new_v"), packed):
        outs.update({(kind, n): a for n, a in zip(_REPLICATED, _unpack(pr, rep_shapes))})
    return (loss, grad_x[None]) + tuple(outs[(kind, n)] for kind in ("grad", "delta", "new_m", "new_v")
                                        for n in _WEIGHTS)
```

```python
import functools
import math

import jax
import jax.numpy as jnp
from jax import lax
from jax.experimental import pallas as pl
from jax.experimental.pallas import tpu as pltpu

F32 = jnp.float32
BF16 = jnp.bfloat16
MESH = pl.DeviceIdType.MESH

D_MODEL = 1024
DEPTH = 4
GDN_HEADS = 8
HEAD_DIM = 128
GDN_CONV = 4
GDN_CHUNK = 64
S5_GROUPS = 64
S5_GROUP = 16
S5_STATE = 64
XA_HEADS = 4
XA_DIM = 512
D_FF = 4096
DN_ALPHA = (2 * DEPTH) ** 0.25
LN_EPS = 1e-5
RMS_EPS = 1e-6
ADAM_LR, ADAM_B1, ADAM_B2, ADAM_EPS, ADAM_WD, ADAM_STEP = 0.001, 0.9, 0.999, 1e-08, 0.01, 10

VMEM_LIMIT_V7X = 56 * 1024 * 1024
LANES = 128
SUBLANES = 8
S5_T = 16
S5_TILES = D_MODEL // LANES
N_CHIPS = 4
N_DEV = 8


def _params(sem):
    return pltpu.CompilerParams(dimension_semantics=sem, vmem_limit_bytes=VMEM_LIMIT_V7X)


def _tile(n, pref):
    if n <= pref:
        return n
    t = (pref // LANES) * LANES
    while n % t:
        t -= LANES
    return t


def _row_tile(n, pref):
    if n % SUBLANES:
        return n
    t = min(pref, n) // SUBLANES * SUBLANES
    while n % t:
        t -= SUBLANES
    return t


def _col_blocked_spec(rows_tile, cols_tile, block_cols, rows_axis, cols_axis):
    r = block_cols // cols_tile

    def index(*ijk):
        c = ijk[cols_axis]
        return (c, ijk[rows_axis], 0) if r == 1 else (c // r, ijk[rows_axis], c % r)

    return pl.BlockSpec((None, rows_tile, cols_tile), index)


def _mm(a, b, *, ta=False, tb=False, acc=None, name, tm=1024, tn=1024, tk=None, out_blocks=0):
    if tk is None:
        tk = 4096 if a.dtype == BF16 and b.dtype == BF16 else 2048
    k_dim, m_dim = a.shape if ta else a.shape[::-1]
    b_rows, b_cols = (b.shape[0], b.shape[1]) if b.ndim == 2 else (b.shape[1], b.shape[0] * b.shape[2])
    n_dim = b_rows if tb else b_cols
    assert (b_cols if tb else b_rows) == k_dim, (a.shape, b.shape, ta, tb)
    limit_n = n_dim // out_blocks if out_blocks else (n_dim if b.ndim == 2 or tb else b.shape[2])
    limit_k = b.shape[2] if (b.ndim == 3 and tb) else k_dim
    tm, tn, tk = _tile(m_dim, tm), _tile(limit_n, min(tn, limit_n)), _tile(limit_k, min(tk, limit_k))
    a_spec = (pl.BlockSpec((tk, tm), lambda i, j, k: (k, i)) if ta else pl.BlockSpec((tm, tk), lambda i, j, k: (i, k)))
    if b.ndim == 3:
        b_spec = (_col_blocked_spec(tn, tk, b.shape[2], 1, 2) if tb else _col_blocked_spec(tk, tn, b.shape[2], 2, 1))
    else:
        b_spec = (pl.BlockSpec((tn, tk), lambda i, j, k: (j, k)) if tb
                  else pl.BlockSpec((tk, tn), lambda i, j, k: (k, j)))
    o_spec = (_col_blocked_spec(tm, tn, n_dim // out_blocks, 0, 1) if out_blocks
              else pl.BlockSpec((tm, tn), lambda i, j, k: (i, j)))
    o_shape = (out_blocks, m_dim, n_dim // out_blocks) if out_blocks else (m_dim, n_dim)
    dn = (((0 if ta else 1,), (1 if tb else 0,)), ((), ()))
    has_acc = acc is not None

    def body(*refs):
        a_ref, b_ref = refs[0], refs[1]
        o_ref = refs[-1]
        k = pl.program_id(2)
        p = lax.dot_general(a_ref[...].astype(BF16), b_ref[...].astype(BF16), dn,
                            preferred_element_type=F32)

        @pl.when(k == 0)
        def _():
            o_ref[...] = p + refs[2][...] if has_acc else p

        @pl.when(k > 0)
        def _():
            o_ref[...] += p

    return pl.pallas_call(
        body, name=name,
        out_shape=jax.ShapeDtypeStruct(o_shape, F32),
        grid=(m_dim // tm, n_dim // tn, k_dim // tk),
        in_specs=[a_spec, b_spec] + ([o_spec] if has_acc else []),
        out_specs=o_spec,
        compiler_params=_params(("parallel", "parallel", "arbitrary")),
    )(*([a, b] + ([acc] if has_acc else [])))


def _mm_relu2(a, b, *, name, tm=1024):
    m_dim, k_dim = a.shape
    n_blocks, _, tn = b.shape
    n_dim = n_blocks * tn
    tm = _tile(m_dim, tm)

    def body(a_ref, b_ref, h_ref, act_ref):
        h = jnp.dot(a_ref[...].astype(BF16), b_ref[...].astype(BF16), preferred_element_type=F32)
        h_ref[...] = h.astype(h_ref.dtype)
        r = jnp.maximum(h, 0.0)
        act_ref[...] = (r * r).astype(BF16)

    o_spec = pl.BlockSpec((tm, tn), lambda i, j: (i, j))
    return pl.pallas_call(
        body, name=name,
        out_shape=(jax.ShapeDtypeStruct((m_dim, n_dim), BF16), jax.ShapeDtypeStruct((m_dim, n_dim), BF16)),
        grid=(m_dim // tm, n_dim // tn),
        in_specs=[pl.BlockSpec((tm, k_dim), lambda i, j: (i, 0)),
                  pl.BlockSpec((None, k_dim, tn), lambda i, j: (j, 0, 0))],
        out_specs=(o_spec, o_spec),
        compiler_params=_params(("parallel", "parallel")),
    )(a, b)


def _mm_relu2_grad(d, b, h, *, name, tm=1024, tn=1024):
    m_dim, k_dim = d.shape
    n_dim = b.shape[0]
    tm, tn = _tile(m_dim, tm), _tile(n_dim, tn)

    def body(d_ref, b_ref, h_ref, o_ref):
        p = lax.dot_general(d_ref[...].astype(BF16), b_ref[...].astype(BF16), ((NT), ((), ())),
                            preferred_element_type=F32)
        o_ref[...] = (p * (2.0 * jnp.maximum(h_ref[...].astype(F32), 0.0))).astype(BF16)

    o_spec = pl.BlockSpec((tm, tn), lambda i, j: (i, j))
    return pl.pallas_call(
        body, name=name,
        out_shape=jax.ShapeDtypeStruct((m_dim, n_dim), BF16),
        grid=(m_dim // tm, n_dim // tn),
        in_specs=[pl.BlockSpec((tm, k_dim), lambda i, j: (i, 0)), pl.BlockSpec((tn, k_dim), lambda i, j: (j, 0)), o_spec],
        out_specs=o_spec,
        compiler_params=_params(("parallel", "parallel")),
    )(d, b, h)


def _rowwise(f, rows, params, row_out, acc_out, *, tm, name):
    length = rows[0].shape[0]
    tm = _row_tile(length, tm)
    nr, npar, nro = len(rows), len(params), len(row_out)

    def body(*refs):
        ins = [r[...] for r in refs[:nr + npar]]
        outs = refs[nr + npar:]
        r_o, a_o = f(*ins)
        for ref, val in zip(outs[:nro], r_o):
            ref[...] = val.astype(ref.dtype)
        i = pl.program_id(0)
        for ref, val in zip(outs[nro:], a_o):
            @pl.when(i == 0)
            def _(ref=ref, val=val):
                ref[...] = val.astype(ref.dtype)

            @pl.when(i > 0)
            def _(ref=ref, val=val):
                ref[...] += val.astype(ref.dtype)

    in_specs = ([pl.BlockSpec((tm, r.shape[1]), lambda i: (i, 0)) for r in rows]
                + [pl.BlockSpec(p.shape, lambda i: (0, 0)) for p in params])
    out_specs = ([pl.BlockSpec((tm, w), lambda i: (i, 0)) for w, _ in row_out]
                 + [pl.BlockSpec(s, lambda i: (0, 0)) for s, _ in acc_out])
    out_shape = ([jax.ShapeDtypeStruct((length, w), dt) for w, dt in row_out]
                 + [jax.ShapeDtypeStruct(s, dt) for s, dt in acc_out])
    res = pl.pallas_call(
        body, name=name, out_shape=out_shape, grid=(length // tm,),
        in_specs=in_specs, out_specs=out_specs,
        compiler_params=_params(("arbitrary",) if acc_out else ("parallel",)),
    )(*rows, *params)
    return res[:nro], res[nro:]


def _rw_fwd(f, rows, params, *, tm, name, out_dtypes=None):
    tm_ = _row_tile(rows[0].shape[0], tm)
    shapes = jax.eval_shape(f, *[jax.ShapeDtypeStruct((tm_, r.shape[1]), r.dtype) for r in rows],
                            *[jax.ShapeDtypeStruct(p.shape, p.dtype) for p in params])
    row_out = [(s.shape[1], s.dtype if out_dtypes is None else dt)
               for s, dt in zip(shapes, out_dtypes or shapes)]
    outs, _ = _rowwise(lambda *v: (f(*v), ()), rows, params, row_out, [], tm=tm, name=name)
    return outs


def _rw_bwd(f, rows, params, cots, *, row_grad, param_grad, tm, name, row_dtypes=None):
    nr, npar, nct = len(rows), len(params), len(cots)

    def g(*vals):
        prim = vals[:nr] + vals[nr + nct:]
        ct = vals[nr:nr + nct]
        _, vjp = jax.vjp(f, *prim)
        grads = vjp(tuple(ct))
        return (tuple(grads[i] for i in range(nr) if row_grad[i]),
                tuple(grads[nr + i] for i in range(npar) if param_grad[i]))

    widths = [rows[i].shape[1] for i in range(nr) if row_grad[i]]
    row_out = list(zip(widths, row_dtypes or [F32] * len(widths)))
    acc_out = [(params[i].shape, F32) for i in range(npar) if param_grad[i]]
    return _rowwise(g, list(rows) + list(cots), params, row_out, acc_out, tm=tm, name=name)


def _f_ln_res(x, h, g, b):
    pre = DN_ALPHA * x + h
    mu = jnp.mean(pre, axis=-1, keepdims=True)
    d = pre - mu
    var = jnp.mean(d * d, axis=-1, keepdims=True)
    return (d * lax.rsqrt(var + LN_EPS) * g + b,)


def _silu(t):
    return t * jax.nn.sigmoid(t)


def _f_gdn_qkv(c):
    a = _silu(c)
    outs = []
    for part, scale in ((0, HEAD_DIM ** -0.5), (1, 1.0)):
        heads = []
        for h in range(GDN_HEADS):
            t = a[:, part * D_MODEL + h * HEAD_DIM: part * D_MODEL + (h + 1) * HEAD_DIM]
            t = t * lax.rsqrt(jnp.sum(t * t, axis=-1, keepdims=True) + 1e-6)
            heads.append(t * scale if scale != 1.0 else t)
        outs.append(jnp.concatenate(heads, axis=-1))
    outs.append(a[:, 2 * D_MODEL:])
    return tuple(outs)


def _f_gdn_out(o, z, norm_g):
    heads = []
    for h in range(GDN_HEADS):
        t = o[:, h * HEAD_DIM:(h + 1) * HEAD_DIM]
        t = t * lax.rsqrt(jnp.mean(t * t, axis=-1, keepdims=True) + RMS_EPS) * norm_g
        heads.append(t)
    return (jnp.concatenate(heads, axis=-1) * _silu(z),)


def _f_attn(xq, kmem, vmem):
    heads = []
    for h in range(XA_HEADS):
        sl = slice(h * HEAD_DIM, (h + 1) * HEAD_DIM)
        s = lax.dot_general(xq[:, sl].astype(BF16), kmem[:, sl].astype(BF16),
                            (((1,), (1,)), ((), ())), preferred_element_type=F32) * (HEAD_DIM ** -0.5)
        m = lax.stop_gradient(jnp.max(s, axis=-1, keepdims=True))
        e = jnp.exp(s - m)
        p = e / jnp.sum(e, axis=-1, keepdims=True)
        heads.append(jnp.dot(p.astype(BF16), vmem[:, sl].astype(BF16), preferred_element_type=F32))
    return (jnp.concatenate(heads, axis=-1),)


def _f_s5_gelu(y, u, d):
    return (jax.nn.gelu(y + d * u),)


def _f_s5_gate(zg, t, b):
    return (zg * jax.nn.sigmoid(t + b),)


def _f_add(a, b):
    return (a + b,)


def _f_add4(a, b, c, d):
    return (((a + b.astype(F32)) + c.astype(F32)) + d.astype(F32),)


def _f_adamw(w, g, m, v):
    m = ADAM_B1 * m + (1.0 - ADAM_B1) * g
    v = ADAM_B2 * v + (1.0 - ADAM_B2) * jnp.square(g)
    m_hat = m / (1.0 - ADAM_B1 ** ADAM_STEP)
    v_hat = v / (1.0 - ADAM_B2 ** ADAM_STEP)
    delta = -ADAM_LR * (m_hat / (jnp.sqrt(v_hat) + ADAM_EPS) + ADAM_WD * w)
    return delta, m, v


def _conv_fwd(u, w, *, tm, name):
    length, chans = u.shape
    tm = min(tm, length)
    tc = _tile(chans, 1024)
    hb = tm // SUBLANES

    def body(cur_ref, prev_ref, w_ref, o_ref, buf):
        i = pl.program_id(1)
        buf[0:SUBLANES, :] = jnp.where(i > 0, prev_ref[...], 0.0)
        buf[SUBLANES:, :] = cur_ref[...]
        acc = buf[pl.ds(SUBLANES - 3, tm), :] * w_ref[0:1, :]
        for k in range(1, GDN_CONV):
            acc = acc + buf[pl.ds(SUBLANES - 3 + k, tm), :] * w_ref[k:k + 1, :]
        o_ref[...] = acc

    return pl.pallas_call(
        body, name=name, out_shape=jax.ShapeDtypeStruct(u.shape, F32),
        grid=(chans // tc, length // tm),
        in_specs=[pl.BlockSpec((tm, tc), lambda j, i: (i, j)),
                  pl.BlockSpec((SUBLANES, tc), lambda j, i: (jnp.maximum(i * hb - 1, 0), j)),
                  pl.BlockSpec((GDN_CONV, tc), lambda j, i: (0, j))],
        out_specs=pl.BlockSpec((tm, tc), lambda j, i: (i, j)),
        scratch_shapes=[pltpu.VMEM((tm + SUBLANES, tc), F32)],
        compiler_params=_params(("parallel", "parallel")),
    )(u, u, w)


def _conv_bwd(u, w, dc, *, tm, name):
    length, chans = u.shape
    tm = min(tm, length)
    tc = _tile(chans, 1024)
    hb = tm // SUBLANES
    last = length // tm - 1

    def body(u_ref, uprev_ref, dc_ref, dcnext_ref, w_ref, du_ref, dw_ref, ubuf, dbuf):
        i = pl.program_id(1)
        ubuf[0:SUBLANES, :] = jnp.where(i > 0, uprev_ref[...], 0.0)
        ubuf[SUBLANES:, :] = u_ref[...]
        dbuf[0:tm, :] = dc_ref[...]
        dbuf[tm:, :] = jnp.where(i < last, dcnext_ref[...], 0.0)
        dcv = dc_ref[...]
        du = dbuf[pl.ds(3, tm), :] * w_ref[0:1, :]
        rows = [jnp.sum(dcv * ubuf[pl.ds(SUBLANES - 3, tm), :], axis=0, keepdims=True)]
        for k in range(1, GDN_CONV):
            du = du + dbuf[pl.ds(3 - k, tm), :] * w_ref[k:k + 1, :]
            rows.append(jnp.sum(dcv * ubuf[pl.ds(SUBLANES - 3 + k, tm), :], axis=0, keepdims=True))
        du_ref[...] = du.astype(du_ref.dtype)
        dwv = jnp.concatenate(rows, axis=0)

        @pl.when(i == 0)
        def _():
            dw_ref[...] = dwv

        @pl.when(i > 0)
        def _():
            dw_ref[...] += dwv

    return pl.pallas_call(
        body, name=name,
        out_shape=(jax.ShapeDtypeStruct(u.shape, BF16), jax.ShapeDtypeStruct((GDN_CONV, chans), F32)),
        grid=(chans // tc, length // tm),
        in_specs=[pl.BlockSpec((tm, tc), lambda j, i: (i, j)),
                  pl.BlockSpec((SUBLANES, tc), lambda j, i: (jnp.maximum(i * hb - 1, 0), j)),
                  pl.BlockSpec((tm, tc), lambda j, i: (i, j)),
                  pl.BlockSpec((SUBLANES, tc), lambda j, i: (jnp.minimum((i + 1) * hb, (last + 1) * hb - 1), j)),
                  pl.BlockSpec((GDN_CONV, tc), lambda j, i: (0, j))],
        out_specs=(pl.BlockSpec((tm, tc), lambda j, i: (i, j)),
                   pl.BlockSpec((GDN_CONV, tc), lambda j, i: (0, j))),
        scratch_shapes=[pltpu.VMEM((tm + SUBLANES, tc), F32), pltpu.VMEM((tm + SUBLANES, tc), F32)],
        compiler_params=_params(("parallel", "arbitrary")),
    )(u, u, dc, dc, w)


def _dot(a, b, dims, precision=None):
    if precision is None:
        a, b = a.astype(BF16), b.astype(BF16)
    return lax.dot_general(a, b, (dims, ((), ())), preferred_element_type=F32, precision=precision)


def _dot3(a, b, dims):
    ah, bh = a.astype(BF16), b.astype(BF16)
    al, bl = (a - ah.astype(F32)).astype(BF16), (b - bh.astype(F32)).astype(BF16)

    def d(x, y):
        return lax.dot_general(x, y, (dims, ((), ())), preferred_element_type=F32)

    return d(ah, bh) + (d(ah, bl) + d(al, bh))


NN = ((1,), (0,))
NT = ((1,), (1,))
TN = ((0,), (0,))
HI = lax.Precision.HIGHEST


def _hmap(f, *lists):
    return [f(*t) for t in zip(*lists)]


@jax.custom_vjp
def _unit_lower_inverse(a):
    c = a[0].shape[0]
    eye = (lax.broadcasted_iota(jnp.int32, (c, c), 0) == lax.broadcasted_iota(jnp.int32, (c, c), 1)).astype(F32)
    p = _hmap(lambda x: -x, a)
    t = _hmap(lambda x: eye + x, p)
    for _ in range(int(math.log2(c)) - 1):
        p = _hmap(lambda x: _dot3(x, x, NN), p)
        t = _hmap(lambda x, y: x + _dot3(x, y, NN), t, p)
    return t


def _uli_fwd(a):
    t = _unit_lower_inverse(a)
    return t, t


def _uli_bwd(t, dt):
    left = _hmap(lambda x, y: _dot3(x, y, TN), t, dt)
    return (_hmap(lambda x, y: -_dot3(x, y, NT), left, t),)


_unit_lower_inverse.defvjp(_uli_fwd, _uli_bwd)


@jax.custom_vjp
def _known_inverse(a, t):
    return t


_known_inverse.defvjp(lambda a, t: (t, t),
                      lambda t, dt: (_uli_bwd(t, dt)[0], _hmap(jnp.zeros_like, t)))


def _gdn_chunk(q, k, v, bl, al, a_log, dt_bias, state, t_known=None):
    c = q[0].shape[0]
    row = lax.broadcasted_iota(jnp.int32, (c, c), 0)
    col = lax.broadcasted_iota(jnp.int32, (c, c), 1)
    causal = row >= col
    strict = row > col
    eye = (row == col).astype(F32)
    beta = _hmap(jax.nn.sigmoid, bl)
    g = _hmap(lambda a_, l_, d_: -jnp.exp(a_) * jax.nn.softplus(l_ + d_), a_log, al, dt_bias)
    g_r = _hmap(lambda x: jnp.sum(eye * x, axis=0, keepdims=True), g)
    gc = _hmap(lambda x: jnp.sum(jnp.where(causal, x, 0.0), axis=1, keepdims=True), g_r)
    gc_r = _hmap(lambda x: jnp.sum(jnp.where(row <= col, x, 0.0), axis=0, keepdims=True), g)
    decay = _hmap(lambda x, y: jnp.where(causal, jnp.exp(jnp.where(causal, x - y, 0.0)), 0.0), gc, gc_r)
    e_gc = _hmap(jnp.exp, gc)
    kb = _hmap(jnp.multiply, k, beta)
    vb = _hmap(jnp.multiply, v, beta)
    a_mat = _hmap(lambda x, y, d: jnp.where(strict, _dot(x, y, NT) * d, 0.0), kb, k, decay)
    t_inv = _unit_lower_inverse(a_mat) if t_known is None else _known_inverse(a_mat, t_known)
    u_blk = _hmap(lambda t, x: _dot(t, x, NN), t_inv, vb)
    w_blk = _hmap(lambda t, x, e: _dot(t, x * e, NN), t_inv, kb, e_gc)
    v_new = _hmap(lambda u, w, s: u - _dot(w, s, NN), u_blk, w_blk, state)
    attn = _hmap(lambda x, y, d: _dot(x, y, NT) * d, q, k, decay)
    o_state = _hmap(lambda x, e, s: _dot(x * e, s, NN), q, e_gc, state)
    o = _hmap(lambda base, at, vn: base + _dot(at, vn, NN), o_state, attn, v_new)
    g_last = _hmap(lambda x: jnp.sum(x, axis=0, keepdims=True), g)
    k_dec = _hmap(lambda x, gl, c_: x * jnp.exp(gl - c_), k, g_last, gc)
    new_state = _hmap(lambda s, gl, kd, vn: s * jnp.exp(gl) + _dot(kd, vn, TN), state, g_last, k_dec, v_new)
    return (o, new_state, t_inv) if t_known is None else (o, new_state)


def _gdn_operands(q_ref, k_ref, v_ref, bav, alog_ref, dtb_ref):
    hs = range(GDN_HEADS)
    cols = [slice(h * HEAD_DIM, (h + 1) * HEAD_DIM) for h in hs]
    return ([q_ref[:, sl] for sl in cols], [k_ref[:, sl] for sl in cols], [v_ref[:, sl] for sl in cols],
            [bav[:, h:h + 1] for h in hs], [bav[:, h + GDN_HEADS:h + GDN_HEADS + 1] for h in hs],
            [alog_ref[h:h + 1, 0:1] for h in hs], [dtb_ref[h:h + 1, 0:1] for h in hs])


def _gdn_scan_fwd(q, k, v, ba, a_log, dt_bias, *, name):
    length = q.shape[0]
    n = length // GDN_CHUNK
    c = GDN_CHUNK

    def body(q_ref, k_ref, v_ref, ba_ref, alog_ref, dtb_ref, o_ref, s_ref, t_ref, state):
        i = pl.program_id(0)

        @pl.when(i == 0)
        def _():
            state[...] = jnp.zeros_like(state)

        bav = ba_ref[...]
        heads = [slice(h * HEAD_DIM, (h + 1) * HEAD_DIM) for h in range(GDN_HEADS)]
        s_in = [state[h] for h in range(GDN_HEADS)]
        o, s_out, t_inv = _gdn_chunk(*_gdn_operands(q_ref, k_ref, v_ref, bav, alog_ref, dtb_ref), s_in)
        for h, sl in enumerate(heads):
            s_ref[h] = s_in[h]
            t_ref[h] = t_inv[h]
            o_ref[:, sl] = o[h]
            state[h] = s_out[h]

    row_spec = pl.BlockSpec((c, D_MODEL), lambda i: (i, 0))
    small = pl.BlockSpec((GDN_HEADS, LANES), lambda i: (0, 0))
    return pl.pallas_call(
        body, name=name,
        out_shape=(jax.ShapeDtypeStruct((length, D_MODEL), F32),
                   jax.ShapeDtypeStruct((n, GDN_HEADS, HEAD_DIM, HEAD_DIM), F32),
                   jax.ShapeDtypeStruct((n, GDN_HEADS, c, c), F32)),
        grid=(n,),
        in_specs=[row_spec, row_spec, row_spec, pl.BlockSpec((c, LANES), lambda i: (i, 0)), small, small],
        out_specs=(row_spec, pl.BlockSpec((None, GDN_HEADS, HEAD_DIM, HEAD_DIM), lambda i: (i, 0, 0, 0)),
                   pl.BlockSpec((None, GDN_HEADS, c, c), lambda i: (i, 0, 0, 0))),
        scratch_shapes=[pltpu.VMEM((GDN_HEADS, HEAD_DIM, HEAD_DIM), F32)],
        compiler_params=_params(("arbitrary",)),
    )(q, k, v, ba, a_log, dt_bias)


def _gdn_scan_bwd(q, k, v, ba, a_log, dt_bias, states, inverses, do, *, name):
    length = q.shape[0]
    n = length // GDN_CHUNK
    c = GDN_CHUNK

    def body(q_ref, k_ref, v_ref, ba_ref, alog_ref, dtb_ref, s_ref, t_ref, do_ref,
             dq_ref, dk_ref, dv_ref, dba_ref, dalog_ref, ddtb_ref, dstate):
        i = pl.program_id(0)

        @pl.when(i == 0)
        def _():
            dstate[...] = jnp.zeros_like(dstate)
            dalog_ref[...] = jnp.zeros_like(dalog_ref)
            ddtb_ref[...] = jnp.zeros_like(ddtb_ref)

        bav = ba_ref[...]
        lane = lax.broadcasted_iota(jnp.int32, (c, LANES), 1)
        sub8 = lax.broadcasted_iota(jnp.int32, (GDN_HEADS, LANES), 0)
        lane8 = lax.broadcasted_iota(jnp.int32, (GDN_HEADS, LANES), 1)
        slab = jnp.zeros((c, LANES), F32)
        dalog_all = jnp.zeros((GDN_HEADS, LANES), F32)
        ddtb_all = jnp.zeros((GDN_HEADS, LANES), F32)
        heads = [slice(h * HEAD_DIM, (h + 1) * HEAD_DIM) for h in range(GDN_HEADS)]
        ds_in = [dstate[h] for h in range(GDN_HEADS)]
        s_in = [s_ref[h] for h in range(GDN_HEADS)]
        t_known = [t_ref[h] for h in range(GDN_HEADS)]
        _, vjp = jax.vjp(functools.partial(_gdn_chunk, t_known=t_known),
                         *_gdn_operands(q_ref, k_ref, v_ref, bav, alog_ref, dtb_ref), s_in)
        dq, dk, dv, dbl, dal, dalog, ddtb, ds = vjp(([do_ref[:, sl] for sl in heads], ds_in))
        for h, sl in enumerate(heads):
            dq_ref[:, sl] = dq[h]
            dk_ref[:, sl] = dk[h]
            dv_ref[:, sl] = dv[h]
            dstate[h] = ds[h]
            slab = slab + jnp.where(lane == h, dbl[h], 0.0) + jnp.where(lane == h + GDN_HEADS, dal[h], 0.0)
            here = (sub8 == h) & (lane8 == 0)
            dalog_all = dalog_all + jnp.where(here, dalog[h], 0.0)
            ddtb_all = ddtb_all + jnp.where(here, ddtb[h], 0.0)
        dba_ref[...] = slab
        dalog_ref[...] += dalog_all
        ddtb_ref[...] += ddtb_all

    row_spec = pl.BlockSpec((c, D_MODEL), lambda i: (n - 1 - i, 0))
    small = pl.BlockSpec((GDN_HEADS, LANES), lambda i: (0, 0))
    return pl.pallas_call(
        body, name=name,
        out_shape=(jax.ShapeDtypeStruct((length, D_MODEL), F32),) * 3
        + (jax.ShapeDtypeStruct((length, LANES), F32),
           jax.ShapeDtypeStruct((GDN_HEADS, LANES), F32), jax.ShapeDtypeStruct((GDN_HEADS, LANES), F32)),
        grid=(n,),
        in_specs=[row_spec, row_spec, row_spec,
                  pl.BlockSpec((c, LANES), lambda i: (n - 1 - i, 0)), small, small,
                  pl.BlockSpec((None, GDN_HEADS, HEAD_DIM, HEAD_DIM), lambda i: (n - 1 - i, 0, 0, 0)),
                  pl.BlockSpec((None, GDN_HEADS, c, c), lambda i: (n - 1 - i, 0, 0, 0)),
                  row_spec],
        out_specs=(row_spec, row_spec, row_spec,
                   pl.BlockSpec((c, LANES), lambda i: (n - 1 - i, 0)), small, small),
        scratch_shapes=[pltpu.VMEM((GDN_HEADS, HEAD_DIM, HEAD_DIM), F32)],
        compiler_params=_params(("arbitrary",)),
    )(q, k, v, ba, a_log, dt_bias, states, inverses, do)


S5_W = S5_T * LANES
S5_S = 2 * 8 * S5_STATE
S5_SH = S5_S // 2


def _iota2(shape):
    return lax.broadcasted_iota(jnp.int32, shape, 0), lax.broadcasted_iota(jnp.int32, shape, 1)


def _s5_rep_t(t, dtype):
    row, col = _iota2((S5_T * S5_GROUP, LANES))
    return ((jnp.right_shift(row, 4) == t) & (jnp.bitwise_and(row, 15) == jnp.bitwise_and(col, 15))).astype(dtype)


def _s5_rep_state(dtype):
    row, col = _iota2((2 * S5_STATE, S5_S))
    return ((jnp.right_shift(row, 6) == jnp.right_shift(col, 9))
            & (jnp.bitwise_and(row, 63) == jnp.bitwise_and(col, 63))).astype(dtype)


def _s5_masks():
    row, col = _iota2((LANES, LANES))
    m_ab = jnp.right_shift(row, 4) == jnp.right_shift(col, 4)
    row, col = _iota2((S5_S, LANES))
    m_e = jnp.bitwise_and(jnp.right_shift(row, 6), 7) == jnp.right_shift(col, 4)
    row, col = _iota2((LANES, S5_S))
    m_f = jnp.right_shift(row, 4) == jnp.bitwise_and(jnp.right_shift(col, 6), 7)
    return m_ab, m_e, m_f


def _s5_expand(kx_ref, ec_ref, fc_ref, kb_scr, e_scr, f_scr):
    m_ab, m_e, m_f = _s5_masks()
    kx = kx_ref[...].astype(BF16)
    ec = ec_ref[...].astype(BF16)
    rep_state = _s5_rep_state(BF16)
    for t in range(S5_T):
        rep = _s5_rep_t(t, BF16)
        cols = slice(t * LANES, (t + 1) * LANES)
        kb_scr[t] = jnp.where(m_ab, jnp.dot(kx, rep, preferred_element_type=F32), 0.0).astype(BF16)
        e_scr[:, cols] = jnp.where(m_e, jnp.dot(ec, rep, preferred_element_type=F32), 0.0).astype(BF16)
        f_scr[cols, :] = jnp.where(m_f, jnp.dot(fc_ref[t].astype(BF16), rep_state, preferred_element_type=F32),
                                   0.0).astype(BF16)


def _s5_token_rows(ref, n):
    return [ref[pl.ds(t, n, stride=S5_T), :].astype(BF16) for t in range(S5_T)]


def _s5_scan_fwd(u, kx, ec, fc, at, *, name):
    length = u.shape[0]
    n = length // S5_T
    assert n % SUBLANES == 0

    def body(u_ref, kx_ref, ec_ref, fc_ref, at_ref, y_ref, h_ref, kb_scr, e_scr, f_scr, g_scr):
        _s5_expand(kx_ref, ec_ref, fc_ref, kb_scr, e_scr, f_scr)
        us = _s5_token_rows(u_ref, n)
        g_scr[...] = jnp.dot(jnp.concatenate(us, axis=1), f_scr[...], preferred_element_type=F32)
        ar, ai = at_ref[:, :S5_SH], at_ref[:, S5_SH:]

        def step(blk, h):
            base = pl.multiple_of(blk * SUBLANES, SUBLANES)
            g8 = g_scr[pl.ds(base, SUBLANES), :]
            rows = []
            for r in range(SUBLANES):
                rows.append(h)
                hr, hi = h[:, :S5_SH], h[:, S5_SH:]
                h = jnp.concatenate([ar * hr - ai * hi, ar * hi + ai * hr], axis=1) + g8[r:r + 1, :]
            h_ref[pl.ds(base, SUBLANES), :] = jnp.concatenate(rows, axis=0)
            return h

        lax.fori_loop(0, n // SUBLANES, step, jnp.zeros((1, S5_S), F32))
        hb = h_ref[...].astype(BF16)
        for t in range(S5_T):
            acc = jnp.dot(hb, e_scr[:, t * LANES:(t + 1) * LANES], preferred_element_type=F32)
            for s in range(t + 1):
                acc = acc + jnp.dot(us[s], kb_scr[t - s], preferred_element_type=F32)
            y_ref[pl.ds(t, n, stride=S5_T), :] = acc

    return pl.pallas_call(
        body, name=name,
        out_shape=(jax.ShapeDtypeStruct((length, D_MODEL), F32), jax.ShapeDtypeStruct((S5_TILES, n, S5_S), F32)),
        grid=(S5_TILES,),
        in_specs=[pl.BlockSpec((length, LANES), lambda k: (0, k)), _s5_spec(LANES, S5_T * S5_GROUP),
                  _s5_spec(S5_S, S5_T * S5_GROUP), _s5_spec(S5_T, LANES, LANES), _s5_spec(1, S5_S)],
        out_specs=(pl.BlockSpec((length, LANES), lambda k: (0, k)), _s5_spec(n, S5_S)),
        scratch_shapes=[pltpu.VMEM((S5_T, LANES, LANES), BF16), pltpu.VMEM((S5_S, S5_W), BF16),
                        pltpu.VMEM((S5_W, S5_S), BF16), pltpu.VMEM((n, S5_S), F32)],
        compiler_params=_params(("parallel",)),
    )(u, kx, ec, fc, at)


def _s5_spec(*tail):
    return pl.BlockSpec((None,) + tail, lambda k: (k,) + (0,) * len(tail))


def _s5_scan_bwd(dy, kx, ec, fc, at, hs, *, name):
    length = dy.shape[0]
    n = length // S5_T

    def body(dy_ref, kx_ref, ec_ref, fc_ref, at_ref, h_ref, du_ref, dg_ref, dat_ref, kb_scr, e_scr, f_scr, dh_scr):
        _s5_expand(kx_ref, ec_ref, fc_ref, kb_scr, e_scr, f_scr)
        dys = _s5_token_rows(dy_ref, n)
        dh_scr[...] = _dot(jnp.concatenate(dys, axis=1), e_scr[...], NT)
        ar, ai = at_ref[:, :S5_SH], at_ref[:, S5_SH:]

        def step(it, carry):
            cy, dat = carry
            base = pl.multiple_of((n // SUBLANES - 1 - it) * SUBLANES, SUBLANES)
            dh8 = dh_scr[pl.ds(base, SUBLANES), :]
            h8 = h_ref[pl.ds(base, SUBLANES), :]
            rows = [None] * SUBLANES
            for r in reversed(range(SUBLANES)):
                rows[r] = cy
                cr, ci = cy[:, :S5_SH], cy[:, S5_SH:]
                hr, hi = h8[r:r + 1, :S5_SH], h8[r:r + 1, S5_SH:]
                dat = dat + jnp.concatenate([cr * hr + ci * hi, ci * hr - cr * hi], axis=1)
                cy = dh8[r:r + 1, :] + jnp.concatenate([ar * cr + ai * ci, ar * ci - ai * cr], axis=1)
            dg_ref[pl.ds(base, SUBLANES), :] = jnp.concatenate(rows, axis=0)
            return cy, dat

        zero = jnp.zeros((1, S5_S), F32)
        _, dat = lax.fori_loop(0, n // SUBLANES, step, (zero, zero))
        dat_ref[...] = dat
        dgb = dg_ref[...].astype(BF16)
        for s in range(S5_T):
            acc = _dot(dgb, f_scr[s * LANES:(s + 1) * LANES, :], NT)
            for t in range(s, S5_T):
                acc = acc + _dot(dys[t], kb_scr[t - s], NT)
            du_ref[pl.ds(s, n, stride=S5_T), :] = acc

    row_spec = pl.BlockSpec((length, LANES), lambda k: (0, k))
    return pl.pallas_call(
        body, name=name,
        out_shape=(jax.ShapeDtypeStruct((length, D_MODEL), F32), jax.ShapeDtypeStruct((S5_TILES, n, S5_S), F32),
                   jax.ShapeDtypeStruct((S5_TILES, 1, S5_S), F32)),
        grid=(S5_TILES,),
        in_specs=[row_spec, _s5_spec(LANES, S5_T * S5_GROUP), _s5_spec(S5_S, S5_T * S5_GROUP),
                  _s5_spec(S5_T, LANES, LANES), _s5_spec(1, S5_S), _s5_spec(n, S5_S)],
        out_specs=(row_spec, _s5_spec(n, S5_S), _s5_spec(1, S5_S)),
        scratch_shapes=[pltpu.VMEM((S5_T, LANES, LANES), BF16), pltpu.VMEM((S5_S, S5_W), BF16),
                        pltpu.VMEM((S5_W, S5_S), BF16), pltpu.VMEM((n, S5_S), F32)],
        compiler_params=_params(("parallel",)),
    )(dy, kx, ec, fc, at, hs)


def _s5_operator_grads(dy, u, hs, dg, *, name):
    length = u.shape[0]
    n = length // S5_T

    def body(dy_ref, u_ref, h_ref, dg_ref, dkx_ref, dec_ref, dfc_ref):
        dys = _s5_token_rows(dy_ref, n)
        us = _s5_token_rows(u_ref, n)
        ucat = jnp.concatenate(us, axis=1)
        m_ab, m_e, m_f = _s5_masks()
        hb = h_ref[...].astype(BF16)
        dgb = dg_ref[...].astype(BF16)
        lane = lax.broadcasted_iota(jnp.int32, (1, LANES), 1)
        lane_group = jnp.right_shift(lane, 4)

        def own_block(x, mask):
            x = jnp.where(mask, x, 0.0)
            for shift in (64, 32, 16):
                x = x + pltpu.roll(x, shift, 1)
            return x

        def place(halves, t, x):
            halves[t // 8] = jnp.where(lane_group == t % 8, x, halves[t // 8])

        dkb = [jnp.zeros((LANES, LANES), F32) for _ in range(S5_T)]
        dec = [jnp.zeros((S5_S, LANES), F32) for _ in range(2)]
        for t in range(S5_T):
            d_t = _dot(ucat, dys[t], TN)
            for s in range(t + 1):
                dkb[t - s] = dkb[t - s] + d_t[s * LANES:(s + 1) * LANES, :]
            place(dec, t, own_block(_dot(hb, dys[t], TN), m_e))
            wide = jnp.where(m_f, _dot(us[t], dgb, TN), 0.0)
            parts = []
            for r in range(2):
                acc = wide[:, r * S5_SH:r * S5_SH + LANES]
                for q in range(1, S5_SH // LANES):
                    acc = acc + wide[:, r * S5_SH + q * LANES:r * S5_SH + (q + 1) * LANES]
                parts.append(acc + pltpu.roll(acc, S5_STATE, 1))
            dfc_ref[t] = jnp.where(lane < S5_STATE, parts[0], parts[1])
        dkx = [jnp.zeros((LANES, LANES), F32) for _ in range(2)]
        for t in range(S5_T):
            place(dkx, t, own_block(dkb[t], m_ab))
        dkx_ref[...] = jnp.concatenate(dkx, axis=1)
        dec_ref[...] = jnp.concatenate(dec, axis=1)

    row_spec = pl.BlockSpec((length, LANES), lambda k: (0, k))
    outs = (_s5_spec(LANES, S5_T * S5_GROUP), _s5_spec(S5_S, S5_T * S5_GROUP), _s5_spec(S5_T, LANES, LANES))
    return pl.pallas_call(
        body, name=name,
        out_shape=(jax.ShapeDtypeStruct((S5_TILES, LANES, S5_T * S5_GROUP), F32),
                   jax.ShapeDtypeStruct((S5_TILES, S5_S, S5_T * S5_GROUP), F32),
                   jax.ShapeDtypeStruct((S5_TILES, S5_T, LANES, LANES), F32)),
        grid=(S5_TILES,),
        in_specs=[row_spec, row_spec, _s5_spec(n, S5_S), _s5_spec(n, S5_S)],
        out_specs=outs,
        compiler_params=_params(("parallel",)),
    )(dy, u, hs, dg)


def _s5_prep(a_re, a_im, b_re, b_im, c_re, c_im, log_dt):
    t_len, tiles = S5_T, S5_TILES
    dt = jnp.exp(log_dt)[:, None]
    mag = jnp.exp(a_re * dt)
    ab_re, ab_im = mag * jnp.cos(a_im * dt), mag * jnp.sin(a_im * dt)
    den = jnp.square(a_re) + jnp.square(a_im)
    n_re, n_im = ab_re - 1.0, ab_im
    f_re = (n_re * a_re + n_im * a_im) / den
    f_im = (n_im * a_re - n_re * a_im) / den
    bb_re = f_re[..., None] * b_re - f_im[..., None] * b_im
    bb_im = f_re[..., None] * b_im + f_im[..., None] * b_re

    def powers(exponents):
        e = exponents[:, None, None]
        m = jnp.exp(e * (a_re * dt))
        return m * jnp.cos(e * (a_im * dt)), m * jnp.sin(e * (a_im * dt))

    p_re, p_im = powers(jnp.arange(t_len + 1, dtype=F32))
    rev_re, rev_im = powers((t_len - 1) - jnp.arange(t_len, dtype=F32))
    ca_re = c_re[None] * p_re[:, :, None, :] - c_im[None] * p_im[:, :, None, :]
    ca_im = c_re[None] * p_im[:, :, None, :] + c_im[None] * p_re[:, :, None, :]
    lag = (jnp.einsum('tgip,gpj->tgij', ca_re[:t_len], bb_re, precision=HI)
           - jnp.einsum('tgip,gpj->tgij', ca_im[:t_len], bb_im, precision=HI))
    kx = lag.reshape(t_len, tiles, 8, S5_GROUP, S5_GROUP).transpose(1, 2, 4, 0, 3)
    kx = kx.reshape(tiles, LANES, t_len * S5_GROUP)
    e_st = jnp.stack([ca_re[1:], -ca_im[1:]])
    e_st = e_st.reshape(2, t_len, tiles, 8, S5_GROUP, S5_STATE).transpose(2, 0, 3, 5, 1, 4)
    ec = e_st.reshape(tiles, S5_S, t_len * S5_GROUP)
    ab_b = jnp.stack([rev_re[..., None] * bb_re[None] - rev_im[..., None] * bb_im[None],
                      rev_re[..., None] * bb_im[None] + rev_im[..., None] * bb_re[None]])
    ab_b = ab_b.reshape(2, t_len, tiles, 8, S5_STATE, S5_GROUP).transpose(2, 1, 3, 5, 0, 4)
    fc = ab_b.reshape(tiles, t_len, LANES, 2 * S5_STATE)
    a_t = jnp.stack([p_re[t_len], p_im[t_len]]).reshape(2, tiles, 8 * S5_STATE).transpose(1, 0, 2)
    return kx, ec, fc, a_t.reshape(tiles, 1, S5_S)


TM_ROW = 256


def _gdn_fwd(x, w, tag):
    qkv = _mm(x, w["wqkv"], name="gdn_proj_qkv")
    z = _mm(x, w["wz"], name="gdn_proj_z")
    ba = _mm(x, w["wba"], name="gdn_proj_ba")
    cv = _conv_fwd(qkv, w["conv_w"], tm=TM_ROW, name="gdn_conv")
    q, k, v = _rw_fwd(_f_gdn_qkv, [cv], [], tm=TM_ROW, name="gdn_qkv")
    o, states, inverses = _gdn_scan_fwd(q, k, v, ba, w["a_log8"], w["dt_bias8"], name="gdn_scan")
    (mix,) = _rw_fwd(_f_gdn_out, [o, z], [w["norm_g"]], tm=TM_ROW, name="gdn_out", out_dtypes=[BF16])
    return mix, (qkv, z, ba, cv, q, k, v, states, inverses, o)


def _gdn_bwd(x, w, saved, dmix, dx_acc):
    qkv, z, ba, cv, q, k, v, states, inverses, o = saved
    (do, dz), (dnorm_g,) = _rw_bwd(_f_gdn_out, [o, z], [w["norm_g"]], [dmix], row_grad=[1, 1], param_grad=[1],
                                   tm=TM_ROW, name="gdn_out_bwd", row_dtypes=[F32, BF16])
    dq, dk, dv, dba, dalog, ddtb = _gdn_scan_bwd(q, k, v, ba, w["a_log8"], w["dt_bias8"], states, inverses, do,
                                                  name="gdn_scan_bwd")
    (dcv,), _ = _rw_bwd(_f_gdn_qkv, [cv], [], [dq, dk, dv], row_grad=[1], param_grad=[], tm=TM_ROW,
                        name="gdn_qkv_bwd")
    dqkv, dconv_w = _conv_bwd(qkv, w["conv_w"], dcv, tm=TM_ROW, name="gdn_conv_bwd")
    dx = _mm(dqkv, w["wqkv"], tb=True, acc=dx_acc, name="gdn_dx_qkv")
    dx = _mm(dz, w["wz"], tb=True, acc=dx, name="gdn_dx_z")
    dx = _mm(dba, w["wba"], tb=True, acc=dx, name="gdn_dx_ba")
    grads = dict(wqkv=_mm(x, dqkv, ta=True, name="gdn_dw_qkv"), wz=_mm(x, dz, ta=True, name="gdn_dw_z"),
                 wba=_mm(x, dba, ta=True, name="gdn_dw_ba"), conv_w=dconv_w,
                 a_log=dalog[:, 0], dt_bias=ddtb[:, 0], norm_g=dnorm_g[0])
    return dx, grads


def _s5_fwd(x, w, tag):
    u = _mm(x, w["wu"], name="s5_proj_u")
    y, hs = _s5_scan_fwd(u, w["kx"], w["ec"], w["fc"], w["a_t"], name="s5_scan")
    (zg,) = _rw_fwd(_f_s5_gelu, [y, u], [w["d"]], tm=TM_ROW, name="s5_gelu")
    t = _mm(zg, w["w_glu"], name="s5_glu")
    (mix,) = _rw_fwd(_f_s5_gate, [zg, t], [w["b_glu"]], tm=TM_ROW, name="s5_gate", out_dtypes=[BF16])
    return mix, (u, hs, y, zg, t)


def _s5_bwd(x, w, saved, dmix, dx_acc):
    u, hs, y, zg, t = saved
    (dzg, dt), (db_glu,) = _rw_bwd(_f_s5_gate, [zg, t], [w["b_glu"]], [dmix], row_grad=[1, 1], param_grad=[1],
                                   tm=TM_ROW, name="s5_gate_bwd", row_dtypes=[F32, BF16])
    dzg = _mm(dt, w["w_glu"], tb=True, acc=dzg, name="s5_dzg")
    dw_glu = _mm(zg, dt, ta=True, name="s5_dw_glu")
    (dy, du), (dd,) = _rw_bwd(_f_s5_gelu, [y, u], [w["d"]], [dzg], row_grad=[1, 1], param_grad=[1],
                              tm=TM_ROW, name="s5_gelu_bwd")
    du_scan, dg, dat = _s5_scan_bwd(dy, w["kx"], w["ec"], w["fc"], w["a_t"], hs, name="s5_scan_bwd")
    dkx, dec, dfc = _s5_operator_grads(dy, u, hs, dg, name="s5_operator_grads")
    (du,) = _rw_fwd(_f_add, [du, du_scan], [], tm=TM_ROW, name="s5_du_add", out_dtypes=[BF16])
    d_a_re, d_a_im, d_b_re, d_b_im, d_c_re, d_c_im, d_log_dt = w["prep_vjp"]((dkx, dec, dfc, dat))
    dx = _mm(du, w["wu"], tb=True, acc=dx_acc, name="s5_dx_u")
    grads = dict(wu=_mm(x, du, ta=True, name="s5_dw_u"), w_glu=dw_glu, b_glu=db_glu[0], d=dd[0],
                 a_re=d_a_re, a_im=d_a_im, b_re=d_b_re, b_im=d_b_im, c_re=d_c_re, c_im=d_c_im, log_dt=d_log_dt)
    return dx, grads


def _ln_res_both(x, h, g, b):
    (y,) = _f_ln_res(x, h, g, b)
    return y, y


def _layer_fwd(x, xb, mem, w, is_gdn):
    mix, msave = (_gdn_fwd if is_gdn else _s5_fwd)(xb, w, "")
    xq = _mm(xb, w["wxq"], name="proj_xq")
    kv = _mm(mem, w["wkv"], name="mem_kv")
    kmem, vmem = kv[:, :XA_DIM], kv[:, XA_DIM:]
    (cross,) = _rw_fwd(_f_attn, [xq], [kmem, vmem], tm=TM_ROW, name="attn", out_dtypes=[BF16])
    h = _mm(mix, w["wo_mix"], name="wo_mix")
    h = _mm(cross, w["wo_cross"], acc=h, name="wo_cross")
    x1, x1b = _rw_fwd(_ln_res_both, [x, h], [w["ln1_g"], w["ln1_b"]], tm=TM_ROW, name="ln_res",
                      out_dtypes=[F32, BF16])
    hm, act = _mm_relu2(x1b, w["w1"], name="mlp_up")
    f = _mm(act, w["w2"], name="mlp_down")
    x2, x2b = _rw_fwd(_ln_res_both, [x1, f], [w["ln2_g"], w["ln2_b"]], tm=TM_ROW, name="ln_res",
                      out_dtypes=[F32, BF16])
    return x2, x2b, (x, xb, msave, xq, kmem, vmem, mix, cross, h, x1, x1b, hm, act, f)


def _layer_bwd(mem, w, is_gdn, saved, dx2, token=None):
    x, xb, msave, xq, kmem, vmem, mix, cross, h, x1, x1b, hm, act, f = saved
    ln2_g = w["ln2_g"] if token is None else w["ln2_g"] + token[0, 0]
    (dx1, df), (dg2, db2) = _rw_bwd(_f_ln_res, [x1, f], [ln2_g, w["ln2_b"]], [dx2], row_grad=[1, 1],
                                    param_grad=[1, 1], tm=TM_ROW, name="ln_res_bwd", row_dtypes=[F32, BF16])
    dhm = _mm_relu2_grad(df, w["w2"], hm, name="mlp_dhm")
    dw2 = _mm(act, df, ta=True, name="mlp_dw2")
    dx1 = _mm(dhm, w["w1"], tb=True, acc=dx1, name="mlp_dx")
    dw1 = _mm(x1b, dhm, ta=True, out_blocks=N_CHIPS, name="mlp_dw1")
    (dx, dh), (dg1, db1) = _rw_bwd(_f_ln_res, [x, h], [w["ln1_g"], w["ln1_b"]], [dx1], row_grad=[1, 1],
                                   param_grad=[1, 1], tm=TM_ROW, name="ln_res_bwd", row_dtypes=[F32, BF16])
    dmix =_mm(dh, w["wo_mix"], tb=True, name="wo_dmix")
    dcross = _mm(dh, w["wo_cross"], tb=True, name="wo_dcross")
    dwo = jnp.concatenate([_mm(mix, dh, ta=True, name="wo_dw_mix"), _mm(cross, dh, ta=True, name="wo_dw_cross")], 0)
    (dxq,), (dkmem, dvmem) = _rw_bwd(_f_attn, [xq], [kmem, vmem], [dcross], row_grad=[1], param_grad=[1, 1],
                                     tm=TM_ROW, name="attn_bwd", row_dtypes=[BF16])
    dwkv = _mm(mem, jnp.concatenate([dkmem, dvmem], axis=1), ta=True, name="mem_dw_kv")
    dx = _mm(dxq, w["wxq"], tb=True, acc=dx, name="dx_xq")
    dwxq = _mm(xb, dxq, ta=True, name="dw_xq")
    dx, mg = (_gdn_bwd if is_gdn else _s5_bwd)(xb, w, msave, dmix, dx)
    grads = dict(mixer=mg, wxq=dwxq, wkv=dwkv, wo=dwo, w1=dw1, w2=dw2,
                 ln1_g=dg1[0], ln1_b=db1[0], ln2_g=dg2[0], ln2_b=db2[0])
    return dx, grads


def _loss_and_grad(y, target):
    def f(yv, tv):
        err = yv - tv
        return (err * (1.0 / D_MODEL),), (0.5 / D_MODEL * jnp.sum(err * err, axis=0, keepdims=True),)

    (dy,), (part,) = _rowwise(f, [y, target], [], [(D_MODEL, F32)], [((1, D_MODEL), F32)], tm=512, name="loss")
    return jnp.sum(part), dy


def _layer_weights(full, i):
    j = i // 2
    w = dict(wkv=full["w_kv_mem"][i].astype(BF16),
             wo_mix=full["w_o"][i][:D_MODEL].astype(BF16), wo_cross=full["w_o"][i][D_MODEL:].astype(BF16),
             ln1_g=full["ln1_g"][i][None], ln1_b=full["ln1_b"][i][None],
             ln2_g=full["ln2_g"][i][None], ln2_b=full["ln2_b"][i][None],
             w1=full["mlp_w1"][i].astype(BF16), w2=full["mlp_w2"][i].astype(BF16))
    if i % 2 == 0:
        w_in = full["gdn_w_in"][j]
        gd = 3 * D_MODEL
        w.update(wqkv=w_in[:, :gd].astype(BF16), wz=w_in[:, gd:gd + D_MODEL].astype(BF16),
                 wba=jnp.pad(w_in[:, gd + D_MODEL:gd + D_MODEL + 2 * GDN_HEADS],
                             ((0, 0), (0, LANES - 2 * GDN_HEADS))).astype(BF16),
                 wxq=w_in[:, gd + D_MODEL + 2 * GDN_HEADS:].astype(BF16),
                 conv_w=full["gdn_conv_w"][j],
                 a_log8=jnp.broadcast_to(full["gdn_a_log"][j][:, None], (GDN_HEADS, LANES)),
                 dt_bias8=jnp.broadcast_to(full["gdn_dt_bias"][j][:, None], (GDN_HEADS, LANES)),
                 norm_g=full["gdn_norm_g"][j][None])
    else:
        w_in = full["s5_w_in"][j]
        (kx, ec, fc, a_t), prep_vjp = jax.vjp(
            _s5_prep, full["s5_a_re"][j], full["s5_a_im"][j], full["s5_b_re"][j], full["s5_b_im"][j],
            full["s5_c_re"][j], full["s5_c_im"][j], full["s5_log_dt"][j])
        w.update(wu=w_in[:, :D_MODEL].astype(BF16), wxq=w_in[:, D_MODEL:].astype(BF16),
                 kx=kx, ec=ec, fc=fc, a_t=a_t, prep_vjp=prep_vjp,
                 d=full["s5_d"][j][None], w_glu=full["s5_w_glu"][j].astype(BF16), b_glu=full["s5_b_glu"][j][None])
    return w


def _sharded_grads(l, i):
    m = l["mixer"]
    out = dict(w_kv_mem=l["wkv"], w_o=l["wo"], mlp_w1=l["w1"], mlp_w2=l["w2"])
    if i % 2 == 0:
        out.update(gdn_w_in=jnp.concatenate([m["wqkv"], m["wz"], m["wba"][:, :2 * GDN_HEADS], l["wxq"]], axis=1),
                   gdn_conv_w=m["conv_w"])
    else:
        out.update(s5_w_in=jnp.concatenate([m["wu"], l["wxq"]], axis=1), s5_d=m["d"], s5_w_glu=m["w_glu"],
                   s5_b_glu=m["b_glu"])
    return out


def _replicated_grads(layer_grads):
    g = layer_grads
    gdn = [g[i]["mixer"] for i in range(DEPTH) if i % 2 == 0]
    s5 = [g[i]["mixer"] for i in range(DEPTH) if i % 2 == 1]
    out = {n: jnp.stack([l[n] for l in g]) for n in ("ln1_g", "ln1_b", "ln2_g", "ln2_b")}
    out.update({"gdn_" + n: jnp.stack([m[n] for m in gdn]) for n in ("a_log", "dt_bias", "norm_g")})
    out.update({"s5_" + n: jnp.stack([m[n] for m in s5])
                for n in ("a_re", "a_im", "b_re", "b_im", "c_re", "c_im", "log_dt")})
    return out


def _local_step(x, mem, target, weights_of, grads_ready):
    lw, saves = [], []
    h, hb = x, x.astype(BF16)
    for i in range(DEPTH):
        lw.append(weights_of(i, h))
        h, hb, s = _layer_fwd(h, hb, mem, lw[i], i % 2 == 0)
        saves.append(s)
    loss, d = _loss_and_grad(h, target)
    grads = [None] * DEPTH
    token = None
    for i in reversed(range(DEPTH)):
        d, grads[i] = _layer_bwd(mem, lw[i], i % 2 == 0, saves[i], d, token)
        token = grads_ready(i, grads[i])
    return loss, d, grads


ANY = pl.BlockSpec(memory_space=pl.ANY)
SHARD_ROWS = 1024
SMALL_ROWS = 128


def _place():
    return lax.axis_index("x"), lax.axis_index("y"), lax.axis_index("c")


def _other_chips(x, y):
    return [(1 - x, y), (x, 1 - y), (1 - x, 1 - y)]


def _all_gather_chips(wpack, *, name):
    rows = wpack.shape[0]
    half = rows // 2

    def body(w_ref, out_ref, send_sems, recv_sems):
        x, y, c = _place()
        sibling = (x, y, 1 - c)
        chips = _other_chips(x, y)

        def blk(cx, cy, cc):
            return out_ref.at[2 * cx + cy, pl.ds(cc * half, half), :]

        def copy(k, src, dst, to):
            return pltpu.make_async_remote_copy(src_ref=src, dst_ref=dst, send_sem=send_sems.at[k],
                                                recv_sem=recv_sems.at[k], device_id=to, device_id_type=MESH)

        first = [copy(j, w_ref.at[pl.ds(c * half, half), :], blk(x, y, c), (cx, cy, c))
                 for j, (cx, cy) in enumerate(chips)]
        for cp in first:
            cp.start()
        passed = [copy(3 + j, blk(cx, cy, c), blk(cx, cy, c), sibling) for j, (cx, cy) in enumerate(chips)]
        for j, (cx, cy) in enumerate(chips):
            copy(j, blk(cx, cy, c), blk(cx, cy, c), (cx, cy, c)).wait_recv()
            passed[j].start()
        for j, (cx, cy) in enumerate(chips):
            copy(3 + j, blk(cx, cy, 1 - c), blk(cx, cy, 1 - c), sibling).wait_recv()
        for cp in first + passed:
            cp.wait_send()

    return pl.pallas_call(
        body, name=name, out_shape=jax.ShapeDtypeStruct((N_CHIPS, rows, D_MODEL), wpack.dtype),
        in_specs=[ANY], out_specs=ANY,
        scratch_shapes=[pltpu.SemaphoreType.DMA((6,)), pltpu.SemaphoreType.DMA((6,))],
    )(wpack)


HBM = pl.BlockSpec(memory_space=pltpu.HBM)
SEM = pl.BlockSpec(memory_space=pltpu.SEMAPHORE)
DATAFLOW = pltpu.SideEffectType.DATAFLOW_SIDE_EFFECTING


def _gather_ici_copies(w_ref, land_ref, send_sems, recv_sems, outgoing):
    x, y, c = _place()
    half = w_ref.shape[0] // 2
    mine = pl.ds(c * half, half)
    return [pltpu.make_async_remote_copy(
        src_ref=w_ref.at[mine, :], dst_ref=land_ref.at[2 * x + y if outgoing else 2 * cx + cy, mine, :],
        send_sem=send_sems.at[j], recv_sem=recv_sems.at[j], device_id=(cx, cy, c), device_id_type=MESH)
        for j, (cx, cy) in enumerate(_other_chips(x, y))]


def _gather_start(wpack, after):
    rows = wpack.shape[0]

    def body(w_ref, land_ref, after_ref, send_sems, recv_sems, w_thru, land_thru, token):
        for cp in _gather_ici_copies(w_ref, land_ref, send_sems, recv_sems, outgoing=True):
            cp.start()
        token[...] = jnp.zeros_like(token)

    land = pltpu.with_memory_space_constraint(lax.empty((N_CHIPS, rows, D_MODEL), wpack.dtype), pltpu.HBM)
    return pl.pallas_call(
        body, name="gather_start",
        out_shape=(pltpu.SemaphoreType.DMA((3,)), pltpu.SemaphoreType.DMA((3,)), pltpu.HBM(wpack.shape, wpack.dtype),
                   pltpu.HBM(land.shape, land.dtype), jax.ShapeDtypeStruct((SUBLANES, LANES), F32)),
        in_specs=(HBM, HBM, ANY), out_specs=(SEM, SEM, HBM, HBM, pl.BlockSpec(memory_space=pltpu.VMEM)),
        input_output_aliases={0: 2, 1: 3},
        compiler_params=pltpu.CompilerParams(has_side_effects=DATAFLOW),
    )(pltpu.with_memory_space_constraint(wpack, pltpu.HBM), land, after)


def _gather_wait(send_sems, recv_sems, w_thru, land_thru, after):
    def body(w_ref, land_ref, send_sems, recv_sems, after_ref, w_dead, land_out):
        for cp in _gather_ici_copies(w_ref, land_ref, send_sems, recv_sems, outgoing=False):
            cp.wait_send()
            cp.wait_recv()

    return pl.pallas_call(
        body, name="gather_wait",
        out_shape=(pltpu.HBM(w_thru.shape, w_thru.dtype), pltpu.HBM(land_thru.shape, land_thru.dtype)),
        in_specs=(HBM, HBM, SEM, SEM, ANY), out_specs=(HBM, HBM), input_output_aliases={0: 0, 1: 1},
        compiler_params=pltpu.CompilerParams(has_side_effects=DATAFLOW),
    )(w_thru, land_thru, send_sems, recv_sems, after)[1]


def _gather_forward(land, *, name):
    rows = land.shape[1]
    half = rows // 2

    def body(in_ref, out_ref, send_sems, recv_sems):
        x, y, c = _place()

        def copy(j, cx, cy, cc):
            rows_of = out_ref.at[2 * cx + cy, pl.ds(cc * half, half), :]
            return pltpu.make_async_remote_copy(src_ref=rows_of, dst_ref=rows_of, send_sem=send_sems.at[j],
                                                recv_sem=recv_sems.at[j], device_id=(x, y, 1 - c), device_id_type=MESH)

        sends = [copy(j, cx, cy, c) for j, (cx, cy) in enumerate(_other_chips(x, y))]
        for cp in sends:
            cp.start()
        for j, (cx, cy) in enumerate(_other_chips(x, y)):
            copy(j, cx, cy, 1 - c).wait_recv()
        for cp in sends:
            cp.wait_send()

    return pl.pallas_call(
        body, name=name, out_shape=jax.ShapeDtypeStruct(land.shape, land.dtype), in_specs=[ANY], out_specs=ANY,
        input_output_aliases={0: 0},
        scratch_shapes=[pltpu.SemaphoreType.DMA((3,)), pltpu.SemaphoreType.DMA((3,))],
    )(land)


def _sibling_swap(buf, *, name):
    def body(in_ref, out_ref, send_sem, recv_sem):
        x, y, c = _place()
        cp = pltpu.make_async_remote_copy(src_ref=in_ref, dst_ref=out_ref, send_sem=send_sem, recv_sem=recv_sem,
                                          device_id=(x, y, 1 - c), device_id_type=MESH)
        cp.start()
        cp.wait()

    return pl.pallas_call(
        body, name=name, out_shape=jax.ShapeDtypeStruct(buf.shape, buf.dtype), in_specs=[ANY], out_specs=ANY,
        scratch_shapes=[pltpu.SemaphoreType.DMA, pltpu.SemaphoreType.DMA],
    )(buf)


def _pair_exchange(gpack, *, name):
    pieces, rows, width = gpack.shape
    half = rows // 2

    def body(in_ref, got_ref, send_sems, recv_sems):
        x, y, c = _place()
        sends = [pltpu.make_async_remote_copy(src_ref=in_ref.at[p, pl.ds((1 - c) * half, half), :],
                                              dst_ref=got_ref.at[p], send_sem=send_sems.at[p],
                                              recv_sem=recv_sems.at[p], device_id=(x, y, 1 - c), device_id_type=MESH)
                 for p in range(pieces)]
        for cp in sends:
            cp.start()
        for cp in sends:
            cp.wait()

    return pl.pallas_call(
        body, name=name, out_shape=jax.ShapeDtypeStruct((pieces, half, width), gpack.dtype),
        in_specs=[ANY], out_specs=ANY,
        scratch_shapes=[pltpu.SemaphoreType.DMA((pieces,)), pltpu.SemaphoreType.DMA((pieces,))],
    )(gpack)


def _pair_add(gpack, got, c, *, name, tm=512):
    pieces, rows, width = gpack.shape
    half = rows // 2
    nb = half // tm

    def body(c_ref, a_ref, b_ref, sum_ref, narrow_ref):
        s = a_ref[...] + b_ref[...]
        sum_ref[...] = s
        narrow_ref[...] = s.astype(BF16)

    blk = pl.BlockSpec((None, tm, width), lambda p, i, c_ref: (p, i, 0))
    return pl.pallas_call(
        body, name=name,
        out_shape=(jax.ShapeDtypeStruct((pieces, half, width), F32), jax.ShapeDtypeStruct((pieces, half, width), BF16)),
        grid_spec=pltpu.PrefetchScalarGridSpec(
            num_scalar_prefetch=1, grid=(pieces, nb),
            in_specs=[pl.BlockSpec((None, tm, width), lambda p, i, c_ref: (p, c_ref[0] * nb + i, 0)), blk],
            out_specs=(blk, blk)),
        compiler_params=_params(("parallel", "parallel")),
    )(c, gpack, got)


def _chip_exchange(pieces, *, name):
    _, rows, width = pieces.shape

    def body(in_ref, out_ref, send_sems, recv_sems):
        x, y, c = _place()
        cps = [pltpu.make_async_remote_copy(src_ref=in_ref.at[2 * cx + cy], dst_ref=out_ref.at[j],
                                            send_sem=send_sems.at[j], recv_sem=recv_sems.at[j],
                                            device_id=(cx, cy, c), device_id_type=MESH)
               for j, (cx, cy) in enumerate(_other_chips(x, y))]
        for cp in cps:
            cp.start()
        for cp in cps:
            cp.wait()

    return pl.pallas_call(
        body, name=name, out_shape=jax.ShapeDtypeStruct((3, rows, width), pieces.dtype), in_specs=[ANY], out_specs=ANY,
        scratch_shapes=[pltpu.SemaphoreType.DMA((3,)), pltpu.SemaphoreType.DMA((3,))],
    )(pieces)


def _chip_exchange_copies(in_ref, land_ref, send_sems, recv_sems):
    x, y, c = _place()
    return [pltpu.make_async_remote_copy(src_ref=in_ref.at[2 * cx + cy], dst_ref=land_ref.at[j],
                                         send_sem=send_sems.at[j], recv_sem=recv_sems.at[j],
                                         device_id=(cx, cy, c), device_id_type=MESH)
            for j, (cx, cy) in enumerate(_other_chips(x, y))]


def _chip_exchange_start(pieces):
    _, rows, width = pieces.shape

    def body(in_ref, land_ref, send_sems, recv_sems, in_thru, land_thru, token):
        for cp in _chip_exchange_copies(in_ref, land_ref, send_sems, recv_sems):
            cp.start()
        token[...] = jnp.zeros_like(token)

    land = pltpu.with_memory_space_constraint(lax.empty((3, rows, width), pieces.dtype), pltpu.HBM)
    return pl.pallas_call(
        body, name="rs_chip_start",
        out_shape=(pltpu.SemaphoreType.DMA((3,)), pltpu.SemaphoreType.DMA((3,)), pltpu.HBM(pieces.shape, pieces.dtype),
                   pltpu.HBM(land.shape, land.dtype), jax.ShapeDtypeStruct((SUBLANES, LANES), F32)),
        in_specs=(HBM, HBM), out_specs=(SEM, SEM, HBM, HBM, pl.BlockSpec(memory_space=pltpu.VMEM)),
        input_output_aliases={0: 2, 1: 3},
        compiler_params=pltpu.CompilerParams(has_side_effects=DATAFLOW),
    )(pltpu.with_memory_space_constraint(pieces, pltpu.HBM), land)


def _chip_exchange_wait(send_sems, recv_sems, in_thru, land_thru, after):
    def body(in_ref, land_ref, send_sems, recv_sems, after_ref, in_dead, land_out):
        for cp in _chip_exchange_copies(in_ref, land_ref, send_sems, recv_sems):
            cp.wait_send()
            cp.wait_recv()

    return pl.pallas_call(
        body, name="rs_chip_wait",
        out_shape=(pltpu.HBM(in_thru.shape, in_thru.dtype), pltpu.HBM(land_thru.shape, land_thru.dtype)),
        in_specs=(HBM, HBM, SEM, SEM, ANY), out_specs=(HBM, HBM), input_output_aliases={0: 0, 1: 1},
        compiler_params=pltpu.CompilerParams(has_side_effects=DATAFLOW),
    )(in_thru, land_thru, send_sems, recv_sems, after)[1]


def _all_reduce_small(v, *, name):
    rows, width = v.shape
    half = rows // 2
    assert half % SUBLANES == 0

    def body(in_ref, out_ref, pair_buf, chip_buf, send_sems, recv_sems):
        x, y, c = _place()
        sibling = (x, y, 1 - c)
        me = 2 * x + y
        mine = pl.ds(pl.multiple_of(c * half, SUBLANES), half)
        other = pl.ds(pl.multiple_of((1 - c) * half, SUBLANES), half)

        def copy(k, src, dst, to):
            return pltpu.make_async_remote_copy(src_ref=src, dst_ref=dst, send_sem=send_sems.at[k],
                                                recv_sem=recv_sems.at[k], device_id=to, device_id_type=MESH)

        swap = copy(0, in_ref.at[other, :], pair_buf, sibling)
        swap.start()
        swap.wait()
        chip_buf[me] = in_ref[mine, :] + pair_buf[...]
        chips = _other_chips(x, y)
        for j, (cx, cy) in enumerate(chips):
            copy(1 + j, chip_buf.at[me], chip_buf.at[me], (cx, cy, c)).start()
        for j, (cx, cy) in enumerate(chips):
            got = copy(1 + j, chip_buf.at[me], chip_buf.at[2 * cx + cy], (cx, cy, c))
            got.wait_send()
            got.wait_recv()
        out_ref[mine, :] = ((chip_buf[0] + chip_buf[1]) + chip_buf[2]) + chip_buf[3]
        share = copy(1 + len(chips), out_ref.at[mine, :], out_ref.at[mine, :], sibling)
        share.start()
        share.wait_send()
        copy(1 + len(chips), out_ref.at[other, :], out_ref.at[other, :], sibling).wait_recv()

    vmem = pl.BlockSpec(memory_space=pltpu.VMEM)
    return pl.pallas_call(
        body, name=name, out_shape=jax.ShapeDtypeStruct(v.shape, v.dtype), in_specs=[vmem], out_specs=vmem,
        scratch_shapes=[pltpu.VMEM((half, width), v.dtype), pltpu.VMEM((N_CHIPS, half, width), v.dtype),
                        pltpu.SemaphoreType.DMA((5,)), pltpu.SemaphoreType.DMA((5,))],
        compiler_params=pltpu.CompilerParams(vmem_limit_bytes=VMEM_LIMIT_V7X),
    )(v)


def _reduce_scatter_begin(gpack, behind):
    x, y, c = _place()
    got = _pair_exchange(gpack, name="rs_pair_swap")
    pair, pair16 = _pair_add(gpack, got, c.astype(jnp.int32).reshape(1), name="rs_pair_add")
    mine = lax.dynamic_index_in_dim(pair, 2 * x + y, axis=0, keepdims=False)
    if behind:
        *in_flight, token = _chip_exchange_start(pair16)
        return dict(mine=mine, in_flight=in_flight), token
    return dict(mine=mine, recv=_chip_exchange(pair16, name="rs_chip_exchange")), None


def _reduce_scatter_end(state, after=None):
    c = lax.axis_index("c")
    recv = state["recv"] if "recv" in state else _chip_exchange_wait(*state["in_flight"], after=after)
    (total,) = _rw_fwd(_f_add4, [state["mine"], recv[0], recv[1], recv[2]], [], tm=512, name="rs_chip_add")
    theirs = _sibling_swap(total, name="rs_share_swap")
    return jnp.concatenate([jnp.where(c == 0, total, theirs), jnp.where(c == 0, theirs, total)], axis=0)


_SHARDED = (("w_kv_mem", 1), ("w_o", 1), ("mlp_w1", 2), ("mlp_w2", 1), ("gdn_w_in", 2), ("gdn_conv_w", 2),
            ("s5_w_in", 2), ("s5_d", 1), ("s5_w_glu", 1), ("s5_b_glu", 1))
_MATMUL_ONLY = ("w_kv_mem", "w_o", "mlp_w1", "mlp_w2", "gdn_w_in", "s5_w_in", "s5_w_glu")
REDUCED_FIRST = (1, 2, 3)
_KEPT_BLOCKED = ("mlp_w1",)
_REPLICATED = ("ln1_g", "ln1_b", "ln2_g", "ln2_b", "gdn_a_log", "gdn_dt_bias", "gdn_norm_g", "s5_a_re", "s5_a_im",
               "s5_b_re", "s5_b_im", "s5_c_re", "s5_c_im", "s5_log_dt")
_WEIGHTS = ("w_kv_mem", "w_o", "ln1_g", "ln1_b", "ln2_g", "ln2_b", "mlp_w1", "mlp_w2", "gdn_w_in", "gdn_conv_w",
            "gdn_a_log", "gdn_dt_bias", "gdn_norm_g", "s5_w_in", "s5_a_re", "s5_a_im", "s5_b_re", "s5_b_im",
            "s5_c_re", "s5_c_im", "s5_log_dt", "s5_d", "s5_w_glu", "s5_b_glu")


ROW_ALIGN = 16


def _n_rows(shape):
    return -(-math.prod(shape) // (ROW_ALIGN * D_MODEL)) * ROW_ALIGN


def _as_rows(a):
    rows = _n_rows(a.shape)
    if a.shape[-1] == D_MODEL and a.size == rows * D_MODEL:
        return a.reshape(-1, D_MODEL)
    flat = a.reshape(-1)
    return jnp.pad(flat, (0, rows * D_MODEL - flat.size)).reshape(rows, D_MODEL)


def _pack(arrs, unit_rows=SHARD_ROWS):
    rows = [_as_rows(a) for a in arrs]
    pad = -sum(r.shape[0] for r in rows) % unit_rows
    if pad:
        rows.append(jnp.zeros((pad, D_MODEL), rows[0].dtype))
    return jnp.concatenate(rows, axis=0)


def _unpack(packed, shapes):
    lead = packed.shape[:-2]
    out, off = [], 0
    for s in shapes:
        r = _n_rows(s)
        seg = lax.slice_in_dim(packed, off, off + r, axis=len(lead))
        if s[-1] != D_MODEL or math.prod(s) != r * D_MODEL:
            seg = lax.slice_in_dim(seg.reshape(lead + (-1,)), 0, math.prod(s), axis=len(lead))
        out.append(seg.reshape(lead + tuple(s)))
        off += r
    return out


def _split3(t):
    hi = t.astype(BF16)
    r1 = t - hi.astype(F32)
    mid = r1.astype(BF16)
    lo = (r1 - mid.astype(F32)).astype(BF16)
    return jnp.stack([hi, mid, lo], axis=-1)


def _join3(t):
    return (t[..., 0].astype(F32) + t[..., 1].astype(F32)) + t[..., 2].astype(F32)


def _merge_chips(blocks, axis):
    return jnp.concatenate([blocks[s] for s in range(N_CHIPS)], axis=axis)


def _pack_for_chips(weights):
    rows = []
    for s in range(N_CHIPS):
        chip = []
        for layers, axis in weights:
            if axis is None:
                blocks = [g[s] for g in layers]
            else:
                n = layers[0].shape[axis] // N_CHIPS
                blocks = [lax.slice_in_dim(g, s * n, (s + 1) * n, axis=axis) for g in layers]
            if math.prod(blocks[0].shape) % (ROW_ALIGN * D_MODEL) == 0:
                chip += [_as_rows(b) for b in blocks]
            else:
                chip.append(_as_rows(jnp.stack(blocks)))
        pad = -sum(r.shape[0] for r in chip) % SHARD_ROWS
        rows += chip + ([jnp.zeros((pad, D_MODEL), F32)] if pad else [])
    return jnp.concatenate(rows, axis=0).reshape(N_CHIPS, -1, D_MODEL)


def kernel(x, mem, w_kv_mem, w_o, ln1_g, ln1_b, ln2_g, ln2_b, mlp_w1, mlp_w2, gdn_w_in, gdn_conv_w, gdn_a_log, gdn_dt_bias, gdn_norm_g, s5_w_in, s5_a_re, s5_a_im, s5_b_re, s5_b_im, s5_c_re, s5_c_im, s5_log_dt, s5_d, s5_w_glu, s5_b_glu, loss_target, m_w_kv_mem, m_w_o, m_ln1_g, m_ln1_b, m_ln2_g, m_ln2_b, m_mlp_w1, m_mlp_w2, m_gdn_w_in, m_gdn_conv_w, m_gdn_a_log, m_gdn_dt_bias, m_gdn_norm_g, m_s5_w_in, m_s5_a_re, m_s5_a_im, m_s5_b_re, m_s5_b_im, m_s5_c_re, m_s5_c_im, m_s5_log_dt, m_s5_d, m_s5_w_glu, m_s5_b_glu, v_w_kv_mem, v_w_o, v_ln1_g, v_ln1_b, v_ln2_g, v_ln2_b, v_mlp_w1, v_mlp_w2, v_gdn_w_in, v_gdn_conv_w, v_gdn_a_log, v_gdn_dt_bias, v_gdn_norm_g, v_s5_w_in, v_s5_a_re, v_s5_a_im, v_s5_b_re, v_s5_b_im, v_s5_c_re, v_s5_c_im, v_s5_log_dt, v_s5_d, v_s5_w_glu, v_s5_b_glu):
    given = dict(locals())
    w = {n: given[n] for n in _WEIGHTS}
    mom = {n: given["m_" + n] for n in _WEIGHTS}
    var = {n: given["v_" + n] for n in _WEIGHTS}
    shard_names = [n for n, _ in _SHARDED]
    shard_shapes = [w[n].shape for n in shard_names]
    rep_shapes = [w[n].shape for n in _REPLICATED]

    wire = {n: w[n].astype(BF16) if n in _MATMUL_ONLY else _split3(w[n]) for n in shard_names}
    first = {n: 0 if n.startswith("s5_") else 1 for n in shard_names}
    me_chip = 2 * lax.axis_index("x") + lax.axis_index("y")
    early = [wire[n][:first[n]] for n in shard_names if first[n]]
    late = [wire[n][first[n]:] for n in shard_names]
    early_pack, late_pack = _pack(early), _pack(late)
    landed = _all_gather_chips(early_pack, name="gather_first_layer")
    landed = lax.dynamic_update_index_in_dim(landed, early_pack, me_chip, axis=0)
    early_blocks = dict(zip([n for n in shard_names if first[n]], _unpack(landed, [a.shape for a in early])))
    send_sems, recv_sems, pack_thru, land_thru, token = _gather_start(late_pack, after=landed)
    axis_of = dict(_SHARDED)

    def merged(n, blk):
        if n in _KEPT_BLOCKED:
            return blk
        return _merge_chips(blk if n in _MATMUL_ONLY else _join3(blk), axis_of[n] - 1)

    late_full = {}

    def weights_of(i, h):
        if i == 0:
            full = {n: [merged(n, blk[:, 0])] for n, blk in early_blocks.items()}
            full["gdn_w_in"][0] = full["gdn_w_in"][0] + token[0, 0].astype(BF16)
        else:
            if not late_full:
                land = _gather_wait(send_sems, recv_sems, pack_thru, land_thru, after=h)
                land = _gather_forward(land, name="gather_forward")
                land = lax.dynamic_update_index_in_dim(land, late_pack, me_chip, axis=0)
                for n, blk in zip(shard_names, _unpack(land, [a.shape for a in late])):
                    late_full[n] = [None] * first[n] + [merged(n, blk[:, t]) for t in range(blk.shape[1])]
            full = dict(late_full)
        full.update({n: w[n] for n in _REPLICATED})
        return _layer_weights(full, i)

    sharded = {}
    in_flight = {}

    def group_pack(layers):
        names = [n for n in shard_names if any(n in sharded[i] for i in layers)]
        per_weight = [[sharded[i][n] for i in layers if n in sharded[i]] for n in names]
        pack = _pack_for_chips([(g, None if n in _KEPT_BLOCKED else axis_of[n] - 1) for n, g in zip(names, per_weight)])
        return pack, names, [(len(g),) + w[n].shape[1:] for n, g in zip(names, per_weight)]

    def grads_ready(i, g):
        sharded[i] = _sharded_grads(g, i)
        if i != REDUCED_FIRST[0]:
            return None
        pack, names, shapes = group_pack(REDUCED_FIRST)
        state, token = _reduce_scatter_begin(pack, behind=True)
        in_flight.update(state=state, names=names, shapes=shapes)
        return token

    loss, grad_x, layer_grads = _local_step(x[0], mem[0], loss_target[0], weights_of, grads_ready)
    loss = lax.psum(loss, ("x", "y", "c"))
    rest = [i for i in range(DEPTH) if i not in REDUCED_FIRST]
    pack, names, shapes = group_pack(rest)
    state, _ = _reduce_scatter_begin(pack, behind=False)
    pieces = {n: [] for n in shard_names}
    for n, g in zip(names, _unpack(_reduce_scatter_end(state), shapes)):
        pieces[n].append(g)
    late = _reduce_scatter_end(in_flight["state"], after=grad_x)
    for n, g in zip(in_flight["names"], _unpack(late, in_flight["shapes"])):
        pieces[n].append(g)
    g_shards = [p[0] if len(p) == 1 else jnp.concatenate(p, axis=0) for p in (pieces[n] for n in shard_names)]

    def pack_small(d):
        return _pack([d[n] for n in _REPLICATED], unit_rows=SMALL_ROWS)

    g_rep = _all_reduce_small(pack_small(_replicated_grads(layer_grads)), name="reduce_replicated")

    def adamw(wp, gp, mp, vp, name):
        return _rw_fwd(_f_adamw, [wp, gp, mp, vp], [], tm=256, name=name)

    outs = {}
    for n, g in zip(shard_names, g_shards):
        flat = (-1, w[n].shape[-1])
        res = adamw(w[n].reshape(flat), g.reshape(flat), mom[n].reshape(flat), var[n].reshape(flat), "adamw_" + n)
        outs[("grad", n)] = g
        outs.update({(kind, n): a.reshape(w[n].shape) for kind, a in zip(("delta", "new_m", "new_v"), res)})
    packed = (g_rep,) + tuple(adamw(pack_small(w), g_rep, pack_small(mom), pack_small(var), "adamw_replicated"))
    for kind, pr in zip(("grad", "delta", "new_m", "new_v"), packed):
        outs.update({(kind, n): a for n, a in zip(_REPLICATED, _unpack(pr, rep_shapes))})
    return (loss, grad_x[None]) + tuple(outs[(kind, n)] for kind in ("grad", "delta", "new_m", "new_v")
                                        for n in _WEIGHTS)
```

```python
import functools
import math

import jax
import jax.numpy as jnp
from jax import lax
from jax.experimental import pallas as pl
from jax.experimental.pallas import tpu as pltpu

F32 = jnp.float32
BF16 = jnp.bfloat16
MESH = pl.DeviceIdType.MESH

D_MODEL = 1024
DEPTH = 4
GDN_HEADS = 8
HEAD_DIM = 128
GDN_CONV = 4
GDN_CHUNK = 64
S5_GROUPS = 64
S5_GROUP = 16
S5_STATE = 64
XA_HEADS = 4
XA_DIM = 512
D_FF = 4096
DN_ALPHA = (2 * DEPTH) ** 0.25
LN_EPS = 1e-5
RMS_EPS = 1e-6
ADAM_LR, ADAM_B1, ADAM_B2, ADAM_EPS, ADAM_WD, ADAM_STEP = 0.001, 0.9, 0.999, 1e-08, 0.01, 10

VMEM_LIMIT_V7X = 56 * 1024 * 1024
LANES = 128
SUBLANES = 8
S5_T = 16
S5_TILES = D_MODEL // LANES
N_CHIPS = 4
N_DEV = 8


def _params(sem):
    return pltpu.CompilerParams(dimension_semantics=sem, vmem_limit_bytes=VMEM_LIMIT_V7X)


def _tile(n, pref):
    if n <= pref:
        return n
    t = (pref // LANES) * LANES
    while n % t:
        t -= LANES
    return t


def _row_tile(n, pref):
    if n % SUBLANES:
        return n
    t = min(pref, n) // SUBLANES * SUBLANES
    while n % t:
        t -= SUBLANES
    return t


def _col_blocked_spec(rows_tile, cols_tile, block_cols, rows_axis, cols_axis):
    r = block_cols // cols_tile

    def index(*ijk):
        c = ijk[cols_axis]
        return (c, ijk[rows_axis], 0) if r == 1 else (c // r, ijk[rows_axis], c % r)

    return pl.BlockSpec((None, rows_tile, cols_tile), index)


def _mm(a, b, *, ta=False, tb=False, acc=None, name, tm=1024, tn=1024, tk=None, out_blocks=0):
    if tk is None:
        tk = 4096 if a.dtype == BF16 and b.dtype == BF16 else 2048
    k_dim, m_dim = a.shape if ta else a.shape[::-1]
    b_rows, b_cols = (b.shape[0], b.shape[1]) if b.ndim == 2 else (b.shape[1], b.shape[0] * b.shape[2])
    n_dim = b_rows if tb else b_cols
    assert (b_cols if tb else b_rows) == k_dim, (a.shape, b.shape, ta, tb)
    limit_n = n_dim // out_blocks if out_blocks else (n_dim if b.ndim == 2 or tb else b.shape[2])
    limit_k = b.shape[2] if (b.ndim == 3 and tb) else k_dim
    tm, tn, tk = _tile(m_dim, tm), _tile(limit_n, min(tn, limit_n)), _tile(limit_k, min(tk, limit_k))
    a_spec = (pl.BlockSpec((tk, tm), lambda i, j, k: (k, i)) if ta else pl.BlockSpec((tm, tk), lambda i, j, k: (i, k)))
    if b.ndim == 3:
        b_spec = (_col_blocked_spec(tn, tk, b.shape[2], 1, 2) if tb else _col_blocked_spec(tk, tn, b.shape[2], 2, 1))
    else:
        b_spec = (pl.BlockSpec((tn, tk), lambda i, j, k: (j, k)) if tb
                  else pl.BlockSpec((tk, tn), lambda i, j, k: (k, j)))
    o_spec = (_col_blocked_spec(tm, tn, n_dim // out_blocks, 0, 1) if out_blocks
              else pl.BlockSpec((tm, tn), lambda i, j, k: (i, j)))
    o_shape = (out_blocks, m_dim, n_dim // out_blocks) if out_blocks else (m_dim, n_dim)
    dn = (((0 if ta else 1,), (1 if tb else 0,)), ((), ()))
    has_acc = acc is not None

    def body(*refs):
        a_ref, b_ref = refs[0], refs[1]
        o_ref = refs[-1]
        k = pl.program_id(2)
        p = lax.dot_general(a_ref[...].astype(BF16), b_ref[...].astype(BF16), dn,
                            preferred_element_type=F32)

        @pl.when(k == 0)
        def _():
            o_ref[...] = p + refs[2][...] if has_acc else p

        @pl.when(k > 0)
        def _():
            o_ref[...] += p

    return pl.pallas_call(
        body, name=name,
        out_shape=jax.ShapeDtypeStruct(o_shape, F32),
        grid=(m_dim // tm, n_dim // tn, k_dim // tk),
        in_specs=[a_spec, b_spec] + ([o_spec] if has_acc else []),
        out_specs=o_spec,
        compiler_params=_params(("parallel", "parallel", "arbitrary")),
    )(*([a, b] + ([acc] if has_acc else [])))


def _mm_relu2(a, b, *, name, tm=1024):
    m_dim, k_dim = a.shape
    n_blocks, _, tn = b.shape
    n_dim = n_blocks * tn
    tm = _tile(m_dim, tm)

    def body(a_ref, b_ref, h_ref, act_ref):
        h = jnp.dot(a_ref[...].astype(BF16), b_ref[...].astype(BF16), preferred_element_type=F32)
        h_ref[...] = h.astype(h_ref.dtype)
        r = jnp.maximum(h, 0.0)
        act_ref[...] = (r * r).astype(BF16)

    o_spec = pl.BlockSpec((tm, tn), lambda i, j: (i, j))
    return pl.pallas_call(
        body, name=name,
        out_shape=(jax.ShapeDtypeStruct((m_dim, n_dim), BF16), jax.ShapeDtypeStruct((m_dim, n_dim), BF16)),
        grid=(m_dim // tm, n_dim // tn),
        in_specs=[pl.BlockSpec((tm, k_dim), lambda i, j: (i, 0)),
                  pl.BlockSpec((None, k_dim, tn), lambda i, j: (j, 0, 0))],
        out_specs=(o_spec, o_spec),
        compiler_params=_params(("parallel", "parallel")),
    )(a, b)


def _mm_relu2_grad(d, b, h, *, name, tm=1024, tn=1024):
    m_dim, k_dim = d.shape
    n_dim = b.shape[0]
    tm, tn = _tile(m_dim, tm), _tile(n_dim, tn)

    def body(d_ref, b_ref, h_ref, o_ref):
        p = lax.dot_general(d_ref[...].astype(BF16), b_ref[...].astype(BF16), ((NT), ((), ())),
                            preferred_element_type=F32)
        o_ref[...] = (p * (2.0 * jnp.maximum(h_ref[...].astype(F32), 0.0))).astype(BF16)

    o_spec = pl.BlockSpec((tm, tn), lambda i, j: (i, j))
    return pl.pallas_call(
        body, name=name,
        out_shape=jax.ShapeDtypeStruct((m_dim, n_dim), BF16),
        grid=(m_dim // tm, n_dim // tn),
        in_specs=[pl.BlockSpec((tm, k_dim), lambda i, j: (i, 0)), pl.BlockSpec((tn, k_dim), lambda i, j: (j, 0)), o_spec],
        out_specs=o_spec,
        compiler_params=_params(("parallel", "parallel")),
    )(d, b, h)


def _rowwise(f, rows, params, row_out, acc_out, *, tm, name):
    length = rows[0].shape[0]
    tm = _row_tile(length, tm)
    nr, npar, nro = len(rows), len(params), len(row_out)

    def body(*refs):
        ins = [r[...] for r in refs[:nr + npar]]
        outs = refs[nr + npar:]
        r_o, a_o = f(*ins)
        for ref, val in zip(outs[:nro], r_o):
            ref[...] = val.astype(ref.dtype)
        i = pl.program_id(0)
        for ref, val in zip(outs[nro:], a_o):
            @pl.when(i == 0)
            def _(ref=ref, val=val):
                ref[...] = val.astype(ref.dtype)

            @pl.when(i > 0)
            def _(ref=ref, val=val):
                ref[...] += val.astype(ref.dtype)

    in_specs = ([pl.BlockSpec((tm, r.shape[1]), lambda i: (i, 0)) for r in rows]
                + [pl.BlockSpec(p.shape, lambda i: (0, 0)) for p in params])
    out_specs = ([pl.BlockSpec((tm, w), lambda i: (i, 0)) for w, _ in row_out]
                 + [pl.BlockSpec(s, lambda i: (0, 0)) for s, _ in acc_out])
    out_shape = ([jax.ShapeDtypeStruct((length, w), dt) for w, dt in row_out]
                 + [jax.ShapeDtypeStruct(s, dt) for s, dt in acc_out])
    res = pl.pallas_call(
        body, name=name, out_shape=out_shape, grid=(length // tm,),
        in_specs=in_specs, out_specs=out_specs,
        compiler_params=_params(("arbitrary",) if acc_out else ("parallel",)),
    )(*rows, *params)
    return res[:nro], res[nro:]


def _rw_fwd(f, rows, params, *, tm, name, out_dtypes=None):
    tm_ = _row_tile(rows[0].shape[0], tm)
    shapes = jax.eval_shape(f, *[jax.ShapeDtypeStruct((tm_, r.shape[1]), r.dtype) for r in rows],
                            *[jax.ShapeDtypeStruct(p.shape, p.dtype) for p in params])
    row_out = [(s.shape[1], s.dtype if out_dtypes is None else dt)
               for s, dt in zip(shapes, out_dtypes or shapes)]
    outs, _ = _rowwise(lambda *v: (f(*v), ()), rows, params, row_out, [], tm=tm, name=name)
    return outs


def _rw_bwd(f, rows, params, cots, *, row_grad, param_grad, tm, name, row_dtypes=None):
    nr, npar, nct = len(rows), len(params), len(cots)

    def g(*vals):
        prim = vals[:nr] + vals[nr + nct:]
        ct = vals[nr:nr + nct]
        _, vjp = jax.vjp(f, *prim)
        grads = vjp(tuple(ct))
        return (tuple(grads[i] for i in range(nr) if row_grad[i]),
                tuple(grads[nr + i] for i in range(npar) if param_grad[i]))

    widths = [rows[i].shape[1] for i in range(nr) if row_grad[i]]
    row_out = list(zip(widths, row_dtypes or [F32] * len(widths)))
    acc_out = [(params[i].shape, F32) for i in range(npar) if param_grad[i]]
    return _rowwise(g, list(rows) + list(cots), params, row_out, acc_out, tm=tm, name=name)


def _f_ln_res(x, h, g, b):
    pre = DN_ALPHA * x + h
    mu = jnp.mean(pre, axis=-1, keepdims=True)
    d = pre - mu
    var = jnp.mean(d * d, axis=-1, keepdims=True)
    return (d * lax.rsqrt(var + LN_EPS) * g + b,)


def _silu(t):
    return t * jax.nn.sigmoid(t)


def _f_gdn_qkv(c):
    a = _silu(c)
    outs = []
    for part, scale in ((0, HEAD_DIM ** -0.5), (1, 1.0)):
        heads = []
        for h in range(GDN_HEADS):
            t = a[:, part * D_MODEL + h * HEAD_DIM: part * D_MODEL + (h + 1) * HEAD_DIM]
            t = t * lax.rsqrt(jnp.sum(t * t, axis=-1, keepdims=True) + 1e-6)
            heads.append(t * scale if scale != 1.0 else t)
        outs.append(jnp.concatenate(heads, axis=-1))
    outs.append(a[:, 2 * D_MODEL:])
    return tuple(outs)


def _f_gdn_out(o, z, norm_g):
    heads = []
    for h in range(GDN_HEADS):
        t = o[:, h * HEAD_DIM:(h + 1) * HEAD_DIM]
        t = t * lax.rsqrt(jnp.mean(t * t, axis=-1, keepdims=True) + RMS_EPS) * norm_g
        heads.append(t)
    return (jnp.concatenate(heads, axis=-1) * _silu(z),)


def _f_attn(xq, kmem, vmem):
    heads = []
    for h in range(XA_HEADS):
        sl = slice(h * HEAD_DIM, (h + 1) * HEAD_DIM)
        s = lax.dot_general(xq[:, sl].astype(BF16), kmem[:, sl].astype(BF16),
                            (((1,), (1,)), ((), ())), preferred_element_type=F32) * (HEAD_DIM ** -0.5)
        m = lax.stop_gradient(jnp.max(s, axis=-1, keepdims=True))
        e = jnp.exp(s - m)
        p = e / jnp.sum(e, axis=-1, keepdims=True)
        heads.append(jnp.dot(p.astype(BF16), vmem[:, sl].astype(BF16), preferred_element_type=F32))
    return (jnp.concatenate(heads, axis=-1),)


def _f_s5_gelu(y, u, d):
    return (jax.nn.gelu(y + d * u),)


def _f_s5_gate(zg, t, b):
    return (zg * jax.nn.sigmoid(t + b),)


def _f_add(a, b):
    return (a + b,)


def _f_add4(a, b, c, d):
    return (((a + b.astype(F32)) + c.astype(F32)) + d.astype(F32),)


def _f_adamw(w, g, m, v):
    m = ADAM_B1 * m + (1.0 - ADAM_B1) * g
    v = ADAM_B2 * v + (1.0 - ADAM_B2) * jnp.square(g)
    m_hat = m / (1.0 - ADAM_B1 ** ADAM_STEP)
    v_hat = v / (1.0 - ADAM_B2 ** ADAM_STEP)
    delta = -ADAM_LR * (m_hat / (jnp.sqrt(v_hat) + ADAM_EPS) + ADAM_WD * w)
    return delta, m, v


def _conv_fwd(u, w, *, tm, name):
    length, chans = u.shape
    tm = min(tm, length)
    tc = _tile(chans, 1024)
    hb = tm // SUBLANES

    def body(cur_ref, prev_ref, w_ref, o_ref, buf):
        i = pl.program_id(1)
        buf[0:SUBLANES, :] = jnp.where(i > 0, prev_ref[...], 0.0)
        buf[SUBLANES:, :] = cur_ref[...]
        acc = buf[pl.ds(SUBLANES - 3, tm), :] * w_ref[0:1, :]
        for k in range(1, GDN_CONV):
            acc = acc + buf[pl.ds(SUBLANES - 3 + k, tm), :] * w_ref[k:k + 1, :]
        o_ref[...] = acc

    return pl.pallas_call(
        body, name=name, out_shape=jax.ShapeDtypeStruct(u.shape, F32),
        grid=(chans // tc, length // tm),
        in_specs=[pl.BlockSpec((tm, tc), lambda j, i: (i, j)),
                  pl.BlockSpec((SUBLANES, tc), lambda j, i: (jnp.maximum(i * hb - 1, 0), j)),
                  pl.BlockSpec((GDN_CONV, tc), lambda j, i: (0, j))],
        out_specs=pl.BlockSpec((tm, tc), lambda j, i: (i, j)),
        scratch_shapes=[pltpu.VMEM((tm + SUBLANES, tc), F32)],
        compiler_params=_params(("parallel", "parallel")),
    )(u, u, w)


def _conv_bwd(u, w, dc, *, tm, name):
    length, chans = u.shape
    tm = min(tm, length)
    tc = _tile(chans, 1024)
    hb = tm // SUBLANES
    last = length // tm - 1

    def body(u_ref, uprev_ref, dc_ref, dcnext_ref, w_ref, du_ref, dw_ref, ubuf, dbuf):
        i = pl.program_id(1)
        ubuf[0:SUBLANES, :] = jnp.where(i > 0, uprev_ref[...], 0.0)
        ubuf[SUBLANES:, :] = u_ref[...]
        dbuf[0:tm, :] = dc_ref[...]
        dbuf[tm:, :] = jnp.where(i < last, dcnext_ref[...], 0.0)
        dcv = dc_ref[...]
        du = dbuf[pl.ds(3, tm), :] * w_ref[0:1, :]
        rows = [jnp.sum(dcv * ubuf[pl.ds(SUBLANES - 3, tm), :], axis=0, keepdims=True)]
        for k in range(1, GDN_CONV):
            du = du + dbuf[pl.ds(3 - k, tm), :] * w_ref[k:k + 1, :]
            rows.append(jnp.sum(dcv * ubuf[pl.ds(SUBLANES - 3 + k, tm), :], axis=0, keepdims=True))
        du_ref[...] = du.astype(du_ref.dtype)
        dwv = jnp.concatenate(rows, axis=0)

        @pl.when(i == 0)
        def _():
            dw_ref[...] = dwv

        @pl.when(i > 0)
        def _():
            dw_ref[...] += dwv

    return pl.pallas_call(
        body, name=name,
        out_shape=(jax.ShapeDtypeStruct(u.shape, BF16), jax.ShapeDtypeStruct((GDN_CONV, chans), F32)),
        grid=(chans // tc, length // tm),
        in_specs=[pl.BlockSpec((tm, tc), lambda j, i: (i, j)),
                  pl.BlockSpec((SUBLANES, tc), lambda j, i: (jnp.maximum(i * hb - 1, 0), j)),
                  pl.BlockSpec((tm, tc), lambda j, i: (i, j)),
                  pl.BlockSpec((SUBLANES, tc), lambda j, i: (jnp.minimum((i + 1) * hb, (last + 1) * hb - 1), j)),
                  pl.BlockSpec((GDN_CONV, tc), lambda j, i: (0, j))],
        out_specs=(pl.BlockSpec((tm, tc), lambda j, i: (i, j)),
                   pl.BlockSpec((GDN_CONV, tc), lambda j, i: (0, j))),
        scratch_shapes=[pltpu.VMEM((tm + SUBLANES, tc), F32), pltpu.VMEM((tm + SUBLANES, tc), F32)],
        compiler_params=_params(("parallel", "arbitrary")),
    )(u, u, dc, dc, w)


def _dot(a, b, dims, precision=None):
    if precision is None:
        a, b = a.astype(BF16), b.astype(BF16)
    return lax.dot_general(a, b, (dims, ((), ())), preferred_element_type=F32, precision=precision)


def _dot3(a, b, dims):
    ah, bh = a.astype(BF16), b.astype(BF16)
    al, bl = (a - ah.astype(F32)).astype(BF16), (b - bh.astype(F32)).astype(BF16)

    def d(x, y):
        return lax.dot_general(x, y, (dims, ((), ())), preferred_element_type=F32)

    return d(ah, bh) + (d(ah, bl) + d(al, bh))


NN = ((1,), (0,))
NT = ((1,), (1,))
TN = ((0,), (0,))
HI = lax.Precision.HIGHEST


def _hmap(f, *lists):
    return [f(*t) for t in zip(*lists)]


@jax.custom_vjp
def _unit_lower_inverse(a):
    c = a[0].shape[0]
    eye = (lax.broadcasted_iota(jnp.int32, (c, c), 0) == lax.broadcasted_iota(jnp.int32, (c, c), 1)).astype(F32)
    p = _hmap(lambda x: -x, a)
    t = _hmap(lambda x: eye + x, p)
    for _ in range(int(math.log2(c)) - 1):
        p = _hmap(lambda x: _dot3(x, x, NN), p)
        t = _hmap(lambda x, y: x + _dot3(x, y, NN), t, p)
    return t


def _uli_fwd(a):
    t = _unit_lower_inverse(a)
    return t, t


def _uli_bwd(t, dt):
    left = _hmap(lambda x, y: _dot3(x, y, TN), t, dt)
    return (_hmap(lambda x, y: -_dot3(x, y, NT), left, t),)


_unit_lower_inverse.defvjp(_uli_fwd, _uli_bwd)


@jax.custom_vjp
def _known_inverse(a, t):
    return t


_known_inverse.defvjp(lambda a, t: (t, t),
                      lambda t, dt: (_uli_bwd(t, dt)[0], _hmap(jnp.zeros_like, t)))


def _gdn_chunk(q, k, v, bl, al, a_log, dt_bias, state, t_known=None):
    c = q[0].shape[0]
    row = lax.broadcasted_iota(jnp.int32, (c, c), 0)
    col = lax.broadcasted_iota(jnp.int32, (c, c), 1)
    causal = row >= col
    strict = row > col
    eye = (row == col).astype(F32)
    beta = _hmap(jax.nn.sigmoid, bl)
    g = _hmap(lambda a_, l_, d_: -jnp.exp(a_) * jax.nn.softplus(l_ + d_), a_log, al, dt_bias)
    g_r = _hmap(lambda x: jnp.sum(eye * x, axis=0, keepdims=True), g)
    gc = _hmap(lambda x: jnp.sum(jnp.where(causal, x, 0.0), axis=1, keepdims=True), g_r)
    gc_r = _hmap(lambda x: jnp.sum(jnp.where(row <= col, x, 0.0), axis=0, keepdims=True), g)
    decay = _hmap(lambda x, y: jnp.where(causal, jnp.exp(jnp.where(causal, x - y, 0.0)), 0.0), gc, gc_r)
    e_gc = _hmap(jnp.exp, gc)
    kb = _hmap(jnp.multiply, k, beta)
    vb = _hmap(jnp.multiply, v, beta)
    a_mat = _hmap(lambda x, y, d: jnp.where(strict, _dot(x, y, NT) * d, 0.0), kb, k, decay)
    t_inv = _unit_lower_inverse(a_mat) if t_known is None else _known_inverse(a_mat, t_known)
    u_blk = _hmap(lambda t, x: _dot(t, x, NN), t_inv, vb)
    w_blk = _hmap(lambda t, x, e: _dot(t, x * e, NN), t_inv, kb, e_gc)
    v_new = _hmap(lambda u, w, s: u - _dot(w, s, NN), u_blk, w_blk, state)
    attn = _hmap(lambda x, y, d: _dot(x, y, NT) * d, q, k, decay)
    o_state = _hmap(lambda x, e, s: _dot(x * e, s, NN), q, e_gc, state)
    o = _hmap(lambda base, at, vn: base + _dot(at, vn, NN), o_state, attn, v_new)
    g_last = _hmap(lambda x: jnp.sum(x, axis=0, keepdims=True), g)
    k_dec = _hmap(lambda x, gl, c_: x * jnp.exp(gl - c_), k, g_last, gc)
    new_state = _hmap(lambda s, gl, kd, vn: s * jnp.exp(gl) + _dot(kd, vn, TN), state, g_last, k_dec, v_new)
    return (o, new_state, t_inv) if t_known is None else (o, new_state)


def _gdn_operands(q_ref, k_ref, v_ref, bav, alog_ref, dtb_ref):
    hs = range(GDN_HEADS)
    cols = [slice(h * HEAD_DIM, (h + 1) * HEAD_DIM) for h in hs]
    return ([q_ref[:, sl] for sl in cols], [k_ref[:, sl] for sl in cols], [v_ref[:, sl] for sl in cols],
            [bav[:, h:h + 1] for h in hs], [bav[:, h + GDN_HEADS:h + GDN_HEADS + 1] for h in hs],
            [alog_ref[h:h + 1, 0:1] for h in hs], [dtb_ref[h:h + 1, 0:1] for h in hs])


def _gdn_scan_fwd(q, k, v, ba, a_log, dt_bias, *, name):
    length = q.shape[0]
    n = length // GDN_CHUNK
    c = GDN_CHUNK

    def body(q_ref, k_ref, v_ref, ba_ref, alog_ref, dtb_ref, o_ref, s_ref, t_ref, state):
        i = pl.program_id(0)

        @pl.when(i == 0)
        def _():
            state[...] = jnp.zeros_like(state)

        bav = ba_ref[...]
        heads = [slice(h * HEAD_DIM, (h + 1) * HEAD_DIM) for h in range(GDN_HEADS)]
        s_in = [state[h] for h in range(GDN_HEADS)]
        o, s_out, t_inv = _gdn_chunk(*_gdn_operands(q_ref, k_ref, v_ref, bav, alog_ref, dtb_ref), s_in)
        for h, sl in enumerate(heads):
            s_ref[h] = s_in[h]
            t_ref[h] = t_inv[h]
            o_ref[:, sl] = o[h]
            state[h] = s_out[h]

    row_spec = pl.BlockSpec((c, D_MODEL), lambda i: (i, 0))
    small = pl.BlockSpec((GDN_HEADS, LANES), lambda i: (0, 0))
    return pl.pallas_call(
        body, name=name,
        out_shape=(jax.ShapeDtypeStruct((length, D_MODEL), F32),
                   jax.ShapeDtypeStruct((n, GDN_HEADS, HEAD_DIM, HEAD_DIM), F32),
                   jax.ShapeDtypeStruct((n, GDN_HEADS, c, c), F32)),
        grid=(n,),
        in_specs=[row_spec, row_spec, row_spec, pl.BlockSpec((c, LANES), lambda i: (i, 0)), small, small],
        out_specs=(row_spec, pl.BlockSpec((None, GDN_HEADS, HEAD_DIM, HEAD_DIM), lambda i: (i, 0, 0, 0)),
                   pl.BlockSpec((None, GDN_HEADS, c, c), lambda i: (i, 0, 0, 0))),
        scratch_shapes=[pltpu.VMEM((GDN_HEADS, HEAD_DIM, HEAD_DIM), F32)],
        compiler_params=_params(("arbitrary",)),
    )(q, k, v, ba, a_log, dt_bias)


def _gdn_scan_bwd(q, k, v, ba, a_log, dt_bias, states, inverses, do, *, name):
    length = q.shape[0]
    n = length // GDN_CHUNK
    c = GDN_CHUNK

    def body(q_ref, k_ref, v_ref, ba_ref, alog_ref, dtb_ref, s_ref, t_ref, do_ref,
             dq_ref, dk_ref, dv_ref, dba_ref, dalog_ref, ddtb_ref, dstate):
        i = pl.program_id(0)

        @pl.when(i == 0)
        def _():
            dstate[...] = jnp.zeros_like(dstate)
            dalog_ref[...] = jnp.zeros_like(dalog_ref)
            ddtb_ref[...] = jnp.zeros_like(ddtb_ref)

        bav = ba_ref[...]
        lane = lax.broadcasted_iota(jnp.int32, (c, LANES), 1)
        sub8 = lax.broadcasted_iota(jnp.int32, (GDN_HEADS, LANES), 0)
        lane8 = lax.broadcasted_iota(jnp.int32, (GDN_HEADS, LANES), 1)
        slab = jnp.zeros((c, LANES), F32)
        dalog_all = jnp.zeros((GDN_HEADS, LANES), F32)
        ddtb_all = jnp.zeros((GDN_HEADS, LANES), F32)
        heads = [slice(h * HEAD_DIM, (h + 1) * HEAD_DIM) for h in range(GDN_HEADS)]
        ds_in = [dstate[h] for h in range(GDN_HEADS)]
        s_in = [s_ref[h] for h in range(GDN_HEADS)]
        t_known = [t_ref[h] for h in range(GDN_HEADS)]
        _, vjp = jax.vjp(functools.partial(_gdn_chunk, t_known=t_known),
                         *_gdn_operands(q_ref, k_ref, v_ref, bav, alog_ref, dtb_ref), s_in)
        dq, dk, dv, dbl, dal, dalog, ddtb, ds = vjp(([do_ref[:, sl] for sl in heads], ds_in))
        for h, sl in enumerate(heads):
            dq_ref[:, sl] = dq[h]
            dk_ref[:, sl] = dk[h]
            dv_ref[:, sl] = dv[h]
            dstate[h] = ds[h]
            slab = slab + jnp.where(lane == h, dbl[h], 0.0) + jnp.where(lane == h + GDN_HEADS, dal[h], 0.0)
            here = (sub8 == h) & (lane8 == 0)
            dalog_all = dalog_all + jnp.where(here, dalog[h], 0.0)
            ddtb_all = ddtb_all + jnp.where(here, ddtb[h], 0.0)
        dba_ref[...] = slab
        dalog_ref[...] += dalog_all
        ddtb_ref[...] += ddtb_all

    row_spec = pl.BlockSpec((c, D_MODEL), lambda i: (n - 1 - i, 0))
    small = pl.BlockSpec((GDN_HEADS, LANES), lambda i: (0, 0))
    return pl.pallas_call(
        body, name=name,
        out_shape=(jax.ShapeDtypeStruct((length, D_MODEL), F32),) * 3
        + (jax.ShapeDtypeStruct((length, LANES), F32),
           jax.ShapeDtypeStruct((GDN_HEADS, LANES), F32), jax.ShapeDtypeStruct((GDN_HEADS, LANES), F32)),
        grid=(n,),
        in_specs=[row_spec, row_spec, row_spec,
                  pl.BlockSpec((c, LANES), lambda i: (n - 1 - i, 0)), small, small,
                  pl.BlockSpec((None, GDN_HEADS, HEAD_DIM, HEAD_DIM), lambda i: (n - 1 - i, 0, 0, 0)),
                  pl.BlockSpec((None, GDN_HEADS, c, c), lambda i: (n - 1 - i, 0, 0, 0)),
                  row_spec],
        out_specs=(row_spec, row_spec, row_spec,
                   pl.BlockSpec((c, LANES), lambda i: (n - 1 - i, 0)), small, small),
        scratch_shapes=[pltpu.VMEM((GDN_HEADS, HEAD_DIM, HEAD_DIM), F32)],
        compiler_params=_params(("arbitrary",)),
    )(q, k, v, ba, a_log, dt_bias, states, inverses, do)


S5_W = S5_T * LANES
S5_S = 2 * 8 * S5_STATE
S5_SH = S5_S // 2


def _iota2(shape):
    return lax.broadcasted_iota(jnp.int32, shape, 0), lax.broadcasted_iota(jnp.int32, shape, 1)


def _s5_rep_t(t, dtype):
    row, col = _iota2((S5_T * S5_GROUP, LANES))
    return ((jnp.right_shift(row, 4) == t) & (jnp.bitwise_and(row, 15) == jnp.bitwise_and(col, 15))).astype(dtype)


def _s5_rep_state(dtype):
    row, col = _iota2((2 * S5_STATE, S5_S))
    return ((jnp.right_shift(row, 6) == jnp.right_shift(col, 9))
            & (jnp.bitwise_and(row, 63) == jnp.bitwise_and(col, 63))).astype(dtype)


def _s5_masks():
    row, col = _iota2((LANES, LANES))
    m_ab = jnp.right_shift(row, 4) == jnp.right_shift(col, 4)
    row, col = _iota2((S5_S, LANES))
    m_e = jnp.bitwise_and(jnp.right_shift(row, 6), 7) == jnp.right_shift(col, 4)
    row, col = _iota2((LANES, S5_S))
    m_f = jnp.right_shift(row, 4) == jnp.bitwise_and(jnp.right_shift(col, 6), 7)
    return m_ab, m_e, m_f


def _s5_expand(kx_ref, ec_ref, fc_ref, kb_scr, e_scr, f_scr):
    m_ab, m_e, m_f = _s5_masks()
    kx = kx_ref[...].astype(BF16)
    ec = ec_ref[...].astype(BF16)
    rep_state = _s5_rep_state(BF16)
    for t in range(S5_T):
        rep = _s5_rep_t(t, BF16)
        cols = slice(t * LANES, (t + 1) * LANES)
        kb_scr[t] = jnp.where(m_ab, jnp.dot(kx, rep, preferred_element_type=F32), 0.0).astype(BF16)
        e_scr[:, cols] = jnp.where(m_e, jnp.dot(ec, rep, preferred_element_type=F32), 0.0).astype(BF16)
        f_scr[cols, :] = jnp.where(m_f, jnp.dot(fc_ref[t].astype(BF16), rep_state, preferred_element_type=F32),
                                   0.0).astype(BF16)


def _s5_token_rows(ref, n):
    return [ref[pl.ds(t, n, stride=S5_T), :].astype(BF16) for t in range(S5_T)]


def _s5_scan_fwd(u, kx, ec, fc, at, *, name):
    length = u.shape[0]
    n = length // S5_T
    assert n % SUBLANES == 0

    def body(u_ref, kx_ref, ec_ref, fc_ref, at_ref, y_ref, h_ref, kb_scr, e_scr, f_scr, g_scr):
        _s5_expand(kx_ref, ec_ref, fc_ref, kb_scr, e_scr, f_scr)
        us = _s5_token_rows(u_ref, n)
        g_scr[...] = jnp.dot(jnp.concatenate(us, axis=1), f_scr[...], preferred_element_type=F32)
        ar, ai = at_ref[:, :S5_SH], at_ref[:, S5_SH:]

        def step(blk, h):
            base = pl.multiple_of(blk * SUBLANES, SUBLANES)
            g8 = g_scr[pl.ds(base, SUBLANES), :]
            rows = []
            for r in range(SUBLANES):
                rows.append(h)
                hr, hi = h[:, :S5_SH], h[:, S5_SH:]
                h = jnp.concatenate([ar * hr - ai * hi, ar * hi + ai * hr], axis=1) + g8[r:r + 1, :]
            h_ref[pl.ds(base, SUBLANES), :] = jnp.concatenate(rows, axis=0)
            return h

        lax.fori_loop(0, n // SUBLANES, step, jnp.zeros((1, S5_S), F32))
        hb = h_ref[...].astype(BF16)
        for t in range(S5_T):
            acc = jnp.dot(hb, e_scr[:, t * LANES:(t + 1) * LANES], preferred_element_type=F32)
            for s in range(t + 1):
                acc = acc + jnp.dot(us[s], kb_scr[t - s], preferred_element_type=F32)
            y_ref[pl.ds(t, n, stride=S5_T), :] = acc

    return pl.pallas_call(
        body, name=name,
        out_shape=(jax.ShapeDtypeStruct((length, D_MODEL), F32), jax.ShapeDtypeStruct((S5_TILES, n, S5_S), F32)),
        grid=(S5_TILES,),
        in_specs=[pl.BlockSpec((length, LANES), lambda k: (0, k)), _s5_spec(LANES, S5_T * S5_GROUP),
                  _s5_spec(S5_S, S5_T * S5_GROUP), _s5_spec(S5_T, LANES, LANES), _s5_spec(1, S5_S)],
        out_specs=(pl.BlockSpec((length, LANES), lambda k: (0, k)), _s5_spec(n, S5_S)),
        scratch_shapes=[pltpu.VMEM((S5_T, LANES, LANES), BF16), pltpu.VMEM((S5_S, S5_W), BF16),
                        pltpu.VMEM((S5_W, S5_S), BF16), pltpu.VMEM((n, S5_S), F32)],
        compiler_params=_params(("parallel",)),
    )(u, kx, ec, fc, at)


def _s5_spec(*tail):
    return pl.BlockSpec((None,) + tail, lambda k: (k,) + (0,) * len(tail))


def _s5_scan_bwd(dy, kx, ec, fc, at, hs, *, name):
    length = dy.shape[0]
    n = length // S5_T

    def body(dy_ref, kx_ref, ec_ref, fc_ref, at_ref, h_ref, du_ref, dg_ref, dat_ref, kb_scr, e_scr, f_scr, dh_scr):
        _s5_expand(kx_ref, ec_ref, fc_ref, kb_scr, e_scr, f_scr)
        dys = _s5_token_rows(dy_ref, n)
        dh_scr[...] = _dot(jnp.concatenate(dys, axis=1), e_scr[...], NT)
        ar, ai = at_ref[:, :S5_SH], at_ref[:, S5_SH:]

        def step(it, carry):
            cy, dat = carry
            base = pl.multiple_of((n // SUBLANES - 1 - it) * SUBLANES, SUBLANES)
            dh8 = dh_scr[pl.ds(base, SUBLANES), :]
            h8 = h_ref[pl.ds(base, SUBLANES), :]
            rows = [None] * SUBLANES
            for r in reversed(range(SUBLANES)):
                rows[r] = cy
                cr, ci = cy[:, :S5_SH], cy[:, S5_SH:]
                hr, hi = h8[r:r + 1, :S5_SH], h8[r:r + 1, S5_SH:]
                dat = dat + jnp.concatenate([cr * hr + ci * hi, ci * hr - cr * hi], axis=1)
                cy = dh8[r:r + 1, :] + jnp.concatenate([ar * cr + ai * ci, ar * ci - ai * cr], axis=1)
            dg_ref[pl.ds(base, SUBLANES), :] = jnp.concatenate(rows, axis=0)
            return cy, dat

        zero = jnp.zeros((1, S5_S), F32)
        _, dat = lax.fori_loop(0, n // SUBLANES, step, (zero, zero))
        dat_ref[...] = dat
        dgb = dg_ref[...].astype(BF16)
        for s in range(S5_T):
            acc = _dot(dgb, f_scr[s * LANES:(s + 1) * LANES, :], NT)
            for t in range(s, S5_T):
                acc = acc + _dot(dys[t], kb_scr[t - s], NT)
            du_ref[pl.ds(s, n, stride=S5_T), :] = acc

    row_spec = pl.BlockSpec((length, LANES), lambda k: (0, k))
    return pl.pallas_call(
        body, name=name,
        out_shape=(jax.ShapeDtypeStruct((length, D_MODEL), F32), jax.ShapeDtypeStruct((S5_TILES, n, S5_S), F32),
                   jax.ShapeDtypeStruct((S5_TILES, 1, S5_S), F32)),
        grid=(S5_TILES,),
        in_specs=[row_spec, _s5_spec(LANES, S5_T * S5_GROUP), _s5_spec(S5_S, S5_T * S5_GROUP),
                  _s5_spec(S5_T, LANES, LANES), _s5_spec(1, S5_S), _s5_spec(n, S5_S)],
        out_specs=(row_spec, _s5_spec(n, S5_S), _s5_spec(1, S5_S)),
        scratch_shapes=[pltpu.VMEM((S5_T, LANES, LANES), BF16), pltpu.VMEM((S5_S, S5_W), BF16),
                        pltpu.VMEM((S5_W, S5_S), BF16), pltpu.VMEM((n, S5_S), F32)],
        compiler_params=_params(("parallel",)),
    )(dy, kx, ec, fc, at, hs)


def _s5_operator_grads(dy, u, hs, dg, *, name):
    length = u.shape[0]
    n = length // S5_T

    def body(dy_ref, u_ref, h_ref, dg_ref, dkx_ref, dec_ref, dfc_ref):
        dys = _s5_token_rows(dy_ref, n)
        us = _s5_token_rows(u_ref, n)
        ucat = jnp.concatenate(us, axis=1)
        m_ab, m_e, m_f = _s5_masks()
        hb = h_ref[...].astype(BF16)
        dgb = dg_ref[...].astype(BF16)
        lane = lax.broadcasted_iota(jnp.int32, (1, LANES), 1)
        lane_group = jnp.right_shift(lane, 4)

        def own_block(x, mask):
            x = jnp.where(mask, x, 0.0)
            for shift in (64, 32, 16):
                x = x + pltpu.roll(x, shift, 1)
            return x

        def place(halves, t, x):
            halves[t // 8] = jnp.where(lane_group == t % 8, x, halves[t // 8])

        dkb = [jnp.zeros((LANES, LANES), F32) for _ in range(S5_T)]
        dec = [jnp.zeros((S5_S, LANES), F32) for _ in range(2)]
        for t in range(S5_T):
            d_t = _dot(ucat, dys[t], TN)
            for s in range(t + 1):
                dkb[t - s] = dkb[t - s] + d_t[s * LANES:(s + 1) * LANES, :]
            place(dec, t, own_block(_dot(hb, dys[t], TN), m_e))
            wide = jnp.where(m_f, _dot(us[t], dgb, TN), 0.0)
            parts = []
            for r in range(2):
                acc = wide[:, r * S5_SH:r * S5_SH + LANES]
                for q in range(1, S5_SH // LANES):
                    acc = acc + wide[:, r * S5_SH + q * LANES:r * S5_SH + (q + 1) * LANES]
                parts.append(acc + pltpu.roll(acc, S5_STATE, 1))
            dfc_ref[t] = jnp.where(lane < S5_STATE, parts[0], parts[1])
        dkx = [jnp.zeros((LANES, LANES), F32) for _ in range(2)]
        for t in range(S5_T):
            place(dkx, t, own_block(dkb[t], m_ab))
        dkx_ref[...] = jnp.concatenate(dkx, axis=1)
        dec_ref[...] = jnp.concatenate(dec, axis=1)

    row_spec = pl.BlockSpec((length, LANES), lambda k: (0, k))
    outs = (_s5_spec(LANES, S5_T * S5_GROUP), _s5_spec(S5_S, S5_T * S5_GROUP), _s5_spec(S5_T, LANES, LANES))
    return pl.pallas_call(
        body, name=name,
        out_shape=(jax.ShapeDtypeStruct((S5_TILES, LANES, S5_T * S5_GROUP), F32),
                   jax.ShapeDtypeStruct((S5_TILES, S5_S, S5_T * S5_GROUP), F32),
                   jax.ShapeDtypeStruct((S5_TILES, S5_T, LANES, LANES), F32)),
        grid=(S5_TILES,),
        in_specs=[row_spec, row_spec, _s5_spec(n, S5_S), _s5_spec(n, S5_S)],
        out_specs=outs,
        compiler_params=_params(("parallel",)),
    )(dy, u, hs, dg)


def _s5_prep(a_re, a_im, b_re, b_im, c_re, c_im, log_dt):
    t_len, tiles = S5_T, S5_TILES
    dt = jnp.exp(log_dt)[:, None]
    mag = jnp.exp(a_re * dt)
    ab_re, ab_im = mag * jnp.cos(a_im * dt), mag * jnp.sin(a_im * dt)
    den = jnp.square(a_re) + jnp.square(a_im)
    n_re, n_im = ab_re - 1.0, ab_im
    f_re = (n_re * a_re + n_im * a_im) / den
    f_im = (n_im * a_re - n_re * a_im) / den
    bb_re = f_re[..., None] * b_re - f_im[..., None] * b_im
    bb_im = f_re[..., None] * b_im + f_im[..., None] * b_re

    def powers(exponents):
        e = exponents[:, None, None]
        m = jnp.exp(e * (a_re * dt))
        return m * jnp.cos(e * (a_im * dt)), m * jnp.sin(e * (a_im * dt))

    p_re, p_im = powers(jnp.arange(t_len + 1, dtype=F32))
    rev_re, rev_im = powers((t_len - 1) - jnp.arange(t_len, dtype=F32))
    ca_re = c_re[None] * p_re[:, :, None, :] - c_im[None] * p_im[:, :, None, :]
    ca_im = c_re[None] * p_im[:, :, None, :] + c_im[None] * p_re[:, :, None, :]
    lag = (jnp.einsum('tgip,gpj->tgij', ca_re[:t_len], bb_re, precision=HI)
           - jnp.einsum('tgip,gpj->tgij', ca_im[:t_len], bb_im, precision=HI))
    kx = lag.reshape(t_len, tiles, 8, S5_GROUP, S5_GROUP).transpose(1, 2, 4, 0, 3)
    kx = kx.reshape(tiles, LANES, t_len * S5_GROUP)
    e_st = jnp.stack([ca_re[1:], -ca_im[1:]])
    e_st = e_st.reshape(2, t_len, tiles, 8, S5_GROUP, S5_STATE).transpose(2, 0, 3, 5, 1, 4)
    ec = e_st.reshape(tiles, S5_S, t_len * S5_GROUP)
    ab_b = jnp.stack([rev_re[..., None] * bb_re[None] - rev_im[..., None] * bb_im[None],
                      rev_re[..., None] * bb_im[None] + rev_im[..., None] * bb_re[None]])
    ab_b = ab_b.reshape(2, t_len, tiles, 8, S5_STATE, S5_GROUP).transpose(2, 1, 3, 5, 0, 4)
    fc = ab_b.reshape(tiles, t_len, LANES, 2 * S5_STATE)
    a_t = jnp.stack([p_re[t_len], p_im[t_len]]).reshape(2, tiles, 8 * S5_STATE).transpose(1, 0, 2)
    return kx, ec, fc, a_t.reshape(tiles, 1, S5_S)


TM_ROW = 256


def _gdn_fwd(x, w, tag):
    qkv = _mm(x, w["wqkv"], name="gdn_proj_qkv")
    z = _mm(x, w["wz"], name="gdn_proj_z")
    ba = _mm(x, w["wba"], name="gdn_proj_ba")
    cv = _conv_fwd(qkv, w["conv_w"], tm=TM_ROW, name="gdn_conv")
    q, k, v = _rw_fwd(_f_gdn_qkv, [cv], [], tm=TM_ROW, name="gdn_qkv")
    o, states, inverses = _gdn_scan_fwd(q, k, v, ba, w["a_log8"], w["dt_bias8"], name="gdn_scan")
    (mix,) = _rw_fwd(_f_gdn_out, [o, z], [w["norm_g"]], tm=TM_ROW, name="gdn_out", out_dtypes=[BF16])
    return mix, (qkv, z, ba, cv, q, k, v, states, inverses, o)


def _gdn_bwd(x, w, saved, dmix, dx_acc, token=None):
    qkv, z, ba, cv, q, k, v, states, inverses, o = saved
    norm_g = w["norm_g"] if token is None else w["norm_g"] + token[0, 0]
    (do, dz), (dnorm_g,) = _rw_bwd(_f_gdn_out, [o, z], [norm_g], [dmix], row_grad=[1, 1], param_grad=[1],
                                   tm=TM_ROW, name="gdn_out_bwd", row_dtypes=[F32, BF16])
    dq, dk, dv, dba, dalog, ddtb = _gdn_scan_bwd(q, k, v, ba, w["a_log8"], w["dt_bias8"], states, inverses, do,
                                                  name="gdn_scan_bwd")
    (dcv,), _ = _rw_bwd(_f_gdn_qkv, [cv], [], [dq, dk, dv], row_grad=[1], param_grad=[], tm=TM_ROW,
                        name="gdn_qkv_bwd")
    dqkv, dconv_w = _conv_bwd(qkv, w["conv_w"], dcv, tm=TM_ROW, name="gdn_conv_bwd")
    dx = _mm(dqkv, w["wqkv"], tb=True, acc=dx_acc, name="gdn_dx_qkv")
    dx = _mm(dz, w["wz"], tb=True, acc=dx, name="gdn_dx_z")
    dx = _mm(dba, w["wba"], tb=True, acc=dx, name="gdn_dx_ba")
    grads = dict(wqkv=_mm(x, dqkv, ta=True, name="gdn_dw_qkv"), wz=_mm(x, dz, ta=True, name="gdn_dw_z"),
                 wba=_mm(x, dba, ta=True, name="gdn_dw_ba"), conv_w=dconv_w,
                 a_log=dalog[:, 0], dt_bias=ddtb[:, 0], norm_g=dnorm_g[0])
    return dx, grads


def _s5_fwd(x, w, tag):
    u = _mm(x, w["wu"], name="s5_proj_u")
    y, hs = _s5_scan_fwd(u, w["kx"], w["ec"], w["fc"], w["a_t"], name="s5_scan")
    (zg,) = _rw_fwd(_f_s5_gelu, [y, u], [w["d"]], tm=TM_ROW, name="s5_gelu")
    t = _mm(zg, w["w_glu"], name="s5_glu")
    (mix,) = _rw_fwd(_f_s5_gate, [zg, t], [w["b_glu"]], tm=TM_ROW, name="s5_gate", out_dtypes=[BF16])
    return mix, (u, hs, y, zg, t)


def _s5_bwd(x, w, saved, dmix, dx_acc, token=None):
    u, hs, y, zg, t = saved
    b_glu = w["b_glu"] if token is None else w["b_glu"] + token[0, 0]
    (dzg, dt), (db_glu,) = _rw_bwd(_f_s5_gate, [zg, t], [b_glu], [dmix], row_grad=[1, 1], param_grad=[1],
                                   tm=TM_ROW, name="s5_gate_bwd", row_dtypes=[F32, BF16])
    dzg = _mm(dt, w["w_glu"], tb=True, acc=dzg, name="s5_dzg")
    dw_glu = _mm(zg, dt, ta=True, name="s5_dw_glu")
    (dy, du), (dd,) = _rw_bwd(_f_s5_gelu, [y, u], [w["d"]], [dzg], row_grad=[1, 1], param_grad=[1],
                              tm=TM_ROW, name="s5_gelu_bwd")
    du_scan, dg, dat = _s5_scan_bwd(dy, w["kx"], w["ec"], w["fc"], w["a_t"], hs, name="s5_scan_bwd")
    dkx, dec, dfc = _s5_operator_grads(dy, u, hs, dg, name="s5_operator_grads")
    (du,) = _rw_fwd(_f_add, [du, du_scan], [], tm=TM_ROW, name="s5_du_add", out_dtypes=[BF16])
    d_a_re, d_a_im, d_b_re, d_b_im, d_c_re, d_c_im, d_log_dt = w["prep_vjp"]((dkx, dec, dfc, dat))
    dx = _mm(du, w["wu"], tb=True, acc=dx_acc, name="s5_dx_u")
    grads = dict(wu=_mm(x, du, ta=True, name="s5_dw_u"), w_glu=dw_glu, b_glu=db_glu[0], d=dd[0],
                 a_re=d_a_re, a_im=d_a_im, b_re=d_b_re, b_im=d_b_im, c_re=d_c_re, c_im=d_c_im, log_dt=d_log_dt)
    return dx, grads


def _ln_res_both(x, h, g, b):
    (y,) = _f_ln_res(x, h, g, b)
    return y, y


def _layer_fwd(x, xb, mem, w, is_gdn):
    mix, msave = (_gdn_fwd if is_gdn else _s5_fwd)(xb, w, "")
    xq = _mm(xb, w["wxq"], name="proj_xq")
    kv = _mm(mem, w["wkv"], name="mem_kv")
    kmem, vmem = kv[:, :XA_DIM], kv[:, XA_DIM:]
    (cross,) = _rw_fwd(_f_attn, [xq], [kmem, vmem], tm=TM_ROW, name="attn", out_dtypes=[BF16])
    h = _mm(mix, w["wo_mix"], name="wo_mix")
    h = _mm(cross, w["wo_cross"], acc=h, name="wo_cross")
    x1, x1b = _rw_fwd(_ln_res_both, [x, h], [w["ln1_g"], w["ln1_b"]], tm=TM_ROW, name="ln_res",
                      out_dtypes=[F32, BF16])
    hm, act = _mm_relu2(x1b, w["w1"], name="mlp_up")
    f = _mm(act, w["w2"], name="mlp_down")
    x2, x2b = _rw_fwd(_ln_res_both, [x1, f], [w["ln2_g"], w["ln2_b"]], tm=TM_ROW, name="ln_res",
                      out_dtypes=[F32, BF16])
    return x2, x2b, (x, xb, msave, xq, kmem, vmem, mix, cross, h, x1, x1b, hm, act, f)


def _layer_bwd(mem, w, is_gdn, saved, dx2, token=None, before_mixer=None):
    x, xb, msave, xq, kmem, vmem, mix, cross, h, x1, x1b, hm, act, f = saved
    ln2_g = w["ln2_g"] if token is None else w["ln2_g"] + token[0, 0]
    (dx1, df), (dg2, db2) = _rw_bwd(_f_ln_res, [x1, f], [ln2_g, w["ln2_b"]], [dx2], row_grad=[1, 1],
                                    param_grad=[1, 1], tm=TM_ROW, name="ln_res_bwd", row_dtypes=[F32, BF16])
    dhm = _mm_relu2_grad(df, w["w2"], hm, name="mlp_dhm")
    dw2 = _mm(act, df, ta=True, name="mlp_dw2")
    dx1 = _mm(dhm, w["w1"], tb=True, acc=dx1, name="mlp_dx")
    dw1 = _mm(x1b, dhm, ta=True, out_blocks=N_CHIPS, name="mlp_dw1")
    (dx, dh), (dg1, db1) = _rw_bwd(_f_ln_res, [x, h], [w["ln1_g"], w["ln1_b"]], [dx1], row_grad=[1, 1],
                                   param_grad=[1, 1], tm=TM_ROW, name="ln_res_bwd", row_dtypes=[F32, BF16])
    dmix =_mm(dh, w["wo_mix"], tb=True, name="wo_dmix")
    dcross = _mm(dh, w["wo_cross"], tb=True, name="wo_dcross")
    dwo = jnp.concatenate([_mm(mix, dh, ta=True, name="wo_dw_mix"), _mm(cross, dh, ta=True, name="wo_dw_cross")], 0)
    (dxq,), (dkmem, dvmem) = _rw_bwd(_f_attn, [xq], [kmem, vmem], [dcross], row_grad=[1], param_grad=[1, 1],
                                     tm=TM_ROW, name="attn_bwd", row_dtypes=[BF16])
    dwkv = _mm(mem, jnp.concatenate([dkmem, dvmem], axis=1), ta=True, name="mem_dw_kv")
    dx = _mm(dxq, w["wxq"], tb=True, acc=dx, name="dx_xq")
    dwxq = _mm(xb, dxq, ta=True, name="dw_xq")
    mixer_token = None if before_mixer is None else before_mixer(dict(w_kv_mem=dwkv, w_o=dwo, mlp_w1=dw1, mlp_w2=dw2))
    dx, mg = (_gdn_bwd if is_gdn else _s5_bwd)(xb, w, msave, dmix, dx, mixer_token)
    grads = dict(mixer=mg, wxq=dwxq, wkv=dwkv, wo=dwo, w1=dw1, w2=dw2,
                 ln1_g=dg1[0], ln1_b=db1[0], ln2_g=dg2[0], ln2_b=db2[0])
    return dx, grads


def _loss_and_grad(y, target):
    def f(yv, tv):
        err = yv - tv
        return (err * (1.0 / D_MODEL),), (0.5 / D_MODEL * jnp.sum(err * err, axis=0, keepdims=True),)

    (dy,), (part,) = _rowwise(f, [y, target], [], [(D_MODEL, F32)], [((1, D_MODEL), F32)], tm=512, name="loss")
    return jnp.sum(part), dy


def _layer_weights(full, i):
    j = i // 2
    w = dict(wkv=full["w_kv_mem"][i].astype(BF16),
             wo_mix=full["w_o"][i][:D_MODEL].astype(BF16), wo_cross=full["w_o"][i][D_MODEL:].astype(BF16),
             ln1_g=full["ln1_g"][i][None], ln1_b=full["ln1_b"][i][None],
             ln2_g=full["ln2_g"][i][None], ln2_b=full["ln2_b"][i][None],
             w1=full["mlp_w1"][i].astype(BF16), w2=full["mlp_w2"][i].astype(BF16))
    if i % 2 == 0:
        w_in = full["gdn_w_in"][j]
        gd = 3 * D_MODEL
        w.update(wqkv=w_in[:, :gd].astype(BF16), wz=w_in[:, gd:gd + D_MODEL].astype(BF16),
                 wba=jnp.pad(w_in[:, gd + D_MODEL:gd + D_MODEL + 2 * GDN_HEADS],
                             ((0, 0), (0, LANES - 2 * GDN_HEADS))).astype(BF16),
                 wxq=w_in[:, gd + D_MODEL + 2 * GDN_HEADS:].astype(BF16),
                 conv_w=full["gdn_conv_w"][j],
                 a_log8=jnp.broadcast_to(full["gdn_a_log"][j][:, None], (GDN_HEADS, LANES)),
                 dt_bias8=jnp.broadcast_to(full["gdn_dt_bias"][j][:, None], (GDN_HEADS, LANES)),
                 norm_g=full["gdn_norm_g"][j][None])
    else:
        w_in = full["s5_w_in"][j]
        (kx, ec, fc, a_t), prep_vjp = jax.vjp(
            _s5_prep, full["s5_a_re"][j], full["s5_a_im"][j], full["s5_b_re"][j], full["s5_b_im"][j],
            full["s5_c_re"][j], full["s5_c_im"][j], full["s5_log_dt"][j])
        w.update(wu=w_in[:, :D_MODEL].astype(BF16), wxq=w_in[:, D_MODEL:].astype(BF16),
                 kx=kx, ec=ec, fc=fc, a_t=a_t, prep_vjp=prep_vjp,
                 d=full["s5_d"][j][None], w_glu=full["s5_w_glu"][j].astype(BF16), b_glu=full["s5_b_glu"][j][None])
    return w


def _sharded_grads(l, i):
    m = l["mixer"]
    out = dict(w_kv_mem=l["wkv"], w_o=l["wo"], mlp_w1=l["w1"], mlp_w2=l["w2"])
    if i % 2 == 0:
        out.update(gdn_w_in=jnp.concatenate([m["wqkv"], m["wz"], m["wba"][:, :2 * GDN_HEADS], l["wxq"]], axis=1),
                   gdn_conv_w=m["conv_w"])
    else:
        out.update(s5_w_in=jnp.concatenate([m["wu"], l["wxq"]], axis=1), s5_d=m["d"], s5_w_glu=m["w_glu"],
                   s5_b_glu=m["b_glu"])
    return out


def _replicated_grads(layer_grads):
    g = layer_grads
    gdn = [g[i]["mixer"] for i in range(DEPTH) if i % 2 == 0]
    s5 = [g[i]["mixer"] for i in range(DEPTH) if i % 2 == 1]
    out = {n: jnp.stack([l[n] for l in g]) for n in ("ln1_g", "ln1_b", "ln2_g", "ln2_b")}
    out.update({"gdn_" + n: jnp.stack([m[n] for m in gdn]) for n in ("a_log", "dt_bias", "norm_g")})
    out.update({"s5_" + n: jnp.stack([m[n] for m in s5])
                for n in ("a_re", "a_im", "b_re", "b_im", "c_re", "c_im", "log_dt")})
    return out


def _local_step(x, mem, target, weights_of, grads_ready, before_first_mixer):
    lw, saves = [], []
    h, hb = x, x.astype(BF16)
    for i in range(DEPTH):
        lw.append(weights_of(i, h))
        h, hb, s = _layer_fwd(h, hb, mem, lw[i], i % 2 == 0)
        saves.append(s)
    loss, d = _loss_and_grad(h, target)
    grads = [None] * DEPTH
    token = None
    for i in reversed(range(DEPTH)):
        d, grads[i] = _layer_bwd(mem, lw[i], i % 2 == 0, saves[i], d, token, None if i else before_first_mixer)
        token = grads_ready(i, grads[i])
    return loss, d, grads


ANY = pl.BlockSpec(memory_space=pl.ANY)
SHARD_ROWS = 1024
SMALL_ROWS = 128


def _place():
    return lax.axis_index("x"), lax.axis_index("y"), lax.axis_index("c")


def _other_chips(x, y):
    return [(1 - x, y), (x, 1 - y), (1 - x, 1 - y)]


def _all_gather_chips(wpack, *, name):
    rows = wpack.shape[0]
    half = rows // 2

    def body(w_ref, out_ref, send_sems, recv_sems):
        x, y, c = _place()
        sibling = (x, y, 1 - c)
        chips = _other_chips(x, y)

        def blk(cx, cy, cc):
            return out_ref.at[2 * cx + cy, pl.ds(cc * half, half), :]

        def copy(k, src, dst, to):
            return pltpu.make_async_remote_copy(src_ref=src, dst_ref=dst, send_sem=send_sems.at[k],
                                                recv_sem=recv_sems.at[k], device_id=to, device_id_type=MESH)

        first = [copy(j, w_ref.at[pl.ds(c * half, half), :], blk(x, y, c), (cx, cy, c))
                 for j, (cx, cy) in enumerate(chips)]
        for cp in first:
            cp.start()
        passed = [copy(3 + j, blk(cx, cy, c), blk(cx, cy, c), sibling) for j, (cx, cy) in enumerate(chips)]
        for j, (cx, cy) in enumerate(chips):
            copy(j, blk(cx, cy, c), blk(cx, cy, c), (cx, cy, c)).wait_recv()
            passed[j].start()
        for j, (cx, cy) in enumerate(chips):
            copy(3 + j, blk(cx, cy, 1 - c), blk(cx, cy, 1 - c), sibling).wait_recv()
        for cp in first + passed:
            cp.wait_send()

    return pl.pallas_call(
        body, name=name, out_shape=jax.ShapeDtypeStruct((N_CHIPS, rows, D_MODEL), wpack.dtype),
        in_specs=[ANY], out_specs=ANY,
        scratch_shapes=[pltpu.SemaphoreType.DMA((6,)), pltpu.SemaphoreType.DMA((6,))],
    )(wpack)


HBM = pl.BlockSpec(memory_space=pltpu.HBM)
SEM = pl.BlockSpec(memory_space=pltpu.SEMAPHORE)
DATAFLOW = pltpu.SideEffectType.DATAFLOW_SIDE_EFFECTING


def _gather_ici_copies(w_ref, land_ref, send_sems, recv_sems, outgoing):
    x, y, c = _place()
    half = w_ref.shape[0] // 2
    mine = pl.ds(c * half, half)
    return [pltpu.make_async_remote_copy(
        src_ref=w_ref.at[mine, :], dst_ref=land_ref.at[2 * x + y if outgoing else 2 * cx + cy, mine, :],
        send_sem=send_sems.at[j], recv_sem=recv_sems.at[j], device_id=(cx, cy, c), device_id_type=MESH)
        for j, (cx, cy) in enumerate(_other_chips(x, y))]


def _gather_start(wpack, after):
    rows = wpack.shape[0]

    def body(w_ref, land_ref, after_ref, send_sems, recv_sems, w_thru, land_thru, token):
        for cp in _gather_ici_copies(w_ref, land_ref, send_sems, recv_sems, outgoing=True):
            cp.start()
        token[...] = jnp.zeros_like(token)

    land = pltpu.with_memory_space_constraint(lax.empty((N_CHIPS, rows, D_MODEL), wpack.dtype), pltpu.HBM)
    return pl.pallas_call(
        body, name="gather_start",
        out_shape=(pltpu.SemaphoreType.DMA((3,)), pltpu.SemaphoreType.DMA((3,)), pltpu.HBM(wpack.shape, wpack.dtype),
                   pltpu.HBM(land.shape, land.dtype), jax.ShapeDtypeStruct((SUBLANES, LANES), F32)),
        in_specs=(HBM, HBM, ANY), out_specs=(SEM, SEM, HBM, HBM, pl.BlockSpec(memory_space=pltpu.VMEM)),
        input_output_aliases={0: 2, 1: 3},
        compiler_params=pltpu.CompilerParams(has_side_effects=DATAFLOW),
    )(pltpu.with_memory_space_constraint(wpack, pltpu.HBM), land, after)


def _gather_wait(send_sems, recv_sems, w_thru, land_thru, after):
    def body(w_ref, land_ref, send_sems, recv_sems, after_ref, w_dead, land_out):
        for cp in _gather_ici_copies(w_ref, land_ref, send_sems, recv_sems, outgoing=False):
            cp.wait_send()
            cp.wait_recv()

    return pl.pallas_call(
        body, name="gather_wait",
        out_shape=(pltpu.HBM(w_thru.shape, w_thru.dtype), pltpu.HBM(land_thru.shape, land_thru.dtype)),
        in_specs=(HBM, HBM, SEM, SEM, ANY), out_specs=(HBM, HBM), input_output_aliases={0: 0, 1: 1},
        compiler_params=pltpu.CompilerParams(has_side_effects=DATAFLOW),
    )(w_thru, land_thru, send_sems, recv_sems, after)[1]


def _gather_forward(land, *, name):
    rows = land.shape[1]
    half = rows // 2

    def body(in_ref, out_ref, send_sems, recv_sems):
        x, y, c = _place()

        def copy(j, cx, cy, cc):
            rows_of = out_ref.at[2 * cx + cy, pl.ds(cc * half, half), :]
            return pltpu.make_async_remote_copy(src_ref=rows_of, dst_ref=rows_of, send_sem=send_sems.at[j],
                                                recv_sem=recv_sems.at[j], device_id=(x, y, 1 - c), device_id_type=MESH)

        sends = [copy(j, cx, cy, c) for j, (cx, cy) in enumerate(_other_chips(x, y))]
        for cp in sends:
            cp.start()
        for j, (cx, cy) in enumerate(_other_chips(x, y)):
            copy(j, cx, cy, 1 - c).wait_recv()
        for cp in sends:
            cp.wait_send()

    return pl.pallas_call(
        body, name=name, out_shape=jax.ShapeDtypeStruct(land.shape, land.dtype), in_specs=[ANY], out_specs=ANY,
        input_output_aliases={0: 0},
        scratch_shapes=[pltpu.SemaphoreType.DMA((3,)), pltpu.SemaphoreType.DMA((3,))],
    )(land)


def _sibling_swap(buf, *, name):
    def body(in_ref, out_ref, send_sem, recv_sem):
        x, y, c = _place()
        cp = pltpu.make_async_remote_copy(src_ref=in_ref, dst_ref=out_ref, send_sem=send_sem, recv_sem=recv_sem,
                                          device_id=(x, y, 1 - c), device_id_type=MESH)
        cp.start()
        cp.wait()

    return pl.pallas_call(
        body, name=name, out_shape=jax.ShapeDtypeStruct(buf.shape, buf.dtype), in_specs=[ANY], out_specs=ANY,
        scratch_shapes=[pltpu.SemaphoreType.DMA, pltpu.SemaphoreType.DMA],
    )(buf)


def _pair_exchange(gpack, *, name):
    pieces, rows, width = gpack.shape
    half = rows // 2

    def body(in_ref, got_ref, send_sems, recv_sems):
        x, y, c = _place()
        sends = [pltpu.make_async_remote_copy(src_ref=in_ref.at[p, pl.ds((1 - c) * half, half), :],
                                              dst_ref=got_ref.at[p], send_sem=send_sems.at[p],
                                              recv_sem=recv_sems.at[p], device_id=(x, y, 1 - c), device_id_type=MESH)
                 for p in range(pieces)]
        for cp in sends:
            cp.start()
        for cp in sends:
            cp.wait()

    return pl.pallas_call(
        body, name=name, out_shape=jax.ShapeDtypeStruct((pieces, half, width), gpack.dtype),
        in_specs=[ANY], out_specs=ANY,
        scratch_shapes=[pltpu.SemaphoreType.DMA((pieces,)), pltpu.SemaphoreType.DMA((pieces,))],
    )(gpack)


def _pair_add(gpack, got, c, *, name, tm=512):
    pieces, rows, width = gpack.shape
    half = rows // 2
    nb = half // tm

    def body(c_ref, a_ref, b_ref, sum_ref, narrow_ref):
        s = a_ref[...] + b_ref[...]
        sum_ref[...] = s
        narrow_ref[...] = s.astype(BF16)

    blk = pl.BlockSpec((None, tm, width), lambda p, i, c_ref: (p, i, 0))
    return pl.pallas_call(
        body, name=name,
        out_shape=(jax.ShapeDtypeStruct((pieces, half, width), F32), jax.ShapeDtypeStruct((pieces, half, width), BF16)),
        grid_spec=pltpu.PrefetchScalarGridSpec(
            num_scalar_prefetch=1, grid=(pieces, nb),
            in_specs=[pl.BlockSpec((None, tm, width), lambda p, i, c_ref: (p, c_ref[0] * nb + i, 0)), blk],
            out_specs=(blk, blk)),
        compiler_params=_params(("parallel", "parallel")),
    )(c, gpack, got)


def _chip_exchange(pieces, *, name):
    _, rows, width = pieces.shape

    def body(in_ref, out_ref, send_sems, recv_sems):
        x, y, c = _place()
        cps = [pltpu.make_async_remote_copy(src_ref=in_ref.at[2 * cx + cy], dst_ref=out_ref.at[j],
                                            send_sem=send_sems.at[j], recv_sem=recv_sems.at[j],
                                            device_id=(cx, cy, c), device_id_type=MESH)
               for j, (cx, cy) in enumerate(_other_chips(x, y))]
        for cp in cps:
            cp.start()
        for cp in cps:
            cp.wait()

    return pl.pallas_call(
        body, name=name, out_shape=jax.ShapeDtypeStruct((3, rows, width), pieces.dtype), in_specs=[ANY], out_specs=ANY,
        scratch_shapes=[pltpu.SemaphoreType.DMA((3,)), pltpu.SemaphoreType.DMA((3,))],
    )(pieces)


def _chip_exchange_copies(in_ref, land_ref, send_sems, recv_sems):
    x, y, c = _place()
    return [pltpu.make_async_remote_copy(src_ref=in_ref.at[2 * cx + cy], dst_ref=land_ref.at[j],
                                         send_sem=send_sems.at[j], recv_sem=recv_sems.at[j],
                                         device_id=(cx, cy, c), device_id_type=MESH)
            for j, (cx, cy) in enumerate(_other_chips(x, y))]


def _chip_exchange_start(pieces):
    _, rows, width = pieces.shape

    def body(in_ref, land_ref, send_sems, recv_sems, in_thru, land_thru, token):
        for cp in _chip_exchange_copies(in_ref, land_ref, send_sems, recv_sems):
            cp.start()
        token[...] = jnp.zeros_like(token)

    land = pltpu.with_memory_space_constraint(lax.empty((3, rows, width), pieces.dtype), pltpu.HBM)
    return pl.pallas_call(
        body, name="rs_chip_start",
        out_shape=(pltpu.SemaphoreType.DMA((3,)), pltpu.SemaphoreType.DMA((3,)), pltpu.HBM(pieces.shape, pieces.dtype),
                   pltpu.HBM(land.shape, land.dtype), jax.ShapeDtypeStruct((SUBLANES, LANES), F32)),
        in_specs=(HBM, HBM), out_specs=(SEM, SEM, HBM, HBM, pl.BlockSpec(memory_space=pltpu.VMEM)),
        input_output_aliases={0: 2, 1: 3},
        compiler_params=pltpu.CompilerParams(has_side_effects=DATAFLOW),
    )(pltpu.with_memory_space_constraint(pieces, pltpu.HBM), land)


def _chip_exchange_wait(send_sems, recv_sems, in_thru, land_thru, after):
    def body(in_ref, land_ref, send_sems, recv_sems, after_ref, in_dead, land_out):
        for cp in _chip_exchange_copies(in_ref, land_ref, send_sems, recv_sems):
            cp.wait_send()
            cp.wait_recv()

    return pl.pallas_call(
        body, name="rs_chip_wait",
        out_shape=(pltpu.HBM(in_thru.shape, in_thru.dtype), pltpu.HBM(land_thru.shape, land_thru.dtype)),
        in_specs=(HBM, HBM, SEM, SEM, ANY), out_specs=(HBM, HBM), input_output_aliases={0: 0, 1: 1},
        compiler_params=pltpu.CompilerParams(has_side_effects=DATAFLOW),
    )(in_thru, land_thru, send_sems, recv_sems, after)[1]


def _all_reduce_small(v, *, name):
    rows, width = v.shape
    half = rows // 2
    assert half % SUBLANES == 0

    def body(in_ref, out_ref, pair_buf, chip_buf, send_sems, recv_sems):
        x, y, c = _place()
        sibling = (x, y, 1 - c)
        me = 2 * x + y
        mine = pl.ds(pl.multiple_of(c * half, SUBLANES), half)
        other = pl.ds(pl.multiple_of((1 - c) * half, SUBLANES), half)

        def copy(k, src, dst, to):
            return pltpu.make_async_remote_copy(src_ref=src, dst_ref=dst, send_sem=send_sems.at[k],
                                                recv_sem=recv_sems.at[k], device_id=to, device_id_type=MESH)

        swap = copy(0, in_ref.at[other, :], pair_buf, sibling)
        swap.start()
        swap.wait()
        chip_buf[me] = in_ref[mine, :] + pair_buf[...]
        chips = _other_chips(x, y)
        for j, (cx, cy) in enumerate(chips):
            copy(1 + j, chip_buf.at[me], chip_buf.at[me], (cx, cy, c)).start()
        for j, (cx, cy) in enumerate(chips):
            got = copy(1 + j, chip_buf.at[me], chip_buf.at[2 * cx + cy], (cx, cy, c))
            got.wait_send()
            got.wait_recv()
        out_ref[mine, :] = ((chip_buf[0] + chip_buf[1]) + chip_buf[2]) + chip_buf[3]
        share = copy(1 + len(chips), out_ref.at[mine, :], out_ref.at[mine, :], sibling)
        share.start()
        share.wait_send()
        copy(1 + len(chips), out_ref.at[other, :], out_ref.at[other, :], sibling).wait_recv()

    vmem = pl.BlockSpec(memory_space=pltpu.VMEM)
    return pl.pallas_call(
        body, name=name, out_shape=jax.ShapeDtypeStruct(v.shape, v.dtype), in_specs=[vmem], out_specs=vmem,
        scratch_shapes=[pltpu.VMEM((half, width), v.dtype), pltpu.VMEM((N_CHIPS, half, width), v.dtype),
                        pltpu.SemaphoreType.DMA((5,)), pltpu.SemaphoreType.DMA((5,))],
        compiler_params=pltpu.CompilerParams(vmem_limit_bytes=VMEM_LIMIT_V7X),
    )(v)


def _reduce_scatter_begin(gpack, behind):
    x, y, c = _place()
    got = _pair_exchange(gpack, name="rs_pair_swap")
    pair, pair16 = _pair_add(gpack, got, c.astype(jnp.int32).reshape(1), name="rs_pair_add")
    mine = lax.dynamic_index_in_dim(pair, 2 * x + y, axis=0, keepdims=False)
    if behind:
        *in_flight, token = _chip_exchange_start(pair16)
        return dict(mine=mine, in_flight=in_flight), token
    return dict(mine=mine, recv=_chip_exchange(pair16, name="rs_chip_exchange")), None


def _reduce_scatter_end(state, after=None):
    c = lax.axis_index("c")
    recv = state["recv"] if "recv" in state else _chip_exchange_wait(*state["in_flight"], after=after)
    (total,) = _rw_fwd(_f_add4, [state["mine"], recv[0], recv[1], recv[2]], [], tm=512, name="rs_chip_add")
    theirs = _sibling_swap(total, name="rs_share_swap")
    return jnp.concatenate([jnp.where(c == 0, total, theirs), jnp.where(c == 0, theirs, total)], axis=0)


_SHARDED = (("w_kv_mem", 1), ("w_o", 1), ("mlp_w1", 2), ("mlp_w2", 1), ("gdn_w_in", 2), ("gdn_conv_w", 2),
            ("s5_w_in", 2), ("s5_d", 1), ("s5_w_glu", 1), ("s5_b_glu", 1))
_MATMUL_ONLY = ("w_kv_mem", "w_o", "mlp_w1", "mlp_w2", "gdn_w_in", "s5_w_in", "s5_w_glu")
_KEPT_BLOCKED = ("mlp_w1",)
_REPLICATED = ("ln1_g", "ln1_b", "ln2_g", "ln2_b", "gdn_a_log", "gdn_dt_bias", "gdn_norm_g", "s5_a_re", "s5_a_im",
               "s5_b_re", "s5_b_im", "s5_c_re", "s5_c_im", "s5_log_dt")
_WEIGHTS = ("w_kv_mem", "w_o", "ln1_g", "ln1_b", "ln2_g", "ln2_b", "mlp_w1", "mlp_w2", "gdn_w_in", "gdn_conv_w",
            "gdn_a_log", "gdn_dt_bias", "gdn_norm_g", "s5_w_in", "s5_a_re", "s5_a_im", "s5_b_re", "s5_b_im",
            "s5_c_re", "s5_c_im", "s5_log_dt", "s5_d", "s5_w_glu", "s5_b_glu")


ROW_ALIGN = 16


def _n_rows(shape):
    return -(-math.prod(shape) // (ROW_ALIGN * D_MODEL)) * ROW_ALIGN


def _as_rows(a):
    rows = _n_rows(a.shape)
    if a.shape[-1] == D_MODEL and a.size == rows * D_MODEL:
        return a.reshape(-1, D_MODEL)
    flat = a.reshape(-1)
    return jnp.pad(flat, (0, rows * D_MODEL - flat.size)).reshape(rows, D_MODEL)


def _pack(arrs, unit_rows=SHARD_ROWS):
    rows = [_as_rows(a) for a in arrs]
    pad = -sum(r.shape[0] for r in rows) % unit_rows
    if pad:
        rows.append(jnp.zeros((pad, D_MODEL), rows[0].dtype))
    return jnp.concatenate(rows, axis=0)


def _unpack(packed, shapes):
    lead = packed.shape[:-2]
    out, off = [], 0
    for s in shapes:
        r = _n_rows(s)
        seg = lax.slice_in_dim(packed, off, off + r, axis=len(lead))
        if s[-1] != D_MODEL or math.prod(s) != r * D_MODEL:
            seg = lax.slice_in_dim(seg.reshape(lead + (-1,)), 0, math.prod(s), axis=len(lead))
        out.append(seg.reshape(lead + tuple(s)))
        off += r
    return out


def _split3(t):
    hi = t.astype(BF16)
    r1 = t - hi.astype(F32)
    mid = r1.astype(BF16)
    lo = (r1 - mid.astype(F32)).astype(BF16)
    return jnp.stack([hi, mid, lo], axis=-1)


def _join3(t):
    return (t[..., 0].astype(F32) + t[..., 1].astype(F32)) + t[..., 2].astype(F32)


def _merge_chips(blocks, axis):
    return jnp.concatenate([blocks[s] for s in range(N_CHIPS)], axis=axis)


def _pack_for_chips(weights):
    rows = []
    for s in range(N_CHIPS):
        chip = []
        for layers, axis in weights:
            if axis is None:
                blocks = [g[s] for g in layers]
            else:
                n = layers[0].shape[axis] // N_CHIPS
                blocks = [lax.slice_in_dim(g, s * n, (s + 1) * n, axis=axis) for g in layers]
            if math.prod(blocks[0].shape) % (ROW_ALIGN * D_MODEL) == 0:
                chip += [_as_rows(b) for b in blocks]
            else:
                chip.append(_as_rows(jnp.stack(blocks)))
        pad = -sum(r.shape[0] for r in chip) % SHARD_ROWS
        rows += chip + ([jnp.zeros((pad, D_MODEL), F32)] if pad else [])
    return jnp.concatenate(rows, axis=0).reshape(N_CHIPS, -1, D_MODEL)


def kernel(x, mem, w_kv_mem, w_o, ln1_g, ln1_b, ln2_g, ln2_b, mlp_w1, mlp_w2, gdn_w_in, gdn_conv_w, gdn_a_log, gdn_dt_bias, gdn_norm_g, s5_w_in, s5_a_re, s5_a_im, s5_b_re, s5_b_im, s5_c_re, s5_c_im, s5_log_dt, s5_d, s5_w_glu, s5_b_glu, loss_target, m_w_kv_mem, m_w_o, m_ln1_g, m_ln1_b, m_ln2_g, m_ln2_b, m_mlp_w1, m_mlp_w2, m_gdn_w_in, m_gdn_conv_w, m_gdn_a_log, m_gdn_dt_bias, m_gdn_norm_g, m_s5_w_in, m_s5_a_re, m_s5_a_im, m_s5_b_re, m_s5_b_im, m_s5_c_re, m_s5_c_im, m_s5_log_dt, m_s5_d, m_s5_w_glu, m_s5_b_glu, v_w_kv_mem, v_w_o, v_ln1_g, v_ln1_b, v_ln2_g, v_ln2_b, v_mlp_w1, v_mlp_w2, v_gdn_w_in, v_gdn_conv_w, v_gdn_a_log, v_gdn_dt_bias, v_gdn_norm_g, v_s5_w_in, v_s5_a_re, v_s5_a_im, v_s5_b_re, v_s5_b_im, v_s5_c_re, v_s5_c_im, v_s5_log_dt, v_s5_d, v_s5_w_glu, v_s5_b_glu):
    given = dict(locals())
    w = {n: given[n] for n in _WEIGHTS}
    mom = {n: given["m_" + n] for n in _WEIGHTS}
    var = {n: given["v_" + n] for n in _WEIGHTS}
    shard_names = [n for n, _ in _SHARDED]
    shard_shapes = [w[n].shape for n in shard_names]
    rep_shapes = [w[n].shape for n in _REPLICATED]

    wire = {n: w[n].astype(BF16) if n in _MATMUL_ONLY else _split3(w[n]) for n in shard_names}
    first = {n: 0 if n.startswith("s5_") else 1 for n in shard_names}
    me_chip = 2 * lax.axis_index("x") + lax.axis_index("y")
    early = [wire[n][:first[n]] for n in shard_names if first[n]]
    late = [wire[n][first[n]:] for n in shard_names]
    early_pack, late_pack = _pack(early), _pack(late)
    landed = _all_gather_chips(early_pack, name="gather_first_layer")
    landed = lax.dynamic_update_index_in_dim(landed, early_pack, me_chip, axis=0)
    early_blocks = dict(zip([n for n in shard_names if first[n]], _unpack(landed, [a.shape for a in early])))
    send_sems, recv_sems, pack_thru, land_thru, token = _gather_start(late_pack, after=landed)
    axis_of = dict(_SHARDED)

    def merged(n, blk):
        if n in _KEPT_BLOCKED:
            return blk
        return _merge_chips(blk if n in _MATMUL_ONLY else _join3(blk), axis_of[n] - 1)

    late_full = {}

    def weights_of(i, h):
        if i == 0:
            full = {n: [merged(n, blk[:, 0])] for n, blk in early_blocks.items()}
            full["gdn_w_in"][0] = full["gdn_w_in"][0] + token[0, 0].astype(BF16)
        else:
            if not late_full:
                land = _gather_wait(send_sems, recv_sems, pack_thru, land_thru, after=h)
                land = _gather_forward(land, name="gather_forward")
                land = lax.dynamic_update_index_in_dim(land, late_pack, me_chip, axis=0)
                for n, blk in zip(shard_names, _unpack(land, [a.shape for a in late])):
                    late_full[n] = [None] * first[n] + [merged(n, blk[:, t]) for t in range(blk.shape[1])]
            full = dict(late_full)
        full.update({n: w[n] for n in _REPLICATED})
        return _layer_weights(full, i)

    sharded = {}
    in_flight = {}
    first_mixer, first_outer = (0, "mixer"), (0, "outer")

    def group_pack(parts):
        names = [n for n in shard_names if any(n in sharded[i] for i in parts)]
        per_weight = [[sharded[i][n] for i in parts if n in sharded[i]] for n in names]
        pack = _pack_for_chips([(g, None if n in _KEPT_BLOCKED else axis_of[n] - 1) for n, g in zip(names, per_weight)])
        return pack, names, [(len(g),) + w[n].shape[1:] for n, g in zip(names, per_weight)]

    def grads_ready(i, g):
        by_weight = _sharded_grads(g, i)
        if i:
            sharded[i] = by_weight
        else:
            sharded[first_mixer] = {n: g for n, g in by_weight.items() if n not in sharded[first_outer]}

    def before_first_mixer(outer):
        sharded[first_outer] = outer
        pack, names, shapes = group_pack([first_outer] + list(range(1, DEPTH)))
        state, token = _reduce_scatter_begin(pack, behind=True)
        in_flight.update(state=state, names=names, shapes=shapes)
        return token

    loss, grad_x, layer_grads = _local_step(x[0], mem[0], loss_target[0], weights_of, grads_ready,
                                            before_first_mixer)
    loss = lax.psum(loss, ("x", "y", "c"))
    pack, names, shapes = group_pack([first_mixer])
    state, _ = _reduce_scatter_begin(pack, behind=False)
    pieces = {n: [] for n in shard_names}
    for n, g in zip(names, _unpack(_reduce_scatter_end(state), shapes)):
        pieces[n].append(g)
    late = _reduce_scatter_end(in_flight["state"], after=grad_x)
    for n, g in zip(in_flight["names"], _unpack(late, in_flight["shapes"])):
        pieces[n].append(g)
    g_shards = [p[0] if len(p) == 1 else jnp.concatenate(p, axis=0) for p in (pieces[n] for n in shard_names)]

    def pack_small(d):
        return _pack([d[n] for n in _REPLICATED], unit_rows=SMALL_ROWS)

    g_rep = _all_reduce_small(pack_small(_replicated_grads(layer_grads)), name="reduce_replicated")

    def adamw(wp, gp, mp, vp, name):
        return _rw_fwd(_f_adamw, [wp, gp, mp, vp], [], tm=256, name=name)

    outs = {}
    for n, g in zip(shard_names, g_shards):
        flat = (-1, w[n].shape[-1])
        res = adamw(w[n].reshape(flat), g.reshape(flat), mom[n].reshape(flat), var[n].reshape(flat), "adamw_" + n)
        outs[("grad", n)] = g
        outs.update({(kind, n): a.reshape(w[n].shape) for kind, a in zip(("delta", "new_m", "new_v"), res)})
    packed = (g_rep,) + tuple(adamw(pack_small(w), g_rep, pack_small(mom), pack_small(var), "adamw_replicated"))
    for kind, pr in zip(("grad", "delta", "new_m", "new_v"), packed):
        outs.update({(kind, n): a for n, a in zip(_REPLICATED, _unpack(pr, rep_shapes))})
    return (loss, grad_x[None]) + tuple(outs[(kind, n)] for kind in ("grad", "delta", "new_m", "new_v")
                                        for n in _WEIGHTS)
```

```python
import functools
import math

import jax
import jax.numpy as jnp
from jax import lax
from jax.experimental import pallas as pl
from jax.experimental.pallas import tpu as pltpu

F32 = jnp.float32
BF16 = jnp.bfloat16
MESH = pl.DeviceIdType.MESH

D_MODEL = 1024
DEPTH = 4
GDN_HEADS = 8
HEAD_DIM = 128
GDN_CONV = 4
GDN_CHUNK = 64
S5_GROUPS = 64
S5_GROUP = 16
S5_STATE = 64
XA_HEADS = 4
XA_DIM = 512
D_FF = 4096
DN_ALPHA = (2 * DEPTH) ** 0.25
LN_EPS = 1e-5
RMS_EPS = 1e-6
ADAM_LR, ADAM_B1, ADAM_B2, ADAM_EPS, ADAM_WD, ADAM_STEP = 0.001, 0.9, 0.999, 1e-08, 0.01, 10

VMEM_LIMIT_V7X = 56 * 1024 * 1024
LANES = 128
SUBLANES = 8
S5_T = 16
S5_TILES = D_MODEL // LANES
N_CHIPS = 4
N_DEV = 8


def _params(sem):
    return pltpu.CompilerParams(dimension_semantics=sem, vmem_limit_bytes=VMEM_LIMIT_V7X)


def _tile(n, pref):
    if n <= pref:
        return n
    t = (pref // LANES) * LANES
    while n % t:
        t -= LANES
    return t


def _row_tile(n, pref):
    if n % SUBLANES:
        return n
    t = min(pref, n) // SUBLANES * SUBLANES
    while n % t:
        t -= SUBLANES
    return t


def _col_blocked_spec(rows_tile, cols_tile, block_cols, rows_axis, cols_axis):
    r = block_cols // cols_tile

    def index(*ijk):
        c = ijk[cols_axis]
        return (c, ijk[rows_axis], 0) if r == 1 else (c // r, ijk[rows_axis], c % r)

    return pl.BlockSpec((None, rows_tile, cols_tile), index)


def _mm(a, b, *, ta=False, tb=False, acc=None, name, tm=1024, tn=1024, tk=None, out_blocks=0):
    if tk is None:
        tk = 4096 if a.dtype == BF16 and b.dtype == BF16 else 2048
    k_dim, m_dim = a.shape if ta else a.shape[::-1]
    b_rows, b_cols = (b.shape[0], b.shape[1]) if b.ndim == 2 else (b.shape[1], b.shape[0] * b.shape[2])
    n_dim = b_rows if tb else b_cols
    assert (b_cols if tb else b_rows) == k_dim, (a.shape, b.shape, ta, tb)
    limit_n = n_dim // out_blocks if out_blocks else (n_dim if b.ndim == 2 or tb else b.shape[2])
    limit_k = b.shape[2] if (b.ndim == 3 and tb) else k_dim
    tm, tn, tk = _tile(m_dim, tm), _tile(limit_n, min(tn, limit_n)), _tile(limit_k, min(tk, limit_k))
    a_spec = (pl.BlockSpec((tk, tm), lambda i, j, k: (k, i)) if ta else pl.BlockSpec((tm, tk), lambda i, j, k: (i, k)))
    if b.ndim == 3:
        b_spec = (_col_blocked_spec(tn, tk, b.shape[2], 1, 2) if tb else _col_blocked_spec(tk, tn, b.shape[2], 2, 1))
    else:
        b_spec = (pl.BlockSpec((tn, tk), lambda i, j, k: (j, k)) if tb
                  else pl.BlockSpec((tk, tn), lambda i, j, k: (k, j)))
    o_spec = (_col_blocked_spec(tm, tn, n_dim // out_blocks, 0, 1) if out_blocks
              else pl.BlockSpec((tm, tn), lambda i, j, k: (i, j)))
    o_shape = (out_blocks, m_dim, n_dim // out_blocks) if out_blocks else (m_dim, n_dim)
    dn = (((0 if ta else 1,), (1 if tb else 0,)), ((), ()))
    has_acc = acc is not None

    def body(*refs):
        a_ref, b_ref = refs[0], refs[1]
        o_ref = refs[-1]
        k = pl.program_id(2)
        p = lax.dot_general(a_ref[...].astype(BF16), b_ref[...].astype(BF16), dn,
                            preferred_element_type=F32)

        @pl.when(k == 0)
        def _():
            o_ref[...] = p + refs[2][...] if has_acc else p

        @pl.when(k > 0)
        def _():
            o_ref[...] += p

    return pl.pallas_call(
        body, name=name,
        out_shape=jax.ShapeDtypeStruct(o_shape, F32),
        grid=(m_dim // tm, n_dim // tn, k_dim // tk),
        in_specs=[a_spec, b_spec] + ([o_spec] if has_acc else []),
        out_specs=o_spec,
        compiler_params=_params(("parallel", "parallel", "arbitrary")),
    )(*([a, b] + ([acc] if has_acc else [])))


def _mm_relu2(a, b, *, name, tm=1024):
    m_dim, k_dim = a.shape
    n_blocks, _, tn = b.shape
    n_dim = n_blocks * tn
    tm = _tile(m_dim, tm)

    def body(a_ref, b_ref, h_ref, act_ref):
        h = jnp.dot(a_ref[...].astype(BF16), b_ref[...].astype(BF16), preferred_element_type=F32)
        h_ref[...] = h.astype(h_ref.dtype)
        r = jnp.maximum(h, 0.0)
        act_ref[...] = (r * r).astype(BF16)

    o_spec = pl.BlockSpec((tm, tn), lambda i, j: (i, j))
    return pl.pallas_call(
        body, name=name,
        out_shape=(jax.ShapeDtypeStruct((m_dim, n_dim), BF16), jax.ShapeDtypeStruct((m_dim, n_dim), BF16)),
        grid=(m_dim // tm, n_dim // tn),
        in_specs=[pl.BlockSpec((tm, k_dim), lambda i, j: (i, 0)),
                  pl.BlockSpec((None, k_dim, tn), lambda i, j: (j, 0, 0))],
        out_specs=(o_spec, o_spec),
        compiler_params=_params(("parallel", "parallel")),
    )(a, b)


def _mm_relu2_grad(d, b, h, *, name, tm=1024, tn=1024):
    m_dim, k_dim = d.shape
    n_dim = b.shape[0]
    tm, tn = _tile(m_dim, tm), _tile(n_dim, tn)

    def body(d_ref, b_ref, h_ref, o_ref):
        p = lax.dot_general(d_ref[...].astype(BF16), b_ref[...].astype(BF16), ((NT), ((), ())),
                            preferred_element_type=F32)
        o_ref[...] = (p * (2.0 * jnp.maximum(h_ref[...].astype(F32), 0.0))).astype(BF16)

    o_spec = pl.BlockSpec((tm, tn), lambda i, j: (i, j))
    return pl.pallas_call(
        body, name=name,
        out_shape=jax.ShapeDtypeStruct((m_dim, n_dim), BF16),
        grid=(m_dim // tm, n_dim // tn),
        in_specs=[pl.BlockSpec((tm, k_dim), lambda i, j: (i, 0)), pl.BlockSpec((tn, k_dim), lambda i, j: (j, 0)), o_spec],
        out_specs=o_spec,
        compiler_params=_params(("parallel", "parallel")),
    )(d, b, h)


def _rowwise(f, rows, params, row_out, acc_out, *, tm, name):
    length = rows[0].shape[0]
    tm = _row_tile(length, tm)
    nr, npar, nro = len(rows), len(params), len(row_out)

    def body(*refs):
        ins = [r[...] for r in refs[:nr + npar]]
        outs = refs[nr + npar:]
        r_o, a_o = f(*ins)
        for ref, val in zip(outs[:nro], r_o):
            ref[...] = val.astype(ref.dtype)
        i = pl.program_id(0)
        for ref, val in zip(outs[nro:], a_o):
            @pl.when(i == 0)
            def _(ref=ref, val=val):
                ref[...] = val.astype(ref.dtype)

            @pl.when(i > 0)
            def _(ref=ref, val=val):
                ref[...] += val.astype(ref.dtype)

    in_specs = ([pl.BlockSpec((tm, r.shape[1]), lambda i: (i, 0)) for r in rows]
                + [pl.BlockSpec(p.shape, lambda i: (0, 0)) for p in params])
    out_specs = ([pl.BlockSpec((tm, w), lambda i: (i, 0)) for w, _ in row_out]
                 + [pl.BlockSpec(s, lambda i: (0, 0)) for s, _ in acc_out])
    out_shape = ([jax.ShapeDtypeStruct((length, w), dt) for w, dt in row_out]
                 + [jax.ShapeDtypeStruct(s, dt) for s, dt in acc_out])
    res = pl.pallas_call(
        body, name=name, out_shape=out_shape, grid=(length // tm,),
        in_specs=in_specs, out_specs=out_specs,
        compiler_params=_params(("arbitrary",) if acc_out else ("parallel",)),
    )(*rows, *params)
    return res[:nro], res[nro:]


def _rw_fwd(f, rows, params, *, tm, name, out_dtypes=None):
    tm_ = _row_tile(rows[0].shape[0], tm)
    shapes = jax.eval_shape(f, *[jax.ShapeDtypeStruct((tm_, r.shape[1]), r.dtype) for r in rows],
                            *[jax.ShapeDtypeStruct(p.shape, p.dtype) for p in params])
    row_out = [(s.shape[1], s.dtype if out_dtypes is None else dt)
               for s, dt in zip(shapes, out_dtypes or shapes)]
    outs, _ = _rowwise(lambda *v: (f(*v), ()), rows, params, row_out, [], tm=tm, name=name)
    return outs


def _rw_bwd(f, rows, params, cots, *, row_grad, param_grad, tm, name, row_dtypes=None):
    nr, npar, nct = len(rows), len(params), len(cots)

    def g(*vals):
        prim = vals[:nr] + vals[nr + nct:]
        ct = vals[nr:nr + nct]
        _, vjp = jax.vjp(f, *prim)
        grads = vjp(tuple(ct))
        return (tuple(grads[i] for i in range(nr) if row_grad[i]),
                tuple(grads[nr + i] for i in range(npar) if param_grad[i]))

    widths = [rows[i].shape[1] for i in range(nr) if row_grad[i]]
    row_out = list(zip(widths, row_dtypes or [F32] * len(widths)))
    acc_out = [(params[i].shape, F32) for i in range(npar) if param_grad[i]]
    return _rowwise(g, list(rows) + list(cots), params, row_out, acc_out, tm=tm, name=name)


def _f_ln_res(x, h, g, b):
    pre = DN_ALPHA * x + h
    mu = jnp.mean(pre, axis=-1, keepdims=True)
    d = pre - mu
    var = jnp.mean(d * d, axis=-1, keepdims=True)
    return (d * lax.rsqrt(var + LN_EPS) * g + b,)


def _silu(t):
    return t * jax.nn.sigmoid(t)


def _f_gdn_qkv(c):
    a = _silu(c)
    outs = []
    for part, scale in ((0, HEAD_DIM ** -0.5), (1, 1.0)):
        heads = []
        for h in range(GDN_HEADS):
            t = a[:, part * D_MODEL + h * HEAD_DIM: part * D_MODEL + (h + 1) * HEAD_DIM]
            t = t * lax.rsqrt(jnp.sum(t * t, axis=-1, keepdims=True) + 1e-6)
            heads.append(t * scale if scale != 1.0 else t)
        outs.append(jnp.concatenate(heads, axis=-1))
    outs.append(a[:, 2 * D_MODEL:])
    return tuple(outs)


def _f_gdn_out(o, z, norm_g):
    heads = []
    for h in range(GDN_HEADS):
        t = o[:, h * HEAD_DIM:(h + 1) * HEAD_DIM]
        t = t * lax.rsqrt(jnp.mean(t * t, axis=-1, keepdims=True) + RMS_EPS) * norm_g
        heads.append(t)
    return (jnp.concatenate(heads, axis=-1) * _silu(z),)


def _f_attn(xq, kmem, vmem):
    heads = []
    for h in range(XA_HEADS):
        sl = slice(h * HEAD_DIM, (h + 1) * HEAD_DIM)
        s = lax.dot_general(xq[:, sl].astype(BF16), kmem[:, sl].astype(BF16),
                            (((1,), (1,)), ((), ())), preferred_element_type=F32) * (HEAD_DIM ** -0.5)
        m = lax.stop_gradient(jnp.max(s, axis=-1, keepdims=True))
        e = jnp.exp(s - m)
        p = e / jnp.sum(e, axis=-1, keepdims=True)
        heads.append(jnp.dot(p.astype(BF16), vmem[:, sl].astype(BF16), preferred_element_type=F32))
    return (jnp.concatenate(heads, axis=-1),)


def _f_s5_gelu(y, u, d):
    return (jax.nn.gelu(y + d * u),)


def _f_s5_gate(zg, t, b):
    return (zg * jax.nn.sigmoid(t + b),)


def _f_add(a, b):
    return (a + b,)


def _f_add4(a, b, c, d):
    return (((a + b.astype(F32)) + c.astype(F32)) + d.astype(F32),)


def _f_adamw(w, g, m, v):
    m = ADAM_B1 * m + (1.0 - ADAM_B1) * g
    v = ADAM_B2 * v + (1.0 - ADAM_B2) * jnp.square(g)
    m_hat = m / (1.0 - ADAM_B1 ** ADAM_STEP)
    v_hat = v / (1.0 - ADAM_B2 ** ADAM_STEP)
    delta = -ADAM_LR * (m_hat / (jnp.sqrt(v_hat) + ADAM_EPS) + ADAM_WD * w)
    return delta, m, v


def _conv_fwd(u, w, *, tm, name):
    length, chans = u.shape
    tm = min(tm, length)
    tc = _tile(chans, 1024)
    hb = tm // SUBLANES

    def body(cur_ref, prev_ref, w_ref, o_ref, buf):
        i = pl.program_id(1)
        buf[0:SUBLANES, :] = jnp.where(i > 0, prev_ref[...], 0.0)
        buf[SUBLANES:, :] = cur_ref[...]
        acc = buf[pl.ds(SUBLANES - 3, tm), :] * w_ref[0:1, :]
        for k in range(1, GDN_CONV):
            acc = acc + buf[pl.ds(SUBLANES - 3 + k, tm), :] * w_ref[k:k + 1, :]
        o_ref[...] = acc

    return pl.pallas_call(
        body, name=name, out_shape=jax.ShapeDtypeStruct(u.shape, F32),
        grid=(chans // tc, length // tm),
        in_specs=[pl.BlockSpec((tm, tc), lambda j, i: (i, j)),
                  pl.BlockSpec((SUBLANES, tc), lambda j, i: (jnp.maximum(i * hb - 1, 0), j)),
                  pl.BlockSpec((GDN_CONV, tc), lambda j, i: (0, j))],
        out_specs=pl.BlockSpec((tm, tc), lambda j, i: (i, j)),
        scratch_shapes=[pltpu.VMEM((tm + SUBLANES, tc), F32)],
        compiler_params=_params(("parallel", "parallel")),
    )(u, u, w)


def _conv_bwd(u, w, dc, *, tm, name):
    length, chans = u.shape
    tm = min(tm, length)
    tc = _tile(chans, 1024)
    hb = tm // SUBLANES
    last = length // tm - 1

    def body(u_ref, uprev_ref, dc_ref, dcnext_ref, w_ref, du_ref, dw_ref, ubuf, dbuf):
        i = pl.program_id(1)
        ubuf[0:SUBLANES, :] = jnp.where(i > 0, uprev_ref[...], 0.0)
        ubuf[SUBLANES:, :] = u_ref[...]
        dbuf[0:tm, :] = dc_ref[...]
        dbuf[tm:, :] = jnp.where(i < last, dcnext_ref[...], 0.0)
        dcv = dc_ref[...]
        du = dbuf[pl.ds(3, tm), :] * w_ref[0:1, :]
        rows = [jnp.sum(dcv * ubuf[pl.ds(SUBLANES - 3, tm), :], axis=0, keepdims=True)]
        for k in range(1, GDN_CONV):
            du = du + dbuf[pl.ds(3 - k, tm), :] * w_ref[k:k + 1, :]
            rows.append(jnp.sum(dcv * ubuf[pl.ds(SUBLANES - 3 + k, tm), :], axis=0, keepdims=True))
        du_ref[...] = du.astype(du_ref.dtype)
        dwv = jnp.concatenate(rows, axis=0)

        @pl.when(i == 0)
        def _():
            dw_ref[...] = dwv

        @pl.when(i > 0)
        def _():
            dw_ref[...] += dwv

    return pl.pallas_call(
        body, name=name,
        out_shape=(jax.ShapeDtypeStruct(u.shape, BF16), jax.ShapeDtypeStruct((GDN_CONV, chans), F32)),
        grid=(chans // tc, length // tm),
        in_specs=[pl.BlockSpec((tm, tc), lambda j, i: (i, j)),
                  pl.BlockSpec((SUBLANES, tc), lambda j, i: (jnp.maximum(i * hb - 1, 0), j)),
                  pl.BlockSpec((tm, tc), lambda j, i: (i, j)),
                  pl.BlockSpec((SUBLANES, tc), lambda j, i: (jnp.minimum((i + 1) * hb, (last + 1) * hb - 1), j)),
                  pl.BlockSpec((GDN_CONV, tc), lambda j, i: (0, j))],
        out_specs=(pl.BlockSpec((tm, tc), lambda j, i: (i, j)),
                   pl.BlockSpec((GDN_CONV, tc), lambda j, i: (0, j))),
        scratch_shapes=[pltpu.VMEM((tm + SUBLANES, tc), F32), pltpu.VMEM((tm + SUBLANES, tc), F32)],
        compiler_params=_params(("parallel", "arbitrary")),
    )(u, u, dc, dc, w)


def _dot(a, b, dims, precision=None):
    if precision is None:
        a, b = a.astype(BF16), b.astype(BF16)
    return lax.dot_general(a, b, (dims, ((), ())), preferred_element_type=F32, precision=precision)


def _dot3(a, b, dims):
    ah, bh = a.astype(BF16), b.astype(BF16)
    al, bl = (a - ah.astype(F32)).astype(BF16), (b - bh.astype(F32)).astype(BF16)

    def d(x, y):
        return lax.dot_general(x, y, (dims, ((), ())), preferred_element_type=F32)

    return d(ah, bh) + (d(ah, bl) + d(al, bh))


NN = ((1,), (0,))
NT = ((1,), (1,))
TN = ((0,), (0,))
HI = lax.Precision.HIGHEST


def _hmap(f, *lists):
    return [f(*t) for t in zip(*lists)]


@jax.custom_vjp
def _unit_lower_inverse(a):
    c = a[0].shape[0]
    eye = (lax.broadcasted_iota(jnp.int32, (c, c), 0) == lax.broadcasted_iota(jnp.int32, (c, c), 1)).astype(F32)
    p = _hmap(lambda x: -x, a)
    t = _hmap(lambda x: eye + x, p)
    for _ in range(int(math.log2(c)) - 1):
        p = _hmap(lambda x: _dot3(x, x, NN), p)
        t = _hmap(lambda x, y: x + _dot3(x, y, NN), t, p)
    return t


def _uli_fwd(a):
    t = _unit_lower_inverse(a)
    return t, t


def _uli_bwd(t, dt):
    left = _hmap(lambda x, y: _dot3(x, y, TN), t, dt)
    return (_hmap(lambda x, y: -_dot3(x, y, NT), left, t),)


_unit_lower_inverse.defvjp(_uli_fwd, _uli_bwd)


@jax.custom_vjp
def _known_inverse(a, t):
    return t


_known_inverse.defvjp(lambda a, t: (t, t),
                      lambda t, dt: (_uli_bwd(t, dt)[0], _hmap(jnp.zeros_like, t)))


def _gdn_chunk(q, k, v, bl, al, a_log, dt_bias, state, t_known=None):
    c = q[0].shape[0]
    row = lax.broadcasted_iota(jnp.int32, (c, c), 0)
    col = lax.broadcasted_iota(jnp.int32, (c, c), 1)
    causal = row >= col
    strict = row > col
    eye = (row == col).astype(F32)
    beta = _hmap(jax.nn.sigmoid, bl)
    g = _hmap(lambda a_, l_, d_: -jnp.exp(a_) * jax.nn.softplus(l_ + d_), a_log, al, dt_bias)
    g_r = _hmap(lambda x: jnp.sum(eye * x, axis=0, keepdims=True), g)
    gc = _hmap(lambda x: jnp.sum(jnp.where(causal, x, 0.0), axis=1, keepdims=True), g_r)
    gc_r = _hmap(lambda x: jnp.sum(jnp.where(row <= col, x, 0.0), axis=0, keepdims=True), g)
    decay = _hmap(lambda x, y: jnp.where(causal, jnp.exp(jnp.where(causal, x - y, 0.0)), 0.0), gc, gc_r)
    e_gc = _hmap(jnp.exp, gc)
    kb = _hmap(jnp.multiply, k, beta)
    vb = _hmap(jnp.multiply, v, beta)
    a_mat = _hmap(lambda x, y, d: jnp.where(strict, _dot(x, y, NT) * d, 0.0), kb, k, decay)
    t_inv = _unit_lower_inverse(a_mat) if t_known is None else _known_inverse(a_mat, t_known)
    u_blk = _hmap(lambda t, x: _dot(t, x, NN), t_inv, vb)
    w_blk = _hmap(lambda t, x, e: _dot(t, x * e, NN), t_inv, kb, e_gc)
    v_new = _hmap(lambda u, w, s: u - _dot(w, s, NN), u_blk, w_blk, state)
    attn = _hmap(lambda x, y, d: _dot(x, y, NT) * d, q, k, decay)
    o_state = _hmap(lambda x, e, s: _dot(x * e, s, NN), q, e_gc, state)
    o = _hmap(lambda base, at, vn: base + _dot(at, vn, NN), o_state, attn, v_new)
    g_last = _hmap(lambda x: jnp.sum(x, axis=0, keepdims=True), g)
    k_dec = _hmap(lambda x, gl, c_: x * jnp.exp(gl - c_), k, g_last, gc)
    new_state = _hmap(lambda s, gl, kd, vn: s * jnp.exp(gl) + _dot(kd, vn, TN), state, g_last, k_dec, v_new)
    return (o, new_state, t_inv) if t_known is None else (o, new_state)


def _gdn_operands(q_ref, k_ref, v_ref, bav, alog_ref, dtb_ref):
    hs = range(GDN_HEADS)
    cols = [slice(h * HEAD_DIM, (h + 1) * HEAD_DIM) for h in hs]
    return ([q_ref[:, sl] for sl in cols], [k_ref[:, sl] for sl in cols], [v_ref[:, sl] for sl in cols],
            [bav[:, h:h + 1] for h in hs], [bav[:, h + GDN_HEADS:h + GDN_HEADS + 1] for h in hs],
            [alog_ref[h:h + 1, 0:1] for h in hs], [dtb_ref[h:h + 1, 0:1] for h in hs])


def _gdn_scan_fwd(q, k, v, ba, a_log, dt_bias, *, name):
    length = q.shape[0]
    n = length // GDN_CHUNK
    c = GDN_CHUNK

    def body(q_ref, k_ref, v_ref, ba_ref, alog_ref, dtb_ref, o_ref, s_ref, t_ref, state):
        i = pl.program_id(0)

        @pl.when(i == 0)
        def _():
            state[...] = jnp.zeros_like(state)

        bav = ba_ref[...]
        heads = [slice(h * HEAD_DIM, (h + 1) * HEAD_DIM) for h in range(GDN_HEADS)]
        s_in = [state[h] for h in range(GDN_HEADS)]
        o, s_out, t_inv = _gdn_chunk(*_gdn_operands(q_ref, k_ref, v_ref, bav, alog_ref, dtb_ref), s_in)
        for h, sl in enumerate(heads):
            s_ref[h] = s_in[h]
            t_ref[h] = t_inv[h]
            o_ref[:, sl] = o[h]
            state[h] = s_out[h]

    row_spec = pl.BlockSpec((c, D_MODEL), lambda i: (i, 0))
    small = pl.BlockSpec((GDN_HEADS, LANES), lambda i: (0, 0))
    return pl.pallas_call(
        body, name=name,
        out_shape=(jax.ShapeDtypeStruct((length, D_MODEL), F32),
                   jax.ShapeDtypeStruct((n, GDN_HEADS, HEAD_DIM, HEAD_DIM), F32),
                   jax.ShapeDtypeStruct((n, GDN_HEADS, c, c), F32)),
        grid=(n,),
        in_specs=[row_spec, row_spec, row_spec, pl.BlockSpec((c, LANES), lambda i: (i, 0)), small, small],
        out_specs=(row_spec, pl.BlockSpec((None, GDN_HEADS, HEAD_DIM, HEAD_DIM), lambda i: (i, 0, 0, 0)),
                   pl.BlockSpec((None, GDN_HEADS, c, c), lambda i: (i, 0, 0, 0))),
        scratch_shapes=[pltpu.VMEM((GDN_HEADS, HEAD_DIM, HEAD_DIM), F32)],
        compiler_params=_params(("arbitrary",)),
    )(q, k, v, ba, a_log, dt_bias)


def _gdn_scan_bwd(q, k, v, ba, a_log, dt_bias, states, inverses, do, *, name):
    length = q.shape[0]
    n = length // GDN_CHUNK
    c = GDN_CHUNK

    def body(q_ref, k_ref, v_ref, ba_ref, alog_ref, dtb_ref, s_ref, t_ref, do_ref,
             dq_ref, dk_ref, dv_ref, dba_ref, dalog_ref, ddtb_ref, dstate):
        i = pl.program_id(0)

        @pl.when(i == 0)
        def _():
            dstate[...] = jnp.zeros_like(dstate)
            dalog_ref[...] = jnp.zeros_like(dalog_ref)
            ddtb_ref[...] = jnp.zeros_like(ddtb_ref)

        bav = ba_ref[...]
        lane = lax.broadcasted_iota(jnp.int32, (c, LANES), 1)
        sub8 = lax.broadcasted_iota(jnp.int32, (GDN_HEADS, LANES), 0)
        lane8 = lax.broadcasted_iota(jnp.int32, (GDN_HEADS, LANES), 1)
        slab = jnp.zeros((c, LANES), F32)
        dalog_all = jnp.zeros((GDN_HEADS, LANES), F32)
        ddtb_all = jnp.zeros((GDN_HEADS, LANES), F32)
        heads = [slice(h * HEAD_DIM, (h + 1) * HEAD_DIM) for h in range(GDN_HEADS)]
        ds_in = [dstate[h] for h in range(GDN_HEADS)]
        s_in = [s_ref[h] for h in range(GDN_HEADS)]
        t_known = [t_ref[h] for h in range(GDN_HEADS)]
        _, vjp = jax.vjp(functools.partial(_gdn_chunk, t_known=t_known),
                         *_gdn_operands(q_ref, k_ref, v_ref, bav, alog_ref, dtb_ref), s_in)
        dq, dk, dv, dbl, dal, dalog, ddtb, ds = vjp(([do_ref[:, sl] for sl in heads], ds_in))
        for h, sl in enumerate(heads):
            dq_ref[:, sl] = dq[h]
            dk_ref[:, sl] = dk[h]
            dv_ref[:, sl] = dv[h]
            dstate[h] = ds[h]
            slab = slab + jnp.where(lane == h, dbl[h], 0.0) + jnp.where(lane == h + GDN_HEADS, dal[h], 0.0)
            here = (sub8 == h) & (lane8 == 0)
            dalog_all = dalog_all + jnp.where(here, dalog[h], 0.0)
            ddtb_all = ddtb_all + jnp.where(here, ddtb[h], 0.0)
        dba_ref[...] = slab
        dalog_ref[...] += dalog_all
        ddtb_ref[...] += ddtb_all

    row_spec = pl.BlockSpec((c, D_MODEL), lambda i: (n - 1 - i, 0))
    small = pl.BlockSpec((GDN_HEADS, LANES), lambda i: (0, 0))
    return pl.pallas_call(
        body, name=name,
        out_shape=(jax.ShapeDtypeStruct((length, D_MODEL), F32),) * 3
        + (jax.ShapeDtypeStruct((length, LANES), F32),
           jax.ShapeDtypeStruct((GDN_HEADS, LANES), F32), jax.ShapeDtypeStruct((GDN_HEADS, LANES), F32)),
        grid=(n,),
        in_specs=[row_spec, row_spec, row_spec,
                  pl.BlockSpec((c, LANES), lambda i: (n - 1 - i, 0)), small, small,
                  pl.BlockSpec((None, GDN_HEADS, HEAD_DIM, HEAD_DIM), lambda i: (n - 1 - i, 0, 0, 0)),
                  pl.BlockSpec((None, GDN_HEADS, c, c), lambda i: (n - 1 - i, 0, 0, 0)),
                  row_spec],
        out_specs=(row_spec, row_spec, row_spec,
                   pl.BlockSpec((c, LANES), lambda i: (n - 1 - i, 0)), small, small),
        scratch_shapes=[pltpu.VMEM((GDN_HEADS, HEAD_DIM, HEAD_DIM), F32)],
        compiler_params=_params(("arbitrary",)),
    )(q, k, v, ba, a_log, dt_bias, states, inverses, do)


S5_W = S5_T * LANES
S5_S = 2 * 8 * S5_STATE
S5_SH = S5_S // 2


def _iota2(shape):
    return lax.broadcasted_iota(jnp.int32, shape, 0), lax.broadcasted_iota(jnp.int32, shape, 1)


def _s5_rep_t(t, dtype):
    row, col = _iota2((S5_T * S5_GROUP, LANES))
    return ((jnp.right_shift(row, 4) == t) & (jnp.bitwise_and(row, 15) == jnp.bitwise_and(col, 15))).astype(dtype)


def _s5_rep_state(dtype):
    row, col = _iota2((2 * S5_STATE, S5_S))
    return ((jnp.right_shift(row, 6) == jnp.right_shift(col, 9))
            & (jnp.bitwise_and(row, 63) == jnp.bitwise_and(col, 63))).astype(dtype)


def _s5_masks():
    row, col = _iota2((LANES, LANES))
    m_ab = jnp.right_shift(row, 4) == jnp.right_shift(col, 4)
    row, col = _iota2((S5_S, LANES))
    m_e = jnp.bitwise_and(jnp.right_shift(row, 6), 7) == jnp.right_shift(col, 4)
    row, col = _iota2((LANES, S5_S))
    m_f = jnp.right_shift(row, 4) == jnp.bitwise_and(jnp.right_shift(col, 6), 7)
    return m_ab, m_e, m_f


def _s5_expand(kx_ref, ec_ref, fc_ref, kb_scr, e_scr, f_scr):
    m_ab, m_e, m_f = _s5_masks()
    kx = kx_ref[...].astype(BF16)
    ec = ec_ref[...].astype(BF16)
    rep_state = _s5_rep_state(BF16)
    for t in range(S5_T):
        rep = _s5_rep_t(t, BF16)
        cols = slice(t * LANES, (t + 1) * LANES)
        kb_scr[t] = jnp.where(m_ab, jnp.dot(kx, rep, preferred_element_type=F32), 0.0).astype(BF16)
        e_scr[:, cols] = jnp.where(m_e, jnp.dot(ec, rep, preferred_element_type=F32), 0.0).astype(BF16)
        f_scr[cols, :] = jnp.where(m_f, jnp.dot(fc_ref[t].astype(BF16), rep_state, preferred_element_type=F32),
                                   0.0).astype(BF16)


def _s5_token_rows(ref, n):
    return [ref[pl.ds(t, n, stride=S5_T), :].astype(BF16) for t in range(S5_T)]


def _s5_scan_fwd(u, kx, ec, fc, at, *, name):
    length = u.shape[0]
    n = length // S5_T
    assert n % SUBLANES == 0

    def body(u_ref, kx_ref, ec_ref, fc_ref, at_ref, y_ref, h_ref, kb_scr, e_scr, f_scr, g_scr):
        _s5_expand(kx_ref, ec_ref, fc_ref, kb_scr, e_scr, f_scr)
        us = _s5_token_rows(u_ref, n)
        g_scr[...] = jnp.dot(jnp.concatenate(us, axis=1), f_scr[...], preferred_element_type=F32)
        ar, ai = at_ref[:, :S5_SH], at_ref[:, S5_SH:]

        def step(blk, h):
            base = pl.multiple_of(blk * SUBLANES, SUBLANES)
            g8 = g_scr[pl.ds(base, SUBLANES), :]
            rows = []
            for r in range(SUBLANES):
                rows.append(h)
                hr, hi = h[:, :S5_SH], h[:, S5_SH:]
                h = jnp.concatenate([ar * hr - ai * hi, ar * hi + ai * hr], axis=1) + g8[r:r + 1, :]
            h_ref[pl.ds(base, SUBLANES), :] = jnp.concatenate(rows, axis=0)
            return h

        lax.fori_loop(0, n // SUBLANES, step, jnp.zeros((1, S5_S), F32))
        hb = h_ref[...].astype(BF16)
        for t in range(S5_T):
            acc = jnp.dot(hb, e_scr[:, t * LANES:(t + 1) * LANES], preferred_element_type=F32)
            for s in range(t + 1):
                acc = acc + jnp.dot(us[s], kb_scr[t - s], preferred_element_type=F32)
            y_ref[pl.ds(t, n, stride=S5_T), :] = acc

    return pl.pallas_call(
        body, name=name,
        out_shape=(jax.ShapeDtypeStruct((length, D_MODEL), F32), jax.ShapeDtypeStruct((S5_TILES, n, S5_S), F32)),
        grid=(S5_TILES,),
        in_specs=[pl.BlockSpec((length, LANES), lambda k: (0, k)), _s5_spec(LANES, S5_T * S5_GROUP),
                  _s5_spec(S5_S, S5_T * S5_GROUP), _s5_spec(S5_T, LANES, LANES), _s5_spec(1, S5_S)],
        out_specs=(pl.BlockSpec((length, LANES), lambda k: (0, k)), _s5_spec(n, S5_S)),
        scratch_shapes=[pltpu.VMEM((S5_T, LANES, LANES), BF16), pltpu.VMEM((S5_S, S5_W), BF16),
                        pltpu.VMEM((S5_W, S5_S), BF16), pltpu.VMEM((n, S5_S), F32)],
        compiler_params=_params(("parallel",)),
    )(u, kx, ec, fc, at)


def _s5_spec(*tail):
    return pl.BlockSpec((None,) + tail, lambda k: (k,) + (0,) * len(tail))


def _s5_scan_bwd(dy, kx, ec, fc, at, hs, *, name):
    length = dy.shape[0]
    n = length // S5_T

    def body(dy_ref, kx_ref, ec_ref, fc_ref, at_ref, h_ref, du_ref, dg_ref, dat_ref, kb_scr, e_scr, f_scr, dh_scr):
        _s5_expand(kx_ref, ec_ref, fc_ref, kb_scr, e_scr, f_scr)
        dys = _s5_token_rows(dy_ref, n)
        dh_scr[...] = _dot(jnp.concatenate(dys, axis=1), e_scr[...], NT)
        ar, ai = at_ref[:, :S5_SH], at_ref[:, S5_SH:]

        def step(it, carry):
            cy, dat = carry
            base = pl.multiple_of((n // SUBLANES - 1 - it) * SUBLANES, SUBLANES)
            dh8 = dh_scr[pl.ds(base, SUBLANES), :]
            h8 = h_ref[pl.ds(base, SUBLANES), :]
            rows = [None] * SUBLANES
            for r in reversed(range(SUBLANES)):
                rows[r] = cy
                cr, ci = cy[:, :S5_SH], cy[:, S5_SH:]
                hr, hi = h8[r:r + 1, :S5_SH], h8[r:r + 1, S5_SH:]
                dat = dat + jnp.concatenate([cr * hr + ci * hi, ci * hr - cr * hi], axis=1)
                cy = dh8[r:r + 1, :] + jnp.concatenate([ar * cr + ai * ci, ar * ci - ai * cr], axis=1)
            dg_ref[pl.ds(base, SUBLANES), :] = jnp.concatenate(rows, axis=0)
            return cy, dat

        zero = jnp.zeros((1, S5_S), F32)
        _, dat = lax.fori_loop(0, n // SUBLANES, step, (zero, zero))
        dat_ref[...] = dat
        dgb = dg_ref[...].astype(BF16)
        for s in range(S5_T):
            acc = _dot(dgb, f_scr[s * LANES:(s + 1) * LANES, :], NT)
            for t in range(s, S5_T):
                acc = acc + _dot(dys[t], kb_scr[t - s], NT)
            du_ref[pl.ds(s, n, stride=S5_T), :] = acc

    row_spec = pl.BlockSpec((length, LANES), lambda k: (0, k))
    return pl.pallas_call(
        body, name=name,
        out_shape=(jax.ShapeDtypeStruct((length, D_MODEL), F32), jax.ShapeDtypeStruct((S5_TILES, n, S5_S), F32),
                   jax.ShapeDtypeStruct((S5_TILES, 1, S5_S), F32)),
        grid=(S5_TILES,),
        in_specs=[row_spec, _s5_spec(LANES, S5_T * S5_GROUP), _s5_spec(S5_S, S5_T * S5_GROUP),
                  _s5_spec(S5_T, LANES, LANES), _s5_spec(1, S5_S), _s5_spec(n, S5_S)],
        out_specs=(row_spec, _s5_spec(n, S5_S), _s5_spec(1, S5_S)),
        scratch_shapes=[pltpu.VMEM((S5_T, LANES, LANES), BF16), pltpu.VMEM((S5_S, S5_W), BF16),
                        pltpu.VMEM((S5_W, S5_S), BF16), pltpu.VMEM((n, S5_S), F32)],
        compiler_params=_params(("parallel",)),
    )(dy, kx, ec, fc, at, hs)


def _s5_operator_grads(dy, u, hs, dg, *, name):
    length = u.shape[0]
    n = length // S5_T

    def body(dy_ref, u_ref, h_ref, dg_ref, dkx_ref, dec_ref, dfc_ref):
        dys = _s5_token_rows(dy_ref, n)
        us = _s5_token_rows(u_ref, n)
        ucat = jnp.concatenate(us, axis=1)
        m_ab, m_e, m_f = _s5_masks()
        hb = h_ref[...].astype(BF16)
        dgb = dg_ref[...].astype(BF16)
        lane = lax.broadcasted_iota(jnp.int32, (1, LANES), 1)
        lane_group = jnp.right_shift(lane, 4)

        def own_block(x, mask):
            x = jnp.where(mask, x, 0.0)
            for shift in (64, 32, 16):
                x = x + pltpu.roll(x, shift, 1)
            return x

        def place(halves, t, x):
            halves[t // 8] = jnp.where(lane_group == t % 8, x, halves[t // 8])

        dkb = [jnp.zeros((LANES, LANES), F32) for _ in range(S5_T)]
        dec = [jnp.zeros((S5_S, LANES), F32) for _ in range(2)]
        for t in range(S5_T):
            d_t = _dot(ucat, dys[t], TN)
            for s in range(t + 1):
                dkb[t - s] = dkb[t - s] + d_t[s * LANES:(s + 1) * LANES, :]
            place(dec, t, own_block(_dot(hb, dys[t], TN), m_e))
            wide = jnp.where(m_f, _dot(us[t], dgb, TN), 0.0)
            parts = []
            for r in range(2):
                acc = wide[:, r * S5_SH:r * S5_SH + LANES]
                for q in range(1, S5_SH // LANES):
                    acc = acc + wide[:, r * S5_SH + q * LANES:r * S5_SH + (q + 1) * LANES]
                parts.append(acc + pltpu.roll(acc, S5_STATE, 1))
            dfc_ref[t] = jnp.where(lane < S5_STATE, parts[0], parts[1])
        dkx = [jnp.zeros((LANES, LANES), F32) for _ in range(2)]
        for t in range(S5_T):
            place(dkx, t, own_block(dkb[t], m_ab))
        dkx_ref[...] = jnp.concatenate(dkx, axis=1)
        dec_ref[...] = jnp.concatenate(dec, axis=1)

    row_spec = pl.BlockSpec((length, LANES), lambda k: (0, k))
    outs = (_s5_spec(LANES, S5_T * S5_GROUP), _s5_spec(S5_S, S5_T * S5_GROUP), _s5_spec(S5_T, LANES, LANES))
    return pl.pallas_call(
        body, name=name,
        out_shape=(jax.ShapeDtypeStruct((S5_TILES, LANES, S5_T * S5_GROUP), F32),
                   jax.ShapeDtypeStruct((S5_TILES, S5_S, S5_T * S5_GROUP), F32),
                   jax.ShapeDtypeStruct((S5_TILES, S5_T, LANES, LANES), F32)),
        grid=(S5_TILES,),
        in_specs=[row_spec, row_spec, _s5_spec(n, S5_S), _s5_spec(n, S5_S)],
        out_specs=outs,
        compiler_params=_params(("parallel",)),
    )(dy, u, hs, dg)


def _s5_prep(a_re, a_im, b_re, b_im, c_re, c_im, log_dt):
    t_len, tiles = S5_T, S5_TILES
    dt = jnp.exp(log_dt)[:, None]
    mag = jnp.exp(a_re * dt)
    ab_re, ab_im = mag * jnp.cos(a_im * dt), mag * jnp.sin(a_im * dt)
    den = jnp.square(a_re) + jnp.square(a_im)
    n_re, n_im = ab_re - 1.0, ab_im
    f_re = (n_re * a_re + n_im * a_im) / den
    f_im = (n_im * a_re - n_re * a_im) / den
    bb_re = f_re[..., None] * b_re - f_im[..., None] * b_im
    bb_im = f_re[..., None] * b_im + f_im[..., None] * b_re

    def powers(exponents):
        e = exponents[:, None, None]
        m = jnp.exp(e * (a_re * dt))
        return m * jnp.cos(e * (a_im * dt)), m * jnp.sin(e * (a_im * dt))

    p_re, p_im = powers(jnp.arange(t_len + 1, dtype=F32))
    rev_re, rev_im = powers((t_len - 1) - jnp.arange(t_len, dtype=F32))
    ca_re = c_re[None] * p_re[:, :, None, :] - c_im[None] * p_im[:, :, None, :]
    ca_im = c_re[None] * p_im[:, :, None, :] + c_im[None] * p_re[:, :, None, :]
    lag = (jnp.einsum('tgip,gpj->tgij', ca_re[:t_len], bb_re, precision=HI)
           - jnp.einsum('tgip,gpj->tgij', ca_im[:t_len], bb_im, precision=HI))
    kx = lag.reshape(t_len, tiles, 8, S5_GROUP, S5_GROUP).transpose(1, 2, 4, 0, 3)
    kx = kx.reshape(tiles, LANES, t_len * S5_GROUP)
    e_st = jnp.stack([ca_re[1:], -ca_im[1:]])
    e_st = e_st.reshape(2, t_len, tiles, 8, S5_GROUP, S5_STATE).transpose(2, 0, 3, 5, 1, 4)
    ec = e_st.reshape(tiles, S5_S, t_len * S5_GROUP)
    ab_b = jnp.stack([rev_re[..., None] * bb_re[None] - rev_im[..., None] * bb_im[None],
                      rev_re[..., None] * bb_im[None] + rev_im[..., None] * bb_re[None]])
    ab_b = ab_b.reshape(2, t_len, tiles, 8, S5_STATE, S5_GROUP).transpose(2, 1, 3, 5, 0, 4)
    fc = ab_b.reshape(tiles, t_len, LANES, 2 * S5_STATE)
    a_t = jnp.stack([p_re[t_len], p_im[t_len]]).reshape(2, tiles, 8 * S5_STATE).transpose(1, 0, 2)
    return kx, ec, fc, a_t.reshape(tiles, 1, S5_S)


TM_ROW = 512


def _gdn_fwd(x, w, tag):
    qkv = _mm(x, w["wqkv"], name="gdn_proj_qkv")
    z = _mm(x, w["wz"], name="gdn_proj_z")
    ba = _mm(x, w["wba"], name="gdn_proj_ba")
    cv = _conv_fwd(qkv, w["conv_w"], tm=TM_ROW, name="gdn_conv")
    q, k, v = _rw_fwd(_f_gdn_qkv, [cv], [], tm=TM_ROW, name="gdn_qkv")
    o, states, inverses = _gdn_scan_fwd(q, k, v, ba, w["a_log8"], w["dt_bias8"], name="gdn_scan")
    (mix,) = _rw_fwd(_f_gdn_out, [o, z], [w["norm_g"]], tm=TM_ROW, name="gdn_out", out_dtypes=[BF16])
    return mix, (qkv, z, ba, cv, q, k, v, states, inverses, o)


def _gdn_bwd(x, w, saved, dmix, dx_acc, token=None):
    qkv, z, ba, cv, q, k, v, states, inverses, o = saved
    norm_g = w["norm_g"] if token is None else w["norm_g"] + token[0, 0]
    (do, dz), (dnorm_g,) = _rw_bwd(_f_gdn_out, [o, z], [norm_g], [dmix], row_grad=[1, 1], param_grad=[1],
                                   tm=TM_ROW, name="gdn_out_bwd", row_dtypes=[F32, BF16])
    dq, dk, dv, dba, dalog, ddtb = _gdn_scan_bwd(q, k, v, ba, w["a_log8"], w["dt_bias8"], states, inverses, do,
                                                  name="gdn_scan_bwd")
    (dcv,), _ = _rw_bwd(_f_gdn_qkv, [cv], [], [dq, dk, dv], row_grad=[1], param_grad=[], tm=TM_ROW,
                        name="gdn_qkv_bwd")
    dqkv, dconv_w = _conv_bwd(qkv, w["conv_w"], dcv, tm=TM_ROW, name="gdn_conv_bwd")
    dx = _mm(dqkv, w["wqkv"], tb=True, acc=dx_acc, name="gdn_dx_qkv")
    dx = _mm(dz, w["wz"], tb=True, acc=dx, name="gdn_dx_z")
    dx = _mm(dba, w["wba"], tb=True, acc=dx, name="gdn_dx_ba")
    grads = dict(wqkv=_mm(x, dqkv, ta=True, name="gdn_dw_qkv"), wz=_mm(x, dz, ta=True, name="gdn_dw_z"),
                 wba=_mm(x, dba, ta=True, name="gdn_dw_ba"), conv_w=dconv_w,
                 a_log=dalog[:, 0], dt_bias=ddtb[:, 0], norm_g=dnorm_g[0])
    return dx, grads


def _s5_fwd(x, w, tag):
    u = _mm(x, w["wu"], name="s5_proj_u")
    y, hs = _s5_scan_fwd(u, w["kx"], w["ec"], w["fc"], w["a_t"], name="s5_scan")
    (zg,) = _rw_fwd(_f_s5_gelu, [y, u], [w["d"]], tm=TM_ROW, name="s5_gelu")
    t = _mm(zg, w["w_glu"], name="s5_glu")
    (mix,) = _rw_fwd(_f_s5_gate, [zg, t], [w["b_glu"]], tm=TM_ROW, name="s5_gate", out_dtypes=[BF16])
    return mix, (u, hs, y, zg, t)


def _s5_bwd(x, w, saved, dmix, dx_acc, token=None):
    u, hs, y, zg, t = saved
    b_glu = w["b_glu"] if token is None else w["b_glu"] + token[0, 0]
    (dzg, dt), (db_glu,) = _rw_bwd(_f_s5_gate, [zg, t], [b_glu], [dmix], row_grad=[1, 1], param_grad=[1],
                                   tm=TM_ROW, name="s5_gate_bwd", row_dtypes=[F32, BF16])
    dzg = _mm(dt, w["w_glu"], tb=True, acc=dzg, name="s5_dzg")
    dw_glu = _mm(zg, dt, ta=True, name="s5_dw_glu")
    (dy, du), (dd,) = _rw_bwd(_f_s5_gelu, [y, u], [w["d"]], [dzg], row_grad=[1, 1], param_grad=[1],
                              tm=TM_ROW, name="s5_gelu_bwd")
    du_scan, dg, dat = _s5_scan_bwd(dy, w["kx"], w["ec"], w["fc"], w["a_t"], hs, name="s5_scan_bwd")
    dkx, dec, dfc = _s5_operator_grads(dy, u, hs, dg, name="s5_operator_grads")
    (du,) = _rw_fwd(_f_add, [du, du_scan], [], tm=TM_ROW, name="s5_du_add", out_dtypes=[BF16])
    d_a_re, d_a_im, d_b_re, d_b_im, d_c_re, d_c_im, d_log_dt = w["prep_vjp"]((dkx, dec, dfc, dat))
    dx = _mm(du, w["wu"], tb=True, acc=dx_acc, name="s5_dx_u")
    grads = dict(wu=_mm(x, du, ta=True, name="s5_dw_u"), w_glu=dw_glu, b_glu=db_glu[0], d=dd[0],
                 a_re=d_a_re, a_im=d_a_im, b_re=d_b_re, b_im=d_b_im, c_re=d_c_re, c_im=d_c_im, log_dt=d_log_dt)
    return dx, grads


def _ln_res_both(x, h, g, b):
    (y,) = _f_ln_res(x, h, g, b)
    return y, y


def _layer_fwd(x, xb, mem, w, is_gdn):
    mix, msave = (_gdn_fwd if is_gdn else _s5_fwd)(xb, w, "")
    xq = _mm(xb, w["wxq"], name="proj_xq")
    kv = _mm(mem, w["wkv"], name="mem_kv")
    kmem, vmem = kv[:, :XA_DIM], kv[:, XA_DIM:]
    (cross,) = _rw_fwd(_f_attn, [xq], [kmem, vmem], tm=TM_ROW, name="attn", out_dtypes=[BF16])
    h = _mm(mix, w["wo_mix"], name="wo_mix")
    h = _mm(cross, w["wo_cross"], acc=h, name="wo_cross")
    x1, x1b = _rw_fwd(_ln_res_both, [x, h], [w["ln1_g"], w["ln1_b"]], tm=TM_ROW, name="ln_res",
                      out_dtypes=[F32, BF16])
    hm, act = _mm_relu2(x1b, w["w1"], name="mlp_up")
    f = _mm(act, w["w2"], name="mlp_down")
    x2, x2b = _rw_fwd(_ln_res_both, [x1, f], [w["ln2_g"], w["ln2_b"]], tm=TM_ROW, name="ln_res",
                      out_dtypes=[F32, BF16])
    return x2, x2b, (x, xb, msave, xq, kmem, vmem, mix, cross, h, x1, x1b, hm, act, f)


def _layer_bwd(mem, w, is_gdn, saved, dx2, token=None, before_mixer=None):
    x, xb, msave, xq, kmem, vmem, mix, cross, h, x1, x1b, hm, act, f = saved
    ln2_g = w["ln2_g"] if token is None else w["ln2_g"] + token[0, 0]
    (dx1, df), (dg2, db2) = _rw_bwd(_f_ln_res, [x1, f], [ln2_g, w["ln2_b"]], [dx2], row_grad=[1, 1],
                                    param_grad=[1, 1], tm=TM_ROW, name="ln_res_bwd", row_dtypes=[F32, BF16])
    dhm = _mm_relu2_grad(df, w["w2"], hm, name="mlp_dhm")
    dw2 = _mm(act, df, ta=True, name="mlp_dw2")
    dx1 = _mm(dhm, w["w1"], tb=True, acc=dx1, name="mlp_dx")
    dw1 = _mm(x1b, dhm, ta=True, out_blocks=N_CHIPS, name="mlp_dw1")
    (dx, dh), (dg1, db1) = _rw_bwd(_f_ln_res, [x, h], [w["ln1_g"], w["ln1_b"]], [dx1], row_grad=[1, 1],
                                   param_grad=[1, 1], tm=TM_ROW, name="ln_res_bwd", row_dtypes=[F32, BF16])
    dmix =_mm(dh, w["wo_mix"], tb=True, name="wo_dmix")
    dcross = _mm(dh, w["wo_cross"], tb=True, name="wo_dcross")
    dwo = jnp.concatenate([_mm(mix, dh, ta=True, name="wo_dw_mix"), _mm(cross, dh, ta=True, name="wo_dw_cross")], 0)
    (dxq,), (dkmem, dvmem) = _rw_bwd(_f_attn, [xq], [kmem, vmem], [dcross], row_grad=[1], param_grad=[1, 1],
                                     tm=TM_ROW, name="attn_bwd", row_dtypes=[BF16])
    dwkv = _mm(mem, jnp.concatenate([dkmem, dvmem], axis=1), ta=True, name="mem_dw_kv")
    dx = _mm(dxq, w["wxq"], tb=True, acc=dx, name="dx_xq")
    dwxq = _mm(xb, dxq, ta=True, name="dw_xq")
    mixer_token = None if before_mixer is None else before_mixer(dict(w_kv_mem=dwkv, w_o=dwo, mlp_w1=dw1, mlp_w2=dw2))
    dx, mg = (_gdn_bwd if is_gdn else _s5_bwd)(xb, w, msave, dmix, dx, mixer_token)
    grads = dict(mixer=mg, wxq=dwxq, wkv=dwkv, wo=dwo, w1=dw1, w2=dw2,
                 ln1_g=dg1[0], ln1_b=db1[0], ln2_g=dg2[0], ln2_b=db2[0])
    return dx, grads


def _loss_and_grad(y, target):
    def f(yv, tv):
        err = yv - tv
        return (err * (1.0 / D_MODEL),), (0.5 / D_MODEL * jnp.sum(err * err, axis=0, keepdims=True),)

    (dy,), (part,) = _rowwise(f, [y, target], [], [(D_MODEL, F32)], [((1, D_MODEL), F32)], tm=512, name="loss")
    return jnp.sum(part), dy


def _layer_weights(full, i):
    j = i // 2
    w = dict(wkv=full["w_kv_mem"][i].astype(BF16),
             wo_mix=full["w_o"][i][:D_MODEL].astype(BF16), wo_cross=full["w_o"][i][D_MODEL:].astype(BF16),
             ln1_g=full["ln1_g"][i][None], ln1_b=full["ln1_b"][i][None],
             ln2_g=full["ln2_g"][i][None], ln2_b=full["ln2_b"][i][None],
             w1=full["mlp_w1"][i].astype(BF16), w2=full["mlp_w2"][i].astype(BF16))
    if i % 2 == 0:
        w_in = full["gdn_w_in"][j]
        gd = 3 * D_MODEL
        w.update(wqkv=w_in[:, :gd].astype(BF16), wz=w_in[:, gd:gd + D_MODEL].astype(BF16),
                 wba=jnp.pad(w_in[:, gd + D_MODEL:gd + D_MODEL + 2 * GDN_HEADS],
                             ((0, 0), (0, LANES - 2 * GDN_HEADS))).astype(BF16),
                 wxq=w_in[:, gd + D_MODEL + 2 * GDN_HEADS:].astype(BF16),
                 conv_w=full["gdn_conv_w"][j],
                 a_log8=jnp.broadcast_to(full["gdn_a_log"][j][:, None], (GDN_HEADS, LANES)),
                 dt_bias8=jnp.broadcast_to(full["gdn_dt_bias"][j][:, None], (GDN_HEADS, LANES)),
                 norm_g=full["gdn_norm_g"][j][None])
    else:
        w_in = full["s5_w_in"][j]
        (kx, ec, fc, a_t), prep_vjp = jax.vjp(
            _s5_prep, full["s5_a_re"][j], full["s5_a_im"][j], full["s5_b_re"][j], full["s5_b_im"][j],
            full["s5_c_re"][j], full["s5_c_im"][j], full["s5_log_dt"][j])
        w.update(wu=w_in[:, :D_MODEL].astype(BF16), wxq=w_in[:, D_MODEL:].astype(BF16),
                 kx=kx, ec=ec, fc=fc, a_t=a_t, prep_vjp=prep_vjp,
                 d=full["s5_d"][j][None], w_glu=full["s5_w_glu"][j].astype(BF16), b_glu=full["s5_b_glu"][j][None])
    return w


def _sharded_grads(l, i):
    m = l["mixer"]
    out = dict(w_kv_mem=l["wkv"], w_o=l["wo"], mlp_w1=l["w1"], mlp_w2=l["w2"])
    if i % 2 == 0:
        out.update(gdn_w_in=jnp.concatenate([m["wqkv"], m["wz"], m["wba"][:, :2 * GDN_HEADS], l["wxq"]], axis=1),
                   gdn_conv_w=m["conv_w"])
    else:
        out.update(s5_w_in=jnp.concatenate([m["wu"], l["wxq"]], axis=1), s5_d=m["d"], s5_w_glu=m["w_glu"],
                   s5_b_glu=m["b_glu"])
    return out


def _replicated_grads(layer_grads):
    g = layer_grads
    gdn = [g[i]["mixer"] for i in range(DEPTH) if i % 2 == 0]
    s5 = [g[i]["mixer"] for i in range(DEPTH) if i % 2 == 1]
    out = {n: jnp.stack([l[n] for l in g]) for n in ("ln1_g", "ln1_b", "ln2_g", "ln2_b")}
    out.update({"gdn_" + n: jnp.stack([m[n] for m in gdn]) for n in ("a_log", "dt_bias", "norm_g")})
    out.update({"s5_" + n: jnp.stack([m[n] for m in s5])
                for n in ("a_re", "a_im", "b_re", "b_im", "c_re", "c_im", "log_dt")})
    return out


def _local_step(x, mem, target, weights_of, grads_ready, before_first_mixer):
    lw, saves = [], []
    h, hb = x, x.astype(BF16)
    for i in range(DEPTH):
        lw.append(weights_of(i, h))
        h, hb, s = _layer_fwd(h, hb, mem, lw[i], i % 2 == 0)
        saves.append(s)
    loss, d = _loss_and_grad(h, target)
    grads = [None] * DEPTH
    token = None
    for i in reversed(range(DEPTH)):
        d, grads[i] = _layer_bwd(mem, lw[i], i % 2 == 0, saves[i], d, token, None if i else before_first_mixer)
        token = grads_ready(i, grads[i])
    return loss, d, grads


ANY = pl.BlockSpec(memory_space=pl.ANY)
SHARD_ROWS = 1024
SMALL_ROWS = 128


def _place():
    return lax.axis_index("x"), lax.axis_index("y"), lax.axis_index("c")


def _other_chips(x, y):
    return [(1 - x, y), (x, 1 - y), (1 - x, 1 - y)]


def _all_gather_chips(wpack, *, name):
    rows = wpack.shape[0]
    half = rows // 2

    def body(w_ref, out_ref, send_sems, recv_sems):
        x, y, c = _place()
        sibling = (x, y, 1 - c)
        chips = _other_chips(x, y)

        def blk(cx, cy, cc):
            return out_ref.at[2 * cx + cy, pl.ds(cc * half, half), :]

        def copy(k, src, dst, to):
            return pltpu.make_async_remote_copy(src_ref=src, dst_ref=dst, send_sem=send_sems.at[k],
                                                recv_sem=recv_sems.at[k], device_id=to, device_id_type=MESH)

        first = [copy(j, w_ref.at[pl.ds(c * half, half), :], blk(x, y, c), (cx, cy, c))
                 for j, (cx, cy) in enumerate(chips)]
        for cp in first:
            cp.start()
        passed = [copy(3 + j, blk(cx, cy, c), blk(cx, cy, c), sibling) for j, (cx, cy) in enumerate(chips)]
        for j, (cx, cy) in enumerate(chips):
            copy(j, blk(cx, cy, c), blk(cx, cy, c), (cx, cy, c)).wait_recv()
            passed[j].start()
        for j, (cx, cy) in enumerate(chips):
            copy(3 + j, blk(cx, cy, 1 - c), blk(cx, cy, 1 - c), sibling).wait_recv()
        for cp in first + passed:
            cp.wait_send()

    return pl.pallas_call(
        body, name=name, out_shape=jax.ShapeDtypeStruct((N_CHIPS, rows, D_MODEL), wpack.dtype),
        in_specs=[ANY], out_specs=ANY,
        scratch_shapes=[pltpu.SemaphoreType.DMA((6,)), pltpu.SemaphoreType.DMA((6,))],
    )(wpack)


HBM = pl.BlockSpec(memory_space=pltpu.HBM)
SEM = pl.BlockSpec(memory_space=pltpu.SEMAPHORE)
DATAFLOW = pltpu.SideEffectType.DATAFLOW_SIDE_EFFECTING


def _gather_ici_copies(w_ref, land_ref, send_sems, recv_sems, outgoing):
    x, y, c = _place()
    half = w_ref.shape[0] // 2
    mine = pl.ds(c * half, half)
    return [pltpu.make_async_remote_copy(
        src_ref=w_ref.at[mine, :], dst_ref=land_ref.at[2 * x + y if outgoing else 2 * cx + cy, mine, :],
        send_sem=send_sems.at[j], recv_sem=recv_sems.at[j], device_id=(cx, cy, c), device_id_type=MESH)
        for j, (cx, cy) in enumerate(_other_chips(x, y))]


def _gather_start(wpack, after):
    rows = wpack.shape[0]

    def body(w_ref, land_ref, after_ref, send_sems, recv_sems, w_thru, land_thru, token):
        for cp in _gather_ici_copies(w_ref, land_ref, send_sems, recv_sems, outgoing=True):
            cp.start()
        token[...] = jnp.zeros_like(token)

    land = pltpu.with_memory_space_constraint(lax.empty((N_CHIPS, rows, D_MODEL), wpack.dtype), pltpu.HBM)
    return pl.pallas_call(
        body, name="gather_start",
        out_shape=(pltpu.SemaphoreType.DMA((3,)), pltpu.SemaphoreType.DMA((3,)), pltpu.HBM(wpack.shape, wpack.dtype),
                   pltpu.HBM(land.shape, land.dtype), jax.ShapeDtypeStruct((SUBLANES, LANES), F32)),
        in_specs=(HBM, HBM, ANY), out_specs=(SEM, SEM, HBM, HBM, pl.BlockSpec(memory_space=pltpu.VMEM)),
        input_output_aliases={0: 2, 1: 3},
        compiler_params=pltpu.CompilerParams(has_side_effects=DATAFLOW),
    )(pltpu.with_memory_space_constraint(wpack, pltpu.HBM), land, after)


def _gather_wait(send_sems, recv_sems, w_thru, land_thru, after):
    def body(w_ref, land_ref, send_sems, recv_sems, after_ref, w_dead, land_out):
        for cp in _gather_ici_copies(w_ref, land_ref, send_sems, recv_sems, outgoing=False):
            cp.wait_send()
            cp.wait_recv()

    return pl.pallas_call(
        body, name="gather_wait",
        out_shape=(pltpu.HBM(w_thru.shape, w_thru.dtype), pltpu.HBM(land_thru.shape, land_thru.dtype)),
        in_specs=(HBM, HBM, SEM, SEM, ANY), out_specs=(HBM, HBM), input_output_aliases={0: 0, 1: 1},
        compiler_params=pltpu.CompilerParams(has_side_effects=DATAFLOW),
    )(w_thru, land_thru, send_sems, recv_sems, after)[1]


def _gather_forward(land, *, name):
    rows = land.shape[1]
    half = rows // 2

    def body(in_ref, out_ref, send_sems, recv_sems):
        x, y, c = _place()

        def copy(j, cx, cy, cc):
            rows_of = out_ref.at[2 * cx + cy, pl.ds(cc * half, half), :]
            return pltpu.make_async_remote_copy(src_ref=rows_of, dst_ref=rows_of, send_sem=send_sems.at[j],
                                                recv_sem=recv_sems.at[j], device_id=(x, y, 1 - c), device_id_type=MESH)

        sends = [copy(j, cx, cy, c) for j, (cx, cy) in enumerate(_other_chips(x, y))]
        for cp in sends:
            cp.start()
        for j, (cx, cy) in enumerate(_other_chips(x, y)):
            copy(j, cx, cy, 1 - c).wait_recv()
        for cp in sends:
            cp.wait_send()

    return pl.pallas_call(
        body, name=name, out_shape=jax.ShapeDtypeStruct(land.shape, land.dtype), in_specs=[ANY], out_specs=ANY,
        input_output_aliases={0: 0},
        scratch_shapes=[pltpu.SemaphoreType.DMA((3,)), pltpu.SemaphoreType.DMA((3,))],
    )(land)


def _sibling_swap(buf, *, name):
    def body(in_ref, out_ref, send_sem, recv_sem):
        x, y, c = _place()
        cp = pltpu.make_async_remote_copy(src_ref=in_ref, dst_ref=out_ref, send_sem=send_sem, recv_sem=recv_sem,
                                          device_id=(x, y, 1 - c), device_id_type=MESH)
        cp.start()
        cp.wait()

    return pl.pallas_call(
        body, name=name, out_shape=jax.ShapeDtypeStruct(buf.shape, buf.dtype), in_specs=[ANY], out_specs=ANY,
        scratch_shapes=[pltpu.SemaphoreType.DMA, pltpu.SemaphoreType.DMA],
    )(buf)


def _pair_exchange(gpack, *, name):
    pieces, rows, width = gpack.shape
    half = rows // 2

    def body(in_ref, got_ref, send_sems, recv_sems):
        x, y, c = _place()
        sends = [pltpu.make_async_remote_copy(src_ref=in_ref.at[p, pl.ds((1 - c) * half, half), :],
                                              dst_ref=got_ref.at[p], send_sem=send_sems.at[p],
                                              recv_sem=recv_sems.at[p], device_id=(x, y, 1 - c), device_id_type=MESH)
                 for p in range(pieces)]
        for cp in sends:
            cp.start()
        for cp in sends:
            cp.wait()

    return pl.pallas_call(
        body, name=name, out_shape=jax.ShapeDtypeStruct((pieces, half, width), gpack.dtype),
        in_specs=[ANY], out_specs=ANY,
        scratch_shapes=[pltpu.SemaphoreType.DMA((pieces,)), pltpu.SemaphoreType.DMA((pieces,))],
    )(gpack)


def _pair_add(gpack, got, c, *, name, tm=512):
    pieces, rows, width = gpack.shape
    half = rows // 2
    nb = half // tm

    def body(c_ref, a_ref, b_ref, sum_ref, narrow_ref):
        s = a_ref[...] + b_ref[...]
        sum_ref[...] = s
        narrow_ref[...] = s.astype(BF16)

    blk = pl.BlockSpec((None, tm, width), lambda p, i, c_ref: (p, i, 0))
    return pl.pallas_call(
        body, name=name,
        out_shape=(jax.ShapeDtypeStruct((pieces, half, width), F32), jax.ShapeDtypeStruct((pieces, half, width), BF16)),
        grid_spec=pltpu.PrefetchScalarGridSpec(
            num_scalar_prefetch=1, grid=(pieces, nb),
            in_specs=[pl.BlockSpec((None, tm, width), lambda p, i, c_ref: (p, c_ref[0] * nb + i, 0)), blk],
            out_specs=(blk, blk)),
        compiler_params=_params(("parallel", "parallel")),
    )(c, gpack, got)


def _chip_exchange(pieces, *, name):
    _, rows, width = pieces.shape

    def body(in_ref, out_ref, send_sems, recv_sems):
        x, y, c = _place()
        cps = [pltpu.make_async_remote_copy(src_ref=in_ref.at[2 * cx + cy], dst_ref=out_ref.at[j],
                                            send_sem=send_sems.at[j], recv_sem=recv_sems.at[j],
                                            device_id=(cx, cy, c), device_id_type=MESH)
               for j, (cx, cy) in enumerate(_other_chips(x, y))]
        for cp in cps:
            cp.start()
        for cp in cps:
            cp.wait()

    return pl.pallas_call(
        body, name=name, out_shape=jax.ShapeDtypeStruct((3, rows, width), pieces.dtype), in_specs=[ANY], out_specs=ANY,
        scratch_shapes=[pltpu.SemaphoreType.DMA((3,)), pltpu.SemaphoreType.DMA((3,))],
    )(pieces)


def _chip_exchange_copies(in_ref, land_ref, send_sems, recv_sems):
    x, y, c = _place()
    return [pltpu.make_async_remote_copy(src_ref=in_ref.at[2 * cx + cy], dst_ref=land_ref.at[j],
                                         send_sem=send_sems.at[j], recv_sem=recv_sems.at[j],
                                         device_id=(cx, cy, c), device_id_type=MESH)
            for j, (cx, cy) in enumerate(_other_chips(x, y))]


def _chip_exchange_start(pieces):
    _, rows, width = pieces.shape

    def body(in_ref, land_ref, send_sems, recv_sems, in_thru, land_thru, token):
        for cp in _chip_exchange_copies(in_ref, land_ref, send_sems, recv_sems):
            cp.start()
        token[...] = jnp.zeros_like(token)

    land = pltpu.with_memory_space_constraint(lax.empty((3, rows, width), pieces.dtype), pltpu.HBM)
    return pl.pallas_call(
        body, name="rs_chip_start",
        out_shape=(pltpu.SemaphoreType.DMA((3,)), pltpu.SemaphoreType.DMA((3,)), pltpu.HBM(pieces.shape, pieces.dtype),
                   pltpu.HBM(land.shape, land.dtype), jax.ShapeDtypeStruct((SUBLANES, LANES), F32)),
        in_specs=(HBM, HBM), out_specs=(SEM, SEM, HBM, HBM, pl.BlockSpec(memory_space=pltpu.VMEM)),
        input_output_aliases={0: 2, 1: 3},
        compiler_params=pltpu.CompilerParams(has_side_effects=DATAFLOW),
    )(pltpu.with_memory_space_constraint(pieces, pltpu.HBM), land)


def _chip_exchange_wait(send_sems, recv_sems, in_thru, land_thru, after):
    def body(in_ref, land_ref, send_sems, recv_sems, after_ref, in_dead, land_out):
        for cp in _chip_exchange_copies(in_ref, land_ref, send_sems, recv_sems):
            cp.wait_send()
            cp.wait_recv()

    return pl.pallas_call(
        body, name="rs_chip_wait",
        out_shape=(pltpu.HBM(in_thru.shape, in_thru.dtype), pltpu.HBM(land_thru.shape, land_thru.dtype)),
        in_specs=(HBM, HBM, SEM, SEM, ANY), out_specs=(HBM, HBM), input_output_aliases={0: 0, 1: 1},
        compiler_params=pltpu.CompilerParams(has_side_effects=DATAFLOW),
    )(in_thru, land_thru, send_sems, recv_sems, after)[1]


def _all_reduce_small(v, *, name):
    rows, width = v.shape
    half = rows // 2
    assert half % SUBLANES == 0

    def body(in_ref, out_ref, pair_buf, chip_buf, send_sems, recv_sems):
        x, y, c = _place()
        sibling = (x, y, 1 - c)
        me = 2 * x + y
        mine = pl.ds(pl.multiple_of(c * half, SUBLANES), half)
        other = pl.ds(pl.multiple_of((1 - c) * half, SUBLANES), half)

        def copy(k, src, dst, to):
            return pltpu.make_async_remote_copy(src_ref=src, dst_ref=dst, send_sem=send_sems.at[k],
                                                recv_sem=recv_sems.at[k], device_id=to, device_id_type=MESH)

        swap = copy(0, in_ref.at[other, :], pair_buf, sibling)
        swap.start()
        swap.wait()
        chip_buf[me] = in_ref[mine, :] + pair_buf[...]
        chips = _other_chips(x, y)
        for j, (cx, cy) in enumerate(chips):
            copy(1 + j, chip_buf.at[me], chip_buf.at[me], (cx, cy, c)).start()
        for j, (cx, cy) in enumerate(chips):
            got = copy(1 + j, chip_buf.at[me], chip_buf.at[2 * cx + cy], (cx, cy, c))
            got.wait_send()
            got.wait_recv()
        out_ref[mine, :] = ((chip_buf[0] + chip_buf[1]) + chip_buf[2]) + chip_buf[3]
        share = copy(1 + len(chips), out_ref.at[mine, :], out_ref.at[mine, :], sibling)
        share.start()
        share.wait_send()
        copy(1 + len(chips), out_ref.at[other, :], out_ref.at[other, :], sibling).wait_recv()

    vmem = pl.BlockSpec(memory_space=pltpu.VMEM)
    return pl.pallas_call(
        body, name=name, out_shape=jax.ShapeDtypeStruct(v.shape, v.dtype), in_specs=[vmem], out_specs=vmem,
        scratch_shapes=[pltpu.VMEM((half, width), v.dtype), pltpu.VMEM((N_CHIPS, half, width), v.dtype),
                        pltpu.SemaphoreType.DMA((5,)), pltpu.SemaphoreType.DMA((5,))],
        compiler_params=pltpu.CompilerParams(vmem_limit_bytes=VMEM_LIMIT_V7X),
    )(v)


def _reduce_scatter_begin(gpack, behind):
    x, y, c = _place()
    got = _pair_exchange(gpack, name="rs_pair_swap")
    pair, pair16 = _pair_add(gpack, got, c.astype(jnp.int32).reshape(1), name="rs_pair_add")
    mine = lax.dynamic_index_in_dim(pair, 2 * x + y, axis=0, keepdims=False)
    if behind:
        *in_flight, token = _chip_exchange_start(pair16)
        return dict(mine=mine, in_flight=in_flight), token
    return dict(mine=mine, recv=_chip_exchange(pair16, name="rs_chip_exchange")), None


def _reduce_scatter_end(state, after=None):
    c = lax.axis_index("c")
    recv = state["recv"] if "recv" in state else _chip_exchange_wait(*state["in_flight"], after=after)
    (total,) = _rw_fwd(_f_add4, [state["mine"], recv[0], recv[1], recv[2]], [], tm=512, name="rs_chip_add")
    theirs = _sibling_swap(total, name="rs_share_swap")
    return jnp.concatenate([jnp.where(c == 0, total, theirs), jnp.where(c == 0, theirs, total)], axis=0)


_SHARDED = (("w_kv_mem", 1), ("w_o", 1), ("mlp_w1", 2), ("mlp_w2", 1), ("gdn_w_in", 2), ("gdn_conv_w", 2),
            ("s5_w_in", 2), ("s5_d", 1), ("s5_w_glu", 1), ("s5_b_glu", 1))
_MATMUL_ONLY = ("w_kv_mem", "w_o", "mlp_w1", "mlp_w2", "gdn_w_in", "s5_w_in", "s5_w_glu")
_KEPT_BLOCKED = ("mlp_w1",)
_REPLICATED = ("ln1_g", "ln1_b", "ln2_g", "ln2_b", "gdn_a_log", "gdn_dt_bias", "gdn_norm_g", "s5_a_re", "s5_a_im",
               "s5_b_re", "s5_b_im", "s5_c_re", "s5_c_im", "s5_log_dt")
_WEIGHTS = ("w_kv_mem", "w_o", "ln1_g", "ln1_b", "ln2_g", "ln2_b", "mlp_w1", "mlp_w2", "gdn_w_in", "gdn_conv_w",
            "gdn_a_log", "gdn_dt_bias", "gdn_norm_g", "s5_w_in", "s5_a_re", "s5_a_im", "s5_b_re", "s5_b_im",
            "s5_c_re", "s5_c_im", "s5_log_dt", "s5_d", "s5_w_glu", "s5_b_glu")


ROW_ALIGN = 16


def _n_rows(shape):
    return -(-math.prod(shape) // (ROW_ALIGN * D_MODEL)) * ROW_ALIGN


def _as_rows(a):
    rows = _n_rows(a.shape)
    if a.shape[-1] == D_MODEL and a.size == rows * D_MODEL:
        return a.reshape(-1, D_MODEL)
    flat = a.reshape(-1)
    return jnp.pad(flat, (0, rows * D_MODEL - flat.size)).reshape(rows, D_MODEL)


def _pack(arrs, unit_rows=SHARD_ROWS):
    rows = [_as_rows(a) for a in arrs]
    pad = -sum(r.shape[0] for r in rows) % unit_rows
    if pad:
        rows.append(jnp.zeros((pad, D_MODEL), rows[0].dtype))
    return jnp.concatenate(rows, axis=0)


def _unpack(packed, shapes):
    lead = packed.shape[:-2]
    out, off = [], 0
    for s in shapes:
        r = _n_rows(s)
        seg = lax.slice_in_dim(packed, off, off + r, axis=len(lead))
        if s[-1] != D_MODEL or math.prod(s) != r * D_MODEL:
            seg = lax.slice_in_dim(seg.reshape(lead + (-1,)), 0, math.prod(s), axis=len(lead))
        out.append(seg.reshape(lead + tuple(s)))
        off += r
    return out


def _split3(t):
    hi = t.astype(BF16)
    r1 = t - hi.astype(F32)
    mid = r1.astype(BF16)
    lo = (r1 - mid.astype(F32)).astype(BF16)
    return jnp.stack([hi, mid, lo], axis=-1)


def _join3(t):
    return (t[..., 0].astype(F32) + t[..., 1].astype(F32)) + t[..., 2].astype(F32)


def _merge_chips(blocks, axis):
    return jnp.concatenate([blocks[s] for s in range(N_CHIPS)], axis=axis)


def _pack_for_chips(weights):
    rows = []
    for s in range(N_CHIPS):
        chip = []
        for layers, axis in weights:
            if axis is None:
                blocks = [g[s] for g in layers]
            else:
                n = layers[0].shape[axis] // N_CHIPS
                blocks = [lax.slice_in_dim(g, s * n, (s + 1) * n, axis=axis) for g in layers]
            if math.prod(blocks[0].shape) % (ROW_ALIGN * D_MODEL) == 0:
                chip += [_as_rows(b) for b in blocks]
            else:
                chip.append(_as_rows(jnp.stack(blocks)))
        pad = -sum(r.shape[0] for r in chip) % SHARD_ROWS
        rows += chip + ([jnp.zeros((pad, D_MODEL), F32)] if pad else [])
    return jnp.concatenate(rows, axis=0).reshape(N_CHIPS, -1, D_MODEL)


def kernel(x, mem, w_kv_mem, w_o, ln1_g, ln1_b, ln2_g, ln2_b, mlp_w1, mlp_w2, gdn_w_in, gdn_conv_w, gdn_a_log, gdn_dt_bias, gdn_norm_g, s5_w_in, s5_a_re, s5_a_im, s5_b_re, s5_b_im, s5_c_re, s5_c_im, s5_log_dt, s5_d, s5_w_glu, s5_b_glu, loss_target, m_w_kv_mem, m_w_o, m_ln1_g, m_ln1_b, m_ln2_g, m_ln2_b, m_mlp_w1, m_mlp_w2, m_gdn_w_in, m_gdn_conv_w, m_gdn_a_log, m_gdn_dt_bias, m_gdn_norm_g, m_s5_w_in, m_s5_a_re, m_s5_a_im, m_s5_b_re, m_s5_b_im, m_s5_c_re, m_s5_c_im, m_s5_log_dt, m_s5_d, m_s5_w_glu, m_s5_b_glu, v_w_kv_mem, v_w_o, v_ln1_g, v_ln1_b, v_ln2_g, v_ln2_b, v_mlp_w1, v_mlp_w2, v_gdn_w_in, v_gdn_conv_w, v_gdn_a_log, v_gdn_dt_bias, v_gdn_norm_g, v_s5_w_in, v_s5_a_re, v_s5_a_im, v_s5_b_re, v_s5_b_im, v_s5_c_re, v_s5_c_im, v_s5_log_dt, v_s5_d, v_s5_w_glu, v_s5_b_glu):
    given = dict(locals())
    w = {n: given[n] for n in _WEIGHTS}
    mom = {n: given["m_" + n] for n in _WEIGHTS}
    var = {n: given["v_" + n] for n in _WEIGHTS}
    shard_names = [n for n, _ in _SHARDED]
    shard_shapes = [w[n].shape for n in shard_names]
    rep_shapes = [w[n].shape for n in _REPLICATED]

    wire = {n: w[n].astype(BF16) if n in _MATMUL_ONLY else _split3(w[n]) for n in shard_names}
    first = {n: 0 if n.startswith("s5_") else 1 for n in shard_names}
    me_chip = 2 * lax.axis_index("x") + lax.axis_index("y")
    early = [wire[n][:first[n]] for n in shard_names if first[n]]
    late = [wire[n][first[n]:] for n in shard_names]
    early_pack, late_pack = _pack(early), _pack(late)
    landed = _all_gather_chips(early_pack, name="gather_first_layer")
    landed = lax.dynamic_update_index_in_dim(landed, early_pack, me_chip, axis=0)
    early_blocks = dict(zip([n for n in shard_names if first[n]], _unpack(landed, [a.shape for a in early])))
    send_sems, recv_sems, pack_thru, land_thru, token = _gather_start(late_pack, after=landed)
    axis_of = dict(_SHARDED)

    def merged(n, blk):
        if n in _KEPT_BLOCKED:
            return blk
        return _merge_chips(blk if n in _MATMUL_ONLY else _join3(blk), axis_of[n] - 1)

    late_full = {}

    def weights_of(i, h):
        if i == 0:
            full = {n: [merged(n, blk[:, 0])] for n, blk in early_blocks.items()}
            full["gdn_w_in"][0] = full["gdn_w_in"][0] + token[0, 0].astype(BF16)
        else:
            if not late_full:
                land = _gather_wait(send_sems, recv_sems, pack_thru, land_thru, after=h)
                land = _gather_forward(land, name="gather_forward")
                land = lax.dynamic_update_index_in_dim(land, late_pack, me_chip, axis=0)
                for n, blk in zip(shard_names, _unpack(land, [a.shape for a in late])):
                    late_full[n] = [None] * first[n] + [merged(n, blk[:, t]) for t in range(blk.shape[1])]
            full = dict(late_full)
        full.update({n: w[n] for n in _REPLICATED})
        return _layer_weights(full, i)

    sharded = {}
    in_flight = {}
    first_mixer, first_outer = (0, "mixer"), (0, "outer")

    def group_pack(parts):
        names = [n for n in shard_names if any(n in sharded[i] for i in parts)]
        per_weight = [[sharded[i][n] for i in parts if n in sharded[i]] for n in names]
        pack = _pack_for_chips([(g, None if n in _KEPT_BLOCKED else axis_of[n] - 1) for n, g in zip(names, per_weight)])
        return pack, names, [(len(g),) + w[n].shape[1:] for n, g in zip(names, per_weight)]

    def grads_ready(i, g):
        by_weight = _sharded_grads(g, i)
        if i:
            sharded[i] = by_weight
        else:
            sharded[first_mixer] = {n: g for n, g in by_weight.items() if n not in sharded[first_outer]}

    def before_first_mixer(outer):
        sharded[first_outer] = outer
        pack, names, shapes = group_pack([first_outer] + list(range(1, DEPTH)))
        state, token = _reduce_scatter_begin(pack, behind=True)
        in_flight.update(state=state, names=names, shapes=shapes)
        return token

    loss, grad_x, layer_grads = _local_step(x[0], mem[0], loss_target[0], weights_of, grads_ready,
                                            before_first_mixer)
    loss = lax.psum(loss, ("x", "y", "c"))
    pack, names, shapes = group_pack([first_mixer])
    state, _ = _reduce_scatter_begin(pack, behind=False)
    pieces = {n: [] for n in shard_names}
    for n, g in zip(names, _unpack(_reduce_scatter_end(state), shapes)):
        pieces[n].append(g)
    late = _reduce_scatter_end(in_flight["state"], after=grad_x)
    for n, g in zip(in_flight["names"], _unpack(late, in_flight["shapes"])):
        pieces[n].append(g)
    g_shards = [p[0] if len(p) == 1 else jnp.concatenate(p, axis=0) for p in (pieces[n] for n in shard_names)]

    def pack_small(d):
        return _pack([d[n] for n in _REPLICATED], unit_rows=SMALL_ROWS)

    g_rep = _all_reduce_small(pack_small(_replicated_grads(layer_grads)), name="reduce_replicated")

    def adamw(wp, gp, mp, vp, name):
        return _rw_fwd(_f_adamw, [wp, gp, mp, vp], [], tm=256, name=name)

    outs = {}
    for n, g in zip(shard_names, g_shards):
        flat = (-1, w[n].shape[-1])
        res = adamw(w[n].reshape(flat), g.reshape(flat), mom[n].reshape(flat), var[n].reshape(flat), "adamw_" + n)
        outs[("grad", n)] = g
        outs.update({(kind, n): a.reshape(w[n].shape) for kind, a in zip(("delta", "new_m", "new_v"), res)})
    packed = (g_rep,) + tuple(adamw(pack_small(w), g_rep, pack_small(mom), pack_small(var), "adamw_replicated"))
    for kind, pr in zip(("grad", "delta", "new_m", "new_v"), packed):
        outs.update({(kind, n): a for n, a in zip(_REPLICATED, _unpack(pr, rep_shapes))})
    return (loss, grad_x[None]) + tuple(outs[(kind, n)] for kind in ("grad", "delta", "new_m", "new_v")
                                        for n in _WEIGHTS)
```

```python
import functools
import math

import jax
import jax.numpy as jnp
from jax import lax
from jax.experimental import pallas as pl
from jax.experimental.pallas import tpu as pltpu

F32 = jnp.float32
BF16 = jnp.bfloat16
MESH = pl.DeviceIdType.MESH

D_MODEL = 1024
DEPTH = 4
GDN_HEADS = 8
HEAD_DIM = 128
GDN_CONV = 4
GDN_CHUNK = 64
S5_GROUPS = 64
S5_GROUP = 16
S5_STATE = 64
XA_HEADS = 4
XA_DIM = 512
D_FF = 4096
DN_ALPHA = (2 * DEPTH) ** 0.25
LN_EPS = 1e-5
RMS_EPS = 1e-6
ADAM_LR, ADAM_B1, ADAM_B2, ADAM_EPS, ADAM_WD, ADAM_STEP = 0.001, 0.9, 0.999, 1e-08, 0.01, 10

VMEM_LIMIT_V7X = 56 * 1024 * 1024
LANES = 128
SUBLANES = 8
S5_T = 16
S5_TILES = D_MODEL // LANES
N_CHIPS = 4
N_DEV = 8


def _params(sem):
    return pltpu.CompilerParams(dimension_semantics=sem, vmem_limit_bytes=VMEM_LIMIT_V7X)


def _tile(n, pref):
    if n <= pref:
        return n
    t = (pref // LANES) * LANES
    while n % t:
        t -= LANES
    return t


def _row_tile(n, pref):
    if n % SUBLANES:
        return n
    t = min(pref, n) // SUBLANES * SUBLANES
    while n % t:
        t -= SUBLANES
    return t


def _col_blocked_spec(rows_tile, cols_tile, block_cols, rows_axis, cols_axis):
    r = block_cols // cols_tile

    def index(*ijk):
        c = ijk[cols_axis]
        return (c, ijk[rows_axis], 0) if r == 1 else (c // r, ijk[rows_axis], c % r)

    return pl.BlockSpec((None, rows_tile, cols_tile), index)


def _mm(a, b, *, ta=False, tb=False, acc=None, name, tm=1024, tn=1024, tk=None, out_blocks=0):
    if tk is None:
        tk = 4096 if a.dtype == BF16 and b.dtype == BF16 else 2048
    k_dim, m_dim = a.shape if ta else a.shape[::-1]
    b_rows, b_cols = (b.shape[0], b.shape[1]) if b.ndim == 2 else (b.shape[1], b.shape[0] * b.shape[2])
    n_dim = b_rows if tb else b_cols
    assert (b_cols if tb else b_rows) == k_dim, (a.shape, b.shape, ta, tb)
    limit_n = n_dim // out_blocks if out_blocks else (n_dim if b.ndim == 2 or tb else b.shape[2])
    limit_k = b.shape[2] if (b.ndim == 3 and tb) else k_dim
    tm, tn, tk = _tile(m_dim, tm), _tile(limit_n, min(tn, limit_n)), _tile(limit_k, min(tk, limit_k))
    a_spec = (pl.BlockSpec((tk, tm), lambda i, j, k: (k, i)) if ta else pl.BlockSpec((tm, tk), lambda i, j, k: (i, k)))
    if b.ndim == 3:
        b_spec = (_col_blocked_spec(tn, tk, b.shape[2], 1, 2) if tb else _col_blocked_spec(tk, tn, b.shape[2], 2, 1))
    else:
        b_spec = (pl.BlockSpec((tn, tk), lambda i, j, k: (j, k)) if tb
                  else pl.BlockSpec((tk, tn), lambda i, j, k: (k, j)))
    o_spec = (_col_blocked_spec(tm, tn, n_dim // out_blocks, 0, 1) if out_blocks
              else pl.BlockSpec((tm, tn), lambda i, j, k: (i, j)))
    o_shape = (out_blocks, m_dim, n_dim // out_blocks) if out_blocks else (m_dim, n_dim)
    dn = (((0 if ta else 1,), (1 if tb else 0,)), ((), ()))
    has_acc = acc is not None

    def body(*refs):
        a_ref, b_ref = refs[0], refs[1]
        o_ref = refs[-1]
        k = pl.program_id(2)
        p = lax.dot_general(a_ref[...].astype(BF16), b_ref[...].astype(BF16), dn,
                            preferred_element_type=F32)

        @pl.when(k == 0)
        def _():
            o_ref[...] = p + refs[2][...] if has_acc else p

        @pl.when(k > 0)
        def _():
            o_ref[...] += p

    return pl.pallas_call(
        body, name=name,
        out_shape=jax.ShapeDtypeStruct(o_shape, F32),
        grid=(m_dim // tm, n_dim // tn, k_dim // tk),
        in_specs=[a_spec, b_spec] + ([o_spec] if has_acc else []),
        out_specs=o_spec,
        compiler_params=_params(("parallel", "parallel", "arbitrary")),
    )(*([a, b] + ([acc] if has_acc else [])))


def _mm_relu2(a, b, *, name, tm=1024):
    m_dim, k_dim = a.shape
    n_blocks, _, tn = b.shape
    n_dim = n_blocks * tn
    tm = _tile(m_dim, tm)

    def body(a_ref, b_ref, h_ref, act_ref):
        h = jnp.dot(a_ref[...].astype(BF16), b_ref[...].astype(BF16), preferred_element_type=F32)
        h_ref[...] = h.astype(h_ref.dtype)
        r = jnp.maximum(h, 0.0)
        act_ref[...] = (r * r).astype(BF16)

    o_spec = pl.BlockSpec((tm, tn), lambda i, j: (i, j))
    return pl.pallas_call(
        body, name=name,
        out_shape=(jax.ShapeDtypeStruct((m_dim, n_dim), BF16), jax.ShapeDtypeStruct((m_dim, n_dim), BF16)),
        grid=(m_dim // tm, n_dim // tn),
        in_specs=[pl.BlockSpec((tm, k_dim), lambda i, j: (i, 0)),
                  pl.BlockSpec((None, k_dim, tn), lambda i, j: (j, 0, 0))],
        out_specs=(o_spec, o_spec),
        compiler_params=_params(("parallel", "parallel")),
    )(a, b)


def _mm_relu2_grad(d, b, h, *, name, tm=1024, tn=1024):
    m_dim, k_dim = d.shape
    n_dim = b.shape[0]
    tm, tn = _tile(m_dim, tm), _tile(n_dim, tn)

    def body(d_ref, b_ref, h_ref, o_ref):
        p = lax.dot_general(d_ref[...].astype(BF16), b_ref[...].astype(BF16), ((NT), ((), ())),
                            preferred_element_type=F32)
        o_ref[...] = (p * (2.0 * jnp.maximum(h_ref[...].astype(F32), 0.0))).astype(BF16)

    o_spec = pl.BlockSpec((tm, tn), lambda i, j: (i, j))
    return pl.pallas_call(
        body, name=name,
        out_shape=jax.ShapeDtypeStruct((m_dim, n_dim), BF16),
        grid=(m_dim // tm, n_dim // tn),
        in_specs=[pl.BlockSpec((tm, k_dim), lambda i, j: (i, 0)), pl.BlockSpec((tn, k_dim), lambda i, j: (j, 0)), o_spec],
        out_specs=o_spec,
        compiler_params=_params(("parallel", "parallel")),
    )(d, b, h)


def _rowwise(f, rows, params, row_out, acc_out, *, tm, name):
    length = rows[0].shape[0]
    tm = _row_tile(length, tm)
    nr, npar, nro = len(rows), len(params), len(row_out)

    def body(*refs):
        ins = [r[...] for r in refs[:nr + npar]]
        outs = refs[nr + npar:]
        r_o, a_o = f(*ins)
        for ref, val in zip(outs[:nro], r_o):
            ref[...] = val.astype(ref.dtype)
        i = pl.program_id(0)
        for ref, val in zip(outs[nro:], a_o):
            @pl.when(i == 0)
            def _(ref=ref, val=val):
                ref[...] = val.astype(ref.dtype)

            @pl.when(i > 0)
            def _(ref=ref, val=val):
                ref[...] += val.astype(ref.dtype)

    in_specs = ([pl.BlockSpec((tm, r.shape[1]), lambda i: (i, 0)) for r in rows]
                + [pl.BlockSpec(p.shape, lambda i: (0, 0)) for p in params])
    out_specs = ([pl.BlockSpec((tm, w), lambda i: (i, 0)) for w, _ in row_out]
                 + [pl.BlockSpec(s, lambda i: (0, 0)) for s, _ in acc_out])
    out_shape = ([jax.ShapeDtypeStruct((length, w), dt) for w, dt in row_out]
                 + [jax.ShapeDtypeStruct(s, dt) for s, dt in acc_out])
    res = pl.pallas_call(
        body, name=name, out_shape=out_shape, grid=(length // tm,),
        in_specs=in_specs, out_specs=out_specs,
        compiler_params=_params(("arbitrary",) if acc_out else ("parallel",)),
    )(*rows, *params)
    return res[:nro], res[nro:]


def _rw_fwd(f, rows, params, *, tm, name, out_dtypes=None):
    tm_ = _row_tile(rows[0].shape[0], tm)
    shapes = jax.eval_shape(f, *[jax.ShapeDtypeStruct((tm_, r.shape[1]), r.dtype) for r in rows],
                            *[jax.ShapeDtypeStruct(p.shape, p.dtype) for p in params])
    row_out = [(s.shape[1], s.dtype if out_dtypes is None else dt)
               for s, dt in zip(shapes, out_dtypes or shapes)]
    outs, _ = _rowwise(lambda *v: (f(*v), ()), rows, params, row_out, [], tm=tm, name=name)
    return outs


def _rw_bwd(f, rows, params, cots, *, row_grad, param_grad, tm, name, row_dtypes=None):
    nr, npar, nct = len(rows), len(params), len(cots)

    def g(*vals):
        prim = vals[:nr] + vals[nr + nct:]
        ct = vals[nr:nr + nct]
        _, vjp = jax.vjp(f, *prim)
        grads = vjp(tuple(ct))
        return (tuple(grads[i] for i in range(nr) if row_grad[i]),
                tuple(grads[nr + i] for i in range(npar) if param_grad[i]))

    widths = [rows[i].shape[1] for i in range(nr) if row_grad[i]]
    row_out = list(zip(widths, row_dtypes or [F32] * len(widths)))
    acc_out = [(params[i].shape, F32) for i in range(npar) if param_grad[i]]
    return _rowwise(g, list(rows) + list(cots), params, row_out, acc_out, tm=tm, name=name)


def _f_ln_res(x, h, g, b):
    pre = DN_ALPHA * x + h
    mu = jnp.mean(pre, axis=-1, keepdims=True)
    d = pre - mu
    var = jnp.mean(d * d, axis=-1, keepdims=True)
    return (d * lax.rsqrt(var + LN_EPS) * g + b,)


def _silu(t):
    return t * jax.nn.sigmoid(t)


def _f_gdn_qkv(c):
    a = _silu(c)
    outs = []
    for part, scale in ((0, HEAD_DIM ** -0.5), (1, 1.0)):
        heads = []
        for h in range(GDN_HEADS):
            t = a[:, part * D_MODEL + h * HEAD_DIM: part * D_MODEL + (h + 1) * HEAD_DIM]
            t = t * lax.rsqrt(jnp.sum(t * t, axis=-1, keepdims=True) + 1e-6)
            heads.append(t * scale if scale != 1.0 else t)
        outs.append(jnp.concatenate(heads, axis=-1))
    outs.append(a[:, 2 * D_MODEL:])
    return tuple(outs)


def _f_gdn_out(o, z, norm_g):
    heads = []
    for h in range(GDN_HEADS):
        t = o[:, h * HEAD_DIM:(h + 1) * HEAD_DIM]
        t = t * lax.rsqrt(jnp.mean(t * t, axis=-1, keepdims=True) + RMS_EPS) * norm_g
        heads.append(t)
    return (jnp.concatenate(heads, axis=-1) * _silu(z),)


def _f_attn(xq, kmem, vmem):
    heads = []
    for h in range(XA_HEADS):
        sl = slice(h * HEAD_DIM, (h + 1) * HEAD_DIM)
        s = lax.dot_general(xq[:, sl].astype(BF16), kmem[:, sl].astype(BF16),
                            (((1,), (1,)), ((), ())), preferred_element_type=F32) * (HEAD_DIM ** -0.5)
        m = lax.stop_gradient(jnp.max(s, axis=-1, keepdims=True))
        e = jnp.exp(s - m)
        p = e / jnp.sum(e, axis=-1, keepdims=True)
        heads.append(jnp.dot(p.astype(BF16), vmem[:, sl].astype(BF16), preferred_element_type=F32))
    return (jnp.concatenate(heads, axis=-1),)


def _f_s5_gelu(y, u, d):
    return (jax.nn.gelu(y + d * u),)


def _f_s5_gate(zg, t, b):
    return (zg * jax.nn.sigmoid(t + b),)


def _f_add(a, b):
    return (a + b,)


def _f_add4(a, b, c, d):
    return (((a + b.astype(F32)) + c.astype(F32)) + d.astype(F32),)


def _f_adamw(w, g, m, v):
    m = ADAM_B1 * m + (1.0 - ADAM_B1) * g
    v = ADAM_B2 * v + (1.0 - ADAM_B2) * jnp.square(g)
    m_hat = m / (1.0 - ADAM_B1 ** ADAM_STEP)
    v_hat = v / (1.0 - ADAM_B2 ** ADAM_STEP)
    delta = -ADAM_LR * (m_hat / (jnp.sqrt(v_hat) + ADAM_EPS) + ADAM_WD * w)
    return delta, m, v


def _conv_fwd(u, w, *, tm, name):
    length, chans = u.shape
    tm = min(tm, length)
    tc = _tile(chans, 1024)
    hb = tm // SUBLANES

    def body(cur_ref, prev_ref, w_ref, o_ref, buf):
        i = pl.program_id(1)
        buf[0:SUBLANES, :] = jnp.where(i > 0, prev_ref[...], 0.0)
        buf[SUBLANES:, :] = cur_ref[...]
        acc = buf[pl.ds(SUBLANES - 3, tm), :] * w_ref[0:1, :]
        for k in range(1, GDN_CONV):
            acc = acc + buf[pl.ds(SUBLANES - 3 + k, tm), :] * w_ref[k:k + 1, :]
        o_ref[...] = acc

    return pl.pallas_call(
        body, name=name, out_shape=jax.ShapeDtypeStruct(u.shape, F32),
        grid=(chans // tc, length // tm),
        in_specs=[pl.BlockSpec((tm, tc), lambda j, i: (i, j)),
                  pl.BlockSpec((SUBLANES, tc), lambda j, i: (jnp.maximum(i * hb - 1, 0), j)),
                  pl.BlockSpec((GDN_CONV, tc), lambda j, i: (0, j))],
        out_specs=pl.BlockSpec((tm, tc), lambda j, i: (i, j)),
        scratch_shapes=[pltpu.VMEM((tm + SUBLANES, tc), F32)],
        compiler_params=_params(("parallel", "parallel")),
    )(u, u, w)


def _conv_bwd(u, w, dc, *, tm, name):
    length, chans = u.shape
    tm = min(tm, length)
    tc = _tile(chans, 1024)
    hb = tm // SUBLANES
    last = length // tm - 1

    def body(u_ref, uprev_ref, dc_ref, dcnext_ref, w_ref, du_ref, dw_ref, ubuf, dbuf):
        i = pl.program_id(1)
        ubuf[0:SUBLANES, :] = jnp.where(i > 0, uprev_ref[...], 0.0)
        ubuf[SUBLANES:, :] = u_ref[...]
        dbuf[0:tm, :] = dc_ref[...]
        dbuf[tm:, :] = jnp.where(i < last, dcnext_ref[...], 0.0)
        dcv = dc_ref[...]
        du = dbuf[pl.ds(3, tm), :] * w_ref[0:1, :]
        rows = [jnp.sum(dcv * ubuf[pl.ds(SUBLANES - 3, tm), :], axis=0, keepdims=True)]
        for k in range(1, GDN_CONV):
            du = du + dbuf[pl.ds(3 - k, tm), :] * w_ref[k:k + 1, :]
            rows.append(jnp.sum(dcv * ubuf[pl.ds(SUBLANES - 3 + k, tm), :], axis=0, keepdims=True))
        du_ref[...] = du.astype(du_ref.dtype)
        dwv = jnp.concatenate(rows, axis=0)

        @pl.when(i == 0)
        def _():
            dw_ref[...] = dwv

        @pl.when(i > 0)
        def _():
            dw_ref[...] += dwv

    return pl.pallas_call(
        body, name=name,
        out_shape=(jax.ShapeDtypeStruct(u.shape, BF16), jax.ShapeDtypeStruct((GDN_CONV, chans), F32)),
        grid=(chans // tc, length // tm),
        in_specs=[pl.BlockSpec((tm, tc), lambda j, i: (i, j)),
                  pl.BlockSpec((SUBLANES, tc), lambda j, i: (jnp.maximum(i * hb - 1, 0), j)),
                  pl.BlockSpec((tm, tc), lambda j, i: (i, j)),
                  pl.BlockSpec((SUBLANES, tc), lambda j, i: (jnp.minimum((i + 1) * hb, (last + 1) * hb - 1), j)),
                  pl.BlockSpec((GDN_CONV, tc), lambda j, i: (0, j))],
        out_specs=(pl.BlockSpec((tm, tc), lambda j, i: (i, j)),
                   pl.BlockSpec((GDN_CONV, tc), lambda j, i: (0, j))),
        scratch_shapes=[pltpu.VMEM((tm + SUBLANES, tc), F32), pltpu.VMEM((tm + SUBLANES, tc), F32)],
        compiler_params=_params(("parallel", "arbitrary")),
    )(u, u, dc, dc, w)


def _dot(a, b, dims, precision=None):
    if precision is None:
        a, b = a.astype(BF16), b.astype(BF16)
    return lax.dot_general(a, b, (dims, ((), ())), preferred_element_type=F32, precision=precision)


def _dot3(a, b, dims):
    ah, bh = a.astype(BF16), b.astype(BF16)
    al, bl = (a - ah.astype(F32)).astype(BF16), (b - bh.astype(F32)).astype(BF16)

    def d(x, y):
        return lax.dot_general(x, y, (dims, ((), ())), preferred_element_type=F32)

    return d(ah, bh) + (d(ah, bl) + d(al, bh))


NN = ((1,), (0,))
NT = ((1,), (1,))
TN = ((0,), (0,))
HI = lax.Precision.HIGHEST


def _hmap(f, *lists):
    return [f(*t) for t in zip(*lists)]


@jax.custom_vjp
def _unit_lower_inverse(a):
    c = a[0].shape[0]
    eye = (lax.broadcasted_iota(jnp.int32, (c, c), 0) == lax.broadcasted_iota(jnp.int32, (c, c), 1)).astype(F32)
    p = _hmap(lambda x: -x, a)
    t = _hmap(lambda x: eye + x, p)
    for _ in range(int(math.log2(c)) - 1):
        p = _hmap(lambda x: _dot3(x, x, NN), p)
        t = _hmap(lambda x, y: x + _dot3(x, y, NN), t, p)
    return t


def _uli_fwd(a):
    t = _unit_lower_inverse(a)
    return t, t


def _uli_bwd(t, dt):
    left = _hmap(lambda x, y: _dot3(x, y, TN), t, dt)
    return (_hmap(lambda x, y: -_dot3(x, y, NT), left, t),)


_unit_lower_inverse.defvjp(_uli_fwd, _uli_bwd)


@jax.custom_vjp
def _known_inverse(a, t):
    return t


_known_inverse.defvjp(lambda a, t: (t, t),
                      lambda t, dt: (_uli_bwd(t, dt)[0], _hmap(jnp.zeros_like, t)))


def _gdn_chunk(q, k, v, bl, al, a_log, dt_bias, state, t_known=None):
    c = q[0].shape[0]
    row = lax.broadcasted_iota(jnp.int32, (c, c), 0)
    col = lax.broadcasted_iota(jnp.int32, (c, c), 1)
    causal = row >= col
    strict = row > col
    eye = (row == col).astype(F32)
    beta = _hmap(jax.nn.sigmoid, bl)
    g = _hmap(lambda a_, l_, d_: -jnp.exp(a_) * jax.nn.softplus(l_ + d_), a_log, al, dt_bias)
    g_r = _hmap(lambda x: jnp.sum(eye * x, axis=0, keepdims=True), g)
    gc = _hmap(lambda x: jnp.sum(jnp.where(causal, x, 0.0), axis=1, keepdims=True), g_r)
    gc_r = _hmap(lambda x: jnp.sum(jnp.where(row <= col, x, 0.0), axis=0, keepdims=True), g)
    decay = _hmap(lambda x, y: jnp.where(causal, jnp.exp(jnp.where(causal, x - y, 0.0)), 0.0), gc, gc_r)
    e_gc = _hmap(jnp.exp, gc)
    kb = _hmap(jnp.multiply, k, beta)
    vb = _hmap(jnp.multiply, v, beta)
    a_mat = _hmap(lambda x, y, d: jnp.where(strict, _dot(x, y, NT) * d, 0.0), kb, k, decay)
    t_inv = _unit_lower_inverse(a_mat) if t_known is None else _known_inverse(a_mat, t_known)
    u_blk = _hmap(lambda t, x: _dot(t, x, NN), t_inv, vb)
    w_blk = _hmap(lambda t, x, e: _dot(t, x * e, NN), t_inv, kb, e_gc)
    v_new = _hmap(lambda u, w, s: u - _dot(w, s, NN), u_blk, w_blk, state)
    attn = _hmap(lambda x, y, d: _dot(x, y, NT) * d, q, k, decay)
    o_state = _hmap(lambda x, e, s: _dot(x * e, s, NN), q, e_gc, state)
    o = _hmap(lambda base, at, vn: base + _dot(at, vn, NN), o_state, attn, v_new)
    g_last = _hmap(lambda x: jnp.sum(x, axis=0, keepdims=True), g)
    k_dec = _hmap(lambda x, gl, c_: x * jnp.exp(gl - c_), k, g_last, gc)
    new_state = _hmap(lambda s, gl, kd, vn: s * jnp.exp(gl) + _dot(kd, vn, TN), state, g_last, k_dec, v_new)
    return (o, new_state, t_inv) if t_known is None else (o, new_state)


def _gdn_operands(q_ref, k_ref, v_ref, bav, alog_ref, dtb_ref):
    hs = range(GDN_HEADS)
    cols = [slice(h * HEAD_DIM, (h + 1) * HEAD_DIM) for h in hs]
    return ([q_ref[:, sl] for sl in cols], [k_ref[:, sl] for sl in cols], [v_ref[:, sl] for sl in cols],
            [bav[:, h:h + 1] for h in hs], [bav[:, h + GDN_HEADS:h + GDN_HEADS + 1] for h in hs],
            [alog_ref[h:h + 1, 0:1] for h in hs], [dtb_ref[h:h + 1, 0:1] for h in hs])


def _gdn_scan_fwd(q, k, v, ba, a_log, dt_bias, *, name):
    length = q.shape[0]
    n = length // GDN_CHUNK
    c = GDN_CHUNK

    def body(q_ref, k_ref, v_ref, ba_ref, alog_ref, dtb_ref, o_ref, s_ref, t_ref, state):
        i = pl.program_id(0)

        @pl.when(i == 0)
        def _():
            state[...] = jnp.zeros_like(state)

        bav = ba_ref[...]
        heads = [slice(h * HEAD_DIM, (h + 1) * HEAD_DIM) for h in range(GDN_HEADS)]
        s_in = [state[h] for h in range(GDN_HEADS)]
        o, s_out, t_inv = _gdn_chunk(*_gdn_operands(q_ref, k_ref, v_ref, bav, alog_ref, dtb_ref), s_in)
        for h, sl in enumerate(heads):
            s_ref[h] = s_in[h]
            t_ref[h] = t_inv[h]
            o_ref[:, sl] = o[h]
            state[h] = s_out[h]

    row_spec = pl.BlockSpec((c, D_MODEL), lambda i: (i, 0))
    small = pl.BlockSpec((GDN_HEADS, LANES), lambda i: (0, 0))
    return pl.pallas_call(
        body, name=name,
        out_shape=(jax.ShapeDtypeStruct((length, D_MODEL), F32),
                   jax.ShapeDtypeStruct((n, GDN_HEADS, HEAD_DIM, HEAD_DIM), F32),
                   jax.ShapeDtypeStruct((n, GDN_HEADS, c, c), F32)),
        grid=(n,),
        in_specs=[row_spec, row_spec, row_spec, pl.BlockSpec((c, LANES), lambda i: (i, 0)), small, small],
        out_specs=(row_spec, pl.BlockSpec((None, GDN_HEADS, HEAD_DIM, HEAD_DIM), lambda i: (i, 0, 0, 0)),
                   pl.BlockSpec((None, GDN_HEADS, c, c), lambda i: (i, 0, 0, 0))),
        scratch_shapes=[pltpu.VMEM((GDN_HEADS, HEAD_DIM, HEAD_DIM), F32)],
        compiler_params=_params(("arbitrary",)),
    )(q, k, v, ba, a_log, dt_bias)


def _gdn_scan_bwd(q, k, v, ba, a_log, dt_bias, states, inverses, do, *, name):
    length = q.shape[0]
    n = length // GDN_CHUNK
    c = GDN_CHUNK

    def body(q_ref, k_ref, v_ref, ba_ref, alog_ref, dtb_ref, s_ref, t_ref, do_ref,
             dq_ref, dk_ref, dv_ref, dba_ref, dalog_ref, ddtb_ref, dstate):
        i = pl.program_id(0)

        @pl.when(i == 0)
        def _():
            dstate[...] = jnp.zeros_like(dstate)
            dalog_ref[...] = jnp.zeros_like(dalog_ref)
            ddtb_ref[...] = jnp.zeros_like(ddtb_ref)

        bav = ba_ref[...]
        lane = lax.broadcasted_iota(jnp.int32, (c, LANES), 1)
        sub8 = lax.broadcasted_iota(jnp.int32, (GDN_HEADS, LANES), 0)
        lane8 = lax.broadcasted_iota(jnp.int32, (GDN_HEADS, LANES), 1)
        slab = jnp.zeros((c, LANES), F32)
        dalog_all = jnp.zeros((GDN_HEADS, LANES), F32)
        ddtb_all = jnp.zeros((GDN_HEADS, LANES), F32)
        heads = [slice(h * HEAD_DIM, (h + 1) * HEAD_DIM) for h in range(GDN_HEADS)]
        ds_in = [dstate[h] for h in range(GDN_HEADS)]
        s_in = [s_ref[h] for h in range(GDN_HEADS)]
        t_known = [t_ref[h] for h in range(GDN_HEADS)]
        _, vjp = jax.vjp(functools.partial(_gdn_chunk, t_known=t_known),
                         *_gdn_operands(q_ref, k_ref, v_ref, bav, alog_ref, dtb_ref), s_in)
        dq, dk, dv, dbl, dal, dalog, ddtb, ds = vjp(([do_ref[:, sl] for sl in heads], ds_in))
        for h, sl in enumerate(heads):
            dq_ref[:, sl] = dq[h]
            dk_ref[:, sl] = dk[h]
            dv_ref[:, sl] = dv[h]
            dstate[h] = ds[h]
            slab = slab + jnp.where(lane == h, dbl[h], 0.0) + jnp.where(lane == h + GDN_HEADS, dal[h], 0.0)
            here = (sub8 == h) & (lane8 == 0)
            dalog_all = dalog_all + jnp.where(here, dalog[h], 0.0)
            ddtb_all = ddtb_all + jnp.where(here, ddtb[h], 0.0)
        dba_ref[...] = slab
        dalog_ref[...] += dalog_all
        ddtb_ref[...] += ddtb_all

    row_spec = pl.BlockSpec((c, D_MODEL), lambda i: (n - 1 - i, 0))
    small = pl.BlockSpec((GDN_HEADS, LANES), lambda i: (0, 0))
    return pl.pallas_call(
        body, name=name,
        out_shape=(jax.ShapeDtypeStruct((length, D_MODEL), F32),) * 3
        + (jax.ShapeDtypeStruct((length, LANES), F32),
           jax.ShapeDtypeStruct((GDN_HEADS, LANES), F32), jax.ShapeDtypeStruct((GDN_HEADS, LANES), F32)),
        grid=(n,),
        in_specs=[row_spec, row_spec, row_spec,
                  pl.BlockSpec((c, LANES), lambda i: (n - 1 - i, 0)), small, small,
                  pl.BlockSpec((None, GDN_HEADS, HEAD_DIM, HEAD_DIM), lambda i: (n - 1 - i, 0, 0, 0)),
                  pl.BlockSpec((None, GDN_HEADS, c, c), lambda i: (n - 1 - i, 0, 0, 0)),
                  row_spec],
        out_specs=(row_spec, row_spec, row_spec,
                   pl.BlockSpec((c, LANES), lambda i: (n - 1 - i, 0)), small, small),
        scratch_shapes=[pltpu.VMEM((GDN_HEADS, HEAD_DIM, HEAD_DIM), F32)],
        compiler_params=_params(("arbitrary",)),
    )(q, k, v, ba, a_log, dt_bias, states, inverses, do)


S5_W = S5_T * LANES
S5_S = 2 * 8 * S5_STATE
S5_SH = S5_S // 2


def _iota2(shape):
    return lax.broadcasted_iota(jnp.int32, shape, 0), lax.broadcasted_iota(jnp.int32, shape, 1)


def _s5_rep_t(t, dtype):
    row, col = _iota2((S5_T * S5_GROUP, LANES))
    return ((jnp.right_shift(row, 4) == t) & (jnp.bitwise_and(row, 15) == jnp.bitwise_and(col, 15))).astype(dtype)


def _s5_rep_state(dtype):
    row, col = _iota2((2 * S5_STATE, S5_S))
    return ((jnp.right_shift(row, 6) == jnp.right_shift(col, 9))
            & (jnp.bitwise_and(row, 63) == jnp.bitwise_and(col, 63))).astype(dtype)


def _s5_masks():
    row, col = _iota2((LANES, LANES))
    m_ab = jnp.right_shift(row, 4) == jnp.right_shift(col, 4)
    row, col = _iota2((S5_S, LANES))
    m_e = jnp.bitwise_and(jnp.right_shift(row, 6), 7) == jnp.right_shift(col, 4)
    row, col = _iota2((LANES, S5_S))
    m_f = jnp.right_shift(row, 4) == jnp.bitwise_and(jnp.right_shift(col, 6), 7)
    return m_ab, m_e, m_f


def _s5_expand(kx_ref, ec_ref, fc_ref, kb_scr, e_scr, f_scr):
    m_ab, m_e, m_f = _s5_masks()
    kx = kx_ref[...].astype(BF16)
    ec = ec_ref[...].astype(BF16)
    rep_state = _s5_rep_state(BF16)
    for t in range(S5_T):
        rep = _s5_rep_t(t, BF16)
        cols = slice(t * LANES, (t + 1) * LANES)
        kb_scr[t] = jnp.where(m_ab, jnp.dot(kx, rep, preferred_element_type=F32), 0.0).astype(BF16)
        e_scr[:, cols] = jnp.where(m_e, jnp.dot(ec, rep, preferred_element_type=F32), 0.0).astype(BF16)
        f_scr[cols, :] = jnp.where(m_f, jnp.dot(fc_ref[t].astype(BF16), rep_state, preferred_element_type=F32),
                                   0.0).astype(BF16)


def _s5_token_rows(ref, n):
    return [ref[pl.ds(t, n, stride=S5_T), :].astype(BF16) for t in range(S5_T)]


def _s5_scan_fwd(u, kx, ec, fc, at, *, name):
    length = u.shape[0]
    n = length // S5_T
    assert n % SUBLANES == 0

    def body(u_ref, kx_ref, ec_ref, fc_ref, at_ref, y_ref, h_ref, kb_scr, e_scr, f_scr, g_scr):
        _s5_expand(kx_ref, ec_ref, fc_ref, kb_scr, e_scr, f_scr)
        us = _s5_token_rows(u_ref, n)
        g_scr[...] = jnp.dot(jnp.concatenate(us, axis=1), f_scr[...], preferred_element_type=F32)
        ar, ai = at_ref[:, :S5_SH], at_ref[:, S5_SH:]

        def step(blk, h):
            base = pl.multiple_of(blk * SUBLANES, SUBLANES)
            g8 = g_scr[pl.ds(base, SUBLANES), :]
            rows = []
            for r in range(SUBLANES):
                rows.append(h)
                hr, hi = h[:, :S5_SH], h[:, S5_SH:]
                h = jnp.concatenate([ar * hr - ai * hi, ar * hi + ai * hr], axis=1) + g8[r:r + 1, :]
            h_ref[pl.ds(base, SUBLANES), :] = jnp.concatenate(rows, axis=0)
            return h

        lax.fori_loop(0, n // SUBLANES, step, jnp.zeros((1, S5_S), F32))
        hb = h_ref[...].astype(BF16)
        for t in range(S5_T):
            acc = jnp.dot(hb, e_scr[:, t * LANES:(t + 1) * LANES], preferred_element_type=F32)
            for s in range(t + 1):
                acc = acc + jnp.dot(us[s], kb_scr[t - s], preferred_element_type=F32)
            y_ref[pl.ds(t, n, stride=S5_T), :] = acc

    return pl.pallas_call(
        body, name=name,
        out_shape=(jax.ShapeDtypeStruct((length, D_MODEL), F32), jax.ShapeDtypeStruct((S5_TILES, n, S5_S), F32)),
        grid=(S5_TILES,),
        in_specs=[pl.BlockSpec((length, LANES), lambda k: (0, k)), _s5_spec(LANES, S5_T * S5_GROUP),
                  _s5_spec(S5_S, S5_T * S5_GROUP), _s5_spec(S5_T, LANES, LANES), _s5_spec(1, S5_S)],
        out_specs=(pl.BlockSpec((length, LANES), lambda k: (0, k)), _s5_spec(n, S5_S)),
        scratch_shapes=[pltpu.VMEM((S5_T, LANES, LANES), BF16), pltpu.VMEM((S5_S, S5_W), BF16),
                        pltpu.VMEM((S5_W, S5_S), BF16), pltpu.VMEM((n, S5_S), F32)],
        compiler_params=_params(("parallel",)),
    )(u, kx, ec, fc, at)


def _s5_spec(*tail):
    return pl.BlockSpec((None,) + tail, lambda k: (k,) + (0,) * len(tail))


def _s5_scan_bwd(dy, kx, ec, fc, at, hs, *, name):
    length = dy.shape[0]
    n = length // S5_T

    def body(dy_ref, kx_ref, ec_ref, fc_ref, at_ref, h_ref, du_ref, dg_ref, dat_ref, kb_scr, e_scr, f_scr, dh_scr):
        _s5_expand(kx_ref, ec_ref, fc_ref, kb_scr, e_scr, f_scr)
        dys = _s5_token_rows(dy_ref, n)
        dh_scr[...] = _dot(jnp.concatenate(dys, axis=1), e_scr[...], NT)
        ar, ai = at_ref[:, :S5_SH], at_ref[:, S5_SH:]

        def step(it, carry):
            cy, dat = carry
            base = pl.multiple_of((n // SUBLANES - 1 - it) * SUBLANES, SUBLANES)
            dh8 = dh_scr[pl.ds(base, SUBLANES), :]
            h8 = h_ref[pl.ds(base, SUBLANES), :]
            rows = [None] * SUBLANES
            for r in reversed(range(SUBLANES)):
                rows[r] = cy
                cr, ci = cy[:, :S5_SH], cy[:, S5_SH:]
                hr, hi = h8[r:r + 1, :S5_SH], h8[r:r + 1, S5_SH:]
                dat = dat + jnp.concatenate([cr * hr + ci * hi, ci * hr - cr * hi], axis=1)
                cy = dh8[r:r + 1, :] + jnp.concatenate([ar * cr + ai * ci, ar * ci - ai * cr], axis=1)
            dg_ref[pl.ds(base, SUBLANES), :] = jnp.concatenate(rows, axis=0)
            return cy, dat

        zero = jnp.zeros((1, S5_S), F32)
        _, dat = lax.fori_loop(0, n // SUBLANES, step, (zero, zero))
        dat_ref[...] = dat
        dgb = dg_ref[...].astype(BF16)
        for s in range(S5_T):
            acc = _dot(dgb, f_scr[s * LANES:(s + 1) * LANES, :], NT)
            for t in range(s, S5_T):
                acc = acc + _dot(dys[t], kb_scr[t - s], NT)
            du_ref[pl.ds(s, n, stride=S5_T), :] = acc

    row_spec = pl.BlockSpec((length, LANES), lambda k: (0, k))
    return pl.pallas_call(
        body, name=name,
        out_shape=(jax.ShapeDtypeStruct((length, D_MODEL), F32), jax.ShapeDtypeStruct((S5_TILES, n, S5_S), F32),
                   jax.ShapeDtypeStruct((S5_TILES, 1, S5_S), F32)),
        grid=(S5_TILES,),
        in_specs=[row_spec, _s5_spec(LANES, S5_T * S5_GROUP), _s5_spec(S5_S, S5_T * S5_GROUP),
                  _s5_spec(S5_T, LANES, LANES), _s5_spec(1, S5_S), _s5_spec(n, S5_S)],
        out_specs=(row_spec, _s5_spec(n, S5_S), _s5_spec(1, S5_S)),
        scratch_shapes=[pltpu.VMEM((S5_T, LANES, LANES), BF16), pltpu.VMEM((S5_S, S5_W), BF16),
                        pltpu.VMEM((S5_W, S5_S), BF16), pltpu.VMEM((n, S5_S), F32)],
        compiler_params=_params(("parallel",)),
    )(dy, kx, ec, fc, at, hs)


def _s5_operator_grads(dy, u, hs, dg, *, name):
    length = u.shape[0]
    n = length // S5_T

    def body(dy_ref, u_ref, h_ref, dg_ref, dkx_ref, dec_ref, dfc_ref):
        dys = _s5_token_rows(dy_ref, n)
        us = _s5_token_rows(u_ref, n)
        ucat = jnp.concatenate(us, axis=1)
        m_ab, m_e, m_f = _s5_masks()
        hb = h_ref[...].astype(BF16)
        dgb = dg_ref[...].astype(BF16)
        lane = lax.broadcasted_iota(jnp.int32, (1, LANES), 1)
        lane_group = jnp.right_shift(lane, 4)

        def own_block(x, mask):
            x = jnp.where(mask, x, 0.0)
            for shift in (64, 32, 16):
                x = x + pltpu.roll(x, shift, 1)
            return x

        def place(halves, t, x):
            halves[t // 8] = jnp.where(lane_group == t % 8, x, halves[t // 8])

        dkb = [jnp.zeros((LANES, LANES), F32) for _ in range(S5_T)]
        dec = [jnp.zeros((S5_S, LANES), F32) for _ in range(2)]
        for t in range(S5_T):
            d_t = _dot(ucat, dys[t], TN)
            for s in range(t + 1):
                dkb[t - s] = dkb[t - s] + d_t[s * LANES:(s + 1) * LANES, :]
            place(dec, t, own_block(_dot(hb, dys[t], TN), m_e))
            wide = jnp.where(m_f, _dot(us[t], dgb, TN), 0.0)
            parts = []
            for r in range(2):
                acc = wide[:, r * S5_SH:r * S5_SH + LANES]
                for q in range(1, S5_SH // LANES):
                    acc = acc + wide[:, r * S5_SH + q * LANES:r * S5_SH + (q + 1) * LANES]
                parts.append(acc + pltpu.roll(acc, S5_STATE, 1))
            dfc_ref[t] = jnp.where(lane < S5_STATE, parts[0], parts[1])
        dkx = [jnp.zeros((LANES, LANES), F32) for _ in range(2)]
        for t in range(S5_T):
            place(dkx, t, own_block(dkb[t], m_ab))
        dkx_ref[...] = jnp.concatenate(dkx, axis=1)
        dec_ref[...] = jnp.concatenate(dec, axis=1)

    row_spec = pl.BlockSpec((length, LANES), lambda k: (0, k))
    outs = (_s5_spec(LANES, S5_T * S5_GROUP), _s5_spec(S5_S, S5_T * S5_GROUP), _s5_spec(S5_T, LANES, LANES))
    return pl.pallas_call(
        body, name=name,
        out_shape=(jax.ShapeDtypeStruct((S5_TILES, LANES, S5_T * S5_GROUP), F32),
                   jax.ShapeDtypeStruct((S5_TILES, S5_S, S5_T * S5_GROUP), F32),
                   jax.ShapeDtypeStruct((S5_TILES, S5_T, LANES, LANES), F32)),
        grid=(S5_TILES,),
        in_specs=[row_spec, row_spec, _s5_spec(n, S5_S), _s5_spec(n, S5_S)],
        out_specs=outs,
        compiler_params=_params(("parallel",)),
    )(dy, u, hs, dg)


def _s5_prep(a_re, a_im, b_re, b_im, c_re, c_im, log_dt):
    t_len, tiles = S5_T, S5_TILES
    dt = jnp.exp(log_dt)[:, None]
    mag = jnp.exp(a_re * dt)
    ab_re, ab_im = mag * jnp.cos(a_im * dt), mag * jnp.sin(a_im * dt)
    den = jnp.square(a_re) + jnp.square(a_im)
    n_re, n_im = ab_re - 1.0, ab_im
    f_re = (n_re * a_re + n_im * a_im) / den
    f_im = (n_im * a_re - n_re * a_im) / den
    bb_re = f_re[..., None] * b_re - f_im[..., None] * b_im
    bb_im = f_re[..., None] * b_im + f_im[..., None] * b_re

    def powers(exponents):
        e = exponents[:, None, None]
        m = jnp.exp(e * (a_re * dt))
        return m * jnp.cos(e * (a_im * dt)), m * jnp.sin(e * (a_im * dt))

    p_re, p_im = powers(jnp.arange(t_len + 1, dtype=F32))
    rev_re, rev_im = powers((t_len - 1) - jnp.arange(t_len, dtype=F32))
    ca_re = c_re[None] * p_re[:, :, None, :] - c_im[None] * p_im[:, :, None, :]
    ca_im = c_re[None] * p_im[:, :, None, :] + c_im[None] * p_re[:, :, None, :]
    lag = (jnp.einsum('tgip,gpj->tgij', ca_re[:t_len], bb_re, precision=HI)
           - jnp.einsum('tgip,gpj->tgij', ca_im[:t_len], bb_im, precision=HI))
    kx = lag.reshape(t_len, tiles, 8, S5_GROUP, S5_GROUP).transpose(1, 2, 4, 0, 3)
    kx = kx.reshape(tiles, LANES, t_len * S5_GROUP)
    e_st = jnp.stack([ca_re[1:], -ca_im[1:]])
    e_st = e_st.reshape(2, t_len, tiles, 8, S5_GROUP, S5_STATE).transpose(2, 0, 3, 5, 1, 4)
    ec = e_st.reshape(tiles, S5_S, t_len * S5_GROUP)
    ab_b = jnp.stack([rev_re[..., None] * bb_re[None] - rev_im[..., None] * bb_im[None],
                      rev_re[..., None] * bb_im[None] + rev_im[..., None] * bb_re[None]])
    ab_b = ab_b.reshape(2, t_len, tiles, 8, S5_STATE, S5_GROUP).transpose(2, 1, 3, 5, 0, 4)
    fc = ab_b.reshape(tiles, t_len, LANES, 2 * S5_STATE)
    a_t = jnp.stack([p_re[t_len], p_im[t_len]]).reshape(2, tiles, 8 * S5_STATE).transpose(1, 0, 2)
    return kx, ec, fc, a_t.reshape(tiles, 1, S5_S)


TM_ROW = 1024
TM_WIDE = 512


def _gdn_fwd(x, w, tag):
    qkv = _mm(x, w["wqkv"], name="gdn_proj_qkv")
    z = _mm(x, w["wz"], name="gdn_proj_z")
    ba = _mm(x, w["wba"], name="gdn_proj_ba")
    cv = _conv_fwd(qkv, w["conv_w"], tm=TM_ROW, name="gdn_conv")
    q, k, v = _rw_fwd(_f_gdn_qkv, [cv], [], tm=TM_WIDE, name="gdn_qkv")
    o, states, inverses = _gdn_scan_fwd(q, k, v, ba, w["a_log8"], w["dt_bias8"], name="gdn_scan")
    (mix,) = _rw_fwd(_f_gdn_out, [o, z], [w["norm_g"]], tm=TM_ROW, name="gdn_out", out_dtypes=[BF16])
    return mix, (qkv, z, ba, cv, q, k, v, states, inverses, o)


def _gdn_bwd(x, w, saved, dmix, dx_acc, token=None):
    qkv, z, ba, cv, q, k, v, states, inverses, o = saved
    norm_g = w["norm_g"] if token is None else w["norm_g"] + token[0, 0]
    (do, dz), (dnorm_g,) = _rw_bwd(_f_gdn_out, [o, z], [norm_g], [dmix], row_grad=[1, 1], param_grad=[1],
                                   tm=TM_ROW, name="gdn_out_bwd", row_dtypes=[F32, BF16])
    dq, dk, dv, dba, dalog, ddtb = _gdn_scan_bwd(q, k, v, ba, w["a_log8"], w["dt_bias8"], states, inverses, do,
                                                  name="gdn_scan_bwd")
    (dcv,), _ = _rw_bwd(_f_gdn_qkv, [cv], [], [dq, dk, dv], row_grad=[1], param_grad=[], tm=TM_WIDE,
                        name="gdn_qkv_bwd")
    dqkv, dconv_w = _conv_bwd(qkv, w["conv_w"], dcv, tm=TM_ROW, name="gdn_conv_bwd")
    dx = _mm(dqkv, w["wqkv"], tb=True, acc=dx_acc, name="gdn_dx_qkv")
    dx = _mm(dz, w["wz"], tb=True, acc=dx, name="gdn_dx_z")
    dx = _mm(dba, w["wba"], tb=True, acc=dx, name="gdn_dx_ba")
    grads = dict(wqkv=_mm(x, dqkv, ta=True, name="gdn_dw_qkv"), wz=_mm(x, dz, ta=True, name="gdn_dw_z"),
                 wba=_mm(x, dba, ta=True, name="gdn_dw_ba"), conv_w=dconv_w,
                 a_log=dalog[:, 0], dt_bias=ddtb[:, 0], norm_g=dnorm_g[0])
    return dx, grads


def _s5_fwd(x, w, tag):
    u = _mm(x, w["wu"], name="s5_proj_u")
    y, hs = _s5_scan_fwd(u, w["kx"], w["ec"], w["fc"], w["a_t"], name="s5_scan")
    (zg,) = _rw_fwd(_f_s5_gelu, [y, u], [w["d"]], tm=TM_ROW, name="s5_gelu")
    t = _mm(zg, w["w_glu"], name="s5_glu")
    (mix,) = _rw_fwd(_f_s5_gate, [zg, t], [w["b_glu"]], tm=TM_ROW, name="s5_gate", out_dtypes=[BF16])
    return mix, (u, hs, y, zg, t)


def _s5_bwd(x, w, saved, dmix, dx_acc, token=None):
    u, hs, y, zg, t = saved
    b_glu = w["b_glu"] if token is None else w["b_glu"] + token[0, 0]
    (dzg, dt), (db_glu,) = _rw_bwd(_f_s5_gate, [zg, t], [b_glu], [dmix], row_grad=[1, 1], param_grad=[1],
                                   tm=TM_ROW, name="s5_gate_bwd", row_dtypes=[F32, BF16])
    dzg = _mm(dt, w["w_glu"], tb=True, acc=dzg, name="s5_dzg")
    dw_glu = _mm(zg, dt, ta=True, name="s5_dw_glu")
    (dy, du), (dd,) = _rw_bwd(_f_s5_gelu, [y, u], [w["d"]], [dzg], row_grad=[1, 1], param_grad=[1],
                              tm=TM_ROW, name="s5_gelu_bwd")
    du_scan, dg, dat = _s5_scan_bwd(dy, w["kx"], w["ec"], w["fc"], w["a_t"], hs, name="s5_scan_bwd")
    dkx, dec, dfc = _s5_operator_grads(dy, u, hs, dg, name="s5_operator_grads")
    (du,) = _rw_fwd(_f_add, [du, du_scan], [], tm=TM_ROW, name="s5_du_add", out_dtypes=[BF16])
    d_a_re, d_a_im, d_b_re, d_b_im, d_c_re, d_c_im, d_log_dt = w["prep_vjp"]((dkx, dec, dfc, dat))
    dx = _mm(du, w["wu"], tb=True, acc=dx_acc, name="s5_dx_u")
    grads = dict(wu=_mm(x, du, ta=True, name="s5_dw_u"), w_glu=dw_glu, b_glu=db_glu[0], d=dd[0],
                 a_re=d_a_re, a_im=d_a_im, b_re=d_b_re, b_im=d_b_im, c_re=d_c_re, c_im=d_c_im, log_dt=d_log_dt)
    return dx, grads


def _ln_res_both(x, h, g, b):
    (y,) = _f_ln_res(x, h, g, b)
    return y, y


def _layer_fwd(x, xb, mem, w, is_gdn):
    mix, msave = (_gdn_fwd if is_gdn else _s5_fwd)(xb, w, "")
    xq = _mm(xb, w["wxq"], name="proj_xq")
    kv = _mm(mem, w["wkv"], name="mem_kv")
    kmem, vmem = kv[:, :XA_DIM], kv[:, XA_DIM:]
    (cross,) = _rw_fwd(_f_attn, [xq], [kmem, vmem], tm=TM_ROW, name="attn", out_dtypes=[BF16])
    h = _mm(mix, w["wo_mix"], name="wo_mix")
    h = _mm(cross, w["wo_cross"], acc=h, name="wo_cross")
    x1, x1b = _rw_fwd(_ln_res_both, [x, h], [w["ln1_g"], w["ln1_b"]], tm=TM_ROW, name="ln_res",
                      out_dtypes=[F32, BF16])
    hm, act = _mm_relu2(x1b, w["w1"], name="mlp_up")
    f = _mm(act, w["w2"], name="mlp_down")
    x2, x2b = _rw_fwd(_ln_res_both, [x1, f], [w["ln2_g"], w["ln2_b"]], tm=TM_ROW, name="ln_res",
                      out_dtypes=[F32, BF16])
    return x2, x2b, (x, xb, msave, xq, kmem, vmem, mix, cross, h, x1, x1b, hm, act, f)


def _layer_bwd(mem, w, is_gdn, saved, dx2, token=None, before_mixer=None):
    x, xb, msave, xq, kmem, vmem, mix, cross, h, x1, x1b, hm, act, f = saved
    ln2_g = w["ln2_g"] if token is None else w["ln2_g"] + token[0, 0]
    (dx1, df), (dg2, db2) = _rw_bwd(_f_ln_res, [x1, f], [ln2_g, w["ln2_b"]], [dx2], row_grad=[1, 1],
                                    param_grad=[1, 1], tm=TM_ROW, name="ln_res_bwd", row_dtypes=[F32, BF16])
    dhm = _mm_relu2_grad(df, w["w2"], hm, name="mlp_dhm")
    dw2 = _mm(act, df, ta=True, name="mlp_dw2")
    dx1 = _mm(dhm, w["w1"], tb=True, acc=dx1, name="mlp_dx")
    dw1 = _mm(x1b, dhm, ta=True, out_blocks=N_CHIPS, name="mlp_dw1")
    (dx, dh), (dg1, db1) = _rw_bwd(_f_ln_res, [x, h], [w["ln1_g"], w["ln1_b"]], [dx1], row_grad=[1, 1],
                                   param_grad=[1, 1], tm=TM_ROW, name="ln_res_bwd", row_dtypes=[F32, BF16])
    dmix =_mm(dh, w["wo_mix"], tb=True, name="wo_dmix")
    dcross = _mm(dh, w["wo_cross"], tb=True, name="wo_dcross")
    dwo = jnp.concatenate([_mm(mix, dh, ta=True, name="wo_dw_mix"), _mm(cross, dh, ta=True, name="wo_dw_cross")], 0)
    (dxq,), (dkmem, dvmem) = _rw_bwd(_f_attn, [xq], [kmem, vmem], [dcross], row_grad=[1], param_grad=[1, 1],
                                     tm=TM_ROW, name="attn_bwd", row_dtypes=[BF16])
    dwkv = _mm(mem, jnp.concatenate([dkmem, dvmem], axis=1), ta=True, name="mem_dw_kv")
    dx = _mm(dxq, w["wxq"], tb=True, acc=dx, name="dx_xq")
    dwxq = _mm(xb, dxq, ta=True, name="dw_xq")
    mixer_token = None if before_mixer is None else before_mixer(dict(w_kv_mem=dwkv, w_o=dwo, mlp_w1=dw1, mlp_w2=dw2))
    dx, mg = (_gdn_bwd if is_gdn else _s5_bwd)(xb, w, msave, dmix, dx, mixer_token)
    grads = dict(mixer=mg, wxq=dwxq, wkv=dwkv, wo=dwo, w1=dw1, w2=dw2,
                 ln1_g=dg1[0], ln1_b=db1[0], ln2_g=dg2[0], ln2_b=db2[0])
    return dx, grads


def _loss_and_grad(y, target):
    def f(yv, tv):
        err = yv - tv
        return (err * (1.0 / D_MODEL),), (0.5 / D_MODEL * jnp.sum(err * err, axis=0, keepdims=True),)

    (dy,), (part,) = _rowwise(f, [y, target], [], [(D_MODEL, F32)], [((1, D_MODEL), F32)], tm=TM_ROW, name="loss")
    return jnp.sum(part), dy


def _layer_weights(full, i):
    j = i // 2
    w = dict(wkv=full["w_kv_mem"][i].astype(BF16),
             wo_mix=full["w_o"][i][:D_MODEL].astype(BF16), wo_cross=full["w_o"][i][D_MODEL:].astype(BF16),
             ln1_g=full["ln1_g"][i][None], ln1_b=full["ln1_b"][i][None],
             ln2_g=full["ln2_g"][i][None], ln2_b=full["ln2_b"][i][None],
             w1=full["mlp_w1"][i].astype(BF16), w2=full["mlp_w2"][i].astype(BF16))
    if i % 2 == 0:
        w_in = full["gdn_w_in"][j]
        gd = 3 * D_MODEL
        w.update(wqkv=w_in[:, :gd].astype(BF16), wz=w_in[:, gd:gd + D_MODEL].astype(BF16),
                 wba=jnp.pad(w_in[:, gd + D_MODEL:gd + D_MODEL + 2 * GDN_HEADS],
                             ((0, 0), (0, LANES - 2 * GDN_HEADS))).astype(BF16),
                 wxq=w_in[:, gd + D_MODEL + 2 * GDN_HEADS:].astype(BF16),
                 conv_w=full["gdn_conv_w"][j],
                 a_log8=jnp.broadcast_to(full["gdn_a_log"][j][:, None], (GDN_HEADS, LANES)),
                 dt_bias8=jnp.broadcast_to(full["gdn_dt_bias"][j][:, None], (GDN_HEADS, LANES)),
                 norm_g=full["gdn_norm_g"][j][None])
    else:
        w_in = full["s5_w_in"][j]
        (kx, ec, fc, a_t), prep_vjp = jax.vjp(
            _s5_prep, full["s5_a_re"][j], full["s5_a_im"][j], full["s5_b_re"][j], full["s5_b_im"][j],
            full["s5_c_re"][j], full["s5_c_im"][j], full["s5_log_dt"][j])
        w.update(wu=w_in[:, :D_MODEL].astype(BF16), wxq=w_in[:, D_MODEL:].astype(BF16),
                 kx=kx, ec=ec, fc=fc, a_t=a_t, prep_vjp=prep_vjp,
                 d=full["s5_d"][j][None], w_glu=full["s5_w_glu"][j].astype(BF16), b_glu=full["s5_b_glu"][j][None])
    return w


def _sharded_grads(l, i):
    m = l["mixer"]
    out = dict(w_kv_mem=l["wkv"], w_o=l["wo"], mlp_w1=l["w1"], mlp_w2=l["w2"])
    if i % 2 == 0:
        out.update(gdn_w_in=jnp.concatenate([m["wqkv"], m["wz"], m["wba"][:, :2 * GDN_HEADS], l["wxq"]], axis=1),
                   gdn_conv_w=m["conv_w"])
    else:
        out.update(s5_w_in=jnp.concatenate([m["wu"], l["wxq"]], axis=1), s5_d=m["d"], s5_w_glu=m["w_glu"],
                   s5_b_glu=m["b_glu"])
    return out


def _replicated_grads(layer_grads):
    g = layer_grads
    gdn = [g[i]["mixer"] for i in range(DEPTH) if i % 2 == 0]
    s5 = [g[i]["mixer"] for i in range(DEPTH) if i % 2 == 1]
    out = {n: jnp.stack([l[n] for l in g]) for n in ("ln1_g", "ln1_b", "ln2_g", "ln2_b")}
    out.update({"gdn_" + n: jnp.stack([m[n] for m in gdn]) for n in ("a_log", "dt_bias", "norm_g")})
    out.update({"s5_" + n: jnp.stack([m[n] for m in s5])
                for n in ("a_re", "a_im", "b_re", "b_im", "c_re", "c_im", "log_dt")})
    return out


def _local_step(x, mem, target, weights_of, grads_ready, before_first_mixer):
    lw, saves = [], []
    h, hb = x, x.astype(BF16)
    for i in range(DEPTH):
        lw.append(weights_of(i, h))
        h, hb, s = _layer_fwd(h, hb, mem, lw[i], i % 2 == 0)
        saves.append(s)
    loss, d = _loss_and_grad(h, target)
    grads = [None] * DEPTH
    token = None
    for i in reversed(range(DEPTH)):
        d, grads[i] = _layer_bwd(mem, lw[i], i % 2 == 0, saves[i], d, token, None if i else before_first_mixer)
        token = grads_ready(i, grads[i])
    return loss, d, grads


ANY = pl.BlockSpec(memory_space=pl.ANY)
SHARD_ROWS = 1024
SMALL_ROWS = 128


def _place():
    return lax.axis_index("x"), lax.axis_index("y"), lax.axis_index("c")


def _other_chips(x, y):
    return [(1 - x, y), (x, 1 - y), (1 - x, 1 - y)]


def _all_gather_chips(wpack, *, name):
    rows = wpack.shape[0]
    half = rows // 2

    def body(w_ref, out_ref, send_sems, recv_sems):
        x, y, c = _place()
        sibling = (x, y, 1 - c)
        chips = _other_chips(x, y)

        def blk(cx, cy, cc):
            return out_ref.at[2 * cx + cy, pl.ds(cc * half, half), :]

        def copy(k, src, dst, to):
            return pltpu.make_async_remote_copy(src_ref=src, dst_ref=dst, send_sem=send_sems.at[k],
                                                recv_sem=recv_sems.at[k], device_id=to, device_id_type=MESH)

        first = [copy(j, w_ref.at[pl.ds(c * half, half), :], blk(x, y, c), (cx, cy, c))
                 for j, (cx, cy) in enumerate(chips)]
        for cp in first:
            cp.start()
        passed = [copy(3 + j, blk(cx, cy, c), blk(cx, cy, c), sibling) for j, (cx, cy) in enumerate(chips)]
        for j, (cx, cy) in enumerate(chips):
            copy(j, blk(cx, cy, c), blk(cx, cy, c), (cx, cy, c)).wait_recv()
            passed[j].start()
        for j, (cx, cy) in enumerate(chips):
            copy(3 + j, blk(cx, cy, 1 - c), blk(cx, cy, 1 - c), sibling).wait_recv()
        for cp in first + passed:
            cp.wait_send()

    return pl.pallas_call(
        body, name=name, out_shape=jax.ShapeDtypeStruct((N_CHIPS, rows, D_MODEL), wpack.dtype),
        in_specs=[ANY], out_specs=ANY,
        scratch_shapes=[pltpu.SemaphoreType.DMA((6,)), pltpu.SemaphoreType.DMA((6,))],
    )(wpack)


HBM = pl.BlockSpec(memory_space=pltpu.HBM)
SEM = pl.BlockSpec(memory_space=pltpu.SEMAPHORE)
DATAFLOW = pltpu.SideEffectType.DATAFLOW_SIDE_EFFECTING


def _gather_ici_copies(w_ref, land_ref, send_sems, recv_sems, outgoing):
    x, y, c = _place()
    half = w_ref.shape[0] // 2
    mine = pl.ds(c * half, half)
    return [pltpu.make_async_remote_copy(
        src_ref=w_ref.at[mine, :], dst_ref=land_ref.at[2 * x + y if outgoing else 2 * cx + cy, mine, :],
        send_sem=send_sems.at[j], recv_sem=recv_sems.at[j], device_id=(cx, cy, c), device_id_type=MESH)
        for j, (cx, cy) in enumerate(_other_chips(x, y))]


def _gather_start(wpack, after):
    rows = wpack.shape[0]

    def body(w_ref, land_ref, after_ref, send_sems, recv_sems, w_thru, land_thru, token):
        for cp in _gather_ici_copies(w_ref, land_ref, send_sems, recv_sems, outgoing=True):
            cp.start()
        token[...] = jnp.zeros_like(token)

    land = pltpu.with_memory_space_constraint(lax.empty((N_CHIPS, rows, D_MODEL), wpack.dtype), pltpu.HBM)
    return pl.pallas_call(
        body, name="gather_start",
        out_shape=(pltpu.SemaphoreType.DMA((3,)), pltpu.SemaphoreType.DMA((3,)), pltpu.HBM(wpack.shape, wpack.dtype),
                   pltpu.HBM(land.shape, land.dtype), jax.ShapeDtypeStruct((SUBLANES, LANES), F32)),
        in_specs=(HBM, HBM, ANY), out_specs=(SEM, SEM, HBM, HBM, pl.BlockSpec(memory_space=pltpu.VMEM)),
        input_output_aliases={0: 2, 1: 3},
        compiler_params=pltpu.CompilerParams(has_side_effects=DATAFLOW),
    )(pltpu.with_memory_space_constraint(wpack, pltpu.HBM), land, after)


def _gather_wait(send_sems, recv_sems, w_thru, land_thru, after):
    def body(w_ref, land_ref, send_sems, recv_sems, after_ref, w_dead, land_out):
        for cp in _gather_ici_copies(w_ref, land_ref, send_sems, recv_sems, outgoing=False):
            cp.wait_send()
            cp.wait_recv()

    return pl.pallas_call(
        body, name="gather_wait",
        out_shape=(pltpu.HBM(w_thru.shape, w_thru.dtype), pltpu.HBM(land_thru.shape, land_thru.dtype)),
        in_specs=(HBM, HBM, SEM, SEM, ANY), out_specs=(HBM, HBM), input_output_aliases={0: 0, 1: 1},
        compiler_params=pltpu.CompilerParams(has_side_effects=DATAFLOW),
    )(w_thru, land_thru, send_sems, recv_sems, after)[1]


def _gather_forward(land, *, name):
    rows = land.shape[1]
    half = rows // 2

    def body(in_ref, out_ref, send_sems, recv_sems):
        x, y, c = _place()

        def copy(j, cx, cy, cc):
            rows_of = out_ref.at[2 * cx + cy, pl.ds(cc * half, half), :]
            return pltpu.make_async_remote_copy(src_ref=rows_of, dst_ref=rows_of, send_sem=send_sems.at[j],
                                                recv_sem=recv_sems.at[j], device_id=(x, y, 1 - c), device_id_type=MESH)

        sends = [copy(j, cx, cy, c) for j, (cx, cy) in enumerate(_other_chips(x, y))]
        for cp in sends:
            cp.start()
        for j, (cx, cy) in enumerate(_other_chips(x, y)):
            copy(j, cx, cy, 1 - c).wait_recv()
        for cp in sends:
            cp.wait_send()

    return pl.pallas_call(
        body, name=name, out_shape=jax.ShapeDtypeStruct(land.shape, land.dtype), in_specs=[ANY], out_specs=ANY,
        input_output_aliases={0: 0},
        scratch_shapes=[pltpu.SemaphoreType.DMA((3,)), pltpu.SemaphoreType.DMA((3,))],
    )(land)


def _sibling_swap(buf, *, name):
    def body(in_ref, out_ref, send_sem, recv_sem):
        x, y, c = _place()
        cp = pltpu.make_async_remote_copy(src_ref=in_ref, dst_ref=out_ref, send_sem=send_sem, recv_sem=recv_sem,
                                          device_id=(x, y, 1 - c), device_id_type=MESH)
        cp.start()
        cp.wait()

    return pl.pallas_call(
        body, name=name, out_shape=jax.ShapeDtypeStruct(buf.shape, buf.dtype), in_specs=[ANY], out_specs=ANY,
        scratch_shapes=[pltpu.SemaphoreType.DMA, pltpu.SemaphoreType.DMA],
    )(buf)


def _pair_exchange(gpack, *, name):
    pieces, rows, width = gpack.shape
    half = rows // 2

    def body(in_ref, got_ref, send_sems, recv_sems):
        x, y, c = _place()
        sends = [pltpu.make_async_remote_copy(src_ref=in_ref.at[p, pl.ds((1 - c) * half, half), :],
                                              dst_ref=got_ref.at[p], send_sem=send_sems.at[p],
                                              recv_sem=recv_sems.at[p], device_id=(x, y, 1 - c), device_id_type=MESH)
                 for p in range(pieces)]
        for cp in sends:
            cp.start()
        for cp in sends:
            cp.wait()

    return pl.pallas_call(
        body, name=name, out_shape=jax.ShapeDtypeStruct((pieces, half, width), gpack.dtype),
        in_specs=[ANY], out_specs=ANY,
        scratch_shapes=[pltpu.SemaphoreType.DMA((pieces,)), pltpu.SemaphoreType.DMA((pieces,))],
    )(gpack)


def _pair_add(gpack, got, c, *, name, tm=512):
    pieces, rows, width = gpack.shape
    half = rows // 2
    nb = half // tm

    def body(c_ref, a_ref, b_ref, sum_ref, narrow_ref):
        s = a_ref[...] + b_ref[...]
        sum_ref[...] = s
        narrow_ref[...] = s.astype(BF16)

    blk = pl.BlockSpec((None, tm, width), lambda p, i, c_ref: (p, i, 0))
    return pl.pallas_call(
        body, name=name,
        out_shape=(jax.ShapeDtypeStruct((pieces, half, width), F32), jax.ShapeDtypeStruct((pieces, half, width), BF16)),
        grid_spec=pltpu.PrefetchScalarGridSpec(
            num_scalar_prefetch=1, grid=(pieces, nb),
            in_specs=[pl.BlockSpec((None, tm, width), lambda p, i, c_ref: (p, c_ref[0] * nb + i, 0)), blk],
            out_specs=(blk, blk)),
        compiler_params=_params(("parallel", "parallel")),
    )(c, gpack, got)


def _chip_exchange(pieces, *, name):
    _, rows, width = pieces.shape

    def body(in_ref, out_ref, send_sems, recv_sems):
        x, y, c = _place()
        cps = [pltpu.make_async_remote_copy(src_ref=in_ref.at[2 * cx + cy], dst_ref=out_ref.at[j],
                                            send_sem=send_sems.at[j], recv_sem=recv_sems.at[j],
                                            device_id=(cx, cy, c), device_id_type=MESH)
               for j, (cx, cy) in enumerate(_other_chips(x, y))]
        for cp in cps:
            cp.start()
        for cp in cps:
            cp.wait()

    return pl.pallas_call(
        body, name=name, out_shape=jax.ShapeDtypeStruct((3, rows, width), pieces.dtype), in_specs=[ANY], out_specs=ANY,
        scratch_shapes=[pltpu.SemaphoreType.DMA((3,)), pltpu.SemaphoreType.DMA((3,))],
    )(pieces)


def _chip_exchange_copies(in_ref, land_ref, send_sems, recv_sems):
    x, y, c = _place()
    return [pltpu.make_async_remote_copy(src_ref=in_ref.at[2 * cx + cy], dst_ref=land_ref.at[j],
                                         send_sem=send_sems.at[j], recv_sem=recv_sems.at[j],
                                         device_id=(cx, cy, c), device_id_type=MESH)
            for j, (cx, cy) in enumerate(_other_chips(x, y))]


def _chip_exchange_start(pieces):
    _, rows, width = pieces.shape

    def body(in_ref, land_ref, send_sems, recv_sems, in_thru, land_thru, token):
        for cp in _chip_exchange_copies(in_ref, land_ref, send_sems, recv_sems):
            cp.start()
        token[...] = jnp.zeros_like(token)

    land = pltpu.with_memory_space_constraint(lax.empty((3, rows, width), pieces.dtype), pltpu.HBM)
    return pl.pallas_call(
        body, name="rs_chip_start",
        out_shape=(pltpu.SemaphoreType.DMA((3,)), pltpu.SemaphoreType.DMA((3,)), pltpu.HBM(pieces.shape, pieces.dtype),
                   pltpu.HBM(land.shape, land.dtype), jax.ShapeDtypeStruct((SUBLANES, LANES), F32)),
        in_specs=(HBM, HBM), out_specs=(SEM, SEM, HBM, HBM, pl.BlockSpec(memory_space=pltpu.VMEM)),
        input_output_aliases={0: 2, 1: 3},
        compiler_params=pltpu.CompilerParams(has_side_effects=DATAFLOW),
    )(pltpu.with_memory_space_constraint(pieces, pltpu.HBM), land)


def _chip_exchange_wait(send_sems, recv_sems, in_thru, land_thru, after):
    def body(in_ref, land_ref, send_sems, recv_sems, after_ref, in_dead, land_out):
        for cp in _chip_exchange_copies(in_ref, land_ref, send_sems, recv_sems):
            cp.wait_send()
            cp.wait_recv()

    return pl.pallas_call(
        body, name="rs_chip_wait",
        out_shape=(pltpu.HBM(in_thru.shape, in_thru.dtype), pltpu.HBM(land_thru.shape, land_thru.dtype)),
        in_specs=(HBM, HBM, SEM, SEM, ANY), out_specs=(HBM, HBM), input_output_aliases={0: 0, 1: 1},
        compiler_params=pltpu.CompilerParams(has_side_effects=DATAFLOW),
    )(in_thru, land_thru, send_sems, recv_sems, after)[1]


def _all_reduce_small(v, *, name):
    rows, width = v.shape
    half = rows // 2
    assert half % SUBLANES == 0

    def body(in_ref, out_ref, pair_buf, chip_buf, send_sems, recv_sems):
        x, y, c = _place()
        sibling = (x, y, 1 - c)
        me = 2 * x + y
        mine = pl.ds(pl.multiple_of(c * half, SUBLANES), half)
        other = pl.ds(pl.multiple_of((1 - c) * half, SUBLANES), half)

        def copy(k, src, dst, to):
            return pltpu.make_async_remote_copy(src_ref=src, dst_ref=dst, send_sem=send_sems.at[k],
                                                recv_sem=recv_sems.at[k], device_id=to, device_id_type=MESH)

        swap = copy(0, in_ref.at[other, :], pair_buf, sibling)
        swap.start()
        swap.wait()
        chip_buf[me] = in_ref[mine, :] + pair_buf[...]
        chips = _other_chips(x, y)
        for j, (cx, cy) in enumerate(chips):
            copy(1 + j, chip_buf.at[me], chip_buf.at[me], (cx, cy, c)).start()
        for j, (cx, cy) in enumerate(chips):
            got = copy(1 + j, chip_buf.at[me], chip_buf.at[2 * cx + cy], (cx, cy, c))
            got.wait_send()
            got.wait_recv()
        out_ref[mine, :] = ((chip_buf[0] + chip_buf[1]) + chip_buf[2]) + chip_buf[3]
        share = copy(1 + len(chips), out_ref.at[mine, :], out_ref.at[mine, :], sibling)
        share.start()
        share.wait_send()
        copy(1 + len(chips), out_ref.at[other, :], out_ref.at[other, :], sibling).wait_recv()

    vmem = pl.BlockSpec(memory_space=pltpu.VMEM)
    return pl.pallas_call(
        body, name=name, out_shape=jax.ShapeDtypeStruct(v.shape, v.dtype), in_specs=[vmem], out_specs=vmem,
        scratch_shapes=[pltpu.VMEM((half, width), v.dtype), pltpu.VMEM((N_CHIPS, half, width), v.dtype),
                        pltpu.SemaphoreType.DMA((5,)), pltpu.SemaphoreType.DMA((5,))],
        compiler_params=pltpu.CompilerParams(vmem_limit_bytes=VMEM_LIMIT_V7X),
    )(v)


def _reduce_scatter_begin(gpack, behind):
    x, y, c = _place()
    got = _pair_exchange(gpack, name="rs_pair_swap")
    pair, pair16 = _pair_add(gpack, got, c.astype(jnp.int32).reshape(1), name="rs_pair_add")
    mine = lax.dynamic_index_in_dim(pair, 2 * x + y, axis=0, keepdims=False)
    if behind:
        *in_flight, token = _chip_exchange_start(pair16)
        return dict(mine=mine, in_flight=in_flight), token
    return dict(mine=mine, recv=_chip_exchange(pair16, name="rs_chip_exchange")), None


def _reduce_scatter_end(state, after=None):
    c = lax.axis_index("c")
    recv = state["recv"] if "recv" in state else _chip_exchange_wait(*state["in_flight"], after=after)
    (total,) = _rw_fwd(_f_add4, [state["mine"], recv[0], recv[1], recv[2]], [], tm=512, name="rs_chip_add")
    theirs = _sibling_swap(total, name="rs_share_swap")
    return jnp.concatenate([jnp.where(c == 0, total, theirs), jnp.where(c == 0, theirs, total)], axis=0)


_SHARDED = (("w_kv_mem", 1), ("w_o", 1), ("mlp_w1", 2), ("mlp_w2", 1), ("gdn_w_in", 2), ("gdn_conv_w", 2),
            ("s5_w_in", 2), ("s5_d", 1), ("s5_w_glu", 1), ("s5_b_glu", 1))
_MATMUL_ONLY = ("w_kv_mem", "w_o", "mlp_w1", "mlp_w2", "gdn_w_in", "s5_w_in", "s5_w_glu")
_KEPT_BLOCKED = ("mlp_w1",)
_REPLICATED = ("ln1_g", "ln1_b", "ln2_g", "ln2_b", "gdn_a_log", "gdn_dt_bias", "gdn_norm_g", "s5_a_re", "s5_a_im",
               "s5_b_re", "s5_b_im", "s5_c_re", "s5_c_im", "s5_log_dt")
_WEIGHTS = ("w_kv_mem", "w_o", "ln1_g", "ln1_b", "ln2_g", "ln2_b", "mlp_w1", "mlp_w2", "gdn_w_in", "gdn_conv_w",
            "gdn_a_log", "gdn_dt_bias", "gdn_norm_g", "s5_w_in", "s5_a_re", "s5_a_im", "s5_b_re", "s5_b_im",
            "s5_c_re", "s5_c_im", "s5_log_dt", "s5_d", "s5_w_glu", "s5_b_glu")


ROW_ALIGN = 16


def _n_rows(shape):
    return -(-math.prod(shape) // (ROW_ALIGN * D_MODEL)) * ROW_ALIGN


def _as_rows(a):
    rows = _n_rows(a.shape)
    if a.shape[-1] == D_MODEL and a.size == rows * D_MODEL:
        return a.reshape(-1, D_MODEL)
    flat = a.reshape(-1)
    return jnp.pad(flat, (0, rows * D_MODEL - flat.size)).reshape(rows, D_MODEL)


def _pack(arrs, unit_rows=SHARD_ROWS):
    rows = [_as_rows(a) for a in arrs]
    pad = -sum(r.shape[0] for r in rows) % unit_rows
    if pad:
        rows.append(jnp.zeros((pad, D_MODEL), rows[0].dtype))
    return jnp.concatenate(rows, axis=0)


def _unpack(packed, shapes):
    lead = packed.shape[:-2]
    out, off = [], 0
    for s in shapes:
        r = _n_rows(s)
        seg = lax.slice_in_dim(packed, off, off + r, axis=len(lead))
        if s[-1] != D_MODEL or math.prod(s) != r * D_MODEL:
            seg = lax.slice_in_dim(seg.reshape(lead + (-1,)), 0, math.prod(s), axis=len(lead))
        out.append(seg.reshape(lead + tuple(s)))
        off += r
    return out


def _split3(t):
    hi = t.astype(BF16)
    r1 = t - hi.astype(F32)
    mid = r1.astype(BF16)
    lo = (r1 - mid.astype(F32)).astype(BF16)
    return jnp.stack([hi, mid, lo], axis=-1)


def _join3(t):
    return (t[..., 0].astype(F32) + t[..., 1].astype(F32)) + t[..., 2].astype(F32)


def _merge_chips(blocks, axis):
    return jnp.concatenate([blocks[s] for s in range(N_CHIPS)], axis=axis)


def _pack_for_chips(weights):
    rows = []
    for s in range(N_CHIPS):
        chip = []
        for layers, axis in weights:
            if axis is None:
                blocks = [g[s] for g in layers]
            else:
                n = layers[0].shape[axis] // N_CHIPS
                blocks = [lax.slice_in_dim(g, s * n, (s + 1) * n, axis=axis) for g in layers]
            if math.prod(blocks[0].shape) % (ROW_ALIGN * D_MODEL) == 0:
                chip += [_as_rows(b) for b in blocks]
            else:
                chip.append(_as_rows(jnp.stack(blocks)))
        pad = -sum(r.shape[0] for r in chip) % SHARD_ROWS
        rows += chip + ([jnp.zeros((pad, D_MODEL), F32)] if pad else [])
    return jnp.concatenate(rows, axis=0).reshape(N_CHIPS, -1, D_MODEL)


def kernel(x, mem, w_kv_mem, w_o, ln1_g, ln1_b, ln2_g, ln2_b, mlp_w1, mlp_w2, gdn_w_in, gdn_conv_w, gdn_a_log, gdn_dt_bias, gdn_norm_g, s5_w_in, s5_a_re, s5_a_im, s5_b_re, s5_b_im, s5_c_re, s5_c_im, s5_log_dt, s5_d, s5_w_glu, s5_b_glu, loss_target, m_w_kv_mem, m_w_o, m_ln1_g, m_ln1_b, m_ln2_g, m_ln2_b, m_mlp_w1, m_mlp_w2, m_gdn_w_in, m_gdn_conv_w, m_gdn_a_log, m_gdn_dt_bias, m_gdn_norm_g, m_s5_w_in, m_s5_a_re, m_s5_a_im, m_s5_b_re, m_s5_b_im, m_s5_c_re, m_s5_c_im, m_s5_log_dt, m_s5_d, m_s5_w_glu, m_s5_b_glu, v_w_kv_mem, v_w_o, v_ln1_g, v_ln1_b, v_ln2_g, v_ln2_b, v_mlp_w1, v_mlp_w2, v_gdn_w_in, v_gdn_conv_w, v_gdn_a_log, v_gdn_dt_bias, v_gdn_norm_g, v_s5_w_in, v_s5_a_re, v_s5_a_im, v_s5_b_re, v_s5_b_im, v_s5_c_re, v_s5_c_im, v_s5_log_dt, v_s5_d, v_s5_w_glu, v_s5_b_glu):
    given = dict(locals())
    w = {n: given[n] for n in _WEIGHTS}
    mom = {n: given["m_" + n] for n in _WEIGHTS}
    var = {n: given["v_" + n] for n in _WEIGHTS}
    shard_names = [n for n, _ in _SHARDED]
    shard_shapes = [w[n].shape for n in shard_names]
    rep_shapes = [w[n].shape for n in _REPLICATED]

    wire = {n: w[n].astype(BF16) if n in _MATMUL_ONLY else _split3(w[n]) for n in shard_names}
    first = {n: 0 if n.startswith("s5_") else 1 for n in shard_names}
    me_chip = 2 * lax.axis_index("x") + lax.axis_index("y")
    early = [wire[n][:first[n]] for n in shard_names if first[n]]
    late = [wire[n][first[n]:] for n in shard_names]
    early_pack, late_pack = _pack(early), _pack(late)
    landed = _all_gather_chips(early_pack, name="gather_first_layer")
    landed = lax.dynamic_update_index_in_dim(landed, early_pack, me_chip, axis=0)
    early_blocks = dict(zip([n for n in shard_names if first[n]], _unpack(landed, [a.shape for a in early])))
    send_sems, recv_sems, pack_thru, land_thru, token = _gather_start(late_pack, after=landed)
    axis_of = dict(_SHARDED)

    def merged(n, blk):
        if n in _KEPT_BLOCKED:
            return blk
        return _merge_chips(blk if n in _MATMUL_ONLY else _join3(blk), axis_of[n] - 1)

    late_full = {}

    def weights_of(i, h):
        if i == 0:
            full = {n: [merged(n, blk[:, 0])] for n, blk in early_blocks.items()}
            full["gdn_w_in"][0] = full["gdn_w_in"][0] + token[0, 0].astype(BF16)
        else:
            if not late_full:
                land = _gather_wait(send_sems, recv_sems, pack_thru, land_thru, after=h)
                land = _gather_forward(land, name="gather_forward")
                land = lax.dynamic_update_index_in_dim(land, late_pack, me_chip, axis=0)
                for n, blk in zip(shard_names, _unpack(land, [a.shape for a in late])):
                    late_full[n] = [None] * first[n] + [merged(n, blk[:, t]) for t in range(blk.shape[1])]
            full = dict(late_full)
        full.update({n: w[n] for n in _REPLICATED})
        return _layer_weights(full, i)

    sharded = {}
    in_flight = {}
    first_mixer, first_outer = (0, "mixer"), (0, "outer")

    def group_pack(parts):
        names = [n for n in shard_names if any(n in sharded[i] for i in parts)]
        per_weight = [[sharded[i][n] for i in parts if n in sharded[i]] for n in names]
        pack = _pack_for_chips([(g, None if n in _KEPT_BLOCKED else axis_of[n] - 1) for n, g in zip(names, per_weight)])
        return pack, names, [(len(g),) + w[n].shape[1:] for n, g in zip(names, per_weight)]

    def grads_ready(i, g):
        by_weight = _sharded_grads(g, i)
        if i:
            sharded[i] = by_weight
        else:
            sharded[first_mixer] = {n: g for n, g in by_weight.items() if n not in sharded[first_outer]}

    def before_first_mixer(outer):
        sharded[first_outer] = outer
        pack, names, shapes = group_pack([first_outer] + list(range(1, DEPTH)))
        state, token = _reduce_scatter_begin(pack, behind=True)
        in_flight.update(state=state, names=names, shapes=shapes)
        return token

    loss, grad_x, layer_grads = _local_step(x[0], mem[0], loss_target[0], weights_of, grads_ready,
                                            before_first_mixer)
    loss = lax.psum(loss, ("x", "y", "c"))
    pack, names, shapes = group_pack([first_mixer])
    state, _ = _reduce_scatter_begin(pack, behind=False)
    pieces = {n: [] for n in shard_names}
    for n, g in zip(names, _unpack(_reduce_scatter_end(state), shapes)):
        pieces[n].append(g)
    late = _reduce_scatter_end(in_flight["state"], after=grad_x)
    for n, g in zip(in_flight["names"], _unpack(late, in_flight["shapes"])):
        pieces[n].append(g)
    g_shards = [p[0] if len(p) == 1 else jnp.concatenate(p, axis=0) for p in (pieces[n] for n in shard_names)]

    def pack_small(d):
        return _pack([d[n] for n in _REPLICATED], unit_rows=SMALL_ROWS)

    g_rep = _all_reduce_small(pack_small(_replicated_grads(layer_grads)), name="reduce_replicated")

    def adamw(wp, gp, mp, vp, name):
        return _rw_fwd(_f_adamw, [wp, gp, mp, vp], [], tm=TM_WIDE, name=name)

    outs = {}
    for n, g in zip(shard_names, g_shards):
        flat = (-1, w[n].shape[-1])
        res = adamw(w[n].reshape(flat), g.reshape(flat), mom[n].reshape(flat), var[n].reshape(flat), "adamw_" + n)
        outs[("grad", n)] = g
        outs.update({(kind, n): a.reshape(w[n].shape) for kind, a in zip(("delta", "new_m", "new_v"), res)})
    packed = (g_rep,) + tuple(adamw(pack_small(w), g_rep, pack_small(mom), pack_small(var), "adamw_replicated"))
    for kind, pr in zip(("grad", "delta", "new_m", "new_v"), packed):
        outs.update({(kind, n): a for n, a in zip(_REPLICATED, _unpack(pr, rep_shapes))})
    return (loss, grad_x[None]) + tuple(outs[(kind, n)] for kind in ("grad", "delta", "new_m", "new_v")
                                        for n in _WEIGHTS)
```

```python
import functools
import math

import jax
import jax.numpy as jnp
from jax import lax
from jax.experimental import pallas as pl
from jax.experimental.pallas import tpu as pltpu

F32 = jnp.float32
BF16 = jnp.bfloat16
MESH = pl.DeviceIdType.MESH

D_MODEL = 1024
DEPTH = 4
GDN_HEADS = 8
HEAD_DIM = 128
GDN_CONV = 4
GDN_CHUNK = 64
S5_GROUP = 16
S5_STATE = 64
XA_HEADS = 4
XA_DIM = 512
DN_ALPHA = (2 * DEPTH) ** 0.25
LN_EPS = 1e-5
RMS_EPS = 1e-6
ADAM_LR, ADAM_B1, ADAM_B2, ADAM_EPS, ADAM_WD, ADAM_STEP = 0.001, 0.9, 0.999, 1e-08, 0.01, 10

VMEM_LIMIT_V7X = 56 * 1024 * 1024
LANES = 128
SUBLANES = 8
S5_T = 16
S5_TILES = D_MODEL // LANES
N_CHIPS = 4


def _params(sem):
    return pltpu.CompilerParams(dimension_semantics=sem, vmem_limit_bytes=VMEM_LIMIT_V7X)


def _tile(n, pref):
    if n <= pref:
        return n
    t = (pref // LANES) * LANES
    while n % t:
        t -= LANES
    return t


def _row_tile(n, pref):
    if n % SUBLANES:
        return n
    t = min(pref, n) // SUBLANES * SUBLANES
    while n % t:
        t -= SUBLANES
    return t


def _col_blocked_spec(rows_tile, cols_tile, block_cols, rows_axis, cols_axis):
    r = block_cols // cols_tile

    def index(*ijk):
        c = ijk[cols_axis]
        return (c, ijk[rows_axis], 0) if r == 1 else (c // r, ijk[rows_axis], c % r)

    return pl.BlockSpec((None, rows_tile, cols_tile), index)


def _mm(a, b, *, ta=False, tb=False, acc=None, name, tm=1024, tn=1024, tk=None, out_blocks=0):
    if tk is None:
        tk = 4096 if a.dtype == BF16 and b.dtype == BF16 else 2048
    k_dim, m_dim = a.shape if ta else a.shape[::-1]
    b_rows, b_cols = (b.shape[0], b.shape[1]) if b.ndim == 2 else (b.shape[1], b.shape[0] * b.shape[2])
    n_dim = b_rows if tb else b_cols
    assert (b_cols if tb else b_rows) == k_dim, (a.shape, b.shape, ta, tb)
    limit_n = n_dim // out_blocks if out_blocks else (n_dim if b.ndim == 2 or tb else b.shape[2])
    limit_k = b.shape[2] if (b.ndim == 3 and tb) else k_dim
    tm, tn, tk = _tile(m_dim, tm), _tile(limit_n, min(tn, limit_n)), _tile(limit_k, min(tk, limit_k))
    a_spec = (pl.BlockSpec((tk, tm), lambda i, j, k: (k, i)) if ta else pl.BlockSpec((tm, tk), lambda i, j, k: (i, k)))
    if b.ndim == 3:
        b_spec = (_col_blocked_spec(tn, tk, b.shape[2], 1, 2) if tb else _col_blocked_spec(tk, tn, b.shape[2], 2, 1))
    else:
        b_spec = (pl.BlockSpec((tn, tk), lambda i, j, k: (j, k)) if tb
                  else pl.BlockSpec((tk, tn), lambda i, j, k: (k, j)))
    o_spec = (_col_blocked_spec(tm, tn, n_dim // out_blocks, 0, 1) if out_blocks
              else pl.BlockSpec((tm, tn), lambda i, j, k: (i, j)))
    o_shape = (out_blocks, m_dim, n_dim // out_blocks) if out_blocks else (m_dim, n_dim)
    dn = (((0 if ta else 1,), (1 if tb else 0,)), ((), ()))
    has_acc = acc is not None

    def body(*refs):
        a_ref, b_ref = refs[0], refs[1]
        o_ref = refs[-1]
        k = pl.program_id(2)
        p = lax.dot_general(a_ref[...].astype(BF16), b_ref[...].astype(BF16), dn,
                            preferred_element_type=F32)

        @pl.when(k == 0)
        def _():
            o_ref[...] = p + refs[2][...] if has_acc else p

        @pl.when(k > 0)
        def _():
            o_ref[...] += p

    return pl.pallas_call(
        body, name=name,
        out_shape=jax.ShapeDtypeStruct(o_shape, F32),
        grid=(m_dim // tm, n_dim // tn, k_dim // tk),
        in_specs=[a_spec, b_spec] + ([o_spec] if has_acc else []),
        out_specs=o_spec,
        compiler_params=_params(("parallel", "parallel", "arbitrary")),
    )(*([a, b] + ([acc] if has_acc else [])))


def _mm_relu2(a, b, *, name, tm=1024):
    m_dim, k_dim = a.shape
    n_blocks, _, tn = b.shape
    n_dim = n_blocks * tn
    tm = _tile(m_dim, tm)

    def body(a_ref, b_ref, h_ref, act_ref):
        h = jnp.dot(a_ref[...].astype(BF16), b_ref[...].astype(BF16), preferred_element_type=F32)
        h_ref[...] = h.astype(h_ref.dtype)
        r = jnp.maximum(h, 0.0)
        act_ref[...] = (r * r).astype(BF16)

    o_spec = pl.BlockSpec((tm, tn), lambda i, j: (i, j))
    return pl.pallas_call(
        body, name=name,
        out_shape=(jax.ShapeDtypeStruct((m_dim, n_dim), BF16), jax.ShapeDtypeStruct((m_dim, n_dim), BF16)),
        grid=(m_dim // tm, n_dim // tn),
        in_specs=[pl.BlockSpec((tm, k_dim), lambda i, j: (i, 0)),
                  pl.BlockSpec((None, k_dim, tn), lambda i, j: (j, 0, 0))],
        out_specs=(o_spec, o_spec),
        compiler_params=_params(("parallel", "parallel")),
    )(a, b)


def _mm_relu2_grad(d, b, h, *, name, tm=1024, tn=1024):
    m_dim, k_dim = d.shape
    n_dim = b.shape[0]
    tm, tn = _tile(m_dim, tm), _tile(n_dim, tn)

    def body(d_ref, b_ref, h_ref, o_ref):
        p = lax.dot_general(d_ref[...].astype(BF16), b_ref[...].astype(BF16), ((NT), ((), ())),
                            preferred_element_type=F32)
        o_ref[...] = (p * (2.0 * jnp.maximum(h_ref[...].astype(F32), 0.0))).astype(BF16)

    o_spec = pl.BlockSpec((tm, tn), lambda i, j: (i, j))
    return pl.pallas_call(
        body, name=name,
        out_shape=jax.ShapeDtypeStruct((m_dim, n_dim), BF16),
        grid=(m_dim // tm, n_dim // tn),
        in_specs=[pl.BlockSpec((tm, k_dim), lambda i, j: (i, 0)), pl.BlockSpec((tn, k_dim), lambda i, j: (j, 0)), o_spec],
        out_specs=o_spec,
        compiler_params=_params(("parallel", "parallel")),
    )(d, b, h)


def _rowwise(f, rows, params, row_out, acc_out, *, tm, name):
    length = rows[0].shape[0]
    tm = _row_tile(length, tm)
    nr, npar, nro = len(rows), len(params), len(row_out)

    def body(*refs):
        ins = [r[...] for r in refs[:nr + npar]]
        outs = refs[nr + npar:]
        r_o, a_o = f(*ins)
        for ref, val in zip(outs[:nro], r_o):
            ref[...] = val.astype(ref.dtype)
        i = pl.program_id(0)
        for ref, val in zip(outs[nro:], a_o):
            @pl.when(i == 0)
            def _(ref=ref, val=val):
                ref[...] = val.astype(ref.dtype)

            @pl.when(i > 0)
            def _(ref=ref, val=val):
                ref[...] += val.astype(ref.dtype)

    in_specs = ([pl.BlockSpec((tm, r.shape[1]), lambda i: (i, 0)) for r in rows]
                + [pl.BlockSpec(p.shape, lambda i: (0, 0)) for p in params])
    out_specs = ([pl.BlockSpec((tm, w), lambda i: (i, 0)) for w, _ in row_out]
                 + [pl.BlockSpec(s, lambda i: (0, 0)) for s, _ in acc_out])
    out_shape = ([jax.ShapeDtypeStruct((length, w), dt) for w, dt in row_out]
                 + [jax.ShapeDtypeStruct(s, dt) for s, dt in acc_out])
    res = pl.pallas_call(
        body, name=name, out_shape=out_shape, grid=(length // tm,),
        in_specs=in_specs, out_specs=out_specs,
        compiler_params=_params(("arbitrary",) if acc_out else ("parallel",)),
    )(*rows, *params)
    return res[:nro], res[nro:]


def _rw_fwd(f, rows, params, *, tm, name, out_dtypes=None):
    tm_ = _row_tile(rows[0].shape[0], tm)
    shapes = jax.eval_shape(f, *[jax.ShapeDtypeStruct((tm_, r.shape[1]), r.dtype) for r in rows],
                            *[jax.ShapeDtypeStruct(p.shape, p.dtype) for p in params])
    row_out = [(s.shape[1], s.dtype if out_dtypes is None else dt)
               for s, dt in zip(shapes, out_dtypes or shapes)]
    outs, _ = _rowwise(lambda *v: (f(*v), ()), rows, params, row_out, [], tm=tm, name=name)
    return outs


def _rw_bwd(f, rows, params, cots, *, row_grad, param_grad, tm, name, row_dtypes=None):
    nr, npar, nct = len(rows), len(params), len(cots)

    def g(*vals):
        prim = vals[:nr] + vals[nr + nct:]
        ct = vals[nr:nr + nct]
        _, vjp = jax.vjp(f, *prim)
        grads = vjp(tuple(ct))
        return (tuple(grads[i] for i in range(nr) if row_grad[i]),
                tuple(grads[nr + i] for i in range(npar) if param_grad[i]))

    widths = [rows[i].shape[1] for i in range(nr) if row_grad[i]]
    row_out = list(zip(widths, row_dtypes or [F32] * len(widths)))
    acc_out = [(params[i].shape, F32) for i in range(npar) if param_grad[i]]
    return _rowwise(g, list(rows) + list(cots), params, row_out, acc_out, tm=tm, name=name)


def _f_ln_res(x, h, g, b):
    pre = DN_ALPHA * x + h
    mu = jnp.mean(pre, axis=-1, keepdims=True)
    d = pre - mu
    var = jnp.mean(d * d, axis=-1, keepdims=True)
    return (d * lax.rsqrt(var + LN_EPS) * g + b,)


def _silu(t):
    return t * jax.nn.sigmoid(t)


def _f_gdn_qkv(c):
    a = _silu(c)
    outs = []
    for part, scale in ((0, HEAD_DIM ** -0.5), (1, 1.0)):
        heads = []
        for h in range(GDN_HEADS):
            t = a[:, part * D_MODEL + h * HEAD_DIM: part * D_MODEL + (h + 1) * HEAD_DIM]
            t = t * lax.rsqrt(jnp.sum(t * t, axis=-1, keepdims=True) + 1e-6)
            heads.append(t * scale if scale != 1.0 else t)
        outs.append(jnp.concatenate(heads, axis=-1))
    outs.append(a[:, 2 * D_MODEL:])
    return tuple(outs)


def _f_gdn_out(o, z, norm_g):
    heads = []
    for h in range(GDN_HEADS):
        t = o[:, h * HEAD_DIM:(h + 1) * HEAD_DIM]
        t = t * lax.rsqrt(jnp.mean(t * t, axis=-1, keepdims=True) + RMS_EPS) * norm_g
        heads.append(t)
    return (jnp.concatenate(heads, axis=-1) * _silu(z),)


def _f_attn(xq, kmem, vmem):
    heads = []
    for h in range(XA_HEADS):
        sl = slice(h * HEAD_DIM, (h + 1) * HEAD_DIM)
        s = lax.dot_general(xq[:, sl].astype(BF16), kmem[:, sl].astype(BF16),
                            (((1,), (1,)), ((), ())), preferred_element_type=F32) * (HEAD_DIM ** -0.5)
        m = lax.stop_gradient(jnp.max(s, axis=-1, keepdims=True))
        e = jnp.exp(s - m)
        p = e / jnp.sum(e, axis=-1, keepdims=True)
        heads.append(jnp.dot(p.astype(BF16), vmem[:, sl].astype(BF16), preferred_element_type=F32))
    return (jnp.concatenate(heads, axis=-1),)


def _f_s5_gelu(y, u, d):
    return (jax.nn.gelu(y + d * u),)


def _f_s5_gate(zg, t, b):
    return (zg * jax.nn.sigmoid(t + b),)


def _f_add(a, b):
    return (a + b,)


def _f_add4(a, b, c, d):
    return (((a + b.astype(F32)) + c.astype(F32)) + d.astype(F32),)


def _f_adamw(w, g, m, v):
    m = ADAM_B1 * m + (1.0 - ADAM_B1) * g
    v = ADAM_B2 * v + (1.0 - ADAM_B2) * jnp.square(g)
    m_hat = m / (1.0 - ADAM_B1 ** ADAM_STEP)
    v_hat = v / (1.0 - ADAM_B2 ** ADAM_STEP)
    delta = -ADAM_LR * (m_hat / (jnp.sqrt(v_hat) + ADAM_EPS) + ADAM_WD * w)
    return delta, m, v


def _conv_fwd(u, w, *, tm, name):
    length, chans = u.shape
    tm = min(tm, length)
    tc = _tile(chans, 1024)
    hb = tm // SUBLANES

    def body(cur_ref, prev_ref, w_ref, o_ref, buf):
        i = pl.program_id(1)
        buf[0:SUBLANES, :] = jnp.where(i > 0, prev_ref[...], 0.0)
        buf[SUBLANES:, :] = cur_ref[...]
        acc = buf[pl.ds(SUBLANES - 3, tm), :] * w_ref[0:1, :]
        for k in range(1, GDN_CONV):
            acc = acc + buf[pl.ds(SUBLANES - 3 + k, tm), :] * w_ref[k:k + 1, :]
        o_ref[...] = acc

    return pl.pallas_call(
        body, name=name, out_shape=jax.ShapeDtypeStruct(u.shape, F32),
        grid=(chans // tc, length // tm),
        in_specs=[pl.BlockSpec((tm, tc), lambda j, i: (i, j)),
                  pl.BlockSpec((SUBLANES, tc), lambda j, i: (jnp.maximum(i * hb - 1, 0), j)),
                  pl.BlockSpec((GDN_CONV, tc), lambda j, i: (0, j))],
        out_specs=pl.BlockSpec((tm, tc), lambda j, i: (i, j)),
        scratch_shapes=[pltpu.VMEM((tm + SUBLANES, tc), F32)],
        compiler_params=_params(("parallel", "parallel")),
    )(u, u, w)


def _conv_bwd(u, w, dc, *, tm, name):
    length, chans = u.shape
    tm = min(tm, length)
    tc = _tile(chans, 1024)
    hb = tm // SUBLANES
    last = length // tm - 1

    def body(u_ref, uprev_ref, dc_ref, dcnext_ref, w_ref, du_ref, dw_ref, ubuf, dbuf):
        i = pl.program_id(1)
        ubuf[0:SUBLANES, :] = jnp.where(i > 0, uprev_ref[...], 0.0)
        ubuf[SUBLANES:, :] = u_ref[...]
        dbuf[0:tm, :] = dc_ref[...]
        dbuf[tm:, :] = jnp.where(i < last, dcnext_ref[...], 0.0)
        dcv = dc_ref[...]
        du = dbuf[pl.ds(3, tm), :] * w_ref[0:1, :]
        rows = [jnp.sum(dcv * ubuf[pl.ds(SUBLANES - 3, tm), :], axis=0, keepdims=True)]
        for k in range(1, GDN_CONV):
            du = du + dbuf[pl.ds(3 - k, tm), :] * w_ref[k:k + 1, :]
            rows.append(jnp.sum(dcv * ubuf[pl.ds(SUBLANES - 3 + k, tm), :], axis=0, keepdims=True))
        du_ref[...] = du.astype(du_ref.dtype)
        dwv = jnp.concatenate(rows, axis=0)

        @pl.when(i == 0)
        def _():
            dw_ref[...] = dwv

        @pl.when(i > 0)
        def _():
            dw_ref[...] += dwv

    return pl.pallas_call(
        body, name=name,
        out_shape=(jax.ShapeDtypeStruct(u.shape, BF16), jax.ShapeDtypeStruct((GDN_CONV, chans), F32)),
        grid=(chans // tc, length // tm),
        in_specs=[pl.BlockSpec((tm, tc), lambda j, i: (i, j)),
                  pl.BlockSpec((SUBLANES, tc), lambda j, i: (jnp.maximum(i * hb - 1, 0), j)),
                  pl.BlockSpec((tm, tc), lambda j, i: (i, j)),
                  pl.BlockSpec((SUBLANES, tc), lambda j, i: (jnp.minimum((i + 1) * hb, (last + 1) * hb - 1), j)),
                  pl.BlockSpec((GDN_CONV, tc), lambda j, i: (0, j))],
        out_specs=(pl.BlockSpec((tm, tc), lambda j, i: (i, j)),
                   pl.BlockSpec((GDN_CONV, tc), lambda j, i: (0, j))),
        scratch_shapes=[pltpu.VMEM((tm + SUBLANES, tc), F32), pltpu.VMEM((tm + SUBLANES, tc), F32)],
        compiler_params=_params(("parallel", "arbitrary")),
    )(u, u, dc, dc, w)


def _dot(a, b, dims, precision=None):
    if precision is None:
        a, b = a.astype(BF16), b.astype(BF16)
    return lax.dot_general(a, b, (dims, ((), ())), preferred_element_type=F32, precision=precision)


def _dot3(a, b, dims):
    ah, bh = a.astype(BF16), b.astype(BF16)
    al, bl = (a - ah.astype(F32)).astype(BF16), (b - bh.astype(F32)).astype(BF16)

    def d(x, y):
        return lax.dot_general(x, y, (dims, ((), ())), preferred_element_type=F32)

    return d(ah, bh) + (d(ah, bl) + d(al, bh))


NN = ((1,), (0,))
NT = ((1,), (1,))
TN = ((0,), (0,))
HI = lax.Precision.HIGHEST


def _hmap(f, *lists):
    return [f(*t) for t in zip(*lists)]


@jax.custom_vjp
def _unit_lower_inverse(a):
    c = a[0].shape[0]
    eye = (lax.broadcasted_iota(jnp.int32, (c, c), 0) == lax.broadcasted_iota(jnp.int32, (c, c), 1)).astype(F32)
    p = _hmap(lambda x: -x, a)
    t = _hmap(lambda x: eye + x, p)
    for _ in range(int(math.log2(c)) - 1):
        p = _hmap(lambda x: _dot3(x, x, NN), p)
        t = _hmap(lambda x, y: x + _dot3(x, y, NN), t, p)
    return t


def _uli_fwd(a):
    t = _unit_lower_inverse(a)
    return t, t


def _uli_bwd(t, dt):
    left = _hmap(lambda x, y: _dot3(x, y, TN), t, dt)
    return (_hmap(lambda x, y: -_dot3(x, y, NT), left, t),)


_unit_lower_inverse.defvjp(_uli_fwd, _uli_bwd)


@jax.custom_vjp
def _known_inverse(a, t):
    return t


_known_inverse.defvjp(lambda a, t: (t, t),
                      lambda t, dt: (_uli_bwd(t, dt)[0], _hmap(jnp.zeros_like, t)))


def _gdn_chunk(q, k, v, bl, al, a_log, dt_bias, state, t_known=None):
    c = q[0].shape[0]
    row = lax.broadcasted_iota(jnp.int32, (c, c), 0)
    col = lax.broadcasted_iota(jnp.int32, (c, c), 1)
    causal = row >= col
    strict = row > col
    eye = (row == col).astype(F32)
    beta = _hmap(jax.nn.sigmoid, bl)
    g = _hmap(lambda a_, l_, d_: -jnp.exp(a_) * jax.nn.softplus(l_ + d_), a_log, al, dt_bias)
    g_r = _hmap(lambda x: jnp.sum(eye * x, axis=0, keepdims=True), g)
    gc = _hmap(lambda x: jnp.sum(jnp.where(causal, x, 0.0), axis=1, keepdims=True), g_r)
    gc_r = _hmap(lambda x: jnp.sum(jnp.where(row <= col, x, 0.0), axis=0, keepdims=True), g)
    decay = _hmap(lambda x, y: jnp.where(causal, jnp.exp(jnp.where(causal, x - y, 0.0)), 0.0), gc, gc_r)
    e_gc = _hmap(jnp.exp, gc)
    kb = _hmap(jnp.multiply, k, beta)
    vb = _hmap(jnp.multiply, v, beta)
    a_mat = _hmap(lambda x, y, d: jnp.where(strict, _dot(x, y, NT) * d, 0.0), kb, k, decay)
    t_inv = _unit_lower_inverse(a_mat) if t_known is None else _known_inverse(a_mat, t_known)
    u_blk = _hmap(lambda t, x: _dot(t, x, NN), t_inv, vb)
    w_blk = _hmap(lambda t, x, e: _dot(t, x * e, NN), t_inv, kb, e_gc)
    v_new = _hmap(lambda u, w, s: u - _dot(w, s, NN), u_blk, w_blk, state)
    attn = _hmap(lambda x, y, d: _dot(x, y, NT) * d, q, k, decay)
    o_state = _hmap(lambda x, e, s: _dot(x * e, s, NN), q, e_gc, state)
    o = _hmap(lambda base, at, vn: base + _dot(at, vn, NN), o_state, attn, v_new)
    g_last = _hmap(lambda x: jnp.sum(x, axis=0, keepdims=True), g)
    k_dec = _hmap(lambda x, gl, c_: x * jnp.exp(gl - c_), k, g_last, gc)
    new_state = _hmap(lambda s, gl, kd, vn: s * jnp.exp(gl) + _dot(kd, vn, TN), state, g_last, k_dec, v_new)
    return (o, new_state, t_inv) if t_known is None else (o, new_state)


def _gdn_operands(q_ref, k_ref, v_ref, bav, alog_ref, dtb_ref):
    hs = range(GDN_HEADS)
    cols = [slice(h * HEAD_DIM, (h + 1) * HEAD_DIM) for h in hs]
    return ([q_ref[:, sl] for sl in cols], [k_ref[:, sl] for sl in cols], [v_ref[:, sl] for sl in cols],
            [bav[:, h:h + 1] for h in hs], [bav[:, h + GDN_HEADS:h + GDN_HEADS + 1] for h in hs],
            [alog_ref[h:h + 1, 0:1] for h in hs], [dtb_ref[h:h + 1, 0:1] for h in hs])


def _gdn_scan_fwd(q, k, v, ba, a_log, dt_bias, *, name):
    length = q.shape[0]
    n = length // GDN_CHUNK
    c = GDN_CHUNK

    def body(q_ref, k_ref, v_ref, ba_ref, alog_ref, dtb_ref, o_ref, s_ref, t_ref, state):
        i = pl.program_id(0)

        @pl.when(i == 0)
        def _():
            state[...] = jnp.zeros_like(state)

        bav = ba_ref[...]
        heads = [slice(h * HEAD_DIM, (h + 1) * HEAD_DIM) for h in range(GDN_HEADS)]
        s_in = [state[h] for h in range(GDN_HEADS)]
        o, s_out, t_inv = _gdn_chunk(*_gdn_operands(q_ref, k_ref, v_ref, bav, alog_ref, dtb_ref), s_in)
        for h, sl in enumerate(heads):
            s_ref[h] = s_in[h]
            t_ref[h] = t_inv[h]
            o_ref[:, sl] = o[h]
            state[h] = s_out[h]

    row_spec = pl.BlockSpec((c, D_MODEL), lambda i: (i, 0))
    small = pl.BlockSpec((GDN_HEADS, LANES), lambda i: (0, 0))
    return pl.pallas_call(
        body, name=name,
        out_shape=(jax.ShapeDtypeStruct((length, D_MODEL), F32),
                   jax.ShapeDtypeStruct((n, GDN_HEADS, HEAD_DIM, HEAD_DIM), F32),
                   jax.ShapeDtypeStruct((n, GDN_HEADS, c, c), F32)),
        grid=(n,),
        in_specs=[row_spec, row_spec, row_spec, pl.BlockSpec((c, LANES), lambda i: (i, 0)), small, small],
        out_specs=(row_spec, pl.BlockSpec((None, GDN_HEADS, HEAD_DIM, HEAD_DIM), lambda i: (i, 0, 0, 0)),
                   pl.BlockSpec((None, GDN_HEADS, c, c), lambda i: (i, 0, 0, 0))),
        scratch_shapes=[pltpu.VMEM((GDN_HEADS, HEAD_DIM, HEAD_DIM), F32)],
        compiler_params=_params(("arbitrary",)),
    )(q, k, v, ba, a_log, dt_bias)


def _gdn_scan_bwd(q, k, v, ba, a_log, dt_bias, states, inverses, do, *, name):
    length = q.shape[0]
    n = length // GDN_CHUNK
    c = GDN_CHUNK

    def body(q_ref, k_ref, v_ref, ba_ref, alog_ref, dtb_ref, s_ref, t_ref, do_ref,
             dq_ref, dk_ref, dv_ref, dba_ref, dalog_ref, ddtb_ref, dstate):
        i = pl.program_id(0)

        @pl.when(i == 0)
        def _():
            dstate[...] = jnp.zeros_like(dstate)
            dalog_ref[...] = jnp.zeros_like(dalog_ref)
            ddtb_ref[...] = jnp.zeros_like(ddtb_ref)

        bav = ba_ref[...]
        lane = lax.broadcasted_iota(jnp.int32, (c, LANES), 1)
        sub8 = lax.broadcasted_iota(jnp.int32, (GDN_HEADS, LANES), 0)
        lane8 = lax.broadcasted_iota(jnp.int32, (GDN_HEADS, LANES), 1)
        slab = jnp.zeros((c, LANES), F32)
        dalog_all = jnp.zeros((GDN_HEADS, LANES), F32)
        ddtb_all = jnp.zeros((GDN_HEADS, LANES), F32)
        heads = [slice(h * HEAD_DIM, (h + 1) * HEAD_DIM) for h in range(GDN_HEADS)]
        ds_in = [dstate[h] for h in range(GDN_HEADS)]
        s_in = [s_ref[h] for h in range(GDN_HEADS)]
        t_known = [t_ref[h] for h in range(GDN_HEADS)]
        _, vjp = jax.vjp(functools.partial(_gdn_chunk, t_known=t_known),
                         *_gdn_operands(q_ref, k_ref, v_ref, bav, alog_ref, dtb_ref), s_in)
        dq, dk, dv, dbl, dal, dalog, ddtb, ds = vjp(([do_ref[:, sl] for sl in heads], ds_in))
        for h, sl in enumerate(heads):
            dq_ref[:, sl] = dq[h]
            dk_ref[:, sl] = dk[h]
            dv_ref[:, sl] = dv[h]
            dstate[h] = ds[h]
            slab = slab + jnp.where(lane == h, dbl[h], 0.0) + jnp.where(lane == h + GDN_HEADS, dal[h], 0.0)
            here = (sub8 == h) & (lane8 == 0)
            dalog_all = dalog_all + jnp.where(here, dalog[h], 0.0)
            ddtb_all = ddtb_all + jnp.where(here, ddtb[h], 0.0)
        dba_ref[...] = slab
        dalog_ref[...] += dalog_all
        ddtb_ref[...] += ddtb_all

    row_spec = pl.BlockSpec((c, D_MODEL), lambda i: (n - 1 - i, 0))
    small = pl.BlockSpec((GDN_HEADS, LANES), lambda i: (0, 0))
    return pl.pallas_call(
        body, name=name,
        out_shape=(jax.ShapeDtypeStruct((length, D_MODEL), F32),) * 3
        + (jax.ShapeDtypeStruct((length, LANES), F32),
           jax.ShapeDtypeStruct((GDN_HEADS, LANES), F32), jax.ShapeDtypeStruct((GDN_HEADS, LANES), F32)),
        grid=(n,),
        in_specs=[row_spec, row_spec, row_spec,
                  pl.BlockSpec((c, LANES), lambda i: (n - 1 - i, 0)), small, small,
                  pl.BlockSpec((None, GDN_HEADS, HEAD_DIM, HEAD_DIM), lambda i: (n - 1 - i, 0, 0, 0)),
                  pl.BlockSpec((None, GDN_HEADS, c, c), lambda i: (n - 1 - i, 0, 0, 0)),
                  row_spec],
        out_specs=(row_spec, row_spec, row_spec,
                   pl.BlockSpec((c, LANES), lambda i: (n - 1 - i, 0)), small, small),
        scratch_shapes=[pltpu.VMEM((GDN_HEADS, HEAD_DIM, HEAD_DIM), F32)],
        compiler_params=_params(("arbitrary",)),
    )(q, k, v, ba, a_log, dt_bias, states, inverses, do)


S5_W = S5_T * LANES
S5_S = 2 * 8 * S5_STATE
S5_SH = S5_S // 2


def _iota2(shape):
    return lax.broadcasted_iota(jnp.int32, shape, 0), lax.broadcasted_iota(jnp.int32, shape, 1)


def _s5_rep_t(t, dtype):
    row, col = _iota2((S5_T * S5_GROUP, LANES))
    return ((jnp.right_shift(row, 4) == t) & (jnp.bitwise_and(row, 15) == jnp.bitwise_and(col, 15))).astype(dtype)


def _s5_rep_state(dtype):
    row, col = _iota2((2 * S5_STATE, S5_S))
    return ((jnp.right_shift(row, 6) == jnp.right_shift(col, 9))
            & (jnp.bitwise_and(row, 63) == jnp.bitwise_and(col, 63))).astype(dtype)


def _s5_masks():
    row, col = _iota2((LANES, LANES))
    m_ab = jnp.right_shift(row, 4) == jnp.right_shift(col, 4)
    row, col = _iota2((S5_S, LANES))
    m_e = jnp.bitwise_and(jnp.right_shift(row, 6), 7) == jnp.right_shift(col, 4)
    row, col = _iota2((LANES, S5_S))
    m_f = jnp.right_shift(row, 4) == jnp.bitwise_and(jnp.right_shift(col, 6), 7)
    return m_ab, m_e, m_f


def _s5_expand(kx_ref, ec_ref, fc_ref, kb_scr, e_scr, f_scr):
    m_ab, m_e, m_f = _s5_masks()
    kx = kx_ref[...].astype(BF16)
    ec = ec_ref[...].astype(BF16)
    rep_state = _s5_rep_state(BF16)
    for t in range(S5_T):
        rep = _s5_rep_t(t, BF16)
        cols = slice(t * LANES, (t + 1) * LANES)
        kb_scr[t] = jnp.where(m_ab, jnp.dot(kx, rep, preferred_element_type=F32), 0.0).astype(BF16)
        e_scr[:, cols] = jnp.where(m_e, jnp.dot(ec, rep, preferred_element_type=F32), 0.0).astype(BF16)
        f_scr[cols, :] = jnp.where(m_f, jnp.dot(fc_ref[t].astype(BF16), rep_state, preferred_element_type=F32),
                                   0.0).astype(BF16)


def _s5_token_rows(ref, n):
    return [ref[pl.ds(t, n, stride=S5_T), :].astype(BF16) for t in range(S5_T)]


def _s5_scan_fwd(u, kx, ec, fc, at, *, name):
    length = u.shape[0]
    n = length // S5_T
    assert n % SUBLANES == 0

    def body(u_ref, kx_ref, ec_ref, fc_ref, at_ref, y_ref, h_ref, kb_scr, e_scr, f_scr, g_scr):
        _s5_expand(kx_ref, ec_ref, fc_ref, kb_scr, e_scr, f_scr)
        us = _s5_token_rows(u_ref, n)
        g_scr[...] = jnp.dot(jnp.concatenate(us, axis=1), f_scr[...], preferred_element_type=F32)
        ar, ai = at_ref[:, :S5_SH], at_ref[:, S5_SH:]

        def step(blk, h):
            base = pl.multiple_of(blk * SUBLANES, SUBLANES)
            g8 = g_scr[pl.ds(base, SUBLANES), :]
            rows = []
            for r in range(SUBLANES):
                rows.append(h)
                hr, hi = h[:, :S5_SH], h[:, S5_SH:]
                h = jnp.concatenate([ar * hr - ai * hi, ar * hi + ai * hr], axis=1) + g8[r:r + 1, :]
            h_ref[pl.ds(base, SUBLANES), :] = jnp.concatenate(rows, axis=0)
            return h

        lax.fori_loop(0, n // SUBLANES, step, jnp.zeros((1, S5_S), F32))
        hb = h_ref[...].astype(BF16)
        for t in range(S5_T):
            acc = jnp.dot(hb, e_scr[:, t * LANES:(t + 1) * LANES], preferred_element_type=F32)
            for s in range(t + 1):
                acc = acc + jnp.dot(us[s], kb_scr[t - s], preferred_element_type=F32)
            y_ref[pl.ds(t, n, stride=S5_T), :] = acc

    return pl.pallas_call(
        body, name=name,
        out_shape=(jax.ShapeDtypeStruct((length, D_MODEL), F32), jax.ShapeDtypeStruct((S5_TILES, n, S5_S), F32)),
        grid=(S5_TILES,),
        in_specs=[pl.BlockSpec((length, LANES), lambda k: (0, k)), _s5_spec(LANES, S5_T * S5_GROUP),
                  _s5_spec(S5_S, S5_T * S5_GROUP), _s5_spec(S5_T, LANES, LANES), _s5_spec(1, S5_S)],
        out_specs=(pl.BlockSpec((length, LANES), lambda k: (0, k)), _s5_spec(n, S5_S)),
        scratch_shapes=[pltpu.VMEM((S5_T, LANES, LANES), BF16), pltpu.VMEM((S5_S, S5_W), BF16),
                        pltpu.VMEM((S5_W, S5_S), BF16), pltpu.VMEM((n, S5_S), F32)],
        compiler_params=_params(("parallel",)),
    )(u, kx, ec, fc, at)


def _s5_spec(*tail):
    return pl.BlockSpec((None,) + tail, lambda k: (k,) + (0,) * len(tail))


def _s5_scan_bwd(dy, kx, ec, fc, at, hs, *, name):
    length = dy.shape[0]
    n = length // S5_T

    def body(dy_ref, kx_ref, ec_ref, fc_ref, at_ref, h_ref, du_ref, dg_ref, dat_ref, kb_scr, e_scr, f_scr, dh_scr):
        _s5_expand(kx_ref, ec_ref, fc_ref, kb_scr, e_scr, f_scr)
        dys = _s5_token_rows(dy_ref, n)
        dh_scr[...] = _dot(jnp.concatenate(dys, axis=1), e_scr[...], NT)
        ar, ai = at_ref[:, :S5_SH], at_ref[:, S5_SH:]

        def step(it, carry):
            cy, dat = carry
            base = pl.multiple_of((n // SUBLANES - 1 - it) * SUBLANES, SUBLANES)
            dh8 = dh_scr[pl.ds(base, SUBLANES), :]
            h8 = h_ref[pl.ds(base, SUBLANES), :]
            rows = [None] * SUBLANES
            for r in reversed(range(SUBLANES)):
                rows[r] = cy
                cr, ci = cy[:, :S5_SH], cy[:, S5_SH:]
                hr, hi = h8[r:r + 1, :S5_SH], h8[r:r + 1, S5_SH:]
                dat = dat + jnp.concatenate([cr * hr + ci * hi, ci * hr - cr * hi], axis=1)
                cy = dh8[r:r + 1, :] + jnp.concatenate([ar * cr + ai * ci, ar * ci - ai * cr], axis=1)
            dg_ref[pl.ds(base, SUBLANES), :] = jnp.concatenate(rows, axis=0)
            return cy, dat

        zero = jnp.zeros((1, S5_S), F32)
        _, dat = lax.fori_loop(0, n // SUBLANES, step, (zero, zero))
        dat_ref[...] = dat
        dgb = dg_ref[...].astype(BF16)
        for s in range(S5_T):
            acc = _dot(dgb, f_scr[s * LANES:(s + 1) * LANES, :], NT)
            for t in range(s, S5_T):
                acc = acc + _dot(dys[t], kb_scr[t - s], NT)
            du_ref[pl.ds(s, n, stride=S5_T), :] = acc

    row_spec = pl.BlockSpec((length, LANES), lambda k: (0, k))
    return pl.pallas_call(
        body, name=name,
        out_shape=(jax.ShapeDtypeStruct((length, D_MODEL), F32), jax.ShapeDtypeStruct((S5_TILES, n, S5_S), F32),
                   jax.ShapeDtypeStruct((S5_TILES, 1, S5_S), F32)),
        grid=(S5_TILES,),
        in_specs=[row_spec, _s5_spec(LANES, S5_T * S5_GROUP), _s5_spec(S5_S, S5_T * S5_GROUP),
                  _s5_spec(S5_T, LANES, LANES), _s5_spec(1, S5_S), _s5_spec(n, S5_S)],
        out_specs=(row_spec, _s5_spec(n, S5_S), _s5_spec(1, S5_S)),
        scratch_shapes=[pltpu.VMEM((S5_T, LANES, LANES), BF16), pltpu.VMEM((S5_S, S5_W), BF16),
                        pltpu.VMEM((S5_W, S5_S), BF16), pltpu.VMEM((n, S5_S), F32)],
        compiler_params=_params(("parallel",)),
    )(dy, kx, ec, fc, at, hs)


def _s5_operator_grads(dy, u, hs, dg, *, name):
    length = u.shape[0]
    n = length // S5_T

    def body(dy_ref, u_ref, h_ref, dg_ref, dkx_ref, dec_ref, dfc_ref):
        dys = _s5_token_rows(dy_ref, n)
        us = _s5_token_rows(u_ref, n)
        ucat = jnp.concatenate(us, axis=1)
        m_ab, m_e, m_f = _s5_masks()
        hb = h_ref[...].astype(BF16)
        dgb = dg_ref[...].astype(BF16)
        lane = lax.broadcasted_iota(jnp.int32, (1, LANES), 1)
        lane_group = jnp.right_shift(lane, 4)

        def own_block(x, mask):
            x = jnp.where(mask, x, 0.0)
            for shift in (64, 32, 16):
                x = x + pltpu.roll(x, shift, 1)
            return x

        def place(halves, t, x):
            halves[t // 8] = jnp.where(lane_group == t % 8, x, halves[t // 8])

        dkb = [jnp.zeros((LANES, LANES), F32) for _ in range(S5_T)]
        dec = [jnp.zeros((S5_S, LANES), F32) for _ in range(2)]
        for t in range(S5_T):
            d_t = _dot(ucat, dys[t], TN)
            for s in range(t + 1):
                dkb[t - s] = dkb[t - s] + d_t[s * LANES:(s + 1) * LANES, :]
            place(dec, t, own_block(_dot(hb, dys[t], TN), m_e))
            wide = jnp.where(m_f, _dot(us[t], dgb, TN), 0.0)
            parts = []
            for r in range(2):
                acc = wide[:, r * S5_SH:r * S5_SH + LANES]
                for q in range(1, S5_SH // LANES):
                    acc = acc + wide[:, r * S5_SH + q * LANES:r * S5_SH + (q + 1) * LANES]
                parts.append(acc + pltpu.roll(acc, S5_STATE, 1))
            dfc_ref[t] = jnp.where(lane < S5_STATE, parts[0], parts[1])
        dkx = [jnp.zeros((LANES, LANES), F32) for _ in range(2)]
        for t in range(S5_T):
            place(dkx, t, own_block(dkb[t], m_ab))
        dkx_ref[...] = jnp.concatenate(dkx, axis=1)
        dec_ref[...] = jnp.concatenate(dec, axis=1)

    row_spec = pl.BlockSpec((length, LANES), lambda k: (0, k))
    outs = (_s5_spec(LANES, S5_T * S5_GROUP), _s5_spec(S5_S, S5_T * S5_GROUP), _s5_spec(S5_T, LANES, LANES))
    return pl.pallas_call(
        body, name=name,
        out_shape=(jax.ShapeDtypeStruct((S5_TILES, LANES, S5_T * S5_GROUP), F32),
                   jax.ShapeDtypeStruct((S5_TILES, S5_S, S5_T * S5_GROUP), F32),
                   jax.ShapeDtypeStruct((S5_TILES, S5_T, LANES, LANES), F32)),
        grid=(S5_TILES,),
        in_specs=[row_spec, row_spec, _s5_spec(n, S5_S), _s5_spec(n, S5_S)],
        out_specs=outs,
        compiler_params=_params(("parallel",)),
    )(dy, u, hs, dg)


def _s5_prep(a_re, a_im, b_re, b_im, c_re, c_im, log_dt):
    t_len, tiles = S5_T, S5_TILES
    dt = jnp.exp(log_dt)[:, None]
    mag = jnp.exp(a_re * dt)
    ab_re, ab_im = mag * jnp.cos(a_im * dt), mag * jnp.sin(a_im * dt)
    den = jnp.square(a_re) + jnp.square(a_im)
    n_re, n_im = ab_re - 1.0, ab_im
    f_re = (n_re * a_re + n_im * a_im) / den
    f_im = (n_im * a_re - n_re * a_im) / den
    bb_re = f_re[..., None] * b_re - f_im[..., None] * b_im
    bb_im = f_re[..., None] * b_im + f_im[..., None] * b_re

    def powers(exponents):
        e = exponents[:, None, None]
        m = jnp.exp(e * (a_re * dt))
        return m * jnp.cos(e * (a_im * dt)), m * jnp.sin(e * (a_im * dt))

    p_re, p_im = powers(jnp.arange(t_len + 1, dtype=F32))
    rev_re, rev_im = powers((t_len - 1) - jnp.arange(t_len, dtype=F32))
    ca_re = c_re[None] * p_re[:, :, None, :] - c_im[None] * p_im[:, :, None, :]
    ca_im = c_re[None] * p_im[:, :, None, :] + c_im[None] * p_re[:, :, None, :]
    lag = (jnp.einsum('tgip,gpj->tgij', ca_re[:t_len], bb_re, precision=HI)
           - jnp.einsum('tgip,gpj->tgij', ca_im[:t_len], bb_im, precision=HI))
    kx = lag.reshape(t_len, tiles, 8, S5_GROUP, S5_GROUP).transpose(1, 2, 4, 0, 3)
    kx = kx.reshape(tiles, LANES, t_len * S5_GROUP)
    e_st = jnp.stack([ca_re[1:], -ca_im[1:]])
    e_st = e_st.reshape(2, t_len, tiles, 8, S5_GROUP, S5_STATE).transpose(2, 0, 3, 5, 1, 4)
    ec = e_st.reshape(tiles, S5_S, t_len * S5_GROUP)
    ab_b = jnp.stack([rev_re[..., None] * bb_re[None] - rev_im[..., None] * bb_im[None],
                      rev_re[..., None] * bb_im[None] + rev_im[..., None] * bb_re[None]])
    ab_b = ab_b.reshape(2, t_len, tiles, 8, S5_STATE, S5_GROUP).transpose(2, 1, 3, 5, 0, 4)
    fc = ab_b.reshape(tiles, t_len, LANES, 2 * S5_STATE)
    a_t = jnp.stack([p_re[t_len], p_im[t_len]]).reshape(2, tiles, 8 * S5_STATE).transpose(1, 0, 2)
    return kx, ec, fc, a_t.reshape(tiles, 1, S5_S)


TM_ROW = 1024
TM_WIDE = 512


def _gdn_fwd(x, w):
    qkv = _mm(x, w["wqkv"], name="gdn_proj_qkv")
    z = _mm(x, w["wz"], name="gdn_proj_z")
    ba = _mm(x, w["wba"], name="gdn_proj_ba")
    cv = _conv_fwd(qkv, w["conv_w"], tm=TM_ROW, name="gdn_conv")
    q, k, v = _rw_fwd(_f_gdn_qkv, [cv], [], tm=TM_WIDE, name="gdn_qkv")
    o, states, inverses = _gdn_scan_fwd(q, k, v, ba, w["a_log8"], w["dt_bias8"], name="gdn_scan")
    (mix,) = _rw_fwd(_f_gdn_out, [o, z], [w["norm_g"]], tm=TM_ROW, name="gdn_out", out_dtypes=[BF16])
    return mix, (qkv, z, ba, cv, q, k, v, states, inverses, o)


def _gdn_bwd(x, w, saved, dmix, dx_acc, token=None):
    qkv, z, ba, cv, q, k, v, states, inverses, o = saved
    norm_g = w["norm_g"] if token is None else w["norm_g"] + token[0, 0]
    (do, dz), (dnorm_g,) = _rw_bwd(_f_gdn_out, [o, z], [norm_g], [dmix], row_grad=[1, 1], param_grad=[1],
                                   tm=TM_ROW, name="gdn_out_bwd", row_dtypes=[F32, BF16])
    dq, dk, dv, dba, dalog, ddtb = _gdn_scan_bwd(q, k, v, ba, w["a_log8"], w["dt_bias8"], states, inverses, do,
                                                  name="gdn_scan_bwd")
    (dcv,), _ = _rw_bwd(_f_gdn_qkv, [cv], [], [dq, dk, dv], row_grad=[1], param_grad=[], tm=TM_WIDE,
                        name="gdn_qkv_bwd")
    dqkv, dconv_w = _conv_bwd(qkv, w["conv_w"], dcv, tm=TM_ROW, name="gdn_conv_bwd")
    dx = _mm(dqkv, w["wqkv"], tb=True, acc=dx_acc, name="gdn_dx_qkv")
    dx = _mm(dz, w["wz"], tb=True, acc=dx, name="gdn_dx_z")
    dx = _mm(dba, w["wba"], tb=True, acc=dx, name="gdn_dx_ba")
    grads = dict(wqkv=_mm(x, dqkv, ta=True, name="gdn_dw_qkv"), wz=_mm(x, dz, ta=True, name="gdn_dw_z"),
                 wba=_mm(x, dba, ta=True, name="gdn_dw_ba"), conv_w=dconv_w,
                 a_log=dalog[:, 0], dt_bias=ddtb[:, 0], norm_g=dnorm_g[0])
    return dx, grads


def _s5_fwd(x, w):
    u = _mm(x, w["wu"], name="s5_proj_u")
    y, hs = _s5_scan_fwd(u, w["kx"], w["ec"], w["fc"], w["a_t"], name="s5_scan")
    (zg,) = _rw_fwd(_f_s5_gelu, [y, u], [w["d"]], tm=TM_ROW, name="s5_gelu")
    t = _mm(zg, w["w_glu"], name="s5_glu")
    (mix,) = _rw_fwd(_f_s5_gate, [zg, t], [w["b_glu"]], tm=TM_ROW, name="s5_gate", out_dtypes=[BF16])
    return mix, (u, hs, y, zg, t)


def _s5_bwd(x, w, saved, dmix, dx_acc, token=None):
    u, hs, y, zg, t = saved
    b_glu = w["b_glu"] if token is None else w["b_glu"] + token[0, 0]
    (dzg, dt), (db_glu,) = _rw_bwd(_f_s5_gate, [zg, t], [b_glu], [dmix], row_grad=[1, 1], param_grad=[1],
                                   tm=TM_ROW, name="s5_gate_bwd", row_dtypes=[F32, BF16])
    dzg = _mm(dt, w["w_glu"], tb=True, acc=dzg, name="s5_dzg")
    dw_glu = _mm(zg, dt, ta=True, name="s5_dw_glu")
    (dy, du), (dd,) = _rw_bwd(_f_s5_gelu, [y, u], [w["d"]], [dzg], row_grad=[1, 1], param_grad=[1],
                              tm=TM_ROW, name="s5_gelu_bwd")
    du_scan, dg, dat = _s5_scan_bwd(dy, w["kx"], w["ec"], w["fc"], w["a_t"], hs, name="s5_scan_bwd")
    dkx, dec, dfc = _s5_operator_grads(dy, u, hs, dg, name="s5_operator_grads")
    (du,) = _rw_fwd(_f_add, [du, du_scan], [], tm=TM_ROW, name="s5_du_add", out_dtypes=[BF16])
    d_a_re, d_a_im, d_b_re, d_b_im, d_c_re, d_c_im, d_log_dt = w["prep_vjp"]((dkx, dec, dfc, dat))
    dx = _mm(du, w["wu"], tb=True, acc=dx_acc, name="s5_dx_u")
    grads = dict(wu=_mm(x, du, ta=True, name="s5_dw_u"), w_glu=dw_glu, b_glu=db_glu[0], d=dd[0],
                 a_re=d_a_re, a_im=d_a_im, b_re=d_b_re, b_im=d_b_im, c_re=d_c_re, c_im=d_c_im, log_dt=d_log_dt)
    return dx, grads


def _ln_res_both(x, h, g, b):
    (y,) = _f_ln_res(x, h, g, b)
    return y, y


def _layer_fwd(x, xb, mem, w, is_gdn):
    mix, msave = (_gdn_fwd if is_gdn else _s5_fwd)(xb, w)
    xq = _mm(xb, w["wxq"], name="proj_xq")
    kv = _mm(mem, w["wkv"], name="mem_kv")
    kmem, vmem = kv[:, :XA_DIM], kv[:, XA_DIM:]
    (cross,) = _rw_fwd(_f_attn, [xq], [kmem, vmem], tm=TM_ROW, name="attn", out_dtypes=[BF16])
    h = _mm(mix, w["wo_mix"], name="wo_mix")
    h = _mm(cross, w["wo_cross"], acc=h, name="wo_cross")
    x1, x1b = _rw_fwd(_ln_res_both, [x, h], [w["ln1_g"], w["ln1_b"]], tm=TM_ROW, name="ln_res",
                      out_dtypes=[F32, BF16])
    hm, act = _mm_relu2(x1b, w["w1"], name="mlp_up")
    f = _mm(act, w["w2"], name="mlp_down")
    x2, x2b = _rw_fwd(_ln_res_both, [x1, f], [w["ln2_g"], w["ln2_b"]], tm=TM_ROW, name="ln_res",
                      out_dtypes=[F32, BF16])
    return x2, x2b, (x, xb, msave, xq, kmem, vmem, mix, cross, h, x1, x1b, hm, act, f)


def _layer_bwd(mem, w, is_gdn, saved, dx2, token=None, before_mixer=None):
    x, xb, msave, xq, kmem, vmem, mix, cross, h, x1, x1b, hm, act, f = saved
    ln2_g = w["ln2_g"] if token is None else w["ln2_g"] + token[0, 0]
    (dx1, df), (dg2, db2) = _rw_bwd(_f_ln_res, [x1, f], [ln2_g, w["ln2_b"]], [dx2], row_grad=[1, 1],
                                    param_grad=[1, 1], tm=TM_ROW, name="ln_res_bwd", row_dtypes=[F32, BF16])
    dhm = _mm_relu2_grad(df, w["w2"], hm, name="mlp_dhm")
    dw2 = _mm(act, df, ta=True, name="mlp_dw2")
    dx1 = _mm(dhm, w["w1"], tb=True, acc=dx1, name="mlp_dx")
    dw1 = _mm(x1b, dhm, ta=True, out_blocks=N_CHIPS, name="mlp_dw1")
    (dx, dh), (dg1, db1) = _rw_bwd(_f_ln_res, [x, h], [w["ln1_g"], w["ln1_b"]], [dx1], row_grad=[1, 1],
                                   param_grad=[1, 1], tm=TM_ROW, name="ln_res_bwd", row_dtypes=[F32, BF16])
    dmix =_mm(dh, w["wo_mix"], tb=True, name="wo_dmix")
    dcross = _mm(dh, w["wo_cross"], tb=True, name="wo_dcross")
    dwo = jnp.concatenate([_mm(mix, dh, ta=True, name="wo_dw_mix"), _mm(cross, dh, ta=True, name="wo_dw_cross")], 0)
    (dxq,), (dkmem, dvmem) = _rw_bwd(_f_attn, [xq], [kmem, vmem], [dcross], row_grad=[1], param_grad=[1, 1],
                                     tm=TM_ROW, name="attn_bwd", row_dtypes=[BF16])
    dwkv = _mm(mem, jnp.concatenate([dkmem, dvmem], axis=1), ta=True, name="mem_dw_kv")
    dx = _mm(dxq, w["wxq"], tb=True, acc=dx, name="dx_xq")
    dwxq = _mm(xb, dxq, ta=True, name="dw_xq")
    mixer_token = None if before_mixer is None else before_mixer(dict(w_kv_mem=dwkv, w_o=dwo, mlp_w1=dw1, mlp_w2=dw2))
    dx, mg = (_gdn_bwd if is_gdn else _s5_bwd)(xb, w, msave, dmix, dx, mixer_token)
    grads = dict(mixer=mg, wxq=dwxq, wkv=dwkv, wo=dwo, w1=dw1, w2=dw2,
                 ln1_g=dg1[0], ln1_b=db1[0], ln2_g=dg2[0], ln2_b=db2[0])
    return dx, grads


def _loss_and_grad(y, target):
    def f(yv, tv):
        err = yv - tv
        return (err * (1.0 / D_MODEL),), (0.5 / D_MODEL * jnp.sum(err * err, axis=0, keepdims=True),)

    (dy,), (part,) = _rowwise(f, [y, target], [], [(D_MODEL, F32)], [((1, D_MODEL), F32)], tm=TM_ROW, name="loss")
    return jnp.sum(part), dy


def _layer_weights(full, i):
    j = i // 2
    w = dict(wkv=full["w_kv_mem"][i].astype(BF16),
             wo_mix=full["w_o"][i][:D_MODEL].astype(BF16), wo_cross=full["w_o"][i][D_MODEL:].astype(BF16),
             ln1_g=full["ln1_g"][i][None], ln1_b=full["ln1_b"][i][None],
             ln2_g=full["ln2_g"][i][None], ln2_b=full["ln2_b"][i][None],
             w1=full["mlp_w1"][i].astype(BF16), w2=full["mlp_w2"][i].astype(BF16))
    if i % 2 == 0:
        w_in = full["gdn_w_in"][j]
        gd = 3 * D_MODEL
        w.update(wqkv=w_in[:, :gd].astype(BF16), wz=w_in[:, gd:gd + D_MODEL].astype(BF16),
                 wba=jnp.pad(w_in[:, gd + D_MODEL:gd + D_MODEL + 2 * GDN_HEADS],
                             ((0, 0), (0, LANES - 2 * GDN_HEADS))).astype(BF16),
                 wxq=w_in[:, gd + D_MODEL + 2 * GDN_HEADS:].astype(BF16),
                 conv_w=full["gdn_conv_w"][j],
                 a_log8=jnp.broadcast_to(full["gdn_a_log"][j][:, None], (GDN_HEADS, LANES)),
                 dt_bias8=jnp.broadcast_to(full["gdn_dt_bias"][j][:, None], (GDN_HEADS, LANES)),
                 norm_g=full["gdn_norm_g"][j][None])
    else:
        w_in = full["s5_w_in"][j]
        (kx, ec, fc, a_t), prep_vjp = full["s5_prep"][j] if "s5_prep" in full else jax.vjp(
            _s5_prep, full["s5_a_re"][j], full["s5_a_im"][j], full["s5_b_re"][j], full["s5_b_im"][j],
            full["s5_c_re"][j], full["s5_c_im"][j], full["s5_log_dt"][j])
        w.update(wu=w_in[:, :D_MODEL].astype(BF16), wxq=w_in[:, D_MODEL:].astype(BF16),
                 kx=kx, ec=ec, fc=fc, a_t=a_t, prep_vjp=prep_vjp,
                 d=full["s5_d"][j][None], w_glu=full["s5_w_glu"][j].astype(BF16), b_glu=full["s5_b_glu"][j][None])
    return w


def _sharded_grads(l, i):
    m = l["mixer"]
    out = dict(w_kv_mem=l["wkv"], w_o=l["wo"], mlp_w1=l["w1"], mlp_w2=l["w2"])
    if i % 2 == 0:
        out.update(gdn_w_in=jnp.concatenate([m["wqkv"], m["wz"], m["wba"][:, :2 * GDN_HEADS], l["wxq"]], axis=1),
                   gdn_conv_w=m["conv_w"])
    else:
        out.update(s5_w_in=jnp.concatenate([m["wu"], l["wxq"]], axis=1), s5_d=m["d"], s5_w_glu=m["w_glu"],
                   s5_b_glu=m["b_glu"])
    return out


def _replicated_grads(layer_grads):
    g = layer_grads
    gdn = [g[i]["mixer"] for i in range(DEPTH) if i % 2 == 0]
    s5 = [g[i]["mixer"] for i in range(DEPTH) if i % 2 == 1]
    out = {n: jnp.stack([l[n] for l in g]) for n in ("ln1_g", "ln1_b", "ln2_g", "ln2_b")}
    out.update({"gdn_" + n: jnp.stack([m[n] for m in gdn]) for n in ("a_log", "dt_bias", "norm_g")})
    out.update({"s5_" + n: jnp.stack([m[n] for m in s5])
                for n in ("a_re", "a_im", "b_re", "b_im", "c_re", "c_im", "log_dt")})
    return out


def _local_step(x, mem, target, weights_of, grads_ready, before_first_mixer):
    lw, saves = [], []
    h, hb = x, x.astype(BF16)
    for i in range(DEPTH):
        lw.append(weights_of(i, h))
        h, hb, s = _layer_fwd(h, hb, mem, lw[i], i % 2 == 0)
        saves.append(s)
    loss, d = _loss_and_grad(h, target)
    grads = [None] * DEPTH
    token = None
    for i in reversed(range(DEPTH)):
        d, grads[i] = _layer_bwd(mem, lw[i], i % 2 == 0, saves[i], d, token, None if i else before_first_mixer)
        token = grads_ready(i, grads[i])
    return loss, d, grads


ANY = pl.BlockSpec(memory_space=pl.ANY)
SHARD_ROWS = 1024
SMALL_ROWS = 128


def _place():
    return lax.axis_index("x"), lax.axis_index("y"), lax.axis_index("c")


def _other_chips(x, y):
    return [(1 - x, y), (x, 1 - y), (1 - x, 1 - y)]


def _all_gather_chips(wpack, *, name):
    rows = wpack.shape[0]
    half = rows // 2

    def body(w_ref, out_ref, send_sems, recv_sems):
        x, y, c = _place()
        sibling = (x, y, 1 - c)
        chips = _other_chips(x, y)

        def blk(cx, cy, cc):
            return out_ref.at[2 * cx + cy, pl.ds(cc * half, half), :]

        def copy(k, src, dst, to):
            return pltpu.make_async_remote_copy(src_ref=src, dst_ref=dst, send_sem=send_sems.at[k],
                                                recv_sem=recv_sems.at[k], device_id=to, device_id_type=MESH)

        first = [copy(j, w_ref.at[pl.ds(c * half, half), :], blk(x, y, c), (cx, cy, c))
                 for j, (cx, cy) in enumerate(chips)]
        for cp in first:
            cp.start()
        passed = [copy(3 + j, blk(cx, cy, c), blk(cx, cy, c), sibling) for j, (cx, cy) in enumerate(chips)]
        for j, (cx, cy) in enumerate(chips):
            copy(j, blk(cx, cy, c), blk(cx, cy, c), (cx, cy, c)).wait_recv()
            passed[j].start()
        for j, (cx, cy) in enumerate(chips):
            copy(3 + j, blk(cx, cy, 1 - c), blk(cx, cy, 1 - c), sibling).wait_recv()
        for cp in first + passed:
            cp.wait_send()

    return pl.pallas_call(
        body, name=name, out_shape=jax.ShapeDtypeStruct((N_CHIPS, rows, D_MODEL), wpack.dtype),
        in_specs=[ANY], out_specs=ANY,
        scratch_shapes=[pltpu.SemaphoreType.DMA((6,)), pltpu.SemaphoreType.DMA((6,))],
    )(wpack)


HBM = pl.BlockSpec(memory_space=pltpu.HBM)
SEM = pl.BlockSpec(memory_space=pltpu.SEMAPHORE)
DATAFLOW = pltpu.SideEffectType.DATAFLOW_SIDE_EFFECTING


def _gather_ici_copies(w_ref, land_ref, send_sems, recv_sems, outgoing):
    x, y, c = _place()
    half = w_ref.shape[0] // 2
    mine = pl.ds(c * half, half)
    return [pltpu.make_async_remote_copy(
        src_ref=w_ref.at[mine, :], dst_ref=land_ref.at[2 * x + y if outgoing else 2 * cx + cy, mine, :],
        send_sem=send_sems.at[j], recv_sem=recv_sems.at[j], device_id=(cx, cy, c), device_id_type=MESH)
        for j, (cx, cy) in enumerate(_other_chips(x, y))]


def _gather_start(wpack, after, *, name):
    rows = wpack.shape[0]

    def body(w_ref, land_ref, after_ref, send_sems, recv_sems, w_thru, land_thru, token):
        for cp in _gather_ici_copies(w_ref, land_ref, send_sems, recv_sems, outgoing=True):
            cp.start()
        token[...] = jnp.zeros_like(token)

    land = pltpu.with_memory_space_constraint(lax.empty((N_CHIPS, rows, D_MODEL), wpack.dtype), pltpu.HBM)
    return pl.pallas_call(
        body, name=name,
        out_shape=(pltpu.SemaphoreType.DMA((3,)), pltpu.SemaphoreType.DMA((3,)), pltpu.HBM(wpack.shape, wpack.dtype),
                   pltpu.HBM(land.shape, land.dtype), jax.ShapeDtypeStruct((SUBLANES, LANES), F32)),
        in_specs=(HBM, HBM, ANY), out_specs=(SEM, SEM, HBM, HBM, pl.BlockSpec(memory_space=pltpu.VMEM)),
        input_output_aliases={0: 2, 1: 3},
        compiler_params=pltpu.CompilerParams(has_side_effects=DATAFLOW),
    )(pltpu.with_memory_space_constraint(wpack, pltpu.HBM), land, after)


def _gather_wait(send_sems, recv_sems, w_thru, land_thru, after, *, name):
    def body(w_ref, land_ref, send_sems, recv_sems, after_ref, w_dead, land_out):
        for cp in _gather_ici_copies(w_ref, land_ref, send_sems, recv_sems, outgoing=False):
            cp.wait_send()
            cp.wait_recv()

    return pl.pallas_call(
        body, name=name,
        out_shape=(pltpu.HBM(w_thru.shape, w_thru.dtype), pltpu.HBM(land_thru.shape, land_thru.dtype)),
        in_specs=(HBM, HBM, SEM, SEM, ANY), out_specs=(HBM, HBM), input_output_aliases={0: 0, 1: 1},
        compiler_params=pltpu.CompilerParams(has_side_effects=DATAFLOW),
    )(w_thru, land_thru, send_sems, recv_sems, after)[1]


def _gather_forward(land, *, name):
    rows = land.shape[1]
    half = rows // 2

    def body(in_ref, out_ref, send_sems, recv_sems):
        x, y, c = _place()

        def copy(j, cx, cy, cc):
            rows_of = out_ref.at[2 * cx + cy, pl.ds(cc * half, half), :]
            return pltpu.make_async_remote_copy(src_ref=rows_of, dst_ref=rows_of, send_sem=send_sems.at[j],
                                                recv_sem=recv_sems.at[j], device_id=(x, y, 1 - c), device_id_type=MESH)

        sends = [copy(j, cx, cy, c) for j, (cx, cy) in enumerate(_other_chips(x, y))]
        for cp in sends:
            cp.start()
        for j, (cx, cy) in enumerate(_other_chips(x, y)):
            copy(j, cx, cy, 1 - c).wait_recv()
        for cp in sends:
            cp.wait_send()

    return pl.pallas_call(
        body, name=name, out_shape=jax.ShapeDtypeStruct(land.shape, land.dtype), in_specs=[ANY], out_specs=ANY,
        input_output_aliases={0: 0},
        scratch_shapes=[pltpu.SemaphoreType.DMA((3,)), pltpu.SemaphoreType.DMA((3,))],
    )(land)


def _sibling_swap(buf, *, name):
    def body(in_ref, out_ref, send_sem, recv_sem):
        x, y, c = _place()
        cp = pltpu.make_async_remote_copy(src_ref=in_ref, dst_ref=out_ref, send_sem=send_sem, recv_sem=recv_sem,
                                          device_id=(x, y, 1 - c), device_id_type=MESH)
        cp.start()
        cp.wait()

    return pl.pallas_call(
        body, name=name, out_shape=jax.ShapeDtypeStruct(buf.shape, buf.dtype), in_specs=[ANY], out_specs=ANY,
        scratch_shapes=[pltpu.SemaphoreType.DMA, pltpu.SemaphoreType.DMA],
    )(buf)


def _pair_exchange(gpack, *, name):
    pieces, rows, width = gpack.shape
    half = rows // 2

    def body(in_ref, got_ref, send_sems, recv_sems):
        x, y, c = _place()
        sends = [pltpu.make_async_remote_copy(src_ref=in_ref.at[p, pl.ds((1 - c) * half, half), :],
                                              dst_ref=got_ref.at[p], send_sem=send_sems.at[p],
                                              recv_sem=recv_sems.at[p], device_id=(x, y, 1 - c), device_id_type=MESH)
                 for p in range(pieces)]
        for cp in sends:
            cp.start()
        for cp in sends:
            cp.wait()

    return pl.pallas_call(
        body, name=name, out_shape=jax.ShapeDtypeStruct((pieces, half, width), gpack.dtype),
        in_specs=[ANY], out_specs=ANY,
        scratch_shapes=[pltpu.SemaphoreType.DMA((pieces,)), pltpu.SemaphoreType.DMA((pieces,))],
    )(gpack)


def _pair_add(gpack, got, c, *, name, tm=512):
    pieces, rows, width = gpack.shape
    half = rows // 2
    nb = half // tm

    def body(c_ref, a_ref, b_ref, sum_ref, narrow_ref):
        s = a_ref[...] + b_ref[...]
        sum_ref[...] = s
        narrow_ref[...] = s.astype(BF16)

    blk = pl.BlockSpec((None, tm, width), lambda p, i, c_ref: (p, i, 0))
    return pl.pallas_call(
        body, name=name,
        out_shape=(jax.ShapeDtypeStruct((pieces, half, width), F32), jax.ShapeDtypeStruct((pieces, half, width), BF16)),
        grid_spec=pltpu.PrefetchScalarGridSpec(
            num_scalar_prefetch=1, grid=(pieces, nb),
            in_specs=[pl.BlockSpec((None, tm, width), lambda p, i, c_ref: (p, c_ref[0] * nb + i, 0)), blk],
            out_specs=(blk, blk)),
        compiler_params=_params(("parallel", "parallel")),
    )(c, gpack, got)


def _chip_exchange(pieces, *, name):
    _, rows, width = pieces.shape

    def body(in_ref, out_ref, send_sems, recv_sems):
        x, y, c = _place()
        cps = [pltpu.make_async_remote_copy(src_ref=in_ref.at[2 * cx + cy], dst_ref=out_ref.at[j],
                                            send_sem=send_sems.at[j], recv_sem=recv_sems.at[j],
                                            device_id=(cx, cy, c), device_id_type=MESH)
               for j, (cx, cy) in enumerate(_other_chips(x, y))]
        for cp in cps:
            cp.start()
        for cp in cps:
            cp.wait()

    return pl.pallas_call(
        body, name=name, out_shape=jax.ShapeDtypeStruct((3, rows, width), pieces.dtype), in_specs=[ANY], out_specs=ANY,
        scratch_shapes=[pltpu.SemaphoreType.DMA((3,)), pltpu.SemaphoreType.DMA((3,))],
    )(pieces)


def _chip_exchange_copies(in_ref, land_ref, send_sems, recv_sems):
    x, y, c = _place()
    return [pltpu.make_async_remote_copy(src_ref=in_ref.at[2 * cx + cy], dst_ref=land_ref.at[j],
                                         send_sem=send_sems.at[j], recv_sem=recv_sems.at[j],
                                         device_id=(cx, cy, c), device_id_type=MESH)
            for j, (cx, cy) in enumerate(_other_chips(x, y))]


def _chip_exchange_start(pieces):
    _, rows, width = pieces.shape

    def body(in_ref, land_ref, send_sems, recv_sems, in_thru, land_thru, token):
        for cp in _chip_exchange_copies(in_ref, land_ref, send_sems, recv_sems):
            cp.start()
        token[...] = jnp.zeros_like(token)

    land = pltpu.with_memory_space_constraint(lax.empty((3, rows, width), pieces.dtype), pltpu.HBM)
    return pl.pallas_call(
        body, name="rs_chip_start",
        out_shape=(pltpu.SemaphoreType.DMA((3,)), pltpu.SemaphoreType.DMA((3,)), pltpu.HBM(pieces.shape, pieces.dtype),
                   pltpu.HBM(land.shape, land.dtype), jax.ShapeDtypeStruct((SUBLANES, LANES), F32)),
        in_specs=(HBM, HBM), out_specs=(SEM, SEM, HBM, HBM, pl.BlockSpec(memory_space=pltpu.VMEM)),
        input_output_aliases={0: 2, 1: 3},
        compiler_params=pltpu.CompilerParams(has_side_effects=DATAFLOW),
    )(pltpu.with_memory_space_constraint(pieces, pltpu.HBM), land)


def _chip_exchange_wait(send_sems, recv_sems, in_thru, land_thru, after):
    def body(in_ref, land_ref, send_sems, recv_sems, after_ref, in_dead, land_out):
        for cp in _chip_exchange_copies(in_ref, land_ref, send_sems, recv_sems):
            cp.wait_send()
            cp.wait_recv()

    return pl.pallas_call(
        body, name="rs_chip_wait",
        out_shape=(pltpu.HBM(in_thru.shape, in_thru.dtype), pltpu.HBM(land_thru.shape, land_thru.dtype)),
        in_specs=(HBM, HBM, SEM, SEM, ANY), out_specs=(HBM, HBM), input_output_aliases={0: 0, 1: 1},
        compiler_params=pltpu.CompilerParams(has_side_effects=DATAFLOW),
    )(in_thru, land_thru, send_sems, recv_sems, after)[1]


def _all_reduce_small(v, *, name):
    rows, width = v.shape
    half = rows // 2
    assert half % SUBLANES == 0

    def body(in_ref, out_ref, pair_buf, chip_buf, send_sems, recv_sems):
        x, y, c = _place()
        sibling = (x, y, 1 - c)
        me = 2 * x + y
        mine = pl.ds(pl.multiple_of(c * half, SUBLANES), half)
        other = pl.ds(pl.multiple_of((1 - c) * half, SUBLANES), half)

        def copy(k, src, dst, to):
            return pltpu.make_async_remote_copy(src_ref=src, dst_ref=dst, send_sem=send_sems.at[k],
                                                recv_sem=recv_sems.at[k], device_id=to, device_id_type=MESH)

        swap = copy(0, in_ref.at[other, :], pair_buf, sibling)
        swap.start()
        swap.wait()
        chip_buf[me] = in_ref[mine, :] + pair_buf[...]
        chips = _other_chips(x, y)
        for j, (cx, cy) in enumerate(chips):
            copy(1 + j, chip_buf.at[me], chip_buf.at[me], (cx, cy, c)).start()
        for j, (cx, cy) in enumerate(chips):
            got = copy(1 + j, chip_buf.at[me], chip_buf.at[2 * cx + cy], (cx, cy, c))
            got.wait_send()
            got.wait_recv()
        out_ref[mine, :] = ((chip_buf[0] + chip_buf[1]) + chip_buf[2]) + chip_buf[3]
        share = copy(1 + len(chips), out_ref.at[mine, :], out_ref.at[mine, :], sibling)
        share.start()
        share.wait_send()
        copy(1 + len(chips), out_ref.at[other, :], out_ref.at[other, :], sibling).wait_recv()

    vmem = pl.BlockSpec(memory_space=pltpu.VMEM)
    return pl.pallas_call(
        body, name=name, out_shape=jax.ShapeDtypeStruct(v.shape, v.dtype), in_specs=[vmem], out_specs=vmem,
        scratch_shapes=[pltpu.VMEM((half, width), v.dtype), pltpu.VMEM((N_CHIPS, half, width), v.dtype),
                        pltpu.SemaphoreType.DMA((5,)), pltpu.SemaphoreType.DMA((5,))],
        compiler_params=pltpu.CompilerParams(vmem_limit_bytes=VMEM_LIMIT_V7X),
    )(v)


def _reduce_scatter_begin(gpack, behind):
    x, y, c = _place()
    got = _pair_exchange(gpack, name="rs_pair_swap")
    pair, pair16 = _pair_add(gpack, got, c.astype(jnp.int32).reshape(1), name="rs_pair_add")
    mine = lax.dynamic_index_in_dim(pair, 2 * x + y, axis=0, keepdims=False)
    if behind:
        *in_flight, token = _chip_exchange_start(pair16)
        return dict(mine=mine, in_flight=in_flight), token
    return dict(mine=mine, recv=_chip_exchange(pair16, name="rs_chip_exchange")), None


def _reduce_scatter_end(state, after=None):
    c = lax.axis_index("c")
    recv = state["recv"] if "recv" in state else _chip_exchange_wait(*state["in_flight"], after=after)
    (total,) = _rw_fwd(_f_add4, [state["mine"], recv[0], recv[1], recv[2]], [], tm=512, name="rs_chip_add")
    theirs = _sibling_swap(total, name="rs_share_swap")
    return jnp.concatenate([jnp.where(c == 0, total, theirs), jnp.where(c == 0, theirs, total)], axis=0)


_SHARDED = (("w_kv_mem", 1), ("w_o", 1), ("mlp_w1", 2), ("mlp_w2", 1), ("gdn_w_in", 2), ("gdn_conv_w", 2),
            ("s5_w_in", 2), ("s5_d", 1), ("s5_w_glu", 1), ("s5_b_glu", 1))
_MATMUL_ONLY = ("w_kv_mem", "w_o", "mlp_w1", "mlp_w2", "gdn_w_in", "s5_w_in", "s5_w_glu")
_KEPT_BLOCKED = ("mlp_w1",)
_REPLICATED = ("ln1_g", "ln1_b", "ln2_g", "ln2_b", "gdn_a_log", "gdn_dt_bias", "gdn_norm_g", "s5_a_re", "s5_a_im",
               "s5_b_re", "s5_b_im", "s5_c_re", "s5_c_im", "s5_log_dt")
_WEIGHTS = ("w_kv_mem", "w_o", "ln1_g", "ln1_b", "ln2_g", "ln2_b", "mlp_w1", "mlp_w2", "gdn_w_in", "gdn_conv_w",
            "gdn_a_log", "gdn_dt_bias", "gdn_norm_g", "s5_w_in", "s5_a_re", "s5_a_im", "s5_b_re", "s5_b_im",
            "s5_c_re", "s5_c_im", "s5_log_dt", "s5_d", "s5_w_glu", "s5_b_glu")


ROW_ALIGN = 16


def _n_rows(shape):
    return -(-math.prod(shape) // (ROW_ALIGN * D_MODEL)) * ROW_ALIGN


def _as_rows(a):
    rows = _n_rows(a.shape)
    if a.shape[-1] == D_MODEL and a.size == rows * D_MODEL:
        return a.reshape(-1, D_MODEL)
    flat = a.reshape(-1)
    return jnp.pad(flat, (0, rows * D_MODEL - flat.size)).reshape(rows, D_MODEL)


def _pack(arrs, unit_rows=SHARD_ROWS):
    rows = [_as_rows(a) for a in arrs]
    pad = -sum(r.shape[0] for r in rows) % unit_rows
    if pad:
        rows.append(jnp.zeros((pad, D_MODEL), rows[0].dtype))
    return jnp.concatenate(rows, axis=0)


def _unpack(packed, shapes):
    lead = packed.shape[:-2]
    out, off = [], 0
    for s in shapes:
        r = _n_rows(s)
        seg = lax.slice_in_dim(packed, off, off + r, axis=len(lead))
        if s[-1] != D_MODEL or math.prod(s) != r * D_MODEL:
            seg = lax.slice_in_dim(seg.reshape(lead + (-1,)), 0, math.prod(s), axis=len(lead))
        out.append(seg.reshape(lead + tuple(s)))
        off += r
    return out


def _split3(t):
    hi = t.astype(BF16)
    r1 = t - hi.astype(F32)
    mid = r1.astype(BF16)
    lo = (r1 - mid.astype(F32)).astype(BF16)
    return jnp.stack([hi, mid, lo], axis=-1)


def _join3(t):
    return (t[..., 0].astype(F32) + t[..., 1].astype(F32)) + t[..., 2].astype(F32)


def _merge_chips(blocks, axis):
    return jnp.concatenate([blocks[s] for s in range(N_CHIPS)], axis=axis)


def _pack_for_chips(weights):
    rows = []
    for s in range(N_CHIPS):
        chip = []
        for layers, axis in weights:
            if axis is None:
                blocks = [g[s] for g in layers]
            else:
                n = layers[0].shape[axis] // N_CHIPS
                blocks = [lax.slice_in_dim(g, s * n, (s + 1) * n, axis=axis) for g in layers]
            if math.prod(blocks[0].shape) % (ROW_ALIGN * D_MODEL) == 0:
                chip += [_as_rows(b) for b in blocks]
            else:
                chip.append(_as_rows(jnp.stack(blocks)))
        pad = -sum(r.shape[0] for r in chip) % SHARD_ROWS
        rows += chip + ([jnp.zeros((pad, D_MODEL), F32)] if pad else [])
    return jnp.concatenate(rows, axis=0).reshape(N_CHIPS, -1, D_MODEL)


def kernel(x, mem, w_kv_mem, w_o, ln1_g, ln1_b, ln2_g, ln2_b, mlp_w1, mlp_w2, gdn_w_in, gdn_conv_w, gdn_a_log, gdn_dt_bias, gdn_norm_g, s5_w_in, s5_a_re, s5_a_im, s5_b_re, s5_b_im, s5_c_re, s5_c_im, s5_log_dt, s5_d, s5_w_glu, s5_b_glu, loss_target, m_w_kv_mem, m_w_o, m_ln1_g, m_ln1_b, m_ln2_g, m_ln2_b, m_mlp_w1, m_mlp_w2, m_gdn_w_in, m_gdn_conv_w, m_gdn_a_log, m_gdn_dt_bias, m_gdn_norm_g, m_s5_w_in, m_s5_a_re, m_s5_a_im, m_s5_b_re, m_s5_b_im, m_s5_c_re, m_s5_c_im, m_s5_log_dt, m_s5_d, m_s5_w_glu, m_s5_b_glu, v_w_kv_mem, v_w_o, v_ln1_g, v_ln1_b, v_ln2_g, v_ln2_b, v_mlp_w1, v_mlp_w2, v_gdn_w_in, v_gdn_conv_w, v_gdn_a_log, v_gdn_dt_bias, v_gdn_norm_g, v_s5_w_in, v_s5_a_re, v_s5_a_im, v_s5_b_re, v_s5_b_im, v_s5_c_re, v_s5_c_im, v_s5_log_dt, v_s5_d, v_s5_w_glu, v_s5_b_glu):
    given = dict(locals())
    w = {n: given[n] for n in _WEIGHTS}
    mom = {n: given["m_" + n] for n in _WEIGHTS}
    var = {n: given["v_" + n] for n in _WEIGHTS}
    shard_names = [n for n, _ in _SHARDED]
    shard_shapes = [w[n].shape for n in shard_names]
    rep_shapes = [w[n].shape for n in _REPLICATED]

    wire = {n: w[n].astype(BF16) if n in _MATMUL_ONLY else _split3(w[n]) for n in shard_names}
    first = {n: 0 if n.startswith("s5_") else 1 for n in shard_names}
    me_chip = 2 * lax.axis_index("x") + lax.axis_index("y")
    early = [wire[n][:first[n]] for n in shard_names if first[n]]
    late = [wire[n][first[n]:] for n in shard_names]
    early_pack, late_pack = _pack(early), _pack(late)
    *first_copies, first_token = _gather_start(early_pack, after=w["s5_log_dt"], name="gather_first_start")
    s5_names = ("s5_a_re", "s5_a_im", "s5_b_re", "s5_b_im", "s5_c_re", "s5_c_im")
    s5_preps = [jax.vjp(_s5_prep, *[w[n][j] for n in s5_names], w["s5_log_dt"][j] + first_token[0, 0])
                for j in range(w["s5_log_dt"].shape[0])]
    landed = _gather_wait(*first_copies, after=s5_preps[-1][0][3], name="gather_first_wait")
    landed = _gather_forward(landed, name="gather_first_forward")
    landed = lax.dynamic_update_index_in_dim(landed, early_pack, me_chip, axis=0)
    early_blocks = dict(zip([n for n in shard_names if first[n]], _unpack(landed, [a.shape for a in early])))
    send_sems, recv_sems, pack_thru, land_thru, token = _gather_start(late_pack, after=landed, name="gather_start")
    axis_of = dict(_SHARDED)

    def merged(n, blk):
        if n in _KEPT_BLOCKED:
            return blk
        return _merge_chips(blk if n in _MATMUL_ONLY else _join3(blk), axis_of[n] - 1)

    late_full = {}

    def weights_of(i, h):
        if i == 0:
            full = {n: [merged(n, blk[:, 0])] for n, blk in early_blocks.items()}
            full["gdn_w_in"][0] = full["gdn_w_in"][0] + token[0, 0].astype(BF16)
        else:
            if not late_full:
                land = _gather_wait(send_sems, recv_sems, pack_thru, land_thru, after=h, name="gather_wait")
                land = _gather_forward(land, name="gather_forward")
                land = lax.dynamic_update_index_in_dim(land, late_pack, me_chip, axis=0)
                for n, blk in zip(shard_names, _unpack(land, [a.shape for a in late])):
                    late_full[n] = [None] * first[n] + [merged(n, blk[:, t]) for t in range(blk.shape[1])]
            full = dict(late_full)
        full.update({n: w[n] for n in _REPLICATED})
        full["s5_prep"] = s5_preps
        return _layer_weights(full, i)

    sharded = {}
    in_flight = {}
    first_mixer, first_outer = (0, "mixer"), (0, "outer")

    def group_pack(parts):
        names = [n for n in shard_names if any(n in sharded[i] for i in parts)]
        per_weight = [[sharded[i][n] for i in parts if n in sharded[i]] for n in names]
        pack = _pack_for_chips([(g, None if n in _KEPT_BLOCKED else axis_of[n] - 1) for n, g in zip(names, per_weight)])
        return pack, names, [(len(g),) + w[n].shape[1:] for n, g in zip(names, per_weight)]

    def grads_ready(i, g):
        by_weight = _sharded_grads(g, i)
        if i:
            sharded[i] = by_weight
        else:
            sharded[first_mixer] = {n: g for n, g in by_weight.items() if n not in sharded[first_outer]}

    def before_first_mixer(outer):
        sharded[first_outer] = outer
        pack, names, shapes = group_pack([first_outer] + list(range(1, DEPTH)))
        state, token = _reduce_scatter_begin(pack, behind=True)
        in_flight.update(state=state, names=names, shapes=shapes)
        return token

    loss, grad_x, layer_grads = _local_step(x[0], mem[0], loss_target[0], weights_of, grads_ready,
                                            before_first_mixer)
    loss = lax.psum(loss, ("x", "y", "c"))
    pack, names, shapes = group_pack([first_mixer])
    state, _ = _reduce_scatter_begin(pack, behind=False)
    pieces = {n: [] for n in shard_names}
    for n, g in zip(names, _unpack(_reduce_scatter_end(state), shapes)):
        pieces[n].append(g)
    late = _reduce_scatter_end(in_flight["state"], after=grad_x)
    for n, g in zip(in_flight["names"], _unpack(late, in_flight["shapes"])):
        pieces[n].append(g)
    g_shards = [p[0] if len(p) == 1 else jnp.concatenate(p, axis=0) for p in (pieces[n] for n in shard_names)]

    def pack_small(d):
        return _pack([d[n] for n in _REPLICATED], unit_rows=SMALL_ROWS)

    g_rep = _all_reduce_small(pack_small(_replicated_grads(layer_grads)), name="reduce_replicated")

    def adamw(wp, gp, mp, vp, name):
        return _rw_fwd(_f_adamw, [wp, gp, mp, vp], [], tm=TM_WIDE, name=name)

    outs = {}
    for n, g in zip(shard_names, g_shards):
        flat = (-1, w[n].shape[-1])
        res = adamw(w[n].reshape(flat), g.reshape(flat), mom[n].reshape(flat), var[n].reshape(flat), "adamw_" + n)
        outs[("grad", n)] = g
        outs.update({(kind, n): a.reshape(w[n].shape) for kind, a in zip(("delta", "new_m", "new_v"), res)})
    packed = (g_rep,) + tuple(adamw(pack_small(w), g_rep, pack_small(mom), pack_small(var), "adamw_replicated"))
    for kind, pr in zip(("grad", "delta", "new_m", "new_v"), packed):
        outs.update({(kind, n): a for n, a in zip(_REPLICATED, _unpack(pr, rep_shapes))})
    return (loss, grad_x[None]) + tuple(outs[(kind, n)] for kind in ("grad", "delta", "new_m", "new_v")
                                        for n in _WEIGHTS)
```

```python
import functools
import math

import jax
import jax.numpy as jnp
from jax import lax
from jax.experimental import pallas as pl
from jax.experimental.pallas import tpu as pltpu

F32 = jnp.float32
BF16 = jnp.bfloat16
MESH = pl.DeviceIdType.MESH

D_MODEL = 1024
DEPTH = 4
GDN_HEADS = 8
HEAD_DIM = 128
GDN_CONV = 4
GDN_CHUNK = 64
S5_GROUP = 16
S5_STATE = 64
XA_HEADS = 4
XA_DIM = 512
DN_ALPHA = (2 * DEPTH) ** 0.25
LN_EPS = 1e-5
RMS_EPS = 1e-6
ADAM_LR, ADAM_B1, ADAM_B2, ADAM_EPS, ADAM_WD, ADAM_STEP = 0.001, 0.9, 0.999, 1e-08, 0.01, 10

VMEM_LIMIT_V7X = 56 * 1024 * 1024
LANES = 128
SUBLANES = 8
S5_T = 16
S5_TILES = D_MODEL // LANES
N_CHIPS = 4


def _params(sem):
    return pltpu.CompilerParams(dimension_semantics=sem, vmem_limit_bytes=VMEM_LIMIT_V7X)


def _tile(n, pref):
    if n <= pref:
        return n
    t = (pref // LANES) * LANES
    while n % t:
        t -= LANES
    return t


def _row_tile(n, pref):
    if n % SUBLANES:
        return n
    t = min(pref, n) // SUBLANES * SUBLANES
    while n % t:
        t -= SUBLANES
    return t


def _col_blocked_spec(rows_tile, cols_tile, block_cols, rows_axis, cols_axis):
    r = block_cols // cols_tile

    def index(*ijk):
        c = ijk[cols_axis]
        return (c, ijk[rows_axis], 0) if r == 1 else (c // r, ijk[rows_axis], c % r)

    return pl.BlockSpec((None, rows_tile, cols_tile), index)


def _mm(a, b, *, ta=False, tb=False, acc=None, name, tm=1024, tn=1024, tk=None, out_blocks=0):
    if tk is None:
        tk = 4096 if a.dtype == BF16 and b.dtype == BF16 else 2048
    k_dim, m_dim = a.shape if ta else a.shape[::-1]
    b_rows, b_cols = (b.shape[0], b.shape[1]) if b.ndim == 2 else (b.shape[1], b.shape[0] * b.shape[2])
    n_dim = b_rows if tb else b_cols
    assert (b_cols if tb else b_rows) == k_dim, (a.shape, b.shape, ta, tb)
    limit_n = n_dim // out_blocks if out_blocks else (n_dim if b.ndim == 2 or tb else b.shape[2])
    limit_k = b.shape[2] if (b.ndim == 3 and tb) else k_dim
    tm, tn, tk = _tile(m_dim, tm), _tile(limit_n, min(tn, limit_n)), _tile(limit_k, min(tk, limit_k))
    a_spec = (pl.BlockSpec((tk, tm), lambda i, j, k: (k, i)) if ta else pl.BlockSpec((tm, tk), lambda i, j, k: (i, k)))
    if b.ndim == 3:
        b_spec = (_col_blocked_spec(tn, tk, b.shape[2], 1, 2) if tb else _col_blocked_spec(tk, tn, b.shape[2], 2, 1))
    else:
        b_spec = (pl.BlockSpec((tn, tk), lambda i, j, k: (j, k)) if tb
                  else pl.BlockSpec((tk, tn), lambda i, j, k: (k, j)))
    o_spec = (_col_blocked_spec(tm, tn, n_dim // out_blocks, 0, 1) if out_blocks
              else pl.BlockSpec((tm, tn), lambda i, j, k: (i, j)))
    o_shape = (out_blocks, m_dim, n_dim // out_blocks) if out_blocks else (m_dim, n_dim)
    dn = (((0 if ta else 1,), (1 if tb else 0,)), ((), ()))
    has_acc = acc is not None

    def body(*refs):
        a_ref, b_ref = refs[0], refs[1]
        o_ref = refs[-1]
        k = pl.program_id(2)
        p = lax.dot_general(a_ref[...].astype(BF16), b_ref[...].astype(BF16), dn,
                            preferred_element_type=F32)

        @pl.when(k == 0)
        def _():
            o_ref[...] = p + refs[2][...] if has_acc else p

        @pl.when(k > 0)
        def _():
            o_ref[...] += p

    return pl.pallas_call(
        body, name=name,
        out_shape=jax.ShapeDtypeStruct(o_shape, F32),
        grid=(m_dim // tm, n_dim // tn, k_dim // tk),
        in_specs=[a_spec, b_spec] + ([o_spec] if has_acc else []),
        out_specs=o_spec,
        compiler_params=_params(("parallel", "parallel", "arbitrary")),
    )(*([a, b] + ([acc] if has_acc else [])))


def _mm_relu2(a, b, *, name, tm=1024):
    m_dim, k_dim = a.shape
    n_blocks, _, tn = b.shape
    n_dim = n_blocks * tn
    tm = _tile(m_dim, tm)

    def body(a_ref, b_ref, h_ref, act_ref):
        h = jnp.dot(a_ref[...].astype(BF16), b_ref[...].astype(BF16), preferred_element_type=F32)
        h_ref[...] = h.astype(h_ref.dtype)
        r = jnp.maximum(h, 0.0)
        act_ref[...] = (r * r).astype(BF16)

    o_spec = pl.BlockSpec((tm, tn), lambda i, j: (i, j))
    return pl.pallas_call(
        body, name=name,
        out_shape=(jax.ShapeDtypeStruct((m_dim, n_dim), BF16), jax.ShapeDtypeStruct((m_dim, n_dim), BF16)),
        grid=(m_dim // tm, n_dim // tn),
        in_specs=[pl.BlockSpec((tm, k_dim), lambda i, j: (i, 0)),
                  pl.BlockSpec((None, k_dim, tn), lambda i, j: (j, 0, 0))],
        out_specs=(o_spec, o_spec),
        compiler_params=_params(("parallel", "parallel")),
    )(a, b)


def _mm_relu2_grad(d, b, h, *, name, tm=1024, tn=1024):
    m_dim, k_dim = d.shape
    n_dim = b.shape[0]
    tm, tn = _tile(m_dim, tm), _tile(n_dim, tn)

    def body(d_ref, b_ref, h_ref, o_ref):
        p = lax.dot_general(d_ref[...].astype(BF16), b_ref[...].astype(BF16), ((NT), ((), ())),
                            preferred_element_type=F32)
        o_ref[...] = (p * (2.0 * jnp.maximum(h_ref[...].astype(F32), 0.0))).astype(BF16)

    o_spec = pl.BlockSpec((tm, tn), lambda i, j: (i, j))
    return pl.pallas_call(
        body, name=name,
        out_shape=jax.ShapeDtypeStruct((m_dim, n_dim), BF16),
        grid=(m_dim // tm, n_dim // tn),
        in_specs=[pl.BlockSpec((tm, k_dim), lambda i, j: (i, 0)), pl.BlockSpec((tn, k_dim), lambda i, j: (j, 0)), o_spec],
        out_specs=o_spec,
        compiler_params=_params(("parallel", "parallel")),
    )(d, b, h)


def _rowwise(f, rows, params, row_out, acc_out, *, tm, name):
    length = rows[0].shape[0]
    tm = _row_tile(length, tm)
    nr, npar, nro = len(rows), len(params), len(row_out)

    def body(*refs):
        ins = [r[...] for r in refs[:nr + npar]]
        outs = refs[nr + npar:]
        r_o, a_o = f(*ins)
        for ref, val in zip(outs[:nro], r_o):
            ref[...] = val.astype(ref.dtype)
        i = pl.program_id(0)
        for ref, val in zip(outs[nro:], a_o):
            @pl.when(i == 0)
            def _(ref=ref, val=val):
                ref[...] = val.astype(ref.dtype)

            @pl.when(i > 0)
            def _(ref=ref, val=val):
                ref[...] += val.astype(ref.dtype)

    in_specs = ([pl.BlockSpec((tm, r.shape[1]), lambda i: (i, 0)) for r in rows]
                + [pl.BlockSpec(p.shape, lambda i: (0, 0)) for p in params])
    out_specs = ([pl.BlockSpec((tm, w), lambda i: (i, 0)) for w, _ in row_out]
                 + [pl.BlockSpec(s, lambda i: (0, 0)) for s, _ in acc_out])
    out_shape = ([jax.ShapeDtypeStruct((length, w), dt) for w, dt in row_out]
                 + [jax.ShapeDtypeStruct(s, dt) for s, dt in acc_out])
    res = pl.pallas_call(
        body, name=name, out_shape=out_shape, grid=(length // tm,),
        in_specs=in_specs, out_specs=out_specs,
        compiler_params=_params(("arbitrary",) if acc_out else ("parallel",)),
    )(*rows, *params)
    return res[:nro], res[nro:]


def _rw_fwd(f, rows, params, *, tm, name, out_dtypes=None):
    tm_ = _row_tile(rows[0].shape[0], tm)
    shapes = jax.eval_shape(f, *[jax.ShapeDtypeStruct((tm_, r.shape[1]), r.dtype) for r in rows],
                            *[jax.ShapeDtypeStruct(p.shape, p.dtype) for p in params])
    row_out = [(s.shape[1], s.dtype if out_dtypes is None else dt)
               for s, dt in zip(shapes, out_dtypes or shapes)]
    outs, _ = _rowwise(lambda *v: (f(*v), ()), rows, params, row_out, [], tm=tm, name=name)
    return outs


def _rw_bwd(f, rows, params, cots, *, row_grad, param_grad, tm, name, row_dtypes=None):
    nr, npar, nct = len(rows), len(params), len(cots)

    def g(*vals):
        prim = vals[:nr] + vals[nr + nct:]
        ct = vals[nr:nr + nct]
        _, vjp = jax.vjp(f, *prim)
        grads = vjp(tuple(ct))
        return (tuple(grads[i] for i in range(nr) if row_grad[i]),
                tuple(grads[nr + i] for i in range(npar) if param_grad[i]))

    widths = [rows[i].shape[1] for i in range(nr) if row_grad[i]]
    row_out = list(zip(widths, row_dtypes or [F32] * len(widths)))
    acc_out = [(params[i].shape, F32) for i in range(npar) if param_grad[i]]
    return _rowwise(g, list(rows) + list(cots), params, row_out, acc_out, tm=tm, name=name)


def _f_ln_res(x, h, g, b):
    pre = DN_ALPHA * x + h
    mu = jnp.mean(pre, axis=-1, keepdims=True)
    d = pre - mu
    var = jnp.mean(d * d, axis=-1, keepdims=True)
    return (d * lax.rsqrt(var + LN_EPS) * g + b,)


def _silu(t):
    return t * jax.nn.sigmoid(t)


def _f_gdn_qkv(c):
    a = _silu(c)
    outs = []
    for part, scale in ((0, HEAD_DIM ** -0.5), (1, 1.0)):
        heads = []
        for h in range(GDN_HEADS):
            t = a[:, part * D_MODEL + h * HEAD_DIM: part * D_MODEL + (h + 1) * HEAD_DIM]
            t = t * lax.rsqrt(jnp.sum(t * t, axis=-1, keepdims=True) + 1e-6)
            heads.append(t * scale if scale != 1.0 else t)
        outs.append(jnp.concatenate(heads, axis=-1))
    outs.append(a[:, 2 * D_MODEL:])
    return tuple(outs)


def _f_gdn_out(o, z, norm_g):
    heads = []
    for h in range(GDN_HEADS):
        t = o[:, h * HEAD_DIM:(h + 1) * HEAD_DIM]
        t = t * lax.rsqrt(jnp.mean(t * t, axis=-1, keepdims=True) + RMS_EPS) * norm_g
        heads.append(t)
    return (jnp.concatenate(heads, axis=-1) * _silu(z),)


def _f_attn(xq, kmem, vmem):
    heads = []
    for h in range(XA_HEADS):
        sl = slice(h * HEAD_DIM, (h + 1) * HEAD_DIM)
        s = lax.dot_general(xq[:, sl].astype(BF16), kmem[:, sl].astype(BF16),
                            (((1,), (1,)), ((), ())), preferred_element_type=F32) * (HEAD_DIM ** -0.5)
        m = lax.stop_gradient(jnp.max(s, axis=-1, keepdims=True))
        e = jnp.exp(s - m)
        p = e / jnp.sum(e, axis=-1, keepdims=True)
        heads.append(jnp.dot(p.astype(BF16), vmem[:, sl].astype(BF16), preferred_element_type=F32))
    return (jnp.concatenate(heads, axis=-1),)


def _f_s5_gelu(y, u, d):
    return (jax.nn.gelu(y + d * u),)


def _f_s5_gate(zg, t, b):
    return (zg * jax.nn.sigmoid(t + b),)


def _f_add(a, b):
    return (a + b,)


def _f_add4(a, b, c, d):
    return (((a + b.astype(F32)) + c.astype(F32)) + d.astype(F32),)


def _f_adamw(w, g, m, v):
    m = ADAM_B1 * m + (1.0 - ADAM_B1) * g
    v = ADAM_B2 * v + (1.0 - ADAM_B2) * jnp.square(g)
    m_hat = m / (1.0 - ADAM_B1 ** ADAM_STEP)
    v_hat = v / (1.0 - ADAM_B2 ** ADAM_STEP)
    delta = -ADAM_LR * (m_hat / (jnp.sqrt(v_hat) + ADAM_EPS) + ADAM_WD * w)
    return delta, m, v


def _conv_fwd(u, w, *, tm, name):
    length, chans = u.shape
    tm = min(tm, length)
    tc = _tile(chans, 1024)
    hb = tm // SUBLANES

    def body(cur_ref, prev_ref, w_ref, o_ref, buf):
        i = pl.program_id(1)
        buf[0:SUBLANES, :] = jnp.where(i > 0, prev_ref[...], 0.0)
        buf[SUBLANES:, :] = cur_ref[...]
        acc = buf[pl.ds(SUBLANES - 3, tm), :] * w_ref[0:1, :]
        for k in range(1, GDN_CONV):
            acc = acc + buf[pl.ds(SUBLANES - 3 + k, tm), :] * w_ref[k:k + 1, :]
        o_ref[...] = acc

    return pl.pallas_call(
        body, name=name, out_shape=jax.ShapeDtypeStruct(u.shape, F32),
        grid=(chans // tc, length // tm),
        in_specs=[pl.BlockSpec((tm, tc), lambda j, i: (i, j)),
                  pl.BlockSpec((SUBLANES, tc), lambda j, i: (jnp.maximum(i * hb - 1, 0), j)),
                  pl.BlockSpec((GDN_CONV, tc), lambda j, i: (0, j))],
        out_specs=pl.BlockSpec((tm, tc), lambda j, i: (i, j)),
        scratch_shapes=[pltpu.VMEM((tm + SUBLANES, tc), F32)],
        compiler_params=_params(("parallel", "parallel")),
    )(u, u, w)


def _conv_bwd(u, w, dc, *, tm, name):
    length, chans = u.shape
    tm = min(tm, length)
    tc = _tile(chans, 1024)
    hb = tm // SUBLANES
    last = length // tm - 1

    def body(u_ref, uprev_ref, dc_ref, dcnext_ref, w_ref, du_ref, dw_ref, ubuf, dbuf):
        i = pl.program_id(1)
        ubuf[0:SUBLANES, :] = jnp.where(i > 0, uprev_ref[...], 0.0)
        ubuf[SUBLANES:, :] = u_ref[...]
        dbuf[0:tm, :] = dc_ref[...]
        dbuf[tm:, :] = jnp.where(i < last, dcnext_ref[...], 0.0)
        dcv = dc_ref[...]
        du = dbuf[pl.ds(3, tm), :] * w_ref[0:1, :]
        rows = [jnp.sum(dcv * ubuf[pl.ds(SUBLANES - 3, tm), :], axis=0, keepdims=True)]
        for k in range(1, GDN_CONV):
            du = du + dbuf[pl.ds(3 - k, tm), :] * w_ref[k:k + 1, :]
            rows.append(jnp.sum(dcv * ubuf[pl.ds(SUBLANES - 3 + k, tm), :], axis=0, keepdims=True))
        du_ref[...] = du.astype(du_ref.dtype)
        dwv = jnp.concatenate(rows, axis=0)

        @pl.when(i == 0)
        def _():
            dw_ref[...] = dwv

        @pl.when(i > 0)
        def _():
            dw_ref[...] += dwv

    return pl.pallas_call(
        body, name=name,
        out_shape=(jax.ShapeDtypeStruct(u.shape, BF16), jax.ShapeDtypeStruct((GDN_CONV, chans), F32)),
        grid=(chans // tc, length // tm),
        in_specs=[pl.BlockSpec((tm, tc), lambda j, i: (i, j)),
                  pl.BlockSpec((SUBLANES, tc), lambda j, i: (jnp.maximum(i * hb - 1, 0), j)),
                  pl.BlockSpec((tm, tc), lambda j, i: (i, j)),
                  pl.BlockSpec((SUBLANES, tc), lambda j, i: (jnp.minimum((i + 1) * hb, (last + 1) * hb - 1), j)),
                  pl.BlockSpec((GDN_CONV, tc), lambda j, i: (0, j))],
        out_specs=(pl.BlockSpec((tm, tc), lambda j, i: (i, j)),
                   pl.BlockSpec((GDN_CONV, tc), lambda j, i: (0, j))),
        scratch_shapes=[pltpu.VMEM((tm + SUBLANES, tc), F32), pltpu.VMEM((tm + SUBLANES, tc), F32)],
        compiler_params=_params(("parallel", "arbitrary")),
    )(u, u, dc, dc, w)


def _dot(a, b, dims, precision=None):
    if precision is None:
        a, b = a.astype(BF16), b.astype(BF16)
    return lax.dot_general(a, b, (dims, ((), ())), preferred_element_type=F32, precision=precision)


def _dot3(a, b, dims):
    ah, bh = a.astype(BF16), b.astype(BF16)
    al, bl = (a - ah.astype(F32)).astype(BF16), (b - bh.astype(F32)).astype(BF16)

    def d(x, y):
        return lax.dot_general(x, y, (dims, ((), ())), preferred_element_type=F32)

    return d(ah, bh) + (d(ah, bl) + d(al, bh))


NN = ((1,), (0,))
NT = ((1,), (1,))
TN = ((0,), (0,))
HI = lax.Precision.HIGHEST


def _hmap(f, *lists):
    return [f(*t) for t in zip(*lists)]


@jax.custom_vjp
def _unit_lower_inverse(a):
    c = a[0].shape[0]
    eye = (lax.broadcasted_iota(jnp.int32, (c, c), 0) == lax.broadcasted_iota(jnp.int32, (c, c), 1)).astype(F32)
    p = _hmap(lambda x: -x, a)
    t = _hmap(lambda x: eye + x, p)
    for _ in range(int(math.log2(c)) - 1):
        p = _hmap(lambda x: _dot3(x, x, NN), p)
        t = _hmap(lambda x, y: x + _dot3(x, y, NN), t, p)
    return t


def _uli_fwd(a):
    t = _unit_lower_inverse(a)
    return t, t


def _uli_bwd(t, dt):
    left = _hmap(lambda x, y: _dot3(x, y, TN), t, dt)
    return (_hmap(lambda x, y: -_dot3(x, y, NT), left, t),)


_unit_lower_inverse.defvjp(_uli_fwd, _uli_bwd)


@jax.custom_vjp
def _known_inverse(a, t):
    return t


_known_inverse.defvjp(lambda a, t: (t, t),
                      lambda t, dt: (_uli_bwd(t, dt)[0], _hmap(jnp.zeros_like, t)))


def _gdn_chunk(q, k, v, bl, al, a_log, dt_bias, state, t_known=None):
    c = q[0].shape[0]
    row = lax.broadcasted_iota(jnp.int32, (c, c), 0)
    col = lax.broadcasted_iota(jnp.int32, (c, c), 1)
    causal = row >= col
    strict = row > col
    eye = (row == col).astype(F32)
    beta = _hmap(jax.nn.sigmoid, bl)
    g = _hmap(lambda a_, l_, d_: -jnp.exp(a_) * jax.nn.softplus(l_ + d_), a_log, al, dt_bias)
    g_r = _hmap(lambda x: jnp.sum(eye * x, axis=0, keepdims=True), g)
    gc = _hmap(lambda x: jnp.sum(jnp.where(causal, x, 0.0), axis=1, keepdims=True), g_r)
    gc_r = _hmap(lambda x: jnp.sum(jnp.where(row <= col, x, 0.0), axis=0, keepdims=True), g)
    decay = _hmap(lambda x, y: jnp.where(causal, jnp.exp(jnp.where(causal, x - y, 0.0)), 0.0), gc, gc_r)
    e_gc = _hmap(jnp.exp, gc)
    kb = _hmap(jnp.multiply, k, beta)
    vb = _hmap(jnp.multiply, v, beta)
    a_mat = _hmap(lambda x, y, d: jnp.where(strict, _dot(x, y, NT) * d, 0.0), kb, k, decay)
    t_inv = _unit_lower_inverse(a_mat) if t_known is None else _known_inverse(a_mat, t_known)
    u_blk = _hmap(lambda t, x: _dot(t, x, NN), t_inv, vb)
    w_blk = _hmap(lambda t, x, e: _dot(t, x * e, NN), t_inv, kb, e_gc)
    v_new = _hmap(lambda u, w, s: u - _dot(w, s, NN), u_blk, w_blk, state)
    attn = _hmap(lambda x, y, d: _dot(x, y, NT) * d, q, k, decay)
    o_state = _hmap(lambda x, e, s: _dot(x * e, s, NN), q, e_gc, state)
    o = _hmap(lambda base, at, vn: base + _dot(at, vn, NN), o_state, attn, v_new)
    g_last = _hmap(lambda x: jnp.sum(x, axis=0, keepdims=True), g)
    k_dec = _hmap(lambda x, gl, c_: x * jnp.exp(gl - c_), k, g_last, gc)
    new_state = _hmap(lambda s, gl, kd, vn: s * jnp.exp(gl) + _dot(kd, vn, TN), state, g_last, k_dec, v_new)
    return (o, new_state, t_inv) if t_known is None else (o, new_state)


def _gdn_operands(q_ref, k_ref, v_ref, bav, alog_ref, dtb_ref):
    hs = range(GDN_HEADS)
    cols = [slice(h * HEAD_DIM, (h + 1) * HEAD_DIM) for h in hs]
    return ([q_ref[:, sl] for sl in cols], [k_ref[:, sl] for sl in cols], [v_ref[:, sl] for sl in cols],
            [bav[:, h:h + 1] for h in hs], [bav[:, h + GDN_HEADS:h + GDN_HEADS + 1] for h in hs],
            [alog_ref[h:h + 1, 0:1] for h in hs], [dtb_ref[h:h + 1, 0:1] for h in hs])


def _gdn_scan_fwd(q, k, v, ba, a_log, dt_bias, *, name):
    length = q.shape[0]
    n = length // GDN_CHUNK
    c = GDN_CHUNK

    def body(q_ref, k_ref, v_ref, ba_ref, alog_ref, dtb_ref, o_ref, s_ref, t_ref, state):
        i = pl.program_id(0)

        @pl.when(i == 0)
        def _():
            state[...] = jnp.zeros_like(state)

        bav = ba_ref[...]
        heads = [slice(h * HEAD_DIM, (h + 1) * HEAD_DIM) for h in range(GDN_HEADS)]
        s_in = [state[h] for h in range(GDN_HEADS)]
        o, s_out, t_inv = _gdn_chunk(*_gdn_operands(q_ref, k_ref, v_ref, bav, alog_ref, dtb_ref), s_in)
        for h, sl in enumerate(heads):
            s_ref[h] = s_in[h]
            t_ref[h] = t_inv[h]
            o_ref[:, sl] = o[h]
            state[h] = s_out[h]

    row_spec = pl.BlockSpec((c, D_MODEL), lambda i: (i, 0))
    small = pl.BlockSpec((GDN_HEADS, LANES), lambda i: (0, 0))
    return pl.pallas_call(
        body, name=name,
        out_shape=(jax.ShapeDtypeStruct((length, D_MODEL), F32),
                   jax.ShapeDtypeStruct((n, GDN_HEADS, HEAD_DIM, HEAD_DIM), F32),
                   jax.ShapeDtypeStruct((n, GDN_HEADS, c, c), F32)),
        grid=(n,),
        in_specs=[row_spec, row_spec, row_spec, pl.BlockSpec((c, LANES), lambda i: (i, 0)), small, small],
        out_specs=(row_spec, pl.BlockSpec((None, GDN_HEADS, HEAD_DIM, HEAD_DIM), lambda i: (i, 0, 0, 0)),
                   pl.BlockSpec((None, GDN_HEADS, c, c), lambda i: (i, 0, 0, 0))),
        scratch_shapes=[pltpu.VMEM((GDN_HEADS, HEAD_DIM, HEAD_DIM), F32)],
        compiler_params=_params(("arbitrary",)),
    )(q, k, v, ba, a_log, dt_bias)


def _gdn_scan_bwd(q, k, v, ba, a_log, dt_bias, states, inverses, do, *, name):
    length = q.shape[0]
    n = length // GDN_CHUNK
    c = GDN_CHUNK

    def body(q_ref, k_ref, v_ref, ba_ref, alog_ref, dtb_ref, s_ref, t_ref, do_ref,
             dq_ref, dk_ref, dv_ref, dba_ref, dalog_ref, ddtb_ref, dstate):
        i = pl.program_id(0)

        @pl.when(i == 0)
        def _():
            dstate[...] = jnp.zeros_like(dstate)
            dalog_ref[...] = jnp.zeros_like(dalog_ref)
            ddtb_ref[...] = jnp.zeros_like(ddtb_ref)

        bav = ba_ref[...]
        lane = lax.broadcasted_iota(jnp.int32, (c, LANES), 1)
        sub8 = lax.broadcasted_iota(jnp.int32, (GDN_HEADS, LANES), 0)
        lane8 = lax.broadcasted_iota(jnp.int32, (GDN_HEADS, LANES), 1)
        slab = jnp.zeros((c, LANES), F32)
        dalog_all = jnp.zeros((GDN_HEADS, LANES), F32)
        ddtb_all = jnp.zeros((GDN_HEADS, LANES), F32)
        heads = [slice(h * HEAD_DIM, (h + 1) * HEAD_DIM) for h in range(GDN_HEADS)]
        ds_in = [dstate[h] for h in range(GDN_HEADS)]
        s_in = [s_ref[h] for h in range(GDN_HEADS)]
        t_known = [t_ref[h] for h in range(GDN_HEADS)]
        _, vjp = jax.vjp(functools.partial(_gdn_chunk, t_known=t_known),
                         *_gdn_operands(q_ref, k_ref, v_ref, bav, alog_ref, dtb_ref), s_in)
        dq, dk, dv, dbl, dal, dalog, ddtb, ds = vjp(([do_ref[:, sl] for sl in heads], ds_in))
        for h, sl in enumerate(heads):
            dq_ref[:, sl] = dq[h]
            dk_ref[:, sl] = dk[h]
            dv_ref[:, sl] = dv[h]
            dstate[h] = ds[h]
            slab = slab + jnp.where(lane == h, dbl[h], 0.0) + jnp.where(lane == h + GDN_HEADS, dal[h], 0.0)
            here = (sub8 == h) & (lane8 == 0)
            dalog_all = dalog_all + jnp.where(here, dalog[h], 0.0)
            ddtb_all = ddtb_all + jnp.where(here, ddtb[h], 0.0)
        dba_ref[...] = slab
        dalog_ref[...] += dalog_all
        ddtb_ref[...] += ddtb_all

    row_spec = pl.BlockSpec((c, D_MODEL), lambda i: (n - 1 - i, 0))
    small = pl.BlockSpec((GDN_HEADS, LANES), lambda i: (0, 0))
    return pl.pallas_call(
        body, name=name,
        out_shape=(jax.ShapeDtypeStruct((length, D_MODEL), F32),) * 3
        + (jax.ShapeDtypeStruct((length, LANES), F32),
           jax.ShapeDtypeStruct((GDN_HEADS, LANES), F32), jax.ShapeDtypeStruct((GDN_HEADS, LANES), F32)),
        grid=(n,),
        in_specs=[row_spec, row_spec, row_spec,
                  pl.BlockSpec((c, LANES), lambda i: (n - 1 - i, 0)), small, small,
                  pl.BlockSpec((None, GDN_HEADS, HEAD_DIM, HEAD_DIM), lambda i: (n - 1 - i, 0, 0, 0)),
                  pl.BlockSpec((None, GDN_HEADS, c, c), lambda i: (n - 1 - i, 0, 0, 0)),
                  row_spec],
        out_specs=(row_spec, row_spec, row_spec,
                   pl.BlockSpec((c, LANES), lambda i: (n - 1 - i, 0)), small, small),
        scratch_shapes=[pltpu.VMEM((GDN_HEADS, HEAD_DIM, HEAD_DIM), F32)],
        compiler_params=_params(("arbitrary",)),
    )(q, k, v, ba, a_log, dt_bias, states, inverses, do)


S5_W = S5_T * LANES
S5_S = 2 * 8 * S5_STATE
S5_SH = S5_S // 2


def _iota2(shape):
    return lax.broadcasted_iota(jnp.int32, shape, 0), lax.broadcasted_iota(jnp.int32, shape, 1)


def _s5_rep_t(t, dtype):
    row, col = _iota2((S5_T * S5_GROUP, LANES))
    return ((jnp.right_shift(row, 4) == t) & (jnp.bitwise_and(row, 15) == jnp.bitwise_and(col, 15))).astype(dtype)


def _s5_rep_state(dtype):
    row, col = _iota2((2 * S5_STATE, S5_S))
    return ((jnp.right_shift(row, 6) == jnp.right_shift(col, 9))
            & (jnp.bitwise_and(row, 63) == jnp.bitwise_and(col, 63))).astype(dtype)


def _s5_masks():
    row, col = _iota2((LANES, LANES))
    m_ab = jnp.right_shift(row, 4) == jnp.right_shift(col, 4)
    row, col = _iota2((S5_S, LANES))
    m_e = jnp.bitwise_and(jnp.right_shift(row, 6), 7) == jnp.right_shift(col, 4)
    row, col = _iota2((LANES, S5_S))
    m_f = jnp.right_shift(row, 4) == jnp.bitwise_and(jnp.right_shift(col, 6), 7)
    return m_ab, m_e, m_f


def _s5_expand(kx_ref, ec_ref, fc_ref, kb_scr, e_scr, f_scr):
    m_ab, m_e, m_f = _s5_masks()
    kx = kx_ref[...].astype(BF16)
    ec = ec_ref[...].astype(BF16)
    rep_state = _s5_rep_state(BF16)
    for t in range(S5_T):
        rep = _s5_rep_t(t, BF16)
        cols = slice(t * LANES, (t + 1) * LANES)
        kb_scr[t] = jnp.where(m_ab, jnp.dot(kx, rep, preferred_element_type=F32), 0.0).astype(BF16)
        e_scr[:, cols] = jnp.where(m_e, jnp.dot(ec, rep, preferred_element_type=F32), 0.0).astype(BF16)
        f_scr[cols, :] = jnp.where(m_f, jnp.dot(fc_ref[t].astype(BF16), rep_state, preferred_element_type=F32),
                                   0.0).astype(BF16)


def _s5_token_rows(ref, n):
    return [ref[pl.ds(t, n, stride=S5_T), :].astype(BF16) for t in range(S5_T)]


def _s5_scan_fwd(u, kx, ec, fc, at, *, name):
    length = u.shape[0]
    n = length // S5_T
    assert n % SUBLANES == 0

    def body(u_ref, kx_ref, ec_ref, fc_ref, at_ref, y_ref, h_ref, kb_scr, e_scr, f_scr, g_scr):
        _s5_expand(kx_ref, ec_ref, fc_ref, kb_scr, e_scr, f_scr)
        us = _s5_token_rows(u_ref, n)
        g_scr[...] = jnp.dot(jnp.concatenate(us, axis=1), f_scr[...], preferred_element_type=F32)
        ar, ai = at_ref[:, :S5_SH], at_ref[:, S5_SH:]

        def step(blk, h):
            base = pl.multiple_of(blk * SUBLANES, SUBLANES)
            g8 = g_scr[pl.ds(base, SUBLANES), :]
            rows = []
            for r in range(SUBLANES):
                rows.append(h)
                hr, hi = h[:, :S5_SH], h[:, S5_SH:]
                h = jnp.concatenate([ar * hr - ai * hi, ar * hi + ai * hr], axis=1) + g8[r:r + 1, :]
            h_ref[pl.ds(base, SUBLANES), :] = jnp.concatenate(rows, axis=0)
            return h

        lax.fori_loop(0, n // SUBLANES, step, jnp.zeros((1, S5_S), F32))
        hb = h_ref[...].astype(BF16)
        for t in range(S5_T):
            acc = jnp.dot(hb, e_scr[:, t * LANES:(t + 1) * LANES], preferred_element_type=F32)
            for s in range(t + 1):
                acc = acc + jnp.dot(us[s], kb_scr[t - s], preferred_element_type=F32)
            y_ref[pl.ds(t, n, stride=S5_T), :] = acc

    return pl.pallas_call(
        body, name=name,
        out_shape=(jax.ShapeDtypeStruct((length, D_MODEL), F32), jax.ShapeDtypeStruct((S5_TILES, n, S5_S), F32)),
        grid=(S5_TILES,),
        in_specs=[pl.BlockSpec((length, LANES), lambda k: (0, k)), _s5_spec(LANES, S5_T * S5_GROUP),
                  _s5_spec(S5_S, S5_T * S5_GROUP), _s5_spec(S5_T, LANES, LANES), _s5_spec(1, S5_S)],
        out_specs=(pl.BlockSpec((length, LANES), lambda k: (0, k)), _s5_spec(n, S5_S)),
        scratch_shapes=[pltpu.VMEM((S5_T, LANES, LANES), BF16), pltpu.VMEM((S5_S, S5_W), BF16),
                        pltpu.VMEM((S5_W, S5_S), BF16), pltpu.VMEM((n, S5_S), F32)],
        compiler_params=_params(("parallel",)),
    )(u, kx, ec, fc, at)


def _s5_spec(*tail):
    return pl.BlockSpec((None,) + tail, lambda k: (k,) + (0,) * len(tail))


def _s5_scan_bwd(dy, kx, ec, fc, at, hs, *, name):
    length = dy.shape[0]
    n = length // S5_T

    def body(dy_ref, kx_ref, ec_ref, fc_ref, at_ref, h_ref, du_ref, dg_ref, dat_ref, kb_scr, e_scr, f_scr, dh_scr):
        _s5_expand(kx_ref, ec_ref, fc_ref, kb_scr, e_scr, f_scr)
        dys = _s5_token_rows(dy_ref, n)
        dh_scr[...] = _dot(jnp.concatenate(dys, axis=1), e_scr[...], NT)
        ar, ai = at_ref[:, :S5_SH], at_ref[:, S5_SH:]

        def step(it, carry):
            cy, dat = carry
            base = pl.multiple_of((n // SUBLANES - 1 - it) * SUBLANES, SUBLANES)
            dh8 = dh_scr[pl.ds(base, SUBLANES), :]
            h8 = h_ref[pl.ds(base, SUBLANES), :]
            rows = [None] * SUBLANES
            for r in reversed(range(SUBLANES)):
                rows[r] = cy
                cr, ci = cy[:, :S5_SH], cy[:, S5_SH:]
                hr, hi = h8[r:r + 1, :S5_SH], h8[r:r + 1, S5_SH:]
                dat = dat + jnp.concatenate([cr * hr + ci * hi, ci * hr - cr * hi], axis=1)
                cy = dh8[r:r + 1, :] + jnp.concatenate([ar * cr + ai * ci, ar * ci - ai * cr], axis=1)
            dg_ref[pl.ds(base, SUBLANES), :] = jnp.concatenate(rows, axis=0)
            return cy, dat

        zero = jnp.zeros((1, S5_S), F32)
        _, dat = lax.fori_loop(0, n // SUBLANES, step, (zero, zero))
        dat_ref[...] = dat
        dgb = dg_ref[...].astype(BF16)
        for s in range(S5_T):
            acc = _dot(dgb, f_scr[s * LANES:(s + 1) * LANES, :], NT)
            for t in range(s, S5_T):
                acc = acc + _dot(dys[t], kb_scr[t - s], NT)
            du_ref[pl.ds(s, n, stride=S5_T), :] = acc

    row_spec = pl.BlockSpec((length, LANES), lambda k: (0, k))
    return pl.pallas_call(
        body, name=name,
        out_shape=(jax.ShapeDtypeStruct((length, D_MODEL), F32), jax.ShapeDtypeStruct((S5_TILES, n, S5_S), F32),
                   jax.ShapeDtypeStruct((S5_TILES, 1, S5_S), F32)),
        grid=(S5_TILES,),
        in_specs=[row_spec, _s5_spec(LANES, S5_T * S5_GROUP), _s5_spec(S5_S, S5_T * S5_GROUP),
                  _s5_spec(S5_T, LANES, LANES), _s5_spec(1, S5_S), _s5_spec(n, S5_S)],
        out_specs=(row_spec, _s5_spec(n, S5_S), _s5_spec(1, S5_S)),
        scratch_shapes=[pltpu.VMEM((S5_T, LANES, LANES), BF16), pltpu.VMEM((S5_S, S5_W), BF16),
                        pltpu.VMEM((S5_W, S5_S), BF16), pltpu.VMEM((n, S5_S), F32)],
        compiler_params=_params(("parallel",)),
    )(dy, kx, ec, fc, at, hs)


def _s5_operator_grads(dy, u, hs, dg, *, name):
    length = u.shape[0]
    n = length // S5_T

    def body(dy_ref, u_ref, h_ref, dg_ref, dkx_ref, dec_ref, dfc_ref):
        dys = _s5_token_rows(dy_ref, n)
        us = _s5_token_rows(u_ref, n)
        ucat = jnp.concatenate(us, axis=1)
        m_ab, m_e, m_f = _s5_masks()
        hb = h_ref[...].astype(BF16)
        dgb = dg_ref[...].astype(BF16)
        lane = lax.broadcasted_iota(jnp.int32, (1, LANES), 1)
        lane_group = jnp.right_shift(lane, 4)

        def own_block(x, mask):
            x = jnp.where(mask, x, 0.0)
            for shift in (64, 32, 16):
                x = x + pltpu.roll(x, shift, 1)
            return x

        def place(halves, t, x):
            halves[t // 8] = jnp.where(lane_group == t % 8, x, halves[t // 8])

        dkb = [jnp.zeros((LANES, LANES), F32) for _ in range(S5_T)]
        dec = [jnp.zeros((S5_S, LANES), F32) for _ in range(2)]
        for t in range(S5_T):
            d_t = _dot(ucat, dys[t], TN)
            for s in range(t + 1):
                dkb[t - s] = dkb[t - s] + d_t[s * LANES:(s + 1) * LANES, :]
            place(dec, t, own_block(_dot(hb, dys[t], TN), m_e))
            wide = jnp.where(m_f, _dot(us[t], dgb, TN), 0.0)
            parts = []
            for r in range(2):
                acc = wide[:, r * S5_SH:r * S5_SH + LANES]
                for q in range(1, S5_SH // LANES):
                    acc = acc + wide[:, r * S5_SH + q * LANES:r * S5_SH + (q + 1) * LANES]
                parts.append(acc + pltpu.roll(acc, S5_STATE, 1))
            dfc_ref[t] = jnp.where(lane < S5_STATE, parts[0], parts[1])
        dkx = [jnp.zeros((LANES, LANES), F32) for _ in range(2)]
        for t in range(S5_T):
            place(dkx, t, own_block(dkb[t], m_ab))
        dkx_ref[...] = jnp.concatenate(dkx, axis=1)
        dec_ref[...] = jnp.concatenate(dec, axis=1)

    row_spec = pl.BlockSpec((length, LANES), lambda k: (0, k))
    outs = (_s5_spec(LANES, S5_T * S5_GROUP), _s5_spec(S5_S, S5_T * S5_GROUP), _s5_spec(S5_T, LANES, LANES))
    return pl.pallas_call(
        body, name=name,
        out_shape=(jax.ShapeDtypeStruct((S5_TILES, LANES, S5_T * S5_GROUP), F32),
                   jax.ShapeDtypeStruct((S5_TILES, S5_S, S5_T * S5_GROUP), F32),
                   jax.ShapeDtypeStruct((S5_TILES, S5_T, LANES, LANES), F32)),
        grid=(S5_TILES,),
        in_specs=[row_spec, row_spec, _s5_spec(n, S5_S), _s5_spec(n, S5_S)],
        out_specs=outs,
        compiler_params=_params(("parallel",)),
    )(dy, u, hs, dg)


def _s5_prep(a_re, a_im, b_re, b_im, c_re, c_im, log_dt):
    t_len, tiles = S5_T, S5_TILES
    dt = jnp.exp(log_dt)[:, None]
    mag = jnp.exp(a_re * dt)
    ab_re, ab_im = mag * jnp.cos(a_im * dt), mag * jnp.sin(a_im * dt)
    den = jnp.square(a_re) + jnp.square(a_im)
    n_re, n_im = ab_re - 1.0, ab_im
    f_re = (n_re * a_re + n_im * a_im) / den
    f_im = (n_im * a_re - n_re * a_im) / den
    bb_re = f_re[..., None] * b_re - f_im[..., None] * b_im
    bb_im = f_re[..., None] * b_im + f_im[..., None] * b_re

    def powers(exponents):
        e = exponents[:, None, None]
        m = jnp.exp(e * (a_re * dt))
        return m * jnp.cos(e * (a_im * dt)), m * jnp.sin(e * (a_im * dt))

    p_re, p_im = powers(jnp.arange(t_len + 1, dtype=F32))
    rev_re, rev_im = powers((t_len - 1) - jnp.arange(t_len, dtype=F32))
    ca_re = c_re[None] * p_re[:, :, None, :] - c_im[None] * p_im[:, :, None, :]
    ca_im = c_re[None] * p_im[:, :, None, :] + c_im[None] * p_re[:, :, None, :]
    lag = (jnp.einsum('tgip,gpj->tgij', ca_re[:t_len], bb_re, precision=HI)
           - jnp.einsum('tgip,gpj->tgij', ca_im[:t_len], bb_im, precision=HI))
    kx = lag.reshape(t_len, tiles, 8, S5_GROUP, S5_GROUP).transpose(1, 2, 4, 0, 3)
    kx = kx.reshape(tiles, LANES, t_len * S5_GROUP)
    e_st = jnp.stack([ca_re[1:], -ca_im[1:]])
    e_st = e_st.reshape(2, t_len, tiles, 8, S5_GROUP, S5_STATE).transpose(2, 0, 3, 5, 1, 4)
    ec = e_st.reshape(tiles, S5_S, t_len * S5_GROUP)
    ab_b = jnp.stack([rev_re[..., None] * bb_re[None] - rev_im[..., None] * bb_im[None],
                      rev_re[..., None] * bb_im[None] + rev_im[..., None] * bb_re[None]])
    ab_b = ab_b.reshape(2, t_len, tiles, 8, S5_STATE, S5_GROUP).transpose(2, 1, 3, 5, 0, 4)
    fc = ab_b.reshape(tiles, t_len, LANES, 2 * S5_STATE)
    a_t = jnp.stack([p_re[t_len], p_im[t_len]]).reshape(2, tiles, 8 * S5_STATE).transpose(1, 0, 2)
    return kx, ec, fc, a_t.reshape(tiles, 1, S5_S)


TM_ROW = 1024
TM_WIDE = 512


def _gdn_fwd(x, w):
    qkv = _mm(x, w["wqkv"], name="gdn_proj_qkv")
    z = _mm(x, w["wz"], name="gdn_proj_z")
    ba = _mm(x, w["wba"], name="gdn_proj_ba")
    cv = _conv_fwd(qkv, w["conv_w"], tm=TM_ROW, name="gdn_conv")
    q, k, v = _rw_fwd(_f_gdn_qkv, [cv], [], tm=TM_WIDE, name="gdn_qkv")
    o, states, inverses = _gdn_scan_fwd(q, k, v, ba, w["a_log8"], w["dt_bias8"], name="gdn_scan")
    (mix,) = _rw_fwd(_f_gdn_out, [o, z], [w["norm_g"]], tm=TM_ROW, name="gdn_out", out_dtypes=[BF16])
    return mix, (qkv, z, ba, cv, q, k, v, states, inverses, o)


def _gdn_bwd(x, w, saved, dmix, dx_acc, token=None):
    qkv, z, ba, cv, q, k, v, states, inverses, o = saved
    norm_g = w["norm_g"] if token is None else w["norm_g"] + token[0, 0]
    (do, dz), (dnorm_g,) = _rw_bwd(_f_gdn_out, [o, z], [norm_g], [dmix], row_grad=[1, 1], param_grad=[1],
                                   tm=TM_ROW, name="gdn_out_bwd", row_dtypes=[F32, BF16])
    dq, dk, dv, dba, dalog, ddtb = _gdn_scan_bwd(q, k, v, ba, w["a_log8"], w["dt_bias8"], states, inverses, do,
                                                  name="gdn_scan_bwd")
    (dcv,), _ = _rw_bwd(_f_gdn_qkv, [cv], [], [dq, dk, dv], row_grad=[1], param_grad=[], tm=TM_WIDE,
                        name="gdn_qkv_bwd")
    dqkv, dconv_w = _conv_bwd(qkv, w["conv_w"], dcv, tm=TM_ROW, name="gdn_conv_bwd")
    dx = _mm(dqkv, w["wqkv"], tb=True, acc=dx_acc, name="gdn_dx_qkv")
    dx = _mm(dz, w["wz"], tb=True, acc=dx, name="gdn_dx_z")
    dx = _mm(dba, w["wba"], tb=True, acc=dx, name="gdn_dx_ba")
    grads = dict(wqkv=_mm(x, dqkv, ta=True, name="gdn_dw_qkv"), wz=_mm(x, dz, ta=True, name="gdn_dw_z"),
                 wba=_mm(x, dba, ta=True, name="gdn_dw_ba"), conv_w=dconv_w,
                 a_log=dalog[:, 0], dt_bias=ddtb[:, 0], norm_g=dnorm_g[0])
    return dx, grads


def _s5_fwd(x, w):
    u = _mm(x, w["wu"], name="s5_proj_u")
    y, hs = _s5_scan_fwd(u, w["kx"], w["ec"], w["fc"], w["a_t"], name="s5_scan")
    (zg,) = _rw_fwd(_f_s5_gelu, [y, u], [w["d"]], tm=TM_ROW, name="s5_gelu")
    t = _mm(zg, w["w_glu"], name="s5_glu")
    (mix,) = _rw_fwd(_f_s5_gate, [zg, t], [w["b_glu"]], tm=TM_ROW, name="s5_gate", out_dtypes=[BF16])
    return mix, (u, hs, y, zg, t)


def _s5_bwd(x, w, saved, dmix, dx_acc, token=None):
    u, hs, y, zg, t = saved
    b_glu = w["b_glu"] if token is None else w["b_glu"] + token[0, 0]
    (dzg, dt), (db_glu,) = _rw_bwd(_f_s5_gate, [zg, t], [b_glu], [dmix], row_grad=[1, 1], param_grad=[1],
                                   tm=TM_ROW, name="s5_gate_bwd", row_dtypes=[F32, BF16])
    dzg = _mm(dt, w["w_glu"], tb=True, acc=dzg, name="s5_dzg")
    dw_glu = _mm(zg, dt, ta=True, name="s5_dw_glu")
    (dy, du), (dd,) = _rw_bwd(_f_s5_gelu, [y, u], [w["d"]], [dzg], row_grad=[1, 1], param_grad=[1],
                              tm=TM_ROW, name="s5_gelu_bwd")
    du_scan, dg, dat = _s5_scan_bwd(dy, w["kx"], w["ec"], w["fc"], w["a_t"], hs, name="s5_scan_bwd")
    dkx, dec, dfc = _s5_operator_grads(dy, u, hs, dg, name="s5_operator_grads")
    (du,) = _rw_fwd(_f_add, [du, du_scan], [], tm=TM_ROW, name="s5_du_add", out_dtypes=[BF16])
    d_a_re, d_a_im, d_b_re, d_b_im, d_c_re, d_c_im, d_log_dt = w["prep_vjp"]((dkx, dec, dfc, dat))
    dx = _mm(du, w["wu"], tb=True, acc=dx_acc, name="s5_dx_u")
    grads = dict(wu=_mm(x, du, ta=True, name="s5_dw_u"), w_glu=dw_glu, b_glu=db_glu[0], d=dd[0],
                 a_re=d_a_re, a_im=d_a_im, b_re=d_b_re, b_im=d_b_im, c_re=d_c_re, c_im=d_c_im, log_dt=d_log_dt)
    return dx, grads


def _ln_res_both(x, h, g, b):
    (y,) = _f_ln_res(x, h, g, b)
    return y, y


def _layer_fwd(x, xb, mem, w, is_gdn):
    mix, msave = (_gdn_fwd if is_gdn else _s5_fwd)(xb, w)
    xq = _mm(xb, w["wxq"], name="proj_xq")
    kv = _mm(mem, w["wkv"], name="mem_kv")
    kmem, vmem = kv[:, :XA_DIM], kv[:, XA_DIM:]
    (cross,) = _rw_fwd(_f_attn, [xq], [kmem, vmem], tm=TM_ROW, name="attn", out_dtypes=[BF16])
    h = _mm(mix, w["wo_mix"], name="wo_mix")
    h = _mm(cross, w["wo_cross"], acc=h, name="wo_cross")
    x1, x1b = _rw_fwd(_ln_res_both, [x, h], [w["ln1_g"], w["ln1_b"]], tm=TM_ROW, name="ln_res",
                      out_dtypes=[F32, BF16])
    hm, act = _mm_relu2(x1b, w["w1"], name="mlp_up")
    f = _mm(act, w["w2"], name="mlp_down")
    x2, x2b = _rw_fwd(_ln_res_both, [x1, f], [w["ln2_g"], w["ln2_b"]], tm=TM_ROW, name="ln_res",
                      out_dtypes=[F32, BF16])
    return x2, x2b, (x, xb, msave, xq, kmem, vmem, mix, cross, h, x1, x1b, hm, act, f)


def _layer_bwd(mem, w, is_gdn, saved, dx2, token=None, before_mixer=None):
    x, xb, msave, xq, kmem, vmem, mix, cross, h, x1, x1b, hm, act, f = saved
    ln2_g = w["ln2_g"] if token is None else w["ln2_g"] + token[0, 0]
    (dx1, df), (dg2, db2) = _rw_bwd(_f_ln_res, [x1, f], [ln2_g, w["ln2_b"]], [dx2], row_grad=[1, 1],
                                    param_grad=[1, 1], tm=TM_ROW, name="ln_res_bwd", row_dtypes=[F32, BF16])
    dhm = _mm_relu2_grad(df, w["w2"], hm, name="mlp_dhm")
    dw2 = _mm(act, df, ta=True, name="mlp_dw2")
    dx1 = _mm(dhm, w["w1"], tb=True, acc=dx1, name="mlp_dx")
    dw1 = _mm(x1b, dhm, ta=True, out_blocks=N_CHIPS, name="mlp_dw1")
    (dx, dh), (dg1, db1) = _rw_bwd(_f_ln_res, [x, h], [w["ln1_g"], w["ln1_b"]], [dx1], row_grad=[1, 1],
                                   param_grad=[1, 1], tm=TM_ROW, name="ln_res_bwd", row_dtypes=[F32, BF16])
    dmix =_mm(dh, w["wo_mix"], tb=True, name="wo_dmix")
    dcross = _mm(dh, w["wo_cross"], tb=True, name="wo_dcross")
    dwo = jnp.concatenate([_mm(mix, dh, ta=True, name="wo_dw_mix"), _mm(cross, dh, ta=True, name="wo_dw_cross")], 0)
    (dxq,), (dkmem, dvmem) = _rw_bwd(_f_attn, [xq], [kmem, vmem], [dcross], row_grad=[1], param_grad=[1, 1],
                                     tm=TM_ROW, name="attn_bwd", row_dtypes=[BF16])
    dwkv = _mm(mem, jnp.concatenate([dkmem, dvmem], axis=1), ta=True, name="mem_dw_kv")
    dx = _mm(dxq, w["wxq"], tb=True, acc=dx, name="dx_xq")
    dwxq = _mm(xb, dxq, ta=True, name="dw_xq")
    mixer_token = None if before_mixer is None else before_mixer(dict(w_kv_mem=dwkv, w_o=dwo, mlp_w1=dw1, mlp_w2=dw2))
    dx, mg = (_gdn_bwd if is_gdn else _s5_bwd)(xb, w, msave, dmix, dx, mixer_token)
    grads = dict(mixer=mg, wxq=dwxq, wkv=dwkv, wo=dwo, w1=dw1, w2=dw2,
                 ln1_g=dg1[0], ln1_b=db1[0], ln2_g=dg2[0], ln2_b=db2[0])
    return dx, grads


def _loss_and_grad(y, target):
    def f(yv, tv):
        err = yv - tv
        return (err * (1.0 / D_MODEL),), (0.5 / D_MODEL * jnp.sum(err * err, axis=0, keepdims=True),)

    (dy,), (part,) = _rowwise(f, [y, target], [], [(D_MODEL, F32)], [((1, D_MODEL), F32)], tm=TM_ROW, name="loss")
    return jnp.sum(part), dy


def _layer_weights(full, i):
    j = i // 2
    w = dict(wkv=full["w_kv_mem"][i].astype(BF16),
             wo_mix=full["w_o"][i][:D_MODEL].astype(BF16), wo_cross=full["w_o"][i][D_MODEL:].astype(BF16),
             ln1_g=full["ln1_g"][i][None], ln1_b=full["ln1_b"][i][None],
             ln2_g=full["ln2_g"][i][None], ln2_b=full["ln2_b"][i][None],
             w1=full["mlp_w1"][i].astype(BF16), w2=full["mlp_w2"][i].astype(BF16))
    if i % 2 == 0:
        w_in = full["gdn_w_in"][j]
        gd = 3 * D_MODEL
        w.update(wqkv=w_in[:, :gd].astype(BF16), wz=w_in[:, gd:gd + D_MODEL].astype(BF16),
                 wba=jnp.pad(w_in[:, gd + D_MODEL:gd + D_MODEL + 2 * GDN_HEADS],
                             ((0, 0), (0, LANES - 2 * GDN_HEADS))).astype(BF16),
                 wxq=w_in[:, gd + D_MODEL + 2 * GDN_HEADS:].astype(BF16),
                 conv_w=full["gdn_conv_w"][j],
                 a_log8=jnp.broadcast_to(full["gdn_a_log"][j][:, None], (GDN_HEADS, LANES)),
                 dt_bias8=jnp.broadcast_to(full["gdn_dt_bias"][j][:, None], (GDN_HEADS, LANES)),
                 norm_g=full["gdn_norm_g"][j][None])
    else:
        w_in = full["s5_w_in"][j]
        (kx, ec, fc, a_t), prep_vjp = full["s5_prep"][j] if "s5_prep" in full else jax.vjp(
            _s5_prep, full["s5_a_re"][j], full["s5_a_im"][j], full["s5_b_re"][j], full["s5_b_im"][j],
            full["s5_c_re"][j], full["s5_c_im"][j], full["s5_log_dt"][j])
        w.update(wu=w_in[:, :D_MODEL].astype(BF16), wxq=w_in[:, D_MODEL:].astype(BF16),
                 kx=kx, ec=ec, fc=fc, a_t=a_t, prep_vjp=prep_vjp,
                 d=full["s5_d"][j][None], w_glu=full["s5_w_glu"][j].astype(BF16), b_glu=full["s5_b_glu"][j][None])
    return w


def _sharded_grads(l, i):
    m = l["mixer"]
    out = dict(w_kv_mem=l["wkv"], w_o=l["wo"], mlp_w1=l["w1"], mlp_w2=l["w2"])
    if i % 2 == 0:
        out.update(gdn_w_in=jnp.concatenate([m["wqkv"], m["wz"], m["wba"][:, :2 * GDN_HEADS], l["wxq"]], axis=1),
                   gdn_conv_w=m["conv_w"])
    else:
        out.update(s5_w_in=jnp.concatenate([m["wu"], l["wxq"]], axis=1), s5_d=m["d"], s5_w_glu=m["w_glu"],
                   s5_b_glu=m["b_glu"])
    return out


def _replicated_grads(layer_grads):
    g = layer_grads
    gdn = [g[i]["mixer"] for i in range(DEPTH) if i % 2 == 0]
    s5 = [g[i]["mixer"] for i in range(DEPTH) if i % 2 == 1]
    out = {n: jnp.stack([l[n] for l in g]) for n in ("ln1_g", "ln1_b", "ln2_g", "ln2_b")}
    out.update({"gdn_" + n: jnp.stack([m[n] for m in gdn]) for n in ("a_log", "dt_bias", "norm_g")})
    out.update({"s5_" + n: jnp.stack([m[n] for m in s5])
                for n in ("a_re", "a_im", "b_re", "b_im", "c_re", "c_im", "log_dt")})
    return out


def _local_step(x, mem, target, weights_of, grads_ready, before_first_mixer):
    lw, saves = [], []
    h, hb = x, x.astype(BF16)
    for i in range(DEPTH):
        lw.append(weights_of(i, h))
        h, hb, s = _layer_fwd(h, hb, mem, lw[i], i % 2 == 0)
        saves.append(s)
    loss, d = _loss_and_grad(h, target)
    grads = [None] * DEPTH
    token = None
    for i in reversed(range(DEPTH)):
        d, grads[i] = _layer_bwd(mem, lw[i], i % 2 == 0, saves[i], d, token, None if i else before_first_mixer)
        token = grads_ready(i, grads[i])
    return loss, d, grads


ANY = pl.BlockSpec(memory_space=pl.ANY)
SHARD_ROWS = 1024
SMALL_ROWS = 128


def _place():
    return lax.axis_index("x"), lax.axis_index("y"), lax.axis_index("c")


def _other_chips(x, y):
    return [(1 - x, y), (x, 1 - y), (1 - x, 1 - y)]


def _all_gather_chips(wpack, *, name):
    rows = wpack.shape[0]
    half = rows // 2

    def body(w_ref, out_ref, send_sems, recv_sems):
        x, y, c = _place()
        sibling = (x, y, 1 - c)
        chips = _other_chips(x, y)

        def blk(cx, cy, cc):
            return out_ref.at[2 * cx + cy, pl.ds(cc * half, half), :]

        def copy(k, src, dst, to):
            return pltpu.make_async_remote_copy(src_ref=src, dst_ref=dst, send_sem=send_sems.at[k],
                                                recv_sem=recv_sems.at[k], device_id=to, device_id_type=MESH)

        first = [copy(j, w_ref.at[pl.ds(c * half, half), :], blk(x, y, c), (cx, cy, c))
                 for j, (cx, cy) in enumerate(chips)]
        for cp in first:
            cp.start()
        passed = [copy(3 + j, blk(cx, cy, c), blk(cx, cy, c), sibling) for j, (cx, cy) in enumerate(chips)]
        for j, (cx, cy) in enumerate(chips):
            copy(j, blk(cx, cy, c), blk(cx, cy, c), (cx, cy, c)).wait_recv()
            passed[j].start()
        for j, (cx, cy) in enumerate(chips):
            copy(3 + j, blk(cx, cy, 1 - c), blk(cx, cy, 1 - c), sibling).wait_recv()
        for cp in first + passed:
            cp.wait_send()

    return pl.pallas_call(
        body, name=name, out_shape=jax.ShapeDtypeStruct((N_CHIPS, rows, D_MODEL), wpack.dtype),
        in_specs=[ANY], out_specs=ANY,
        scratch_shapes=[pltpu.SemaphoreType.DMA((6,)), pltpu.SemaphoreType.DMA((6,))],
    )(wpack)


HBM = pl.BlockSpec(memory_space=pltpu.HBM)
SEM = pl.BlockSpec(memory_space=pltpu.SEMAPHORE)
DATAFLOW = pltpu.SideEffectType.DATAFLOW_SIDE_EFFECTING


def _gather_ici_copies(w_ref, land_ref, send_sems, recv_sems, outgoing):
    x, y, c = _place()
    half = w_ref.shape[0] // 2
    mine = pl.ds(c * half, half)
    return [pltpu.make_async_remote_copy(
        src_ref=w_ref.at[mine, :], dst_ref=land_ref.at[2 * x + y if outgoing else 2 * cx + cy, mine, :],
        send_sem=send_sems.at[j], recv_sem=recv_sems.at[j], device_id=(cx, cy, c), device_id_type=MESH)
        for j, (cx, cy) in enumerate(_other_chips(x, y))]


def _gather_start(wpack, after, *, name):
    rows = wpack.shape[0]

    def body(w_ref, land_ref, after_ref, send_sems, recv_sems, w_thru, land_thru, token):
        for cp in _gather_ici_copies(w_ref, land_ref, send_sems, recv_sems, outgoing=True):
            cp.start()
        token[...] = jnp.zeros_like(token)

    land = pltpu.with_memory_space_constraint(lax.empty((N_CHIPS, rows, D_MODEL), wpack.dtype), pltpu.HBM)
    return pl.pallas_call(
        body, name=name,
        out_shape=(pltpu.SemaphoreType.DMA((3,)), pltpu.SemaphoreType.DMA((3,)), pltpu.HBM(wpack.shape, wpack.dtype),
                   pltpu.HBM(land.shape, land.dtype), jax.ShapeDtypeStruct((SUBLANES, LANES), F32)),
        in_specs=(HBM, HBM, ANY), out_specs=(SEM, SEM, HBM, HBM, pl.BlockSpec(memory_space=pltpu.VMEM)),
        input_output_aliases={0: 2, 1: 3},
        compiler_params=pltpu.CompilerParams(has_side_effects=DATAFLOW),
    )(pltpu.with_memory_space_constraint(wpack, pltpu.HBM), land, after)


def _gather_wait(send_sems, recv_sems, w_thru, land_thru, after, *, name):
    def body(w_ref, land_ref, send_sems, recv_sems, after_ref, w_dead, land_out):
        for cp in _gather_ici_copies(w_ref, land_ref, send_sems, recv_sems, outgoing=False):
            cp.wait_send()
            cp.wait_recv()

    return pl.pallas_call(
        body, name=name,
        out_shape=(pltpu.HBM(w_thru.shape, w_thru.dtype), pltpu.HBM(land_thru.shape, land_thru.dtype)),
        in_specs=(HBM, HBM, SEM, SEM, ANY), out_specs=(HBM, HBM), input_output_aliases={0: 0, 1: 1},
        compiler_params=pltpu.CompilerParams(has_side_effects=DATAFLOW),
    )(w_thru, land_thru, send_sems, recv_sems, after)[1]


def _gather_forward(land, *, name):
    rows = land.shape[1]
    half = rows // 2

    def body(in_ref, out_ref, send_sems, recv_sems):
        x, y, c = _place()

        def copy(j, cx, cy, cc):
            rows_of = out_ref.at[2 * cx + cy, pl.ds(cc * half, half), :]
            return pltpu.make_async_remote_copy(src_ref=rows_of, dst_ref=rows_of, send_sem=send_sems.at[j],
                                                recv_sem=recv_sems.at[j], device_id=(x, y, 1 - c), device_id_type=MESH)

        sends = [copy(j, cx, cy, c) for j, (cx, cy) in enumerate(_other_chips(x, y))]
        for cp in sends:
            cp.start()
        for j, (cx, cy) in enumerate(_other_chips(x, y)):
            copy(j, cx, cy, 1 - c).wait_recv()
        for cp in sends:
            cp.wait_send()

    return pl.pallas_call(
        body, name=name, out_shape=jax.ShapeDtypeStruct(land.shape, land.dtype), in_specs=[ANY], out_specs=ANY,
        input_output_aliases={0: 0},
        scratch_shapes=[pltpu.SemaphoreType.DMA((3,)), pltpu.SemaphoreType.DMA((3,))],
    )(land)


def _sibling_swap(buf, *, name):
    def body(in_ref, out_ref, send_sem, recv_sem):
        x, y, c = _place()
        cp = pltpu.make_async_remote_copy(src_ref=in_ref, dst_ref=out_ref, send_sem=send_sem, recv_sem=recv_sem,
                                          device_id=(x, y, 1 - c), device_id_type=MESH)
        cp.start()
        cp.wait()

    return pl.pallas_call(
        body, name=name, out_shape=jax.ShapeDtypeStruct(buf.shape, buf.dtype), in_specs=[ANY], out_specs=ANY,
        scratch_shapes=[pltpu.SemaphoreType.DMA, pltpu.SemaphoreType.DMA],
    )(buf)


def _pair_exchange(gpack, *, name):
    pieces, rows, width = gpack.shape
    half = rows // 2

    def body(in_ref, got_ref, send_sems, recv_sems):
        x, y, c = _place()
        sends = [pltpu.make_async_remote_copy(src_ref=in_ref.at[p, pl.ds((1 - c) * half, half), :],
                                              dst_ref=got_ref.at[p], send_sem=send_sems.at[p],
                                              recv_sem=recv_sems.at[p], device_id=(x, y, 1 - c), device_id_type=MESH)
                 for p in range(pieces)]
        for cp in sends:
            cp.start()
        for cp in sends:
            cp.wait()

    return pl.pallas_call(
        body, name=name, out_shape=jax.ShapeDtypeStruct((pieces, half, width), gpack.dtype),
        in_specs=[ANY], out_specs=ANY,
        scratch_shapes=[pltpu.SemaphoreType.DMA((pieces,)), pltpu.SemaphoreType.DMA((pieces,))],
    )(gpack)


def _pair_add(gpack, got, c, *, name, tm=512):
    pieces, rows, width = gpack.shape
    half = rows // 2
    nb = half // tm

    def body(c_ref, a_ref, b_ref, sum_ref, narrow_ref):
        s = a_ref[...] + b_ref[...]
        sum_ref[...] = s
        narrow_ref[...] = s.astype(BF16)

    blk = pl.BlockSpec((None, tm, width), lambda p, i, c_ref: (p, i, 0))
    return pl.pallas_call(
        body, name=name,
        out_shape=(jax.ShapeDtypeStruct((pieces, half, width), F32), jax.ShapeDtypeStruct((pieces, half, width), BF16)),
        grid_spec=pltpu.PrefetchScalarGridSpec(
            num_scalar_prefetch=1, grid=(pieces, nb),
            in_specs=[pl.BlockSpec((None, tm, width), lambda p, i, c_ref: (p, c_ref[0] * nb + i, 0)), blk],
            out_specs=(blk, blk)),
        compiler_params=_params(("parallel", "parallel")),
    )(c, gpack, got)


def _chip_exchange(pieces, *, name):
    _, rows, width = pieces.shape

    def body(in_ref, out_ref, send_sems, recv_sems):
        x, y, c = _place()
        cps = [pltpu.make_async_remote_copy(src_ref=in_ref.at[2 * cx + cy], dst_ref=out_ref.at[j],
                                            send_sem=send_sems.at[j], recv_sem=recv_sems.at[j],
                                            device_id=(cx, cy, c), device_id_type=MESH)
               for j, (cx, cy) in enumerate(_other_chips(x, y))]
        for cp in cps:
            cp.start()
        for cp in cps:
            cp.wait()

    return pl.pallas_call(
        body, name=name, out_shape=jax.ShapeDtypeStruct((3, rows, width), pieces.dtype), in_specs=[ANY], out_specs=ANY,
        scratch_shapes=[pltpu.SemaphoreType.DMA((3,)), pltpu.SemaphoreType.DMA((3,))],
    )(pieces)


def _chip_exchange_copies(in_ref, land_ref, send_sems, recv_sems):
    x, y, c = _place()
    return [pltpu.make_async_remote_copy(src_ref=in_ref.at[2 * cx + cy], dst_ref=land_ref.at[j],
                                         send_sem=send_sems.at[j], recv_sem=recv_sems.at[j],
                                         device_id=(cx, cy, c), device_id_type=MESH)
            for j, (cx, cy) in enumerate(_other_chips(x, y))]


def _chip_exchange_start(pieces):
    _, rows, width = pieces.shape

    def body(in_ref, land_ref, send_sems, recv_sems, in_thru, land_thru, token):
        for cp in _chip_exchange_copies(in_ref, land_ref, send_sems, recv_sems):
            cp.start()
        token[...] = jnp.zeros_like(token)

    land = pltpu.with_memory_space_constraint(lax.empty((3, rows, width), pieces.dtype), pltpu.HBM)
    return pl.pallas_call(
        body, name="rs_chip_start",
        out_shape=(pltpu.SemaphoreType.DMA((3,)), pltpu.SemaphoreType.DMA((3,)), pltpu.HBM(pieces.shape, pieces.dtype),
                   pltpu.HBM(land.shape, land.dtype), jax.ShapeDtypeStruct((SUBLANES, LANES), F32)),
        in_specs=(HBM, HBM), out_specs=(SEM, SEM, HBM, HBM, pl.BlockSpec(memory_space=pltpu.VMEM)),
        input_output_aliases={0: 2, 1: 3},
        compiler_params=pltpu.CompilerParams(has_side_effects=DATAFLOW),
    )(pltpu.with_memory_space_constraint(pieces, pltpu.HBM), land)


def _chip_exchange_wait(send_sems, recv_sems, in_thru, land_thru, after):
    def body(in_ref, land_ref, send_sems, recv_sems, after_ref, in_dead, land_out):
        for cp in _chip_exchange_copies(in_ref, land_ref, send_sems, recv_sems):
            cp.wait_send()
            cp.wait_recv()

    return pl.pallas_call(
        body, name="rs_chip_wait",
        out_shape=(pltpu.HBM(in_thru.shape, in_thru.dtype), pltpu.HBM(land_thru.shape, land_thru.dtype)),
        in_specs=(HBM, HBM, SEM, SEM, ANY), out_specs=(HBM, HBM), input_output_aliases={0: 0, 1: 1},
        compiler_params=pltpu.CompilerParams(has_side_effects=DATAFLOW),
    )(in_thru, land_thru, send_sems, recv_sems, after)[1]


def _all_reduce_small(v, *, name):
    rows, width = v.shape
    half = rows // 2
    assert half % SUBLANES == 0

    def body(in_ref, out_ref, pair_buf, chip_buf, send_sems, recv_sems):
        x, y, c = _place()
        sibling = (x, y, 1 - c)
        me = 2 * x + y
        mine = pl.ds(pl.multiple_of(c * half, SUBLANES), half)
        other = pl.ds(pl.multiple_of((1 - c) * half, SUBLANES), half)

        def copy(k, src, dst, to):
            return pltpu.make_async_remote_copy(src_ref=src, dst_ref=dst, send_sem=send_sems.at[k],
                                                recv_sem=recv_sems.at[k], device_id=to, device_id_type=MESH)

        swap = copy(0, in_ref.at[other, :], pair_buf, sibling)
        swap.start()
        swap.wait()
        chip_buf[me] = in_ref[mine, :] + pair_buf[...]
        chips = _other_chips(x, y)
        for j, (cx, cy) in enumerate(chips):
            copy(1 + j, chip_buf.at[me], chip_buf.at[me], (cx, cy, c)).start()
        for j, (cx, cy) in enumerate(chips):
            got = copy(1 + j, chip_buf.at[me], chip_buf.at[2 * cx + cy], (cx, cy, c))
            got.wait_send()
            got.wait_recv()
        out_ref[mine, :] = ((chip_buf[0] + chip_buf[1]) + chip_buf[2]) + chip_buf[3]
        share = copy(1 + len(chips), out_ref.at[mine, :], out_ref.at[mine, :], sibling)
        share.start()
        share.wait_send()
        copy(1 + len(chips), out_ref.at[other, :], out_ref.at[other, :], sibling).wait_recv()

    vmem = pl.BlockSpec(memory_space=pltpu.VMEM)
    return pl.pallas_call(
        body, name=name, out_shape=jax.ShapeDtypeStruct(v.shape, v.dtype), in_specs=[vmem], out_specs=vmem,
        scratch_shapes=[pltpu.VMEM((half, width), v.dtype), pltpu.VMEM((N_CHIPS, half, width), v.dtype),
                        pltpu.SemaphoreType.DMA((5,)), pltpu.SemaphoreType.DMA((5,))],
        compiler_params=pltpu.CompilerParams(vmem_limit_bytes=VMEM_LIMIT_V7X),
    )(v)


def _reduce_scatter_begin(gpack, behind):
    x, y, c = _place()
    got = _pair_exchange(gpack, name="rs_pair_swap")
    pair, pair16 = _pair_add(gpack, got, c.astype(jnp.int32).reshape(1), name="rs_pair_add")
    mine = lax.dynamic_index_in_dim(pair, 2 * x + y, axis=0, keepdims=False)
    if behind:
        *in_flight, token = _chip_exchange_start(pair16)
        return dict(mine=mine, in_flight=in_flight), token
    return dict(mine=mine, recv=_chip_exchange(pair16, name="rs_chip_exchange")), None


def _reduce_scatter_end(state, after=None):
    c = lax.axis_index("c")
    recv = state["recv"] if "recv" in state else _chip_exchange_wait(*state["in_flight"], after=after)
    (total,) = _rw_fwd(_f_add4, [state["mine"], recv[0], recv[1], recv[2]], [], tm=512, name="rs_chip_add")
    theirs = _sibling_swap(total, name="rs_share_swap")
    return jnp.concatenate([jnp.where(c == 0, total, theirs), jnp.where(c == 0, theirs, total)], axis=0)


_SHARDED = (("w_kv_mem", 1), ("w_o", 1), ("mlp_w1", 2), ("mlp_w2", 1), ("gdn_w_in", 2), ("gdn_conv_w", 2),
            ("s5_w_in", 2), ("s5_d", 1), ("s5_w_glu", 1), ("s5_b_glu", 1))
_MATMUL_ONLY = ("w_kv_mem", "w_o", "mlp_w1", "mlp_w2", "gdn_w_in", "s5_w_in", "s5_w_glu")
_KEPT_BLOCKED = ("mlp_w1",)
_REPLICATED = ("ln1_g", "ln1_b", "ln2_g", "ln2_b", "gdn_a_log", "gdn_dt_bias", "gdn_norm_g", "s5_a_re", "s5_a_im",
               "s5_b_re", "s5_b_im", "s5_c_re", "s5_c_im", "s5_log_dt")
_WEIGHTS = ("w_kv_mem", "w_o", "ln1_g", "ln1_b", "ln2_g", "ln2_b", "mlp_w1", "mlp_w2", "gdn_w_in", "gdn_conv_w",
            "gdn_a_log", "gdn_dt_bias", "gdn_norm_g", "s5_w_in", "s5_a_re", "s5_a_im", "s5_b_re", "s5_b_im",
            "s5_c_re", "s5_c_im", "s5_log_dt", "s5_d", "s5_w_glu", "s5_b_glu")


ROW_ALIGN = 16


def _n_rows(shape):
    return -(-math.prod(shape) // (ROW_ALIGN * D_MODEL)) * ROW_ALIGN


def _as_rows(a):
    rows = _n_rows(a.shape)
    if a.shape[-1] == D_MODEL and a.size == rows * D_MODEL:
        return a.reshape(-1, D_MODEL)
    flat = a.reshape(-1)
    return jnp.pad(flat, (0, rows * D_MODEL - flat.size)).reshape(rows, D_MODEL)


def _pack(arrs, unit_rows=SHARD_ROWS):
    rows = [_as_rows(a) for a in arrs]
    pad = -sum(r.shape[0] for r in rows) % unit_rows
    if pad:
        rows.append(jnp.zeros((pad, D_MODEL), rows[0].dtype))
    return jnp.concatenate(rows, axis=0)


def _unpack(packed, shapes):
    lead = packed.shape[:-2]
    out, off = [], 0
    for s in shapes:
        r = _n_rows(s)
        seg = lax.slice_in_dim(packed, off, off + r, axis=len(lead))
        if s[-1] != D_MODEL or math.prod(s) != r * D_MODEL:
            seg = lax.slice_in_dim(seg.reshape(lead + (-1,)), 0, math.prod(s), axis=len(lead))
        out.append(seg.reshape(lead + tuple(s)))
        off += r
    return out


def _split3(t):
    hi = t.astype(BF16)
    r1 = t - hi.astype(F32)
    mid = r1.astype(BF16)
    lo = (r1 - mid.astype(F32)).astype(BF16)
    return jnp.stack([hi, mid, lo], axis=-1)


def _join3(t):
    return (t[..., 0].astype(F32) + t[..., 1].astype(F32)) + t[..., 2].astype(F32)


def _merge_chips(blocks, axis):
    return jnp.concatenate([blocks[s] for s in range(N_CHIPS)], axis=axis)


def _pack_for_chips(weights):
    rows = []
    for s in range(N_CHIPS):
        chip = []
        for layers, axis in weights:
            if axis is None:
                blocks = [g[s] for g in layers]
            else:
                n = layers[0].shape[axis] // N_CHIPS
                blocks = [lax.slice_in_dim(g, s * n, (s + 1) * n, axis=axis) for g in layers]
            if math.prod(blocks[0].shape) % (ROW_ALIGN * D_MODEL) == 0:
                chip += [_as_rows(b) for b in blocks]
            else:
                chip.append(_as_rows(jnp.stack(blocks)))
        pad = -sum(r.shape[0] for r in chip) % SHARD_ROWS
        rows += chip + ([jnp.zeros((pad, D_MODEL), F32)] if pad else [])
    return jnp.concatenate(rows, axis=0).reshape(N_CHIPS, -1, D_MODEL)


def kernel(x, mem, w_kv_mem, w_o, ln1_g, ln1_b, ln2_g, ln2_b, mlp_w1, mlp_w2, gdn_w_in, gdn_conv_w, gdn_a_log, gdn_dt_bias, gdn_norm_g, s5_w_in, s5_a_re, s5_a_im, s5_b_re, s5_b_im, s5_c_re, s5_c_im, s5_log_dt, s5_d, s5_w_glu, s5_b_glu, loss_target, m_w_kv_mem, m_w_o, m_ln1_g, m_ln1_b, m_ln2_g, m_ln2_b, m_mlp_w1, m_mlp_w2, m_gdn_w_in, m_gdn_conv_w, m_gdn_a_log, m_gdn_dt_bias, m_gdn_norm_g, m_s5_w_in, m_s5_a_re, m_s5_a_im, m_s5_b_re, m_s5_b_im, m_s5_c_re, m_s5_c_im, m_s5_log_dt, m_s5_d, m_s5_w_glu, m_s5_b_glu, v_w_kv_mem, v_w_o, v_ln1_g, v_ln1_b, v_ln2_g, v_ln2_b, v_mlp_w1, v_mlp_w2, v_gdn_w_in, v_gdn_conv_w, v_gdn_a_log, v_gdn_dt_bias, v_gdn_norm_g, v_s5_w_in, v_s5_a_re, v_s5_a_im, v_s5_b_re, v_s5_b_im, v_s5_c_re, v_s5_c_im, v_s5_log_dt, v_s5_d, v_s5_w_glu, v_s5_b_glu):
    given = dict(locals())
    w = {n: given[n] for n in _WEIGHTS}
    mom = {n: given["m_" + n] for n in _WEIGHTS}
    var = {n: given["v_" + n] for n in _WEIGHTS}
    shard_names = [n for n, _ in _SHARDED]
    shard_shapes = [w[n].shape for n in shard_names]
    rep_shapes = [w[n].shape for n in _REPLICATED]

    wire = {n: w[n].astype(BF16) if n in _MATMUL_ONLY else _split3(w[n]) for n in shard_names}
    first = {n: 0 if n.startswith("s5_") else 1 for n in shard_names}
    me_chip = 2 * lax.axis_index("x") + lax.axis_index("y")
    early = [wire[n][:first[n]] for n in shard_names if first[n]]
    late = [wire[n][first[n]:] for n in shard_names]
    early_pack, late_pack = _pack(early), _pack(late)
    *first_copies, first_token = _gather_start(early_pack, after=w["s5_log_dt"], name="gather_first_start")
    s5_names = ("s5_a_re", "s5_a_im", "s5_b_re", "s5_b_im", "s5_c_re", "s5_c_im")
    s5_preps = [jax.vjp(_s5_prep, *[w[n][j] for n in s5_names], w["s5_log_dt"][j] + first_token[0, 0])
                for j in range(w["s5_log_dt"].shape[0])]
    all_there = sum(op.reshape(-1)[:1] for ops, _ in s5_preps for op in ops)
    landed = _gather_wait(*first_copies, after=all_there, name="gather_first_wait")
    landed = _gather_forward(landed, name="gather_first_forward")
    landed = lax.dynamic_update_index_in_dim(landed, early_pack, me_chip, axis=0)
    early_blocks = dict(zip([n for n in shard_names if first[n]], _unpack(landed, [a.shape for a in early])))
    send_sems, recv_sems, pack_thru, land_thru, token = _gather_start(late_pack, after=landed, name="gather_start")
    axis_of = dict(_SHARDED)

    def merged(n, blk):
        if n in _KEPT_BLOCKED:
            return blk
        return _merge_chips(blk if n in _MATMUL_ONLY else _join3(blk), axis_of[n] - 1)

    late_full = {}

    def weights_of(i, h):
        if i == 0:
            full = {n: [merged(n, blk[:, 0])] for n, blk in early_blocks.items()}
            full["gdn_w_in"][0] = full["gdn_w_in"][0] + token[0, 0].astype(BF16)
        else:
            if not late_full:
                land = _gather_wait(send_sems, recv_sems, pack_thru, land_thru, after=h, name="gather_wait")
                land = _gather_forward(land, name="gather_forward")
                land = lax.dynamic_update_index_in_dim(land, late_pack, me_chip, axis=0)
                for n, blk in zip(shard_names, _unpack(land, [a.shape for a in late])):
                    late_full[n] = [None] * first[n] + [merged(n, blk[:, t]) for t in range(blk.shape[1])]
            full = dict(late_full)
        full.update({n: w[n] for n in _REPLICATED})
        full["s5_prep"] = s5_preps
        return _layer_weights(full, i)

    sharded = {}
    in_flight = {}
    first_mixer, first_outer = (0, "mixer"), (0, "outer")

    def group_pack(parts):
        names = [n for n in shard_names if any(n in sharded[i] for i in parts)]
        per_weight = [[sharded[i][n] for i in parts if n in sharded[i]] for n in names]
        pack = _pack_for_chips([(g, None if n in _KEPT_BLOCKED else axis_of[n] - 1) for n, g in zip(names, per_weight)])
        return pack, names, [(len(g),) + w[n].shape[1:] for n, g in zip(names, per_weight)]

    def grads_ready(i, g):
        by_weight = _sharded_grads(g, i)
        if i:
            sharded[i] = by_weight
        else:
            sharded[first_mixer] = {n: g for n, g in by_weight.items() if n not in sharded[first_outer]}

    def before_first_mixer(outer):
        sharded[first_outer] = outer
        pack, names, shapes = group_pack([first_outer] + list(range(1, DEPTH)))
        state, token = _reduce_scatter_begin(pack, behind=True)
        in_flight.update(state=state, names=names, shapes=shapes)
        return token

    loss, grad_x, layer_grads = _local_step(x[0], mem[0], loss_target[0], weights_of, grads_ready,
                                            before_first_mixer)
    loss = lax.psum(loss, ("x", "y", "c"))
    pack, names, shapes = group_pack([first_mixer])
    state, _ = _reduce_scatter_begin(pack, behind=False)
    pieces = {n: [] for n in shard_names}
    for n, g in zip(names, _unpack(_reduce_scatter_end(state), shapes)):
        pieces[n].append(g)
    late = _reduce_scatter_end(in_flight["state"], after=grad_x)
    for n, g in zip(in_flight["names"], _unpack(late, in_flight["shapes"])):
        pieces[n].append(g)
    g_shards = [p[0] if len(p) == 1 else jnp.concatenate(p, axis=0) for p in (pieces[n] for n in shard_names)]

    def pack_small(d):
        return _pack([d[n] for n in _REPLICATED], unit_rows=SMALL_ROWS)

    g_rep = _all_reduce_small(pack_small(_replicated_grads(layer_grads)), name="reduce_replicated")

    def adamw(wp, gp, mp, vp, name):
        return _rw_fwd(_f_adamw, [wp, gp, mp, vp], [], tm=TM_WIDE, name=name)

    outs = {}
    for n, g in zip(shard_names, g_shards):
        flat = (-1, w[n].shape[-1])
        res = adamw(w[n].reshape(flat), g.reshape(flat), mom[n].reshape(flat), var[n].reshape(flat), "adamw_" + n)
        outs[("grad", n)] = g
        outs.update({(kind, n): a.reshape(w[n].shape) for kind, a in zip(("delta", "new_m", "new_v"), res)})
    packed = (g_rep,) + tuple(adamw(pack_small(w), g_rep, pack_small(mom), pack_small(var), "adamw_replicated"))
    for kind, pr in zip(("grad", "delta", "new_m", "new_v"), packed):
        outs.update({(kind, n): a for n, a in zip(_REPLICATED, _unpack(pr, rep_shapes))})
    return (loss, grad_x[None]) + tuple(outs[(kind, n)] for kind in ("grad", "delta", "new_m", "new_v")
                                        for n in _WEIGHTS)
```

```python
import functools
import math

import jax
import jax.numpy as jnp
from jax import lax
from jax.experimental import pallas as pl
from jax.experimental.pallas import tpu as pltpu

F32 = jnp.float32
BF16 = jnp.bfloat16
MESH = pl.DeviceIdType.MESH

D_MODEL = 1024
DEPTH = 4
GDN_HEADS = 8
HEAD_DIM = 128
GDN_CONV = 4
GDN_CHUNK = 64
S5_GROUP = 16
S5_STATE = 64
XA_HEADS = 4
XA_DIM = 512
DN_ALPHA = (2 * DEPTH) ** 0.25
LN_EPS = 1e-5
RMS_EPS = 1e-6
ADAM_LR, ADAM_B1, ADAM_B2, ADAM_EPS, ADAM_WD, ADAM_STEP = 0.001, 0.9, 0.999, 1e-08, 0.01, 10

VMEM_LIMIT_V7X = 56 * 1024 * 1024
LANES = 128
SUBLANES = 8
S5_T = 16
S5_TILES = D_MODEL // LANES
N_CHIPS = 4


def _params(sem):
    return pltpu.CompilerParams(dimension_semantics=sem, vmem_limit_bytes=VMEM_LIMIT_V7X)


def _tile(n, pref):
    if n <= pref:
        return n
    t = (pref // LANES) * LANES
    while n % t:
        t -= LANES
    return t


def _row_tile(n, pref):
    if n % SUBLANES:
        return n
    t = min(pref, n) // SUBLANES * SUBLANES
    while n % t:
        t -= SUBLANES
    return t


def _col_blocked_spec(rows_tile, cols_tile, block_cols, rows_axis, cols_axis):
    r = block_cols // cols_tile

    def index(*ijk):
        c = ijk[cols_axis]
        return (c, ijk[rows_axis], 0) if r == 1 else (c // r, ijk[rows_axis], c % r)

    return pl.BlockSpec((None, rows_tile, cols_tile), index)


def _mm(a, b, *, ta=False, tb=False, acc=None, name, tm=1024, tn=1024, tk=None, out_blocks=0):
    if tk is None:
        tk = 4096 if a.dtype == BF16 and b.dtype == BF16 else 2048
    k_dim, m_dim = a.shape if ta else a.shape[::-1]
    b_rows, b_cols = (b.shape[0], b.shape[1]) if b.ndim == 2 else (b.shape[1], b.shape[0] * b.shape[2])
    n_dim = b_rows if tb else b_cols
    assert (b_cols if tb else b_rows) == k_dim, (a.shape, b.shape, ta, tb)
    limit_n = n_dim // out_blocks if out_blocks else (n_dim if b.ndim == 2 or tb else b.shape[2])
    limit_k = b.shape[2] if (b.ndim == 3 and tb) else k_dim
    tm, tn, tk = _tile(m_dim, tm), _tile(limit_n, min(tn, limit_n)), _tile(limit_k, min(tk, limit_k))
    a_spec = (pl.BlockSpec((tk, tm), lambda i, j, k: (k, i)) if ta else pl.BlockSpec((tm, tk), lambda i, j, k: (i, k)))
    if b.ndim == 3:
        b_spec = (_col_blocked_spec(tn, tk, b.shape[2], 1, 2) if tb else _col_blocked_spec(tk, tn, b.shape[2], 2, 1))
    else:
        b_spec = (pl.BlockSpec((tn, tk), lambda i, j, k: (j, k)) if tb
                  else pl.BlockSpec((tk, tn), lambda i, j, k: (k, j)))
    o_spec = (_col_blocked_spec(tm, tn, n_dim // out_blocks, 0, 1) if out_blocks
              else pl.BlockSpec((tm, tn), lambda i, j, k: (i, j)))
    o_shape = (out_blocks, m_dim, n_dim // out_blocks) if out_blocks else (m_dim, n_dim)
    dn = (((0 if ta else 1,), (1 if tb else 0,)), ((), ()))
    has_acc = acc is not None

    def body(*refs):
        a_ref, b_ref = refs[0], refs[1]
        o_ref = refs[-1]
        k = pl.program_id(2)
        p = lax.dot_general(a_ref[...].astype(BF16), b_ref[...].astype(BF16), dn,
                            preferred_element_type=F32)

        @pl.when(k == 0)
        def _():
            o_ref[...] = p + refs[2][...] if has_acc else p

        @pl.when(k > 0)
        def _():
            o_ref[...] += p

    return pl.pallas_call(
        body, name=name,
        out_shape=jax.ShapeDtypeStruct(o_shape, F32),
        grid=(m_dim // tm, n_dim // tn, k_dim // tk),
        in_specs=[a_spec, b_spec] + ([o_spec] if has_acc else []),
        out_specs=o_spec,
        compiler_params=_params(("parallel", "parallel", "arbitrary")),
    )(*([a, b] + ([acc] if has_acc else [])))


def _mm_relu2(a, b, *, name, tm=1024):
    m_dim, k_dim = a.shape
    n_blocks, _, tn = b.shape
    n_dim = n_blocks * tn
    tm = _tile(m_dim, tm)

    def body(a_ref, b_ref, h_ref, act_ref):
        h = jnp.dot(a_ref[...].astype(BF16), b_ref[...].astype(BF16), preferred_element_type=F32)
        h_ref[...] = h.astype(h_ref.dtype)
        r = jnp.maximum(h, 0.0)
        act_ref[...] = (r * r).astype(BF16)

    o_spec = pl.BlockSpec((tm, tn), lambda i, j: (i, j))
    return pl.pallas_call(
        body, name=name,
        out_shape=(jax.ShapeDtypeStruct((m_dim, n_dim), BF16), jax.ShapeDtypeStruct((m_dim, n_dim), BF16)),
        grid=(m_dim // tm, n_dim // tn),
        in_specs=[pl.BlockSpec((tm, k_dim), lambda i, j: (i, 0)),
                  pl.BlockSpec((None, k_dim, tn), lambda i, j: (j, 0, 0))],
        out_specs=(o_spec, o_spec),
        compiler_params=_params(("parallel", "parallel")),
    )(a, b)


def _mm_relu2_grad(d, b, h, *, name, tm=1024, tn=1024):
    m_dim, k_dim = d.shape
    n_dim = b.shape[0]
    tm, tn = _tile(m_dim, tm), _tile(n_dim, tn)

    def body(d_ref, b_ref, h_ref, o_ref):
        p = lax.dot_general(d_ref[...].astype(BF16), b_ref[...].astype(BF16), ((NT), ((), ())),
                            preferred_element_type=F32)
        o_ref[...] = (p * (2.0 * jnp.maximum(h_ref[...].astype(F32), 0.0))).astype(BF16)

    o_spec = pl.BlockSpec((tm, tn), lambda i, j: (i, j))
    return pl.pallas_call(
        body, name=name,
        out_shape=jax.ShapeDtypeStruct((m_dim, n_dim), BF16),
        grid=(m_dim // tm, n_dim // tn),
        in_specs=[pl.BlockSpec((tm, k_dim), lambda i, j: (i, 0)), pl.BlockSpec((tn, k_dim), lambda i, j: (j, 0)), o_spec],
        out_specs=o_spec,
        compiler_params=_params(("parallel", "parallel")),
    )(d, b, h)


def _rowwise(f, rows, params, row_out, acc_out, *, tm, name):
    length = rows[0].shape[0]
    tm = _row_tile(length, tm)
    nr, npar, nro = len(rows), len(params), len(row_out)

    def body(*refs):
        ins = [r[...] for r in refs[:nr + npar]]
        outs = refs[nr + npar:]
        r_o, a_o = f(*ins)
        for ref, val in zip(outs[:nro], r_o):
            ref[...] = val.astype(ref.dtype)
        i = pl.program_id(0)
        for ref, val in zip(outs[nro:], a_o):
            @pl.when(i == 0)
            def _(ref=ref, val=val):
                ref[...] = val.astype(ref.dtype)

            @pl.when(i > 0)
            def _(ref=ref, val=val):
                ref[...] += val.astype(ref.dtype)

    in_specs = ([pl.BlockSpec((tm, r.shape[1]), lambda i: (i, 0)) for r in rows]
                + [pl.BlockSpec(p.shape, lambda i: (0, 0)) for p in params])
    out_specs = ([pl.BlockSpec((tm, w), lambda i: (i, 0)) for w, _ in row_out]
                 + [pl.BlockSpec(s, lambda i: (0, 0)) for s, _ in acc_out])
    out_shape = ([jax.ShapeDtypeStruct((length, w), dt) for w, dt in row_out]
                 + [jax.ShapeDtypeStruct(s, dt) for s, dt in acc_out])
    res = pl.pallas_call(
        body, name=name, out_shape=out_shape, grid=(length // tm,),
        in_specs=in_specs, out_specs=out_specs,
        compiler_params=_params(("arbitrary",) if acc_out else ("parallel",)),
    )(*rows, *params)
    return res[:nro], res[nro:]


def _rw_fwd(f, rows, params, *, tm, name, out_dtypes=None):
    tm_ = _row_tile(rows[0].shape[0], tm)
    shapes = jax.eval_shape(f, *[jax.ShapeDtypeStruct((tm_, r.shape[1]), r.dtype) for r in rows],
                            *[jax.ShapeDtypeStruct(p.shape, p.dtype) for p in params])
    row_out = [(s.shape[1], s.dtype if out_dtypes is None else dt)
               for s, dt in zip(shapes, out_dtypes or shapes)]
    outs, _ = _rowwise(lambda *v: (f(*v), ()), rows, params, row_out, [], tm=tm, name=name)
    return outs


def _rw_bwd(f, rows, params, cots, *, row_grad, param_grad, tm, name, row_dtypes=None):
    nr, npar, nct = len(rows), len(params), len(cots)

    def g(*vals):
        prim = vals[:nr] + vals[nr + nct:]
        ct = vals[nr:nr + nct]
        _, vjp = jax.vjp(f, *prim)
        grads = vjp(tuple(ct))
        return (tuple(grads[i] for i in range(nr) if row_grad[i]),
                tuple(grads[nr + i] for i in range(npar) if param_grad[i]))

    widths = [rows[i].shape[1] for i in range(nr) if row_grad[i]]
    row_out = list(zip(widths, row_dtypes or [F32] * len(widths)))
    acc_out = [(params[i].shape, F32) for i in range(npar) if param_grad[i]]
    return _rowwise(g, list(rows) + list(cots), params, row_out, acc_out, tm=tm, name=name)


def _f_ln_res(x, h, g, b):
    pre = DN_ALPHA * x + h
    mu = jnp.mean(pre, axis=-1, keepdims=True)
    d = pre - mu
    var = jnp.mean(d * d, axis=-1, keepdims=True)
    return (d * lax.rsqrt(var + LN_EPS) * g + b,)


def _silu(t):
    return t * jax.nn.sigmoid(t)


def _f_gdn_qkv(c):
    a = _silu(c)
    outs = []
    for part, scale in ((0, HEAD_DIM ** -0.5), (1, 1.0)):
        heads = []
        for h in range(GDN_HEADS):
            t = a[:, part * D_MODEL + h * HEAD_DIM: part * D_MODEL + (h + 1) * HEAD_DIM]
            t = t * lax.rsqrt(jnp.sum(t * t, axis=-1, keepdims=True) + 1e-6)
            heads.append(t * scale if scale != 1.0 else t)
        outs.append(jnp.concatenate(heads, axis=-1))
    outs.append(a[:, 2 * D_MODEL:])
    return tuple(outs)


def _f_gdn_out(o, z, norm_g):
    heads = []
    for h in range(GDN_HEADS):
        t = o[:, h * HEAD_DIM:(h + 1) * HEAD_DIM]
        t = t * lax.rsqrt(jnp.mean(t * t, axis=-1, keepdims=True) + RMS_EPS) * norm_g
        heads.append(t)
    return (jnp.concatenate(heads, axis=-1) * _silu(z),)


def _f_attn(xq, kmem, vmem):
    heads = []
    for h in range(XA_HEADS):
        sl = slice(h * HEAD_DIM, (h + 1) * HEAD_DIM)
        s = lax.dot_general(xq[:, sl].astype(BF16), kmem[:, sl].astype(BF16),
                            (((1,), (1,)), ((), ())), preferred_element_type=F32) * (HEAD_DIM ** -0.5)
        m = lax.stop_gradient(jnp.max(s, axis=-1, keepdims=True))
        e = jnp.exp(s - m)
        p = e / jnp.sum(e, axis=-1, keepdims=True)
        heads.append(jnp.dot(p.astype(BF16), vmem[:, sl].astype(BF16), preferred_element_type=F32))
    return (jnp.concatenate(heads, axis=-1),)


def _f_s5_gelu(y, u, d):
    return (jax.nn.gelu(y + d * u),)


def _f_s5_gate(zg, t, b):
    return (zg * jax.nn.sigmoid(t + b),)


def _f_add(a, b):
    return (a + b,)


def _f_add4(a, b, c, d):
    return (((a + b.astype(F32)) + c.astype(F32)) + d.astype(F32),)


def _f_adamw(w, g, m, v):
    m = ADAM_B1 * m + (1.0 - ADAM_B1) * g
    v = ADAM_B2 * v + (1.0 - ADAM_B2) * jnp.square(g)
    m_hat = m / (1.0 - ADAM_B1 ** ADAM_STEP)
    v_hat = v / (1.0 - ADAM_B2 ** ADAM_STEP)
    delta = -ADAM_LR * (m_hat / (jnp.sqrt(v_hat) + ADAM_EPS) + ADAM_WD * w)
    return delta, m, v


def _conv_fwd(u, w, *, tm, name):
    length, chans = u.shape
    tm = min(tm, length)
    tc = D_MODEL
    assert chans == 3 * tc
    hb = tm // SUBLANES

    def body(cur_ref, prev_ref, w_ref, o_ref, act_ref, buf):
        part, i = pl.program_id(0), pl.program_id(1)
        buf[0:SUBLANES, :] = jnp.where(i > 0, prev_ref[...], 0.0)
        buf[SUBLANES:, :] = cur_ref[...]
        acc = buf[pl.ds(SUBLANES - 3, tm), :] * w_ref[0:1, :]
        for k in range(1, GDN_CONV):
            acc = acc + buf[pl.ds(SUBLANES - 3 + k, tm), :] * w_ref[k:k + 1, :]
        o_ref[...] = acc
        a = _silu(acc)
        heads = []
        for h in range(GDN_HEADS):
            t = a[:, h * HEAD_DIM:(h + 1) * HEAD_DIM]
            heads.append(t * lax.rsqrt(jnp.sum(t * t, axis=-1, keepdims=True) + 1e-6))
        normed = jnp.concatenate(heads, axis=-1)
        act_ref[...] = jnp.where(part == 0, normed * (HEAD_DIM ** -0.5), jnp.where(part == 1, normed, a))

    blk = pl.BlockSpec((tm, tc), lambda j, i: (i, j))
    return pl.pallas_call(
        body, name=name, out_shape=(jax.ShapeDtypeStruct(u.shape, F32), jax.ShapeDtypeStruct(u.shape, F32)),
        grid=(chans // tc, length // tm),
        in_specs=[blk, pl.BlockSpec((SUBLANES, tc), lambda j, i: (jnp.maximum(i * hb - 1, 0), j)),
                  pl.BlockSpec((GDN_CONV, tc), lambda j, i: (0, j))],
        out_specs=(blk, blk),
        scratch_shapes=[pltpu.VMEM((tm + SUBLANES, tc), F32)],
        compiler_params=_params(("parallel", "parallel")),
    )(u, u, w)


def _conv_bwd(u, w, dc, *, tm, name):
    length, chans = u.shape
    tm = min(tm, length)
    tc = _tile(chans, 1024)
    hb = tm // SUBLANES
    last = length // tm - 1

    def body(u_ref, uprev_ref, dc_ref, dcnext_ref, w_ref, du_ref, dw_ref, ubuf, dbuf):
        i = pl.program_id(1)
        ubuf[0:SUBLANES, :] = jnp.where(i > 0, uprev_ref[...], 0.0)
        ubuf[SUBLANES:, :] = u_ref[...]
        dbuf[0:tm, :] = dc_ref[...]
        dbuf[tm:, :] = jnp.where(i < last, dcnext_ref[...], 0.0)
        dcv = dc_ref[...]
        du = dbuf[pl.ds(3, tm), :] * w_ref[0:1, :]
        rows = [jnp.sum(dcv * ubuf[pl.ds(SUBLANES - 3, tm), :], axis=0, keepdims=True)]
        for k in range(1, GDN_CONV):
            du = du + dbuf[pl.ds(3 - k, tm), :] * w_ref[k:k + 1, :]
            rows.append(jnp.sum(dcv * ubuf[pl.ds(SUBLANES - 3 + k, tm), :], axis=0, keepdims=True))
        du_ref[...] = du.astype(du_ref.dtype)
        dwv = jnp.concatenate(rows, axis=0)

        @pl.when(i == 0)
        def _():
            dw_ref[...] = dwv

        @pl.when(i > 0)
        def _():
            dw_ref[...] += dwv

    return pl.pallas_call(
        body, name=name,
        out_shape=(jax.ShapeDtypeStruct(u.shape, BF16), jax.ShapeDtypeStruct((GDN_CONV, chans), F32)),
        grid=(chans // tc, length // tm),
        in_specs=[pl.BlockSpec((tm, tc), lambda j, i: (i, j)),
                  pl.BlockSpec((SUBLANES, tc), lambda j, i: (jnp.maximum(i * hb - 1, 0), j)),
                  pl.BlockSpec((tm, tc), lambda j, i: (i, j)),
                  pl.BlockSpec((SUBLANES, tc), lambda j, i: (jnp.minimum((i + 1) * hb, (last + 1) * hb - 1), j)),
                  pl.BlockSpec((GDN_CONV, tc), lambda j, i: (0, j))],
        out_specs=(pl.BlockSpec((tm, tc), lambda j, i: (i, j)),
                   pl.BlockSpec((GDN_CONV, tc), lambda j, i: (0, j))),
        scratch_shapes=[pltpu.VMEM((tm + SUBLANES, tc), F32), pltpu.VMEM((tm + SUBLANES, tc), F32)],
        compiler_params=_params(("parallel", "arbitrary")),
    )(u, u, dc, dc, w)


def _dot(a, b, dims, precision=None):
    if precision is None:
        a, b = a.astype(BF16), b.astype(BF16)
    return lax.dot_general(a, b, (dims, ((), ())), preferred_element_type=F32, precision=precision)


def _dot3(a, b, dims):
    ah, bh = a.astype(BF16), b.astype(BF16)
    al, bl = (a - ah.astype(F32)).astype(BF16), (b - bh.astype(F32)).astype(BF16)

    def d(x, y):
        return lax.dot_general(x, y, (dims, ((), ())), preferred_element_type=F32)

    return d(ah, bh) + (d(ah, bl) + d(al, bh))


NN = ((1,), (0,))
NT = ((1,), (1,))
TN = ((0,), (0,))
HI = lax.Precision.HIGHEST


def _hmap(f, *lists):
    return [f(*t) for t in zip(*lists)]


@jax.custom_vjp
def _unit_lower_inverse(a):
    c = a[0].shape[0]
    eye = (lax.broadcasted_iota(jnp.int32, (c, c), 0) == lax.broadcasted_iota(jnp.int32, (c, c), 1)).astype(F32)
    p = _hmap(lambda x: -x, a)
    t = _hmap(lambda x: eye + x, p)
    for _ in range(int(math.log2(c)) - 1):
        p = _hmap(lambda x: _dot3(x, x, NN), p)
        t = _hmap(lambda x, y: x + _dot3(x, y, NN), t, p)
    return t


def _uli_fwd(a):
    t = _unit_lower_inverse(a)
    return t, t


def _uli_bwd(t, dt):
    left = _hmap(lambda x, y: _dot3(x, y, TN), t, dt)
    return (_hmap(lambda x, y: -_dot3(x, y, NT), left, t),)


_unit_lower_inverse.defvjp(_uli_fwd, _uli_bwd)


@jax.custom_vjp
def _known_inverse(a, t):
    return t


_known_inverse.defvjp(lambda a, t: (t, t),
                      lambda t, dt: (_uli_bwd(t, dt)[0], _hmap(jnp.zeros_like, t)))


def _gdn_chunk(q, k, v, bl, al, a_log, dt_bias, state, t_known=None):
    c = q[0].shape[0]
    row = lax.broadcasted_iota(jnp.int32, (c, c), 0)
    col = lax.broadcasted_iota(jnp.int32, (c, c), 1)
    causal = row >= col
    strict = row > col
    eye = (row == col).astype(F32)
    beta = _hmap(jax.nn.sigmoid, bl)
    g = _hmap(lambda a_, l_, d_: -jnp.exp(a_) * jax.nn.softplus(l_ + d_), a_log, al, dt_bias)
    g_r = _hmap(lambda x: jnp.sum(eye * x, axis=0, keepdims=True), g)
    gc = _hmap(lambda x: jnp.sum(jnp.where(causal, x, 0.0), axis=1, keepdims=True), g_r)
    gc_r = _hmap(lambda x: jnp.sum(jnp.where(row <= col, x, 0.0), axis=0, keepdims=True), g)
    decay = _hmap(lambda x, y: jnp.where(causal, jnp.exp(jnp.where(causal, x - y, 0.0)), 0.0), gc, gc_r)
    e_gc = _hmap(jnp.exp, gc)
    kb = _hmap(jnp.multiply, k, beta)
    vb = _hmap(jnp.multiply, v, beta)
    a_mat = _hmap(lambda x, y, d: jnp.where(strict, _dot(x, y, NT) * d, 0.0), kb, k, decay)
    t_inv = _unit_lower_inverse(a_mat) if t_known is None else _known_inverse(a_mat, t_known)
    u_blk = _hmap(lambda t, x: _dot(t, x, NN), t_inv, vb)
    w_blk = _hmap(lambda t, x, e: _dot(t, x * e, NN), t_inv, kb, e_gc)
    v_new = _hmap(lambda u, w, s: u - _dot(w, s, NN), u_blk, w_blk, state)
    attn = _hmap(lambda x, y, d: _dot(x, y, NT) * d, q, k, decay)
    o_state = _hmap(lambda x, e, s: _dot(x * e, s, NN), q, e_gc, state)
    o = _hmap(lambda base, at, vn: base + _dot(at, vn, NN), o_state, attn, v_new)
    g_last = _hmap(lambda x: jnp.sum(x, axis=0, keepdims=True), g)
    k_dec = _hmap(lambda x, gl, c_: x * jnp.exp(gl - c_), k, g_last, gc)
    new_state = _hmap(lambda s, gl, kd, vn: s * jnp.exp(gl) + _dot(kd, vn, TN), state, g_last, k_dec, v_new)
    return (o, new_state, t_inv) if t_known is None else (o, new_state)


def _qkv_specs(q, c, row_block):
    packed = q.shape[1] == 3 * D_MODEL
    return [pl.BlockSpec((c, D_MODEL), lambda i, p=p: (row_block(i), p if packed else 0)) for p in range(3)]


def _gdn_operands(q_ref, k_ref, v_ref, bav, alog_ref, dtb_ref):
    hs = range(GDN_HEADS)
    cols = [slice(h * HEAD_DIM, (h + 1) * HEAD_DIM) for h in hs]
    return ([q_ref[:, sl] for sl in cols], [k_ref[:, sl] for sl in cols], [v_ref[:, sl] for sl in cols],
            [bav[:, h:h + 1] for h in hs], [bav[:, h + GDN_HEADS:h + GDN_HEADS + 1] for h in hs],
            [alog_ref[h:h + 1, 0:1] for h in hs], [dtb_ref[h:h + 1, 0:1] for h in hs])


def _gdn_scan_fwd(q, k, v, ba, a_log, dt_bias, *, name):
    length = q.shape[0]
    n = length // GDN_CHUNK
    c = GDN_CHUNK

    def body(q_ref, k_ref, v_ref, ba_ref, alog_ref, dtb_ref, o_ref, s_ref, t_ref, state):
        i = pl.program_id(0)

        @pl.when(i == 0)
        def _():
            state[...] = jnp.zeros_like(state)

        bav = ba_ref[...]
        heads = [slice(h * HEAD_DIM, (h + 1) * HEAD_DIM) for h in range(GDN_HEADS)]
        s_in = [state[h] for h in range(GDN_HEADS)]
        o, s_out, t_inv = _gdn_chunk(*_gdn_operands(q_ref, k_ref, v_ref, bav, alog_ref, dtb_ref), s_in)
        for h, sl in enumerate(heads):
            s_ref[h] = s_in[h]
            t_ref[h] = t_inv[h]
            o_ref[:, sl] = o[h]
            state[h] = s_out[h]

    row_spec = pl.BlockSpec((c, D_MODEL), lambda i: (i, 0))
    small = pl.BlockSpec((GDN_HEADS, LANES), lambda i: (0, 0))
    return pl.pallas_call(
        body, name=name,
        out_shape=(jax.ShapeDtypeStruct((length, D_MODEL), F32),
                   jax.ShapeDtypeStruct((n, GDN_HEADS, HEAD_DIM, HEAD_DIM), F32),
                   jax.ShapeDtypeStruct((n, GDN_HEADS, c, c), F32)),
        grid=(n,),
        in_specs=_qkv_specs(q, c, lambda i: i) + [pl.BlockSpec((c, LANES), lambda i: (i, 0)), small, small],
        out_specs=(row_spec, pl.BlockSpec((None, GDN_HEADS, HEAD_DIM, HEAD_DIM), lambda i: (i, 0, 0, 0)),
                   pl.BlockSpec((None, GDN_HEADS, c, c), lambda i: (i, 0, 0, 0))),
        scratch_shapes=[pltpu.VMEM((GDN_HEADS, HEAD_DIM, HEAD_DIM), F32)],
        compiler_params=_params(("arbitrary",)),
    )(q, k, v, ba, a_log, dt_bias)


def _gdn_scan_bwd(q, k, v, ba, a_log, dt_bias, states, inverses, do, *, name):
    length = q.shape[0]
    n = length // GDN_CHUNK
    c = GDN_CHUNK

    def body(q_ref, k_ref, v_ref, ba_ref, alog_ref, dtb_ref, s_ref, t_ref, do_ref,
             dq_ref, dk_ref, dv_ref, dba_ref, dalog_ref, ddtb_ref, dstate):
        i = pl.program_id(0)

        @pl.when(i == 0)
        def _():
            dstate[...] = jnp.zeros_like(dstate)
            dalog_ref[...] = jnp.zeros_like(dalog_ref)
            ddtb_ref[...] = jnp.zeros_like(ddtb_ref)

        bav = ba_ref[...]
        lane = lax.broadcasted_iota(jnp.int32, (c, LANES), 1)
        sub8 = lax.broadcasted_iota(jnp.int32, (GDN_HEADS, LANES), 0)
        lane8 = lax.broadcasted_iota(jnp.int32, (GDN_HEADS, LANES), 1)
        slab = jnp.zeros((c, LANES), F32)
        dalog_all = jnp.zeros((GDN_HEADS, LANES), F32)
        ddtb_all = jnp.zeros((GDN_HEADS, LANES), F32)
        heads = [slice(h * HEAD_DIM, (h + 1) * HEAD_DIM) for h in range(GDN_HEADS)]
        ds_in = [dstate[h] for h in range(GDN_HEADS)]
        s_in = [s_ref[h] for h in range(GDN_HEADS)]
        t_known = [t_ref[h] for h in range(GDN_HEADS)]
        _, vjp = jax.vjp(functools.partial(_gdn_chunk, t_known=t_known),
                         *_gdn_operands(q_ref, k_ref, v_ref, bav, alog_ref, dtb_ref), s_in)
        dq, dk, dv, dbl, dal, dalog, ddtb, ds = vjp(([do_ref[:, sl] for sl in heads], ds_in))
        for h, sl in enumerate(heads):
            dq_ref[:, sl] = dq[h]
            dk_ref[:, sl] = dk[h]
            dv_ref[:, sl] = dv[h]
            dstate[h] = ds[h]
            slab = slab + jnp.where(lane == h, dbl[h], 0.0) + jnp.where(lane == h + GDN_HEADS, dal[h], 0.0)
            here = (sub8 == h) & (lane8 == 0)
            dalog_all = dalog_all + jnp.where(here, dalog[h], 0.0)
            ddtb_all = ddtb_all + jnp.where(here, ddtb[h], 0.0)
        dba_ref[...] = slab
        dalog_ref[...] += dalog_all
        ddtb_ref[...] += ddtb_all

    row_spec = pl.BlockSpec((c, D_MODEL), lambda i: (n - 1 - i, 0))
    small = pl.BlockSpec((GDN_HEADS, LANES), lambda i: (0, 0))
    return pl.pallas_call(
        body, name=name,
        out_shape=(jax.ShapeDtypeStruct((length, D_MODEL), F32),) * 3
        + (jax.ShapeDtypeStruct((length, LANES), F32),
           jax.ShapeDtypeStruct((GDN_HEADS, LANES), F32), jax.ShapeDtypeStruct((GDN_HEADS, LANES), F32)),
        grid=(n,),
        in_specs=_qkv_specs(q, c, lambda i: n - 1 - i) + [
                  pl.BlockSpec((c, LANES), lambda i: (n - 1 - i, 0)), small, small,
                  pl.BlockSpec((None, GDN_HEADS, HEAD_DIM, HEAD_DIM), lambda i: (n - 1 - i, 0, 0, 0)),
                  pl.BlockSpec((None, GDN_HEADS, c, c), lambda i: (n - 1 - i, 0, 0, 0)),
                  row_spec],
        out_specs=(row_spec, row_spec, row_spec,
                   pl.BlockSpec((c, LANES), lambda i: (n - 1 - i, 0)), small, small),
        scratch_shapes=[pltpu.VMEM((GDN_HEADS, HEAD_DIM, HEAD_DIM), F32)],
        compiler_params=_params(("arbitrary",)),
    )(q, k, v, ba, a_log, dt_bias, states, inverses, do)


S5_W = S5_T * LANES
S5_S = 2 * 8 * S5_STATE
S5_SH = S5_S // 2


def _iota2(shape):
    return lax.broadcasted_iota(jnp.int32, shape, 0), lax.broadcasted_iota(jnp.int32, shape, 1)


def _s5_rep_t(t, dtype):
    row, col = _iota2((S5_T * S5_GROUP, LANES))
    return ((jnp.right_shift(row, 4) == t) & (jnp.bitwise_and(row, 15) == jnp.bitwise_and(col, 15))).astype(dtype)


def _s5_rep_state(dtype):
    row, col = _iota2((2 * S5_STATE, S5_S))
    return ((jnp.right_shift(row, 6) == jnp.right_shift(col, 9))
            & (jnp.bitwise_and(row, 63) == jnp.bitwise_and(col, 63))).astype(dtype)


def _s5_masks():
    row, col = _iota2((LANES, LANES))
    m_ab = jnp.right_shift(row, 4) == jnp.right_shift(col, 4)
    row, col = _iota2((S5_S, LANES))
    m_e = jnp.bitwise_and(jnp.right_shift(row, 6), 7) == jnp.right_shift(col, 4)
    row, col = _iota2((LANES, S5_S))
    m_f = jnp.right_shift(row, 4) == jnp.bitwise_and(jnp.right_shift(col, 6), 7)
    return m_ab, m_e, m_f


def _s5_expand(kx_ref, ec_ref, fc_ref, kb_scr, e_scr, f_scr):
    m_ab, m_e, m_f = _s5_masks()
    kx = kx_ref[...].astype(BF16)
    ec = ec_ref[...].astype(BF16)
    rep_state = _s5_rep_state(BF16)
    for t in range(S5_T):
        rep = _s5_rep_t(t, BF16)
        cols = slice(t * LANES, (t + 1) * LANES)
        kb_scr[t] = jnp.where(m_ab, jnp.dot(kx, rep, preferred_element_type=F32), 0.0).astype(BF16)
        e_scr[:, cols] = jnp.where(m_e, jnp.dot(ec, rep, preferred_element_type=F32), 0.0).astype(BF16)
        f_scr[cols, :] = jnp.where(m_f, jnp.dot(fc_ref[t].astype(BF16), rep_state, preferred_element_type=F32),
                                   0.0).astype(BF16)


def _s5_token_rows(ref, n):
    return [ref[pl.ds(t, n, stride=S5_T), :].astype(BF16) for t in range(S5_T)]


def _s5_scan_fwd(u, kx, ec, fc, at, *, name):
    length = u.shape[0]
    n = length // S5_T
    assert n % SUBLANES == 0

    def body(u_ref, kx_ref, ec_ref, fc_ref, at_ref, y_ref, h_ref, kb_scr, e_scr, f_scr, g_scr):
        _s5_expand(kx_ref, ec_ref, fc_ref, kb_scr, e_scr, f_scr)
        us = _s5_token_rows(u_ref, n)
        g_scr[...] = jnp.dot(jnp.concatenate(us, axis=1), f_scr[...], preferred_element_type=F32)
        ar, ai = at_ref[:, :S5_SH], at_ref[:, S5_SH:]

        def step(blk, h):
            base = pl.multiple_of(blk * SUBLANES, SUBLANES)
            g8 = g_scr[pl.ds(base, SUBLANES), :]
            rows = []
            for r in range(SUBLANES):
                rows.append(h)
                hr, hi = h[:, :S5_SH], h[:, S5_SH:]
                h = jnp.concatenate([ar * hr - ai * hi, ar * hi + ai * hr], axis=1) + g8[r:r + 1, :]
            h_ref[pl.ds(base, SUBLANES), :] = jnp.concatenate(rows, axis=0)
            return h

        lax.fori_loop(0, n // SUBLANES, step, jnp.zeros((1, S5_S), F32))
        hb = h_ref[...].astype(BF16)
        for t in range(S5_T):
            acc = jnp.dot(hb, e_scr[:, t * LANES:(t + 1) * LANES], preferred_element_type=F32)
            for s in range(t + 1):
                acc = acc + jnp.dot(us[s], kb_scr[t - s], preferred_element_type=F32)
            y_ref[pl.ds(t, n, stride=S5_T), :] = acc

    return pl.pallas_call(
        body, name=name,
        out_shape=(jax.ShapeDtypeStruct((length, D_MODEL), F32), jax.ShapeDtypeStruct((S5_TILES, n, S5_S), F32)),
        grid=(S5_TILES,),
        in_specs=[pl.BlockSpec((length, LANES), lambda k: (0, k)), _s5_spec(LANES, S5_T * S5_GROUP),
                  _s5_spec(S5_S, S5_T * S5_GROUP), _s5_spec(S5_T, LANES, LANES), _s5_spec(1, S5_S)],
        out_specs=(pl.BlockSpec((length, LANES), lambda k: (0, k)), _s5_spec(n, S5_S)),
        scratch_shapes=[pltpu.VMEM((S5_T, LANES, LANES), BF16), pltpu.VMEM((S5_S, S5_W), BF16),
                        pltpu.VMEM((S5_W, S5_S), BF16), pltpu.VMEM((n, S5_S), F32)],
        compiler_params=_params(("parallel",)),
    )(u, kx, ec, fc, at)


def _s5_spec(*tail):
    return pl.BlockSpec((None,) + tail, lambda k: (k,) + (0,) * len(tail))


def _s5_scan_bwd(dy, kx, ec, fc, at, hs, *, name):
    length = dy.shape[0]
    n = length // S5_T

    def body(dy_ref, kx_ref, ec_ref, fc_ref, at_ref, h_ref, du_ref, dg_ref, dat_ref, kb_scr, e_scr, f_scr, dh_scr):
        _s5_expand(kx_ref, ec_ref, fc_ref, kb_scr, e_scr, f_scr)
        dys = _s5_token_rows(dy_ref, n)
        dh_scr[...] = _dot(jnp.concatenate(dys, axis=1), e_scr[...], NT)
        ar, ai = at_ref[:, :S5_SH], at_ref[:, S5_SH:]

        def step(it, carry):
            cy, dat = carry
            base = pl.multiple_of((n // SUBLANES - 1 - it) * SUBLANES, SUBLANES)
            dh8 = dh_scr[pl.ds(base, SUBLANES), :]
            h8 = h_ref[pl.ds(base, SUBLANES), :]
            rows = [None] * SUBLANES
            for r in reversed(range(SUBLANES)):
                rows[r] = cy
                cr, ci = cy[:, :S5_SH], cy[:, S5_SH:]
                hr, hi = h8[r:r + 1, :S5_SH], h8[r:r + 1, S5_SH:]
                dat = dat + jnp.concatenate([cr * hr + ci * hi, ci * hr - cr * hi], axis=1)
                cy = dh8[r:r + 1, :] + jnp.concatenate([ar * cr + ai * ci, ar * ci - ai * cr], axis=1)
            dg_ref[pl.ds(base, SUBLANES), :] = jnp.concatenate(rows, axis=0)
            return cy, dat

        zero = jnp.zeros((1, S5_S), F32)
        _, dat = lax.fori_loop(0, n // SUBLANES, step, (zero, zero))
        dat_ref[...] = dat
        dgb = dg_ref[...].astype(BF16)
        for s in range(S5_T):
            acc = _dot(dgb, f_scr[s * LANES:(s + 1) * LANES, :], NT)
            for t in range(s, S5_T):
                acc = acc + _dot(dys[t], kb_scr[t - s], NT)
            du_ref[pl.ds(s, n, stride=S5_T), :] = acc

    row_spec = pl.BlockSpec((length, LANES), lambda k: (0, k))
    return pl.pallas_call(
        body, name=name,
        out_shape=(jax.ShapeDtypeStruct((length, D_MODEL), F32), jax.ShapeDtypeStruct((S5_TILES, n, S5_S), F32),
                   jax.ShapeDtypeStruct((S5_TILES, 1, S5_S), F32)),
        grid=(S5_TILES,),
        in_specs=[row_spec, _s5_spec(LANES, S5_T * S5_GROUP), _s5_spec(S5_S, S5_T * S5_GROUP),
                  _s5_spec(S5_T, LANES, LANES), _s5_spec(1, S5_S), _s5_spec(n, S5_S)],
        out_specs=(row_spec, _s5_spec(n, S5_S), _s5_spec(1, S5_S)),
        scratch_shapes=[pltpu.VMEM((S5_T, LANES, LANES), BF16), pltpu.VMEM((S5_S, S5_W), BF16),
                        pltpu.VMEM((S5_W, S5_S), BF16), pltpu.VMEM((n, S5_S), F32)],
        compiler_params=_params(("parallel",)),
    )(dy, kx, ec, fc, at, hs)


def _s5_operator_grads(dy, u, hs, dg, *, name):
    length = u.shape[0]
    n = length // S5_T

    def body(dy_ref, u_ref, h_ref, dg_ref, dkx_ref, dec_ref, dfc_ref):
        dys = _s5_token_rows(dy_ref, n)
        us = _s5_token_rows(u_ref, n)
        ucat = jnp.concatenate(us, axis=1)
        m_ab, m_e, m_f = _s5_masks()
        hb = h_ref[...].astype(BF16)
        dgb = dg_ref[...].astype(BF16)
        lane = lax.broadcasted_iota(jnp.int32, (1, LANES), 1)
        lane_group = jnp.right_shift(lane, 4)

        def own_block(x, mask):
            x = jnp.where(mask, x, 0.0)
            for shift in (64, 32, 16):
                x = x + pltpu.roll(x, shift, 1)
            return x

        def place(halves, t, x):
            halves[t // 8] = jnp.where(lane_group == t % 8, x, halves[t // 8])

        dkb = [jnp.zeros((LANES, LANES), F32) for _ in range(S5_T)]
        dec = [jnp.zeros((S5_S, LANES), F32) for _ in range(2)]
        for t in range(S5_T):
            d_t = _dot(ucat, dys[t], TN)
            for s in range(t + 1):
                dkb[t - s] = dkb[t - s] + d_t[s * LANES:(s + 1) * LANES, :]
            place(dec, t, own_block(_dot(hb, dys[t], TN), m_e))
            wide = jnp.where(m_f, _dot(us[t], dgb, TN), 0.0)
            parts = []
            for r in range(2):
                acc = wide[:, r * S5_SH:r * S5_SH + LANES]
                for q in range(1, S5_SH // LANES):
                    acc = acc + wide[:, r * S5_SH + q * LANES:r * S5_SH + (q + 1) * LANES]
                parts.append(acc + pltpu.roll(acc, S5_STATE, 1))
            dfc_ref[t] = jnp.where(lane < S5_STATE, parts[0], parts[1])
        dkx = [jnp.zeros((LANES, LANES), F32) for _ in range(2)]
        for t in range(S5_T):
            place(dkx, t, own_block(dkb[t], m_ab))
        dkx_ref[...] = jnp.concatenate(dkx, axis=1)
        dec_ref[...] = jnp.concatenate(dec, axis=1)

    row_spec = pl.BlockSpec((length, LANES), lambda k: (0, k))
    outs = (_s5_spec(LANES, S5_T * S5_GROUP), _s5_spec(S5_S, S5_T * S5_GROUP), _s5_spec(S5_T, LANES, LANES))
    return pl.pallas_call(
        body, name=name,
        out_shape=(jax.ShapeDtypeStruct((S5_TILES, LANES, S5_T * S5_GROUP), F32),
                   jax.ShapeDtypeStruct((S5_TILES, S5_S, S5_T * S5_GROUP), F32),
                   jax.ShapeDtypeStruct((S5_TILES, S5_T, LANES, LANES), F32)),
        grid=(S5_TILES,),
        in_specs=[row_spec, row_spec, _s5_spec(n, S5_S), _s5_spec(n, S5_S)],
        out_specs=outs,
        compiler_params=_params(("parallel",)),
    )(dy, u, hs, dg)


def _s5_prep(a_re, a_im, b_re, b_im, c_re, c_im, log_dt):
    t_len, tiles = S5_T, S5_TILES
    dt = jnp.exp(log_dt)[:, None]
    mag = jnp.exp(a_re * dt)
    ab_re, ab_im = mag * jnp.cos(a_im * dt), mag * jnp.sin(a_im * dt)
    den = jnp.square(a_re) + jnp.square(a_im)
    n_re, n_im = ab_re - 1.0, ab_im
    f_re = (n_re * a_re + n_im * a_im) / den
    f_im = (n_im * a_re - n_re * a_im) / den
    bb_re = f_re[..., None] * b_re - f_im[..., None] * b_im
    bb_im = f_re[..., None] * b_im + f_im[..., None] * b_re

    def powers(exponents):
        e = exponents[:, None, None]
        m = jnp.exp(e * (a_re * dt))
        return m * jnp.cos(e * (a_im * dt)), m * jnp.sin(e * (a_im * dt))

    p_re, p_im = powers(jnp.arange(t_len + 1, dtype=F32))
    rev_re, rev_im = powers((t_len - 1) - jnp.arange(t_len, dtype=F32))
    ca_re = c_re[None] * p_re[:, :, None, :] - c_im[None] * p_im[:, :, None, :]
    ca_im = c_re[None] * p_im[:, :, None, :] + c_im[None] * p_re[:, :, None, :]
    lag = (jnp.einsum('tgip,gpj->tgij', ca_re[:t_len], bb_re, precision=HI)
           - jnp.einsum('tgip,gpj->tgij', ca_im[:t_len], bb_im, precision=HI))
    kx = lag.reshape(t_len, tiles, 8, S5_GROUP, S5_GROUP).transpose(1, 2, 4, 0, 3)
    kx = kx.reshape(tiles, LANES, t_len * S5_GROUP)
    e_st = jnp.stack([ca_re[1:], -ca_im[1:]])
    e_st = e_st.reshape(2, t_len, tiles, 8, S5_GROUP, S5_STATE).transpose(2, 0, 3, 5, 1, 4)
    ec = e_st.reshape(tiles, S5_S, t_len * S5_GROUP)
    ab_b = jnp.stack([rev_re[..., None] * bb_re[None] - rev_im[..., None] * bb_im[None],
                      rev_re[..., None] * bb_im[None] + rev_im[..., None] * bb_re[None]])
    ab_b = ab_b.reshape(2, t_len, tiles, 8, S5_STATE, S5_GROUP).transpose(2, 1, 3, 5, 0, 4)
    fc = ab_b.reshape(tiles, t_len, LANES, 2 * S5_STATE)
    a_t = jnp.stack([p_re[t_len], p_im[t_len]]).reshape(2, tiles, 8 * S5_STATE).transpose(1, 0, 2)
    return kx, ec, fc, a_t.reshape(tiles, 1, S5_S)


TM_ROW = 1024
TM_WIDE = 512


def _gdn_fwd(x, w):
    qkv = _mm(x, w["wqkv"], name="gdn_proj_qkv")
    z = _mm(x, w["wz"], name="gdn_proj_z")
    ba = _mm(x, w["wba"], name="gdn_proj_ba")
    cv, act = _conv_fwd(qkv, w["conv_w"], tm=TM_ROW, name="gdn_conv_qkv")
    q = k = v = act
    o, states, inverses = _gdn_scan_fwd(q, k, v, ba, w["a_log8"], w["dt_bias8"], name="gdn_scan")
    (mix,) = _rw_fwd(_f_gdn_out, [o, z], [w["norm_g"]], tm=TM_ROW, name="gdn_out", out_dtypes=[BF16])
    return mix, (qkv, z, ba, cv, q, k, v, states, inverses, o)


def _gdn_bwd(x, w, saved, dmix, dx_acc, token=None):
    qkv, z, ba, cv, q, k, v, states, inverses, o = saved
    norm_g = w["norm_g"] if token is None else w["norm_g"] + token[0, 0]
    (do, dz), (dnorm_g,) = _rw_bwd(_f_gdn_out, [o, z], [norm_g], [dmix], row_grad=[1, 1], param_grad=[1],
                                   tm=TM_ROW, name="gdn_out_bwd", row_dtypes=[F32, BF16])
    dq, dk, dv, dba, dalog, ddtb = _gdn_scan_bwd(q, k, v, ba, w["a_log8"], w["dt_bias8"], states, inverses, do,
                                                  name="gdn_scan_bwd")
    (dcv,), _ = _rw_bwd(_f_gdn_qkv, [cv], [], [dq, dk, dv], row_grad=[1], param_grad=[], tm=TM_WIDE,
                        name="gdn_qkv_bwd")
    dqkv, dconv_w = _conv_bwd(qkv, w["conv_w"], dcv, tm=TM_ROW, name="gdn_conv_bwd")
    dx = _mm(dqkv, w["wqkv"], tb=True, acc=dx_acc, name="gdn_dx_qkv")
    dx = _mm(dz, w["wz"], tb=True, acc=dx, name="gdn_dx_z")
    dx = _mm(dba, w["wba"], tb=True, acc=dx, name="gdn_dx_ba")
    grads = dict(wqkv=_mm(x, dqkv, ta=True, name="gdn_dw_qkv"), wz=_mm(x, dz, ta=True, name="gdn_dw_z"),
                 wba=_mm(x, dba, ta=True, name="gdn_dw_ba"), conv_w=dconv_w,
                 a_log=dalog[:, 0], dt_bias=ddtb[:, 0], norm_g=dnorm_g[0])
    return dx, grads


def _s5_fwd(x, w):
    u = _mm(x, w["wu"], name="s5_proj_u")
    y, hs = _s5_scan_fwd(u, w["kx"], w["ec"], w["fc"], w["a_t"], name="s5_scan")
    (zg,) = _rw_fwd(_f_s5_gelu, [y, u], [w["d"]], tm=TM_ROW, name="s5_gelu")
    t = _mm(zg, w["w_glu"], name="s5_glu")
    (mix,) = _rw_fwd(_f_s5_gate, [zg, t], [w["b_glu"]], tm=TM_ROW, name="s5_gate", out_dtypes=[BF16])
    return mix, (u, hs, y, zg, t)


def _s5_bwd(x, w, saved, dmix, dx_acc, token=None):
    u, hs, y, zg, t = saved
    b_glu = w["b_glu"] if token is None else w["b_glu"] + token[0, 0]
    (dzg, dt), (db_glu,) = _rw_bwd(_f_s5_gate, [zg, t], [b_glu], [dmix], row_grad=[1, 1], param_grad=[1],
                                   tm=TM_ROW, name="s5_gate_bwd", row_dtypes=[F32, BF16])
    dzg = _mm(dt, w["w_glu"], tb=True, acc=dzg, name="s5_dzg")
    dw_glu = _mm(zg, dt, ta=True, name="s5_dw_glu")
    (dy, du), (dd,) = _rw_bwd(_f_s5_gelu, [y, u], [w["d"]], [dzg], row_grad=[1, 1], param_grad=[1],
                              tm=TM_ROW, name="s5_gelu_bwd")
    du_scan, dg, dat = _s5_scan_bwd(dy, w["kx"], w["ec"], w["fc"], w["a_t"], hs, name="s5_scan_bwd")
    dkx, dec, dfc = _s5_operator_grads(dy, u, hs, dg, name="s5_operator_grads")
    (du,) = _rw_fwd(_f_add, [du, du_scan], [], tm=TM_ROW, name="s5_du_add", out_dtypes=[BF16])
    d_a_re, d_a_im, d_b_re, d_b_im, d_c_re, d_c_im, d_log_dt = w["prep_vjp"]((dkx, dec, dfc, dat))
    dx = _mm(du, w["wu"], tb=True, acc=dx_acc, name="s5_dx_u")
    grads = dict(wu=_mm(x, du, ta=True, name="s5_dw_u"), w_glu=dw_glu, b_glu=db_glu[0], d=dd[0],
                 a_re=d_a_re, a_im=d_a_im, b_re=d_b_re, b_im=d_b_im, c_re=d_c_re, c_im=d_c_im, log_dt=d_log_dt)
    return dx, grads


def _ln_res_both(x, h, g, b):
    (y,) = _f_ln_res(x, h, g, b)
    return y, y


def _layer_fwd(x, xb, mem, w, is_gdn):
    mix, msave = (_gdn_fwd if is_gdn else _s5_fwd)(xb, w)
    xq = _mm(xb, w["wxq"], name="proj_xq")
    kv = _mm(mem, w["wkv"], name="mem_kv")
    kmem, vmem = kv[:, :XA_DIM], kv[:, XA_DIM:]
    (cross,) = _rw_fwd(_f_attn, [xq], [kmem, vmem], tm=TM_ROW, name="attn", out_dtypes=[BF16])
    h = _mm(mix, w["wo_mix"], name="wo_mix")
    h = _mm(cross, w["wo_cross"], acc=h, name="wo_cross")
    x1, x1b = _rw_fwd(_ln_res_both, [x, h], [w["ln1_g"], w["ln1_b"]], tm=TM_ROW, name="ln_res",
                      out_dtypes=[F32, BF16])
    hm, act = _mm_relu2(x1b, w["w1"], name="mlp_up")
    f = _mm(act, w["w2"], name="mlp_down")
    x2, x2b = _rw_fwd(_ln_res_both, [x1, f], [w["ln2_g"], w["ln2_b"]], tm=TM_ROW, name="ln_res",
                      out_dtypes=[F32, BF16])
    return x2, x2b, (x, xb, msave, xq, kmem, vmem, mix, cross, h, x1, x1b, hm, act, f)


def _layer_bwd(mem, w, is_gdn, saved, dx2, token=None, before_mixer=None):
    x, xb, msave, xq, kmem, vmem, mix, cross, h, x1, x1b, hm, act, f = saved
    ln2_g = w["ln2_g"] if token is None else w["ln2_g"] + token[0, 0]
    (dx1, df), (dg2, db2) = _rw_bwd(_f_ln_res, [x1, f], [ln2_g, w["ln2_b"]], [dx2], row_grad=[1, 1],
                                    param_grad=[1, 1], tm=TM_ROW, name="ln_res_bwd", row_dtypes=[F32, BF16])
    dhm = _mm_relu2_grad(df, w["w2"], hm, name="mlp_dhm")
    dw2 = _mm(act, df, ta=True, name="mlp_dw2")
    dx1 = _mm(dhm, w["w1"], tb=True, acc=dx1, name="mlp_dx")
    dw1 = _mm(x1b, dhm, ta=True, out_blocks=N_CHIPS, name="mlp_dw1")
    (dx, dh), (dg1, db1) = _rw_bwd(_f_ln_res, [x, h], [w["ln1_g"], w["ln1_b"]], [dx1], row_grad=[1, 1],
                                   param_grad=[1, 1], tm=TM_ROW, name="ln_res_bwd", row_dtypes=[F32, BF16])
    dmix =_mm(dh, w["wo_mix"], tb=True, name="wo_dmix")
    dcross = _mm(dh, w["wo_cross"], tb=True, name="wo_dcross")
    dwo = jnp.concatenate([_mm(mix, dh, ta=True, name="wo_dw_mix"), _mm(cross, dh, ta=True, name="wo_dw_cross")], 0)
    (dxq,), (dkmem, dvmem) = _rw_bwd(_f_attn, [xq], [kmem, vmem], [dcross], row_grad=[1], param_grad=[1, 1],
                                     tm=TM_ROW, name="attn_bwd", row_dtypes=[BF16])
    dwkv = _mm(mem, jnp.concatenate([dkmem, dvmem], axis=1), ta=True, name="mem_dw_kv")
    dx = _mm(dxq, w["wxq"], tb=True, acc=dx, name="dx_xq")
    dwxq = _mm(xb, dxq, ta=True, name="dw_xq")
    mixer_token = None if before_mixer is None else before_mixer(dict(w_kv_mem=dwkv, w_o=dwo, mlp_w1=dw1, mlp_w2=dw2))
    dx, mg = (_gdn_bwd if is_gdn else _s5_bwd)(xb, w, msave, dmix, dx, mixer_token)
    grads = dict(mixer=mg, wxq=dwxq, wkv=dwkv, wo=dwo, w1=dw1, w2=dw2,
                 ln1_g=dg1[0], ln1_b=db1[0], ln2_g=dg2[0], ln2_b=db2[0])
    return dx, grads


def _loss_and_grad(y, target):
    def f(yv, tv):
        err = yv - tv
        return (err * (1.0 / D_MODEL),), (0.5 / D_MODEL * jnp.sum(err * err, axis=0, keepdims=True),)

    (dy,), (part,) = _rowwise(f, [y, target], [], [(D_MODEL, F32)], [((1, D_MODEL), F32)], tm=TM_ROW, name="loss")
    return jnp.sum(part), dy


def _layer_weights(full, i):
    j = i // 2
    w = dict(wkv=full["w_kv_mem"][i].astype(BF16),
             wo_mix=full["w_o"][i][:D_MODEL].astype(BF16), wo_cross=full["w_o"][i][D_MODEL:].astype(BF16),
             ln1_g=full["ln1_g"][i][None], ln1_b=full["ln1_b"][i][None],
             ln2_g=full["ln2_g"][i][None], ln2_b=full["ln2_b"][i][None],
             w1=full["mlp_w1"][i].astype(BF16), w2=full["mlp_w2"][i].astype(BF16))
    if i % 2 == 0:
        w_in = full["gdn_w_in"][j]
        gd = 3 * D_MODEL
        w.update(wqkv=w_in[:, :gd].astype(BF16), wz=w_in[:, gd:gd + D_MODEL].astype(BF16),
                 wba=jnp.pad(w_in[:, gd + D_MODEL:gd + D_MODEL + 2 * GDN_HEADS],
                             ((0, 0), (0, LANES - 2 * GDN_HEADS))).astype(BF16),
                 wxq=w_in[:, gd + D_MODEL + 2 * GDN_HEADS:].astype(BF16),
                 conv_w=full["gdn_conv_w"][j],
                 a_log8=jnp.broadcast_to(full["gdn_a_log"][j][:, None], (GDN_HEADS, LANES)),
                 dt_bias8=jnp.broadcast_to(full["gdn_dt_bias"][j][:, None], (GDN_HEADS, LANES)),
                 norm_g=full["gdn_norm_g"][j][None])
    else:
        w_in = full["s5_w_in"][j]
        (kx, ec, fc, a_t), prep_vjp = full["s5_prep"][j] if "s5_prep" in full else jax.vjp(
            _s5_prep, full["s5_a_re"][j], full["s5_a_im"][j], full["s5_b_re"][j], full["s5_b_im"][j],
            full["s5_c_re"][j], full["s5_c_im"][j], full["s5_log_dt"][j])
        w.update(wu=w_in[:, :D_MODEL].astype(BF16), wxq=w_in[:, D_MODEL:].astype(BF16),
                 kx=kx, ec=ec, fc=fc, a_t=a_t, prep_vjp=prep_vjp,
                 d=full["s5_d"][j][None], w_glu=full["s5_w_glu"][j].astype(BF16), b_glu=full["s5_b_glu"][j][None])
    return w


def _sharded_grads(l, i):
    m = l["mixer"]
    out = dict(w_kv_mem=l["wkv"], w_o=l["wo"], mlp_w1=l["w1"], mlp_w2=l["w2"])
    if i % 2 == 0:
        out.update(gdn_w_in=jnp.concatenate([m["wqkv"], m["wz"], m["wba"][:, :2 * GDN_HEADS], l["wxq"]], axis=1),
                   gdn_conv_w=m["conv_w"])
    else:
        out.update(s5_w_in=jnp.concatenate([m["wu"], l["wxq"]], axis=1), s5_d=m["d"], s5_w_glu=m["w_glu"],
                   s5_b_glu=m["b_glu"])
    return out


def _replicated_grads(layer_grads):
    g = layer_grads
    gdn = [g[i]["mixer"] for i in range(DEPTH) if i % 2 == 0]
    s5 = [g[i]["mixer"] for i in range(DEPTH) if i % 2 == 1]
    out = {n: jnp.stack([l[n] for l in g]) for n in ("ln1_g", "ln1_b", "ln2_g", "ln2_b")}
    out.update({"gdn_" + n: jnp.stack([m[n] for m in gdn]) for n in ("a_log", "dt_bias", "norm_g")})
    out.update({"s5_" + n: jnp.stack([m[n] for m in s5])
                for n in ("a_re", "a_im", "b_re", "b_im", "c_re", "c_im", "log_dt")})
    return out


def _local_step(x, mem, target, weights_of, grads_ready, before_first_mixer):
    lw, saves = [], []
    h, hb = x, x.astype(BF16)
    for i in range(DEPTH):
        lw.append(weights_of(i, h))
        h, hb, s = _layer_fwd(h, hb, mem, lw[i], i % 2 == 0)
        saves.append(s)
    loss, d = _loss_and_grad(h, target)
    grads = [None] * DEPTH
    token = None
    for i in reversed(range(DEPTH)):
        d, grads[i] = _layer_bwd(mem, lw[i], i % 2 == 0, saves[i], d, token, None if i else before_first_mixer)
        token = grads_ready(i, grads[i])
    return loss, d, grads


ANY = pl.BlockSpec(memory_space=pl.ANY)
SHARD_ROWS = 1024
SMALL_ROWS = 128


def _place():
    return lax.axis_index("x"), lax.axis_index("y"), lax.axis_index("c")


def _other_chips(x, y):
    return [(1 - x, y), (x, 1 - y), (1 - x, 1 - y)]


HBM = pl.BlockSpec(memory_space=pltpu.HBM)
SEM = pl.BlockSpec(memory_space=pltpu.SEMAPHORE)
DATAFLOW = pltpu.SideEffectType.DATAFLOW_SIDE_EFFECTING


def _gather_ici_copies(w_ref, land_ref, send_sems, recv_sems, outgoing):
    x, y, c = _place()
    half = w_ref.shape[0] // 2
    mine = pl.ds(c * half, half)
    return [pltpu.make_async_remote_copy(
        src_ref=w_ref.at[mine, :], dst_ref=land_ref.at[2 * x + y if outgoing else 2 * cx + cy, mine, :],
        send_sem=send_sems.at[j], recv_sem=recv_sems.at[j], device_id=(cx, cy, c), device_id_type=MESH)
        for j, (cx, cy) in enumerate(_other_chips(x, y))]


def _gather_start(wpack, after, *, name):
    rows = wpack.shape[0]

    def body(w_ref, land_ref, after_ref, send_sems, recv_sems, w_thru, land_thru, token):
        for cp in _gather_ici_copies(w_ref, land_ref, send_sems, recv_sems, outgoing=True):
            cp.start()
        token[...] = jnp.zeros_like(token)

    land = pltpu.with_memory_space_constraint(lax.empty((N_CHIPS, rows, D_MODEL), wpack.dtype), pltpu.HBM)
    return pl.pallas_call(
        body, name=name,
        out_shape=(pltpu.SemaphoreType.DMA((3,)), pltpu.SemaphoreType.DMA((3,)), pltpu.HBM(wpack.shape, wpack.dtype),
                   pltpu.HBM(land.shape, land.dtype), jax.ShapeDtypeStruct((SUBLANES, LANES), F32)),
        in_specs=(HBM, HBM, ANY), out_specs=(SEM, SEM, HBM, HBM, pl.BlockSpec(memory_space=pltpu.VMEM)),
        input_output_aliases={0: 2, 1: 3},
        compiler_params=pltpu.CompilerParams(has_side_effects=DATAFLOW),
    )(pltpu.with_memory_space_constraint(wpack, pltpu.HBM), land, after)


def _gather_wait(send_sems, recv_sems, w_thru, land_thru, after, *, name):
    def body(w_ref, land_ref, send_sems, recv_sems, after_ref, w_dead, land_out):
        for cp in _gather_ici_copies(w_ref, land_ref, send_sems, recv_sems, outgoing=False):
            cp.wait_send()
            cp.wait_recv()

    return pl.pallas_call(
        body, name=name,
        out_shape=(pltpu.HBM(w_thru.shape, w_thru.dtype), pltpu.HBM(land_thru.shape, land_thru.dtype)),
        in_specs=(HBM, HBM, SEM, SEM, ANY), out_specs=(HBM, HBM), input_output_aliases={0: 0, 1: 1},
        compiler_params=pltpu.CompilerParams(has_side_effects=DATAFLOW),
    )(w_thru, land_thru, send_sems, recv_sems, after)[1]


def _gather_forward(land, *, name):
    rows = land.shape[1]
    half = rows // 2

    def body(in_ref, out_ref, send_sems, recv_sems):
        x, y, c = _place()

        def copy(j, cx, cy, cc):
            rows_of = out_ref.at[2 * cx + cy, pl.ds(cc * half, half), :]
            return pltpu.make_async_remote_copy(src_ref=rows_of, dst_ref=rows_of, send_sem=send_sems.at[j],
                                                recv_sem=recv_sems.at[j], device_id=(x, y, 1 - c), device_id_type=MESH)

        sends = [copy(j, cx, cy, c) for j, (cx, cy) in enumerate(_other_chips(x, y))]
        for cp in sends:
            cp.start()
        for j, (cx, cy) in enumerate(_other_chips(x, y)):
            copy(j, cx, cy, 1 - c).wait_recv()
        for cp in sends:
            cp.wait_send()

    return pl.pallas_call(
        body, name=name, out_shape=jax.ShapeDtypeStruct(land.shape, land.dtype), in_specs=[ANY], out_specs=ANY,
        input_output_aliases={0: 0},
        scratch_shapes=[pltpu.SemaphoreType.DMA((3,)), pltpu.SemaphoreType.DMA((3,))],
    )(land)


def _sibling_swap(buf, *, name):
    def body(in_ref, out_ref, send_sem, recv_sem):
        x, y, c = _place()
        cp = pltpu.make_async_remote_copy(src_ref=in_ref, dst_ref=out_ref, send_sem=send_sem, recv_sem=recv_sem,
                                          device_id=(x, y, 1 - c), device_id_type=MESH)
        cp.start()
        cp.wait()

    return pl.pallas_call(
        body, name=name, out_shape=jax.ShapeDtypeStruct(buf.shape, buf.dtype), in_specs=[ANY], out_specs=ANY,
        scratch_shapes=[pltpu.SemaphoreType.DMA, pltpu.SemaphoreType.DMA],
    )(buf)


def _pair_exchange(gpack, *, name):
    pieces, rows, width = gpack.shape
    half = rows // 2

    def body(in_ref, got_ref, send_sems, recv_sems):
        x, y, c = _place()
        sends = [pltpu.make_async_remote_copy(src_ref=in_ref.at[p, pl.ds((1 - c) * half, half), :],
                                              dst_ref=got_ref.at[p], send_sem=send_sems.at[p],
                                              recv_sem=recv_sems.at[p], device_id=(x, y, 1 - c), device_id_type=MESH)
                 for p in range(pieces)]
        for cp in sends:
            cp.start()
        for cp in sends:
            cp.wait()

    return pl.pallas_call(
        body, name=name, out_shape=jax.ShapeDtypeStruct((pieces, half, width), gpack.dtype),
        in_specs=[ANY], out_specs=ANY,
        scratch_shapes=[pltpu.SemaphoreType.DMA((pieces,)), pltpu.SemaphoreType.DMA((pieces,))],
    )(gpack)


def _pair_add(gpack, got, c, *, name, tm=512):
    pieces, rows, width = gpack.shape
    half = rows // 2
    nb = half // tm

    def body(c_ref, a_ref, b_ref, sum_ref, narrow_ref):
        s = a_ref[...] + b_ref[...]
        sum_ref[...] = s
        narrow_ref[...] = s.astype(BF16)

    blk = pl.BlockSpec((None, tm, width), lambda p, i, c_ref: (p, i, 0))
    return pl.pallas_call(
        body, name=name,
        out_shape=(jax.ShapeDtypeStruct((pieces, half, width), F32), jax.ShapeDtypeStruct((pieces, half, width), BF16)),
        grid_spec=pltpu.PrefetchScalarGridSpec(
            num_scalar_prefetch=1, grid=(pieces, nb),
            in_specs=[pl.BlockSpec((None, tm, width), lambda p, i, c_ref: (p, c_ref[0] * nb + i, 0)), blk],
            out_specs=(blk, blk)),
        compiler_params=_params(("parallel", "parallel")),
    )(c, gpack, got)


def _chip_exchange(pieces, *, name):
    _, rows, width = pieces.shape

    def body(in_ref, out_ref, send_sems, recv_sems):
        x, y, c = _place()
        cps = [pltpu.make_async_remote_copy(src_ref=in_ref.at[2 * cx + cy], dst_ref=out_ref.at[j],
                                            send_sem=send_sems.at[j], recv_sem=recv_sems.at[j],
                                            device_id=(cx, cy, c), device_id_type=MESH)
               for j, (cx, cy) in enumerate(_other_chips(x, y))]
        for cp in cps:
            cp.start()
        for cp in cps:
            cp.wait()

    return pl.pallas_call(
        body, name=name, out_shape=jax.ShapeDtypeStruct((3, rows, width), pieces.dtype), in_specs=[ANY], out_specs=ANY,
        scratch_shapes=[pltpu.SemaphoreType.DMA((3,)), pltpu.SemaphoreType.DMA((3,))],
    )(pieces)


def _chip_exchange_copies(in_ref, land_ref, send_sems, recv_sems):
    x, y, c = _place()
    return [pltpu.make_async_remote_copy(src_ref=in_ref.at[2 * cx + cy], dst_ref=land_ref.at[j],
                                         send_sem=send_sems.at[j], recv_sem=recv_sems.at[j],
                                         device_id=(cx, cy, c), device_id_type=MESH)
            for j, (cx, cy) in enumerate(_other_chips(x, y))]


def _chip_exchange_start(pieces):
    _, rows, width = pieces.shape

    def body(in_ref, land_ref, send_sems, recv_sems, in_thru, land_thru, token):
        for cp in _chip_exchange_copies(in_ref, land_ref, send_sems, recv_sems):
            cp.start()
        token[...] = jnp.zeros_like(token)

    land = pltpu.with_memory_space_constraint(lax.empty((3, rows, width), pieces.dtype), pltpu.HBM)
    return pl.pallas_call(
        body, name="rs_chip_start",
        out_shape=(pltpu.SemaphoreType.DMA((3,)), pltpu.SemaphoreType.DMA((3,)), pltpu.HBM(pieces.shape, pieces.dtype),
                   pltpu.HBM(land.shape, land.dtype), jax.ShapeDtypeStruct((SUBLANES, LANES), F32)),
        in_specs=(HBM, HBM), out_specs=(SEM, SEM, HBM, HBM, pl.BlockSpec(memory_space=pltpu.VMEM)),
        input_output_aliases={0: 2, 1: 3},
        compiler_params=pltpu.CompilerParams(has_side_effects=DATAFLOW),
    )(pltpu.with_memory_space_constraint(pieces, pltpu.HBM), land)


def _chip_exchange_wait(send_sems, recv_sems, in_thru, land_thru, after):
    def body(in_ref, land_ref, send_sems, recv_sems, after_ref, in_dead, land_out):
        for cp in _chip_exchange_copies(in_ref, land_ref, send_sems, recv_sems):
            cp.wait_send()
            cp.wait_recv()

    return pl.pallas_call(
        body, name="rs_chip_wait",
        out_shape=(pltpu.HBM(in_thru.shape, in_thru.dtype), pltpu.HBM(land_thru.shape, land_thru.dtype)),
        in_specs=(HBM, HBM, SEM, SEM, ANY), out_specs=(HBM, HBM), input_output_aliases={0: 0, 1: 1},
        compiler_params=pltpu.CompilerParams(has_side_effects=DATAFLOW),
    )(in_thru, land_thru, send_sems, recv_sems, after)[1]


def _all_reduce_small(v, *, name):
    rows, width = v.shape
    half = rows // 2
    assert half % SUBLANES == 0

    def body(in_ref, out_ref, pair_buf, chip_buf, send_sems, recv_sems):
        x, y, c = _place()
        sibling = (x, y, 1 - c)
        me = 2 * x + y
        mine = pl.ds(pl.multiple_of(c * half, SUBLANES), half)
        other = pl.ds(pl.multiple_of((1 - c) * half, SUBLANES), half)

        def copy(k, src, dst, to):
            return pltpu.make_async_remote_copy(src_ref=src, dst_ref=dst, send_sem=send_sems.at[k],
                                                recv_sem=recv_sems.at[k], device_id=to, device_id_type=MESH)

        swap = copy(0, in_ref.at[other, :], pair_buf, sibling)
        swap.start()
        swap.wait()
        chip_buf[me] = in_ref[mine, :] + pair_buf[...]
        chips = _other_chips(x, y)
        for j, (cx, cy) in enumerate(chips):
            copy(1 + j, chip_buf.at[me], chip_buf.at[me], (cx, cy, c)).start()
        for j, (cx, cy) in enumerate(chips):
            got = copy(1 + j, chip_buf.at[me], chip_buf.at[2 * cx + cy], (cx, cy, c))
            got.wait_send()
            got.wait_recv()
        out_ref[mine, :] = ((chip_buf[0] + chip_buf[1]) + chip_buf[2]) + chip_buf[3]
        share = copy(1 + len(chips), out_ref.at[mine, :], out_ref.at[mine, :], sibling)
        share.start()
        share.wait_send()
        copy(1 + len(chips), out_ref.at[other, :], out_ref.at[other, :], sibling).wait_recv()

    vmem = pl.BlockSpec(memory_space=pltpu.VMEM)
    return pl.pallas_call(
        body, name=name, out_shape=jax.ShapeDtypeStruct(v.shape, v.dtype), in_specs=[vmem], out_specs=vmem,
        scratch_shapes=[pltpu.VMEM((half, width), v.dtype), pltpu.VMEM((N_CHIPS, half, width), v.dtype),
                        pltpu.SemaphoreType.DMA((5,)), pltpu.SemaphoreType.DMA((5,))],
        compiler_params=pltpu.CompilerParams(vmem_limit_bytes=VMEM_LIMIT_V7X),
    )(v)


def _reduce_scatter_begin(gpack, behind):
    x, y, c = _place()
    got = _pair_exchange(gpack, name="rs_pair_swap")
    pair, pair16 = _pair_add(gpack, got, c.astype(jnp.int32).reshape(1), name="rs_pair_add")
    mine = lax.dynamic_index_in_dim(pair, 2 * x + y, axis=0, keepdims=False)
    if behind:
        *in_flight, token = _chip_exchange_start(pair16)
        return dict(mine=mine, in_flight=in_flight), token
    return dict(mine=mine, recv=_chip_exchange(pair16, name="rs_chip_exchange")), None


def _reduce_scatter_end(state, after=None):
    c = lax.axis_index("c")
    recv = state["recv"] if "recv" in state else _chip_exchange_wait(*state["in_flight"], after=after)
    (total,) = _rw_fwd(_f_add4, [state["mine"], recv[0], recv[1], recv[2]], [], tm=512, name="rs_chip_add")
    theirs = _sibling_swap(total, name="rs_share_swap")
    return jnp.concatenate([jnp.where(c == 0, total, theirs), jnp.where(c == 0, theirs, total)], axis=0)


_SHARDED = (("w_kv_mem", 1), ("w_o", 1), ("mlp_w1", 2), ("mlp_w2", 1), ("gdn_w_in", 2), ("gdn_conv_w", 2),
            ("s5_w_in", 2), ("s5_d", 1), ("s5_w_glu", 1), ("s5_b_glu", 1))
_MATMUL_ONLY = ("w_kv_mem", "w_o", "mlp_w1", "mlp_w2", "gdn_w_in", "s5_w_in", "s5_w_glu")
_KEPT_BLOCKED = ("mlp_w1",)
_REPLICATED = ("ln1_g", "ln1_b", "ln2_g", "ln2_b", "gdn_a_log", "gdn_dt_bias", "gdn_norm_g", "s5_a_re", "s5_a_im",
               "s5_b_re", "s5_b_im", "s5_c_re", "s5_c_im", "s5_log_dt")
_WEIGHTS = ("w_kv_mem", "w_o", "ln1_g", "ln1_b", "ln2_g", "ln2_b", "mlp_w1", "mlp_w2", "gdn_w_in", "gdn_conv_w",
            "gdn_a_log", "gdn_dt_bias", "gdn_norm_g", "s5_w_in", "s5_a_re", "s5_a_im", "s5_b_re", "s5_b_im",
            "s5_c_re", "s5_c_im", "s5_log_dt", "s5_d", "s5_w_glu", "s5_b_glu")


ROW_ALIGN = 16


def _n_rows(shape):
    return -(-math.prod(shape) // (ROW_ALIGN * D_MODEL)) * ROW_ALIGN


def _as_rows(a):
    rows = _n_rows(a.shape)
    if a.shape[-1] == D_MODEL and a.size == rows * D_MODEL:
        return a.reshape(-1, D_MODEL)
    flat = a.reshape(-1)
    return jnp.pad(flat, (0, rows * D_MODEL - flat.size)).reshape(rows, D_MODEL)


def _pack(arrs, unit_rows=SHARD_ROWS):
    rows = [_as_rows(a) for a in arrs]
    pad = -sum(r.shape[0] for r in rows) % unit_rows
    if pad:
        rows.append(jnp.zeros((pad, D_MODEL), rows[0].dtype))
    return jnp.concatenate(rows, axis=0)


def _unpack(packed, shapes):
    lead = packed.shape[:-2]
    out, off = [], 0
    for s in shapes:
        r = _n_rows(s)
        seg = lax.slice_in_dim(packed, off, off + r, axis=len(lead))
        if s[-1] != D_MODEL or math.prod(s) != r * D_MODEL:
            seg = lax.slice_in_dim(seg.reshape(lead + (-1,)), 0, math.prod(s), axis=len(lead))
        out.append(seg.reshape(lead + tuple(s)))
        off += r
    return out


def _split3(t):
    hi = t.astype(BF16)
    r1 = t - hi.astype(F32)
    mid = r1.astype(BF16)
    lo = (r1 - mid.astype(F32)).astype(BF16)
    return jnp.stack([hi, mid, lo], axis=-1)


def _join3(t):
    return (t[..., 0].astype(F32) + t[..., 1].astype(F32)) + t[..., 2].astype(F32)


def _merge_chips(blocks, axis):
    return jnp.concatenate([blocks[s] for s in range(N_CHIPS)], axis=axis)


def _pack_for_chips(weights):
    rows = []
    for s in range(N_CHIPS):
        chip = []
        for layers, axis in weights:
            if axis is None:
                blocks = [g[s] for g in layers]
            else:
                n = layers[0].shape[axis] // N_CHIPS
                blocks = [lax.slice_in_dim(g, s * n, (s + 1) * n, axis=axis) for g in layers]
            if math.prod(blocks[0].shape) % (ROW_ALIGN * D_MODEL) == 0:
                chip += [_as_rows(b) for b in blocks]
            else:
                chip.append(_as_rows(jnp.stack(blocks)))
        pad = -sum(r.shape[0] for r in chip) % SHARD_ROWS
        rows += chip + ([jnp.zeros((pad, D_MODEL), F32)] if pad else [])
    return jnp.concatenate(rows, axis=0).reshape(N_CHIPS, -1, D_MODEL)


def kernel(x, mem, w_kv_mem, w_o, ln1_g, ln1_b, ln2_g, ln2_b, mlp_w1, mlp_w2, gdn_w_in, gdn_conv_w, gdn_a_log, gdn_dt_bias, gdn_norm_g, s5_w_in, s5_a_re, s5_a_im, s5_b_re, s5_b_im, s5_c_re, s5_c_im, s5_log_dt, s5_d, s5_w_glu, s5_b_glu, loss_target, m_w_kv_mem, m_w_o, m_ln1_g, m_ln1_b, m_ln2_g, m_ln2_b, m_mlp_w1, m_mlp_w2, m_gdn_w_in, m_gdn_conv_w, m_gdn_a_log, m_gdn_dt_bias, m_gdn_norm_g, m_s5_w_in, m_s5_a_re, m_s5_a_im, m_s5_b_re, m_s5_b_im, m_s5_c_re, m_s5_c_im, m_s5_log_dt, m_s5_d, m_s5_w_glu, m_s5_b_glu, v_w_kv_mem, v_w_o, v_ln1_g, v_ln1_b, v_ln2_g, v_ln2_b, v_mlp_w1, v_mlp_w2, v_gdn_w_in, v_gdn_conv_w, v_gdn_a_log, v_gdn_dt_bias, v_gdn_norm_g, v_s5_w_in, v_s5_a_re, v_s5_a_im, v_s5_b_re, v_s5_b_im, v_s5_c_re, v_s5_c_im, v_s5_log_dt, v_s5_d, v_s5_w_glu, v_s5_b_glu):
    given = dict(locals())
    w = {n: given[n] for n in _WEIGHTS}
    mom = {n: given["m_" + n] for n in _WEIGHTS}
    var = {n: given["v_" + n] for n in _WEIGHTS}
    shard_names = [n for n, _ in _SHARDED]
    shard_shapes = [w[n].shape for n in shard_names]
    rep_shapes = [w[n].shape for n in _REPLICATED]

    wire = {n: w[n].astype(BF16) if n in _MATMUL_ONLY else _split3(w[n]) for n in shard_names}
    first = {n: 0 if n.startswith("s5_") else 1 for n in shard_names}
    me_chip = 2 * lax.axis_index("x") + lax.axis_index("y")
    early = [wire[n][:first[n]] for n in shard_names if first[n]]
    late = [wire[n][first[n]:] for n in shard_names]
    early_pack, late_pack = _pack(early), _pack(late)
    *first_copies, first_token = _gather_start(early_pack, after=w["s5_log_dt"], name="gather_first_start")
    s5_names = ("s5_a_re", "s5_a_im", "s5_b_re", "s5_b_im", "s5_c_re", "s5_c_im")
    s5_preps = [jax.vjp(_s5_prep, *[w[n][j] for n in s5_names], w["s5_log_dt"][j] + first_token[0, 0])
                for j in range(w["s5_log_dt"].shape[0])]
    landed = _gather_wait(*first_copies, after=s5_preps[-1][0][3], name="gather_first_wait")
    landed = _gather_forward(landed, name="gather_first_forward")
    landed = lax.dynamic_update_index_in_dim(landed, early_pack, me_chip, axis=0)
    early_blocks = dict(zip([n for n in shard_names if first[n]], _unpack(landed, [a.shape for a in early])))
    send_sems, recv_sems, pack_thru, land_thru, token = _gather_start(late_pack, after=landed, name="gather_start")
    axis_of = dict(_SHARDED)

    def merged(n, blk):
        if n in _KEPT_BLOCKED:
            return blk
        return _merge_chips(blk if n in _MATMUL_ONLY else _join3(blk), axis_of[n] - 1)

    late_full = {}

    def weights_of(i, h):
        if i == 0:
            full = {n: [merged(n, blk[:, 0])] for n, blk in early_blocks.items()}
            full["gdn_w_in"][0] = full["gdn_w_in"][0] + token[0, 0].astype(BF16)
        else:
            if not late_full:
                land = _gather_wait(send_sems, recv_sems, pack_thru, land_thru, after=h, name="gather_wait")
                land = _gather_forward(land, name="gather_forward")
                land = lax.dynamic_update_index_in_dim(land, late_pack, me_chip, axis=0)
                for n, blk in zip(shard_names, _unpack(land, [a.shape for a in late])):
                    late_full[n] = [None] * first[n] + [merged(n, blk[:, t]) for t in range(blk.shape[1])]
            full = dict(late_full)
        full.update({n: w[n] for n in _REPLICATED})
        full["s5_prep"] = s5_preps
        return _layer_weights(full, i)

    sharded = {}
    in_flight = {}
    first_mixer, first_outer = (0, "mixer"), (0, "outer")

    def group_pack(parts):
        names = [n for n in shard_names if any(n in sharded[i] for i in parts)]
        per_weight = [[sharded[i][n] for i in parts if n in sharded[i]] for n in names]
        pack = _pack_for_chips([(g, None if n in _KEPT_BLOCKED else axis_of[n] - 1) for n, g in zip(names, per_weight)])
        return pack, names, [(len(g),) + w[n].shape[1:] for n, g in zip(names, per_weight)]

    def grads_ready(i, g):
        by_weight = _sharded_grads(g, i)
        if i:
            sharded[i] = by_weight
        else:
            sharded[first_mixer] = {n: g for n, g in by_weight.items() if n not in sharded[first_outer]}

    def before_first_mixer(outer):
        sharded[first_outer] = outer
        pack, names, shapes = group_pack([first_outer] + list(range(1, DEPTH)))
        state, token = _reduce_scatter_begin(pack, behind=True)
        in_flight.update(state=state, names=names, shapes=shapes)
        return token

    loss, grad_x, layer_grads = _local_step(x[0], mem[0], loss_target[0], weights_of, grads_ready,
                                            before_first_mixer)
    loss = lax.psum(loss, ("x", "y", "c"))
    pack, names, shapes = group_pack([first_mixer])
    state, _ = _reduce_scatter_begin(pack, behind=False)
    pieces = {n: [] for n in shard_names}
    for n, g in zip(names, _unpack(_reduce_scatter_end(state), shapes)):
        pieces[n].append(g)
    late = _reduce_scatter_end(in_flight["state"], after=grad_x)
    for n, g in zip(in_flight["names"], _unpack(late, in_flight["shapes"])):
        pieces[n].append(g)
    g_shards = [p[0] if len(p) == 1 else jnp.concatenate(p, axis=0) for p in (pieces[n] for n in shard_names)]

    def pack_small(d):
        return _pack([d[n] for n in _REPLICATED], unit_rows=SMALL_ROWS)

    g_rep = _all_reduce_small(pack_small(_replicated_grads(layer_grads)), name="reduce_replicated")

    def adamw(wp, gp, mp, vp, name):
        return _rw_fwd(_f_adamw, [wp, gp, mp, vp], [], tm=TM_WIDE, name=name)

    outs = {}
    for n, g in zip(shard_names, g_shards):
        flat = (-1, w[n].shape[-1])
        res = adamw(w[n].reshape(flat), g.reshape(flat), mom[n].reshape(flat), var[n].reshape(flat), "adamw_" + n)
        outs[("grad", n)] = g
        outs.update({(kind, n): a.reshape(w[n].shape) for kind, a in zip(("delta", "new_m", "new_v"), res)})
    packed = (g_rep,) + tuple(adamw(pack_small(w), g_rep, pack_small(mom), pack_small(var), "adamw_replicated"))
    for kind, pr in zip(("grad", "delta", "new_m", "new_v"), packed):
        outs.update({(kind, n): a for n, a in zip(_REPLICATED, _unpack(pr, rep_shapes))})
    return (loss, grad_x[None]) + tuple(outs[(kind, n)] for kind in ("grad", "delta", "new_m", "new_v")
                                        for n in _WEIGHTS)
```
